```python
import math
import jax, jax.numpy as jnp
from jax import lax
import numpy as np

D_MODEL = 1024
BATCH = 32
SEQ = 2048
DEPTH = 1

N_META = 16
BLOCK_Q = 128
HEAD_DIM = 64
H_SB = 8
H_FOX = 8
W_SB = H_SB * HEAD_DIM
W_FOX = H_FOX * HEAD_DIM
D_FF = ((8 * D_MODEL // 3 + 127) // 128) * 128
CONV_W = 3
RMS_EPS = 1e-6
IN_SIZES = (W_SB, W_SB, W_SB, W_FOX, W_FOX, W_FOX, H_FOX, D_MODEL, D_MODEL)
IN_COLS = 3 * W_SB + 3 * W_FOX + H_FOX + 2 * D_MODEL

kernel_name = "hybrid_stickbreak_fox_convffn"


def rmsnorm(x, g):
    xf = x.astype(jnp.float32)
    r = lax.rsqrt(jnp.mean(xf * xf, axis=-1, keepdims=True) + RMS_EPS)
    return (xf * r).astype(x.dtype) * g


def split_heads(t, n_heads):
    b, l, _ = t.shape
    return t.reshape(b, l, n_heads, HEAD_DIM).transpose(0, 2, 1, 3)


def merge_heads(t):
    b, h, l, d = t.shape
    return t.transpose(0, 2, 1, 3).reshape(b, l, h * d)


def query_blocks(total_len):
    bounds = [(0, min(N_META, total_len))]
    for s in range(N_META, total_len, BLOCK_Q):
        bounds.append((s, min(s + BLOCK_Q, total_len)))
    return bounds


def stick_breaking_block(q, k, v, start):
    tq, tk = q.shape[2], k.shape[2]
    z = jnp.einsum('bhqd,bhkd->bhqk', q, k).astype(jnp.float32) / math.sqrt(HEAD_DIM)
    t_idx = start + jnp.arange(tq)[:, None]
    s_idx = jnp.arange(tk)[None, :]
    strict = s_idx < t_idx
    log_beta = jax.nn.log_sigmoid(z)
    log_keep = jnp.where(strict, jax.nn.log_sigmoid(-z), 0.0)
    after = lax.cumsum(log_keep, axis=3, reverse=True) - log_keep
    w = jnp.where(strict, jnp.exp(log_beta + after), 0.0)
    return jnp.einsum('bhqk,bhkd->bhqd', w.astype(v.dtype), v)


def forgetting_block(q, k, v, cum_log_f, start):
    tq, tk = q.shape[2], k.shape[2]
    z = jnp.einsum('bhqd,bhkd->bhqk', q, k).astype(jnp.float32) / math.sqrt(HEAD_DIM)
    z = z + cum_log_f[:, :, start:start + tq, None] - cum_log_f[:, :, None, :tk]
    t_idx = start + jnp.arange(tq)[:, None]
    s_idx = jnp.arange(tk)[None, :]
    z = jnp.where(s_idx <= t_idx, z, -jnp.inf)
    p = jax.nn.softmax(z, axis=-1)
    return jnp.einsum('bhqk,bhkd->bhqd', p.astype(v.dtype), v)


def hybrid_mixer(h, w_in, b_forget, w_branch_sb, w_branch_fox, w_out):
    total_len = h.shape[1]
    proj = h @ w_in
    split_at = np.cumsum(IN_SIZES)[:-1].tolist()
    q_sb, k_sb, v_sb, q_fx, k_fx, v_fx, f_logit, g_sb, g_fx = jnp.split(proj, split_at, axis=-1)
    q_sb, k_sb, v_sb = split_heads(q_sb, H_SB), split_heads(k_sb, H_SB), split_heads(v_sb, H_SB)
    q_fx, k_fx, v_fx = split_heads(q_fx, H_FOX), split_heads(k_fx, H_FOX), split_heads(v_fx, H_FOX)
    log_f = jax.nn.log_sigmoid(f_logit.astype(jnp.float32) + b_forget.astype(jnp.float32))
    cum_log_f = jnp.cumsum(log_f, axis=1).transpose(0, 2, 1)

    out_sb, out_fx = [], []
    for start, end in query_blocks(total_len):
        out_sb.append(stick_breaking_block(q_sb[:, :, start:end], k_sb[:, :, :end], v_sb[:, :, :end], start))
        out_fx.append(forgetting_block(q_fx[:, :, start:end], k_fx[:, :, :end], v_fx[:, :, :end], cum_log_f, start))
    o_sb = merge_heads(jnp.concatenate(out_sb, axis=2))
    o_fx = merge_heads(jnp.concatenate(out_fx, axis=2))

    merged = jax.nn.sigmoid(g_sb) * (o_sb @ w_branch_sb) + jax.nn.sigmoid(g_fx) * (o_fx @ w_branch_fox)
    return merged @ w_out


def conv_ffn(h, w_up, conv_w, w_down):
    total_len = h.shape[1]
    u = h @ w_up
    u_pad = jnp.pad(u, ((0, 0), (CONV_W - 1, 0), (0, 0)))
    uc = sum(conv_w[j] * u_pad[:, j:j + total_len] for j in range(CONV_W))
    a, b = jnp.split(uc, 2, axis=-1)
    return (jax.nn.silu(a) * b) @ w_down


def _fwd_setup_inputs(seed: int = 0) -> dict:
    key = jax.random.key(seed)
    ks = jax.random.split(key, 14)
    f32 = jnp.float32
    x = jax.random.normal(ks[0], (BATCH, SEQ, D_MODEL), f32)
    meta_tokens = jax.random.normal(ks[1], (N_META, D_MODEL), f32)
    norm_mix_g = 1.0 + 0.02 * jax.random.normal(ks[2], (DEPTH, D_MODEL), f32)
    w_in = jax.random.normal(ks[3], (DEPTH, D_MODEL, IN_COLS), f32) * D_MODEL ** -0.5
    b_forget = jnp.linspace(1.0, 6.0, H_FOX, dtype=f32)[None, :] + 0.1 * jax.random.normal(ks[4], (DEPTH, H_FOX), f32)
    w_branch_sb = jax.random.normal(ks[5], (DEPTH, W_SB, D_MODEL), f32) * W_SB ** -0.5
    w_branch_fox = jax.random.normal(ks[6], (DEPTH, W_FOX, D_MODEL), f32) * W_FOX ** -0.5
    w_out = jax.random.normal(ks[7], (DEPTH, D_MODEL, D_MODEL), f32) * D_MODEL ** -0.5
    norm_ffn_g = 1.0 + 0.02 * jax.random.normal(ks[8], (DEPTH, D_MODEL), f32)
    w_up = jax.random.normal(ks[9], (DEPTH, D_MODEL, 2 * D_FF), f32) * D_MODEL ** -0.5
    conv_w = jax.random.normal(ks[10], (DEPTH, CONV_W, 2 * D_FF), f32) * CONV_W ** -0.5
    w_down = jax.random.normal(ks[11], (DEPTH, D_FF, D_MODEL), f32) * D_FF ** -0.5
    norm_final_g = 1.0 + 0.02 * jax.random.normal(ks[12], (D_MODEL,), f32)
    return {"x": x, "meta_tokens": meta_tokens, "norm_mix_g": norm_mix_g, "w_in": w_in,
            "b_forget": b_forget, "w_branch_sb": w_branch_sb, "w_branch_fox": w_branch_fox,
            "w_out": w_out, "norm_ffn_g": norm_ffn_g, "w_up": w_up, "conv_w": conv_w,
            "w_down": w_down, "norm_final_g": norm_final_g}


def _fwd_reference(x, meta_tokens, norm_mix_g, w_in, b_forget, w_branch_sb, w_branch_fox,
              w_out, norm_ffn_g, w_up, conv_w, w_down, norm_final_g):
    b = x.shape[0]
    meta = jnp.broadcast_to(meta_tokens[None].astype(x.dtype), (b, N_META, D_MODEL))
    h = jnp.concatenate([meta, x], axis=1)
    for layer in range(DEPTH):
        h = h + hybrid_mixer(rmsnorm(h, norm_mix_g[layer]), w_in[layer], b_forget[layer],
                             w_branch_sb[layer], w_branch_fox[layer], w_out[layer])
        h = h + conv_ffn(rmsnorm(h, norm_ffn_g[layer]), w_up[layer], conv_w[layer], w_down[layer])
    return rmsnorm(h, norm_final_g)[:, N_META:]


import jax as _jax
import jax.numpy as _jnp

TWIN_FORMAT = 'train_step'
FWD_PARAMS = ['x', 'meta_tokens', 'norm_mix_g', 'w_in', 'b_forget', 'w_branch_sb', 'w_branch_fox', 'w_out', 'norm_ffn_g', 'w_up', 'conv_w', 'w_down', 'norm_final_g']
TWIN_WEIGHTS = ['meta_tokens', 'norm_mix_g', 'w_in', 'b_forget', 'w_branch_sb', 'w_branch_fox', 'w_out', 'norm_ffn_g', 'w_up', 'conv_w', 'w_down', 'norm_final_g']
TWIN_DIFF_INPUT = 'x'
TWIN_INPUTS = ['x', 'meta_tokens', 'norm_mix_g', 'w_in', 'b_forget', 'w_branch_sb', 'w_branch_fox', 'w_out', 'norm_ffn_g', 'w_up', 'conv_w', 'w_down', 'norm_final_g', 'loss_target', 'm_meta_tokens', 'm_norm_mix_g', 'm_w_in', 'm_b_forget', 'm_w_branch_sb', 'm_w_branch_fox', 'm_w_out', 'm_norm_ffn_g', 'm_w_up', 'm_conv_w', 'm_w_down', 'm_norm_final_g', 'v_meta_tokens', 'v_norm_mix_g', 'v_w_in', 'v_b_forget', 'v_w_branch_sb', 'v_w_branch_fox', 'v_w_out', 'v_norm_ffn_g', 'v_w_up', 'v_conv_w', 'v_w_down', 'v_norm_final_g']
TWIN_OUTPUTS = ['loss', 'grad_x', 'grad_meta_tokens', 'grad_norm_mix_g', 'grad_w_in', 'grad_b_forget', 'grad_w_branch_sb', 'grad_w_branch_fox', 'grad_w_out', 'grad_norm_ffn_g', 'grad_w_up', 'grad_conv_w', 'grad_w_down', 'grad_norm_final_g', 'delta_meta_tokens', 'delta_norm_mix_g', 'delta_w_in', 'delta_b_forget', 'delta_w_branch_sb', 'delta_w_branch_fox', 'delta_w_out', 'delta_norm_ffn_g', 'delta_w_up', 'delta_conv_w', 'delta_w_down', 'delta_norm_final_g', 'new_m_meta_tokens', 'new_m_norm_mix_g', 'new_m_w_in', 'new_m_b_forget', 'new_m_w_branch_sb', 'new_m_w_branch_fox', 'new_m_w_out', 'new_m_norm_ffn_g', 'new_m_w_up', 'new_m_conv_w', 'new_m_w_down', 'new_m_norm_final_g', 'new_v_meta_tokens', 'new_v_norm_mix_g', 'new_v_w_in', 'new_v_b_forget', 'new_v_w_branch_sb', 'new_v_w_branch_fox', 'new_v_w_out', 'new_v_norm_ffn_g', 'new_v_w_up', 'new_v_conv_w', 'new_v_w_down', 'new_v_norm_final_g']
TWIN_LEAF_KINDS = {'loss': 'loss', 'grad_x': 'grad_x', 'grad_meta_tokens': 'grad_w', 'grad_norm_mix_g': 'grad_w', 'grad_w_in': 'grad_w', 'grad_b_forget': 'grad_w', 'grad_w_branch_sb': 'grad_w', 'grad_w_branch_fox': 'grad_w', 'grad_w_out': 'grad_w', 'grad_norm_ffn_g': 'grad_w', 'grad_w_up': 'grad_w', 'grad_conv_w': 'grad_w', 'grad_w_down': 'grad_w', 'grad_norm_final_g': 'grad_w', 'delta_meta_tokens': 'delta_w', 'delta_norm_mix_g': 'delta_w', 'delta_w_in': 'delta_w', 'delta_b_forget': 'delta_w', 'delta_w_branch_sb': 'delta_w', 'delta_w_branch_fox': 'delta_w', 'delta_w_out': 'delta_w', 'delta_norm_ffn_g': 'delta_w', 'delta_w_up': 'delta_w', 'delta_conv_w': 'delta_w', 'delta_w_down': 'delta_w', 'delta_norm_final_g': 'delta_w', 'new_m_meta_tokens': 'new_m', 'new_m_norm_mix_g': 'new_m', 'new_m_w_in': 'new_m', 'new_m_b_forget': 'new_m', 'new_m_w_branch_sb': 'new_m', 'new_m_w_branch_fox': 'new_m', 'new_m_w_out': 'new_m', 'new_m_norm_ffn_g': 'new_m', 'new_m_w_up': 'new_m', 'new_m_conv_w': 'new_m', 'new_m_w_down': 'new_m', 'new_m_norm_final_g': 'new_m', 'new_v_meta_tokens': 'new_v', 'new_v_norm_mix_g': 'new_v', 'new_v_w_in': 'new_v', 'new_v_b_forget': 'new_v', 'new_v_w_branch_sb': 'new_v', 'new_v_w_branch_fox': 'new_v', 'new_v_w_out': 'new_v', 'new_v_norm_ffn_g': 'new_v', 'new_v_w_up': 'new_v', 'new_v_conv_w': 'new_v', 'new_v_w_down': 'new_v', 'new_v_norm_final_g': 'new_v'}


def _forward(args):
    return _fwd_reference(*[args[k] for k in FWD_PARAMS])


def _output_shape():
    out = _jax.eval_shape(lambda: _forward(_fwd_setup_inputs(0)))
    return out.shape, out.dtype

N_MICROBATCH = 1
ADAM_LR = 0.001
ADAM_B1 = 0.9
ADAM_B2 = 0.999
ADAM_EPS = 1e-08
ADAM_WD = 0.01
ADAM_STEP = 10
PER_EXAMPLE_BATCH_AXIS = {'x': 0, 'loss_target': 0}
SHARED_INPUTS = []
_WEIGHT_DTYPES = {'meta_tokens': _jnp.float32, 'norm_mix_g': _jnp.float32, 'w_in': _jnp.float32, 'b_forget': _jnp.float32, 'w_branch_sb': _jnp.float32, 'w_branch_fox': _jnp.float32, 'w_out': _jnp.float32, 'norm_ffn_g': _jnp.float32, 'w_up': _jnp.float32, 'conv_w': _jnp.float32, 'w_down': _jnp.float32, 'norm_final_g': _jnp.float32}
MOMENT_SCALE = {'meta_tokens': 4.567863e-03, 'norm_mix_g': 1.403901e-01, 'w_in': 6.060870e-02, 'b_forget': 3.366269e-01, 'w_branch_sb': 9.202363e-02, 'w_branch_fox': 4.598381e-02, 'w_out': 1.022239e-01, 'norm_ffn_g': 1.788415e-01, 'w_up': 7.620481e-02, 'conv_w': 7.674177e-02, 'w_down': 1.245099e-01, 'norm_final_g': 6.404346e+01}


def _to_microbatches(a, axis):
    t = _jnp.moveaxis(a, axis, 0)
    t = t.reshape((N_MICROBATCH, t.shape[0] // N_MICROBATCH) + t.shape[1:])
    return _jnp.moveaxis(t, 1, axis + 1)


def setup_inputs(seed: int = 0) -> dict:
    inp = _fwd_setup_inputs(seed)
    key = _jax.random.fold_in(_jax.random.key(seed), 7919)
    shape, _ = _output_shape()
    out = dict(inp)
    out["loss_target"] = _jax.random.normal(_jax.random.fold_in(key, 0), shape, _jnp.float32)
    for i, name in enumerate(TWIN_WEIGHTS):
        w = inp[name].astype(_jnp.float32)
        if MOMENT_SCALE is None:
            s = _jnp.sqrt(_jnp.mean(_jnp.square(w)) + 1e-30)
        else:
            s = MOMENT_SCALE[name]
        km, kv = _jax.random.split(_jax.random.fold_in(key, i + 1))
        out[name] = w
        out["m_" + name] = s * _jax.random.normal(km, w.shape, _jnp.float32)
        out["v_" + name] = (s * s) * _jax.random.uniform(kv, w.shape, _jnp.float32, 0.5, 1.5)
    if N_MICROBATCH > 1:
        for name, axis in PER_EXAMPLE_BATCH_AXIS.items():
            out[name] = _to_microbatches(out[name], axis)
    return {'x': out['x'], 'meta_tokens': out['meta_tokens'], 'norm_mix_g': out['norm_mix_g'], 'w_in': out['w_in'], 'b_forget': out['b_forget'], 'w_branch_sb': out['w_branch_sb'], 'w_branch_fox': out['w_branch_fox'], 'w_out': out['w_out'], 'norm_ffn_g': out['norm_ffn_g'], 'w_up': out['w_up'], 'conv_w': out['conv_w'], 'w_down': out['w_down'], 'norm_final_g': out['norm_final_g'], 'loss_target': out['loss_target'], 'm_meta_tokens': out['m_meta_tokens'], 'm_norm_mix_g': out['m_norm_mix_g'], 'm_w_in': out['m_w_in'], 'm_b_forget': out['m_b_forget'], 'm_w_branch_sb': out['m_w_branch_sb'], 'm_w_branch_fox': out['m_w_branch_fox'], 'm_w_out': out['m_w_out'], 'm_norm_ffn_g': out['m_norm_ffn_g'], 'm_w_up': out['m_w_up'], 'm_conv_w': out['m_conv_w'], 'm_w_down': out['m_w_down'], 'm_norm_final_g': out['m_norm_final_g'], 'v_meta_tokens': out['v_meta_tokens'], 'v_norm_mix_g': out['v_norm_mix_g'], 'v_w_in': out['v_w_in'], 'v_b_forget': out['v_b_forget'], 'v_w_branch_sb': out['v_w_branch_sb'], 'v_w_branch_fox': out['v_w_branch_fox'], 'v_w_out': out['v_w_out'], 'v_norm_ffn_g': out['v_norm_ffn_g'], 'v_w_up': out['v_w_up'], 'v_conv_w': out['v_conv_w'], 'v_w_down': out['v_w_down'], 'v_norm_final_g': out['v_norm_final_g']}


def _loss(weights, diff, rest, loss_target):
    with _jax.named_scope("forward"):
        args = {**rest, TWIN_DIFF_INPUT: diff, **{k: w.astype(_WEIGHT_DTYPES[k]) for k, w in weights.items()}}
        y = _forward(args)
    with _jax.named_scope("loss_head"):
        err = _jnp.square(y.astype(_jnp.float32) - loss_target)
        return 0.5 * _jnp.sum(_jnp.mean(err, axis=-1)) if err.ndim else 0.5 * err


def _adamw(w, g, m, v):
    m = ADAM_B1 * m + (1.0 - ADAM_B1) * g
    v = ADAM_B2 * v + (1.0 - ADAM_B2) * _jnp.square(g)
    m_hat = m / (1.0 - ADAM_B1 ** ADAM_STEP)
    v_hat = v / (1.0 - ADAM_B2 ** ADAM_STEP)
    delta = -ADAM_LR * (m_hat / (_jnp.sqrt(v_hat) + ADAM_EPS) + ADAM_WD * w)
    return delta, m, v


def reference(x, meta_tokens, norm_mix_g, w_in, b_forget, w_branch_sb, w_branch_fox, w_out, norm_ffn_g, w_up, conv_w, w_down, norm_final_g, loss_target, m_meta_tokens, m_norm_mix_g, m_w_in, m_b_forget, m_w_branch_sb, m_w_branch_fox, m_w_out, m_norm_ffn_g, m_w_up, m_conv_w, m_w_down, m_norm_final_g, v_meta_tokens, v_norm_mix_g, v_w_in, v_b_forget, v_w_branch_sb, v_w_branch_fox, v_w_out, v_norm_ffn_g, v_w_up, v_conv_w, v_w_down, v_norm_final_g):
    given = dict(x=x, meta_tokens=meta_tokens, norm_mix_g=norm_mix_g, w_in=w_in, b_forget=b_forget, w_branch_sb=w_branch_sb, w_branch_fox=w_branch_fox, w_out=w_out, norm_ffn_g=norm_ffn_g, w_up=w_up, conv_w=conv_w, w_down=w_down, norm_final_g=norm_final_g, loss_target=loss_target, m_meta_tokens=m_meta_tokens, m_norm_mix_g=m_norm_mix_g, m_w_in=m_w_in, m_b_forget=m_b_forget, m_w_branch_sb=m_w_branch_sb, m_w_branch_fox=m_w_branch_fox, m_w_out=m_w_out, m_norm_ffn_g=m_norm_ffn_g, m_w_up=m_w_up, m_conv_w=m_conv_w, m_w_down=m_w_down, m_norm_final_g=m_norm_final_g, v_meta_tokens=v_meta_tokens, v_norm_mix_g=v_norm_mix_g, v_w_in=v_w_in, v_b_forget=v_b_forget, v_w_branch_sb=v_w_branch_sb, v_w_branch_fox=v_w_branch_fox, v_w_out=v_w_out, v_norm_ffn_g=v_norm_ffn_g, v_w_up=v_w_up, v_conv_w=v_conv_w, v_w_down=v_w_down, v_norm_final_g=v_norm_final_g)
    weights = {n: given[n] for n in TWIN_WEIGHTS}
    shared = {n: given[n] for n in SHARED_INPUTS}
    per_example = {n: given[n] for n in ['x']}
    grad_fn = _jax.value_and_grad(_loss, argnums=(0, 1))

    def one_microbatch(ex, loss_target):
        ex = dict(ex)
        diff = ex.pop(TWIN_DIFF_INPUT)
        return grad_fn(weights, diff, {**shared, **ex}, loss_target)

    if N_MICROBATCH == 1:
        loss, (grad_w, grad_x) = one_microbatch(per_example, given["loss_target"])
    else:
        def body(carry, xs):
            loss_sum, grad_sum = carry
            l_k, (gw_k, gx_k) = one_microbatch(xs[0], xs[1])
            with _jax.named_scope("update"):
                return (loss_sum + l_k, _jax.tree.map(_jnp.add, grad_sum, gw_k)), gx_k

        init = (_jnp.zeros((), _jnp.float32), _jax.tree.map(_jnp.zeros_like, weights))
        (loss, grad_w), grad_x = _jax.lax.scan(body, init, (per_example, given["loss_target"]))
    with _jax.named_scope("update"):
        delta_w, new_m, new_v = {}, {}, {}
        for n in TWIN_WEIGHTS:
            delta_w[n], new_m[n], new_v[n] = _adamw(weights[n], grad_w[n], given["m_" + n], given["v_" + n])
    return (loss, grad_x, *[grad_w[n] for n in TWIN_WEIGHTS], *[delta_w[n] for n in TWIN_WEIGHTS],
            *[new_m[n] for n in TWIN_WEIGHTS], *[new_v[n] for n in TWIN_WEIGHTS])
```

```python
import functools

import jax
import jax.numpy as jnp
from jax import lax
from jax.experimental import pallas as pl
from jax.experimental.pallas import tpu as pltpu

F32 = jnp.float32
BF16 = jnp.bfloat16

N_DEV = 8
D = 1024
N_META = 16
SEQ = 2048
L_REAL = N_META + SEQ
LP = 2176
LA = 2304
BQ = 256
NBLK = LA // BQ
HEAD = 64
NH = 8
W_ATT = NH * HEAD
D_FF = 2816
IN_COLS = 5128
QKV = 6 * W_ATT
IN_P = 5376
F_COL = QKV + 2 * D
FFC = 256
RMS_EPS = 1e-6
LR, B1, B2, EPS, WD, STEP = 0.001, 0.9, 0.999, 1e-08, 0.01, 10
VMEM_LIMIT = 56 * 1024 * 1024

MESH = pl.DeviceIdType.MESH


def _cparams(*sem):
    return pltpu.CompilerParams(dimension_semantics=sem if sem else None, vmem_limit_bytes=VMEM_LIMIT)


def _all_gather(x, name):
    rows, cols = x.shape

    def body(x_ref, out_ref, send_sems, recv_sems, local_sem):
        mx, my, mc = lax.axis_index("x"), lax.axis_index("y"), lax.axis_index("c")
        me, sibling = (mx, my, mc), (mx, my, 1 - mc)
        chips = [(1 - mx, my), (mx, 1 - my), (1 - mx, 1 - my)]

        def slot(px, py, pc):
            return out_ref.at[4 * px + 2 * py + pc]

        def copy(k, block, to, src=None):
            return pltpu.make_async_remote_copy(
                src_ref=slot(*block) if src is None else src, dst_ref=slot(*block),
                send_sem=send_sems.at[k], recv_sem=recv_sems.at[k],
                device_id=to, device_id_type=MESH)

        mine = pltpu.make_async_copy(x_ref, slot(*me), local_sem)
        mine.start()
        first = [copy(0, me, sibling, src=x_ref)]
        first += [copy(1 + j, me, (*chip, mc), src=x_ref) for j, chip in enumerate(chips)]
        for cp in first:
            cp.start()
        passed = [copy(4 + j, (*chip, mc), sibling) for j, chip in enumerate(chips)]
        for j, chip in enumerate(chips):
            copy(1 + j, (*chip, mc), me).wait_recv()
            passed[j].start()
        copy(0, sibling, me).wait_recv()
        for j, chip in enumerate(chips):
            copy(4 + j, (*chip, 1 - mc), me).wait_recv()
        for cp in first + passed:
            cp.wait_send()
        mine.wait()

    return pl.pallas_call(
        body, name=name,
        out_shape=jax.ShapeDtypeStruct((N_DEV, rows, cols), x.dtype),
        in_specs=[pl.BlockSpec(memory_space=pl.ANY)],
        out_specs=pl.BlockSpec(memory_space=pl.ANY),
        scratch_shapes=[pltpu.SemaphoreType.DMA((7,)), pltpu.SemaphoreType.DMA((7,)), pltpu.SemaphoreType.DMA],
    )(x)


def _exchange(src, name):
    _, rows, cols = src.shape

    def body(src_ref, dst_ref, send_sems, recv_sems, local_sem):
        mx, my, mc = lax.axis_index("x"), lax.axis_index("y"), lax.axis_index("c")
        me_idx = 4 * mx + 2 * my + mc
        mine = pltpu.make_async_copy(src_ref.at[me_idx], dst_ref.at[me_idx], local_sem)
        mine.start()
        copies = []
        for k in range(1, N_DEV):
            px, py, pc = mx ^ (k >> 2), my ^ ((k >> 1) & 1), mc ^ (k & 1)
            copies.append(pltpu.make_async_remote_copy(
                src_ref=src_ref.at[4 * px + 2 * py + pc], dst_ref=dst_ref.at[me_idx],
                send_sem=send_sems.at[k - 1], recv_sem=recv_sems.at[k - 1],
                device_id=(px, py, pc), device_id_type=MESH))
        for cp in copies:
            cp.start()
        for cp in copies:
            cp.wait_recv()
        for cp in copies:
            cp.wait_send()
        mine.wait()

    return pl.pallas_call(
        body, name=name,
        out_shape=jax.ShapeDtypeStruct(src.shape, src.dtype),
        in_specs=[pl.BlockSpec(memory_space=pl.ANY)],
        out_specs=pl.BlockSpec(memory_space=pl.ANY),
        scratch_shapes=[pltpu.SemaphoreType.DMA((7,)), pltpu.SemaphoreType.DMA((7,)), pltpu.SemaphoreType.DMA],
    )(src)


def _matmul(a, b, *, out_dtype, tm, tn, tk, ta=False, tb=False, name):
    if ta:
        kdim, m = a.shape
    else:
        m, kdim = a.shape
    n = b.shape[0] if tb else b.shape[1]
    assert m % tm == 0 and n % tn == 0 and kdim % tk == 0, (name, a.shape, b.shape, tm, tn, tk)
    nk = kdim // tk

    def body(a_ref, b_ref, o_ref, *scratch):
        av, bv = a_ref[...], b_ref[...]
        if ta:
            p = lax.dot_general(av, bv, (((0,), (0,)), ((), ())), preferred_element_type=F32)
        elif tb:
            p = lax.dot_general(av, bv, (((1,), (1,)), ((), ())), preferred_element_type=F32)
        else:
            p = jnp.dot(av, bv, preferred_element_type=F32)
        if nk == 1:
            o_ref[...] = p.astype(o_ref.dtype)
        else:
            acc_ref, = scratch
            k = pl.program_id(2)

            @pl.when(k == 0)
            def _():
                acc_ref[...] = p

            @pl.when(k > 0)
            def _():
                acc_ref[...] += p

            @pl.when(k == nk - 1)
            def _():
                o_ref[...] = acc_ref[...].astype(o_ref.dtype)

    a_spec = pl.BlockSpec((tk, tm), lambda i, j, k: (k, i)) if ta else pl.BlockSpec((tm, tk), lambda i, j, k: (i, k))
    b_spec = pl.BlockSpec((tn, tk), lambda i, j, k: (j, k)) if tb else pl.BlockSpec((tk, tn), lambda i, j, k: (k, j))
    return pl.pallas_call(
        body, name=name,
        out_shape=jax.ShapeDtypeStruct((m, n), out_dtype),
        grid=(m // tm, n // tn, nk),
        in_specs=[a_spec, b_spec],
        out_specs=pl.BlockSpec((tm, tn), lambda i, j, k: (i, j)),
        scratch_shapes=[] if nk == 1 else [pltpu.VMEM((tm, tn), F32)],
        compiler_params=_cparams("parallel", "parallel", "arbitrary"),
    )(a, b)


TR = 272


def _rms(h):
    return lax.rsqrt(jnp.mean(h * h, axis=-1, keepdims=True) + RMS_EPS)


def _norm_fwd(h, delta, g, name):
    t = h.shape[0]
    row = pl.BlockSpec((TR, D), lambda i: (i, 0))
    vec = pl.BlockSpec((1, D), lambda i: (0, 0))

    if delta is None:
        def body(h_ref, g_ref, n_ref):
            hv = h_ref[...]
            n_ref[...] = ((hv * _rms(hv)) * g_ref[...]).astype(BF16)

        n = pl.pallas_call(
            body, name=name, out_shape=jax.ShapeDtypeStruct((t, D), BF16), grid=(t // TR,),
            in_specs=[row, vec], out_specs=row, compiler_params=_cparams("parallel"))(h, g)
        return h, n

    def body(h_ref, d_ref, g_ref, hn_ref, n_ref):
        hv = h_ref[...] + d_ref[...]
        hn_ref[...] = hv
        n_ref[...] = ((hv * _rms(hv)) * g_ref[...]).astype(BF16)

    return pl.pallas_call(
        body, name=name,
        out_shape=(jax.ShapeDtypeStruct((t, D), F32), jax.ShapeDtypeStruct((t, D), BF16)), grid=(t // TR,),
        in_specs=[row, row, vec], out_specs=(row, row), compiler_params=_cparams("parallel"))(h, delta, g)


def _rms_bwd_math(hv, dn, gv):
    r = _rms(hv)
    hr = hv * r
    dng = dn * gv
    dh = r * (dng - hr * jnp.mean(dng * hr, axis=-1, keepdims=True))
    return dh, dn * hr


def _final_loss_bwd(h1, delta, g, tgt, name):
    t = h1.shape[0]
    row = pl.BlockSpec((TR, D), lambda i: (i, 0))
    vec = pl.BlockSpec((1, D), lambda i: (0, 0))
    tiles_per_seq = LP // TR

    def body(h_ref, d_ref, g_ref, t_ref, loss_ref, dh_ref, dhb_ref, dg_ref):
        i = pl.program_id(0)
        hv = h_ref[...] + d_ref[...]
        gv = g_ref[...]
        r = _rms(hv)
        hr = hv * r
        y = hr * gv
        pos = (i % tiles_per_seq) * TR + lax.broadcasted_iota(jnp.int32, (TR, 1), 0)
        valid = (pos >= N_META) & (pos < L_REAL)
        err = jnp.where(valid, y - t_ref[...], 0.0)
        part = 0.5 * jnp.sum(jnp.mean(err * err, axis=-1, keepdims=True))
        dy = err * (1.0 / D)
        dng = dy * gv
        dh = r * (dng - hr * jnp.mean(dng * hr, axis=-1, keepdims=True))
        dh_ref[...] = dh
        dhb_ref[...] = dh.astype(BF16)
        dgp = jnp.sum(dy * hr, axis=0, keepdims=True)

        @pl.when(i == 0)
        def _():
            loss_ref[...] = jnp.zeros_like(loss_ref)
            dg_ref[...] = jnp.zeros_like(dg_ref)

        loss_ref[...] += part
        dg_ref[...] += dgp

    return pl.pallas_call(
        body, name=name,
        out_shape=(jax.ShapeDtypeStruct((8, 128), F32), jax.ShapeDtypeStruct((t, D), F32),
                   jax.ShapeDtypeStruct((t, D), BF16), jax.ShapeDtypeStruct((1, D), F32)),
        grid=(t // TR,),
        in_specs=[row, row, vec, row],
        out_specs=(pl.BlockSpec((8, 128), lambda i: (0, 0)), row, row, vec),
        compiler_params=_cparams("arbitrary"))(h1, delta, g, tgt)


def _norm_bwd(h, dn, g, dres, with_bf16, name):
    t = h.shape[0]
    row = pl.BlockSpec((TR, D), lambda i: (i, 0))
    vec = pl.BlockSpec((1, D), lambda i: (0, 0))

    def body(h_ref, dn_ref, g_ref, dres_ref, *outs):
        i = pl.program_id(0)
        dh, dgrow = _rms_bwd_math(h_ref[...], dn_ref[...], g_ref[...])
        dh = dh + dres_ref[...]
        outs[0][...] = dh
        if with_bf16:
            outs[1][...] = dh.astype(BF16)
        dg_ref = outs[-1]

        @pl.when(i == 0)
        def _():
            dg_ref[...] = jnp.zeros_like(dg_ref)

        dg_ref[...] += jnp.sum(dgrow, axis=0, keepdims=True)

    shapes = [jax.ShapeDtypeStruct((t, D), F32)]
    specs = [row]
    if with_bf16:
        shapes.append(jax.ShapeDtypeStruct((t, D), BF16))
        specs.append(row)
    shapes.append(jax.ShapeDtypeStruct((1, D), F32))
    specs.append(vec)
    return pl.pallas_call(
        body, name=name, out_shape=tuple(shapes), grid=(t // TR,),
        in_specs=[row, row, vec, row], out_specs=tuple(specs),
        compiler_params=_cparams("arbitrary"))(h, dn, g, dres)


GATE_SB_BLK = QKV // D
GATE_FX_BLK = QKV // D + 1


def _sigmoid(x):
    return 1.0 / (1.0 + jnp.exp(-x))


def _merge_fwd(p_sb, p_fx, proj, name):
    t = p_sb.shape[0]
    row = pl.BlockSpec((TR, D), lambda i: (i, 0))

    def body(ps_ref, pf_ref, gs_ref, gf_ref, o_ref):
        o_ref[...] = (_sigmoid(gs_ref[...]) * ps_ref[...] + _sigmoid(gf_ref[...]) * pf_ref[...]).astype(BF16)

    return pl.pallas_call(
        body, name=name, out_shape=jax.ShapeDtypeStruct((t, D), BF16), grid=(t // TR,),
        in_specs=[row, row, pl.BlockSpec((TR, D), lambda i: (i, GATE_SB_BLK)),
                  pl.BlockSpec((TR, D), lambda i: (i, GATE_FX_BLK))],
        out_specs=row, compiler_params=_cparams("parallel"))(p_sb, p_fx, proj, proj)


def _merge_bwd(dm, p_sb, p_fx, proj, name):
    t = dm.shape[0]
    row = pl.BlockSpec((TR, D), lambda i: (i, 0))

    def body(dm_ref, ps_ref, pf_ref, gs_ref, gf_ref, dps_ref, dpf_ref, dgs_ref, dgf_ref):
        dmv = dm_ref[...]
        ss, sf = _sigmoid(gs_ref[...]), _sigmoid(gf_ref[...])
        dps_ref[...] = (dmv * ss).astype(BF16)
        dpf_ref[...] = (dmv * sf).astype(BF16)
        dgs_ref[...] = (dmv * ps_ref[...] * ss * (1.0 - ss)).astype(BF16)
        dgf_ref[...] = (dmv * pf_ref[...] * sf * (1.0 - sf)).astype(BF16)

    out = jax.ShapeDtypeStruct((t, D), BF16)
    return pl.pallas_call(
        body, name=name, out_shape=(out, out, out, out), grid=(t // TR,),
        in_specs=[row, row, row, pl.BlockSpec((TR, D), lambda i: (i, GATE_SB_BLK)),
                  pl.BlockSpec((TR, D), lambda i: (i, GATE_FX_BLK))],
        out_specs=(row, row, row, row), compiler_params=_cparams("parallel"))(dm, p_sb, p_fx, proj, proj)


CH = 272


def _chunk(c, n=CH):
    return pl.ds(pl.multiple_of(c * CH, 8), n)


def _conv_taps(u_ref, c):
    x = u_ref[_chunk(c), :]
    prev = u_ref[pl.ds(pl.multiple_of(jnp.maximum(c * CH - 8, 0), 8), 8), :]
    xx = jnp.concatenate([jnp.where(c == 0, 0.0, prev), x], axis=0)
    return x, pltpu.roll(xx, 1, 0)[8:], pltpu.roll(xx, 2, 0)[8:]


def _conv_glu_fwd(u, cw, nseq, name):
    nblk = D_FF // FFC

    def body(u_ref, cw_ref, o_ref):
        cwv = cw_ref[...]

        def step(c, _):
            x, x1, x2 = _conv_taps(u_ref, c)
            uc = cwv[0:1, :] * x2 + cwv[1:2, :] * x1 + cwv[2:3, :] * x
            a, b = uc[:, :FFC], uc[:, FFC:]
            o_ref[_chunk(c), :] = (a * _sigmoid(a) * b).astype(BF16)
            return 0

        lax.fori_loop(0, LP // CH, step, 0)

    return pl.pallas_call(
        body, name=name, out_shape=jax.ShapeDtypeStruct((nseq * LP, D_FF), BF16), grid=(nseq, nblk),
        in_specs=[pl.BlockSpec((LP, 2 * FFC), lambda s, j: (s, j)), pl.BlockSpec((3, 2 * FFC), lambda s, j: (0, j))],
        out_specs=pl.BlockSpec((LP, FFC), lambda s, j: (s, j)),
        compiler_params=_cparams("parallel", "parallel"))(u, cw)


def _conv_glu_bwd(u, cw, dact, nseq, name):
    nblk = D_FF // FFC
    nch = LP // CH

    def body(u_ref, cw_ref, da_ref, du_ref, dcw_ref):
        s = pl.program_id(1)
        cwv = cw_ref[...]

        def step(k, carry):
            nxt, p0, p1, p2 = carry
            c = nch - 1 - k
            x, x1, x2 = _conv_taps(u_ref, c)
            uc = cwv[0:1, :] * x2 + cwv[1:2, :] * x1 + cwv[2:3, :] * x
            a, b = uc[:, :FFC], uc[:, FFC:]
            sa = _sigmoid(a)
            dactv = da_ref[_chunk(c), :]
            da = dactv * b * (sa * (1.0 + a * (1.0 - sa)))
            db = dactv * (a * sa)
            duc = jnp.concatenate([da, db], axis=1)
            dd = jnp.concatenate([duc, nxt], axis=0)
            du = (cwv[2:3, :] * duc + cwv[1:2, :] * pltpu.roll(dd, CH + 7, 0)[:CH]
                  + cwv[0:1, :] * pltpu.roll(dd, CH + 6, 0)[:CH])
            du_ref[_chunk(c), :] = du.astype(BF16)
            return (duc[:8], p0 + jnp.sum(duc * x2, axis=0, keepdims=True),
                    p1 + jnp.sum(duc * x1, axis=0, keepdims=True), p2 + jnp.sum(duc * x, axis=0, keepdims=True))

        zrow = jnp.zeros((1, 2 * FFC), F32)
        _, p0, p1, p2 = lax.fori_loop(0, nch, step, (jnp.zeros((8, 2 * FFC), F32), zrow, zrow, zrow))

        @pl.when(s == 0)
        def _():
            dcw_ref[...] = jnp.zeros_like(dcw_ref)

        dcw_ref[...] += jnp.concatenate([p0, p1, p2], axis=0)

    return pl.pallas_call(
        body, name=name,
        out_shape=(jax.ShapeDtypeStruct((nseq * LP, 2 * D_FF), BF16), jax.ShapeDtypeStruct((3, 2 * D_FF), F32)),
        grid=(nblk, nseq),
        in_specs=[pl.BlockSpec((LP, 2 * FFC), lambda j, s: (s, j)), pl.BlockSpec((3, 2 * FFC), lambda j, s: (0, j)),
                  pl.BlockSpec((LP, FFC), lambda j, s: (s, j))],
        out_specs=(pl.BlockSpec((LP, 2 * FFC), lambda j, s: (s, j)), pl.BlockSpec((3, 2 * FFC), lambda j, s: (0, j))),
        compiler_params=_cparams("parallel", "arbitrary"))(u, cw, dact)


F_BLK = F_COL // 128
CB = 128


def _split3(x):
    hi = x.astype(BF16)
    r1 = x - hi.astype(F32)
    mid = r1.astype(BF16)
    lo = (r1 - mid.astype(F32)).astype(BF16)
    return hi, mid, lo


def _tri_dot(tri, x):
    hi, mid, lo = _split3(x)
    d = functools.partial(jnp.dot, preferred_element_type=F32)
    return d(tri, hi) + d(tri, mid) + d(tri, lo)


def _log_sigmoid(x):
    return jnp.minimum(x, 0.0) - jnp.log(1.0 + jnp.exp(-jnp.abs(x)))


def _gate_fwd(proj, bf, nseq, name):
    def body(f_ref, b_ref, c_ref):
        r_i = lax.broadcasted_iota(jnp.int32, (CB, CB), 0)
        c_i = lax.broadcasted_iota(jnp.int32, (CB, CB), 1)
        tri = (c_i <= r_i).astype(BF16)
        bv = b_ref[...]

        def step(k, carry):
            rows = pl.ds(pl.multiple_of(k * CB, CB), CB)
            lf = _log_sigmoid(f_ref[rows, :] + bv)
            c_ref[rows, :] = _tri_dot(tri, lf) + carry
            return carry + jnp.sum(lf, axis=0, keepdims=True)

        lax.fori_loop(0, LP // CB, step, jnp.zeros((1, 128), F32))

    return pl.pallas_call(
        body, name=name, out_shape=jax.ShapeDtypeStruct((nseq * LP, 128), F32), grid=(nseq,),
        in_specs=[pl.BlockSpec((LP, 128), lambda s: (s, F_BLK)), pl.BlockSpec((1, 128), lambda s: (0, 0))],
        out_specs=pl.BlockSpec((LP, 128), lambda s: (s, 0)),
        compiler_params=_cparams("parallel"))(proj, bf)


def _gate_bwd(proj, bf, dc, nseq, name):
    def body(f_ref, b_ref, dc_ref, df_ref, db_ref):
        s = pl.program_id(0)
        r_i = lax.broadcasted_iota(jnp.int32, (CB, CB), 0)
        c_i = lax.broadcasted_iota(jnp.int32, (CB, CB), 1)
        tri = (c_i >= r_i).astype(BF16)
        bv = b_ref[...]

        def step(kk, carry):
            carry_c, carry_b = carry
            k = LP // CB - 1 - kk
            rows = pl.ds(pl.multiple_of(k * CB, CB), CB)
            dcv = dc_ref[rows, :]
            dlf = _tri_dot(tri, dcv) + carry_c
            df = dlf * _sigmoid(-(f_ref[rows, :] + bv))
            df_ref[rows, :] = df
            return carry_c + jnp.sum(dcv, axis=0, keepdims=True), carry_b + jnp.sum(df, axis=0, keepdims=True)

        zero = jnp.zeros((1, 128), F32)
        _, dbp = lax.fori_loop(0, LP // CB, step, (zero, zero))

        @pl.when(s == 0)
        def _():
            db_ref[...] = jnp.zeros_like(db_ref)

        db_ref[...] += dbp

    return pl.pallas_call(
        body, name=name,
        out_shape=(jax.ShapeDtypeStruct((nseq * LP, 128), F32), jax.ShapeDtypeStruct((1, 128), F32)), grid=(nseq,),
        in_specs=[pl.BlockSpec((LP, 128), lambda s: (s, F_BLK)), pl.BlockSpec((1, 128), lambda s: (0, 0)),
                  pl.BlockSpec((LP, 128), lambda s: (s, 0))],
        out_specs=(pl.BlockSpec((LP, 128), lambda s: (s, 0)), pl.BlockSpec((1, 128), lambda s: (0, 0))),
        compiler_params=_cparams("arbitrary"))(proj, bf, dc)


SCALE = 0.125
NEG = -1e30


def _dot_nt(a, b):
    return lax.dot_general(a, b, (((1,), (1,)), ((), ())), preferred_element_type=F32)


def _dot_tn(a, b):
    return lax.dot_general(a, b, (((0,), (0,)), ((), ())), preferred_element_type=F32)


def _dot(a, b):
    return jnp.dot(a, b, preferred_element_type=F32)


def _blk(i):
    return pl.ds(pl.multiple_of(i * BQ, BQ), BQ)


def _tile_iotas():
    return lax.broadcasted_iota(jnp.int32, (BQ, BQ), 0), lax.broadcasted_iota(jnp.int32, (BQ, BQ), 1)


def _sb_weights(qi, kj, strict, r_after, u_suf):
    z = _dot_nt(qi, kj) * SCALE
    sp = jnp.maximum(z, 0.0) + jnp.log(1.0 + jnp.exp(-jnp.abs(z)))
    lk = jnp.where(strict, -sp, 0.0)
    hi = lk.astype(BF16)
    lo = (lk - hi.astype(F32)).astype(BF16)
    suf = _dot(hi, u_suf) + _dot(lo, u_suf)
    w = jnp.where(strict, jnp.exp(z - sp + r_after + suf), 0.0)
    return w, sp, lk


HEAD_SPEC = pl.BlockSpec((1, LA, HEAD), lambda g: (g, 0, 0))
LANE_SPEC = pl.BlockSpec((1, LA, 128), lambda g: (g, 0, 0))


def _sb_fwd(q, k, v, name):
    g = q.shape[0]

    def body(q_ref, k_ref, v_ref, o_ref, rs_ref, acc_ref, r_ref, rb_ref):
        row, col = _tile_iotas()
        u_suf = (row > col).astype(BF16)
        lane = lax.broadcasted_iota(jnp.int32, (BQ, 128), 1)

        def qblock(i, _):
            qi = q_ref[0, _blk(i), :]
            acc_ref[...] = jnp.zeros_like(acc_ref)
            r_ref[...] = jnp.zeros_like(r_ref)
            rb_ref[...] = jnp.zeros_like(rb_ref)

            def kblock(jj, _):
                j = i - jj
                strict = (col + j * BQ) < (row + i * BQ)
                r_after = r_ref[...]
                w, _, lk = _sb_weights(qi, k_ref[0, _blk(j), :], strict, r_after, u_suf)
                acc_ref[...] += _dot(w.astype(BF16), v_ref[0, _blk(j), :])
                rb_ref[...] = jnp.where(lane == j, r_after, rb_ref[...])
                r_ref[...] = r_after + jnp.sum(lk, axis=1, keepdims=True)
                return 0

            lax.fori_loop(0, i + 1, kblock, 0)
            o_ref[0, _blk(i), :] = acc_ref[...]
            rs_ref[0, _blk(i), :] = rb_ref[...]
            return 0

        lax.fori_loop(0, NBLK, qblock, 0)

    return pl.pallas_call(
        body, name=name,
        out_shape=(jax.ShapeDtypeStruct((g, LA, HEAD), F32), jax.ShapeDtypeStruct((g, LA, 128), F32)),
        grid=(g,), in_specs=[HEAD_SPEC, HEAD_SPEC, HEAD_SPEC], out_specs=(HEAD_SPEC, LANE_SPEC),
        scratch_shapes=[pltpu.VMEM((BQ, HEAD), F32), pltpu.VMEM((BQ, 1), F32), pltpu.VMEM((BQ, 128), F32)],
        compiler_params=_cparams("parallel"))(q, k, v)


def _sb_bwd(q, k, v, do, rs, name):
    g = q.shape[0]

    def body(q_ref, k_ref, v_ref, do_ref, rs_ref, dq_ref, dk_ref, dv_ref, dqa_ref, dka_ref, dva_ref, ep_ref):
        row, col = _tile_iotas()
        u_suf = (row > col).astype(BF16)
        u_pre = (row < col).astype(BF16)
        lane = lax.broadcasted_iota(jnp.int32, (BQ, 128), 1)
        dka_ref[...] = jnp.zeros_like(dka_ref)
        dva_ref[...] = jnp.zeros_like(dva_ref)

        def qblock(i, _):
            qi = q_ref[0, _blk(i), :]
            doi = do_ref[0, _blk(i), :]
            rb = rs_ref[0, _blk(i), :]
            dqa_ref[...] = jnp.zeros_like(dqa_ref)
            ep_ref[...] = jnp.zeros_like(ep_ref)

            def kblock(j, _):
                strict = (col + j * BQ) < (row + i * BQ)
                r_after = jnp.sum(jnp.where(lane == j, rb, 0.0), axis=1, keepdims=True)
                kj, vj = k_ref[0, _blk(j), :], v_ref[0, _blk(j), :]
                w, sp, _ = _sb_weights(qi, kj, strict, r_after, u_suf)
                e = _dot_nt(doi, vj) * w
                e_pre = ep_ref[...] + _dot(e.astype(BF16), u_pre)
                ep_ref[...] += jnp.sum(e, axis=1, keepdims=True)
                sneg = jnp.exp(-sp)
                dz = jnp.where(strict, e * sneg - (1.0 - sneg) * e_pre, 0.0).astype(BF16)
                dqa_ref[...] += _dot(dz, kj)
                dka_ref[_blk(j), :] += _dot_tn(dz, qi)
                dva_ref[_blk(j), :] += _dot_tn(w.astype(BF16), doi)
                return 0

            lax.fori_loop(0, i + 1, kblock, 0)
            dq_ref[0, _blk(i), :] = (dqa_ref[...] * SCALE).astype(BF16)
            return 0

        lax.fori_loop(0, NBLK, qblock, 0)
        dk_ref[0] = (dka_ref[...] * SCALE).astype(BF16)
        dv_ref[0] = dva_ref[...].astype(BF16)

    out = jax.ShapeDtypeStruct((g, LA, HEAD), BF16)
    return pl.pallas_call(
        body, name=name, out_shape=(out, out, out), grid=(g,),
        in_specs=[HEAD_SPEC, HEAD_SPEC, HEAD_SPEC, HEAD_SPEC, LANE_SPEC],
        out_specs=(HEAD_SPEC, HEAD_SPEC, HEAD_SPEC),
        scratch_shapes=[pltpu.VMEM((BQ, HEAD), F32), pltpu.VMEM((LA, HEAD), F32), pltpu.VMEM((LA, HEAD), F32),
                        pltpu.VMEM((BQ, 1), F32)],
        compiler_params=_cparams("parallel"))(q, k, v, do, rs)


CCOL_SPEC = pl.BlockSpec((1, LA, 1), lambda g: (g, 0, 0))
CROW_SPEC = pl.BlockSpec((1, 16, BQ), lambda g: (g, 0, 0))


def _fox_scores(qi, kj, cq, ck, causal):
    z = _dot_nt(qi, kj) * SCALE + (cq - ck)
    return jnp.where(causal, z, NEG)


def _fox_fwd(q, k, v, ccol, crow, name):
    g = q.shape[0]

    def body(q_ref, k_ref, v_ref, cc_ref, cr_ref, o_ref, lse_ref, acc_ref, m_ref, l_ref):
        row, col = _tile_iotas()

        def qblock(i, _):
            qi = q_ref[0, _blk(i), :]
            cq = cc_ref[0, _blk(i), :]
            acc_ref[...] = jnp.zeros_like(acc_ref)
            m_ref[...] = jnp.full_like(m_ref, NEG)
            l_ref[...] = jnp.zeros_like(l_ref)

            def kblock(j, _):
                causal = (col + j * BQ) <= (row + i * BQ)
                z = _fox_scores(qi, k_ref[0, _blk(j), :], cq, cr_ref[0, pl.ds(j, 1), :], causal)
                m_old = m_ref[...]
                m_new = jnp.maximum(m_old, jnp.max(z, axis=1, keepdims=True))
                alpha = jnp.exp(m_old - m_new)
                p = jnp.exp(z - m_new)
                l_ref[...] = alpha * l_ref[...] + jnp.sum(p, axis=1, keepdims=True)
                acc_ref[...] = alpha * acc_ref[...] + _dot(p.astype(BF16), v_ref[0, _blk(j), :])
                m_ref[...] = m_new
                return 0

            lax.fori_loop(0, i + 1, kblock, 0)
            o_ref[0, _blk(i), :] = acc_ref[...] / l_ref[...]
            lse_ref[0, _blk(i), :] = jnp.broadcast_to(m_ref[...] + jnp.log(l_ref[...]), (BQ, 128))
            return 0

        lax.fori_loop(0, NBLK, qblock, 0)

    return pl.pallas_call(
        body, name=name,
        out_shape=(jax.ShapeDtypeStruct((g, LA, HEAD), F32), jax.ShapeDtypeStruct((g, LA, 128), F32)),
        grid=(g,), in_specs=[HEAD_SPEC, HEAD_SPEC, HEAD_SPEC, CCOL_SPEC, CROW_SPEC], out_specs=(HEAD_SPEC, LANE_SPEC),
        scratch_shapes=[pltpu.VMEM((BQ, HEAD), F32), pltpu.VMEM((BQ, 1), F32), pltpu.VMEM((BQ, 1), F32)],
        compiler_params=_cparams("parallel"))(q, k, v, ccol, crow)


def _fox_bwd(q, k, v, ccol, crow, o, lse, do, name):
    g = q.shape[0]

    def body(q_ref, k_ref, v_ref, cc_ref, cr_ref, o_ref, lse_ref, do_ref, dq_ref, dk_ref, dv_ref, dc_ref, dcq_ref,
             dqa_ref, dka_ref, dva_ref, rsum_ref):
        row, col = _tile_iotas()
        sub = lax.broadcasted_iota(jnp.int32, (16, BQ), 0)
        dka_ref[...] = jnp.zeros_like(dka_ref)
        dva_ref[...] = jnp.zeros_like(dva_ref)
        dc_ref[0] = jnp.zeros((16, BQ), F32)

        def qblock(i, _):
            qi = q_ref[0, _blk(i), :]
            doi = do_ref[0, _blk(i), :]
            cq = cc_ref[0, _blk(i), :]
            lse_i = lse_ref[0, _blk(i), 0:1]
            delta = jnp.sum(doi.astype(F32) * o_ref[0, _blk(i), :], axis=1, keepdims=True)
            dqa_ref[...] = jnp.zeros_like(dqa_ref)
            rsum_ref[...] = jnp.zeros_like(rsum_ref)

            def kblock(j, _):
                causal = (col + j * BQ) <= (row + i * BQ)
                kj, vj = k_ref[0, _blk(j), :], v_ref[0, _blk(j), :]
                z = _fox_scores(qi, kj, cq, cr_ref[0, pl.ds(j, 1), :], causal)
                p = jnp.exp(z - lse_i)
                ds = p * (_dot_nt(doi, vj) - delta)
                dsb = ds.astype(BF16)
                dqa_ref[...] += _dot(dsb, kj)
                dka_ref[_blk(j), :] += _dot_tn(dsb, qi)
                dva_ref[_blk(j), :] += _dot_tn(p.astype(BF16), doi)
                dc_ref[0] = dc_ref[0] - jnp.where(sub == j, jnp.sum(ds, axis=0, keepdims=True), 0.0)
                rsum_ref[...] += jnp.sum(ds, axis=1, keepdims=True)
                return 0

            lax.fori_loop(0, i + 1, kblock, 0)
            dq_ref[0, _blk(i), :] = (dqa_ref[...] * SCALE).astype(BF16)
            dcq_ref[0, _blk(i), :] = rsum_ref[...]
            return 0

        lax.fori_loop(0, NBLK, qblock, 0)
        dk_ref[0] = (dka_ref[...] * SCALE).astype(BF16)
        dv_ref[0] = dva_ref[...].astype(BF16)

    out = jax.ShapeDtypeStruct((g, LA, HEAD), BF16)
    return pl.pallas_call(
        body, name=name,
        out_shape=(out, out, out, jax.ShapeDtypeStruct((g, 16, BQ), F32), jax.ShapeDtypeStruct((g, LA, 1), F32)),
        grid=(g,),
        in_specs=[HEAD_SPEC, HEAD_SPEC, HEAD_SPEC, CCOL_SPEC, CROW_SPEC, HEAD_SPEC, LANE_SPEC, HEAD_SPEC],
        out_specs=(HEAD_SPEC, HEAD_SPEC, HEAD_SPEC, CROW_SPEC, CCOL_SPEC),
        scratch_shapes=[pltpu.VMEM((BQ, HEAD), F32), pltpu.VMEM((LA, HEAD), F32), pltpu.VMEM((LA, HEAD), F32),
                        pltpu.VMEM((BQ, 1), F32)],
        compiler_params=_cparams("parallel"))(q, k, v, ccol, crow, o, lse, do)


def _adamw_math(w, g, m, v):
    m = B1 * m + (1.0 - B1) * g
    v = B2 * v + (1.0 - B2) * (g * g)
    m_hat = m / (1.0 - B1 ** STEP)
    v_hat = v / (1.0 - B2 ** STEP)
    delta = -LR * (m_hat / (jnp.sqrt(v_hat) + EPS) + WD * w)
    return delta, m, v


def _sum_adamw(parts, w, m, v, tr, name):
    _, rows, _ = parts.shape
    assert rows % tr == 0

    def body(p_ref, w_ref, m_ref, v_ref, g_ref, d_ref, nm_ref, nv_ref):
        gsum = p_ref[0].astype(F32)
        for s in range(1, N_DEV):
            gsum = gsum + p_ref[s].astype(F32)
        d, nm, nv = _adamw_math(w_ref[...], gsum, m_ref[...], v_ref[...])
        g_ref[...] = gsum
        d_ref[...] = d
        nm_ref[...] = nm
        nv_ref[...] = nv

    flat = pl.BlockSpec((tr, 128), lambda i: (i, 0))
    out = jax.ShapeDtypeStruct((rows, 128), F32)
    return pl.pallas_call(
        body, name=name, out_shape=(out, out, out, out), grid=(rows // tr,),
        in_specs=[pl.BlockSpec((N_DEV, tr, 128), lambda i: (0, i, 0)), flat, flat, flat],
        out_specs=(flat, flat, flat, flat), compiler_params=_cparams("parallel"))(parts, w, m, v)


BIG = (("w_in", (D, IN_COLS // N_DEV), 1), ("w_branch_sb", (W_ATT, D // N_DEV), 1), ("w_branch_fox", (W_ATT, D // N_DEV), 1),
       ("w_out", (D // N_DEV, D), 0), ("w_up", (D, 2 * D_FF // N_DEV), 1), ("w_down", (D_FF // N_DEV, D), 0))
SMALL = (("conv_w", (3, 2 * D_FF // N_DEV), 1), ("meta_tokens", (N_META, D // N_DEV), 1))
REPL = (("norm_mix_g", D), ("norm_ffn_g", D), ("norm_final_g", D), ("b_forget", NH))


def _size(shape):
    return shape[0] * shape[1]


BIG_ELEMS = sum(_size(s) for _, s, _ in BIG)
SMALL_ELEMS = sum(_size(s) for _, s, _ in SMALL)
PACK_ROWS = 15664
PACK_TR = PACK_ROWS // 11
BIG_ROWS = 15632
SMALL_ROWS = 40
REPL_ROWS = 32
assert BIG_ELEMS <= BIG_ROWS * 128 and BIG_ELEMS + SMALL_ELEMS <= PACK_ROWS * 128 and SMALL_ELEMS <= SMALL_ROWS * 128


def _pack_rows(flat_parts, rows, dtype):
    flat = jnp.concatenate([p.astype(dtype) for p in flat_parts], axis=-1)
    pad = rows * 128 - flat.shape[-1]
    flat = jnp.pad(flat, [(0, 0)] * (flat.ndim - 1) + [(0, pad)])
    return flat.reshape(flat.shape[:-1] + (rows, 128))


def _unpack(flat, table):
    out, off = {}, 0
    for name, shape, *_ in table:
        n = _size(shape)
        out[name] = flat[..., off:off + n].reshape(flat.shape[:-1] + shape)
        off += n
    return out


def _to_full(gathered, axis):
    if axis == 0:
        return gathered.reshape(-1, gathered.shape[2])
    return gathered.transpose(1, 0, 2).reshape(gathered.shape[1], -1)


def _to_shards(full, axis):
    r, c = full.shape
    if axis == 0:
        return full.reshape(N_DEV, r // N_DEV, c)
    return full.reshape(r, N_DEV, c // N_DEV).transpose(1, 0, 2)


def _in_to_padded(w):
    f = w[..., QKV:QKV + NH]
    zeros = jnp.zeros(w.shape[:-1] + (IN_P - F_COL - NH,), w.dtype)
    return jnp.concatenate([w[..., :QKV], w[..., QKV + NH:], f, zeros], axis=-1)


def _in_from_padded(w):
    return jnp.concatenate([w[..., :QKV], w[..., F_COL:F_COL + NH], w[..., QKV:F_COL]], axis=-1)


def _ff_interleave(w):
    lead = w.shape[:-1]
    return w.reshape(lead + (2, D_FF // FFC, FFC)).swapaxes(-3, -2).reshape(lead + (2 * D_FF,))


def _ff_deinterleave(w):
    lead = w.shape[:-1]
    return w.reshape(lead + (D_FF // FFC, 2, FFC)).swapaxes(-3, -2).reshape(lead + (2 * D_FF,))


def _split_heads(t, nseq):
    n = t.shape[1] // W_ATT
    t = t.reshape(nseq, LP, n, NH, HEAD).transpose(2, 0, 3, 1, 4)
    t = jnp.pad(t, ((0, 0), (0, 0), (0, 0), (0, LA - LP), (0, 0)))
    return t.reshape(n, nseq * NH, LA, HEAD)


def _merge_heads(t, nseq):
    n = t.shape[0]
    t = t.reshape(n, nseq, NH, LA, HEAD)[:, :, :, :LP]
    return t.transpose(1, 3, 0, 2, 4).reshape(nseq * LP, n * W_ATT)


def _local_step(x, tgt, meta, g_mix, w_in_p, b_forget, w_bsb, w_bfx, w_out, g_ffn, w_up_i, cw_i, w_down, g_final):
    nseq = x.shape[0]
    t = nseq * LP
    tm = LP // 2
    mm = functools.partial(_matmul, tm=tm)

    def pad_seq(front, body):
        return jnp.concatenate([front, body, jnp.zeros((nseq, LP - L_REAL, D), F32)], axis=1).reshape(t, D)

    h0 = pad_seq(jnp.broadcast_to(meta[None], (nseq, N_META, D)), x)
    tgt_p = pad_seq(jnp.zeros((nseq, N_META, D), F32), tgt)
    bf = jnp.pad(b_forget.reshape(1, NH), ((0, 0), (0, 128 - NH)))

    _, n1 = _norm_fwd(h0, None, g_mix, "norm1")
    proj = mm(n1, w_in_p, out_dtype=F32, tn=1792, tk=D, name="in_proj")
    heads = _split_heads(proj[:, :QKV].astype(BF16), nseq)
    q_sb, k_sb, v_sb, q_fx, k_fx, v_fx = (heads[i] for i in range(6))
    c = _gate_fwd(proj, bf, nseq, "gate_fwd")
    c_h = jnp.pad(c[:, :NH].reshape(nseq, LP, NH).transpose(0, 2, 1), ((0, 0), (0, 0), (0, LA - LP)))
    ccol = c_h.reshape(nseq * NH, LA, 1)
    crow = jnp.pad(c_h.reshape(nseq * NH, NBLK, BQ), ((0, 0), (0, 16 - NBLK), (0, 0)))
    o_sb, rs = _sb_fwd(q_sb, k_sb, v_sb, "sb_fwd")
    o_fx, lse = _fox_fwd(q_fx, k_fx, v_fx, ccol, crow, "fox_fwd")
    o_cat = _merge_heads(jnp.stack([o_sb, o_fx]).astype(BF16), nseq)
    o_sb_t, o_fx_t = o_cat[:, :W_ATT], o_cat[:, W_ATT:]
    p_sb = mm(o_sb_t, w_bsb, out_dtype=F32, tn=D, tk=W_ATT, name="branch_sb")
    p_fx = mm(o_fx_t, w_bfx, out_dtype=F32, tn=D, tk=W_ATT, name="branch_fox")
    merged = _merge_fwd(p_sb, p_fx, proj, "merge_fwd")
    mix = mm(merged, w_out, out_dtype=F32, tn=D, tk=D, name="out_proj")
    h1, n2 = _norm_fwd(h0, mix, g_ffn, "norm2")
    u = mm(n2, w_up_i, out_dtype=F32, tn=1408, tk=D, name="up_proj")
    act = _conv_glu_fwd(u, cw_i, nseq, "conv_glu_fwd")
    ffn = mm(act, w_down, out_dtype=F32, tn=D, tk=1408, name="down_proj")

    loss, dh2, dh2b, dg_final = _final_loss_bwd(h1, ffn, g_final, tgt_p, "final")
    d_down = _matmul(act, dh2b, out_dtype=F32, tm=1408, tn=D, tk=tm, ta=True, name="d_w_down")
    dact = mm(dh2b, w_down, out_dtype=F32, tn=1408, tk=D, tb=True, name="d_act")
    du, d_cw = _conv_glu_bwd(u, cw_i, dact, nseq, "conv_glu_bwd")
    d_up = _matmul(n2, du, out_dtype=F32, tm=D, tn=1408, tk=tm, ta=True, name="d_w_up")
    dn2 = mm(du, w_up_i, out_dtype=F32, tn=D, tk=1408, tb=True, name="d_n2")
    dh1, dh1b, dg_ffn = _norm_bwd(h1, dn2, g_ffn, dh2, True, "norm2_bwd")
    d_out = _matmul(merged, dh1b, out_dtype=F32, tm=D, tn=D, tk=tm, ta=True, name="d_w_out")
    dmerged = mm(dh1b, w_out, out_dtype=F32, tn=D, tk=D, tb=True, name="d_merged")
    dp_sb, dp_fx, dg_sb, dg_fx = _merge_bwd(dmerged, p_sb, p_fx, proj, "merge_bwd")
    d_bsb = _matmul(o_sb_t, dp_sb, out_dtype=F32, tm=W_ATT, tn=D, tk=tm, ta=True, name="d_w_branch_sb")
    d_bfx = _matmul(o_fx_t, dp_fx, out_dtype=F32, tm=W_ATT, tn=D, tk=tm, ta=True, name="d_w_branch_fox")
    do_sb = mm(dp_sb, w_bsb, out_dtype=BF16, tn=W_ATT, tk=D, tb=True, name="d_o_sb")
    do_fx = mm(dp_fx, w_bfx, out_dtype=BF16, tn=W_ATT, tk=D, tb=True, name="d_o_fox")
    do_h = _split_heads(jnp.concatenate([do_sb, do_fx], axis=1), nseq)
    dq_sb, dk_sb, dv_sb = _sb_bwd(q_sb, k_sb, v_sb, do_h[0], rs, "sb_bwd")
    dq_fx, dk_fx, dv_fx, dck, dcq = _fox_bwd(q_fx, k_fx, v_fx, ccol, crow, o_fx, lse, do_h[1], "fox_bwd")
    dc = dck[:, :NBLK].reshape(nseq, NH, LA) + dcq.reshape(nseq, NH, LA)
    dc_t = dc[:, :, :LP].transpose(0, 2, 1).reshape(t, NH)
    df, d_bf = _gate_bwd(proj, bf, jnp.pad(dc_t, ((0, 0), (0, 128 - NH))), nseq, "gate_bwd")
    dqkv = _merge_heads(jnp.stack([dq_sb, dk_sb, dv_sb, dq_fx, dk_fx, dv_fx]), nseq)
    dproj = jnp.concatenate([dqkv, dg_sb, dg_fx, df.astype(BF16), jnp.zeros((t, IN_P - F_COL - 128), BF16)], axis=1)
    d_in = _matmul(n1, dproj, out_dtype=F32, tm=D, tn=1792, tk=tm, ta=True, name="d_w_in")
    dn1 = mm(dproj, w_in_p, out_dtype=F32, tn=D, tk=1792, tb=True, name="d_n1")
    dh0, dg_mix = _norm_bwd(h0, dn1, g_mix, dh1, False, "norm1_bwd")
    dh0 = dh0.reshape(nseq, LP, D)
    grads = dict(meta_tokens=jnp.sum(dh0[:, :N_META], axis=0), norm_mix_g=dg_mix, w_in=d_in, b_forget=d_bf[:, :NH],
                 w_branch_sb=d_bsb, w_branch_fox=d_bfx, w_out=d_out, norm_ffn_g=dg_ffn, w_up=d_up, conv_w=d_cw,
                 w_down=d_down, norm_final_g=dg_final)
    return loss[0, 0], dh0[:, N_META:L_REAL], grads


def kernel(x, meta_tokens, norm_mix_g, w_in, b_forget, w_branch_sb, w_branch_fox, w_out, norm_ffn_g, w_up, conv_w, w_down, norm_final_g, loss_target, m_meta_tokens, m_norm_mix_g, m_w_in, m_b_forget, m_w_branch_sb, m_w_branch_fox, m_w_out, m_norm_ffn_g, m_w_up, m_conv_w, m_w_down, m_norm_final_g, v_meta_tokens, v_norm_mix_g, v_w_in, v_b_forget, v_w_branch_sb, v_w_branch_fox, v_w_out, v_norm_ffn_g, v_w_up, v_conv_w, v_w_down, v_norm_final_g):
    w = dict(meta_tokens=meta_tokens, norm_mix_g=norm_mix_g, w_in=w_in, b_forget=b_forget, w_branch_sb=w_branch_sb,
             w_branch_fox=w_branch_fox, w_out=w_out, norm_ffn_g=norm_ffn_g, w_up=w_up, conv_w=conv_w, w_down=w_down,
             norm_final_g=norm_final_g)
    m = dict(meta_tokens=m_meta_tokens, norm_mix_g=m_norm_mix_g, w_in=m_w_in, b_forget=m_b_forget,
             w_branch_sb=m_w_branch_sb, w_branch_fox=m_w_branch_fox, w_out=m_w_out, norm_ffn_g=m_norm_ffn_g,
             w_up=m_w_up, conv_w=m_conv_w, w_down=m_w_down, norm_final_g=m_norm_final_g)
    v = dict(meta_tokens=v_meta_tokens, norm_mix_g=v_norm_mix_g, w_in=v_w_in, b_forget=v_b_forget,
             w_branch_sb=v_w_branch_sb, w_branch_fox=v_w_branch_fox, w_out=v_w_out, norm_ffn_g=v_norm_ffn_g,
             w_up=v_w_up, conv_w=v_conv_w, w_down=v_w_down, norm_final_g=v_norm_final_g)
    shapes = {k: a.shape for k, a in w.items()}

    def shard2d(tree, table):
        return [tree[name].reshape(-1) for name, *_ in table]

    big = _all_gather(_pack_rows(shard2d(w, BIG), BIG_ROWS, BF16), "gather_matrices")
    small = _all_gather(_pack_rows(shard2d(w, SMALL), SMALL_ROWS, F32), "gather_small")
    full = {name: _to_full(a, axis) for (name, _, axis), a in
            zip(BIG + SMALL, list(_unpack(big.reshape(N_DEV, -1), BIG).values())
                + list(_unpack(small.reshape(N_DEV, -1), SMALL).values()))}

    loss, grad_x, grads = _local_step(
        x, loss_target, full["meta_tokens"], norm_mix_g.reshape(1, D), _in_to_padded(full["w_in"]), b_forget,
        full["w_branch_sb"], full["w_branch_fox"], full["w_out"], norm_ffn_g.reshape(1, D),
        _ff_interleave(full["w_up"]), _ff_interleave(full["conv_w"]), full["w_down"], norm_final_g.reshape(1, D))
    grads["w_in"] = _in_from_padded(grads["w_in"])
    grads["w_up"] = _ff_deinterleave(grads["w_up"])
    grads["conv_w"] = _ff_deinterleave(grads["conv_w"])

    parts = _pack_rows([_to_shards(grads[name], axis).reshape(N_DEV, -1) for name, _, axis in BIG + SMALL], PACK_ROWS, BF16)
    parts = _exchange(parts, "exchange_grads")
    packed = [_pack_rows(shard2d(tree, BIG + SMALL), PACK_ROWS, F32) for tree in (w, m, v)]
    outs = _sum_adamw(parts, *packed, PACK_TR, "sum_adamw")
    sharded = [_unpack(o.reshape(-1), BIG + SMALL) for o in outs]

    rparts = _pack_rows([jnp.pad(grads[name].reshape(-1), (0, (-n) % 128)) for name, n in REPL], REPL_ROWS, F32)
    rparts = _all_gather(rparts, "gather_replicated_grads")
    rpacked = [_pack_rows([jnp.pad(tree[name].reshape(-1), (0, (-n) % 128)) for name, n in REPL], REPL_ROWS, F32)
               for tree in (w, m, v)]
    routs = _sum_adamw(rparts, *rpacked, REPL_ROWS, "sum_adamw_replicated")
    rtable = tuple((name, (1, n + (-n) % 128)) for name, n in REPL)
    repl = [{name: a[0, :n] for (name, n), a in zip(REPL, _unpack(o.reshape(-1), rtable).values())} for o in routs]

    names = list(w)
    result = [lax.psum(loss, ("x", "y", "c")), grad_x]
    for k in range(4):
        for name in names:
            src = sharded[k] if name in sharded[k] else repl[k]
            result.append(src[name].reshape(shapes[name]))
    return tuple(result)
```

```python
import functools

import jax
import jax.numpy as jnp
from jax import lax
from jax.experimental import pallas as pl
from jax.experimental.pallas import tpu as pltpu

F32 = jnp.float32
BF16 = jnp.bfloat16

N_DEV = 8
LANES = 128
D = 1024
N_META = 16
SEQ = 2048
L_REAL = N_META + SEQ
LP = 2304
BQ = 256
NBLK = LP // BQ
HEAD = 64
NH = 8
W_ATT = NH * HEAD
PAIR_W = 3 * LANES
D_FF = 2816
IN_COLS = 5128
QKV = 6 * W_ATT
IN_P = 5376
GATE_COL = QKV
F_COL = QKV + 2 * D
FFC = 256
RMS_EPS = 1e-6
LR, B1, B2, EPS, WD, STEP = 0.001, 0.9, 0.999, 1e-08, 0.01, 10
VMEM_LIMIT = 56 * 1024 * 1024

MESH = pl.DeviceIdType.MESH
ANY = pl.BlockSpec(memory_space=pl.ANY)


def _cparams(*sem):
    return pltpu.CompilerParams(dimension_semantics=sem if sem else None, vmem_limit_bytes=VMEM_LIMIT)


def _all_gather(xs, name):
    n = len(xs)

    def body(*refs):
        x_refs, out_refs = refs[:n], refs[n:2 * n]
        send_sems, recv_sems, local_sems = refs[2 * n:]
        mx, my, mc = lax.axis_index("x"), lax.axis_index("y"), lax.axis_index("c")
        me, sibling = (mx, my, mc), (mx, my, 1 - mc)
        chips = [(1 - mx, my), (mx, 1 - my), (1 - mx, 1 - my)]

        def copy(a, k, block, to, own=False):
            px, py, pc = block
            slot = out_refs[a].at[4 * px + 2 * py + pc]
            return pltpu.make_async_remote_copy(
                src_ref=x_refs[a] if own else slot, dst_ref=slot,
                send_sem=send_sems.at[7 * a + k], recv_sem=recv_sems.at[7 * a + k],
                device_id=to, device_id_type=MESH)

        mine = [pltpu.make_async_copy(x_refs[a], out_refs[a].at[4 * mx + 2 * my + mc], local_sems.at[a]) for a in range(n)]
        for cp in mine:
            cp.start()
        first = []
        for a in range(n):
            first.append(copy(a, 0, me, sibling, own=True))
            first += [copy(a, 1 + j, me, (*chip, mc), own=True) for j, chip in enumerate(chips)]
        for cp in first:
            cp.start()
        passed = []
        for j, chip in enumerate(chips):
            for a in range(n):
                copy(a, 1 + j, (*chip, mc), me).wait_recv()
                fwd = copy(a, 4 + j, (*chip, mc), sibling)
                fwd.start()
                passed.append(fwd)
        for a in range(n):
            copy(a, 0, sibling, me).wait_recv()
            for j, chip in enumerate(chips):
                copy(a, 4 + j, (*chip, 1 - mc), me).wait_recv()
        for cp in first + passed:
            cp.wait_send()
        for cp in mine:
            cp.wait()

    return pl.pallas_call(
        body, name=name,
        out_shape=tuple(jax.ShapeDtypeStruct((N_DEV,) + x.shape, x.dtype) for x in xs),
        in_specs=[ANY] * n, out_specs=tuple([ANY] * n),
        scratch_shapes=[pltpu.SemaphoreType.DMA((7 * n,)), pltpu.SemaphoreType.DMA((7 * n,)),
                        pltpu.SemaphoreType.DMA((n,))],
    )(*xs)


def _exchange(srcs, name):
    n = len(srcs)

    def body(*refs):
        src_refs, dst_refs = refs[:n], refs[n:2 * n]
        send_sems, recv_sems, local_sems = refs[2 * n:]
        mx, my, mc = lax.axis_index("x"), lax.axis_index("y"), lax.axis_index("c")
        me_idx = 4 * mx + 2 * my + mc
        mine = [pltpu.make_async_copy(src_refs[a].at[me_idx], dst_refs[a].at[me_idx], local_sems.at[a]) for a in range(n)]
        for cp in mine:
            cp.start()
        copies = []
        for k in range(1, N_DEV):
            px, py, pc = mx ^ (k >> 2), my ^ ((k >> 1) & 1), mc ^ (k & 1)
            for a in range(n):
                copies.append(pltpu.make_async_remote_copy(
                    src_ref=src_refs[a].at[4 * px + 2 * py + pc], dst_ref=dst_refs[a].at[me_idx],
                    send_sem=send_sems.at[7 * a + k - 1], recv_sem=recv_sems.at[7 * a + k - 1],
                    device_id=(px, py, pc), device_id_type=MESH))
        for cp in copies:
            cp.start()
        for cp in copies:
            cp.wait_recv()
        for cp in copies:
            cp.wait_send()
        for cp in mine:
            cp.wait()

    return pl.pallas_call(
        body, name=name,
        out_shape=tuple(jax.ShapeDtypeStruct(s.shape, s.dtype) for s in srcs),
        in_specs=[ANY] * n, out_specs=tuple([ANY] * n),
        scratch_shapes=[pltpu.SemaphoreType.DMA((7 * n,)), pltpu.SemaphoreType.DMA((7 * n,)),
                        pltpu.SemaphoreType.DMA((n,))],
    )(*srcs)


def _pad_rows(front, body_rows, nseq, name):
    tail = LP - L_REAL

    def body(f_ref, b_ref, o_ref, z_ref, sems):
        z_ref[...] = jnp.zeros_like(z_ref)
        copies = []
        for s in range(nseq):
            copies.append(pltpu.make_async_copy(f_ref, o_ref.at[s, pl.ds(0, N_META)], sems.at[3 * s]))
            copies.append(pltpu.make_async_copy(b_ref.at[s], o_ref.at[s, pl.ds(N_META, SEQ)], sems.at[3 * s + 1]))
            copies.append(pltpu.make_async_copy(z_ref, o_ref.at[s, pl.ds(L_REAL, tail)], sems.at[3 * s + 2]))
        for cp in copies:
            cp.start()
        for cp in copies:
            cp.wait()

    return pl.pallas_call(
        body, name=name, out_shape=jax.ShapeDtypeStruct((nseq, LP, D), F32),
        in_specs=[pl.BlockSpec(memory_space=pltpu.VMEM), ANY], out_specs=ANY,
        scratch_shapes=[pltpu.VMEM((tail, D), F32), pltpu.SemaphoreType.DMA((3 * nseq,))])(front, body_rows)


def _real_rows(h, nseq, name):
    def body(h_ref, o_ref, sems):
        copies = [pltpu.make_async_copy(h_ref.at[s, pl.ds(N_META, SEQ)], o_ref.at[s], sems.at[s]) for s in range(nseq)]
        for cp in copies:
            cp.start()
        for cp in copies:
            cp.wait()

    return pl.pallas_call(
        body, name=name, out_shape=jax.ShapeDtypeStruct((nseq, SEQ, D), F32),
        in_specs=[ANY], out_specs=ANY, scratch_shapes=[pltpu.SemaphoreType.DMA((nseq,))])(h)


def _plan_cols(n_q, n_dcols, src_of):
    plan = {}
    for q in range(n_q):
        for dblk in range(n_dcols // LANES):
            segs, key, start = [], None, 0
            for lane in range(LANES + 1):
                new = None
                if lane < LANES:
                    src = src_of(q, dblk * LANES + lane)
                    if src is not None:
                        new = (src[0], src[1] // LANES, (lane - src[1] % LANES) % LANES)
                if new != key:
                    if key is not None:
                        segs.append((*key, start, lane))
                    key, start = new, lane
            plan[(q, dblk)] = segs
    return plan


def _relayout(src, n_q, n_dcols, src_of, out_dtype, tr, name):
    n_p, rows, scols = src.shape
    plan = _plan_cols(n_q, n_dcols, src_of)

    def body(s_ref, d_ref):
        lane = lax.broadcasted_iota(jnp.int32, (tr, LANES), 1)
        for (q, dblk), segs in plan.items():
            acc = jnp.zeros((tr, LANES), F32)
            for p, sblk, rot, lo, hi in segs:
                x = s_ref[p, :, sblk * LANES:(sblk + 1) * LANES].astype(F32)
                if rot:
                    x = pltpu.roll(x, rot, 1)
                acc = x if (lo, hi) == (0, LANES) else jnp.where((lane >= lo) & (lane < hi), x, acc)
            d_ref[q, :, dblk * LANES:(dblk + 1) * LANES] = acc.astype(out_dtype)

    return pl.pallas_call(
        body, name=name, out_shape=jax.ShapeDtypeStruct((n_q, rows, n_dcols), out_dtype), grid=(rows // tr,),
        in_specs=[pl.BlockSpec((n_p, tr, scols), lambda i: (0, i, 0))],
        out_specs=pl.BlockSpec((n_q, tr, n_dcols), lambda i: (0, i, 0)),
        compiler_params=_cparams("parallel"))(src)


def _in_padded_to_orig(d):
    if d < QKV:
        kind, r = divmod(d, 4 * PAIR_W)
        pair, r = divmod(r, PAIR_W)
        part, r = divmod(r, LANES)
        return kind * 3 * W_ATT + part * W_ATT + pair * LANES + r
    if d < F_COL:
        return d + NH
    if d < F_COL + NH:
        return d - 2 * D
    return None


_IN_ORIG_TO_PADDED = {_in_padded_to_orig(d): d for d in range(IN_P) if _in_padded_to_orig(d) is not None}


def _up_inter_to_orig(d):
    j, r = divmod(d, 2 * FFC)
    part, r = divmod(r, FFC)
    return part * D_FF + j * FFC + r


_UP_ORIG_TO_INTER = {_up_inter_to_orig(d): d for d in range(2 * D_FF)}
IN_SHARD = IN_COLS // N_DEV
UP_SHARD = 2 * D_FF // N_DEV
SHARD_P = 768
ATT_SHARD = D // N_DEV


def _gathered_to_full(n_shard, to_orig):
    def src_of(q, d):
        c = to_orig(d)
        return None if c is None else (c // n_shard, c % n_shard)
    return src_of


def _full_to_shards(n_shard, from_orig):
    def src_of(q, d):
        return (0, from_orig(q * n_shard + d)) if d < n_shard else None
    return src_of


def _matmul(a, b, *, out_dtype, tm, tn, tk, ta=False, tb=False, name):
    if ta:
        kdim, m = a.shape
    else:
        m, kdim = a.shape
    n = b.shape[0] if tb else b.shape[1]
    assert m % tm == 0 and n % tn == 0 and kdim % tk == 0, (name, a.shape, b.shape, tm, tn, tk)
    nk = kdim // tk

    def body(a_ref, b_ref, o_ref, *scratch):
        av, bv = a_ref[...], b_ref[...]
        if ta:
            p = lax.dot_general(av, bv, (((0,), (0,)), ((), ())), preferred_element_type=F32)
        elif tb:
            p = lax.dot_general(av, bv, (((1,), (1,)), ((), ())), preferred_element_type=F32)
        else:
            p = jnp.dot(av, bv, preferred_element_type=F32)
        if nk == 1:
            o_ref[...] = p.astype(o_ref.dtype)
        else:
            acc_ref, = scratch
            k = pl.program_id(2)

            @pl.when(k == 0)
            def _():
                acc_ref[...] = p

            @pl.when(k > 0)
            def _():
                acc_ref[...] += p

            @pl.when(k == nk - 1)
            def _():
                o_ref[...] = acc_ref[...].astype(o_ref.dtype)

    a_spec = pl.BlockSpec((tk, tm), lambda i, j, k: (k, i)) if ta else pl.BlockSpec((tm, tk), lambda i, j, k: (i, k))
    b_spec = pl.BlockSpec((tn, tk), lambda i, j, k: (j, k)) if tb else pl.BlockSpec((tk, tn), lambda i, j, k: (k, j))
    return pl.pallas_call(
        body, name=name,
        out_shape=jax.ShapeDtypeStruct((m, n), out_dtype),
        grid=(m // tm, n // tn, nk),
        in_specs=[a_spec, b_spec],
        out_specs=pl.BlockSpec((tm, tn), lambda i, j, k: (i, j)),
        scratch_shapes=[] if nk == 1 else [pltpu.VMEM((tm, tn), F32)],
        compiler_params=_cparams("parallel", "parallel", "arbitrary"),
    )(a, b)


TR = 288


def _rms(h):
    return lax.rsqrt(jnp.mean(h * h, axis=-1, keepdims=True) + RMS_EPS)


def _norm_fwd(h, delta, g, name):
    t = h.shape[0]
    row = pl.BlockSpec((TR, D), lambda i: (i, 0))
    vec = pl.BlockSpec((1, D), lambda i: (0, 0))

    if delta is None:
        def body(h_ref, g_ref, n_ref):
            hv = h_ref[...]
            n_ref[...] = ((hv * _rms(hv)) * g_ref[...]).astype(BF16)

        n = pl.pallas_call(
            body, name=name, out_shape=jax.ShapeDtypeStruct((t, D), BF16), grid=(t // TR,),
            in_specs=[row, vec], out_specs=row, compiler_params=_cparams("parallel"))(h, g)
        return h, n

    def body(h_ref, d_ref, g_ref, hn_ref, n_ref):
        hv = h_ref[...] + d_ref[...]
        hn_ref[...] = hv
        n_ref[...] = ((hv * _rms(hv)) * g_ref[...]).astype(BF16)

    return pl.pallas_call(
        body, name=name,
        out_shape=(jax.ShapeDtypeStruct((t, D), F32), jax.ShapeDtypeStruct((t, D), BF16)), grid=(t // TR,),
        in_specs=[row, row, vec], out_specs=(row, row), compiler_params=_cparams("parallel"))(h, delta, g)


def _rms_bwd_math(hv, dn, gv):
    r = _rms(hv)
    hr = hv * r
    dng = dn * gv
    dh = r * (dng - hr * jnp.mean(dng * hr, axis=-1, keepdims=True))
    return dh, dn * hr


def _final_loss_bwd(h1, delta, g, tgt, name):
    t = h1.shape[0]
    row = pl.BlockSpec((TR, D), lambda i: (i, 0))
    vec = pl.BlockSpec((1, D), lambda i: (0, 0))
    tiles_per_seq = LP // TR

    def body(h_ref, d_ref, g_ref, t_ref, loss_ref, dh_ref, dhb_ref, dg_ref):
        i = pl.program_id(0)
        hv = h_ref[...] + d_ref[...]
        gv = g_ref[...]
        r = _rms(hv)
        hr = hv * r
        y = hr * gv
        pos = (i % tiles_per_seq) * TR + lax.broadcasted_iota(jnp.int32, (TR, 1), 0)
        valid = (pos >= N_META) & (pos < L_REAL)
        err = jnp.where(valid, y - t_ref[...], 0.0)
        part = 0.5 * jnp.sum(jnp.mean(err * err, axis=-1, keepdims=True))
        dy = err * (1.0 / D)
        dng = dy * gv
        dh = r * (dng - hr * jnp.mean(dng * hr, axis=-1, keepdims=True))
        dh_ref[...] = dh
        dhb_ref[...] = dh.astype(BF16)
        dgp = jnp.sum(dy * hr, axis=0, keepdims=True)

        @pl.when(i == 0)
        def _():
            loss_ref[...] = jnp.zeros_like(loss_ref)
            dg_ref[...] = jnp.zeros_like(dg_ref)

        loss_ref[...] += part
        dg_ref[...] += dgp

    return pl.pallas_call(
        body, name=name,
        out_shape=(jax.ShapeDtypeStruct((8, 128), F32), jax.ShapeDtypeStruct((t, D), F32),
                   jax.ShapeDtypeStruct((t, D), BF16), jax.ShapeDtypeStruct((1, D), F32)),
        grid=(t // TR,),
        in_specs=[row, row, vec, row],
        out_specs=(pl.BlockSpec((8, 128), lambda i: (0, 0)), row, row, vec),
        compiler_params=_cparams("arbitrary"))(h1, delta, g, tgt)


def _norm_bwd(h, dn, g, dres, with_bf16, name):
    t = h.shape[0]
    row = pl.BlockSpec((TR, D), lambda i: (i, 0))
    vec = pl.BlockSpec((1, D), lambda i: (0, 0))

    def body(h_ref, dn_ref, g_ref, dres_ref, *outs):
        i = pl.program_id(0)
        dh, dgrow = _rms_bwd_math(h_ref[...], dn_ref[...], g_ref[...])
        dh = dh + dres_ref[...]
        outs[0][...] = dh
        if with_bf16:
            outs[1][...] = dh.astype(BF16)
        dg_ref = outs[-1]

        @pl.when(i == 0)
        def _():
            dg_ref[...] = jnp.zeros_like(dg_ref)

        dg_ref[...] += jnp.sum(dgrow, axis=0, keepdims=True)

    shapes = [jax.ShapeDtypeStruct((t, D), F32)]
    specs = [row]
    if with_bf16:
        shapes.append(jax.ShapeDtypeStruct((t, D), BF16))
        specs.append(row)
    shapes.append(jax.ShapeDtypeStruct((1, D), F32))
    specs.append(vec)
    return pl.pallas_call(
        body, name=name, out_shape=tuple(shapes), grid=(t // TR,),
        in_specs=[row, row, vec, row], out_specs=tuple(specs),
        compiler_params=_cparams("arbitrary"))(h, dn, g, dres)


GATE_BLK = GATE_COL // D


def _sigmoid(x):
    return 1.0 / (1.0 + jnp.exp(-x))


def _merge_fwd(p_sb, p_fx, proj, name):
    t = p_sb.shape[0]
    row = pl.BlockSpec((TR, D), lambda i: (i, 0))

    def body(ps_ref, pf_ref, gs_ref, gf_ref, o_ref):
        o_ref[...] = (_sigmoid(gs_ref[...]) * ps_ref[...] + _sigmoid(gf_ref[...]) * pf_ref[...]).astype(BF16)

    return pl.pallas_call(
        body, name=name, out_shape=jax.ShapeDtypeStruct((t, D), BF16), grid=(t // TR,),
        in_specs=[row, row, pl.BlockSpec((TR, D), lambda i: (i, GATE_BLK)),
                  pl.BlockSpec((TR, D), lambda i: (i, GATE_BLK + 1))],
        out_specs=row, compiler_params=_cparams("parallel"))(p_sb, p_fx, proj, proj)


def _merge_bwd(dm, p, proj, dproj, which, name):
    t = dm.shape[0]
    row = pl.BlockSpec((TR, D), lambda i: (i, 0))
    gate = pl.BlockSpec((TR, D), lambda i: (i, GATE_BLK + which))

    def body(dm_ref, p_ref, g_ref, *rest):
        dp_ref, dg_ref = rest[-2:]
        dmv = dm_ref[...]
        s = _sigmoid(g_ref[...])
        dp_ref[...] = (dmv * s).astype(BF16)
        dg_ref[...] = (dmv * p_ref[...] * s * (1.0 - s)).astype(BF16)

    out_shape = (jax.ShapeDtypeStruct((t, D), BF16), jax.ShapeDtypeStruct((t, IN_P), BF16))
    if dproj is None:
        return pl.pallas_call(
            body, name=name, out_shape=out_shape, grid=(t // TR,), in_specs=[row, row, gate],
            out_specs=(row, gate), compiler_params=_cparams("parallel"))(dm, p, proj)
    return pl.pallas_call(
        body, name=name, out_shape=out_shape, grid=(t // TR,), in_specs=[row, row, gate, ANY],
        out_specs=(row, gate), input_output_aliases={3: 1}, compiler_params=_cparams("parallel"))(dm, p, proj, dproj)


CH = 288


def _chunk(c, n=CH):
    return pl.ds(pl.multiple_of(c * CH, 8), n)


def _conv_taps(u_ref, c):
    x = u_ref[_chunk(c), :]
    prev = u_ref[pl.ds(pl.multiple_of(jnp.maximum(c * CH - 8, 0), 8), 8), :]
    xx = jnp.concatenate([jnp.where(c == 0, 0.0, prev), x], axis=0)
    return x, pltpu.roll(xx, 1, 0)[8:], pltpu.roll(xx, 2, 0)[8:]


def _conv_glu_fwd(u, cw, nseq, name):
    nblk = D_FF // FFC

    def body(u_ref, cw_ref, o_ref):
        cwv = cw_ref[...]

        def step(c, _):
            x, x1, x2 = _conv_taps(u_ref, c)
            uc = cwv[0:1, :] * x2 + cwv[1:2, :] * x1 + cwv[2:3, :] * x
            a, b = uc[:, :FFC], uc[:, FFC:]
            o_ref[_chunk(c), :] = (a * _sigmoid(a) * b).astype(BF16)
            return 0

        lax.fori_loop(0, LP // CH, step, 0)

    return pl.pallas_call(
        body, name=name, out_shape=jax.ShapeDtypeStruct((nseq * LP, D_FF), BF16), grid=(nseq, nblk),
        in_specs=[pl.BlockSpec((LP, 2 * FFC), lambda s, j: (s, j)), pl.BlockSpec((3, 2 * FFC), lambda s, j: (0, j))],
        out_specs=pl.BlockSpec((LP, FFC), lambda s, j: (s, j)),
        compiler_params=_cparams("parallel", "parallel"))(u, cw)


def _conv_glu_bwd(u, cw, dact, nseq, name):
    nblk = D_FF // FFC
    nch = LP // CH

    def body(u_ref, cw_ref, da_ref, du_ref, dcw_ref):
        s = pl.program_id(1)
        cwv = cw_ref[...]

        def step(k, carry):
            nxt, p0, p1, p2 = carry
            c = nch - 1 - k
            x, x1, x2 = _conv_taps(u_ref, c)
            uc = cwv[0:1, :] * x2 + cwv[1:2, :] * x1 + cwv[2:3, :] * x
            a, b = uc[:, :FFC], uc[:, FFC:]
            sa = _sigmoid(a)
            dactv = da_ref[_chunk(c), :]
            da = dactv * b * (sa * (1.0 + a * (1.0 - sa)))
            db = dactv * (a * sa)
            duc = jnp.concatenate([da, db], axis=1)
            dd = jnp.concatenate([duc, nxt], axis=0)
            du = (cwv[2:3, :] * duc + cwv[1:2, :] * pltpu.roll(dd, CH + 7, 0)[:CH]
                  + cwv[0:1, :] * pltpu.roll(dd, CH + 6, 0)[:CH])
            du_ref[_chunk(c), :] = du.astype(BF16)
            return (duc[:8], p0 + jnp.sum(duc * x2, axis=0, keepdims=True),
                    p1 + jnp.sum(duc * x1, axis=0, keepdims=True), p2 + jnp.sum(duc * x, axis=0, keepdims=True))

        zrow = jnp.zeros((1, 2 * FFC), F32)
        _, p0, p1, p2 = lax.fori_loop(0, nch, step, (jnp.zeros((8, 2 * FFC), F32), zrow, zrow, zrow))

        @pl.when(s == 0)
        def _():
            dcw_ref[...] = jnp.zeros_like(dcw_ref)

        dcw_ref[...] += jnp.concatenate([p0, p1, p2], axis=0)

    return pl.pallas_call(
        body, name=name,
        out_shape=(jax.ShapeDtypeStruct((nseq * LP, 2 * D_FF), BF16), jax.ShapeDtypeStruct((3, 2 * D_FF), F32)),
        grid=(nblk, nseq),
        in_specs=[pl.BlockSpec((LP, 2 * FFC), lambda j, s: (s, j)), pl.BlockSpec((3, 2 * FFC), lambda j, s: (0, j)),
                  pl.BlockSpec((LP, FFC), lambda j, s: (s, j))],
        out_specs=(pl.BlockSpec((LP, 2 * FFC), lambda j, s: (s, j)), pl.BlockSpec((3, 2 * FFC), lambda j, s: (0, j))),
        compiler_params=_cparams("parallel", "arbitrary"))(u, cw, dact)


F_BLK = F_COL // LANES
CB = 128


def _split3(x):
    hi = x.astype(BF16)
    r1 = x - hi.astype(F32)
    mid = r1.astype(BF16)
    lo = (r1 - mid.astype(F32)).astype(BF16)
    return hi, mid, lo


def _tri_dot(tri, x):
    hi, mid, lo = _split3(x)
    d = functools.partial(jnp.dot, preferred_element_type=F32)
    return d(tri, hi) + d(tri, mid) + d(tri, lo)


def _log_sigmoid(x):
    return jnp.minimum(x, 0.0) - jnp.log(1.0 + jnp.exp(-jnp.abs(x)))


def _gate_fwd(proj, bf, nseq, name):
    def body(f_ref, b_ref, c_ref):
        r_i = lax.broadcasted_iota(jnp.int32, (CB, CB), 0)
        c_i = lax.broadcasted_iota(jnp.int32, (CB, CB), 1)
        tri = (c_i <= r_i).astype(BF16)
        bv = b_ref[...]

        def step(k, carry):
            rows = pl.ds(pl.multiple_of(k * CB, CB), CB)
            lf = _log_sigmoid(f_ref[rows, :] + bv)
            c_ref[rows, :] = _tri_dot(tri, lf) + carry
            return carry + jnp.sum(lf, axis=0, keepdims=True)

        lax.fori_loop(0, LP // CB, step, jnp.zeros((1, LANES), F32))

    return pl.pallas_call(
        body, name=name, out_shape=jax.ShapeDtypeStruct((nseq * LP, LANES), F32), grid=(nseq,),
        in_specs=[pl.BlockSpec((LP, LANES), lambda s: (s, F_BLK)), pl.BlockSpec((1, LANES), lambda s: (0, 0))],
        out_specs=pl.BlockSpec((LP, LANES), lambda s: (s, 0)),
        compiler_params=_cparams("parallel"))(proj, bf)


def _gate_bwd(proj, bf, dc, dproj, nseq, name):
    def body(f_ref, b_ref, dc_ref, _, df_ref, db_ref):
        s = pl.program_id(0)
        r_i = lax.broadcasted_iota(jnp.int32, (CB, CB), 0)
        c_i = lax.broadcasted_iota(jnp.int32, (CB, CB), 1)
        tri = (c_i >= r_i).astype(BF16)
        bv = b_ref[...]

        def step(kk, carry):
            carry_c, carry_b = carry
            k = LP // CB - 1 - kk
            rows = pl.ds(pl.multiple_of(k * CB, CB), CB)
            dcv = dc_ref[rows, :]
            dlf = _tri_dot(tri, dcv) + carry_c
            df = dlf * _sigmoid(-(f_ref[rows, :] + bv))
            df_ref[rows, :] = jnp.concatenate([df, jnp.zeros_like(df)], axis=1).astype(BF16)
            return carry_c + jnp.sum(dcv, axis=0, keepdims=True), carry_b + jnp.sum(df, axis=0, keepdims=True)

        zero = jnp.zeros((1, LANES), F32)
        _, dbp = lax.fori_loop(0, LP // CB, step, (zero, zero))

        @pl.when(s == 0)
        def _():
            db_ref[...] = jnp.zeros_like(db_ref)

        db_ref[...] += dbp

    return pl.pallas_call(
        body, name=name,
        out_shape=(jax.ShapeDtypeStruct(dproj.shape, BF16), jax.ShapeDtypeStruct((1, LANES), F32)), grid=(nseq,),
        in_specs=[pl.BlockSpec((LP, LANES), lambda s: (s, F_BLK)), pl.BlockSpec((1, LANES), lambda s: (0, 0)),
                  pl.BlockSpec((LP, LANES), lambda s: (s, 0)), ANY],
        out_specs=(pl.BlockSpec((LP, 2 * LANES), lambda s: (s, F_COL // (2 * LANES))), pl.BlockSpec((1, LANES), lambda s: (0, 0))),
        input_output_aliases={3: 0},
        compiler_params=_cparams("arbitrary"))(proj, bf, dc, dproj)


SCALE = 0.125
NEG = -1e30


def _dot_nt(a, b):
    return lax.dot_general(a, b, (((1,), (1,)), ((), ())), preferred_element_type=F32)


def _dot_tn(a, b):
    return lax.dot_general(a, b, (((0,), (0,)), ((), ())), preferred_element_type=F32)


def _dot(a, b):
    return jnp.dot(a, b, preferred_element_type=F32)


def _blk(i):
    return pl.ds(pl.multiple_of(i * BQ, BQ), BQ)


def _tile_iotas():
    return lax.broadcasted_iota(jnp.int32, (BQ, BQ), 0), lax.broadcasted_iota(jnp.int32, (BQ, BQ), 1)


def _lane_iota():
    return lax.broadcasted_iota(jnp.int32, (BQ, LANES), 1)


def _head_lanes(hh):
    lane = _lane_iota()
    return (lane >= hh * HEAD) & (lane < (hh + 1) * HEAD)


def _only(mask, x):
    return jnp.where(mask, x, jnp.zeros_like(x))


def _pick_lane(x, idx):
    return jnp.sum(jnp.where(_lane_iota() == idx, x, 0.0), axis=1, keepdims=True)


def _load_qkv(p_ref, q_s, k_s, v_s):
    q_s[...] = p_ref[:, 0:LANES].astype(BF16)
    k_s[...] = p_ref[:, LANES:2 * LANES].astype(BF16)
    v_s[...] = p_ref[:, 2 * LANES:3 * LANES].astype(BF16)


def _sb_weights(qi, kj, strict, r_after, u_suf):
    z = _dot_nt(qi, kj) * SCALE
    sp = jnp.maximum(z, 0.0) + jnp.log(1.0 + jnp.exp(-jnp.abs(z)))
    lk = jnp.where(strict, -sp, 0.0)
    hi = lk.astype(BF16)
    lo = (lk - hi.astype(F32)).astype(BF16)
    suf = _dot(hi, u_suf) + _dot(lo, u_suf)
    w = jnp.where(strict, jnp.exp(z - sp + r_after + suf), 0.0)
    return w, sp, lk


def _pair_spec(kind):
    return pl.BlockSpec((LP, PAIR_W), lambda s, p: (s, 4 * kind + p))


HEADS_SPEC = pl.BlockSpec((LP, LANES), lambda s, p: (s, p))
SEQ_SPEC = pl.BlockSpec((LP, LANES), lambda s, p: (s, 0))
QKV_SCRATCH = [pltpu.VMEM((LP, LANES), BF16)] * 3
RS_STRIDE = 16


def _sb_fwd(proj, nseq, name):
    t = nseq * LP

    def body(p_ref, o_ref, rs_ref, q_s, k_s, v_s, acc_ref, r_ref, rb_ref):
        _load_qkv(p_ref, q_s, k_s, v_s)
        row, col = _tile_iotas()
        u_suf = (row > col).astype(BF16)
        lane = _lane_iota()

        def qblock(i, _):
            acc_ref[...] = jnp.zeros_like(acc_ref)
            rb_ref[...] = jnp.zeros_like(rb_ref)

            def head(hh, _):
                mine = _head_lanes(hh)
                qi = _only(mine, q_s[_blk(i), :])
                r_ref[...] = jnp.zeros_like(r_ref)

                def kblock(jj, _):
                    j = i - jj
                    strict = (col + j * BQ) < (row + i * BQ)
                    r_after = r_ref[...]
                    w, _, lk = _sb_weights(qi, k_s[_blk(j), :], strict, r_after, u_suf)
                    acc_ref[...] += _dot(w.astype(BF16), _only(mine, v_s[_blk(j), :]))
                    rb_ref[...] = jnp.where(lane == RS_STRIDE * hh + j, r_after, rb_ref[...])
                    r_ref[...] = r_after + jnp.sum(lk, axis=1, keepdims=True)
                    return 0

                lax.fori_loop(0, i + 1, kblock, 0)
                return 0

            lax.fori_loop(0, 2, head, 0)
            o_ref[_blk(i), :] = acc_ref[...].astype(BF16)
            rs_ref[_blk(i), :] = rb_ref[...]
            return 0

        lax.fori_loop(0, NBLK, qblock, 0)

    return pl.pallas_call(
        body, name=name,
        out_shape=(jax.ShapeDtypeStruct((t, W_ATT), BF16), jax.ShapeDtypeStruct((t, W_ATT), F32)),
        grid=(nseq, NH // 2), in_specs=[_pair_spec(0)], out_specs=(HEADS_SPEC, HEADS_SPEC),
        scratch_shapes=QKV_SCRATCH + [pltpu.VMEM((BQ, LANES), F32), pltpu.VMEM((BQ, 1), F32), pltpu.VMEM((BQ, LANES), F32)],
        compiler_params=_cparams("parallel", "parallel"))(proj)


def _sb_bwd(proj, do, rs, dproj, nseq, name):
    def body(p_ref, do_ref, rs_ref, _, dp_ref, q_s, k_s, v_s, dqa_ref, dka_ref, dva_ref, ep_ref):
        _load_qkv(p_ref, q_s, k_s, v_s)
        row, col = _tile_iotas()
        u_suf = (row > col).astype(BF16)
        u_pre = (row < col).astype(BF16)
        dka_ref[...] = jnp.zeros_like(dka_ref)
        dva_ref[...] = jnp.zeros_like(dva_ref)

        def qblock(i, _):
            rb = rs_ref[_blk(i), :]
            dqa_ref[...] = jnp.zeros_like(dqa_ref)

            def head(hh, _):
                mine = _head_lanes(hh)
                qi = _only(mine, q_s[_blk(i), :])
                doi = _only(mine, do_ref[_blk(i), :])
                ep_ref[...] = jnp.zeros_like(ep_ref)

                def kblock(j, _):
                    strict = (col + j * BQ) < (row + i * BQ)
                    r_after = _pick_lane(rb, RS_STRIDE * hh + j)
                    kj, vj = k_s[_blk(j), :], v_s[_blk(j), :]
                    w, sp, _ = _sb_weights(qi, kj, strict, r_after, u_suf)
                    e = _dot_nt(doi, vj) * w
                    e_pre = ep_ref[...] + _dot(e.astype(BF16), u_pre)
                    ep_ref[...] += jnp.sum(e, axis=1, keepdims=True)
                    sneg = jnp.exp(-sp)
                    dz = jnp.where(strict, e * sneg - (1.0 - sneg) * e_pre, 0.0).astype(BF16)
                    dqa_ref[...] += _dot(dz, _only(mine, kj))
                    dka_ref[_blk(j), :] += _dot_tn(dz, qi)
                    dva_ref[_blk(j), :] += _dot_tn(w.astype(BF16), doi)
                    return 0

                lax.fori_loop(0, i + 1, kblock, 0)
                return 0

            lax.fori_loop(0, 2, head, 0)
            dp_ref[_blk(i), 0:LANES] = (dqa_ref[...] * SCALE).astype(BF16)
            return 0

        lax.fori_loop(0, NBLK, qblock, 0)
        dp_ref[:, LANES:2 * LANES] = (dka_ref[...] * SCALE).astype(BF16)
        dp_ref[:, 2 * LANES:3 * LANES] = dva_ref[...].astype(BF16)

    return pl.pallas_call(
        body, name=name, out_shape=jax.ShapeDtypeStruct(dproj.shape, BF16), grid=(nseq, NH // 2),
        in_specs=[_pair_spec(0), HEADS_SPEC, HEADS_SPEC, ANY], out_specs=_pair_spec(0),
        input_output_aliases={3: 0},
        scratch_shapes=QKV_SCRATCH + [pltpu.VMEM((BQ, LANES), F32), pltpu.VMEM((LP, LANES), F32),
                                      pltpu.VMEM((LP, LANES), F32), pltpu.VMEM((BQ, 1), F32)],
        compiler_params=_cparams("parallel", "parallel"))(proj, do, rs, dproj)


CROW_SPEC = pl.BlockSpec((None, NH, LP), lambda s, p: (s, 0, 0))


def _fox_scores(qi, kj, cq, ck, causal):
    z = _dot_nt(qi, kj) * SCALE + (cq - ck)
    return jnp.where(causal, z, NEG)


def _key_cols(cr_ref, head, j):
    return cr_ref[pl.ds(head, 1), pl.ds(pl.multiple_of(j * BQ, BQ), BQ)]


def _fox_fwd(proj, c, crow, nseq, name):
    t = nseq * LP

    def body(p_ref, c_ref, cr_ref, o_ref, o32_ref, lse_ref, q_s, k_s, v_s, out_ref, acc_ref, m_ref, l_ref, lb_ref):
        _load_qkv(p_ref, q_s, k_s, v_s)
        row, col = _tile_iotas()
        lane = _lane_iota()
        pair = pl.program_id(1)

        def qblock(i, _):
            out_ref[...] = jnp.zeros_like(out_ref)
            lb_ref[...] = jnp.zeros_like(lb_ref)
            cblk = c_ref[_blk(i), :]

            def head(hh, _):
                mine = _head_lanes(hh)
                qi = _only(mine, q_s[_blk(i), :])
                cq = _pick_lane(cblk, 2 * pair + hh)
                acc_ref[...] = jnp.zeros_like(acc_ref)
                m_ref[...] = jnp.full_like(m_ref, NEG)
                l_ref[...] = jnp.zeros_like(l_ref)

                def kblock(j, _):
                    causal = (col + j * BQ) <= (row + i * BQ)
                    z = _fox_scores(qi, k_s[_blk(j), :], cq, _key_cols(cr_ref, 2 * pair + hh, j), causal)
                    m_old = m_ref[...]
                    m_new = jnp.maximum(m_old, jnp.max(z, axis=1, keepdims=True))
                    alpha = jnp.exp(m_old - m_new)
                    p = jnp.exp(z - m_new)
                    l_ref[...] = alpha * l_ref[...] + jnp.sum(p, axis=1, keepdims=True)
                    acc_ref[...] = alpha * acc_ref[...] + _dot(p.astype(BF16), _only(mine, v_s[_blk(j), :]))
                    m_ref[...] = m_new
                    return 0

                lax.fori_loop(0, i + 1, kblock, 0)
                out_ref[...] += acc_ref[...] / l_ref[...]
                lb_ref[...] = jnp.where(lane == hh, m_ref[...] + jnp.log(l_ref[...]), lb_ref[...])
                return 0

            lax.fori_loop(0, 2, head, 0)
            o_ref[_blk(i), :] = out_ref[...].astype(BF16)
            o32_ref[_blk(i), :] = out_ref[...]
            lse_ref[_blk(i), :] = lb_ref[...]
            return 0

        lax.fori_loop(0, NBLK, qblock, 0)

    return pl.pallas_call(
        body, name=name,
        out_shape=(jax.ShapeDtypeStruct((t, W_ATT), BF16), jax.ShapeDtypeStruct((t, W_ATT), F32),
                   jax.ShapeDtypeStruct((t, W_ATT), F32)),
        grid=(nseq, NH // 2), in_specs=[_pair_spec(1), SEQ_SPEC, CROW_SPEC], out_specs=(HEADS_SPEC, HEADS_SPEC, HEADS_SPEC),
        scratch_shapes=QKV_SCRATCH + [pltpu.VMEM((BQ, LANES), F32), pltpu.VMEM((BQ, LANES), F32), pltpu.VMEM((BQ, 1), F32),
                                      pltpu.VMEM((BQ, 1), F32), pltpu.VMEM((BQ, LANES), F32)],
        compiler_params=_cparams("parallel", "parallel"))(proj, c, crow)


def _fox_bwd(proj, c, crow, o32, lse, do, dproj, nseq, name):
    t = nseq * LP

    def body(p_ref, c_ref, cr_ref, o_ref, lse_ref, do_ref, _, dp_ref, dck_ref, dcq_ref,
             q_s, k_s, v_s, dqa_ref, dka_ref, dva_ref, rsum_ref):
        _load_qkv(p_ref, q_s, k_s, v_s)
        row, col = _tile_iotas()
        lane = _lane_iota()
        sub = lax.broadcasted_iota(jnp.int32, (NH, BQ), 0)
        pair = pl.program_id(1)
        dka_ref[...] = jnp.zeros_like(dka_ref)
        dva_ref[...] = jnp.zeros_like(dva_ref)

        @pl.when(pair == 0)
        def _():
            dck_ref[...] = jnp.zeros_like(dck_ref)
            dcq_ref[...] = jnp.zeros_like(dcq_ref)

        def qblock(i, _):
            dqa_ref[...] = jnp.zeros_like(dqa_ref)
            cblk = c_ref[_blk(i), :]
            lblk = lse_ref[_blk(i), :]

            def head(hh, _):
                mine = _head_lanes(hh)
                qi = _only(mine, q_s[_blk(i), :])
                doi = _only(mine, do_ref[_blk(i), :])
                cq = _pick_lane(cblk, 2 * pair + hh)
                lse_i = _pick_lane(lblk, hh)
                delta = jnp.sum(doi.astype(F32) * o_ref[_blk(i), :], axis=1, keepdims=True)
                rsum_ref[...] = jnp.zeros_like(rsum_ref)

                def kblock(j, _):
                    causal = (col + j * BQ) <= (row + i * BQ)
                    kj, vj = k_s[_blk(j), :], v_s[_blk(j), :]
                    z = _fox_scores(qi, kj, cq, _key_cols(cr_ref, 2 * pair + hh, j), causal)
                    p = jnp.exp(z - lse_i)
                    ds = p * (_dot_nt(doi, vj) - delta)
                    dsb = ds.astype(BF16)
                    dqa_ref[...] += _dot(dsb, _only(mine, kj))
                    dka_ref[_blk(j), :] += _dot_tn(dsb, qi)
                    dva_ref[_blk(j), :] += _dot_tn(p.astype(BF16), doi)
                    keys = pl.ds(pl.multiple_of(j * BQ, BQ), BQ)
                    dck_ref[:, keys] = dck_ref[:, keys] - jnp.where(sub == 2 * pair + hh, jnp.sum(ds, axis=0, keepdims=True), 0.0)
                    rsum_ref[...] += jnp.sum(ds, axis=1, keepdims=True)
                    return 0

                lax.fori_loop(0, i + 1, kblock, 0)
                dcq_ref[_blk(i), :] += jnp.where(lane == 2 * pair + hh, rsum_ref[...], 0.0)
                return 0

            lax.fori_loop(0, 2, head, 0)
            dp_ref[_blk(i), 0:LANES] = (dqa_ref[...] * SCALE).astype(BF16)
            return 0

        lax.fori_loop(0, NBLK, qblock, 0)
        dp_ref[:, LANES:2 * LANES] = (dka_ref[...] * SCALE).astype(BF16)
        dp_ref[:, 2 * LANES:3 * LANES] = dva_ref[...].astype(BF16)

    return pl.pallas_call(
        body, name=name,
        out_shape=(jax.ShapeDtypeStruct(dproj.shape, BF16), jax.ShapeDtypeStruct((nseq, NH, LP), F32),
                   jax.ShapeDtypeStruct((t, LANES), F32)),
        grid=(nseq, NH // 2),
        in_specs=[_pair_spec(1), SEQ_SPEC, CROW_SPEC, HEADS_SPEC, HEADS_SPEC, HEADS_SPEC, ANY],
        out_specs=(_pair_spec(1), CROW_SPEC, SEQ_SPEC),
        input_output_aliases={6: 0},
        scratch_shapes=QKV_SCRATCH + [pltpu.VMEM((BQ, LANES), F32), pltpu.VMEM((LP, LANES), F32),
                                      pltpu.VMEM((LP, LANES), F32), pltpu.VMEM((BQ, 1), F32)],
        compiler_params=_cparams("parallel", "arbitrary"))(proj, c, crow, o32, lse, do, dproj)


def _adamw_math(w, g, m, v):
    m = B1 * m + (1.0 - B1) * g
    v = B2 * v + (1.0 - B2) * (g * g)
    m_hat = m / (1.0 - B1 ** STEP)
    v_hat = v / (1.0 - B2 ** STEP)
    delta = -LR * (m_hat / (jnp.sqrt(v_hat) + EPS) + WD * w)
    return delta, m, v


def _sum_adamw(parts, w, m, v, tr, name):
    rows, cols = w.shape
    cp = parts.shape[2]
    assert rows % tr == 0 and parts.shape[1] == rows

    def body(p_ref, w_ref, m_ref, v_ref, g_ref, d_ref, nm_ref, nv_ref):
        gsum = p_ref[0].astype(F32)
        for s in range(1, N_DEV):
            gsum = gsum + p_ref[s].astype(F32)
        gsum = gsum[:, :cols]
        d, nm, nv = _adamw_math(w_ref[...], gsum, m_ref[...], v_ref[...])
        g_ref[...] = gsum
        d_ref[...] = d
        nm_ref[...] = nm
        nv_ref[...] = nv

    blk = pl.BlockSpec((tr, cols), lambda i: (i, 0))
    out = jax.ShapeDtypeStruct((rows, cols), F32)
    return pl.pallas_call(
        body, name=name, out_shape=(out, out, out, out), grid=(rows // tr,),
        in_specs=[pl.BlockSpec((N_DEV, tr, cp), lambda i: (0, i, 0)), blk, blk, blk],
        out_specs=(blk, blk, blk, blk), compiler_params=_cparams("parallel"))(parts, w, m, v)


def _local_step(x, tgt, meta, g_mix, w_in_p, b_forget, w_bsb, w_bfx, w_out, g_ffn, w_up_i, cw_i, w_down, g_final):
    nseq = x.shape[0]
    t = nseq * LP
    tm = LP // 2
    mm = functools.partial(_matmul, tm=tm)

    h0 = _pad_rows(meta, x, nseq, "pad_x").reshape(t, D)
    tgt_p = _pad_rows(jnp.zeros((N_META, D), F32), tgt, nseq, "pad_target").reshape(t, D)
    bf = jnp.pad(b_forget.reshape(1, NH), ((0, 0), (0, LANES - NH)))

    _, n1 = _norm_fwd(h0, None, g_mix, "norm1")
    proj = mm(n1, w_in_p, out_dtype=F32, tn=1792, tk=D, name="in_proj")
    c = _gate_fwd(proj, bf, nseq, "gate_fwd")
    crow = c[:, :NH].reshape(nseq, LP, NH).transpose(0, 2, 1)
    o_sb, rs = _sb_fwd(proj, nseq, "sb_fwd")
    o_fx, o_fx32, lse = _fox_fwd(proj, c, crow, nseq, "fox_fwd")
    p_sb = mm(o_sb, w_bsb, out_dtype=F32, tn=D, tk=W_ATT, name="branch_sb")
    p_fx = mm(o_fx, w_bfx, out_dtype=F32, tn=D, tk=W_ATT, name="branch_fox")
    merged = _merge_fwd(p_sb, p_fx, proj, "merge_fwd")
    mix = mm(merged, w_out, out_dtype=F32, tn=D, tk=D, name="out_proj")
    h1, n2 = _norm_fwd(h0, mix, g_ffn, "norm2")
    u = mm(n2, w_up_i, out_dtype=F32, tn=1408, tk=D, name="up_proj")
    act = _conv_glu_fwd(u, cw_i, nseq, "conv_glu_fwd")
    ffn = mm(act, w_down, out_dtype=F32, tn=D, tk=1408, name="down_proj")

    loss, dh2, dh2b, dg_final = _final_loss_bwd(h1, ffn, g_final, tgt_p, "final")
    d_down = _matmul(act, dh2b, out_dtype=BF16, tm=1408, tn=D, tk=tm, ta=True, name="d_w_down")
    dact = mm(dh2b, w_down, out_dtype=F32, tn=1408, tk=D, tb=True, name="d_act")
    du, d_cw = _conv_glu_bwd(u, cw_i, dact, nseq, "conv_glu_bwd")
    d_up = _matmul(n2, du, out_dtype=BF16, tm=D, tn=1408, tk=tm, ta=True, name="d_w_up")
    dn2 = mm(du, w_up_i, out_dtype=F32, tn=D, tk=1408, tb=True, name="d_n2")
    dh1, dh1b, dg_ffn = _norm_bwd(h1, dn2, g_ffn, dh2, True, "norm2_bwd")
    d_out = _matmul(merged, dh1b, out_dtype=BF16, tm=D, tn=D, tk=tm, ta=True, name="d_w_out")
    dmerged = mm(dh1b, w_out, out_dtype=F32, tn=D, tk=D, tb=True, name="d_merged")
    dp_sb, dproj = _merge_bwd(dmerged, p_sb, proj, None, 0, "merge_bwd_sb")
    dp_fx, dproj = _merge_bwd(dmerged, p_fx, proj, dproj, 1, "merge_bwd_fox")
    d_bsb = _matmul(o_sb, dp_sb, out_dtype=BF16, tm=W_ATT, tn=D, tk=tm, ta=True, name="d_w_branch_sb")
    d_bfx = _matmul(o_fx, dp_fx, out_dtype=BF16, tm=W_ATT, tn=D, tk=tm, ta=True, name="d_w_branch_fox")
    do_sb = mm(dp_sb, w_bsb, out_dtype=BF16, tn=W_ATT, tk=D, tb=True, name="d_o_sb")
    do_fx = mm(dp_fx, w_bfx, out_dtype=BF16, tn=W_ATT, tk=D, tb=True, name="d_o_fox")
    dproj = _sb_bwd(proj, do_sb, rs, dproj, nseq, "sb_bwd")
    dproj, dck, dcq = _fox_bwd(proj, c, crow, o_fx32, lse, do_fx, dproj, nseq, "fox_bwd")
    dc = dcq + jnp.pad(dck.transpose(0, 2, 1).reshape(t, NH), ((0, 0), (0, LANES - NH)))
    dproj, d_bf = _gate_bwd(proj, bf, dc, dproj, nseq, "gate_bwd")
    d_in = _matmul(n1, dproj, out_dtype=BF16, tm=D, tn=1792, tk=tm, ta=True, name="d_w_in")
    dn1 = mm(dproj, w_in_p, out_dtype=F32, tn=D, tk=1792, tb=True, name="d_n1")
    dh0, dg_mix = _norm_bwd(h0, dn1, g_mix, dh1, False, "norm1_bwd")
    dh0 = dh0.reshape(nseq, LP, D)
    grads = dict(meta_tokens=jnp.sum(dh0[:, :N_META], axis=0), norm_mix_g=dg_mix, w_in=d_in, b_forget=d_bf[:, :NH],
                 w_branch_sb=d_bsb, w_branch_fox=d_bfx, w_out=d_out, norm_ffn_g=dg_ffn, w_up=d_up, conv_w=d_cw,
                 w_down=d_down, norm_final_g=dg_final)
    return loss[0, 0], _real_rows(dh0, nseq, "grad_x"), grads


REPL = (("norm_mix_g", D), ("norm_ffn_g", D), ("norm_final_g", D), ("b_forget", LANES))
REPL_ROWS = 32


def _pack_repl(tree):
    rows = [jnp.pad(tree[name].reshape(-1), (0, n - tree[name].size)).reshape(-1, LANES) for name, n in REPL]
    packed = jnp.concatenate(rows, axis=0)
    return jnp.pad(packed, ((0, REPL_ROWS - packed.shape[0]), (0, 0)))


def _unpack_repl(packed, shapes):
    out, r = {}, 0
    for name, n in REPL:
        size = 1
        for s in shapes[name]:
            size *= s
        out[name] = packed[r:r + n // LANES].reshape(-1)[:size].reshape(shapes[name])
        r += n // LANES
    return out


def kernel(x, meta_tokens, norm_mix_g, w_in, b_forget, w_branch_sb, w_branch_fox, w_out, norm_ffn_g, w_up, conv_w, w_down, norm_final_g, loss_target, m_meta_tokens, m_norm_mix_g, m_w_in, m_b_forget, m_w_branch_sb, m_w_branch_fox, m_w_out, m_norm_ffn_g, m_w_up, m_conv_w, m_w_down, m_norm_final_g, v_meta_tokens, v_norm_mix_g, v_w_in, v_b_forget, v_w_branch_sb, v_w_branch_fox, v_w_out, v_norm_ffn_g, v_w_up, v_conv_w, v_w_down, v_norm_final_g):
    w = dict(meta_tokens=meta_tokens, norm_mix_g=norm_mix_g, w_in=w_in, b_forget=b_forget, w_branch_sb=w_branch_sb,
             w_branch_fox=w_branch_fox, w_out=w_out, norm_ffn_g=norm_ffn_g, w_up=w_up, conv_w=conv_w, w_down=w_down,
             norm_final_g=norm_final_g)
    m = dict(meta_tokens=m_meta_tokens, norm_mix_g=m_norm_mix_g, w_in=m_w_in, b_forget=m_b_forget,
             w_branch_sb=m_w_branch_sb, w_branch_fox=m_w_branch_fox, w_out=m_w_out, norm_ffn_g=m_norm_ffn_g,
             w_up=m_w_up, conv_w=m_conv_w, w_down=m_w_down, norm_final_g=m_norm_final_g)
    v = dict(meta_tokens=v_meta_tokens, norm_mix_g=v_norm_mix_g, w_in=v_w_in, b_forget=v_b_forget,
             w_branch_sb=v_w_branch_sb, w_branch_fox=v_w_branch_fox, w_out=v_w_out, norm_ffn_g=v_norm_ffn_g,
             w_up=v_w_up, conv_w=v_conv_w, w_down=v_w_down, norm_final_g=v_norm_final_g)
    shapes = {k: a.shape for k, a in w.items()}
    sharded = ("w_in", "w_branch_sb", "w_branch_fox", "w_out", "w_up", "w_down", "conv_w", "meta_tokens")
    mat = lambda tree, name: tree[name].reshape(tree[name].shape[-2:])

    def lane_pad(a, width):
        return jnp.pad(a, ((0, 0), (0, width - a.shape[1])))

    g_in, g_bsb, g_bfx, g_out, g_up, g_down, g_cw, g_meta = _all_gather(
        [lane_pad(mat(w, "w_in").astype(BF16), SHARD_P), mat(w, "w_branch_sb").astype(BF16),
         mat(w, "w_branch_fox").astype(BF16), mat(w, "w_out").astype(BF16), lane_pad(mat(w, "w_up").astype(BF16), SHARD_P),
         mat(w, "w_down").astype(BF16), mat(w, "conv_w"), mat(w, "meta_tokens")], "gather_weights")
    w_in_p = _relayout(g_in, 1, IN_P, _gathered_to_full(IN_SHARD, _in_padded_to_orig), BF16, 256, "w_in_cols")[0]
    w_up_i = _relayout(g_up, 1, 2 * D_FF, _gathered_to_full(UP_SHARD, _up_inter_to_orig), BF16, 256, "w_up_cols")[0]
    w_bsb = _relayout(g_bsb, 1, D, _gathered_to_full(ATT_SHARD, lambda d: d), BF16, 256, "w_bsb_cols")[0]
    w_bfx = _relayout(g_bfx, 1, D, _gathered_to_full(ATT_SHARD, lambda d: d), BF16, 256, "w_bfx_cols")[0]
    cw_full = g_cw.transpose(1, 0, 2).reshape(3, 2 * D_FF)
    cw_i = cw_full.reshape(3, 2, D_FF // FFC, FFC).transpose(0, 2, 1, 3).reshape(3, 2 * D_FF)
    meta_full = g_meta.transpose(1, 0, 2).reshape(N_META, D)

    loss, grad_x, grads = _local_step(
        x, loss_target, meta_full, norm_mix_g.reshape(1, D), w_in_p, b_forget, w_bsb, w_bfx, g_out.reshape(D, D),
        norm_ffn_g.reshape(1, D), w_up_i, cw_i, g_down.reshape(D_FF, D), norm_final_g.reshape(1, D))

    d_cw = grads["conv_w"].reshape(3, D_FF // FFC, 2, FFC).transpose(0, 2, 1, 3).reshape(3, 2 * D_FF)
    parts = _exchange(
        [_relayout(grads["w_in"][None], N_DEV, SHARD_P, _full_to_shards(IN_SHARD, _IN_ORIG_TO_PADDED.get), BF16, 256, "d_w_in_shards"),
         _relayout(grads["w_branch_sb"][None], N_DEV, ATT_SHARD, _full_to_shards(ATT_SHARD, lambda c: c), BF16, 256, "d_w_bsb_shards"),
         _relayout(grads["w_branch_fox"][None], N_DEV, ATT_SHARD, _full_to_shards(ATT_SHARD, lambda c: c), BF16, 256, "d_w_bfx_shards"),
         grads["w_out"].reshape(N_DEV, D // N_DEV, D),
         _relayout(grads["w_up"][None], N_DEV, SHARD_P, _full_to_shards(UP_SHARD, _UP_ORIG_TO_INTER.get), BF16, 256, "d_w_up_shards"),
         grads["w_down"].reshape(N_DEV, D_FF // N_DEV, D),
         d_cw.reshape(3, N_DEV, UP_SHARD).transpose(1, 0, 2),
         grads["meta_tokens"].reshape(N_META, N_DEV, ATT_SHARD).transpose(1, 0, 2)], "exchange_grads")
    tiles = dict(w_in=256, w_branch_sb=256, w_branch_fox=256, w_out=D // N_DEV, w_up=256, w_down=D_FF // N_DEV,
                 conv_w=3, meta_tokens=N_META)
    new = {name: _sum_adamw(p, mat(w, name), mat(m, name), mat(v, name), tiles[name], "adamw_" + name)
           for name, p in zip(sharded, parts)}

    rparts, = _all_gather([_pack_repl(grads)], "gather_replicated_grads")
    routs = _sum_adamw(rparts, _pack_repl(w), _pack_repl(m), _pack_repl(v), REPL_ROWS, "adamw_replicated")
    repl = [_unpack_repl(o, shapes) for o in routs]

    result = [lax.psum(loss, ("x", "y", "c")), grad_x]
    for k in range(4):
        for name in w:
            result.append(new[name][k].reshape(shapes[name]) if name in new else repl[k][name])
    return tuple(result)
```

```python
import functools

import jax
import jax.numpy as jnp
from jax import lax
from jax.experimental import pallas as pl
from jax.experimental.pallas import tpu as pltpu

F32 = jnp.float32
BF16 = jnp.bfloat16

N_DEV = 8
LANES = 128
D = 1024
N_META = 16
SEQ = 2048
L_REAL = N_META + SEQ
LP = 2304
BQ = 256
NBLK = LP // BQ
HEAD = 64
NH = 8
W_ATT = NH * HEAD
PAIR_W = 3 * LANES
D_FF = 2816
IN_COLS = 5128
QKV = 6 * W_ATT
IN_P = 5376
GATE_COL = QKV
F_COL = QKV + 2 * D
FFC = 256
RMS_EPS = 1e-6
LR, B1, B2, EPS, WD, STEP = 0.001, 0.9, 0.999, 1e-08, 0.01, 10
VMEM_LIMIT = 56 * 1024 * 1024

MESH = pl.DeviceIdType.MESH
ANY = pl.BlockSpec(memory_space=pl.ANY)


def _cparams(*sem):
    return pltpu.CompilerParams(dimension_semantics=sem if sem else None, vmem_limit_bytes=VMEM_LIMIT)


def _all_gather(xs, name):
    n = len(xs)

    def body(*refs):
        x_refs, out_refs = refs[:n], refs[n:2 * n]
        send_sems, recv_sems, local_sems = refs[2 * n:]
        mx, my, mc = lax.axis_index("x"), lax.axis_index("y"), lax.axis_index("c")
        me, sibling = (mx, my, mc), (mx, my, 1 - mc)
        chips = [(1 - mx, my), (mx, 1 - my), (1 - mx, 1 - my)]

        def copy(a, k, block, to, own=False):
            px, py, pc = block
            slot = out_refs[a].at[4 * px + 2 * py + pc]
            return pltpu.make_async_remote_copy(
                src_ref=x_refs[a] if own else slot, dst_ref=slot,
                send_sem=send_sems.at[7 * a + k], recv_sem=recv_sems.at[7 * a + k],
                device_id=to, device_id_type=MESH)

        mine = [pltpu.make_async_copy(x_refs[a], out_refs[a].at[4 * mx + 2 * my + mc], local_sems.at[a]) for a in range(n)]
        for cp in mine:
            cp.start()
        first = []
        for a in range(n):
            first.append(copy(a, 0, me, sibling, own=True))
            first += [copy(a, 1 + j, me, (*chip, mc), own=True) for j, chip in enumerate(chips)]
        for cp in first:
            cp.start()
        passed = []
        for j, chip in enumerate(chips):
            for a in range(n):
                copy(a, 1 + j, (*chip, mc), me).wait_recv()
                fwd = copy(a, 4 + j, (*chip, mc), sibling)
                fwd.start()
                passed.append(fwd)
        for a in range(n):
            copy(a, 0, sibling, me).wait_recv()
            for j, chip in enumerate(chips):
                copy(a, 4 + j, (*chip, 1 - mc), me).wait_recv()
        for cp in first + passed:
            cp.wait_send()
        for cp in mine:
            cp.wait()

    return pl.pallas_call(
        body, name=name,
        out_shape=tuple(jax.ShapeDtypeStruct((N_DEV,) + x.shape, x.dtype) for x in xs),
        in_specs=[ANY] * n, out_specs=tuple([ANY] * n),
        scratch_shapes=[pltpu.SemaphoreType.DMA((7 * n,)), pltpu.SemaphoreType.DMA((7 * n,)),
                        pltpu.SemaphoreType.DMA((n,))],
    )(*xs)


def _exchange(srcs, name):
    n = len(srcs)

    def body(*refs):
        src_refs, dst_refs = refs[:n], refs[n:2 * n]
        send_sems, recv_sems, local_sems = refs[2 * n:]
        mx, my, mc = lax.axis_index("x"), lax.axis_index("y"), lax.axis_index("c")
        me_idx = 4 * mx + 2 * my + mc
        mine = [pltpu.make_async_copy(src_refs[a].at[me_idx], dst_refs[a].at[me_idx], local_sems.at[a]) for a in range(n)]
        for cp in mine:
            cp.start()
        copies = []
        for k in range(1, N_DEV):
            px, py, pc = mx ^ (k >> 2), my ^ ((k >> 1) & 1), mc ^ (k & 1)
            for a in range(n):
                copies.append(pltpu.make_async_remote_copy(
                    src_ref=src_refs[a].at[4 * px + 2 * py + pc], dst_ref=dst_refs[a].at[me_idx],
                    send_sem=send_sems.at[7 * a + k - 1], recv_sem=recv_sems.at[7 * a + k - 1],
                    device_id=(px, py, pc), device_id_type=MESH))
        for cp in copies:
            cp.start()
        for cp in copies:
            cp.wait_recv()
        for cp in copies:
            cp.wait_send()
        for cp in mine:
            cp.wait()

    return pl.pallas_call(
        body, name=name,
        out_shape=tuple(jax.ShapeDtypeStruct(s.shape, s.dtype) for s in srcs),
        in_specs=[ANY] * n, out_specs=tuple([ANY] * n),
        scratch_shapes=[pltpu.SemaphoreType.DMA((7 * n,)), pltpu.SemaphoreType.DMA((7 * n,)),
                        pltpu.SemaphoreType.DMA((n,))],
    )(*srcs)


ROWS_PER_COPY = 256


def _pad_rows(front, body_rows, nseq, name):
    tail = LP - L_REAL
    nblk = SEQ // ROWS_PER_COPY

    def body(f_ref, b_ref, o_ref, z_ref, sems):
        s, i = pl.program_id(0), pl.program_id(1)
        rows = pltpu.make_async_copy(b_ref, o_ref.at[pl.ds(s, 1), pl.ds(N_META + i * ROWS_PER_COPY, ROWS_PER_COPY)], sems.at[0])
        rows.start()

        @pl.when(i == 0)
        def _():
            z_ref[...] = jnp.zeros_like(z_ref)
            head = pltpu.make_async_copy(f_ref, o_ref.at[s, pl.ds(0, N_META)], sems.at[1])
            zeros = pltpu.make_async_copy(z_ref, o_ref.at[s, pl.ds(L_REAL, tail)], sems.at[2])
            head.start()
            zeros.start()
            head.wait()
            zeros.wait()

        rows.wait()

    return pl.pallas_call(
        body, name=name, out_shape=jax.ShapeDtypeStruct((nseq, LP, D), F32), grid=(nseq, nblk),
        in_specs=[pl.BlockSpec((N_META, D), lambda s, i: (0, 0)), pl.BlockSpec((1, ROWS_PER_COPY, D), lambda s, i: (s, i, 0))],
        out_specs=ANY,
        scratch_shapes=[pltpu.VMEM((tail, D), F32), pltpu.SemaphoreType.DMA((3,))],
        compiler_params=_cparams("arbitrary", "arbitrary"))(front, body_rows)


def _real_rows(h, nseq, name):
    nblk = SEQ // ROWS_PER_COPY

    def body(h_ref, o_ref, sem):
        s, i = pl.program_id(0), pl.program_id(1)
        rows = pltpu.make_async_copy(h_ref.at[pl.ds(s, 1), pl.ds(N_META + i * ROWS_PER_COPY, ROWS_PER_COPY)], o_ref, sem)
        rows.start()
        rows.wait()

    return pl.pallas_call(
        body, name=name, out_shape=jax.ShapeDtypeStruct((nseq, SEQ, D), F32), grid=(nseq, nblk),
        in_specs=[ANY], out_specs=pl.BlockSpec((1, ROWS_PER_COPY, D), lambda s, i: (s, i, 0)),
        scratch_shapes=[pltpu.SemaphoreType.DMA],
        compiler_params=_cparams("arbitrary", "arbitrary"))(h)


def _plan_cols(n_q, n_dcols, src_of):
    plan = {}
    for q in range(n_q):
        for dblk in range(n_dcols // LANES):
            segs, key, start = [], None, 0
            for lane in range(LANES + 1):
                new = None
                if lane < LANES:
                    src = src_of(q, dblk * LANES + lane)
                    if src is not None:
                        new = (src[0], src[1] // LANES, (lane - src[1] % LANES) % LANES)
                if new != key:
                    if key is not None:
                        segs.append((*key, start, lane))
                    key, start = new, lane
            plan[(q, dblk)] = segs
    return plan


def _relayout(src, n_q, n_dcols, src_of, out_dtype, tr, name):
    n_p, rows, scols = src.shape
    plan = _plan_cols(n_q, n_dcols, src_of)

    def body(s_ref, d_ref):
        lane = lax.broadcasted_iota(jnp.int32, (tr, LANES), 1)
        for (q, dblk), segs in plan.items():
            acc = jnp.zeros((tr, LANES), F32)
            for p, sblk, rot, lo, hi in segs:
                x = s_ref[p, :, sblk * LANES:(sblk + 1) * LANES].astype(F32)
                if rot:
                    x = pltpu.roll(x, rot, 1)
                acc = x if (lo, hi) == (0, LANES) else jnp.where((lane >= lo) & (lane < hi), x, acc)
            d_ref[q, :, dblk * LANES:(dblk + 1) * LANES] = acc.astype(out_dtype)

    return pl.pallas_call(
        body, name=name, out_shape=jax.ShapeDtypeStruct((n_q, rows, n_dcols), out_dtype), grid=(rows // tr,),
        in_specs=[pl.BlockSpec((n_p, tr, scols), lambda i: (0, i, 0))],
        out_specs=pl.BlockSpec((n_q, tr, n_dcols), lambda i: (0, i, 0)),
        compiler_params=_cparams("parallel"))(src)


def _in_padded_to_orig(d):
    if d < QKV:
        kind, r = divmod(d, 4 * PAIR_W)
        pair, r = divmod(r, PAIR_W)
        part, r = divmod(r, LANES)
        return kind * 3 * W_ATT + part * W_ATT + pair * LANES + r
    if d < F_COL:
        return d + NH
    if d < F_COL + NH:
        return d - 2 * D
    return None


_IN_ORIG_TO_PADDED = {_in_padded_to_orig(d): d for d in range(IN_P) if _in_padded_to_orig(d) is not None}


def _up_inter_to_orig(d):
    j, r = divmod(d, 2 * FFC)
    part, r = divmod(r, FFC)
    return part * D_FF + j * FFC + r


_UP_ORIG_TO_INTER = {_up_inter_to_orig(d): d for d in range(2 * D_FF)}
IN_SHARD = IN_COLS // N_DEV
UP_SHARD = 2 * D_FF // N_DEV
SHARD_P = 768
ATT_SHARD = D // N_DEV


def _gathered_to_full(n_shard, to_orig):
    def src_of(q, d):
        c = to_orig(d)
        return None if c is None else (c // n_shard, c % n_shard)
    return src_of


def _full_to_shards(n_shard, from_orig):
    def src_of(q, d):
        return (0, from_orig(q * n_shard + d)) if d < n_shard else None
    return src_of


def _matmul(a, b, *, out_dtype, tm, tn, tk, ta=False, tb=False, name):
    if ta:
        kdim, m = a.shape
    else:
        m, kdim = a.shape
    n = b.shape[0] if tb else b.shape[1]
    assert m % tm == 0 and n % tn == 0 and kdim % tk == 0, (name, a.shape, b.shape, tm, tn, tk)
    nk = kdim // tk

    def body(a_ref, b_ref, o_ref, *scratch):
        av, bv = a_ref[...], b_ref[...]
        if ta:
            p = lax.dot_general(av, bv, (((0,), (0,)), ((), ())), preferred_element_type=F32)
        elif tb:
            p = lax.dot_general(av, bv, (((1,), (1,)), ((), ())), preferred_element_type=F32)
        else:
            p = jnp.dot(av, bv, preferred_element_type=F32)
        if nk == 1:
            o_ref[...] = p.astype(o_ref.dtype)
        else:
            acc_ref, = scratch
            k = pl.program_id(2)

            @pl.when(k == 0)
            def _():
                acc_ref[...] = p

            @pl.when(k > 0)
            def _():
                acc_ref[...] += p

            @pl.when(k == nk - 1)
            def _():
                o_ref[...] = acc_ref[...].astype(o_ref.dtype)

    a_spec = pl.BlockSpec((tk, tm), lambda i, j, k: (k, i)) if ta else pl.BlockSpec((tm, tk), lambda i, j, k: (i, k))
    b_spec = pl.BlockSpec((tn, tk), lambda i, j, k: (j, k)) if tb else pl.BlockSpec((tk, tn), lambda i, j, k: (k, j))
    return pl.pallas_call(
        body, name=name,
        out_shape=jax.ShapeDtypeStruct((m, n), out_dtype),
        grid=(m // tm, n // tn, nk),
        in_specs=[a_spec, b_spec],
        out_specs=pl.BlockSpec((tm, tn), lambda i, j, k: (i, j)),
        scratch_shapes=[] if nk == 1 else [pltpu.VMEM((tm, tn), F32)],
        compiler_params=_cparams("parallel", "parallel", "arbitrary"),
    )(a, b)


TR = 288


def _rms(h):
    return lax.rsqrt(jnp.mean(h * h, axis=-1, keepdims=True) + RMS_EPS)


def _norm_fwd(h, delta, g, name):
    t = h.shape[0]
    row = pl.BlockSpec((TR, D), lambda i: (i, 0))
    vec = pl.BlockSpec((1, D), lambda i: (0, 0))

    if delta is None:
        def body(h_ref, g_ref, n_ref):
            hv = h_ref[...]
            n_ref[...] = ((hv * _rms(hv)) * g_ref[...]).astype(BF16)

        n = pl.pallas_call(
            body, name=name, out_shape=jax.ShapeDtypeStruct((t, D), BF16), grid=(t // TR,),
            in_specs=[row, vec], out_specs=row, compiler_params=_cparams("parallel"))(h, g)
        return h, n

    def body(h_ref, d_ref, g_ref, hn_ref, n_ref):
        hv = h_ref[...] + d_ref[...]
        hn_ref[...] = hv
        n_ref[...] = ((hv * _rms(hv)) * g_ref[...]).astype(BF16)

    return pl.pallas_call(
        body, name=name,
        out_shape=(jax.ShapeDtypeStruct((t, D), F32), jax.ShapeDtypeStruct((t, D), BF16)), grid=(t // TR,),
        in_specs=[row, row, vec], out_specs=(row, row), compiler_params=_cparams("parallel"))(h, delta, g)


def _rms_bwd_math(hv, dn, gv):
    r = _rms(hv)
    hr = hv * r
    dng = dn * gv
    dh = r * (dng - hr * jnp.mean(dng * hr, axis=-1, keepdims=True))
    return dh, dn * hr


def _final_loss_bwd(h1, delta, g, tgt, name):
    t = h1.shape[0]
    row = pl.BlockSpec((TR, D), lambda i: (i, 0))
    vec = pl.BlockSpec((1, D), lambda i: (0, 0))
    tiles_per_seq = LP // TR

    def body(h_ref, d_ref, g_ref, t_ref, loss_ref, dh_ref, dhb_ref, dg_ref):
        i = pl.program_id(0)
        hv = h_ref[...] + d_ref[...]
        gv = g_ref[...]
        r = _rms(hv)
        hr = hv * r
        y = hr * gv
        pos = (i % tiles_per_seq) * TR + lax.broadcasted_iota(jnp.int32, (TR, 1), 0)
        valid = (pos >= N_META) & (pos < L_REAL)
        err = jnp.where(valid, y - t_ref[...], 0.0)
        part = 0.5 * jnp.sum(jnp.mean(err * err, axis=-1, keepdims=True))
        dy = err * (1.0 / D)
        dng = dy * gv
        dh = r * (dng - hr * jnp.mean(dng * hr, axis=-1, keepdims=True))
        dh_ref[...] = dh
        dhb_ref[...] = dh.astype(BF16)
        dgp = jnp.sum(dy * hr, axis=0, keepdims=True)

        @pl.when(i == 0)
        def _():
            loss_ref[...] = jnp.zeros_like(loss_ref)
            dg_ref[...] = jnp.zeros_like(dg_ref)

        loss_ref[...] += part
        dg_ref[...] += dgp

    return pl.pallas_call(
        body, name=name,
        out_shape=(jax.ShapeDtypeStruct((8, 128), F32), jax.ShapeDtypeStruct((t, D), F32),
                   jax.ShapeDtypeStruct((t, D), BF16), jax.ShapeDtypeStruct((1, D), F32)),
        grid=(t // TR,),
        in_specs=[row, row, vec, row],
        out_specs=(pl.BlockSpec((8, 128), lambda i: (0, 0)), row, row, vec),
        compiler_params=_cparams("arbitrary"))(h1, delta, g, tgt)


def _norm_bwd(h, dn, g, dres, with_bf16, name):
    t = h.shape[0]
    row = pl.BlockSpec((TR, D), lambda i: (i, 0))
    vec = pl.BlockSpec((1, D), lambda i: (0, 0))

    def body(h_ref, dn_ref, g_ref, dres_ref, *outs):
        i = pl.program_id(0)
        dh, dgrow = _rms_bwd_math(h_ref[...], dn_ref[...], g_ref[...])
        dh = dh + dres_ref[...]
        outs[0][...] = dh
        if with_bf16:
            outs[1][...] = dh.astype(BF16)
        dg_ref = outs[-1]

        @pl.when(i == 0)
        def _():
            dg_ref[...] = jnp.zeros_like(dg_ref)

        dg_ref[...] += jnp.sum(dgrow, axis=0, keepdims=True)

    shapes = [jax.ShapeDtypeStruct((t, D), F32)]
    specs = [row]
    if with_bf16:
        shapes.append(jax.ShapeDtypeStruct((t, D), BF16))
        specs.append(row)
    shapes.append(jax.ShapeDtypeStruct((1, D), F32))
    specs.append(vec)
    return pl.pallas_call(
        body, name=name, out_shape=tuple(shapes), grid=(t // TR,),
        in_specs=[row, row, vec, row], out_specs=tuple(specs),
        compiler_params=_cparams("arbitrary"))(h, dn, g, dres)


GATE_BLK = GATE_COL // D


def _sigmoid(x):
    return 1.0 / (1.0 + jnp.exp(-x))


def _merge_fwd(p_sb, p_fx, proj, name):
    t = p_sb.shape[0]
    row = pl.BlockSpec((TR, D), lambda i: (i, 0))

    def body(ps_ref, pf_ref, gs_ref, gf_ref, o_ref):
        o_ref[...] = (_sigmoid(gs_ref[...]) * ps_ref[...] + _sigmoid(gf_ref[...]) * pf_ref[...]).astype(BF16)

    return pl.pallas_call(
        body, name=name, out_shape=jax.ShapeDtypeStruct((t, D), BF16), grid=(t // TR,),
        in_specs=[row, row, pl.BlockSpec((TR, D), lambda i: (i, GATE_BLK)),
                  pl.BlockSpec((TR, D), lambda i: (i, GATE_BLK + 1))],
        out_specs=row, compiler_params=_cparams("parallel"))(p_sb, p_fx, proj, proj)


def _merge_bwd(dm, p, proj, dproj, which, name):
    t = dm.shape[0]
    row = pl.BlockSpec((TR, D), lambda i: (i, 0))
    gate = pl.BlockSpec((TR, D), lambda i: (i, GATE_BLK + which))

    def body(dm_ref, p_ref, g_ref, *rest):
        dp_ref, dg_ref = rest[-2:]
        dmv = dm_ref[...]
        s = _sigmoid(g_ref[...])
        dp_ref[...] = (dmv * s).astype(BF16)
        dg_ref[...] = (dmv * p_ref[...] * s * (1.0 - s)).astype(BF16)

    out_shape = (jax.ShapeDtypeStruct((t, D), BF16), jax.ShapeDtypeStruct((t, IN_P), BF16))
    if dproj is None:
        return pl.pallas_call(
            body, name=name, out_shape=out_shape, grid=(t // TR,), in_specs=[row, row, gate],
            out_specs=(row, gate), compiler_params=_cparams("parallel"))(dm, p, proj)
    return pl.pallas_call(
        body, name=name, out_shape=out_shape, grid=(t // TR,), in_specs=[row, row, gate, ANY],
        out_specs=(row, gate), input_output_aliases={3: 1}, compiler_params=_cparams("parallel"))(dm, p, proj, dproj)


CH = 288


def _chunk(c, n=CH):
    return pl.ds(pl.multiple_of(c * CH, 8), n)


def _conv_taps(u_ref, c):
    x = u_ref[_chunk(c), :]
    prev = u_ref[pl.ds(pl.multiple_of(jnp.maximum(c * CH - 8, 0), 8), 8), :]
    xx = jnp.concatenate([jnp.where(c == 0, 0.0, prev), x], axis=0)
    return x, pltpu.roll(xx, 1, 0)[8:], pltpu.roll(xx, 2, 0)[8:]


def _conv_glu_fwd(u, cw, nseq, name):
    nblk = D_FF // FFC

    def body(u_ref, cw_ref, o_ref):
        cwv = cw_ref[...]

        def step(c, _):
            x, x1, x2 = _conv_taps(u_ref, c)
            uc = cwv[0:1, :] * x2 + cwv[1:2, :] * x1 + cwv[2:3, :] * x
            a, b = uc[:, :FFC], uc[:, FFC:]
            o_ref[_chunk(c), :] = (a * _sigmoid(a) * b).astype(BF16)
            return 0

        lax.fori_loop(0, LP // CH, step, 0)

    return pl.pallas_call(
        body, name=name, out_shape=jax.ShapeDtypeStruct((nseq * LP, D_FF), BF16), grid=(nseq, nblk),
        in_specs=[pl.BlockSpec((LP, 2 * FFC), lambda s, j: (s, j)), pl.BlockSpec((3, 2 * FFC), lambda s, j: (0, j))],
        out_specs=pl.BlockSpec((LP, FFC), lambda s, j: (s, j)),
        compiler_params=_cparams("parallel", "parallel"))(u, cw)


def _conv_glu_bwd(u, cw, dact, nseq, name):
    nblk = D_FF // FFC
    nch = LP // CH

    def body(u_ref, cw_ref, da_ref, du_ref, dcw_ref):
        s = pl.program_id(1)
        cwv = cw_ref[...]

        def step(k, carry):
            nxt, p0, p1, p2 = carry
            c = nch - 1 - k
            x, x1, x2 = _conv_taps(u_ref, c)
            uc = cwv[0:1, :] * x2 + cwv[1:2, :] * x1 + cwv[2:3, :] * x
            a, b = uc[:, :FFC], uc[:, FFC:]
            sa = _sigmoid(a)
            dactv = da_ref[_chunk(c), :]
            da = dactv * b * (sa * (1.0 + a * (1.0 - sa)))
            db = dactv * (a * sa)
            duc = jnp.concatenate([da, db], axis=1)
            dd = jnp.concatenate([duc, nxt], axis=0)
            du = (cwv[2:3, :] * duc + cwv[1:2, :] * pltpu.roll(dd, CH + 7, 0)[:CH]
                  + cwv[0:1, :] * pltpu.roll(dd, CH + 6, 0)[:CH])
            du_ref[_chunk(c), :] = du.astype(BF16)
            return (duc[:8], p0 + jnp.sum(duc * x2, axis=0, keepdims=True),
                    p1 + jnp.sum(duc * x1, axis=0, keepdims=True), p2 + jnp.sum(duc * x, axis=0, keepdims=True))

        zrow = jnp.zeros((1, 2 * FFC), F32)
        _, p0, p1, p2 = lax.fori_loop(0, nch, step, (jnp.zeros((8, 2 * FFC), F32), zrow, zrow, zrow))

        @pl.when(s == 0)
        def _():
            dcw_ref[...] = jnp.zeros_like(dcw_ref)

        dcw_ref[...] += jnp.concatenate([p0, p1, p2], axis=0)

    return pl.pallas_call(
        body, name=name,
        out_shape=(jax.ShapeDtypeStruct((nseq * LP, 2 * D_FF), BF16), jax.ShapeDtypeStruct((3, 2 * D_FF), F32)),
        grid=(nblk, nseq),
        in_specs=[pl.BlockSpec((LP, 2 * FFC), lambda j, s: (s, j)), pl.BlockSpec((3, 2 * FFC), lambda j, s: (0, j)),
                  pl.BlockSpec((LP, FFC), lambda j, s: (s, j))],
        out_specs=(pl.BlockSpec((LP, 2 * FFC), lambda j, s: (s, j)), pl.BlockSpec((3, 2 * FFC), lambda j, s: (0, j))),
        compiler_params=_cparams("parallel", "arbitrary"))(u, cw, dact)


F_BLK = F_COL // LANES
CB = 128


def _split3(x):
    hi = x.astype(BF16)
    r1 = x - hi.astype(F32)
    mid = r1.astype(BF16)
    lo = (r1 - mid.astype(F32)).astype(BF16)
    return hi, mid, lo


def _tri_dot(tri, x):
    hi, mid, lo = _split3(x)
    d = functools.partial(jnp.dot, preferred_element_type=F32)
    return d(tri, hi) + d(tri, mid) + d(tri, lo)


def _log_sigmoid(x):
    return jnp.minimum(x, 0.0) - jnp.log(1.0 + jnp.exp(-jnp.abs(x)))


def _gate_fwd(proj, bf, nseq, name):
    def body(f_ref, b_ref, c_ref):
        r_i = lax.broadcasted_iota(jnp.int32, (CB, CB), 0)
        c_i = lax.broadcasted_iota(jnp.int32, (CB, CB), 1)
        tri = (c_i <= r_i).astype(BF16)
        bv = b_ref[...]

        def step(k, carry):
            rows = pl.ds(pl.multiple_of(k * CB, CB), CB)
            lf = _log_sigmoid(f_ref[rows, :] + bv)
            c_ref[rows, :] = _tri_dot(tri, lf) + carry
            return carry + jnp.sum(lf, axis=0, keepdims=True)

        lax.fori_loop(0, LP // CB, step, jnp.zeros((1, LANES), F32))

    return pl.pallas_call(
        body, name=name, out_shape=jax.ShapeDtypeStruct((nseq * LP, LANES), F32), grid=(nseq,),
        in_specs=[pl.BlockSpec((LP, LANES), lambda s: (s, F_BLK)), pl.BlockSpec((1, LANES), lambda s: (0, 0))],
        out_specs=pl.BlockSpec((LP, LANES), lambda s: (s, 0)),
        compiler_params=_cparams("parallel"))(proj, bf)


def _gate_bwd(proj, bf, dc, dproj, nseq, name):
    def body(f_ref, b_ref, dc_ref, _, df_ref, db_ref):
        s = pl.program_id(0)
        r_i = lax.broadcasted_iota(jnp.int32, (CB, CB), 0)
        c_i = lax.broadcasted_iota(jnp.int32, (CB, CB), 1)
        tri = (c_i >= r_i).astype(BF16)
        bv = b_ref[...]

        def step(kk, carry):
            carry_c, carry_b = carry
            k = LP // CB - 1 - kk
            rows = pl.ds(pl.multiple_of(k * CB, CB), CB)
            dcv = dc_ref[rows, :]
            dlf = _tri_dot(tri, dcv) + carry_c
            df = dlf * _sigmoid(-(f_ref[rows, :] + bv))
            df_ref[rows, :] = jnp.concatenate([df, jnp.zeros_like(df)], axis=1).astype(BF16)
            return carry_c + jnp.sum(dcv, axis=0, keepdims=True), carry_b + jnp.sum(df, axis=0, keepdims=True)

        zero = jnp.zeros((1, LANES), F32)
        _, dbp = lax.fori_loop(0, LP // CB, step, (zero, zero))

        @pl.when(s == 0)
        def _():
            db_ref[...] = jnp.zeros_like(db_ref)

        db_ref[...] += dbp

    return pl.pallas_call(
        body, name=name,
        out_shape=(jax.ShapeDtypeStruct(dproj.shape, BF16), jax.ShapeDtypeStruct((1, LANES), F32)), grid=(nseq,),
        in_specs=[pl.BlockSpec((LP, LANES), lambda s: (s, F_BLK)), pl.BlockSpec((1, LANES), lambda s: (0, 0)),
                  pl.BlockSpec((LP, LANES), lambda s: (s, 0)), ANY],
        out_specs=(pl.BlockSpec((LP, 2 * LANES), lambda s: (s, F_COL // (2 * LANES))), pl.BlockSpec((1, LANES), lambda s: (0, 0))),
        input_output_aliases={3: 0},
        compiler_params=_cparams("arbitrary"))(proj, bf, dc, dproj)


SCALE = 0.125
NEG = -1e30


def _dot_nt(a, b):
    return lax.dot_general(a, b, (((1,), (1,)), ((), ())), preferred_element_type=F32)


def _dot_tn(a, b):
    return lax.dot_general(a, b, (((0,), (0,)), ((), ())), preferred_element_type=F32)


def _dot(a, b):
    return jnp.dot(a, b, preferred_element_type=F32)


def _blk(i):
    return pl.ds(pl.multiple_of(i * BQ, BQ), BQ)


def _tile_iotas():
    return lax.broadcasted_iota(jnp.int32, (BQ, BQ), 0), lax.broadcasted_iota(jnp.int32, (BQ, BQ), 1)


def _lane_iota():
    return lax.broadcasted_iota(jnp.int32, (BQ, LANES), 1)


def _head_lanes(hh):
    lane = _lane_iota()
    return (lane >= hh * HEAD) & (lane < (hh + 1) * HEAD)


def _only(mask, x):
    return jnp.where(mask, x, jnp.zeros_like(x))


def _pick_lane(x, idx):
    return jnp.sum(jnp.where(_lane_iota() == idx, x, 0.0), axis=1, keepdims=True)


def _load_qkv(p_ref, q_s, k_s, v_s):
    q_s[...] = p_ref[:, 0:LANES].astype(BF16)
    k_s[...] = p_ref[:, LANES:2 * LANES].astype(BF16)
    v_s[...] = p_ref[:, 2 * LANES:3 * LANES].astype(BF16)


def _sb_weights(qi, kj, strict, r_after, u_suf):
    z = _dot_nt(qi, kj) * SCALE
    sp = jnp.maximum(z, 0.0) + jnp.log(1.0 + jnp.exp(-jnp.abs(z)))
    lk = jnp.where(strict, -sp, 0.0)
    hi = lk.astype(BF16)
    lo = (lk - hi.astype(F32)).astype(BF16)
    suf = _dot(hi, u_suf) + _dot(lo, u_suf)
    w = jnp.where(strict, jnp.exp(z - sp + r_after + suf), 0.0)
    return w, sp, lk


def _pair_spec(kind):
    return pl.BlockSpec((LP, PAIR_W), lambda s, p: (s, 4 * kind + p))


HEADS_SPEC = pl.BlockSpec((LP, LANES), lambda s, p: (s, p))
SEQ_SPEC = pl.BlockSpec((LP, LANES), lambda s, p: (s, 0))
QKV_SCRATCH = [pltpu.VMEM((LP, LANES), BF16)] * 3
RS_STRIDE = 16


def _sb_fwd(proj, nseq, name):
    t = nseq * LP

    def body(p_ref, o_ref, rs_ref, q_s, k_s, v_s, acc_ref, r_ref, rb_ref):
        _load_qkv(p_ref, q_s, k_s, v_s)
        row, col = _tile_iotas()
        u_suf = (row > col).astype(BF16)
        lane = _lane_iota()

        def qblock(i, _):
            acc_ref[...] = jnp.zeros_like(acc_ref)
            rb_ref[...] = jnp.zeros_like(rb_ref)

            def head(hh, _):
                mine = _head_lanes(hh)
                qi = _only(mine, q_s[_blk(i), :])
                r_ref[...] = jnp.zeros_like(r_ref)

                def kblock(jj, _):
                    j = i - jj
                    strict = (col + j * BQ) < (row + i * BQ)
                    r_after = r_ref[...]
                    w, _, lk = _sb_weights(qi, k_s[_blk(j), :], strict, r_after, u_suf)
                    acc_ref[...] += _dot(w.astype(BF16), _only(mine, v_s[_blk(j), :]))
                    rb_ref[...] = jnp.where(lane == RS_STRIDE * hh + j, r_after, rb_ref[...])
                    r_ref[...] = r_after + jnp.sum(lk, axis=1, keepdims=True)
                    return 0

                lax.fori_loop(0, i + 1, kblock, 0)
                return 0

            lax.fori_loop(0, 2, head, 0)
            o_ref[_blk(i), :] = acc_ref[...].astype(BF16)
            rs_ref[_blk(i), :] = rb_ref[...]
            return 0

        lax.fori_loop(0, NBLK, qblock, 0)

    return pl.pallas_call(
        body, name=name,
        out_shape=(jax.ShapeDtypeStruct((t, W_ATT), BF16), jax.ShapeDtypeStruct((t, W_ATT), F32)),
        grid=(nseq, NH // 2), in_specs=[_pair_spec(0)], out_specs=(HEADS_SPEC, HEADS_SPEC),
        scratch_shapes=QKV_SCRATCH + [pltpu.VMEM((BQ, LANES), F32), pltpu.VMEM((BQ, 1), F32), pltpu.VMEM((BQ, LANES), F32)],
        compiler_params=_cparams("parallel", "parallel"))(proj)


def _sb_bwd(proj, do, rs, dproj, nseq, name):
    def body(p_ref, do_ref, rs_ref, _, dp_ref, q_s, k_s, v_s, dqa_ref, dka_ref, dva_ref, ep_ref):
        _load_qkv(p_ref, q_s, k_s, v_s)
        row, col = _tile_iotas()
        u_suf = (row > col).astype(BF16)
        u_pre = (row < col).astype(BF16)
        dka_ref[...] = jnp.zeros_like(dka_ref)
        dva_ref[...] = jnp.zeros_like(dva_ref)

        def qblock(i, _):
            rb = rs_ref[_blk(i), :]
            dqa_ref[...] = jnp.zeros_like(dqa_ref)

            def head(hh, _):
                mine = _head_lanes(hh)
                qi = _only(mine, q_s[_blk(i), :])
                doi = _only(mine, do_ref[_blk(i), :])
                ep_ref[...] = jnp.zeros_like(ep_ref)

                def kblock(j, _):
                    strict = (col + j * BQ) < (row + i * BQ)
                    r_after = _pick_lane(rb, RS_STRIDE * hh + j)
                    kj, vj = k_s[_blk(j), :], v_s[_blk(j), :]
                    w, sp, _ = _sb_weights(qi, kj, strict, r_after, u_suf)
                    e = _dot_nt(doi, vj) * w
                    e_pre = ep_ref[...] + _dot(e.astype(BF16), u_pre)
                    ep_ref[...] += jnp.sum(e, axis=1, keepdims=True)
                    sneg = jnp.exp(-sp)
                    dz = jnp.where(strict, e * sneg - (1.0 - sneg) * e_pre, 0.0).astype(BF16)
                    dqa_ref[...] += _dot(dz, _only(mine, kj))
                    dka_ref[_blk(j), :] += _dot_tn(dz, qi)
                    dva_ref[_blk(j), :] += _dot_tn(w.astype(BF16), doi)
                    return 0

                lax.fori_loop(0, i + 1, kblock, 0)
                return 0

            lax.fori_loop(0, 2, head, 0)
            dp_ref[_blk(i), 0:LANES] = (dqa_ref[...] * SCALE).astype(BF16)
            return 0

        lax.fori_loop(0, NBLK, qblock, 0)
        dp_ref[:, LANES:2 * LANES] = (dka_ref[...] * SCALE).astype(BF16)
        dp_ref[:, 2 * LANES:3 * LANES] = dva_ref[...].astype(BF16)

    return pl.pallas_call(
        body, name=name, out_shape=jax.ShapeDtypeStruct(dproj.shape, BF16), grid=(nseq, NH // 2),
        in_specs=[_pair_spec(0), HEADS_SPEC, HEADS_SPEC, ANY], out_specs=_pair_spec(0),
        input_output_aliases={3: 0},
        scratch_shapes=QKV_SCRATCH + [pltpu.VMEM((BQ, LANES), F32), pltpu.VMEM((LP, LANES), F32),
                                      pltpu.VMEM((LP, LANES), F32), pltpu.VMEM((BQ, 1), F32)],
        compiler_params=_cparams("parallel", "parallel"))(proj, do, rs, dproj)


CROW_SPEC = pl.BlockSpec((None, NH, LP), lambda s, p: (s, 0, 0))


def _fox_scores(qi, kj, cq, ck, causal):
    z = _dot_nt(qi, kj) * SCALE + (cq - ck)
    return jnp.where(causal, z, NEG)


def _key_cols(cr_ref, head, j):
    return cr_ref[pl.ds(head, 1), pl.ds(pl.multiple_of(j * BQ, BQ), BQ)]


def _fox_fwd(proj, c, crow, nseq, name):
    t = nseq * LP

    def body(p_ref, c_ref, cr_ref, o_ref, o32_ref, lse_ref, q_s, k_s, v_s, out_ref, acc_ref, m_ref, l_ref, lb_ref):
        _load_qkv(p_ref, q_s, k_s, v_s)
        row, col = _tile_iotas()
        lane = _lane_iota()
        pair = pl.program_id(1)

        def qblock(i, _):
            out_ref[...] = jnp.zeros_like(out_ref)
            lb_ref[...] = jnp.zeros_like(lb_ref)
            cblk = c_ref[_blk(i), :]

            def head(hh, _):
                mine = _head_lanes(hh)
                qi = _only(mine, q_s[_blk(i), :])
                cq = _pick_lane(cblk, 2 * pair + hh)
                acc_ref[...] = jnp.zeros_like(acc_ref)
                m_ref[...] = jnp.full_like(m_ref, NEG)
                l_ref[...] = jnp.zeros_like(l_ref)

                def kblock(j, _):
                    causal = (col + j * BQ) <= (row + i * BQ)
                    z = _fox_scores(qi, k_s[_blk(j), :], cq, _key_cols(cr_ref, 2 * pair + hh, j), causal)
                    m_old = m_ref[...]
                    m_new = jnp.maximum(m_old, jnp.max(z, axis=1, keepdims=True))
                    alpha = jnp.exp(m_old - m_new)
                    p = jnp.exp(z - m_new)
                    l_ref[...] = alpha * l_ref[...] + jnp.sum(p, axis=1, keepdims=True)
                    acc_ref[...] = alpha * acc_ref[...] + _dot(p.astype(BF16), _only(mine, v_s[_blk(j), :]))
                    m_ref[...] = m_new
                    return 0

                lax.fori_loop(0, i + 1, kblock, 0)
                out_ref[...] += acc_ref[...] / l_ref[...]
                lb_ref[...] = jnp.where(lane == hh, m_ref[...] + jnp.log(l_ref[...]), lb_ref[...])
                return 0

            lax.fori_loop(0, 2, head, 0)
            o_ref[_blk(i), :] = out_ref[...].astype(BF16)
            o32_ref[_blk(i), :] = out_ref[...]
            lse_ref[_blk(i), :] = lb_ref[...]
            return 0

        lax.fori_loop(0, NBLK, qblock, 0)

    return pl.pallas_call(
        body, name=name,
        out_shape=(jax.ShapeDtypeStruct((t, W_ATT), BF16), jax.ShapeDtypeStruct((t, W_ATT), F32),
                   jax.ShapeDtypeStruct((t, W_ATT), F32)),
        grid=(nseq, NH // 2), in_specs=[_pair_spec(1), SEQ_SPEC, CROW_SPEC], out_specs=(HEADS_SPEC, HEADS_SPEC, HEADS_SPEC),
        scratch_shapes=QKV_SCRATCH + [pltpu.VMEM((BQ, LANES), F32), pltpu.VMEM((BQ, LANES), F32), pltpu.VMEM((BQ, 1), F32),
                                      pltpu.VMEM((BQ, 1), F32), pltpu.VMEM((BQ, LANES), F32)],
        compiler_params=_cparams("parallel", "parallel"))(proj, c, crow)


def _fox_bwd(proj, c, crow, o32, lse, do, dproj, nseq, name):
    t = nseq * LP

    def body(p_ref, c_ref, cr_ref, o_ref, lse_ref, do_ref, _, dp_ref, dck_ref, dcq_ref,
             q_s, k_s, v_s, dqa_ref, dka_ref, dva_ref, rsum_ref):
        _load_qkv(p_ref, q_s, k_s, v_s)
        row, col = _tile_iotas()
        lane = _lane_iota()
        sub = lax.broadcasted_iota(jnp.int32, (NH, BQ), 0)
        pair = pl.program_id(1)
        dka_ref[...] = jnp.zeros_like(dka_ref)
        dva_ref[...] = jnp.zeros_like(dva_ref)

        @pl.when(pair == 0)
        def _():
            dck_ref[...] = jnp.zeros_like(dck_ref)
            dcq_ref[...] = jnp.zeros_like(dcq_ref)

        def qblock(i, _):
            dqa_ref[...] = jnp.zeros_like(dqa_ref)
            cblk = c_ref[_blk(i), :]
            lblk = lse_ref[_blk(i), :]

            def head(hh, _):
                mine = _head_lanes(hh)
                qi = _only(mine, q_s[_blk(i), :])
                doi = _only(mine, do_ref[_blk(i), :])
                cq = _pick_lane(cblk, 2 * pair + hh)
                lse_i = _pick_lane(lblk, hh)
                delta = jnp.sum(doi.astype(F32) * o_ref[_blk(i), :], axis=1, keepdims=True)
                rsum_ref[...] = jnp.zeros_like(rsum_ref)

                def kblock(j, _):
                    causal = (col + j * BQ) <= (row + i * BQ)
                    kj, vj = k_s[_blk(j), :], v_s[_blk(j), :]
                    z = _fox_scores(qi, kj, cq, _key_cols(cr_ref, 2 * pair + hh, j), causal)
                    p = jnp.exp(z - lse_i)
                    ds = p * (_dot_nt(doi, vj) - delta)
                    dsb = ds.astype(BF16)
                    dqa_ref[...] += _dot(dsb, _only(mine, kj))
                    dka_ref[_blk(j), :] += _dot_tn(dsb, qi)
                    dva_ref[_blk(j), :] += _dot_tn(p.astype(BF16), doi)
                    keys = pl.ds(pl.multiple_of(j * BQ, BQ), BQ)
                    dck_ref[:, keys] = dck_ref[:, keys] - jnp.where(sub == 2 * pair + hh, jnp.sum(ds, axis=0, keepdims=True), 0.0)
                    rsum_ref[...] += jnp.sum(ds, axis=1, keepdims=True)
                    return 0

                lax.fori_loop(0, i + 1, kblock, 0)
                dcq_ref[_blk(i), :] += jnp.where(lane == 2 * pair + hh, rsum_ref[...], 0.0)
                return 0

            lax.fori_loop(0, 2, head, 0)
            dp_ref[_blk(i), 0:LANES] = (dqa_ref[...] * SCALE).astype(BF16)
            return 0

        lax.fori_loop(0, NBLK, qblock, 0)
        dp_ref[:, LANES:2 * LANES] = (dka_ref[...] * SCALE).astype(BF16)
        dp_ref[:, 2 * LANES:3 * LANES] = dva_ref[...].astype(BF16)

    return pl.pallas_call(
        body, name=name,
        out_shape=(jax.ShapeDtypeStruct(dproj.shape, BF16), jax.ShapeDtypeStruct((nseq, NH, LP), F32),
                   jax.ShapeDtypeStruct((t, LANES), F32)),
        grid=(nseq, NH // 2),
        in_specs=[_pair_spec(1), SEQ_SPEC, CROW_SPEC, HEADS_SPEC, HEADS_SPEC, HEADS_SPEC, ANY],
        out_specs=(_pair_spec(1), CROW_SPEC, SEQ_SPEC),
        input_output_aliases={6: 0},
        scratch_shapes=QKV_SCRATCH + [pltpu.VMEM((BQ, LANES), F32), pltpu.VMEM((LP, LANES), F32),
                                      pltpu.VMEM((LP, LANES), F32), pltpu.VMEM((BQ, 1), F32)],
        compiler_params=_cparams("parallel", "arbitrary"))(proj, c, crow, o32, lse, do, dproj)


def _adamw_math(w, g, m, v):
    m = B1 * m + (1.0 - B1) * g
    v = B2 * v + (1.0 - B2) * (g * g)
    m_hat = m / (1.0 - B1 ** STEP)
    v_hat = v / (1.0 - B2 ** STEP)
    delta = -LR * (m_hat / (jnp.sqrt(v_hat) + EPS) + WD * w)
    return delta, m, v


def _sum_adamw(parts, w, m, v, tr, name):
    rows, cols = w.shape
    cp = parts.shape[2]
    assert rows % tr == 0 and parts.shape[1] == rows

    def body(p_ref, w_ref, m_ref, v_ref, g_ref, d_ref, nm_ref, nv_ref):
        gsum = p_ref[0].astype(F32)
        for s in range(1, N_DEV):
            gsum = gsum + p_ref[s].astype(F32)
        gsum = gsum[:, :cols]
        d, nm, nv = _adamw_math(w_ref[...], gsum, m_ref[...], v_ref[...])
        g_ref[...] = gsum
        d_ref[...] = d
        nm_ref[...] = nm
        nv_ref[...] = nv

    blk = pl.BlockSpec((tr, cols), lambda i: (i, 0))
    out = jax.ShapeDtypeStruct((rows, cols), F32)
    return pl.pallas_call(
        body, name=name, out_shape=(out, out, out, out), grid=(rows // tr,),
        in_specs=[pl.BlockSpec((N_DEV, tr, cp), lambda i: (0, i, 0)), blk, blk, blk],
        out_specs=(blk, blk, blk, blk), compiler_params=_cparams("parallel"))(parts, w, m, v)


def _local_step(x, tgt, meta, g_mix, w_in_p, b_forget, w_bsb, w_bfx, w_out, g_ffn, w_up_i, cw_i, w_down, g_final):
    nseq = x.shape[0]
    t = nseq * LP
    tm = LP // 2
    mm = functools.partial(_matmul, tm=tm)

    h0 = _pad_rows(meta, x, nseq, "pad_x").reshape(t, D)
    tgt_p = _pad_rows(jnp.zeros((N_META, D), F32), tgt, nseq, "pad_target").reshape(t, D)
    bf = jnp.pad(b_forget.reshape(1, NH), ((0, 0), (0, LANES - NH)))

    _, n1 = _norm_fwd(h0, None, g_mix, "norm1")
    proj = mm(n1, w_in_p, out_dtype=F32, tn=1792, tk=D, name="in_proj")
    c = _gate_fwd(proj, bf, nseq, "gate_fwd")
    crow = c[:, :NH].reshape(nseq, LP, NH).transpose(0, 2, 1)
    o_sb, rs = _sb_fwd(proj, nseq, "sb_fwd")
    o_fx, o_fx32, lse = _fox_fwd(proj, c, crow, nseq, "fox_fwd")
    p_sb = mm(o_sb, w_bsb, out_dtype=F32, tn=D, tk=W_ATT, name="branch_sb")
    p_fx = mm(o_fx, w_bfx, out_dtype=F32, tn=D, tk=W_ATT, name="branch_fox")
    merged = _merge_fwd(p_sb, p_fx, proj, "merge_fwd")
    mix = mm(merged, w_out, out_dtype=F32, tn=D, tk=D, name="out_proj")
    h1, n2 = _norm_fwd(h0, mix, g_ffn, "norm2")
    u = mm(n2, w_up_i, out_dtype=F32, tn=1408, tk=D, name="up_proj")
    act = _conv_glu_fwd(u, cw_i, nseq, "conv_glu_fwd")
    ffn = mm(act, w_down, out_dtype=F32, tn=D, tk=1408, name="down_proj")

    loss, dh2, dh2b, dg_final = _final_loss_bwd(h1, ffn, g_final, tgt_p, "final")
    d_down = _matmul(act, dh2b, out_dtype=BF16, tm=1408, tn=D, tk=tm, ta=True, name="d_w_down")
    dact = mm(dh2b, w_down, out_dtype=F32, tn=1408, tk=D, tb=True, name="d_act")
    du, d_cw = _conv_glu_bwd(u, cw_i, dact, nseq, "conv_glu_bwd")
    d_up = _matmul(n2, du, out_dtype=BF16, tm=D, tn=1408, tk=tm, ta=True, name="d_w_up")
    dn2 = mm(du, w_up_i, out_dtype=F32, tn=D, tk=1408, tb=True, name="d_n2")
    dh1, dh1b, dg_ffn = _norm_bwd(h1, dn2, g_ffn, dh2, True, "norm2_bwd")
    d_out = _matmul(merged, dh1b, out_dtype=BF16, tm=D, tn=D, tk=tm, ta=True, name="d_w_out")
    dmerged = mm(dh1b, w_out, out_dtype=F32, tn=D, tk=D, tb=True, name="d_merged")
    dp_sb, dproj = _merge_bwd(dmerged, p_sb, proj, None, 0, "merge_bwd_sb")
    dp_fx, dproj = _merge_bwd(dmerged, p_fx, proj, dproj, 1, "merge_bwd_fox")
    d_bsb = _matmul(o_sb, dp_sb, out_dtype=BF16, tm=W_ATT, tn=D, tk=tm, ta=True, name="d_w_branch_sb")
    d_bfx = _matmul(o_fx, dp_fx, out_dtype=BF16, tm=W_ATT, tn=D, tk=tm, ta=True, name="d_w_branch_fox")
    do_sb = mm(dp_sb, w_bsb, out_dtype=BF16, tn=W_ATT, tk=D, tb=True, name="d_o_sb")
    do_fx = mm(dp_fx, w_bfx, out_dtype=BF16, tn=W_ATT, tk=D, tb=True, name="d_o_fox")
    dproj = _sb_bwd(proj, do_sb, rs, dproj, nseq, "sb_bwd")
    dproj, dck, dcq = _fox_bwd(proj, c, crow, o_fx32, lse, do_fx, dproj, nseq, "fox_bwd")
    dc = dcq + jnp.pad(dck.transpose(0, 2, 1).reshape(t, NH), ((0, 0), (0, LANES - NH)))
    dproj, d_bf = _gate_bwd(proj, bf, dc, dproj, nseq, "gate_bwd")
    d_in = _matmul(n1, dproj, out_dtype=BF16, tm=D, tn=1792, tk=tm, ta=True, name="d_w_in")
    dn1 = mm(dproj, w_in_p, out_dtype=F32, tn=D, tk=1792, tb=True, name="d_n1")
    dh0, dg_mix = _norm_bwd(h0, dn1, g_mix, dh1, False, "norm1_bwd")
    dh0 = dh0.reshape(nseq, LP, D)
    grads = dict(meta_tokens=jnp.sum(dh0[:, :N_META], axis=0), norm_mix_g=dg_mix, w_in=d_in, b_forget=d_bf[:, :NH],
                 w_branch_sb=d_bsb, w_branch_fox=d_bfx, w_out=d_out, norm_ffn_g=dg_ffn, w_up=d_up, conv_w=d_cw,
                 w_down=d_down, norm_final_g=dg_final)
    return loss[0, 0], _real_rows(dh0, nseq, "grad_x"), grads


REPL = (("norm_mix_g", D), ("norm_ffn_g", D), ("norm_final_g", D), ("b_forget", LANES))
REPL_ROWS = 32


def _pack_repl(tree):
    rows = [jnp.pad(tree[name].reshape(-1), (0, n - tree[name].size)).reshape(-1, LANES) for name, n in REPL]
    packed = jnp.concatenate(rows, axis=0)
    return jnp.pad(packed, ((0, REPL_ROWS - packed.shape[0]), (0, 0)))


def _unpack_repl(packed, shapes):
    out, r = {}, 0
    for name, n in REPL:
        size = 1
        for s in shapes[name]:
            size *= s
        out[name] = packed[r:r + n // LANES].reshape(-1)[:size].reshape(shapes[name])
        r += n // LANES
    return out


def kernel(x, meta_tokens, norm_mix_g, w_in, b_forget, w_branch_sb, w_branch_fox, w_out, norm_ffn_g, w_up, conv_w, w_down, norm_final_g, loss_target, m_meta_tokens, m_norm_mix_g, m_w_in, m_b_forget, m_w_branch_sb, m_w_branch_fox, m_w_out, m_norm_ffn_g, m_w_up, m_conv_w, m_w_down, m_norm_final_g, v_meta_tokens, v_norm_mix_g, v_w_in, v_b_forget, v_w_branch_sb, v_w_branch_fox, v_w_out, v_norm_ffn_g, v_w_up, v_conv_w, v_w_down, v_norm_final_g):
    w = dict(meta_tokens=meta_tokens, norm_mix_g=norm_mix_g, w_in=w_in, b_forget=b_forget, w_branch_sb=w_branch_sb,
             w_branch_fox=w_branch_fox, w_out=w_out, norm_ffn_g=norm_ffn_g, w_up=w_up, conv_w=conv_w, w_down=w_down,
             norm_final_g=norm_final_g)
    m = dict(meta_tokens=m_meta_tokens, norm_mix_g=m_norm_mix_g, w_in=m_w_in, b_forget=m_b_forget,
             w_branch_sb=m_w_branch_sb, w_branch_fox=m_w_branch_fox, w_out=m_w_out, norm_ffn_g=m_norm_ffn_g,
             w_up=m_w_up, conv_w=m_conv_w, w_down=m_w_down, norm_final_g=m_norm_final_g)
    v = dict(meta_tokens=v_meta_tokens, norm_mix_g=v_norm_mix_g, w_in=v_w_in, b_forget=v_b_forget,
             w_branch_sb=v_w_branch_sb, w_branch_fox=v_w_branch_fox, w_out=v_w_out, norm_ffn_g=v_norm_ffn_g,
             w_up=v_w_up, conv_w=v_conv_w, w_down=v_w_down, norm_final_g=v_norm_final_g)
    shapes = {k: a.shape for k, a in w.items()}
    sharded = ("w_in", "w_branch_sb", "w_branch_fox", "w_out", "w_up", "w_down", "conv_w", "meta_tokens")
    mat = lambda tree, name: tree[name].reshape(tree[name].shape[-2:])

    def lane_pad(a, width):
        return jnp.pad(a, ((0, 0), (0, width - a.shape[1])))

    g_in, g_bsb, g_bfx, g_out, g_up, g_down, g_cw, g_meta = _all_gather(
        [lane_pad(mat(w, "w_in").astype(BF16), SHARD_P), mat(w, "w_branch_sb").astype(BF16),
         mat(w, "w_branch_fox").astype(BF16), mat(w, "w_out").astype(BF16), lane_pad(mat(w, "w_up").astype(BF16), SHARD_P),
         mat(w, "w_down").astype(BF16), mat(w, "conv_w"), mat(w, "meta_tokens")], "gather_weights")
    w_in_p = _relayout(g_in, 1, IN_P, _gathered_to_full(IN_SHARD, _in_padded_to_orig), BF16, 256, "w_in_cols")[0]
    w_up_i = _relayout(g_up, 1, 2 * D_FF, _gathered_to_full(UP_SHARD, _up_inter_to_orig), BF16, 256, "w_up_cols")[0]
    w_bsb = _relayout(g_bsb, 1, D, _gathered_to_full(ATT_SHARD, lambda d: d), BF16, 256, "w_bsb_cols")[0]
    w_bfx = _relayout(g_bfx, 1, D, _gathered_to_full(ATT_SHARD, lambda d: d), BF16, 256, "w_bfx_cols")[0]
    cw_full = g_cw.transpose(1, 0, 2).reshape(3, 2 * D_FF)
    cw_i = cw_full.reshape(3, 2, D_FF // FFC, FFC).transpose(0, 2, 1, 3).reshape(3, 2 * D_FF)
    meta_full = g_meta.transpose(1, 0, 2).reshape(N_META, D)

    loss, grad_x, grads = _local_step(
        x, loss_target, meta_full, norm_mix_g.reshape(1, D), w_in_p, b_forget, w_bsb, w_bfx, g_out.reshape(D, D),
        norm_ffn_g.reshape(1, D), w_up_i, cw_i, g_down.reshape(D_FF, D), norm_final_g.reshape(1, D))

    d_cw = grads["conv_w"].reshape(3, D_FF // FFC, 2, FFC).transpose(0, 2, 1, 3).reshape(3, 2 * D_FF)
    parts = _exchange(
        [_relayout(grads["w_in"][None], N_DEV, SHARD_P, _full_to_shards(IN_SHARD, _IN_ORIG_TO_PADDED.get), BF16, 256, "d_w_in_shards"),
         _relayout(grads["w_branch_sb"][None], N_DEV, ATT_SHARD, _full_to_shards(ATT_SHARD, lambda c: c), BF16, 256, "d_w_bsb_shards"),
         _relayout(grads["w_branch_fox"][None], N_DEV, ATT_SHARD, _full_to_shards(ATT_SHARD, lambda c: c), BF16, 256, "d_w_bfx_shards"),
         grads["w_out"].reshape(N_DEV, D // N_DEV, D),
         _relayout(grads["w_up"][None], N_DEV, SHARD_P, _full_to_shards(UP_SHARD, _UP_ORIG_TO_INTER.get), BF16, 256, "d_w_up_shards"),
         grads["w_down"].reshape(N_DEV, D_FF // N_DEV, D),
         d_cw.reshape(3, N_DEV, UP_SHARD).transpose(1, 0, 2),
         grads["meta_tokens"].reshape(N_META, N_DEV, ATT_SHARD).transpose(1, 0, 2)], "exchange_grads")
    tiles = dict(w_in=256, w_branch_sb=256, w_branch_fox=256, w_out=D // N_DEV, w_up=256, w_down=D_FF // N_DEV,
                 conv_w=3, meta_tokens=N_META)
    new = {name: _sum_adamw(p, mat(w, name), mat(m, name), mat(v, name), tiles[name], "adamw_" + name)
           for name, p in zip(sharded, parts)}

    rparts, = _all_gather([_pack_repl(grads)], "gather_replicated_grads")
    routs = _sum_adamw(rparts, _pack_repl(w), _pack_repl(m), _pack_repl(v), REPL_ROWS, "adamw_replicated")
    repl = [_unpack_repl(o, shapes) for o in routs]

    result = [lax.psum(loss, ("x", "y", "c")), grad_x]
    for k in range(4):
        for name in w:
            result.append(new[name][k].reshape(shapes[name]) if name in new else repl[k][name])
    return tuple(result)
```

```python
import functools

import jax
import jax.numpy as jnp
from jax import lax
from jax.experimental import pallas as pl
from jax.experimental.pallas import tpu as pltpu

F32 = jnp.float32
BF16 = jnp.bfloat16

N_DEV = 8
LANES = 128
D = 1024
N_META = 16
SEQ = 2048
L_REAL = N_META + SEQ
LP = 2304
BQ = 256
NBLK = LP // BQ
HEAD = 64
NH = 8
W_ATT = NH * HEAD
PAIR_W = 3 * LANES
D_FF = 2816
IN_COLS = 5128
QKV = 6 * W_ATT
IN_P = 5376
GATE_COL = QKV
F_COL = QKV + 2 * D
FFC = 256
RMS_EPS = 1e-6
LR, B1, B2, EPS, WD, STEP = 0.001, 0.9, 0.999, 1e-08, 0.01, 10
VMEM_LIMIT = 56 * 1024 * 1024

MESH = pl.DeviceIdType.MESH
ANY = pl.BlockSpec(memory_space=pl.ANY)


def _cparams(*sem):
    return pltpu.CompilerParams(dimension_semantics=sem if sem else None, vmem_limit_bytes=VMEM_LIMIT)


def _all_gather(xs, name):
    n = len(xs)

    def body(*refs):
        x_refs, out_refs = refs[:n], refs[n:2 * n]
        send_sems, recv_sems, local_sems = refs[2 * n:]
        mx, my, mc = lax.axis_index("x"), lax.axis_index("y"), lax.axis_index("c")
        me, sibling = (mx, my, mc), (mx, my, 1 - mc)
        chips = [(1 - mx, my), (mx, 1 - my), (1 - mx, 1 - my)]

        def copy(a, k, block, to, own=False):
            px, py, pc = block
            slot = out_refs[a].at[4 * px + 2 * py + pc]
            return pltpu.make_async_remote_copy(
                src_ref=x_refs[a] if own else slot, dst_ref=slot,
                send_sem=send_sems.at[7 * a + k], recv_sem=recv_sems.at[7 * a + k],
                device_id=to, device_id_type=MESH)

        mine = [pltpu.make_async_copy(x_refs[a], out_refs[a].at[4 * mx + 2 * my + mc], local_sems.at[a]) for a in range(n)]
        for cp in mine:
            cp.start()
        first = []
        for a in range(n):
            first.append(copy(a, 0, me, sibling, own=True))
            first += [copy(a, 1 + j, me, (*chip, mc), own=True) for j, chip in enumerate(chips)]
        for cp in first:
            cp.start()
        passed = []
        for j, chip in enumerate(chips):
            for a in range(n):
                copy(a, 1 + j, (*chip, mc), me).wait_recv()
                fwd = copy(a, 4 + j, (*chip, mc), sibling)
                fwd.start()
                passed.append(fwd)
        for a in range(n):
            copy(a, 0, sibling, me).wait_recv()
            for j, chip in enumerate(chips):
                copy(a, 4 + j, (*chip, 1 - mc), me).wait_recv()
        for cp in first + passed:
            cp.wait_send()
        for cp in mine:
            cp.wait()

    return pl.pallas_call(
        body, name=name,
        out_shape=tuple(jax.ShapeDtypeStruct((N_DEV,) + x.shape, x.dtype) for x in xs),
        in_specs=[ANY] * n, out_specs=tuple([ANY] * n),
        scratch_shapes=[pltpu.SemaphoreType.DMA((7 * n,)), pltpu.SemaphoreType.DMA((7 * n,)),
                        pltpu.SemaphoreType.DMA((n,))],
    )(*xs)


def _exchange(srcs, name):
    n = len(srcs)

    def body(*refs):
        src_refs, dst_refs = refs[:n], refs[n:2 * n]
        send_sems, recv_sems, local_sems = refs[2 * n:]
        mx, my, mc = lax.axis_index("x"), lax.axis_index("y"), lax.axis_index("c")
        me_idx = 4 * mx + 2 * my + mc
        mine = [pltpu.make_async_copy(src_refs[a].at[me_idx], dst_refs[a].at[me_idx], local_sems.at[a]) for a in range(n)]
        for cp in mine:
            cp.start()
        copies = []
        for k in range(1, N_DEV):
            px, py, pc = mx ^ (k >> 2), my ^ ((k >> 1) & 1), mc ^ (k & 1)
            for a in range(n):
                copies.append(pltpu.make_async_remote_copy(
                    src_ref=src_refs[a].at[4 * px + 2 * py + pc], dst_ref=dst_refs[a].at[me_idx],
                    send_sem=send_sems.at[7 * a + k - 1], recv_sem=recv_sems.at[7 * a + k - 1],
                    device_id=(px, py, pc), device_id_type=MESH))
        for cp in copies:
            cp.start()
        for cp in copies:
            cp.wait_recv()
        for cp in copies:
            cp.wait_send()
        for cp in mine:
            cp.wait()

    return pl.pallas_call(
        body, name=name,
        out_shape=tuple(jax.ShapeDtypeStruct(s.shape, s.dtype) for s in srcs),
        in_specs=[ANY] * n, out_specs=tuple([ANY] * n),
        scratch_shapes=[pltpu.SemaphoreType.DMA((7 * n,)), pltpu.SemaphoreType.DMA((7 * n,)),
                        pltpu.SemaphoreType.DMA((n,))],
    )(*srcs)


ROWS_PER_COPY = 256


def _pad_rows(front, body_rows, nseq, name):
    tail = LP - L_REAL
    nblk = SEQ // ROWS_PER_COPY

    def body(f_ref, b_ref, o_ref, z_ref, sems):
        s, i = pl.program_id(0), pl.program_id(1)
        rows = pltpu.make_async_copy(b_ref, o_ref.at[pl.ds(s, 1), pl.ds(N_META + i * ROWS_PER_COPY, ROWS_PER_COPY)], sems.at[0])
        rows.start()

        @pl.when(i == 0)
        def _():
            z_ref[...] = jnp.zeros_like(z_ref)
            head = pltpu.make_async_copy(f_ref, o_ref.at[s, pl.ds(0, N_META)], sems.at[1])
            zeros = pltpu.make_async_copy(z_ref, o_ref.at[s, pl.ds(L_REAL, tail)], sems.at[2])
            head.start()
            zeros.start()
            head.wait()
            zeros.wait()

        rows.wait()

    return pl.pallas_call(
        body, name=name, out_shape=jax.ShapeDtypeStruct((nseq, LP, D), F32), grid=(nseq, nblk),
        in_specs=[pl.BlockSpec((N_META, D), lambda s, i: (0, 0)), pl.BlockSpec((1, ROWS_PER_COPY, D), lambda s, i: (s, i, 0))],
        out_specs=ANY,
        scratch_shapes=[pltpu.VMEM((tail, D), F32), pltpu.SemaphoreType.DMA((3,))],
        compiler_params=_cparams("arbitrary", "arbitrary"))(front, body_rows)


def _real_rows(h, nseq, name):
    nblk = SEQ // ROWS_PER_COPY

    def body(h_ref, o_ref, sem):
        s, i = pl.program_id(0), pl.program_id(1)
        rows = pltpu.make_async_copy(h_ref.at[pl.ds(s, 1), pl.ds(N_META + i * ROWS_PER_COPY, ROWS_PER_COPY)], o_ref, sem)
        rows.start()
        rows.wait()

    return pl.pallas_call(
        body, name=name, out_shape=jax.ShapeDtypeStruct((nseq, SEQ, D), F32), grid=(nseq, nblk),
        in_specs=[ANY], out_specs=pl.BlockSpec((1, ROWS_PER_COPY, D), lambda s, i: (s, i, 0)),
        scratch_shapes=[pltpu.SemaphoreType.DMA],
        compiler_params=_cparams("arbitrary", "arbitrary"))(h)


def _plan_cols(n_q, n_dcols, src_of):
    plan = {}
    for q in range(n_q):
        for dblk in range(n_dcols // LANES):
            segs, key, start = [], None, 0
            for lane in range(LANES + 1):
                new = None
                if lane < LANES:
                    src = src_of(q, dblk * LANES + lane)
                    if src is not None:
                        new = (src[0], src[1] // LANES, (lane - src[1] % LANES) % LANES)
                if new != key:
                    if key is not None:
                        segs.append((*key, start, lane))
                    key, start = new, lane
            plan[(q, dblk)] = segs
    return plan


def _relayout(src, n_q, n_dcols, src_of, out_dtype, tr, name):
    n_p, rows, scols = src.shape
    plan = _plan_cols(n_q, n_dcols, src_of)

    def body(s_ref, d_ref):
        lane = lax.broadcasted_iota(jnp.int32, (tr, LANES), 1)
        for (q, dblk), segs in plan.items():
            acc = jnp.zeros((tr, LANES), F32)
            for p, sblk, rot, lo, hi in segs:
                x = s_ref[p, :, sblk * LANES:(sblk + 1) * LANES].astype(F32)
                if rot:
                    x = pltpu.roll(x, rot, 1)
                acc = x if (lo, hi) == (0, LANES) else jnp.where((lane >= lo) & (lane < hi), x, acc)
            d_ref[q, :, dblk * LANES:(dblk + 1) * LANES] = acc.astype(out_dtype)

    return pl.pallas_call(
        body, name=name, out_shape=jax.ShapeDtypeStruct((n_q, rows, n_dcols), out_dtype), grid=(rows // tr,),
        in_specs=[pl.BlockSpec((n_p, tr, scols), lambda i: (0, i, 0))],
        out_specs=pl.BlockSpec((n_q, tr, n_dcols), lambda i: (0, i, 0)),
        compiler_params=_cparams("parallel"))(src)


def _in_padded_to_orig(d):
    if d < QKV:
        kind, r = divmod(d, 4 * PAIR_W)
        pair, r = divmod(r, PAIR_W)
        part, r = divmod(r, LANES)
        return kind * 3 * W_ATT + part * W_ATT + pair * LANES + r
    if d < F_COL:
        return d + NH
    if d < F_COL + NH:
        return d - 2 * D
    return None


_IN_ORIG_TO_PADDED = {_in_padded_to_orig(d): d for d in range(IN_P) if _in_padded_to_orig(d) is not None}


def _up_inter_to_orig(d):
    j, r = divmod(d, 2 * FFC)
    part, r = divmod(r, FFC)
    return part * D_FF + j * FFC + r


_UP_ORIG_TO_INTER = {_up_inter_to_orig(d): d for d in range(2 * D_FF)}
IN_SHARD = IN_COLS // N_DEV
UP_SHARD = 2 * D_FF // N_DEV
SHARD_P = 768
ATT_SHARD = D // N_DEV


def _gathered_to_full(n_shard, to_orig):
    def src_of(q, d):
        c = to_orig(d)
        return None if c is None else (c // n_shard, c % n_shard)
    return src_of


def _full_to_shards(n_shard, from_orig):
    def src_of(q, d):
        return (0, from_orig(q * n_shard + d)) if d < n_shard else None
    return src_of


def _matmul(a, b, *, out_dtype, tm, tn, tk, ta=False, tb=False, name):
    if ta:
        kdim, m = a.shape
    else:
        m, kdim = a.shape
    n = b.shape[0] if tb else b.shape[1]
    assert m % tm == 0 and n % tn == 0 and kdim % tk == 0, (name, a.shape, b.shape, tm, tn, tk)
    nk = kdim // tk

    def body(a_ref, b_ref, o_ref, *scratch):
        av, bv = a_ref[...], b_ref[...]
        if ta:
            p = lax.dot_general(av, bv, (((0,), (0,)), ((), ())), preferred_element_type=F32)
        elif tb:
            p = lax.dot_general(av, bv, (((1,), (1,)), ((), ())), preferred_element_type=F32)
        else:
            p = jnp.dot(av, bv, preferred_element_type=F32)
        if nk == 1:
            o_ref[...] = p.astype(o_ref.dtype)
        else:
            acc_ref, = scratch
            k = pl.program_id(2)

            @pl.when(k == 0)
            def _():
                acc_ref[...] = p

            @pl.when(k > 0)
            def _():
                acc_ref[...] += p

            @pl.when(k == nk - 1)
            def _():
                o_ref[...] = acc_ref[...].astype(o_ref.dtype)

    a_spec = pl.BlockSpec((tk, tm), lambda i, j, k: (k, i)) if ta else pl.BlockSpec((tm, tk), lambda i, j, k: (i, k))
    b_spec = pl.BlockSpec((tn, tk), lambda i, j, k: (j, k)) if tb else pl.BlockSpec((tk, tn), lambda i, j, k: (k, j))
    return pl.pallas_call(
        body, name=name,
        out_shape=jax.ShapeDtypeStruct((m, n), out_dtype),
        grid=(m // tm, n // tn, nk),
        in_specs=[a_spec, b_spec],
        out_specs=pl.BlockSpec((tm, tn), lambda i, j, k: (i, j)),
        scratch_shapes=[] if nk == 1 else [pltpu.VMEM((tm, tn), F32)],
        compiler_params=_cparams("parallel", "parallel", "arbitrary"),
    )(a, b)


TR = 288


def _rms(h):
    return lax.rsqrt(jnp.mean(h * h, axis=-1, keepdims=True) + RMS_EPS)


def _norm_fwd(h, delta, g, name):
    t = h.shape[0]
    row = pl.BlockSpec((TR, D), lambda i: (i, 0))
    vec = pl.BlockSpec((1, D), lambda i: (0, 0))

    if delta is None:
        def body(h_ref, g_ref, n_ref):
            hv = h_ref[...]
            n_ref[...] = ((hv * _rms(hv)) * g_ref[...]).astype(BF16)

        n = pl.pallas_call(
            body, name=name, out_shape=jax.ShapeDtypeStruct((t, D), BF16), grid=(t // TR,),
            in_specs=[row, vec], out_specs=row, compiler_params=_cparams("parallel"))(h, g)
        return h, n

    def body(h_ref, d_ref, g_ref, hn_ref, n_ref):
        hv = h_ref[...] + d_ref[...]
        hn_ref[...] = hv
        n_ref[...] = ((hv * _rms(hv)) * g_ref[...]).astype(BF16)

    return pl.pallas_call(
        body, name=name,
        out_shape=(jax.ShapeDtypeStruct((t, D), F32), jax.ShapeDtypeStruct((t, D), BF16)), grid=(t // TR,),
        in_specs=[row, row, vec], out_specs=(row, row), compiler_params=_cparams("parallel"))(h, delta, g)


def _rms_bwd_math(hv, dn, gv):
    r = _rms(hv)
    hr = hv * r
    dng = dn * gv
    dh = r * (dng - hr * jnp.mean(dng * hr, axis=-1, keepdims=True))
    return dh, dn * hr


def _final_loss_bwd(h1, delta, g, tgt, name):
    t = h1.shape[0]
    row = pl.BlockSpec((TR, D), lambda i: (i, 0))
    vec = pl.BlockSpec((1, D), lambda i: (0, 0))
    tiles_per_seq = LP // TR

    def body(h_ref, d_ref, g_ref, t_ref, loss_ref, dh_ref, dhb_ref, dg_ref):
        i = pl.program_id(0)
        hv = h_ref[...] + d_ref[...]
        gv = g_ref[...]
        r = _rms(hv)
        hr = hv * r
        y = hr * gv
        pos = (i % tiles_per_seq) * TR + lax.broadcasted_iota(jnp.int32, (TR, 1), 0)
        valid = (pos >= N_META) & (pos < L_REAL)
        err = jnp.where(valid, y - t_ref[...], 0.0)
        part = 0.5 * jnp.sum(jnp.mean(err * err, axis=-1, keepdims=True))
        dy = err * (1.0 / D)
        dng = dy * gv
        dh = r * (dng - hr * jnp.mean(dng * hr, axis=-1, keepdims=True))
        dh_ref[...] = dh
        dhb_ref[...] = dh.astype(BF16)
        dgp = jnp.sum(dy * hr, axis=0, keepdims=True)

        @pl.when(i == 0)
        def _():
            loss_ref[...] = jnp.zeros_like(loss_ref)
            dg_ref[...] = jnp.zeros_like(dg_ref)

        loss_ref[...] += part
        dg_ref[...] += dgp

    return pl.pallas_call(
        body, name=name,
        out_shape=(jax.ShapeDtypeStruct((8, 128), F32), jax.ShapeDtypeStruct((t, D), F32),
                   jax.ShapeDtypeStruct((t, D), BF16), jax.ShapeDtypeStruct((1, D), F32)),
        grid=(t // TR,),
        in_specs=[row, row, vec, row],
        out_specs=(pl.BlockSpec((8, 128), lambda i: (0, 0)), row, row, vec),
        compiler_params=_cparams("arbitrary"))(h1, delta, g, tgt)


def _norm_bwd(h, dn, g, dres, with_bf16, name):
    t = h.shape[0]
    row = pl.BlockSpec((TR, D), lambda i: (i, 0))
    vec = pl.BlockSpec((1, D), lambda i: (0, 0))

    def body(h_ref, dn_ref, g_ref, dres_ref, *outs):
        i = pl.program_id(0)
        dh, dgrow = _rms_bwd_math(h_ref[...], dn_ref[...], g_ref[...])
        dh = dh + dres_ref[...]
        outs[0][...] = dh
        if with_bf16:
            outs[1][...] = dh.astype(BF16)
        dg_ref = outs[-1]

        @pl.when(i == 0)
        def _():
            dg_ref[...] = jnp.zeros_like(dg_ref)

        dg_ref[...] += jnp.sum(dgrow, axis=0, keepdims=True)

    shapes = [jax.ShapeDtypeStruct((t, D), F32)]
    specs = [row]
    if with_bf16:
        shapes.append(jax.ShapeDtypeStruct((t, D), BF16))
        specs.append(row)
    shapes.append(jax.ShapeDtypeStruct((1, D), F32))
    specs.append(vec)
    return pl.pallas_call(
        body, name=name, out_shape=tuple(shapes), grid=(t // TR,),
        in_specs=[row, row, vec, row], out_specs=tuple(specs),
        compiler_params=_cparams("arbitrary"))(h, dn, g, dres)


GATE_BLK = GATE_COL // D


def _sigmoid(x):
    return 1.0 / (1.0 + jnp.exp(-x))


def _merge_fwd(p_sb, p_fx, proj, name):
    t = p_sb.shape[0]
    row = pl.BlockSpec((TR, D), lambda i: (i, 0))

    def body(ps_ref, pf_ref, gs_ref, gf_ref, o_ref):
        o_ref[...] = (_sigmoid(gs_ref[...]) * ps_ref[...] + _sigmoid(gf_ref[...]) * pf_ref[...]).astype(BF16)

    return pl.pallas_call(
        body, name=name, out_shape=jax.ShapeDtypeStruct((t, D), BF16), grid=(t // TR,),
        in_specs=[row, row, pl.BlockSpec((TR, D), lambda i: (i, GATE_BLK)),
                  pl.BlockSpec((TR, D), lambda i: (i, GATE_BLK + 1))],
        out_specs=row, compiler_params=_cparams("parallel"))(p_sb, p_fx, proj, proj)


def _merge_bwd(dm, p, proj, dproj, which, name):
    t = dm.shape[0]
    row = pl.BlockSpec((TR, D), lambda i: (i, 0))
    gate = pl.BlockSpec((TR, D), lambda i: (i, GATE_BLK + which))

    def body(dm_ref, p_ref, g_ref, *rest):
        dp_ref, dg_ref = rest[-2:]
        dmv = dm_ref[...]
        s = _sigmoid(g_ref[...])
        dp_ref[...] = (dmv * s).astype(BF16)
        dg_ref[...] = (dmv * p_ref[...] * s * (1.0 - s)).astype(BF16)

    out_shape = (jax.ShapeDtypeStruct((t, D), BF16), jax.ShapeDtypeStruct((t, IN_P), BF16))
    if dproj is None:
        return pl.pallas_call(
            body, name=name, out_shape=out_shape, grid=(t // TR,), in_specs=[row, row, gate],
            out_specs=(row, gate), compiler_params=_cparams("parallel"))(dm, p, proj)
    return pl.pallas_call(
        body, name=name, out_shape=out_shape, grid=(t // TR,), in_specs=[row, row, gate, ANY],
        out_specs=(row, gate), input_output_aliases={3: 1}, compiler_params=_cparams("parallel"))(dm, p, proj, dproj)


CH = 288


def _chunk(c, n=CH):
    return pl.ds(pl.multiple_of(c * CH, 8), n)


def _conv_taps(u_ref, c):
    x = u_ref[_chunk(c), :]
    prev = u_ref[pl.ds(pl.multiple_of(jnp.maximum(c * CH - 8, 0), 8), 8), :]
    xx = jnp.concatenate([jnp.where(c == 0, 0.0, prev), x], axis=0)
    return x, pltpu.roll(xx, 1, 0)[8:], pltpu.roll(xx, 2, 0)[8:]


def _conv_glu_fwd(u, cw, nseq, name):
    nblk = D_FF // FFC

    def body(u_ref, cw_ref, o_ref):
        cwv = cw_ref[...]

        def step(c, _):
            x, x1, x2 = _conv_taps(u_ref, c)
            uc = cwv[0:1, :] * x2 + cwv[1:2, :] * x1 + cwv[2:3, :] * x
            a, b = uc[:, :FFC], uc[:, FFC:]
            o_ref[_chunk(c), :] = (a * _sigmoid(a) * b).astype(BF16)
            return 0

        lax.fori_loop(0, LP // CH, step, 0)

    return pl.pallas_call(
        body, name=name, out_shape=jax.ShapeDtypeStruct((nseq * LP, D_FF), BF16), grid=(nseq, nblk),
        in_specs=[pl.BlockSpec((LP, 2 * FFC), lambda s, j: (s, j)), pl.BlockSpec((3, 2 * FFC), lambda s, j: (0, j))],
        out_specs=pl.BlockSpec((LP, FFC), lambda s, j: (s, j)),
        compiler_params=_cparams("parallel", "parallel"))(u, cw)


def _conv_glu_bwd(u, cw, dact, nseq, name):
    nblk = D_FF // FFC
    nch = LP // CH

    def body(u_ref, cw_ref, da_ref, du_ref, dcw_ref):
        s = pl.program_id(1)
        cwv = cw_ref[...]

        def step(k, carry):
            nxt, p0, p1, p2 = carry
            c = nch - 1 - k
            x, x1, x2 = _conv_taps(u_ref, c)
            uc = cwv[0:1, :] * x2 + cwv[1:2, :] * x1 + cwv[2:3, :] * x
            a, b = uc[:, :FFC], uc[:, FFC:]
            sa = _sigmoid(a)
            dactv = da_ref[_chunk(c), :]
            da = dactv * b * (sa * (1.0 + a * (1.0 - sa)))
            db = dactv * (a * sa)
            duc = jnp.concatenate([da, db], axis=1)
            dd = jnp.concatenate([duc, nxt], axis=0)
            du = (cwv[2:3, :] * duc + cwv[1:2, :] * pltpu.roll(dd, CH + 7, 0)[:CH]
                  + cwv[0:1, :] * pltpu.roll(dd, CH + 6, 0)[:CH])
            du_ref[_chunk(c), :] = du.astype(BF16)
            return (duc[:8], p0 + jnp.sum(duc * x2, axis=0, keepdims=True),
                    p1 + jnp.sum(duc * x1, axis=0, keepdims=True), p2 + jnp.sum(duc * x, axis=0, keepdims=True))

        zrow = jnp.zeros((1, 2 * FFC), F32)
        _, p0, p1, p2 = lax.fori_loop(0, nch, step, (jnp.zeros((8, 2 * FFC), F32), zrow, zrow, zrow))

        @pl.when(s == 0)
        def _():
            dcw_ref[...] = jnp.zeros_like(dcw_ref)

        dcw_ref[...] += jnp.concatenate([p0, p1, p2], axis=0)

    return pl.pallas_call(
        body, name=name,
        out_shape=(jax.ShapeDtypeStruct((nseq * LP, 2 * D_FF), BF16), jax.ShapeDtypeStruct((3, 2 * D_FF), F32)),
        grid=(nblk, nseq),
        in_specs=[pl.BlockSpec((LP, 2 * FFC), lambda j, s: (s, j)), pl.BlockSpec((3, 2 * FFC), lambda j, s: (0, j)),
                  pl.BlockSpec((LP, FFC), lambda j, s: (s, j))],
        out_specs=(pl.BlockSpec((LP, 2 * FFC), lambda j, s: (s, j)), pl.BlockSpec((3, 2 * FFC), lambda j, s: (0, j))),
        compiler_params=_cparams("parallel", "arbitrary"))(u, cw, dact)


F_BLK = F_COL // LANES
CB = 128


def _split3(x):
    hi = x.astype(BF16)
    r1 = x - hi.astype(F32)
    mid = r1.astype(BF16)
    lo = (r1 - mid.astype(F32)).astype(BF16)
    return hi, mid, lo


def _tri_dot(tri, x):
    hi, mid, lo = _split3(x)
    d = functools.partial(jnp.dot, preferred_element_type=F32)
    return d(tri, hi) + d(tri, mid) + d(tri, lo)


def _log_sigmoid(x):
    return jnp.minimum(x, 0.0) - jnp.log(1.0 + jnp.exp(-jnp.abs(x)))


def _gate_fwd(proj, bf, nseq, name):
    def body(f_ref, b_ref, c_ref):
        r_i = lax.broadcasted_iota(jnp.int32, (CB, CB), 0)
        c_i = lax.broadcasted_iota(jnp.int32, (CB, CB), 1)
        tri = (c_i <= r_i).astype(BF16)
        bv = b_ref[...]

        def step(k, carry):
            rows = pl.ds(pl.multiple_of(k * CB, CB), CB)
            lf = _log_sigmoid(f_ref[rows, :] + bv)
            c_ref[rows, :] = _tri_dot(tri, lf) + carry
            return carry + jnp.sum(lf, axis=0, keepdims=True)

        lax.fori_loop(0, LP // CB, step, jnp.zeros((1, LANES), F32))

    return pl.pallas_call(
        body, name=name, out_shape=jax.ShapeDtypeStruct((nseq * LP, LANES), F32), grid=(nseq,),
        in_specs=[pl.BlockSpec((LP, LANES), lambda s: (s, F_BLK)), pl.BlockSpec((1, LANES), lambda s: (0, 0))],
        out_specs=pl.BlockSpec((LP, LANES), lambda s: (s, 0)),
        compiler_params=_cparams("parallel"))(proj, bf)


def _gate_bwd(proj, bf, dc, dproj, nseq, name):
    def body(f_ref, b_ref, dc_ref, _, df_ref, db_ref):
        s = pl.program_id(0)
        r_i = lax.broadcasted_iota(jnp.int32, (CB, CB), 0)
        c_i = lax.broadcasted_iota(jnp.int32, (CB, CB), 1)
        tri = (c_i >= r_i).astype(BF16)
        bv = b_ref[...]

        def step(kk, carry):
            carry_c, carry_b = carry
            k = LP // CB - 1 - kk
            rows = pl.ds(pl.multiple_of(k * CB, CB), CB)
            dcv = dc_ref[rows, :]
            dlf = _tri_dot(tri, dcv) + carry_c
            df = dlf * _sigmoid(-(f_ref[rows, :] + bv))
            df_ref[rows, :] = jnp.concatenate([df, jnp.zeros_like(df)], axis=1).astype(BF16)
            return carry_c + jnp.sum(dcv, axis=0, keepdims=True), carry_b + jnp.sum(df, axis=0, keepdims=True)

        zero = jnp.zeros((1, LANES), F32)
        _, dbp = lax.fori_loop(0, LP // CB, step, (zero, zero))

        @pl.when(s == 0)
        def _():
            db_ref[...] = jnp.zeros_like(db_ref)

        db_ref[...] += dbp

    return pl.pallas_call(
        body, name=name,
        out_shape=(jax.ShapeDtypeStruct(dproj.shape, BF16), jax.ShapeDtypeStruct((1, LANES), F32)), grid=(nseq,),
        in_specs=[pl.BlockSpec((LP, LANES), lambda s: (s, F_BLK)), pl.BlockSpec((1, LANES), lambda s: (0, 0)),
                  pl.BlockSpec((LP, LANES), lambda s: (s, 0)), ANY],
        out_specs=(pl.BlockSpec((LP, 2 * LANES), lambda s: (s, F_COL // (2 * LANES))), pl.BlockSpec((1, LANES), lambda s: (0, 0))),
        input_output_aliases={3: 0},
        compiler_params=_cparams("arbitrary"))(proj, bf, dc, dproj)


SCALE = 0.125
NEG = -1e30


def _dot_nt(a, b):
    return lax.dot_general(a, b, (((1,), (1,)), ((), ())), preferred_element_type=F32)


def _dot_tn(a, b):
    return lax.dot_general(a, b, (((0,), (0,)), ((), ())), preferred_element_type=F32)


def _dot(a, b):
    return jnp.dot(a, b, preferred_element_type=F32)


def _blk(i):
    return pl.ds(pl.multiple_of(i * BQ, BQ), BQ)


def _tile_iotas():
    return lax.broadcasted_iota(jnp.int32, (BQ, BQ), 0), lax.broadcasted_iota(jnp.int32, (BQ, BQ), 1)


def _lane_iota():
    return lax.broadcasted_iota(jnp.int32, (BQ, LANES), 1)


def _head_masks():
    lane = _lane_iota()
    return lane < HEAD, lane >= HEAD


def _only(mask, x):
    return jnp.where(mask, x, jnp.zeros_like(x))


def _pick_lane(x, idx):
    return jnp.sum(jnp.where(_lane_iota() == idx, x, 0.0), axis=1, keepdims=True)


def _load_qkv(p_ref, q_s, k_s, v_s):
    q_s[...] = (p_ref[:, 0:LANES] * SCALE).astype(BF16)
    k_s[...] = p_ref[:, LANES:2 * LANES].astype(BF16)
    v_s[...] = p_ref[:, 2 * LANES:3 * LANES].astype(BF16)


def _softplus(z):
    return jnp.maximum(z, 0.0) + jnp.log(1.0 + jnp.exp(-jnp.abs(z)))


def _hi_lo(x):
    hi = x.astype(BF16)
    return hi, (x - hi.astype(F32)).astype(BF16)


def _sb_tile_weights(q2, kj, strict, r2, u_suf):
    z = [_dot_nt(q, kj) for q in q2]
    sp = [_softplus(zh) for zh in z]
    lk = [-sph if strict is None else jnp.where(strict, -sph, 0.0) for sph in sp]
    parts = [_hi_lo(lkh) for lkh in lk]
    suf = [_dot(hi, u_suf) + _dot(lo, u_suf) for hi, lo in parts]
    w = [jnp.exp(z[h] - sp[h] + r2[h] + suf[h]) for h in range(2)]
    if strict is not None:
        w = [jnp.where(strict, wh, 0.0) for wh in w]
    return w, sp, lk


def _pair_spec(kind):
    return pl.BlockSpec((LP, PAIR_W), lambda s, p: (s, 4 * kind + p))


HEADS_SPEC = pl.BlockSpec((LP, LANES), lambda s, p: (s, p))
SEQ_SPEC = pl.BlockSpec((LP, LANES), lambda s, p: (s, 0))
QKV_SCRATCH = [pltpu.VMEM((LP, LANES), BF16)] * 3
RS_STRIDE = 16


def _sb_fwd(proj, nseq, name):
    t = nseq * LP

    def body(p_ref, o_ref, rs_ref, q_s, k_s, v_s, acc_ref, r_ref, rb_ref):
        _load_qkv(p_ref, q_s, k_s, v_s)
        row, col = _tile_iotas()
        u_suf = (row > col).astype(BF16)
        diag = col < row
        lane = _lane_iota()
        heads = _head_masks()

        def qblock(i, _):
            acc_ref[...] = jnp.zeros_like(acc_ref)
            rb_ref[...] = jnp.zeros_like(rb_ref)
            r_ref[...] = jnp.zeros_like(r_ref)
            qb = q_s[_blk(i), :]

            def tile(j, strict):
                kj, vj = k_s[_blk(j), :], v_s[_blk(j), :]
                r2 = [r_ref[0], r_ref[1]]
                w, _, lk = _sb_tile_weights([_only(heads[h], qb) for h in range(2)], kj, strict, r2, u_suf)
                acc_ref[...] += (_dot(w[0].astype(BF16), _only(heads[0], vj)) + _dot(w[1].astype(BF16), _only(heads[1], vj)))
                rb_ref[...] = jnp.where(lane == j, r2[0], jnp.where(lane == RS_STRIDE + j, r2[1], rb_ref[...]))
                for h in range(2):
                    r_ref[h] = r2[h] + jnp.sum(lk[h], axis=1, keepdims=True)

            tile(i, diag)

            def kblock(jj, _):
                tile(i - jj, None)
                return 0

            lax.fori_loop(1, i + 1, kblock, 0)
            o_ref[_blk(i), :] = acc_ref[...].astype(BF16)
            rs_ref[_blk(i), :] = rb_ref[...]
            return 0

        lax.fori_loop(0, NBLK, qblock, 0)

    return pl.pallas_call(
        body, name=name,
        out_shape=(jax.ShapeDtypeStruct((t, W_ATT), BF16), jax.ShapeDtypeStruct((t, W_ATT), F32)),
        grid=(nseq, NH // 2), in_specs=[_pair_spec(0)], out_specs=(HEADS_SPEC, HEADS_SPEC),
        scratch_shapes=QKV_SCRATCH + [pltpu.VMEM((BQ, LANES), F32), pltpu.VMEM((2, BQ, 1), F32), pltpu.VMEM((BQ, LANES), F32)],
        compiler_params=_cparams("parallel", "parallel"))(proj)


def _sb_bwd(proj, do, rs, dproj, nseq, name):
    def body(p_ref, do_ref, rs_ref, _, dp_ref, q_s, k_s, v_s, dqa_ref, dka_ref, dva_ref, ep_ref):
        _load_qkv(p_ref, q_s, k_s, v_s)
        row, col = _tile_iotas()
        u_suf = (row > col).astype(BF16)
        u_pre = (row < col).astype(BF16)
        diag = col < row
        heads = _head_masks()
        dka_ref[...] = jnp.zeros_like(dka_ref)
        dva_ref[...] = jnp.zeros_like(dva_ref)

        def qblock(i, _):
            rb = rs_ref[_blk(i), :]
            qb = q_s[_blk(i), :]
            dob = do_ref[_blk(i), :]
            dqa_ref[...] = jnp.zeros_like(dqa_ref)
            ep_ref[...] = jnp.zeros_like(ep_ref)

            def tile(j, strict):
                kj, vj = k_s[_blk(j), :], v_s[_blk(j), :]
                q2 = [_only(heads[h], qb) for h in range(2)]
                do2 = [_only(heads[h], dob) for h in range(2)]
                r2 = [_pick_lane(rb, RS_STRIDE * h + j) for h in range(2)]
                dw = [_dot_nt(do2[h], vj) for h in range(2)]
                w, sp, _ = _sb_tile_weights(q2, kj, strict, r2, u_suf)
                e = [dw[h] * w[h] for h in range(2)]
                e_pre = [ep_ref[h] + _dot(e[h].astype(BF16), u_pre) for h in range(2)]
                dz = []
                for h in range(2):
                    ep_ref[h] += jnp.sum(e[h], axis=1, keepdims=True)
                    sneg = jnp.exp(-sp[h])
                    dzh = e[h] * sneg - (1.0 - sneg) * e_pre[h]
                    if strict is not None:
                        dzh = jnp.where(strict, dzh, 0.0)
                    dz.append(dzh.astype(BF16))
                dqa_ref[...] += _dot(dz[0], _only(heads[0], kj)) + _dot(dz[1], _only(heads[1], kj))
                dka_ref[_blk(j), :] += _dot_tn(dz[0], q2[0]) + _dot_tn(dz[1], q2[1])
                dva_ref[_blk(j), :] += _dot_tn(w[0].astype(BF16), do2[0]) + _dot_tn(w[1].astype(BF16), do2[1])

            def kblock(j, _):
                tile(j, None)
                return 0

            lax.fori_loop(0, i, kblock, 0)
            tile(i, diag)
            dp_ref[_blk(i), 0:LANES] = (dqa_ref[...] * SCALE).astype(BF16)
            return 0

        lax.fori_loop(0, NBLK, qblock, 0)
        dp_ref[:, LANES:2 * LANES] = dka_ref[...].astype(BF16)
        dp_ref[:, 2 * LANES:3 * LANES] = dva_ref[...].astype(BF16)

    return pl.pallas_call(
        body, name=name, out_shape=jax.ShapeDtypeStruct(dproj.shape, BF16), grid=(nseq, NH // 2),
        in_specs=[_pair_spec(0), HEADS_SPEC, HEADS_SPEC, ANY], out_specs=_pair_spec(0),
        input_output_aliases={3: 0},
        scratch_shapes=QKV_SCRATCH + [pltpu.VMEM((BQ, LANES), F32), pltpu.VMEM((LP, LANES), F32),
                                      pltpu.VMEM((LP, LANES), F32), pltpu.VMEM((2, BQ, 1), F32)],
        compiler_params=_cparams("parallel", "parallel"))(proj, do, rs, dproj)


CROW_SPEC = pl.BlockSpec((None, NH, LP), lambda s, p: (s, 0, 0))


def _fox_scores(qi, kj, cq, ck, causal):
    z = _dot_nt(qi, kj) + (cq - ck)
    return z if causal is None else jnp.where(causal, z, NEG)


def _key_cols(cr_ref, head, j):
    return cr_ref[pl.ds(head, 1), pl.ds(pl.multiple_of(j * BQ, BQ), BQ)]


def _fox_fwd(proj, c, crow, nseq, name):
    t = nseq * LP

    def body(p_ref, c_ref, cr_ref, o_ref, o32_ref, lse_ref, q_s, k_s, v_s, acc_ref, m_ref, l_ref):
        _load_qkv(p_ref, q_s, k_s, v_s)
        row, col = _tile_iotas()
        diag = col <= row
        lane = _lane_iota()
        heads = _head_masks()
        pair = pl.program_id(1)

        def qblock(i, _):
            cblk = c_ref[_blk(i), :]
            qb = q_s[_blk(i), :]
            cq = [_pick_lane(cblk, 2 * pair + hh) for hh in range(2)]
            acc_ref[...] = jnp.zeros_like(acc_ref)
            m_ref[...] = jnp.full_like(m_ref, NEG)
            l_ref[...] = jnp.zeros_like(l_ref)

            def tile(j, causal):
                kj, vj = k_s[_blk(j), :], v_s[_blk(j), :]
                z = [_fox_scores(_only(heads[h], qb), kj, cq[h], _key_cols(cr_ref, 2 * pair + h, j), causal) for h in range(2)]
                p, alpha = [], []
                for h in range(2):
                    m_old = m_ref[h]
                    m_new = jnp.maximum(m_old, jnp.max(z[h], axis=1, keepdims=True))
                    alpha.append(jnp.exp(m_old - m_new))
                    ph = jnp.exp(z[h] - m_new)
                    l_ref[h] = alpha[h] * l_ref[h] + jnp.sum(ph, axis=1, keepdims=True)
                    m_ref[h] = m_new
                    p.append(ph.astype(BF16))
                pv = [_dot(p[h], _only(heads[h], vj)) for h in range(2)]
                for h in range(2):
                    acc_ref[h] = alpha[h] * acc_ref[h] + pv[h]

            def kblock(j, _):
                tile(j, None)
                return 0

            lax.fori_loop(0, i, kblock, 0)
            tile(i, diag)
            out = acc_ref[0] / l_ref[0] + acc_ref[1] / l_ref[1]
            o_ref[_blk(i), :] = out.astype(BF16)
            o32_ref[_blk(i), :] = out
            lse = [m_ref[hh] + jnp.log(l_ref[hh]) for hh in range(2)]
            lse_ref[_blk(i), :] = jnp.where(lane == 0, lse[0], jnp.where(lane == 1, lse[1], 0.0))
            return 0

        lax.fori_loop(0, NBLK, qblock, 0)

    return pl.pallas_call(
        body, name=name,
        out_shape=(jax.ShapeDtypeStruct((t, W_ATT), BF16), jax.ShapeDtypeStruct((t, W_ATT), F32),
                   jax.ShapeDtypeStruct((t, W_ATT), F32)),
        grid=(nseq, NH // 2), in_specs=[_pair_spec(1), SEQ_SPEC, CROW_SPEC], out_specs=(HEADS_SPEC, HEADS_SPEC, HEADS_SPEC),
        scratch_shapes=QKV_SCRATCH + [pltpu.VMEM((2, BQ, LANES), F32), pltpu.VMEM((2, BQ, 1), F32), pltpu.VMEM((2, BQ, 1), F32)],
        compiler_params=_cparams("parallel", "parallel"))(proj, c, crow)


def _fox_bwd(proj, c, crow, o32, lse, do, dproj, nseq, name):
    t = nseq * LP

    def body(p_ref, c_ref, cr_ref, o_ref, lse_ref, do_ref, _, dp_ref, dck_ref, dcq_ref,
             q_s, k_s, v_s, dqa_ref, dka_ref, dva_ref, rsum_ref):
        _load_qkv(p_ref, q_s, k_s, v_s)
        row, col = _tile_iotas()
        diag = col <= row
        lane = _lane_iota()
        heads = _head_masks()
        sub = lax.broadcasted_iota(jnp.int32, (NH, BQ), 0)
        pair = pl.program_id(1)
        dka_ref[...] = jnp.zeros_like(dka_ref)
        dva_ref[...] = jnp.zeros_like(dva_ref)

        @pl.when(pair == 0)
        def _():
            dck_ref[...] = jnp.zeros_like(dck_ref)
            dcq_ref[...] = jnp.zeros_like(dcq_ref)

        def qblock(i, _):
            dqa_ref[...] = jnp.zeros_like(dqa_ref)
            rsum_ref[...] = jnp.zeros_like(rsum_ref)
            cblk = c_ref[_blk(i), :]
            lblk = lse_ref[_blk(i), :]
            qb = q_s[_blk(i), :]
            dob = do_ref[_blk(i), :]
            prod = dob.astype(F32) * o_ref[_blk(i), :]
            cq = [_pick_lane(cblk, 2 * pair + hh) for hh in range(2)]
            lse_i = [_pick_lane(lblk, hh) for hh in range(2)]
            delta = [jnp.sum(_only(heads[hh], prod), axis=1, keepdims=True) for hh in range(2)]

            def tile(j, causal):
                kj, vj = k_s[_blk(j), :], v_s[_blk(j), :]
                keys = pl.ds(pl.multiple_of(j * BQ, BQ), BQ)
                q2 = [_only(heads[h], qb) for h in range(2)]
                do2 = [_only(heads[h], dob) for h in range(2)]
                z = [_fox_scores(q2[h], kj, cq[h], _key_cols(cr_ref, 2 * pair + h, j), causal) for h in range(2)]
                dp = [_dot_nt(do2[h], vj) for h in range(2)]
                p = [jnp.exp(z[h] - lse_i[h]) for h in range(2)]
                ds = [p[h] * (dp[h] - delta[h]) for h in range(2)]
                dsb = [d.astype(BF16) for d in ds]
                dqa_ref[...] += _dot(dsb[0], _only(heads[0], kj)) + _dot(dsb[1], _only(heads[1], kj))
                dka_ref[_blk(j), :] += _dot_tn(dsb[0], q2[0]) + _dot_tn(dsb[1], q2[1])
                dva_ref[_blk(j), :] += _dot_tn(p[0].astype(BF16), do2[0]) + _dot_tn(p[1].astype(BF16), do2[1])
                col_sums = (jnp.where(sub == 2 * pair, jnp.sum(ds[0], axis=0, keepdims=True), 0.0)
                            + jnp.where(sub == 2 * pair + 1, jnp.sum(ds[1], axis=0, keepdims=True), 0.0))
                dck_ref[:, keys] = dck_ref[:, keys] - col_sums
                for h in range(2):
                    rsum_ref[h] += jnp.sum(ds[h], axis=1, keepdims=True)

            def kblock(j, _):
                tile(j, None)
                return 0

            lax.fori_loop(0, i, kblock, 0)
            tile(i, diag)
            dp_ref[_blk(i), 0:LANES] = (dqa_ref[...] * SCALE).astype(BF16)
            dcq_ref[_blk(i), :] += (jnp.where(lane == 2 * pair, rsum_ref[0], 0.0)
                                    + jnp.where(lane == 2 * pair + 1, rsum_ref[1], 0.0))
            return 0

        lax.fori_loop(0, NBLK, qblock, 0)
        dp_ref[:, LANES:2 * LANES] = dka_ref[...].astype(BF16)
        dp_ref[:, 2 * LANES:3 * LANES] = dva_ref[...].astype(BF16)

    return pl.pallas_call(
        body, name=name,
        out_shape=(jax.ShapeDtypeStruct(dproj.shape, BF16), jax.ShapeDtypeStruct((nseq, NH, LP), F32),
                   jax.ShapeDtypeStruct((t, LANES), F32)),
        grid=(nseq, NH // 2),
        in_specs=[_pair_spec(1), SEQ_SPEC, CROW_SPEC, HEADS_SPEC, HEADS_SPEC, HEADS_SPEC, ANY],
        out_specs=(_pair_spec(1), CROW_SPEC, SEQ_SPEC),
        input_output_aliases={6: 0},
        scratch_shapes=QKV_SCRATCH + [pltpu.VMEM((BQ, LANES), F32), pltpu.VMEM((LP, LANES), F32),
                                      pltpu.VMEM((LP, LANES), F32), pltpu.VMEM((2, BQ, 1), F32)],
        compiler_params=_cparams("parallel", "arbitrary"))(proj, c, crow, o32, lse, do, dproj)


def _adamw_math(w, g, m, v):
    m = B1 * m + (1.0 - B1) * g
    v = B2 * v + (1.0 - B2) * (g * g)
    m_hat = m / (1.0 - B1 ** STEP)
    v_hat = v / (1.0 - B2 ** STEP)
    delta = -LR * (m_hat / (jnp.sqrt(v_hat) + EPS) + WD * w)
    return delta, m, v


def _sum_adamw(parts, w, m, v, tr, name):
    rows, cols = w.shape
    cp = parts.shape[2]
    assert rows % tr == 0 and parts.shape[1] == rows

    def body(p_ref, w_ref, m_ref, v_ref, g_ref, d_ref, nm_ref, nv_ref):
        gsum = p_ref[0].astype(F32)
        for s in range(1, N_DEV):
            gsum = gsum + p_ref[s].astype(F32)
        gsum = gsum[:, :cols]
        d, nm, nv = _adamw_math(w_ref[...], gsum, m_ref[...], v_ref[...])
        g_ref[...] = gsum
        d_ref[...] = d
        nm_ref[...] = nm
        nv_ref[...] = nv

    blk = pl.BlockSpec((tr, cols), lambda i: (i, 0))
    out = jax.ShapeDtypeStruct((rows, cols), F32)
    return pl.pallas_call(
        body, name=name, out_shape=(out, out, out, out), grid=(rows // tr,),
        in_specs=[pl.BlockSpec((N_DEV, tr, cp), lambda i: (0, i, 0)), blk, blk, blk],
        out_specs=(blk, blk, blk, blk), compiler_params=_cparams("parallel"))(parts, w, m, v)


def _local_step(x, tgt, meta, g_mix, w_in_p, b_forget, w_bsb, w_bfx, w_out, g_ffn, w_up_i, cw_i, w_down, g_final):
    nseq = x.shape[0]
    t = nseq * LP
    tm = LP // 2
    mm = functools.partial(_matmul, tm=tm)

    h0 = _pad_rows(meta, x, nseq, "pad_x").reshape(t, D)
    tgt_p = _pad_rows(jnp.zeros((N_META, D), F32), tgt, nseq, "pad_target").reshape(t, D)
    bf = jnp.pad(b_forget.reshape(1, NH), ((0, 0), (0, LANES - NH)))

    _, n1 = _norm_fwd(h0, None, g_mix, "norm1")
    proj = mm(n1, w_in_p, out_dtype=F32, tn=1792, tk=D, name="in_proj")
    c = _gate_fwd(proj, bf, nseq, "gate_fwd")
    crow = c[:, :NH].reshape(nseq, LP, NH).transpose(0, 2, 1)
    o_sb, rs = _sb_fwd(proj, nseq, "sb_fwd")
    o_fx, o_fx32, lse = _fox_fwd(proj, c, crow, nseq, "fox_fwd")
    p_sb = mm(o_sb, w_bsb, out_dtype=F32, tn=D, tk=W_ATT, name="branch_sb")
    p_fx = mm(o_fx, w_bfx, out_dtype=F32, tn=D, tk=W_ATT, name="branch_fox")
    merged = _merge_fwd(p_sb, p_fx, proj, "merge_fwd")
    mix = mm(merged, w_out, out_dtype=F32, tn=D, tk=D, name="out_proj")
    h1, n2 = _norm_fwd(h0, mix, g_ffn, "norm2")
    u = mm(n2, w_up_i, out_dtype=F32, tn=1408, tk=D, name="up_proj")
    act = _conv_glu_fwd(u, cw_i, nseq, "conv_glu_fwd")
    ffn = mm(act, w_down, out_dtype=F32, tn=D, tk=1408, name="down_proj")

    loss, dh2, dh2b, dg_final = _final_loss_bwd(h1, ffn, g_final, tgt_p, "final")
    d_down = _matmul(act, dh2b, out_dtype=BF16, tm=1408, tn=D, tk=tm, ta=True, name="d_w_down")
    dact = mm(dh2b, w_down, out_dtype=F32, tn=1408, tk=D, tb=True, name="d_act")
    du, d_cw = _conv_glu_bwd(u, cw_i, dact, nseq, "conv_glu_bwd")
    d_up = _matmul(n2, du, out_dtype=BF16, tm=D, tn=1408, tk=tm, ta=True, name="d_w_up")
    dn2 = mm(du, w_up_i, out_dtype=F32, tn=D, tk=1408, tb=True, name="d_n2")
    dh1, dh1b, dg_ffn = _norm_bwd(h1, dn2, g_ffn, dh2, True, "norm2_bwd")
    d_out = _matmul(merged, dh1b, out_dtype=BF16, tm=D, tn=D, tk=tm, ta=True, name="d_w_out")
    dmerged = mm(dh1b, w_out, out_dtype=F32, tn=D, tk=D, tb=True, name="d_merged")
    dp_sb, dproj = _merge_bwd(dmerged, p_sb, proj, None, 0, "merge_bwd_sb")
    dp_fx, dproj = _merge_bwd(dmerged, p_fx, proj, dproj, 1, "merge_bwd_fox")
    d_bsb = _matmul(o_sb, dp_sb, out_dtype=BF16, tm=W_ATT, tn=D, tk=tm, ta=True, name="d_w_branch_sb")
    d_bfx = _matmul(o_fx, dp_fx, out_dtype=BF16, tm=W_ATT, tn=D, tk=tm, ta=True, name="d_w_branch_fox")
    do_sb = mm(dp_sb, w_bsb, out_dtype=BF16, tn=W_ATT, tk=D, tb=True, name="d_o_sb")
    do_fx = mm(dp_fx, w_bfx, out_dtype=BF16, tn=W_ATT, tk=D, tb=True, name="d_o_fox")
    dproj = _sb_bwd(proj, do_sb, rs, dproj, nseq, "sb_bwd")
    dproj, dck, dcq = _fox_bwd(proj, c, crow, o_fx32, lse, do_fx, dproj, nseq, "fox_bwd")
    dc = dcq + jnp.pad(dck.transpose(0, 2, 1).reshape(t, NH), ((0, 0), (0, LANES - NH)))
    dproj, d_bf = _gate_bwd(proj, bf, dc, dproj, nseq, "gate_bwd")
    d_in = _matmul(n1, dproj, out_dtype=BF16, tm=D, tn=1792, tk=tm, ta=True, name="d_w_in")
    dn1 = mm(dproj, w_in_p, out_dtype=F32, tn=D, tk=1792, tb=True, name="d_n1")
    dh0, dg_mix = _norm_bwd(h0, dn1, g_mix, dh1, False, "norm1_bwd")
    dh0 = dh0.reshape(nseq, LP, D)
    grads = dict(meta_tokens=jnp.sum(dh0[:, :N_META], axis=0), norm_mix_g=dg_mix, w_in=d_in, b_forget=d_bf[:, :NH],
                 w_branch_sb=d_bsb, w_branch_fox=d_bfx, w_out=d_out, norm_ffn_g=dg_ffn, w_up=d_up, conv_w=d_cw,
                 w_down=d_down, norm_final_g=dg_final)
    return loss[0, 0], _real_rows(dh0, nseq, "grad_x"), grads


REPL = (("norm_mix_g", D), ("norm_ffn_g", D), ("norm_final_g", D), ("b_forget", LANES))
REPL_ROWS = 32


def _pack_repl(tree):
    rows = [jnp.pad(tree[name].reshape(-1), (0, n - tree[name].size)).reshape(-1, LANES) for name, n in REPL]
    packed = jnp.concatenate(rows, axis=0)
    return jnp.pad(packed, ((0, REPL_ROWS - packed.shape[0]), (0, 0)))


def _unpack_repl(packed, shapes):
    out, r = {}, 0
    for name, n in REPL:
        size = 1
        for s in shapes[name]:
            size *= s
        out[name] = packed[r:r + n // LANES].reshape(-1)[:size].reshape(shapes[name])
        r += n // LANES
    return out


def kernel(x, meta_tokens, norm_mix_g, w_in, b_forget, w_branch_sb, w_branch_fox, w_out, norm_ffn_g, w_up, conv_w, w_down, norm_final_g, loss_target, m_meta_tokens, m_norm_mix_g, m_w_in, m_b_forget, m_w_branch_sb, m_w_branch_fox, m_w_out, m_norm_ffn_g, m_w_up, m_conv_w, m_w_down, m_norm_final_g, v_meta_tokens, v_norm_mix_g, v_w_in, v_b_forget, v_w_branch_sb, v_w_branch_fox, v_w_out, v_norm_ffn_g, v_w_up, v_conv_w, v_w_down, v_norm_final_g):
    w = dict(meta_tokens=meta_tokens, norm_mix_g=norm_mix_g, w_in=w_in, b_forget=b_forget, w_branch_sb=w_branch_sb,
             w_branch_fox=w_branch_fox, w_out=w_out, norm_ffn_g=norm_ffn_g, w_up=w_up, conv_w=conv_w, w_down=w_down,
             norm_final_g=norm_final_g)
    m = dict(meta_tokens=m_meta_tokens, norm_mix_g=m_norm_mix_g, w_in=m_w_in, b_forget=m_b_forget,
             w_branch_sb=m_w_branch_sb, w_branch_fox=m_w_branch_fox, w_out=m_w_out, norm_ffn_g=m_norm_ffn_g,
             w_up=m_w_up, conv_w=m_conv_w, w_down=m_w_down, norm_final_g=m_norm_final_g)
    v = dict(meta_tokens=v_meta_tokens, norm_mix_g=v_norm_mix_g, w_in=v_w_in, b_forget=v_b_forget,
             w_branch_sb=v_w_branch_sb, w_branch_fox=v_w_branch_fox, w_out=v_w_out, norm_ffn_g=v_norm_ffn_g,
             w_up=v_w_up, conv_w=v_conv_w, w_down=v_w_down, norm_final_g=v_norm_final_g)
    shapes = {k: a.shape for k, a in w.items()}
    sharded = ("w_in", "w_branch_sb", "w_branch_fox", "w_out", "w_up", "w_down", "conv_w", "meta_tokens")
    mat = lambda tree, name: tree[name].reshape(tree[name].shape[-2:])

    def lane_pad(a, width):
        return jnp.pad(a, ((0, 0), (0, width - a.shape[1])))

    g_in, g_bsb, g_bfx, g_out, g_up, g_down, g_cw, g_meta = _all_gather(
        [lane_pad(mat(w, "w_in").astype(BF16), SHARD_P), mat(w, "w_branch_sb").astype(BF16),
         mat(w, "w_branch_fox").astype(BF16), mat(w, "w_out").astype(BF16), lane_pad(mat(w, "w_up").astype(BF16), SHARD_P),
         mat(w, "w_down").astype(BF16), mat(w, "conv_w"), mat(w, "meta_tokens")], "gather_weights")
    w_in_p = _relayout(g_in, 1, IN_P, _gathered_to_full(IN_SHARD, _in_padded_to_orig), BF16, 256, "w_in_cols")[0]
    w_up_i = _relayout(g_up, 1, 2 * D_FF, _gathered_to_full(UP_SHARD, _up_inter_to_orig), BF16, 256, "w_up_cols")[0]
    w_bsb = _relayout(g_bsb, 1, D, _gathered_to_full(ATT_SHARD, lambda d: d), BF16, 256, "w_bsb_cols")[0]
    w_bfx = _relayout(g_bfx, 1, D, _gathered_to_full(ATT_SHARD, lambda d: d), BF16, 256, "w_bfx_cols")[0]
    cw_full = g_cw.transpose(1, 0, 2).reshape(3, 2 * D_FF)
    cw_i = cw_full.reshape(3, 2, D_FF // FFC, FFC).transpose(0, 2, 1, 3).reshape(3, 2 * D_FF)
    meta_full = g_meta.transpose(1, 0, 2).reshape(N_META, D)

    loss, grad_x, grads = _local_step(
        x, loss_target, meta_full, norm_mix_g.reshape(1, D), w_in_p, b_forget, w_bsb, w_bfx, g_out.reshape(D, D),
        norm_ffn_g.reshape(1, D), w_up_i, cw_i, g_down.reshape(D_FF, D), norm_final_g.reshape(1, D))

    d_cw = grads["conv_w"].reshape(3, D_FF // FFC, 2, FFC).transpose(0, 2, 1, 3).reshape(3, 2 * D_FF)
    parts = _exchange(
        [_relayout(grads["w_in"][None], N_DEV, SHARD_P, _full_to_shards(IN_SHARD, _IN_ORIG_TO_PADDED.get), BF16, 256, "d_w_in_shards"),
         _relayout(grads["w_branch_sb"][None], N_DEV, ATT_SHARD, _full_to_shards(ATT_SHARD, lambda c: c), BF16, 256, "d_w_bsb_shards"),
         _relayout(grads["w_branch_fox"][None], N_DEV, ATT_SHARD, _full_to_shards(ATT_SHARD, lambda c: c), BF16, 256, "d_w_bfx_shards"),
         grads["w_out"].reshape(N_DEV, D // N_DEV, D),
         _relayout(grads["w_up"][None], N_DEV, SHARD_P, _full_to_shards(UP_SHARD, _UP_ORIG_TO_INTER.get), BF16, 256, "d_w_up_shards"),
         grads["w_down"].reshape(N_DEV, D_FF // N_DEV, D),
         d_cw.reshape(3, N_DEV, UP_SHARD).transpose(1, 0, 2),
         grads["meta_tokens"].reshape(N_META, N_DEV, ATT_SHARD).transpose(1, 0, 2)], "exchange_grads")
    tiles = dict(w_in=256, w_branch_sb=256, w_branch_fox=256, w_out=D // N_DEV, w_up=256, w_down=D_FF // N_DEV,
                 conv_w=3, meta_tokens=N_META)
    new = {name: _sum_adamw(p, mat(w, name), mat(m, name), mat(v, name), tiles[name], "adamw_" + name)
           for name, p in zip(sharded, parts)}

    rparts, = _all_gather([_pack_repl(grads)], "gather_replicated_grads")
    routs = _sum_adamw(rparts, _pack_repl(w), _pack_repl(m), _pack_repl(v), REPL_ROWS, "adamw_replicated")
    repl = [_unpack_repl(o, shapes) for o in routs]

    result = [lax.psum(loss, ("x", "y", "c")), grad_x]
    for k in range(4):
        for name in w:
            result.append(new[name][k].reshape(shapes[name]) if name in new else repl[k][name])
    return tuple(result)
```

```python
import functools

import jax
import jax.numpy as jnp
from jax import lax
from jax.experimental import pallas as pl
from jax.experimental.pallas import tpu as pltpu

F32 = jnp.float32
BF16 = jnp.bfloat16

N_DEV = 8
LANES = 128
D = 1024
N_META = 16
SEQ = 2048
L_REAL = N_META + SEQ
LP = 2304
BQ = 256
NBLK = LP // BQ
HEAD = 64
NH = 8
W_ATT = NH * HEAD
PAIR_W = 3 * LANES
D_FF = 2816
IN_COLS = 5128
QKV = 6 * W_ATT
IN_P = 5376
GATE_COL = QKV
F_COL = QKV + 2 * D
FFC = 256
RMS_EPS = 1e-6
LR, B1, B2, EPS, WD, STEP = 0.001, 0.9, 0.999, 1e-08, 0.01, 10
VMEM_LIMIT = 56 * 1024 * 1024

MESH = pl.DeviceIdType.MESH
ANY = pl.BlockSpec(memory_space=pl.ANY)


def _cparams(*sem):
    return pltpu.CompilerParams(dimension_semantics=sem if sem else None, vmem_limit_bytes=VMEM_LIMIT)


def _all_gather(xs, name):
    n = len(xs)

    def body(*refs):
        x_refs, out_refs = refs[:n], refs[n:2 * n]
        send_sems, recv_sems, local_sems = refs[2 * n:]
        mx, my, mc = lax.axis_index("x"), lax.axis_index("y"), lax.axis_index("c")
        me, sibling = (mx, my, mc), (mx, my, 1 - mc)
        chips = [(1 - mx, my), (mx, 1 - my), (1 - mx, 1 - my)]

        def copy(a, k, block, to, own=False):
            px, py, pc = block
            slot = out_refs[a].at[4 * px + 2 * py + pc]
            return pltpu.make_async_remote_copy(
                src_ref=x_refs[a] if own else slot, dst_ref=slot,
                send_sem=send_sems.at[7 * a + k], recv_sem=recv_sems.at[7 * a + k],
                device_id=to, device_id_type=MESH)

        mine = [pltpu.make_async_copy(x_refs[a], out_refs[a].at[4 * mx + 2 * my + mc], local_sems.at[a]) for a in range(n)]
        for cp in mine:
            cp.start()
        first = []
        for a in range(n):
            first.append(copy(a, 0, me, sibling, own=True))
            first += [copy(a, 1 + j, me, (*chip, mc), own=True) for j, chip in enumerate(chips)]
        for cp in first:
            cp.start()
        passed = []
        for j, chip in enumerate(chips):
            for a in range(n):
                copy(a, 1 + j, (*chip, mc), me).wait_recv()
                fwd = copy(a, 4 + j, (*chip, mc), sibling)
                fwd.start()
                passed.append(fwd)
        for a in range(n):
            copy(a, 0, sibling, me).wait_recv()
            for j, chip in enumerate(chips):
                copy(a, 4 + j, (*chip, 1 - mc), me).wait_recv()
        for cp in first + passed:
            cp.wait_send()
        for cp in mine:
            cp.wait()

    return pl.pallas_call(
        body, name=name,
        out_shape=tuple(jax.ShapeDtypeStruct((N_DEV,) + x.shape, x.dtype) for x in xs),
        in_specs=[ANY] * n, out_specs=tuple([ANY] * n),
        scratch_shapes=[pltpu.SemaphoreType.DMA((7 * n,)), pltpu.SemaphoreType.DMA((7 * n,)),
                        pltpu.SemaphoreType.DMA((n,))],
    )(*xs)


def _exchange(srcs, name):
    n = len(srcs)

    def body(*refs):
        src_refs, dst_refs = refs[:n], refs[n:2 * n]
        send_sems, recv_sems, local_sems = refs[2 * n:]
        mx, my, mc = lax.axis_index("x"), lax.axis_index("y"), lax.axis_index("c")
        me_idx = 4 * mx + 2 * my + mc
        mine = [pltpu.make_async_copy(src_refs[a].at[me_idx], dst_refs[a].at[me_idx], local_sems.at[a]) for a in range(n)]
        for cp in mine:
            cp.start()
        copies = []
        for k in range(1, N_DEV):
            px, py, pc = mx ^ (k >> 2), my ^ ((k >> 1) & 1), mc ^ (k & 1)
            for a in range(n):
                copies.append(pltpu.make_async_remote_copy(
                    src_ref=src_refs[a].at[4 * px + 2 * py + pc], dst_ref=dst_refs[a].at[me_idx],
                    send_sem=send_sems.at[7 * a + k - 1], recv_sem=recv_sems.at[7 * a + k - 1],
                    device_id=(px, py, pc), device_id_type=MESH))
        for cp in copies:
            cp.start()
        for cp in copies:
            cp.wait_recv()
        for cp in copies:
            cp.wait_send()
        for cp in mine:
            cp.wait()

    return pl.pallas_call(
        body, name=name,
        out_shape=tuple(jax.ShapeDtypeStruct(s.shape, s.dtype) for s in srcs),
        in_specs=[ANY] * n, out_specs=tuple([ANY] * n),
        scratch_shapes=[pltpu.SemaphoreType.DMA((7 * n,)), pltpu.SemaphoreType.DMA((7 * n,)),
                        pltpu.SemaphoreType.DMA((n,))],
    )(*srcs)


ROWS_PER_COPY = 256


def _pad_rows(front, body_rows, nseq, name):
    tail = LP - L_REAL
    nblk = SEQ // ROWS_PER_COPY

    def body(f_ref, b_ref, o_ref, z_ref, sems):
        s, i = pl.program_id(0), pl.program_id(1)
        rows = pltpu.make_async_copy(b_ref, o_ref.at[pl.ds(s, 1), pl.ds(N_META + i * ROWS_PER_COPY, ROWS_PER_COPY)], sems.at[0])
        rows.start()

        @pl.when(i == 0)
        def _():
            z_ref[...] = jnp.zeros_like(z_ref)
            head = pltpu.make_async_copy(f_ref, o_ref.at[s, pl.ds(0, N_META)], sems.at[1])
            zeros = pltpu.make_async_copy(z_ref, o_ref.at[s, pl.ds(L_REAL, tail)], sems.at[2])
            head.start()
            zeros.start()
            head.wait()
            zeros.wait()

        rows.wait()

    return pl.pallas_call(
        body, name=name, out_shape=jax.ShapeDtypeStruct((nseq, LP, D), F32), grid=(nseq, nblk),
        in_specs=[pl.BlockSpec((N_META, D), lambda s, i: (0, 0)), pl.BlockSpec((1, ROWS_PER_COPY, D), lambda s, i: (s, i, 0))],
        out_specs=ANY,
        scratch_shapes=[pltpu.VMEM((tail, D), F32), pltpu.SemaphoreType.DMA((3,))],
        compiler_params=_cparams("arbitrary", "arbitrary"))(front, body_rows)


def _real_rows(h, nseq, name):
    nblk = SEQ // ROWS_PER_COPY

    def body(h_ref, o_ref, sem):
        s, i = pl.program_id(0), pl.program_id(1)
        rows = pltpu.make_async_copy(h_ref.at[pl.ds(s, 1), pl.ds(N_META + i * ROWS_PER_COPY, ROWS_PER_COPY)], o_ref, sem)
        rows.start()
        rows.wait()

    return pl.pallas_call(
        body, name=name, out_shape=jax.ShapeDtypeStruct((nseq, SEQ, D), F32), grid=(nseq, nblk),
        in_specs=[ANY], out_specs=pl.BlockSpec((1, ROWS_PER_COPY, D), lambda s, i: (s, i, 0)),
        scratch_shapes=[pltpu.SemaphoreType.DMA],
        compiler_params=_cparams("arbitrary", "arbitrary"))(h)


def _plan_cols(n_q, n_dcols, src_of):
    plan = {}
    for q in range(n_q):
        for dblk in range(n_dcols // LANES):
            segs, key, start = [], None, 0
            for lane in range(LANES + 1):
                new = None
                if lane < LANES:
                    src = src_of(q, dblk * LANES + lane)
                    if src is not None:
                        new = (src[0], src[1] // LANES, (lane - src[1] % LANES) % LANES)
                if new != key:
                    if key is not None:
                        segs.append((*key, start, lane))
                    key, start = new, lane
            plan[(q, dblk)] = segs
    return plan


def _relayout(src, n_q, n_dcols, src_of, out_dtype, tr, name):
    n_p, rows, scols = src.shape
    plan = _plan_cols(n_q, n_dcols, src_of)

    def body(s_ref, d_ref):
        lane = lax.broadcasted_iota(jnp.int32, (tr, LANES), 1)
        for (q, dblk), segs in plan.items():
            acc = jnp.zeros((tr, LANES), F32)
            for p, sblk, rot, lo, hi in segs:
                x = s_ref[p, :, sblk * LANES:(sblk + 1) * LANES].astype(F32)
                if rot:
                    x = pltpu.roll(x, rot, 1)
                acc = x if (lo, hi) == (0, LANES) else jnp.where((lane >= lo) & (lane < hi), x, acc)
            d_ref[q, :, dblk * LANES:(dblk + 1) * LANES] = acc.astype(out_dtype)

    return pl.pallas_call(
        body, name=name, out_shape=jax.ShapeDtypeStruct((n_q, rows, n_dcols), out_dtype), grid=(rows // tr,),
        in_specs=[pl.BlockSpec((n_p, tr, scols), lambda i: (0, i, 0))],
        out_specs=pl.BlockSpec((n_q, tr, n_dcols), lambda i: (0, i, 0)),
        compiler_params=_cparams("parallel"))(src)


def _in_padded_to_orig(d):
    if d < QKV:
        kind, r = divmod(d, 4 * PAIR_W)
        pair, r = divmod(r, PAIR_W)
        part, r = divmod(r, LANES)
        return kind * 3 * W_ATT + part * W_ATT + pair * LANES + r
    if d < F_COL:
        return d + NH
    if d < F_COL + NH:
        return d - 2 * D
    return None


_IN_ORIG_TO_PADDED = {_in_padded_to_orig(d): d for d in range(IN_P) if _in_padded_to_orig(d) is not None}


def _up_inter_to_orig(d):
    j, r = divmod(d, 2 * FFC)
    part, r = divmod(r, FFC)
    return part * D_FF + j * FFC + r


_UP_ORIG_TO_INTER = {_up_inter_to_orig(d): d for d in range(2 * D_FF)}
IN_SHARD = IN_COLS // N_DEV
UP_SHARD = 2 * D_FF // N_DEV
SHARD_P = 768
ATT_SHARD = D // N_DEV


def _gathered_to_full(n_shard, to_orig):
    def src_of(q, d):
        c = to_orig(d)
        return None if c is None else (c // n_shard, c % n_shard)
    return src_of


def _full_to_shards(n_shard, from_orig):
    def src_of(q, d):
        return (0, from_orig(q * n_shard + d)) if d < n_shard else None
    return src_of


def _matmul(a, b, *, out_dtype, tm, tn, tk, ta=False, tb=False, name):
    if ta:
        kdim, m = a.shape
    else:
        m, kdim = a.shape
    n = b.shape[0] if tb else b.shape[1]
    assert m % tm == 0 and n % tn == 0 and kdim % tk == 0, (name, a.shape, b.shape, tm, tn, tk)
    nk = kdim // tk

    def body(a_ref, b_ref, o_ref, *scratch):
        av, bv = a_ref[...], b_ref[...]
        if ta:
            p = lax.dot_general(av, bv, (((0,), (0,)), ((), ())), preferred_element_type=F32)
        elif tb:
            p = lax.dot_general(av, bv, (((1,), (1,)), ((), ())), preferred_element_type=F32)
        else:
            p = jnp.dot(av, bv, preferred_element_type=F32)
        if nk == 1:
            o_ref[...] = p.astype(o_ref.dtype)
        else:
            acc_ref, = scratch
            k = pl.program_id(2)

            @pl.when(k == 0)
            def _():
                acc_ref[...] = p

            @pl.when(k > 0)
            def _():
                acc_ref[...] += p

            @pl.when(k == nk - 1)
            def _():
                o_ref[...] = acc_ref[...].astype(o_ref.dtype)

    a_spec = pl.BlockSpec((tk, tm), lambda i, j, k: (k, i)) if ta else pl.BlockSpec((tm, tk), lambda i, j, k: (i, k))
    b_spec = pl.BlockSpec((tn, tk), lambda i, j, k: (j, k)) if tb else pl.BlockSpec((tk, tn), lambda i, j, k: (k, j))
    return pl.pallas_call(
        body, name=name,
        out_shape=jax.ShapeDtypeStruct((m, n), out_dtype),
        grid=(m // tm, n // tn, nk),
        in_specs=[a_spec, b_spec],
        out_specs=pl.BlockSpec((tm, tn), lambda i, j, k: (i, j)),
        scratch_shapes=[] if nk == 1 else [pltpu.VMEM((tm, tn), F32)],
        compiler_params=_cparams("parallel", "parallel", "arbitrary"),
    )(a, b)


TR = 288


def _rms(h):
    return lax.rsqrt(jnp.mean(h * h, axis=-1, keepdims=True) + RMS_EPS)


def _norm_fwd(h, delta, g, name):
    t = h.shape[0]
    row = pl.BlockSpec((TR, D), lambda i: (i, 0))
    vec = pl.BlockSpec((1, D), lambda i: (0, 0))

    if delta is None:
        def body(h_ref, g_ref, n_ref):
            hv = h_ref[...]
            n_ref[...] = ((hv * _rms(hv)) * g_ref[...]).astype(BF16)

        n = pl.pallas_call(
            body, name=name, out_shape=jax.ShapeDtypeStruct((t, D), BF16), grid=(t // TR,),
            in_specs=[row, vec], out_specs=row, compiler_params=_cparams("parallel"))(h, g)
        return h, n

    def body(h_ref, d_ref, g_ref, hn_ref, n_ref):
        hv = h_ref[...] + d_ref[...]
        hn_ref[...] = hv
        n_ref[...] = ((hv * _rms(hv)) * g_ref[...]).astype(BF16)

    return pl.pallas_call(
        body, name=name,
        out_shape=(jax.ShapeDtypeStruct((t, D), F32), jax.ShapeDtypeStruct((t, D), BF16)), grid=(t // TR,),
        in_specs=[row, row, vec], out_specs=(row, row), compiler_params=_cparams("parallel"))(h, delta, g)


def _rms_bwd_math(hv, dn, gv):
    r = _rms(hv)
    hr = hv * r
    dng = dn * gv
    dh = r * (dng - hr * jnp.mean(dng * hr, axis=-1, keepdims=True))
    return dh, dn * hr


def _final_loss_bwd(h1, delta, g, tgt, name):
    t = h1.shape[0]
    row = pl.BlockSpec((TR, D), lambda i: (i, 0))
    vec = pl.BlockSpec((1, D), lambda i: (0, 0))
    tiles_per_seq = LP // TR

    def body(h_ref, d_ref, g_ref, t_ref, loss_ref, dh_ref, dhb_ref, dg_ref):
        i = pl.program_id(0)
        hv = h_ref[...] + d_ref[...]
        gv = g_ref[...]
        r = _rms(hv)
        hr = hv * r
        y = hr * gv
        pos = (i % tiles_per_seq) * TR + lax.broadcasted_iota(jnp.int32, (TR, 1), 0)
        valid = (pos >= N_META) & (pos < L_REAL)
        err = jnp.where(valid, y - t_ref[...], 0.0)
        part = 0.5 * jnp.sum(jnp.mean(err * err, axis=-1, keepdims=True))
        dy = err * (1.0 / D)
        dng = dy * gv
        dh = r * (dng - hr * jnp.mean(dng * hr, axis=-1, keepdims=True))
        dh_ref[...] = dh
        dhb_ref[...] = dh.astype(BF16)
        dgp = jnp.sum(dy * hr, axis=0, keepdims=True)

        @pl.when(i == 0)
        def _():
            loss_ref[...] = jnp.zeros_like(loss_ref)
            dg_ref[...] = jnp.zeros_like(dg_ref)

        loss_ref[...] += part
        dg_ref[...] += dgp

    return pl.pallas_call(
        body, name=name,
        out_shape=(jax.ShapeDtypeStruct((8, 128), F32), jax.ShapeDtypeStruct((t, D), F32),
                   jax.ShapeDtypeStruct((t, D), BF16), jax.ShapeDtypeStruct((1, D), F32)),
        grid=(t // TR,),
        in_specs=[row, row, vec, row],
        out_specs=(pl.BlockSpec((8, 128), lambda i: (0, 0)), row, row, vec),
        compiler_params=_cparams("arbitrary"))(h1, delta, g, tgt)


def _norm_bwd(h, dn, g, dres, with_bf16, name):
    t = h.shape[0]
    row = pl.BlockSpec((TR, D), lambda i: (i, 0))
    vec = pl.BlockSpec((1, D), lambda i: (0, 0))

    def body(h_ref, dn_ref, g_ref, dres_ref, *outs):
        i = pl.program_id(0)
        dh, dgrow = _rms_bwd_math(h_ref[...], dn_ref[...], g_ref[...])
        dh = dh + dres_ref[...]
        outs[0][...] = dh
        if with_bf16:
            outs[1][...] = dh.astype(BF16)
        dg_ref = outs[-1]

        @pl.when(i == 0)
        def _():
            dg_ref[...] = jnp.zeros_like(dg_ref)

        dg_ref[...] += jnp.sum(dgrow, axis=0, keepdims=True)

    shapes = [jax.ShapeDtypeStruct((t, D), F32)]
    specs = [row]
    if with_bf16:
        shapes.append(jax.ShapeDtypeStruct((t, D), BF16))
        specs.append(row)
    shapes.append(jax.ShapeDtypeStruct((1, D), F32))
    specs.append(vec)
    return pl.pallas_call(
        body, name=name, out_shape=tuple(shapes), grid=(t // TR,),
        in_specs=[row, row, vec, row], out_specs=tuple(specs),
        compiler_params=_cparams("arbitrary"))(h, dn, g, dres)


GATE_BLK = GATE_COL // D


def _sigmoid(x):
    return 1.0 / (1.0 + jnp.exp(-x))


def _merge_fwd(p_sb, p_fx, proj, name):
    t = p_sb.shape[0]
    row = pl.BlockSpec((TR, D), lambda i: (i, 0))

    def body(ps_ref, pf_ref, gs_ref, gf_ref, o_ref):
        o_ref[...] = (_sigmoid(gs_ref[...]) * ps_ref[...] + _sigmoid(gf_ref[...]) * pf_ref[...]).astype(BF16)

    return pl.pallas_call(
        body, name=name, out_shape=jax.ShapeDtypeStruct((t, D), BF16), grid=(t // TR,),
        in_specs=[row, row, pl.BlockSpec((TR, D), lambda i: (i, GATE_BLK)),
                  pl.BlockSpec((TR, D), lambda i: (i, GATE_BLK + 1))],
        out_specs=row, compiler_params=_cparams("parallel"))(p_sb, p_fx, proj, proj)


def _merge_bwd(dm, p, proj, dproj, which, name):
    t = dm.shape[0]
    row = pl.BlockSpec((TR, D), lambda i: (i, 0))
    gate = pl.BlockSpec((TR, D), lambda i: (i, GATE_BLK + which))

    def body(dm_ref, p_ref, g_ref, *rest):
        dp_ref, dg_ref = rest[-2:]
        dmv = dm_ref[...]
        s = _sigmoid(g_ref[...])
        dp_ref[...] = (dmv * s).astype(BF16)
        dg_ref[...] = (dmv * p_ref[...] * s * (1.0 - s)).astype(BF16)

    out_shape = (jax.ShapeDtypeStruct((t, D), BF16), jax.ShapeDtypeStruct((t, IN_P), BF16))
    if dproj is None:
        return pl.pallas_call(
            body, name=name, out_shape=out_shape, grid=(t // TR,), in_specs=[row, row, gate],
            out_specs=(row, gate), compiler_params=_cparams("parallel"))(dm, p, proj)
    return pl.pallas_call(
        body, name=name, out_shape=out_shape, grid=(t // TR,), in_specs=[row, row, gate, ANY],
        out_specs=(row, gate), input_output_aliases={3: 1}, compiler_params=_cparams("parallel"))(dm, p, proj, dproj)


CH = 288


def _chunk(c, n=CH):
    return pl.ds(pl.multiple_of(c * CH, 8), n)


def _conv_taps(u_ref, c):
    x = u_ref[_chunk(c), :]
    prev = u_ref[pl.ds(pl.multiple_of(jnp.maximum(c * CH - 8, 0), 8), 8), :]
    xx = jnp.concatenate([jnp.where(c == 0, 0.0, prev), x], axis=0)
    return x, pltpu.roll(xx, 1, 0)[8:], pltpu.roll(xx, 2, 0)[8:]


def _conv_glu_fwd(u, cw, nseq, name):
    nblk = D_FF // FFC

    def body(u_ref, cw_ref, o_ref):
        cwv = cw_ref[...]

        def step(c, _):
            x, x1, x2 = _conv_taps(u_ref, c)
            uc = cwv[0:1, :] * x2 + cwv[1:2, :] * x1 + cwv[2:3, :] * x
            a, b = uc[:, :FFC], uc[:, FFC:]
            o_ref[_chunk(c), :] = (a * _sigmoid(a) * b).astype(BF16)
            return 0

        lax.fori_loop(0, LP // CH, step, 0)

    return pl.pallas_call(
        body, name=name, out_shape=jax.ShapeDtypeStruct((nseq * LP, D_FF), BF16), grid=(nseq, nblk),
        in_specs=[pl.BlockSpec((LP, 2 * FFC), lambda s, j: (s, j)), pl.BlockSpec((3, 2 * FFC), lambda s, j: (0, j))],
        out_specs=pl.BlockSpec((LP, FFC), lambda s, j: (s, j)),
        compiler_params=_cparams("parallel", "parallel"))(u, cw)


def _conv_glu_bwd(u, cw, dact, nseq, name):
    nblk = D_FF // FFC
    nch = LP // CH

    def body(u_ref, cw_ref, da_ref, du_ref, dcw_ref):
        s = pl.program_id(1)
        cwv = cw_ref[...]

        def step(k, carry):
            nxt, p0, p1, p2 = carry
            c = nch - 1 - k
            x, x1, x2 = _conv_taps(u_ref, c)
            uc = cwv[0:1, :] * x2 + cwv[1:2, :] * x1 + cwv[2:3, :] * x
            a, b = uc[:, :FFC], uc[:, FFC:]
            sa = _sigmoid(a)
            dactv = da_ref[_chunk(c), :]
            da = dactv * b * (sa * (1.0 + a * (1.0 - sa)))
            db = dactv * (a * sa)
            duc = jnp.concatenate([da, db], axis=1)
            dd = jnp.concatenate([duc, nxt], axis=0)
            du = (cwv[2:3, :] * duc + cwv[1:2, :] * pltpu.roll(dd, CH + 7, 0)[:CH]
                  + cwv[0:1, :] * pltpu.roll(dd, CH + 6, 0)[:CH])
            du_ref[_chunk(c), :] = du.astype(BF16)
            return (duc[:8], p0 + jnp.sum(duc * x2, axis=0, keepdims=True),
                    p1 + jnp.sum(duc * x1, axis=0, keepdims=True), p2 + jnp.sum(duc * x, axis=0, keepdims=True))

        zrow = jnp.zeros((1, 2 * FFC), F32)
        _, p0, p1, p2 = lax.fori_loop(0, nch, step, (jnp.zeros((8, 2 * FFC), F32), zrow, zrow, zrow))

        @pl.when(s == 0)
        def _():
            dcw_ref[...] = jnp.zeros_like(dcw_ref)

        dcw_ref[...] += jnp.concatenate([p0, p1, p2], axis=0)

    return pl.pallas_call(
        body, name=name,
        out_shape=(jax.ShapeDtypeStruct((nseq * LP, 2 * D_FF), BF16), jax.ShapeDtypeStruct((3, 2 * D_FF), F32)),
        grid=(nblk, nseq),
        in_specs=[pl.BlockSpec((LP, 2 * FFC), lambda j, s: (s, j)), pl.BlockSpec((3, 2 * FFC), lambda j, s: (0, j)),
                  pl.BlockSpec((LP, FFC), lambda j, s: (s, j))],
        out_specs=(pl.BlockSpec((LP, 2 * FFC), lambda j, s: (s, j)), pl.BlockSpec((3, 2 * FFC), lambda j, s: (0, j))),
        compiler_params=_cparams("parallel", "arbitrary"))(u, cw, dact)


F_BLK = F_COL // LANES
CB = 128


def _split3(x):
    hi = x.astype(BF16)
    r1 = x - hi.astype(F32)
    mid = r1.astype(BF16)
    lo = (r1 - mid.astype(F32)).astype(BF16)
    return hi, mid, lo


def _tri_dot(tri, x):
    hi, mid, lo = _split3(x)
    d = functools.partial(jnp.dot, preferred_element_type=F32)
    return d(tri, hi) + d(tri, mid) + d(tri, lo)


def _log_sigmoid(x):
    return jnp.minimum(x, 0.0) - jnp.log(1.0 + jnp.exp(-jnp.abs(x)))


def _gate_fwd(proj, bf, nseq, name):
    def body(f_ref, b_ref, c_ref):
        r_i = lax.broadcasted_iota(jnp.int32, (CB, CB), 0)
        c_i = lax.broadcasted_iota(jnp.int32, (CB, CB), 1)
        tri = (c_i <= r_i).astype(BF16)
        bv = b_ref[...]

        def step(k, carry):
            rows = pl.ds(pl.multiple_of(k * CB, CB), CB)
            lf = _log_sigmoid(f_ref[rows, :] + bv)
            c_ref[rows, :] = _tri_dot(tri, lf) + carry
            return carry + jnp.sum(lf, axis=0, keepdims=True)

        lax.fori_loop(0, LP // CB, step, jnp.zeros((1, LANES), F32))

    return pl.pallas_call(
        body, name=name, out_shape=jax.ShapeDtypeStruct((nseq * LP, LANES), F32), grid=(nseq,),
        in_specs=[pl.BlockSpec((LP, LANES), lambda s: (s, F_BLK)), pl.BlockSpec((1, LANES), lambda s: (0, 0))],
        out_specs=pl.BlockSpec((LP, LANES), lambda s: (s, 0)),
        compiler_params=_cparams("parallel"))(proj, bf)


def _gate_bwd(proj, bf, dc, dproj, nseq, name):
    def body(f_ref, b_ref, dc_ref, _, df_ref, db_ref):
        s = pl.program_id(0)
        r_i = lax.broadcasted_iota(jnp.int32, (CB, CB), 0)
        c_i = lax.broadcasted_iota(jnp.int32, (CB, CB), 1)
        tri = (c_i >= r_i).astype(BF16)
        bv = b_ref[...]

        def step(kk, carry):
            carry_c, carry_b = carry
            k = LP // CB - 1 - kk
            rows = pl.ds(pl.multiple_of(k * CB, CB), CB)
            dcv = dc_ref[rows, :]
            dlf = _tri_dot(tri, dcv) + carry_c
            df = dlf * _sigmoid(-(f_ref[rows, :] + bv))
            df_ref[rows, :] = jnp.concatenate([df, jnp.zeros_like(df)], axis=1).astype(BF16)
            return carry_c + jnp.sum(dcv, axis=0, keepdims=True), carry_b + jnp.sum(df, axis=0, keepdims=True)

        zero = jnp.zeros((1, LANES), F32)
        _, dbp = lax.fori_loop(0, LP // CB, step, (zero, zero))

        @pl.when(s == 0)
        def _():
            db_ref[...] = jnp.zeros_like(db_ref)

        db_ref[...] += dbp

    return pl.pallas_call(
        body, name=name,
        out_shape=(jax.ShapeDtypeStruct(dproj.shape, BF16), jax.ShapeDtypeStruct((1, LANES), F32)), grid=(nseq,),
        in_specs=[pl.BlockSpec((LP, LANES), lambda s: (s, F_BLK)), pl.BlockSpec((1, LANES), lambda s: (0, 0)),
                  pl.BlockSpec((LP, LANES), lambda s: (s, 0)), ANY],
        out_specs=(pl.BlockSpec((LP, 2 * LANES), lambda s: (s, F_COL // (2 * LANES))), pl.BlockSpec((1, LANES), lambda s: (0, 0))),
        input_output_aliases={3: 0},
        compiler_params=_cparams("arbitrary"))(proj, bf, dc, dproj)


SCALE = 0.125
NEG = -1e30


def _dot_nt(a, b):
    return lax.dot_general(a, b, (((1,), (1,)), ((), ())), preferred_element_type=F32)


def _dot_tn(a, b):
    return lax.dot_general(a, b, (((0,), (0,)), ((), ())), preferred_element_type=F32)


def _dot(a, b):
    return jnp.dot(a, b, preferred_element_type=F32)


def _blk(i):
    return pl.ds(pl.multiple_of(i * BQ, BQ), BQ)


def _tile_iotas():
    return lax.broadcasted_iota(jnp.int32, (BQ, BQ), 0), lax.broadcasted_iota(jnp.int32, (BQ, BQ), 1)


def _lane_iota():
    return lax.broadcasted_iota(jnp.int32, (BQ, LANES), 1)


def _head_masks():
    lane = _lane_iota()
    return lane < HEAD, lane >= HEAD


def _only(mask, x):
    return jnp.where(mask, x, jnp.zeros_like(x))


def _pick_lane(x, idx):
    return jnp.sum(jnp.where(_lane_iota() == idx, x, 0.0), axis=1, keepdims=True)


def _chains(npair):
    return [(pp, h) for pp in range(npair) for h in range(2)]


def _load_qkv(p_ref, q_s, k_s, v_s):
    for pp in range(q_s.shape[0]):
        base = pp * PAIR_W
        q_s[pp] = (p_ref[:, base:base + LANES] * SCALE).astype(BF16)
        k_s[pp] = p_ref[:, base + LANES:base + 2 * LANES].astype(BF16)
        v_s[pp] = p_ref[:, base + 2 * LANES:base + 3 * LANES].astype(BF16)


def _softplus(z):
    return jnp.maximum(z, 0.0) + jnp.log(1.0 + jnp.exp(-jnp.abs(z)))


def _hi_lo(x):
    hi = x.astype(BF16)
    return hi, (x - hi.astype(F32)).astype(BF16)


def _sb_tile_weights(q, k, strict, r, u_suf):
    n = len(q)
    z = [_dot_nt(q[c], k[c]) for c in range(n)]
    sp = [_softplus(zc) for zc in z]
    lk = [-spc if strict is None else jnp.where(strict, -spc, 0.0) for spc in sp]
    parts = [_hi_lo(lkc) for lkc in lk]
    suf = [_dot(hi, u_suf) + _dot(lo, u_suf) for hi, lo in parts]
    w = [jnp.exp(z[c] - sp[c] + r[c] + suf[c]) for c in range(n)]
    if strict is not None:
        w = [jnp.where(strict, wc, 0.0) for wc in w]
    r_next = [r[c] + suf[c][:, 0:1] + lk[c][:, 0:1] for c in range(n)]
    return w, sp, r_next


def _group_spec(kind, npair):
    return pl.BlockSpec((LP, npair * PAIR_W), lambda s, g: (s, (NH // (2 * npair)) * kind + g))


def _gheads_spec(npair):
    return pl.BlockSpec((LP, npair * LANES), lambda s, g: (s, g))


def _qkv_scratch(npair):
    return [pltpu.VMEM((npair, LP, LANES), BF16)] * 3


SEQ_SPEC = pl.BlockSpec((LP, LANES), lambda s, g: (s, 0))
RS_STRIDE = 16


def _pair_cols(pp):
    return slice(pp * LANES, (pp + 1) * LANES)


def _sb_fwd(proj, nseq, npair, name):
    t = nseq * LP
    chains = _chains(npair)

    def body(p_ref, o_ref, rs_ref, q_s, k_s, v_s, acc_ref, r_ref, rb_ref):
        _load_qkv(p_ref, q_s, k_s, v_s)
        row, col = _tile_iotas()
        u_suf = (row > col).astype(BF16)
        diag = col < row
        lane = _lane_iota()
        heads = _head_masks()

        def qblock(i, _):
            acc_ref[...] = jnp.zeros_like(acc_ref)
            rb_ref[...] = jnp.zeros_like(rb_ref)
            r_ref[...] = jnp.zeros_like(r_ref)
            qb = [q_s[pp, _blk(i), :] for pp in range(npair)]

            def tile(j, strict):
                kj = [k_s[pp, _blk(j), :] for pp in range(npair)]
                vj = [v_s[pp, _blk(j), :] for pp in range(npair)]
                r = [r_ref[c] for c in range(len(chains))]
                w, _, r_next = _sb_tile_weights([_only(heads[h], qb[pp]) for pp, h in chains],
                                                [kj[pp] for pp, _ in chains], strict, r, u_suf)
                pv = [_dot(w[c].astype(BF16), _only(heads[h], vj[pp])) for c, (pp, h) in enumerate(chains)]
                for pp in range(npair):
                    acc_ref[pp] += pv[2 * pp] + pv[2 * pp + 1]
                    rb_ref[pp] = jnp.where(lane == j, r[2 * pp], jnp.where(lane == RS_STRIDE + j, r[2 * pp + 1], rb_ref[pp]))
                for c in range(len(chains)):
                    r_ref[c] = r_next[c]

            tile(i, diag)

            def kblock(jj, _):
                tile(i - jj, None)
                return 0

            lax.fori_loop(1, i + 1, kblock, 0)
            for pp in range(npair):
                o_ref[_blk(i), _pair_cols(pp)] = acc_ref[pp].astype(BF16)
                rs_ref[_blk(i), _pair_cols(pp)] = rb_ref[pp]
            return 0

        lax.fori_loop(0, NBLK, qblock, 0)

    return pl.pallas_call(
        body, name=name,
        out_shape=(jax.ShapeDtypeStruct((t, W_ATT), BF16), jax.ShapeDtypeStruct((t, W_ATT), F32)),
        grid=(nseq, NH // (2 * npair)), in_specs=[_group_spec(0, npair)], out_specs=(_gheads_spec(npair), _gheads_spec(npair)),
        scratch_shapes=_qkv_scratch(npair) + [pltpu.VMEM((npair, BQ, LANES), F32), pltpu.VMEM((2 * npair, BQ, 1), F32),
                                      pltpu.VMEM((npair, BQ, LANES), F32)],
        compiler_params=_cparams("parallel", "parallel"))(proj)


def _sb_bwd(proj, do, rs, dproj, nseq, npair, name):
    chains = _chains(npair)

    def body(p_ref, do_ref, rs_ref, _, dp_ref, q_s, k_s, v_s, dqa_ref, dka_ref, dva_ref, ep_ref):
        _load_qkv(p_ref, q_s, k_s, v_s)
        row, col = _tile_iotas()
        u_suf = (row > col).astype(BF16)
        u_pre = (row < col).astype(BF16)
        diag = col < row
        heads = _head_masks()
        dka_ref[...] = jnp.zeros_like(dka_ref)
        dva_ref[...] = jnp.zeros_like(dva_ref)
        nc = len(chains)

        def qblock(i, _):
            rb = [rs_ref[_blk(i), _pair_cols(pp)] for pp in range(npair)]
            qb = [q_s[pp, _blk(i), :] for pp in range(npair)]
            dob = [do_ref[_blk(i), _pair_cols(pp)] for pp in range(npair)]
            dqa_ref[...] = jnp.zeros_like(dqa_ref)
            ep_ref[...] = jnp.zeros_like(ep_ref)

            def tile(j, strict):
                kj = [k_s[pp, _blk(j), :] for pp in range(npair)]
                vj = [v_s[pp, _blk(j), :] for pp in range(npair)]
                q = [_only(heads[h], qb[pp]) for pp, h in chains]
                dov = [_only(heads[h], dob[pp]) for pp, h in chains]
                r = [_pick_lane(rb[pp], RS_STRIDE * h + j) for pp, h in chains]
                dw = [_dot_nt(dov[c], vj[pp]) for c, (pp, _) in enumerate(chains)]
                w, sp, _ = _sb_tile_weights(q, [kj[pp] for pp, _ in chains], strict, r, u_suf)
                e = [dw[c] * w[c] for c in range(nc)]
                e_pre = [ep_ref[c] + _dot(e[c].astype(BF16), u_pre) for c in range(nc)]
                dz = []
                for c in range(nc):
                    ep_ref[c] += jnp.sum(e[c], axis=1, keepdims=True)
                    sneg = jnp.exp(-sp[c])
                    dzc = e[c] * sneg - (1.0 - sneg) * e_pre[c]
                    if strict is not None:
                        dzc = jnp.where(strict, dzc, 0.0)
                    dz.append(dzc.astype(BF16))
                dq = [_dot(dz[c], _only(heads[h], kj[pp])) for c, (pp, h) in enumerate(chains)]
                dk = [_dot_tn(dz[c], q[c]) for c in range(nc)]
                dv = [_dot_tn(w[c].astype(BF16), dov[c]) for c in range(nc)]
                for pp in range(npair):
                    dqa_ref[pp] += dq[2 * pp] + dq[2 * pp + 1]
                    dka_ref[pp, _blk(j), :] += dk[2 * pp] + dk[2 * pp + 1]
                    dva_ref[pp, _blk(j), :] += dv[2 * pp] + dv[2 * pp + 1]

            def kblock(j, _):
                tile(j, None)
                return 0

            lax.fori_loop(0, i, kblock, 0)
            tile(i, diag)
            for pp in range(npair):
                dp_ref[_blk(i), pp * PAIR_W:pp * PAIR_W + LANES] = (dqa_ref[pp] * SCALE).astype(BF16)
            return 0

        lax.fori_loop(0, NBLK, qblock, 0)
        for pp in range(npair):
            dp_ref[:, pp * PAIR_W + LANES:pp * PAIR_W + 2 * LANES] = dka_ref[pp].astype(BF16)
            dp_ref[:, pp * PAIR_W + 2 * LANES:pp * PAIR_W + 3 * LANES] = dva_ref[pp].astype(BF16)

    return pl.pallas_call(
        body, name=name, out_shape=jax.ShapeDtypeStruct(dproj.shape, BF16), grid=(nseq, NH // (2 * npair)),
        in_specs=[_group_spec(0, npair), _gheads_spec(npair), _gheads_spec(npair), ANY], out_specs=_group_spec(0, npair),
        input_output_aliases={3: 0},
        scratch_shapes=_qkv_scratch(npair) + [pltpu.VMEM((npair, BQ, LANES), F32), pltpu.VMEM((npair, LP, LANES), F32),
                                      pltpu.VMEM((npair, LP, LANES), F32), pltpu.VMEM((2 * npair, BQ, 1), F32)],
        compiler_params=_cparams("parallel", "parallel"))(proj, do, rs, dproj)


CROW_SPEC = pl.BlockSpec((None, NH, LP), lambda s, g: (s, 0, 0))


def _fox_scores(qi, kj, cq, ck, causal):
    z = _dot_nt(qi, kj) + (cq - ck)
    return z if causal is None else jnp.where(causal, z, NEG)


def _key_cols(cr_ref, head, j):
    return cr_ref[pl.ds(head, 1), pl.ds(pl.multiple_of(j * BQ, BQ), BQ)]


def _fox_fwd(proj, c, crow, nseq, npair, name):
    t = nseq * LP
    chains = _chains(npair)

    def body(p_ref, c_ref, cr_ref, o_ref, o32_ref, lse_ref, q_s, k_s, v_s, acc_ref, m_ref, l_ref):
        _load_qkv(p_ref, q_s, k_s, v_s)
        row, col = _tile_iotas()
        diag = col <= row
        lane = _lane_iota()
        heads = _head_masks()
        head0 = 2 * npair * pl.program_id(1)
        nc = len(chains)

        def qblock(i, _):
            cblk = c_ref[_blk(i), :]
            qb = [q_s[pp, _blk(i), :] for pp in range(npair)]
            cq = [_pick_lane(cblk, head0 + c) for c in range(nc)]
            acc_ref[...] = jnp.zeros_like(acc_ref)
            m_ref[...] = jnp.full_like(m_ref, NEG)
            l_ref[...] = jnp.zeros_like(l_ref)

            def tile(j, causal):
                kj = [k_s[pp, _blk(j), :] for pp in range(npair)]
                vj = [v_s[pp, _blk(j), :] for pp in range(npair)]
                z = [_fox_scores(_only(heads[h], qb[pp]), kj[pp], cq[c], _key_cols(cr_ref, head0 + c, j), causal)
                     for c, (pp, h) in enumerate(chains)]
                p, alpha = [], []
                for c in range(nc):
                    m_old = m_ref[c]
                    m_new = jnp.maximum(m_old, jnp.max(z[c], axis=1, keepdims=True))
                    alpha.append(jnp.exp(m_old - m_new))
                    pc = jnp.exp(z[c] - m_new)
                    l_ref[c] = alpha[c] * l_ref[c] + jnp.sum(pc, axis=1, keepdims=True)
                    m_ref[c] = m_new
                    p.append(pc.astype(BF16))
                pv = [_dot(p[c], _only(heads[h], vj[pp])) for c, (pp, h) in enumerate(chains)]
                for c in range(nc):
                    acc_ref[c] = alpha[c] * acc_ref[c] + pv[c]

            def kblock(j, _):
                tile(j, None)
                return 0

            lax.fori_loop(0, i, kblock, 0)
            tile(i, diag)
            for pp in range(npair):
                out = acc_ref[2 * pp] / l_ref[2 * pp] + acc_ref[2 * pp + 1] / l_ref[2 * pp + 1]
                o_ref[_blk(i), _pair_cols(pp)] = out.astype(BF16)
                o32_ref[_blk(i), _pair_cols(pp)] = out
                lse = [m_ref[2 * pp + h] + jnp.log(l_ref[2 * pp + h]) for h in range(2)]
                lse_ref[_blk(i), _pair_cols(pp)] = jnp.where(lane == 0, lse[0], jnp.where(lane == 1, lse[1], 0.0))
            return 0

        lax.fori_loop(0, NBLK, qblock, 0)

    return pl.pallas_call(
        body, name=name,
        out_shape=(jax.ShapeDtypeStruct((t, W_ATT), BF16), jax.ShapeDtypeStruct((t, W_ATT), F32),
                   jax.ShapeDtypeStruct((t, W_ATT), F32)),
        grid=(nseq, NH // (2 * npair)), in_specs=[_group_spec(1, npair), SEQ_SPEC, CROW_SPEC], out_specs=(_gheads_spec(npair), _gheads_spec(npair), _gheads_spec(npair)),
        scratch_shapes=_qkv_scratch(npair) + [pltpu.VMEM((2 * npair, BQ, LANES), F32), pltpu.VMEM((2 * npair, BQ, 1), F32),
                                      pltpu.VMEM((2 * npair, BQ, 1), F32)],
        compiler_params=_cparams("parallel", "parallel"))(proj, c, crow)


def _fox_bwd(proj, c, crow, o32, lse, do, dproj, nseq, npair, name):
    t = nseq * LP
    chains = _chains(npair)

    def body(p_ref, c_ref, cr_ref, o_ref, lse_ref, do_ref, _, dp_ref, dck_ref, dcq_ref,
             q_s, k_s, v_s, dqa_ref, dka_ref, dva_ref, rsum_ref):
        _load_qkv(p_ref, q_s, k_s, v_s)
        row, col = _tile_iotas()
        diag = col <= row
        lane = _lane_iota()
        heads = _head_masks()
        sub = lax.broadcasted_iota(jnp.int32, (NH, BQ), 0)
        group = pl.program_id(1)
        head0 = 2 * npair * group
        nc = len(chains)
        dka_ref[...] = jnp.zeros_like(dka_ref)
        dva_ref[...] = jnp.zeros_like(dva_ref)

        @pl.when(group == 0)
        def _():
            dck_ref[...] = jnp.zeros_like(dck_ref)
            dcq_ref[...] = jnp.zeros_like(dcq_ref)

        def qblock(i, _):
            dqa_ref[...] = jnp.zeros_like(dqa_ref)
            rsum_ref[...] = jnp.zeros_like(rsum_ref)
            cblk = c_ref[_blk(i), :]
            qb = [q_s[pp, _blk(i), :] for pp in range(npair)]
            dob = [do_ref[_blk(i), _pair_cols(pp)] for pp in range(npair)]
            prod = [dob[pp].astype(F32) * o_ref[_blk(i), _pair_cols(pp)] for pp in range(npair)]
            cq = [_pick_lane(cblk, head0 + c) for c in range(nc)]
            lse_i = [_pick_lane(lse_ref[_blk(i), _pair_cols(pp)], h) for pp, h in chains]
            delta = [jnp.sum(_only(heads[h], prod[pp]), axis=1, keepdims=True) for pp, h in chains]

            def tile(j, causal):
                kj = [k_s[pp, _blk(j), :] for pp in range(npair)]
                vj = [v_s[pp, _blk(j), :] for pp in range(npair)]
                keys = pl.ds(pl.multiple_of(j * BQ, BQ), BQ)
                q = [_only(heads[h], qb[pp]) for pp, h in chains]
                dov = [_only(heads[h], dob[pp]) for pp, h in chains]
                z = [_fox_scores(q[c], kj[pp], cq[c], _key_cols(cr_ref, head0 + c, j), causal)
                     for c, (pp, _) in enumerate(chains)]
                dpv = [_dot_nt(dov[c], vj[pp]) for c, (pp, _) in enumerate(chains)]
                p = [jnp.exp(z[c] - lse_i[c]) for c in range(nc)]
                ds = [p[c] * (dpv[c] - delta[c]) for c in range(nc)]
                dsb = [d.astype(BF16) for d in ds]
                dq = [_dot(dsb[c], _only(heads[h], kj[pp])) for c, (pp, h) in enumerate(chains)]
                dk = [_dot_tn(dsb[c], q[c]) for c in range(nc)]
                dv = [_dot_tn(p[c].astype(BF16), dov[c]) for c in range(nc)]
                for pp in range(npair):
                    dqa_ref[pp] += dq[2 * pp] + dq[2 * pp + 1]
                    dka_ref[pp, _blk(j), :] += dk[2 * pp] + dk[2 * pp + 1]
                    dva_ref[pp, _blk(j), :] += dv[2 * pp] + dv[2 * pp + 1]
                col_sums = jnp.zeros((NH, BQ), F32)
                for c in range(nc):
                    col_sums = col_sums + jnp.where(sub == head0 + c, jnp.sum(ds[c], axis=0, keepdims=True), 0.0)
                    rsum_ref[c] += jnp.sum(ds[c], axis=1, keepdims=True)
                dck_ref[:, keys] = dck_ref[:, keys] - col_sums

            def kblock(j, _):
                tile(j, None)
                return 0

            lax.fori_loop(0, i, kblock, 0)
            tile(i, diag)
            row_sums = jnp.zeros((BQ, LANES), F32)
            for c in range(nc):
                row_sums = row_sums + jnp.where(lane == head0 + c, rsum_ref[c], 0.0)
            dcq_ref[_blk(i), :] += row_sums
            for pp in range(npair):
                dp_ref[_blk(i), pp * PAIR_W:pp * PAIR_W + LANES] = (dqa_ref[pp] * SCALE).astype(BF16)
            return 0

        lax.fori_loop(0, NBLK, qblock, 0)
        for pp in range(npair):
            dp_ref[:, pp * PAIR_W + LANES:pp * PAIR_W + 2 * LANES] = dka_ref[pp].astype(BF16)
            dp_ref[:, pp * PAIR_W + 2 * LANES:pp * PAIR_W + 3 * LANES] = dva_ref[pp].astype(BF16)

    return pl.pallas_call(
        body, name=name,
        out_shape=(jax.ShapeDtypeStruct(dproj.shape, BF16), jax.ShapeDtypeStruct((nseq, NH, LP), F32),
                   jax.ShapeDtypeStruct((t, LANES), F32)),
        grid=(nseq, NH // (2 * npair)),
        in_specs=[_group_spec(1, npair), SEQ_SPEC, CROW_SPEC, _gheads_spec(npair), _gheads_spec(npair), _gheads_spec(npair), ANY],
        out_specs=(_group_spec(1, npair), CROW_SPEC, SEQ_SPEC),
        input_output_aliases={6: 0},
        scratch_shapes=_qkv_scratch(npair) + [pltpu.VMEM((npair, BQ, LANES), F32), pltpu.VMEM((npair, LP, LANES), F32),
                                      pltpu.VMEM((npair, LP, LANES), F32), pltpu.VMEM((2 * npair, BQ, 1), F32)],
        compiler_params=_cparams("parallel", "arbitrary"))(proj, c, crow, o32, lse, do, dproj)


def _adamw_math(w, g, m, v):
    m = B1 * m + (1.0 - B1) * g
    v = B2 * v + (1.0 - B2) * (g * g)
    m_hat = m / (1.0 - B1 ** STEP)
    v_hat = v / (1.0 - B2 ** STEP)
    delta = -LR * (m_hat / (jnp.sqrt(v_hat) + EPS) + WD * w)
    return delta, m, v


def _sum_adamw(parts, w, m, v, tr, name):
    rows, cols = w.shape
    cp = parts.shape[2]
    assert rows % tr == 0 and parts.shape[1] == rows

    def body(p_ref, w_ref, m_ref, v_ref, g_ref, d_ref, nm_ref, nv_ref):
        gsum = p_ref[0].astype(F32)
        for s in range(1, N_DEV):
            gsum = gsum + p_ref[s].astype(F32)
        gsum = gsum[:, :cols]
        d, nm, nv = _adamw_math(w_ref[...], gsum, m_ref[...], v_ref[...])
        g_ref[...] = gsum
        d_ref[...] = d
        nm_ref[...] = nm
        nv_ref[...] = nv

    blk = pl.BlockSpec((tr, cols), lambda i: (i, 0))
    out = jax.ShapeDtypeStruct((rows, cols), F32)
    return pl.pallas_call(
        body, name=name, out_shape=(out, out, out, out), grid=(rows // tr,),
        in_specs=[pl.BlockSpec((N_DEV, tr, cp), lambda i: (0, i, 0)), blk, blk, blk],
        out_specs=(blk, blk, blk, blk), compiler_params=_cparams("parallel"))(parts, w, m, v)


def _local_step(x, tgt, meta, g_mix, w_in_p, b_forget, w_bsb, w_bfx, w_out, g_ffn, w_up_i, cw_i, w_down, g_final):
    nseq = x.shape[0]
    t = nseq * LP
    tm = LP // 2
    mm = functools.partial(_matmul, tm=tm)

    h0 = _pad_rows(meta, x, nseq, "pad_x").reshape(t, D)
    tgt_p = _pad_rows(jnp.zeros((N_META, D), F32), tgt, nseq, "pad_target").reshape(t, D)
    bf = jnp.pad(b_forget.reshape(1, NH), ((0, 0), (0, LANES - NH)))

    _, n1 = _norm_fwd(h0, None, g_mix, "norm1")
    proj = mm(n1, w_in_p, out_dtype=F32, tn=1792, tk=D, name="in_proj")
    c = _gate_fwd(proj, bf, nseq, "gate_fwd")
    crow = c[:, :NH].reshape(nseq, LP, NH).transpose(0, 2, 1)
    o_sb, rs = _sb_fwd(proj, nseq, 2, "sb_fwd")
    o_fx, o_fx32, lse = _fox_fwd(proj, c, crow, nseq, 1, "fox_fwd")
    p_sb = mm(o_sb, w_bsb, out_dtype=F32, tn=D, tk=W_ATT, name="branch_sb")
    p_fx = mm(o_fx, w_bfx, out_dtype=F32, tn=D, tk=W_ATT, name="branch_fox")
    merged = _merge_fwd(p_sb, p_fx, proj, "merge_fwd")
    mix = mm(merged, w_out, out_dtype=F32, tn=D, tk=D, name="out_proj")
    h1, n2 = _norm_fwd(h0, mix, g_ffn, "norm2")
    u = mm(n2, w_up_i, out_dtype=F32, tn=1408, tk=D, name="up_proj")
    act = _conv_glu_fwd(u, cw_i, nseq, "conv_glu_fwd")
    ffn = mm(act, w_down, out_dtype=F32, tn=D, tk=1408, name="down_proj")

    loss, dh2, dh2b, dg_final = _final_loss_bwd(h1, ffn, g_final, tgt_p, "final")
    d_down = _matmul(act, dh2b, out_dtype=BF16, tm=1408, tn=D, tk=tm, ta=True, name="d_w_down")
    dact = mm(dh2b, w_down, out_dtype=F32, tn=1408, tk=D, tb=True, name="d_act")
    du, d_cw = _conv_glu_bwd(u, cw_i, dact, nseq, "conv_glu_bwd")
    d_up = _matmul(n2, du, out_dtype=BF16, tm=D, tn=1408, tk=tm, ta=True, name="d_w_up")
    dn2 = mm(du, w_up_i, out_dtype=F32, tn=D, tk=1408, tb=True, name="d_n2")
    dh1, dh1b, dg_ffn = _norm_bwd(h1, dn2, g_ffn, dh2, True, "norm2_bwd")
    d_out = _matmul(merged, dh1b, out_dtype=BF16, tm=D, tn=D, tk=tm, ta=True, name="d_w_out")
    dmerged = mm(dh1b, w_out, out_dtype=F32, tn=D, tk=D, tb=True, name="d_merged")
    dp_sb, dproj = _merge_bwd(dmerged, p_sb, proj, None, 0, "merge_bwd_sb")
    dp_fx, dproj = _merge_bwd(dmerged, p_fx, proj, dproj, 1, "merge_bwd_fox")
    d_bsb = _matmul(o_sb, dp_sb, out_dtype=BF16, tm=W_ATT, tn=D, tk=tm, ta=True, name="d_w_branch_sb")
    d_bfx = _matmul(o_fx, dp_fx, out_dtype=BF16, tm=W_ATT, tn=D, tk=tm, ta=True, name="d_w_branch_fox")
    do_sb = mm(dp_sb, w_bsb, out_dtype=BF16, tn=W_ATT, tk=D, tb=True, name="d_o_sb")
    do_fx = mm(dp_fx, w_bfx, out_dtype=BF16, tn=W_ATT, tk=D, tb=True, name="d_o_fox")
    dproj = _sb_bwd(proj, do_sb, rs, dproj, nseq, 2, "sb_bwd")
    dproj, dck, dcq = _fox_bwd(proj, c, crow, o_fx32, lse, do_fx, dproj, nseq, 2, "fox_bwd")
    dc = dcq + jnp.pad(dck.transpose(0, 2, 1).reshape(t, NH), ((0, 0), (0, LANES - NH)))
    dproj, d_bf = _gate_bwd(proj, bf, dc, dproj, nseq, "gate_bwd")
    d_in = _matmul(n1, dproj, out_dtype=BF16, tm=D, tn=1792, tk=tm, ta=True, name="d_w_in")
    dn1 = mm(dproj, w_in_p, out_dtype=F32, tn=D, tk=1792, tb=True, name="d_n1")
    dh0, dg_mix = _norm_bwd(h0, dn1, g_mix, dh1, False, "norm1_bwd")
    dh0 = dh0.reshape(nseq, LP, D)
    grads = dict(meta_tokens=jnp.sum(dh0[:, :N_META], axis=0), norm_mix_g=dg_mix, w_in=d_in, b_forget=d_bf[:, :NH],
                 w_branch_sb=d_bsb, w_branch_fox=d_bfx, w_out=d_out, norm_ffn_g=dg_ffn, w_up=d_up, conv_w=d_cw,
                 w_down=d_down, norm_final_g=dg_final)
    return loss[0, 0], _real_rows(dh0, nseq, "grad_x"), grads


REPL = (("norm_mix_g", D), ("norm_ffn_g", D), ("norm_final_g", D), ("b_forget", LANES))
REPL_ROWS = 32


def _pack_repl(tree):
    rows = [jnp.pad(tree[name].reshape(-1), (0, n - tree[name].size)).reshape(-1, LANES) for name, n in REPL]
    packed = jnp.concatenate(rows, axis=0)
    return jnp.pad(packed, ((0, REPL_ROWS - packed.shape[0]), (0, 0)))


def _unpack_repl(packed, shapes):
    out, r = {}, 0
    for name, n in REPL:
        size = 1
        for s in shapes[name]:
            size *= s
        out[name] = packed[r:r + n // LANES].reshape(-1)[:size].reshape(shapes[name])
        r += n // LANES
    return out


def kernel(x, meta_tokens, norm_mix_g, w_in, b_forget, w_branch_sb, w_branch_fox, w_out, norm_ffn_g, w_up, conv_w, w_down, norm_final_g, loss_target, m_meta_tokens, m_norm_mix_g, m_w_in, m_b_forget, m_w_branch_sb, m_w_branch_fox, m_w_out, m_norm_ffn_g, m_w_up, m_conv_w, m_w_down, m_norm_final_g, v_meta_tokens, v_norm_mix_g, v_w_in, v_b_forget, v_w_branch_sb, v_w_branch_fox, v_w_out, v_norm_ffn_g, v_w_up, v_conv_w, v_w_down, v_norm_final_g):
    w = dict(meta_tokens=meta_tokens, norm_mix_g=norm_mix_g, w_in=w_in, b_forget=b_forget, w_branch_sb=w_branch_sb,
             w_branch_fox=w_branch_fox, w_out=w_out, norm_ffn_g=norm_ffn_g, w_up=w_up, conv_w=conv_w, w_down=w_down,
             norm_final_g=norm_final_g)
    m = dict(meta_tokens=m_meta_tokens, norm_mix_g=m_norm_mix_g, w_in=m_w_in, b_forget=m_b_forget,
             w_branch_sb=m_w_branch_sb, w_branch_fox=m_w_branch_fox, w_out=m_w_out, norm_ffn_g=m_norm_ffn_g,
             w_up=m_w_up, conv_w=m_conv_w, w_down=m_w_down, norm_final_g=m_norm_final_g)
    v = dict(meta_tokens=v_meta_tokens, norm_mix_g=v_norm_mix_g, w_in=v_w_in, b_forget=v_b_forget,
             w_branch_sb=v_w_branch_sb, w_branch_fox=v_w_branch_fox, w_out=v_w_out, norm_ffn_g=v_norm_ffn_g,
             w_up=v_w_up, conv_w=v_conv_w, w_down=v_w_down, norm_final_g=v_norm_final_g)
    shapes = {k: a.shape for k, a in w.items()}
    sharded = ("w_in", "w_branch_sb", "w_branch_fox", "w_out", "w_up", "w_down", "conv_w", "meta_tokens")
    mat = lambda tree, name: tree[name].reshape(tree[name].shape[-2:])

    def lane_pad(a, width):
        return jnp.pad(a, ((0, 0), (0, width - a.shape[1])))

    g_in, g_bsb, g_bfx, g_out, g_up, g_down, g_cw, g_meta = _all_gather(
        [lane_pad(mat(w, "w_in").astype(BF16), SHARD_P), mat(w, "w_branch_sb").astype(BF16),
         mat(w, "w_branch_fox").astype(BF16), mat(w, "w_out").astype(BF16), lane_pad(mat(w, "w_up").astype(BF16), SHARD_P),
         mat(w, "w_down").astype(BF16), mat(w, "conv_w"), mat(w, "meta_tokens")], "gather_weights")
    w_in_p = _relayout(g_in, 1, IN_P, _gathered_to_full(IN_SHARD, _in_padded_to_orig), BF16, 256, "w_in_cols")[0]
    w_up_i = _relayout(g_up, 1, 2 * D_FF, _gathered_to_full(UP_SHARD, _up_inter_to_orig), BF16, 256, "w_up_cols")[0]
    w_bsb = _relayout(g_bsb, 1, D, _gathered_to_full(ATT_SHARD, lambda d: d), BF16, 256, "w_bsb_cols")[0]
    w_bfx = _relayout(g_bfx, 1, D, _gathered_to_full(ATT_SHARD, lambda d: d), BF16, 256, "w_bfx_cols")[0]
    cw_full = g_cw.transpose(1, 0, 2).reshape(3, 2 * D_FF)
    cw_i = cw_full.reshape(3, 2, D_FF // FFC, FFC).transpose(0, 2, 1, 3).reshape(3, 2 * D_FF)
    meta_full = g_meta.transpose(1, 0, 2).reshape(N_META, D)

    loss, grad_x, grads = _local_step(
        x, loss_target, meta_full, norm_mix_g.reshape(1, D), w_in_p, b_forget, w_bsb, w_bfx, g_out.reshape(D, D),
        norm_ffn_g.reshape(1, D), w_up_i, cw_i, g_down.reshape(D_FF, D), norm_final_g.reshape(1, D))

    d_cw = grads["conv_w"].reshape(3, D_FF // FFC, 2, FFC).transpose(0, 2, 1, 3).reshape(3, 2 * D_FF)
    parts = _exchange(
        [_relayout(grads["w_in"][None], N_DEV, SHARD_P, _full_to_shards(IN_SHARD, _IN_ORIG_TO_PADDED.get), BF16, 256, "d_w_in_shards"),
         _relayout(grads["w_branch_sb"][None], N_DEV, ATT_SHARD, _full_to_shards(ATT_SHARD, lambda c: c), BF16, 256, "d_w_bsb_shards"),
         _relayout(grads["w_branch_fox"][None], N_DEV, ATT_SHARD, _full_to_shards(ATT_SHARD, lambda c: c), BF16, 256, "d_w_bfx_shards"),
         grads["w_out"].reshape(N_DEV, D // N_DEV, D),
         _relayout(grads["w_up"][None], N_DEV, SHARD_P, _full_to_shards(UP_SHARD, _UP_ORIG_TO_INTER.get), BF16, 256, "d_w_up_shards"),
         grads["w_down"].reshape(N_DEV, D_FF // N_DEV, D),
         d_cw.reshape(3, N_DEV, UP_SHARD).transpose(1, 0, 2),
         grads["meta_tokens"].reshape(N_META, N_DEV, ATT_SHARD).transpose(1, 0, 2)], "exchange_grads")
    tiles = dict(w_in=256, w_branch_sb=256, w_branch_fox=256, w_out=D // N_DEV, w_up=256, w_down=D_FF // N_DEV,
                 conv_w=3, meta_tokens=N_META)
    new = {name: _sum_adamw(p, mat(w, name), mat(m, name), mat(v, name), tiles[name], "adamw_" + name)
           for name, p in zip(sharded, parts)}

    rparts, = _all_gather([_pack_repl(grads)], "gather_replicated_grads")
    routs = _sum_adamw(rparts, _pack_repl(w), _pack_repl(m), _pack_repl(v), REPL_ROWS, "adamw_replicated")
    repl = [_unpack_repl(o, shapes) for o in routs]

    result = [lax.psum(loss, ("x", "y", "c")), grad_x]
    for k in range(4):
        for name in w:
            result.append(new[name][k].reshape(shapes[name]) if name in new else repl[k][name])
    return tuple(result)
```

```python
import functools

import jax
import jax.numpy as jnp
from jax import lax
from jax.experimental import pallas as pl
from jax.experimental.pallas import tpu as pltpu

F32 = jnp.float32
BF16 = jnp.bfloat16

N_DEV = 8
LANES = 128
D = 1024
N_META = 16
SEQ = 2048
L_REAL = N_META + SEQ
LP = 2304
BQ = 256
NBLK = LP // BQ
HEAD = 64
NH = 8
W_ATT = NH * HEAD
PAIR_W = 3 * LANES
D_FF = 2816
IN_COLS = 5128
QKV = 6 * W_ATT
IN_P = 5376
GATE_COL = QKV
F_COL = QKV + 2 * D
FFC = 256
RMS_EPS = 1e-6
LR, B1, B2, EPS, WD, STEP = 0.001, 0.9, 0.999, 1e-08, 0.01, 10
VMEM_LIMIT = 56 * 1024 * 1024

MESH = pl.DeviceIdType.MESH
ANY = pl.BlockSpec(memory_space=pl.ANY)


def _cparams(*sem):
    return pltpu.CompilerParams(dimension_semantics=sem if sem else None, vmem_limit_bytes=VMEM_LIMIT)


def _all_gather(xs, name):
    n = len(xs)

    def body(*refs):
        x_refs, out_refs = refs[:n], refs[n:2 * n]
        send_sems, recv_sems, local_sems = refs[2 * n:]
        mx, my, mc = lax.axis_index("x"), lax.axis_index("y"), lax.axis_index("c")
        me, sibling = (mx, my, mc), (mx, my, 1 - mc)
        chips = [(1 - mx, my), (mx, 1 - my), (1 - mx, 1 - my)]

        def copy(a, k, block, to, own=False):
            px, py, pc = block
            slot = out_refs[a].at[4 * px + 2 * py + pc]
            return pltpu.make_async_remote_copy(
                src_ref=x_refs[a] if own else slot, dst_ref=slot,
                send_sem=send_sems.at[7 * a + k], recv_sem=recv_sems.at[7 * a + k],
                device_id=to, device_id_type=MESH)

        mine = [pltpu.make_async_copy(x_refs[a], out_refs[a].at[4 * mx + 2 * my + mc], local_sems.at[a]) for a in range(n)]
        for cp in mine:
            cp.start()
        first = []
        for a in range(n):
            first.append(copy(a, 0, me, sibling, own=True))
            first += [copy(a, 1 + j, me, (*chip, mc), own=True) for j, chip in enumerate(chips)]
        for cp in first:
            cp.start()
        passed = []
        for j, chip in enumerate(chips):
            for a in range(n):
                copy(a, 1 + j, (*chip, mc), me).wait_recv()
                fwd = copy(a, 4 + j, (*chip, mc), sibling)
                fwd.start()
                passed.append(fwd)
        for a in range(n):
            copy(a, 0, sibling, me).wait_recv()
            for j, chip in enumerate(chips):
                copy(a, 4 + j, (*chip, 1 - mc), me).wait_recv()
        for cp in first + passed:
            cp.wait_send()
        for cp in mine:
            cp.wait()

    return pl.pallas_call(
        body, name=name,
        out_shape=tuple(jax.ShapeDtypeStruct((N_DEV,) + x.shape, x.dtype) for x in xs),
        in_specs=[ANY] * n, out_specs=tuple([ANY] * n),
        scratch_shapes=[pltpu.SemaphoreType.DMA((7 * n,)), pltpu.SemaphoreType.DMA((7 * n,)),
                        pltpu.SemaphoreType.DMA((n,))],
    )(*xs)


def _exchange(srcs, name):
    n = len(srcs)

    def body(*refs):
        src_refs, dst_refs = refs[:n], refs[n:2 * n]
        send_sems, recv_sems, local_sems = refs[2 * n:]
        mx, my, mc = lax.axis_index("x"), lax.axis_index("y"), lax.axis_index("c")
        me_idx = 4 * mx + 2 * my + mc
        mine = [pltpu.make_async_copy(src_refs[a].at[me_idx], dst_refs[a].at[me_idx], local_sems.at[a]) for a in range(n)]
        for cp in mine:
            cp.start()
        copies = []
        for k in range(1, N_DEV):
            px, py, pc = mx ^ (k >> 2), my ^ ((k >> 1) & 1), mc ^ (k & 1)
            for a in range(n):
                copies.append(pltpu.make_async_remote_copy(
                    src_ref=src_refs[a].at[4 * px + 2 * py + pc], dst_ref=dst_refs[a].at[me_idx],
                    send_sem=send_sems.at[7 * a + k - 1], recv_sem=recv_sems.at[7 * a + k - 1],
                    device_id=(px, py, pc), device_id_type=MESH))
        for cp in copies:
            cp.start()
        for cp in copies:
            cp.wait_recv()
        for cp in copies:
            cp.wait_send()
        for cp in mine:
            cp.wait()

    return pl.pallas_call(
        body, name=name,
        out_shape=tuple(jax.ShapeDtypeStruct(s.shape, s.dtype) for s in srcs),
        in_specs=[ANY] * n, out_specs=tuple([ANY] * n),
        scratch_shapes=[pltpu.SemaphoreType.DMA((7 * n,)), pltpu.SemaphoreType.DMA((7 * n,)),
                        pltpu.SemaphoreType.DMA((n,))],
    )(*srcs)


HBM = pl.BlockSpec(memory_space=pltpu.HBM)
SEM = pl.BlockSpec(memory_space=pltpu.SEMAPHORE)
EFFECT = pltpu.SideEffectType.DATAFLOW_SIDE_EFFECTING


def _peer_copies(src_refs, land_refs, send_sems, recv_sems, per_peer):
    mx, my, mc = lax.axis_index("x"), lax.axis_index("y"), lax.axis_index("c")
    me_idx = 4 * mx + 2 * my + mc
    copies = []
    for k in range(1, N_DEV):
        px, py, pc = mx ^ (k >> 2), my ^ ((k >> 1) & 1), mc ^ (k & 1)
        for a, (src, land) in enumerate(zip(src_refs, land_refs)):
            copies.append(pltpu.make_async_remote_copy(
                src_ref=src.at[4 * px + 2 * py + pc] if per_peer else src, dst_ref=land.at[me_idx],
                send_sem=send_sems.at[7 * a + k - 1], recv_sem=recv_sems.at[7 * a + k - 1],
                device_id=(px, py, pc), device_id_type=MESH))
    return me_idx, copies


def _remote_start(srcs, per_peer, name):
    n = len(srcs)
    lands = [lax.empty(s.shape if per_peer else (N_DEV,) + s.shape, s.dtype) for s in srcs]

    def body(*refs):
        src_refs, land_refs = refs[:n], refs[n:2 * n]
        send_sems, recv_sems = refs[2 * n:2 * n + 2]
        token, local_sems = refs[4 * n + 2], refs[4 * n + 3]
        me_idx, copies = _peer_copies(src_refs, land_refs, send_sems, recv_sems, per_peer)
        mine = [pltpu.make_async_copy(src.at[me_idx] if per_peer else src, land.at[me_idx], local_sems.at[a])
                for a, (src, land) in enumerate(zip(src_refs, land_refs))]
        for cp in mine + copies:
            cp.start()
        for cp in mine:
            cp.wait()
        token[...] = jnp.zeros_like(token)

    thru = [pltpu.HBM(a.shape, a.dtype) for a in list(srcs) + lands]
    out = pl.pallas_call(
        body, name=name,
        out_shape=(pltpu.SemaphoreType.DMA((7 * n,)), pltpu.SemaphoreType.DMA((7 * n,)), *thru,
                   jax.ShapeDtypeStruct((8, LANES), F32)),
        in_specs=[HBM] * (2 * n), out_specs=(SEM, SEM, *([HBM] * (2 * n)), pl.BlockSpec(memory_space=pltpu.VMEM)),
        input_output_aliases={i: 2 + i for i in range(2 * n)},
        scratch_shapes=[pltpu.SemaphoreType.DMA((n,))],
        compiler_params=pltpu.CompilerParams(has_side_effects=EFFECT),
    )(*[pltpu.with_memory_space_constraint(a, pltpu.HBM) for a in list(srcs) + lands])
    return dict(sems=out[:2], bufs=out[2:2 * n + 2], per_peer=per_peer), out[-1]


def _remote_wait(pending, after, name):
    bufs = pending["bufs"]
    n = len(bufs) // 2
    per_peer = pending["per_peer"]

    def body(*refs):
        src_refs, land_refs = refs[:n], refs[n:2 * n]
        send_sems, recv_sems = refs[2 * n:2 * n + 2]
        _, copies = _peer_copies(src_refs, land_refs, send_sems, recv_sems, per_peer)
        for cp in copies:
            cp.wait_send()
        for cp in copies:
            cp.wait_recv()

    out = pl.pallas_call(
        body, name=name, out_shape=tuple(pltpu.HBM(a.shape, a.dtype) for a in bufs),
        in_specs=[HBM] * (2 * n) + [SEM, SEM, ANY], out_specs=tuple([HBM] * (2 * n)),
        input_output_aliases={i: i for i in range(2 * n)},
        compiler_params=pltpu.CompilerParams(has_side_effects=EFFECT),
    )(*bufs, *pending["sems"], after)
    return out[n:]


ROWS_PER_COPY = 256


def _pad_rows(front, body_rows, nseq, name):
    tail = LP - L_REAL
    nblk = SEQ // ROWS_PER_COPY

    def body(f_ref, b_ref, o_ref, z_ref, sems):
        s, i = pl.program_id(0), pl.program_id(1)
        rows = pltpu.make_async_copy(b_ref, o_ref.at[pl.ds(s, 1), pl.ds(N_META + i * ROWS_PER_COPY, ROWS_PER_COPY)], sems.at[0])
        rows.start()

        @pl.when(i == 0)
        def _():
            z_ref[...] = jnp.zeros_like(z_ref)
            head = pltpu.make_async_copy(f_ref, o_ref.at[s, pl.ds(0, N_META)], sems.at[1])
            zeros = pltpu.make_async_copy(z_ref, o_ref.at[s, pl.ds(L_REAL, tail)], sems.at[2])
            head.start()
            zeros.start()
            head.wait()
            zeros.wait()

        rows.wait()

    return pl.pallas_call(
        body, name=name, out_shape=jax.ShapeDtypeStruct((nseq, LP, D), F32), grid=(nseq, nblk),
        in_specs=[pl.BlockSpec((N_META, D), lambda s, i: (0, 0)), pl.BlockSpec((1, ROWS_PER_COPY, D), lambda s, i: (s, i, 0))],
        out_specs=ANY,
        scratch_shapes=[pltpu.VMEM((tail, D), F32), pltpu.SemaphoreType.DMA((3,))],
        compiler_params=_cparams("arbitrary", "arbitrary"))(front, body_rows)


def _real_rows(h, nseq, name):
    nblk = SEQ // ROWS_PER_COPY

    def body(h_ref, o_ref, sem):
        s, i = pl.program_id(0), pl.program_id(1)
        rows = pltpu.make_async_copy(h_ref.at[pl.ds(s, 1), pl.ds(N_META + i * ROWS_PER_COPY, ROWS_PER_COPY)], o_ref, sem)
        rows.start()
        rows.wait()

    return pl.pallas_call(
        body, name=name, out_shape=jax.ShapeDtypeStruct((nseq, SEQ, D), F32), grid=(nseq, nblk),
        in_specs=[ANY], out_specs=pl.BlockSpec((1, ROWS_PER_COPY, D), lambda s, i: (s, i, 0)),
        scratch_shapes=[pltpu.SemaphoreType.DMA],
        compiler_params=_cparams("arbitrary", "arbitrary"))(h)


def _plan_cols(n_q, n_dcols, src_of):
    plan = {}
    for q in range(n_q):
        for dblk in range(n_dcols // LANES):
            segs, key, start = [], None, 0
            for lane in range(LANES + 1):
                new = None
                if lane < LANES:
                    src = src_of(q, dblk * LANES + lane)
                    if src is not None:
                        new = (src[0], src[1] // LANES, (lane - src[1] % LANES) % LANES)
                if new != key:
                    if key is not None:
                        segs.append((*key, start, lane))
                    key, start = new, lane
            plan[(q, dblk)] = segs
    return plan


def _relayout(src, n_q, n_dcols, src_of, out_dtype, tr, name):
    n_p, rows, scols = src.shape
    plan = _plan_cols(n_q, n_dcols, src_of)

    def body(s_ref, d_ref):
        lane = lax.broadcasted_iota(jnp.int32, (tr, LANES), 1)
        for (q, dblk), segs in plan.items():
            acc = jnp.zeros((tr, LANES), F32)
            for p, sblk, rot, lo, hi in segs:
                x = s_ref[p, :, sblk * LANES:(sblk + 1) * LANES].astype(F32)
                if rot:
                    x = pltpu.roll(x, rot, 1)
                acc = x if (lo, hi) == (0, LANES) else jnp.where((lane >= lo) & (lane < hi), x, acc)
            d_ref[q, :, dblk * LANES:(dblk + 1) * LANES] = acc.astype(out_dtype)

    return pl.pallas_call(
        body, name=name, out_shape=jax.ShapeDtypeStruct((n_q, rows, n_dcols), out_dtype), grid=(rows // tr,),
        in_specs=[pl.BlockSpec((n_p, tr, scols), lambda i: (0, i, 0))],
        out_specs=pl.BlockSpec((n_q, tr, n_dcols), lambda i: (0, i, 0)),
        compiler_params=_cparams("parallel"))(src)


def _in_padded_to_orig(d):
    if d < QKV:
        kind, r = divmod(d, 4 * PAIR_W)
        pair, r = divmod(r, PAIR_W)
        part, r = divmod(r, LANES)
        return kind * 3 * W_ATT + part * W_ATT + pair * LANES + r
    if d < F_COL:
        return d + NH
    if d < F_COL + NH:
        return d - 2 * D
    return None


_IN_ORIG_TO_PADDED = {_in_padded_to_orig(d): d for d in range(IN_P) if _in_padded_to_orig(d) is not None}


def _up_inter_to_orig(d):
    j, r = divmod(d, 2 * FFC)
    part, r = divmod(r, FFC)
    return part * D_FF + j * FFC + r


_UP_ORIG_TO_INTER = {_up_inter_to_orig(d): d for d in range(2 * D_FF)}
IN_SHARD = IN_COLS // N_DEV
UP_SHARD = 2 * D_FF // N_DEV
SHARD_P = 768
ATT_SHARD = D // N_DEV


def _gathered_to_full(n_shard, to_orig):
    def src_of(q, d):
        c = to_orig(d)
        return None if c is None else (c // n_shard, c % n_shard)
    return src_of


def _full_to_shards(n_shard, from_orig):
    def src_of(q, d):
        return (0, from_orig(q * n_shard + d)) if d < n_shard else None
    return src_of


def _matmul(a, b, *, out_dtype, tm, tn, tk, ta=False, tb=False, name):
    if ta:
        kdim, m = a.shape
    else:
        m, kdim = a.shape
    n = b.shape[0] if tb else b.shape[1]
    assert m % tm == 0 and n % tn == 0 and kdim % tk == 0, (name, a.shape, b.shape, tm, tn, tk)
    nk = kdim // tk

    def body(a_ref, b_ref, o_ref, *scratch):
        av, bv = a_ref[...], b_ref[...]
        if ta:
            p = lax.dot_general(av, bv, (((0,), (0,)), ((), ())), preferred_element_type=F32)
        elif tb:
            p = lax.dot_general(av, bv, (((1,), (1,)), ((), ())), preferred_element_type=F32)
        else:
            p = jnp.dot(av, bv, preferred_element_type=F32)
        if nk == 1:
            o_ref[...] = p.astype(o_ref.dtype)
        else:
            acc_ref, = scratch
            k = pl.program_id(2)

            @pl.when(k == 0)
            def _():
                acc_ref[...] = p

            @pl.when(k > 0)
            def _():
                acc_ref[...] += p

            @pl.when(k == nk - 1)
            def _():
                o_ref[...] = acc_ref[...].astype(o_ref.dtype)

    a_spec = pl.BlockSpec((tk, tm), lambda i, j, k: (k, i)) if ta else pl.BlockSpec((tm, tk), lambda i, j, k: (i, k))
    b_spec = pl.BlockSpec((tn, tk), lambda i, j, k: (j, k)) if tb else pl.BlockSpec((tk, tn), lambda i, j, k: (k, j))
    return pl.pallas_call(
        body, name=name,
        out_shape=jax.ShapeDtypeStruct((m, n), out_dtype),
        grid=(m // tm, n // tn, nk),
        in_specs=[a_spec, b_spec],
        out_specs=pl.BlockSpec((tm, tn), lambda i, j, k: (i, j)),
        scratch_shapes=[] if nk == 1 else [pltpu.VMEM((tm, tn), F32)],
        compiler_params=_cparams("parallel", "parallel", "arbitrary"),
    )(a, b)


TR = 288


def _rms(h):
    return lax.rsqrt(jnp.mean(h * h, axis=-1, keepdims=True) + RMS_EPS)


def _norm_fwd(h, delta, g, name):
    t = h.shape[0]
    row = pl.BlockSpec((TR, D), lambda i: (i, 0))
    vec = pl.BlockSpec((1, D), lambda i: (0, 0))

    if delta is None:
        def body(h_ref, g_ref, n_ref):
            hv = h_ref[...]
            n_ref[...] = ((hv * _rms(hv)) * g_ref[...]).astype(BF16)

        n = pl.pallas_call(
            body, name=name, out_shape=jax.ShapeDtypeStruct((t, D), BF16), grid=(t // TR,),
            in_specs=[row, vec], out_specs=row, compiler_params=_cparams("parallel"))(h, g)
        return h, n

    def body(h_ref, d_ref, g_ref, hn_ref, n_ref):
        hv = h_ref[...] + d_ref[...]
        hn_ref[...] = hv
        n_ref[...] = ((hv * _rms(hv)) * g_ref[...]).astype(BF16)

    return pl.pallas_call(
        body, name=name,
        out_shape=(jax.ShapeDtypeStruct((t, D), F32), jax.ShapeDtypeStruct((t, D), BF16)), grid=(t // TR,),
        in_specs=[row, row, vec], out_specs=(row, row), compiler_params=_cparams("parallel"))(h, delta, g)


def _rms_bwd_math(hv, dn, gv):
    r = _rms(hv)
    hr = hv * r
    dng = dn * gv
    dh = r * (dng - hr * jnp.mean(dng * hr, axis=-1, keepdims=True))
    return dh, dn * hr


def _final_loss_bwd(h1, delta, g, tgt, name):
    t = h1.shape[0]
    row = pl.BlockSpec((TR, D), lambda i: (i, 0))
    vec = pl.BlockSpec((1, D), lambda i: (0, 0))
    tiles_per_seq = LP // TR

    def body(h_ref, d_ref, g_ref, t_ref, loss_ref, dh_ref, dhb_ref, dg_ref):
        i = pl.program_id(0)
        hv = h_ref[...] + d_ref[...]
        gv = g_ref[...]
        r = _rms(hv)
        hr = hv * r
        y = hr * gv
        pos = (i % tiles_per_seq) * TR + lax.broadcasted_iota(jnp.int32, (TR, 1), 0)
        valid = (pos >= N_META) & (pos < L_REAL)
        err = jnp.where(valid, y - t_ref[...], 0.0)
        part = 0.5 * jnp.sum(jnp.mean(err * err, axis=-1, keepdims=True))
        dy = err * (1.0 / D)
        dng = dy * gv
        dh = r * (dng - hr * jnp.mean(dng * hr, axis=-1, keepdims=True))
        dh_ref[...] = dh
        dhb_ref[...] = dh.astype(BF16)
        dgp = jnp.sum(dy * hr, axis=0, keepdims=True)

        @pl.when(i == 0)
        def _():
            loss_ref[...] = jnp.zeros_like(loss_ref)
            dg_ref[...] = jnp.zeros_like(dg_ref)

        loss_ref[...] += part
        dg_ref[...] += dgp

    return pl.pallas_call(
        body, name=name,
        out_shape=(jax.ShapeDtypeStruct((8, 128), F32), jax.ShapeDtypeStruct((t, D), F32),
                   jax.ShapeDtypeStruct((t, D), BF16), jax.ShapeDtypeStruct((1, D), F32)),
        grid=(t // TR,),
        in_specs=[row, row, vec, row],
        out_specs=(pl.BlockSpec((8, 128), lambda i: (0, 0)), row, row, vec),
        compiler_params=_cparams("arbitrary"))(h1, delta, g, tgt)


def _norm_bwd(h, dn, g, dres, with_bf16, name):
    t = h.shape[0]
    row = pl.BlockSpec((TR, D), lambda i: (i, 0))
    vec = pl.BlockSpec((1, D), lambda i: (0, 0))

    def body(h_ref, dn_ref, g_ref, dres_ref, *outs):
        i = pl.program_id(0)
        dh, dgrow = _rms_bwd_math(h_ref[...], dn_ref[...], g_ref[...])
        dh = dh + dres_ref[...]
        outs[0][...] = dh
        if with_bf16:
            outs[1][...] = dh.astype(BF16)
        dg_ref = outs[-1]

        @pl.when(i == 0)
        def _():
            dg_ref[...] = jnp.zeros_like(dg_ref)

        dg_ref[...] += jnp.sum(dgrow, axis=0, keepdims=True)

    shapes = [jax.ShapeDtypeStruct((t, D), F32)]
    specs = [row]
    if with_bf16:
        shapes.append(jax.ShapeDtypeStruct((t, D), BF16))
        specs.append(row)
    shapes.append(jax.ShapeDtypeStruct((1, D), F32))
    specs.append(vec)
    return pl.pallas_call(
        body, name=name, out_shape=tuple(shapes), grid=(t // TR,),
        in_specs=[row, row, vec, row], out_specs=tuple(specs),
        compiler_params=_cparams("arbitrary"))(h, dn, g, dres)


GATE_BLK = GATE_COL // D


def _sigmoid(x):
    return 1.0 / (1.0 + jnp.exp(-x))


def _merge_fwd(p_sb, p_fx, proj, name):
    t = p_sb.shape[0]
    row = pl.BlockSpec((TR, D), lambda i: (i, 0))

    def body(ps_ref, pf_ref, gs_ref, gf_ref, o_ref):
        o_ref[...] = (_sigmoid(gs_ref[...]) * ps_ref[...] + _sigmoid(gf_ref[...]) * pf_ref[...]).astype(BF16)

    return pl.pallas_call(
        body, name=name, out_shape=jax.ShapeDtypeStruct((t, D), BF16), grid=(t // TR,),
        in_specs=[row, row, pl.BlockSpec((TR, D), lambda i: (i, GATE_BLK)),
                  pl.BlockSpec((TR, D), lambda i: (i, GATE_BLK + 1))],
        out_specs=row, compiler_params=_cparams("parallel"))(p_sb, p_fx, proj, proj)


def _merge_bwd(dm, p, proj, dproj, which, name):
    t = dm.shape[0]
    row = pl.BlockSpec((TR, D), lambda i: (i, 0))
    gate = pl.BlockSpec((TR, D), lambda i: (i, GATE_BLK + which))

    def body(dm_ref, p_ref, g_ref, *rest):
        dp_ref, dg_ref = rest[-2:]
        dmv = dm_ref[...]
        s = _sigmoid(g_ref[...])
        dp_ref[...] = (dmv * s).astype(BF16)
        dg_ref[...] = (dmv * p_ref[...] * s * (1.0 - s)).astype(BF16)

    out_shape = (jax.ShapeDtypeStruct((t, D), BF16), jax.ShapeDtypeStruct((t, IN_P), BF16))
    if dproj is None:
        return pl.pallas_call(
            body, name=name, out_shape=out_shape, grid=(t // TR,), in_specs=[row, row, gate],
            out_specs=(row, gate), compiler_params=_cparams("parallel"))(dm, p, proj)
    return pl.pallas_call(
        body, name=name, out_shape=out_shape, grid=(t // TR,), in_specs=[row, row, gate, ANY],
        out_specs=(row, gate), input_output_aliases={3: 1}, compiler_params=_cparams("parallel"))(dm, p, proj, dproj)


CH = 288


def _chunk(c, n=CH):
    return pl.ds(pl.multiple_of(c * CH, 8), n)


def _conv_taps(u_ref, c):
    x = u_ref[_chunk(c), :]
    prev = u_ref[pl.ds(pl.multiple_of(jnp.maximum(c * CH - 8, 0), 8), 8), :]
    xx = jnp.concatenate([jnp.where(c == 0, 0.0, prev), x], axis=0)
    return x, pltpu.roll(xx, 1, 0)[8:], pltpu.roll(xx, 2, 0)[8:]


def _conv_glu_fwd(u, cw, nseq, name):
    nblk = D_FF // FFC

    def body(u_ref, cw_ref, o_ref):
        cwv = cw_ref[...]

        def step(c, _):
            x, x1, x2 = _conv_taps(u_ref, c)
            uc = cwv[0:1, :] * x2 + cwv[1:2, :] * x1 + cwv[2:3, :] * x
            a, b = uc[:, :FFC], uc[:, FFC:]
            o_ref[_chunk(c), :] = (a * _sigmoid(a) * b).astype(BF16)
            return 0

        lax.fori_loop(0, LP // CH, step, 0)

    return pl.pallas_call(
        body, name=name, out_shape=jax.ShapeDtypeStruct((nseq * LP, D_FF), BF16), grid=(nseq, nblk),
        in_specs=[pl.BlockSpec((LP, 2 * FFC), lambda s, j: (s, j)), pl.BlockSpec((3, 2 * FFC), lambda s, j: (0, j))],
        out_specs=pl.BlockSpec((LP, FFC), lambda s, j: (s, j)),
        compiler_params=_cparams("parallel", "parallel"))(u, cw)


def _conv_glu_bwd(u, cw, dact, nseq, name):
    nblk = D_FF // FFC
    nch = LP // CH

    def body(u_ref, cw_ref, da_ref, du_ref, dcw_ref):
        s = pl.program_id(1)
        cwv = cw_ref[...]

        def step(k, carry):
            nxt, p0, p1, p2 = carry
            c = nch - 1 - k
            x, x1, x2 = _conv_taps(u_ref, c)
            uc = cwv[0:1, :] * x2 + cwv[1:2, :] * x1 + cwv[2:3, :] * x
            a, b = uc[:, :FFC], uc[:, FFC:]
            sa = _sigmoid(a)
            dactv = da_ref[_chunk(c), :]
            da = dactv * b * (sa * (1.0 + a * (1.0 - sa)))
            db = dactv * (a * sa)
            duc = jnp.concatenate([da, db], axis=1)
            dd = jnp.concatenate([duc, nxt], axis=0)
            du = (cwv[2:3, :] * duc + cwv[1:2, :] * pltpu.roll(dd, CH + 7, 0)[:CH]
                  + cwv[0:1, :] * pltpu.roll(dd, CH + 6, 0)[:CH])
            du_ref[_chunk(c), :] = du.astype(BF16)
            return (duc[:8], p0 + jnp.sum(duc * x2, axis=0, keepdims=True),
                    p1 + jnp.sum(duc * x1, axis=0, keepdims=True), p2 + jnp.sum(duc * x, axis=0, keepdims=True))

        zrow = jnp.zeros((1, 2 * FFC), F32)
        _, p0, p1, p2 = lax.fori_loop(0, nch, step, (jnp.zeros((8, 2 * FFC), F32), zrow, zrow, zrow))

        @pl.when(s == 0)
        def _():
            dcw_ref[...] = jnp.zeros_like(dcw_ref)

        dcw_ref[...] += jnp.concatenate([p0, p1, p2], axis=0)

    return pl.pallas_call(
        body, name=name,
        out_shape=(jax.ShapeDtypeStruct((nseq * LP, 2 * D_FF), BF16), jax.ShapeDtypeStruct((3, 2 * D_FF), F32)),
        grid=(nblk, nseq),
        in_specs=[pl.BlockSpec((LP, 2 * FFC), lambda j, s: (s, j)), pl.BlockSpec((3, 2 * FFC), lambda j, s: (0, j)),
                  pl.BlockSpec((LP, FFC), lambda j, s: (s, j))],
        out_specs=(pl.BlockSpec((LP, 2 * FFC), lambda j, s: (s, j)), pl.BlockSpec((3, 2 * FFC), lambda j, s: (0, j))),
        compiler_params=_cparams("parallel", "arbitrary"))(u, cw, dact)


F_BLK = F_COL // LANES
CB = 128


def _split3(x):
    hi = x.astype(BF16)
    r1 = x - hi.astype(F32)
    mid = r1.astype(BF16)
    lo = (r1 - mid.astype(F32)).astype(BF16)
    return hi, mid, lo


def _tri_dot(tri, x):
    hi, mid, lo = _split3(x)
    d = functools.partial(jnp.dot, preferred_element_type=F32)
    return d(tri, hi) + d(tri, mid) + d(tri, lo)


def _log_sigmoid(x):
    return jnp.minimum(x, 0.0) - jnp.log(1.0 + jnp.exp(-jnp.abs(x)))


def _gate_fwd(proj, bf, nseq, name):
    def body(f_ref, b_ref, c_ref):
        r_i = lax.broadcasted_iota(jnp.int32, (CB, CB), 0)
        c_i = lax.broadcasted_iota(jnp.int32, (CB, CB), 1)
        tri = (c_i <= r_i).astype(BF16)
        bv = b_ref[...]

        def step(k, carry):
            rows = pl.ds(pl.multiple_of(k * CB, CB), CB)
            lf = _log_sigmoid(f_ref[rows, :] + bv)
            c_ref[rows, :] = _tri_dot(tri, lf) + carry
            return carry + jnp.sum(lf, axis=0, keepdims=True)

        lax.fori_loop(0, LP // CB, step, jnp.zeros((1, LANES), F32))

    return pl.pallas_call(
        body, name=name, out_shape=jax.ShapeDtypeStruct((nseq * LP, LANES), F32), grid=(nseq,),
        in_specs=[pl.BlockSpec((LP, LANES), lambda s: (s, F_BLK)), pl.BlockSpec((1, LANES), lambda s: (0, 0))],
        out_specs=pl.BlockSpec((LP, LANES), lambda s: (s, 0)),
        compiler_params=_cparams("parallel"))(proj, bf)


def _gate_bwd(proj, bf, dc, dproj, nseq, name):
    def body(f_ref, b_ref, dc_ref, _, df_ref, db_ref):
        s = pl.program_id(0)
        r_i = lax.broadcasted_iota(jnp.int32, (CB, CB), 0)
        c_i = lax.broadcasted_iota(jnp.int32, (CB, CB), 1)
        tri = (c_i >= r_i).astype(BF16)
        bv = b_ref[...]

        def step(kk, carry):
            carry_c, carry_b = carry
            k = LP // CB - 1 - kk
            rows = pl.ds(pl.multiple_of(k * CB, CB), CB)
            dcv = dc_ref[rows, :]
            dlf = _tri_dot(tri, dcv) + carry_c
            df = dlf * _sigmoid(-(f_ref[rows, :] + bv))
            df_ref[rows, :] = jnp.concatenate([df, jnp.zeros_like(df)], axis=1).astype(BF16)
            return carry_c + jnp.sum(dcv, axis=0, keepdims=True), carry_b + jnp.sum(df, axis=0, keepdims=True)

        zero = jnp.zeros((1, LANES), F32)
        _, dbp = lax.fori_loop(0, LP // CB, step, (zero, zero))

        @pl.when(s == 0)
        def _():
            db_ref[...] = jnp.zeros_like(db_ref)

        db_ref[...] += dbp

    return pl.pallas_call(
        body, name=name,
        out_shape=(jax.ShapeDtypeStruct(dproj.shape, BF16), jax.ShapeDtypeStruct((1, LANES), F32)), grid=(nseq,),
        in_specs=[pl.BlockSpec((LP, LANES), lambda s: (s, F_BLK)), pl.BlockSpec((1, LANES), lambda s: (0, 0)),
                  pl.BlockSpec((LP, LANES), lambda s: (s, 0)), ANY],
        out_specs=(pl.BlockSpec((LP, 2 * LANES), lambda s: (s, F_COL // (2 * LANES))), pl.BlockSpec((1, LANES), lambda s: (0, 0))),
        input_output_aliases={3: 0},
        compiler_params=_cparams("arbitrary"))(proj, bf, dc, dproj)


SCALE = 0.125
NEG = -1e30


def _dot_nt(a, b):
    return lax.dot_general(a, b, (((1,), (1,)), ((), ())), preferred_element_type=F32)


def _dot_tn(a, b):
    return lax.dot_general(a, b, (((0,), (0,)), ((), ())), preferred_element_type=F32)


def _dot(a, b):
    return jnp.dot(a, b, preferred_element_type=F32)


def _blk(i):
    return pl.ds(pl.multiple_of(i * BQ, BQ), BQ)


def _tile_iotas():
    return lax.broadcasted_iota(jnp.int32, (BQ, BQ), 0), lax.broadcasted_iota(jnp.int32, (BQ, BQ), 1)


def _lane_iota():
    return lax.broadcasted_iota(jnp.int32, (BQ, LANES), 1)


def _head_masks():
    lane = _lane_iota()
    return lane < HEAD, lane >= HEAD


def _only(mask, x):
    return jnp.where(mask, x, jnp.zeros_like(x))


def _pick_lane(x, idx):
    return jnp.sum(jnp.where(_lane_iota() == idx, x, 0.0), axis=1, keepdims=True)


def _chains(npair):
    return [(pp, h) for pp in range(npair) for h in range(2)]


def _load_qkv(p_ref, q_s, k_s, v_s):
    for pp in range(q_s.shape[0]):
        base = pp * PAIR_W
        q_s[pp] = (p_ref[:, base:base + LANES] * SCALE).astype(BF16)
        k_s[pp] = p_ref[:, base + LANES:base + 2 * LANES].astype(BF16)
        v_s[pp] = p_ref[:, base + 2 * LANES:base + 3 * LANES].astype(BF16)


def _softplus(z):
    return jnp.maximum(z, 0.0) + jnp.log(1.0 + jnp.exp(-jnp.abs(z)))


def _hi_lo(x):
    hi = x.astype(BF16)
    return hi, (x - hi.astype(F32)).astype(BF16)


def _sb_tile_weights(q, k, strict, r, u_suf):
    n = len(q)
    z = [_dot_nt(q[c], k[c]) for c in range(n)]
    sp = [_softplus(zc) for zc in z]
    lk = [-spc if strict is None else jnp.where(strict, -spc, 0.0) for spc in sp]
    parts = [_hi_lo(lkc) for lkc in lk]
    suf = [_dot(hi, u_suf) + _dot(lo, u_suf) for hi, lo in parts]
    w = [jnp.exp(z[c] - sp[c] + r[c] + suf[c]) for c in range(n)]
    if strict is not None:
        w = [jnp.where(strict, wc, 0.0) for wc in w]
    r_next = [r[c] + suf[c][:, 0:1] + lk[c][:, 0:1] for c in range(n)]
    return w, sp, r_next


def _group_spec(kind, npair):
    return pl.BlockSpec((LP, npair * PAIR_W), lambda s, g: (s, (NH // (2 * npair)) * kind + g))


def _gheads_spec(npair):
    return pl.BlockSpec((LP, npair * LANES), lambda s, g: (s, g))


def _qkv_scratch(npair):
    return [pltpu.VMEM((npair, LP, LANES), BF16)] * 3


SEQ_SPEC = pl.BlockSpec((LP, LANES), lambda s, g: (s, 0))
RS_STRIDE = 16


def _pair_cols(pp):
    return slice(pp * LANES, (pp + 1) * LANES)


def _sb_fwd(proj, nseq, npair, name):
    t = nseq * LP
    chains = _chains(npair)

    def body(p_ref, o_ref, rs_ref, q_s, k_s, v_s, acc_ref, r_ref, rb_ref):
        _load_qkv(p_ref, q_s, k_s, v_s)
        row, col = _tile_iotas()
        u_suf = (row > col).astype(BF16)
        diag = col < row
        lane = _lane_iota()
        heads = _head_masks()

        def qblock(i, _):
            acc_ref[...] = jnp.zeros_like(acc_ref)
            rb_ref[...] = jnp.zeros_like(rb_ref)
            r_ref[...] = jnp.zeros_like(r_ref)
            qb = [q_s[pp, _blk(i), :] for pp in range(npair)]

            def tile(j, strict):
                kj = [k_s[pp, _blk(j), :] for pp in range(npair)]
                vj = [v_s[pp, _blk(j), :] for pp in range(npair)]
                r = [r_ref[c] for c in range(len(chains))]
                w, _, r_next = _sb_tile_weights([_only(heads[h], qb[pp]) for pp, h in chains],
                                                [kj[pp] for pp, _ in chains], strict, r, u_suf)
                pv = [_dot(w[c].astype(BF16), _only(heads[h], vj[pp])) for c, (pp, h) in enumerate(chains)]
                for pp in range(npair):
                    acc_ref[pp] += pv[2 * pp] + pv[2 * pp + 1]
                    rb_ref[pp] = jnp.where(lane == j, r[2 * pp], jnp.where(lane == RS_STRIDE + j, r[2 * pp + 1], rb_ref[pp]))
                for c in range(len(chains)):
                    r_ref[c] = r_next[c]

            tile(i, diag)

            def kblock(jj, _):
                tile(i - jj, None)
                return 0

            lax.fori_loop(1, i + 1, kblock, 0)
            for pp in range(npair):
                o_ref[_blk(i), _pair_cols(pp)] = acc_ref[pp].astype(BF16)
                rs_ref[_blk(i), _pair_cols(pp)] = rb_ref[pp]
            return 0

        lax.fori_loop(0, NBLK, qblock, 0)

    return pl.pallas_call(
        body, name=name,
        out_shape=(jax.ShapeDtypeStruct((t, W_ATT), BF16), jax.ShapeDtypeStruct((t, W_ATT), F32)),
        grid=(nseq, NH // (2 * npair)), in_specs=[_group_spec(0, npair)], out_specs=(_gheads_spec(npair), _gheads_spec(npair)),
        scratch_shapes=_qkv_scratch(npair) + [pltpu.VMEM((npair, BQ, LANES), F32), pltpu.VMEM((2 * npair, BQ, 1), F32),
                                      pltpu.VMEM((npair, BQ, LANES), F32)],
        compiler_params=_cparams("parallel", "parallel"))(proj)


def _sb_bwd(proj, do, rs, dproj, after, nseq, npair, name):
    chains = _chains(npair)

    def body(p_ref, do_ref, rs_ref, _, _after, dp_ref, q_s, k_s, v_s, dqa_ref, dka_ref, dva_ref, ep_ref):
        _load_qkv(p_ref, q_s, k_s, v_s)
        row, col = _tile_iotas()
        u_suf = (row > col).astype(BF16)
        u_pre = (row < col).astype(BF16)
        diag = col < row
        heads = _head_masks()
        dka_ref[...] = jnp.zeros_like(dka_ref)
        dva_ref[...] = jnp.zeros_like(dva_ref)
        nc = len(chains)

        def qblock(i, _):
            rb = [rs_ref[_blk(i), _pair_cols(pp)] for pp in range(npair)]
            qb = [q_s[pp, _blk(i), :] for pp in range(npair)]
            dob = [do_ref[_blk(i), _pair_cols(pp)] for pp in range(npair)]
            dqa_ref[...] = jnp.zeros_like(dqa_ref)
            ep_ref[...] = jnp.zeros_like(ep_ref)

            def tile(j, strict):
                kj = [k_s[pp, _blk(j), :] for pp in range(npair)]
                vj = [v_s[pp, _blk(j), :] for pp in range(npair)]
                q = [_only(heads[h], qb[pp]) for pp, h in chains]
                dov = [_only(heads[h], dob[pp]) for pp, h in chains]
                r = [_pick_lane(rb[pp], RS_STRIDE * h + j) for pp, h in chains]
                dw = [_dot_nt(dov[c], vj[pp]) for c, (pp, _) in enumerate(chains)]
                w, sp, _ = _sb_tile_weights(q, [kj[pp] for pp, _ in chains], strict, r, u_suf)
                e = [dw[c] * w[c] for c in range(nc)]
                e_pre = [ep_ref[c] + _dot(e[c].astype(BF16), u_pre) for c in range(nc)]
                dz = []
                for c in range(nc):
                    ep_ref[c] += jnp.sum(e[c], axis=1, keepdims=True)
                    sneg = jnp.exp(-sp[c])
                    dzc = e[c] * sneg - (1.0 - sneg) * e_pre[c]
                    if strict is not None:
                        dzc = jnp.where(strict, dzc, 0.0)
                    dz.append(dzc.astype(BF16))
                dq = [_dot(dz[c], _only(heads[h], kj[pp])) for c, (pp, h) in enumerate(chains)]
                dk = [_dot_tn(dz[c], q[c]) for c in range(nc)]
                dv = [_dot_tn(w[c].astype(BF16), dov[c]) for c in range(nc)]
                for pp in range(npair):
                    dqa_ref[pp] += dq[2 * pp] + dq[2 * pp + 1]
                    dka_ref[pp, _blk(j), :] += dk[2 * pp] + dk[2 * pp + 1]
                    dva_ref[pp, _blk(j), :] += dv[2 * pp] + dv[2 * pp + 1]

            def kblock(j, _):
                tile(j, None)
                return 0

            lax.fori_loop(0, i, kblock, 0)
            tile(i, diag)
            for pp in range(npair):
                dp_ref[_blk(i), pp * PAIR_W:pp * PAIR_W + LANES] = (dqa_ref[pp] * SCALE).astype(BF16)
            return 0

        lax.fori_loop(0, NBLK, qblock, 0)
        for pp in range(npair):
            dp_ref[:, pp * PAIR_W + LANES:pp * PAIR_W + 2 * LANES] = dka_ref[pp].astype(BF16)
            dp_ref[:, pp * PAIR_W + 2 * LANES:pp * PAIR_W + 3 * LANES] = dva_ref[pp].astype(BF16)

    return pl.pallas_call(
        body, name=name, out_shape=jax.ShapeDtypeStruct(dproj.shape, BF16), grid=(nseq, NH // (2 * npair)),
        in_specs=[_group_spec(0, npair), _gheads_spec(npair), _gheads_spec(npair), ANY, ANY], out_specs=_group_spec(0, npair),
        input_output_aliases={3: 0},
        scratch_shapes=_qkv_scratch(npair) + [pltpu.VMEM((npair, BQ, LANES), F32), pltpu.VMEM((npair, LP, LANES), F32),
                                      pltpu.VMEM((npair, LP, LANES), F32), pltpu.VMEM((2 * npair, BQ, 1), F32)],
        compiler_params=_cparams("parallel", "parallel"))(proj, do, rs, dproj, after)


CROW_SPEC = pl.BlockSpec((None, NH, LP), lambda s, g: (s, 0, 0))


def _fox_scores(qi, kj, cq, ck, causal):
    z = _dot_nt(qi, kj) + (cq - ck)
    return z if causal is None else jnp.where(causal, z, NEG)


def _key_cols(cr_ref, head, j):
    return cr_ref[pl.ds(head, 1), pl.ds(pl.multiple_of(j * BQ, BQ), BQ)]


def _fox_fwd(proj, c, crow, nseq, npair, name):
    t = nseq * LP
    chains = _chains(npair)

    def body(p_ref, c_ref, cr_ref, o_ref, o32_ref, lse_ref, q_s, k_s, v_s, acc_ref, m_ref, l_ref):
        _load_qkv(p_ref, q_s, k_s, v_s)
        row, col = _tile_iotas()
        diag = col <= row
        lane = _lane_iota()
        heads = _head_masks()
        head0 = 2 * npair * pl.program_id(1)
        nc = len(chains)

        def qblock(i, _):
            cblk = c_ref[_blk(i), :]
            qb = [q_s[pp, _blk(i), :] for pp in range(npair)]
            cq = [_pick_lane(cblk, head0 + c) for c in range(nc)]
            acc_ref[...] = jnp.zeros_like(acc_ref)
            m_ref[...] = jnp.full_like(m_ref, NEG)
            l_ref[...] = jnp.zeros_like(l_ref)

            def tile(j, causal):
                kj = [k_s[pp, _blk(j), :] for pp in range(npair)]
                vj = [v_s[pp, _blk(j), :] for pp in range(npair)]
                z = [_fox_scores(_only(heads[h], qb[pp]), kj[pp], cq[c], _key_cols(cr_ref, head0 + c, j), causal)
                     for c, (pp, h) in enumerate(chains)]
                p, alpha = [], []
                for c in range(nc):
                    m_old = m_ref[c]
                    m_new = jnp.maximum(m_old, jnp.max(z[c], axis=1, keepdims=True))
                    alpha.append(jnp.exp(m_old - m_new))
                    pc = jnp.exp(z[c] - m_new)
                    l_ref[c] = alpha[c] * l_ref[c] + jnp.sum(pc, axis=1, keepdims=True)
                    m_ref[c] = m_new
                    p.append(pc.astype(BF16))
                pv = [_dot(p[c], _only(heads[h], vj[pp])) for c, (pp, h) in enumerate(chains)]
                for c in range(nc):
                    acc_ref[c] = alpha[c] * acc_ref[c] + pv[c]

            def kblock(j, _):
                tile(j, None)
                return 0

            lax.fori_loop(0, i, kblock, 0)
            tile(i, diag)
            for pp in range(npair):
                out = acc_ref[2 * pp] / l_ref[2 * pp] + acc_ref[2 * pp + 1] / l_ref[2 * pp + 1]
                o_ref[_blk(i), _pair_cols(pp)] = out.astype(BF16)
                o32_ref[_blk(i), _pair_cols(pp)] = out
                lse = [m_ref[2 * pp + h] + jnp.log(l_ref[2 * pp + h]) for h in range(2)]
                lse_ref[_blk(i), _pair_cols(pp)] = jnp.where(lane == 0, lse[0], jnp.where(lane == 1, lse[1], 0.0))
            return 0

        lax.fori_loop(0, NBLK, qblock, 0)

    return pl.pallas_call(
        body, name=name,
        out_shape=(jax.ShapeDtypeStruct((t, W_ATT), BF16), jax.ShapeDtypeStruct((t, W_ATT), F32),
                   jax.ShapeDtypeStruct((t, W_ATT), F32)),
        grid=(nseq, NH // (2 * npair)), in_specs=[_group_spec(1, npair), SEQ_SPEC, CROW_SPEC], out_specs=(_gheads_spec(npair), _gheads_spec(npair), _gheads_spec(npair)),
        scratch_shapes=_qkv_scratch(npair) + [pltpu.VMEM((2 * npair, BQ, LANES), F32), pltpu.VMEM((2 * npair, BQ, 1), F32),
                                      pltpu.VMEM((2 * npair, BQ, 1), F32)],
        compiler_params=_cparams("parallel", "parallel"))(proj, c, crow)


def _fox_bwd(proj, c, crow, o32, lse, do, dproj, nseq, npair, name):
    t = nseq * LP
    chains = _chains(npair)

    def body(p_ref, c_ref, cr_ref, o_ref, lse_ref, do_ref, _, dp_ref, dck_ref, dcq_ref,
             q_s, k_s, v_s, dqa_ref, dka_ref, dva_ref, rsum_ref):
        _load_qkv(p_ref, q_s, k_s, v_s)
        row, col = _tile_iotas()
        diag = col <= row
        lane = _lane_iota()
        heads = _head_masks()
        sub = lax.broadcasted_iota(jnp.int32, (NH, BQ), 0)
        group = pl.program_id(1)
        head0 = 2 * npair * group
        nc = len(chains)
        dka_ref[...] = jnp.zeros_like(dka_ref)
        dva_ref[...] = jnp.zeros_like(dva_ref)

        @pl.when(group == 0)
        def _():
            dck_ref[...] = jnp.zeros_like(dck_ref)
            dcq_ref[...] = jnp.zeros_like(dcq_ref)

        def qblock(i, _):
            dqa_ref[...] = jnp.zeros_like(dqa_ref)
            rsum_ref[...] = jnp.zeros_like(rsum_ref)
            cblk = c_ref[_blk(i), :]
            qb = [q_s[pp, _blk(i), :] for pp in range(npair)]
            dob = [do_ref[_blk(i), _pair_cols(pp)] for pp in range(npair)]
            prod = [dob[pp].astype(F32) * o_ref[_blk(i), _pair_cols(pp)] for pp in range(npair)]
            cq = [_pick_lane(cblk, head0 + c) for c in range(nc)]
            lse_i = [_pick_lane(lse_ref[_blk(i), _pair_cols(pp)], h) for pp, h in chains]
            delta = [jnp.sum(_only(heads[h], prod[pp]), axis=1, keepdims=True) for pp, h in chains]

            def tile(j, causal):
                kj = [k_s[pp, _blk(j), :] for pp in range(npair)]
                vj = [v_s[pp, _blk(j), :] for pp in range(npair)]
                keys = pl.ds(pl.multiple_of(j * BQ, BQ), BQ)
                q = [_only(heads[h], qb[pp]) for pp, h in chains]
                dov = [_only(heads[h], dob[pp]) for pp, h in chains]
                z = [_fox_scores(q[c], kj[pp], cq[c], _key_cols(cr_ref, head0 + c, j), causal)
                     for c, (pp, _) in enumerate(chains)]
                dpv = [_dot_nt(dov[c], vj[pp]) for c, (pp, _) in enumerate(chains)]
                p = [jnp.exp(z[c] - lse_i[c]) for c in range(nc)]
                ds = [p[c] * (dpv[c] - delta[c]) for c in range(nc)]
                dsb = [d.astype(BF16) for d in ds]
                dq = [_dot(dsb[c], _only(heads[h], kj[pp])) for c, (pp, h) in enumerate(chains)]
                dk = [_dot_tn(dsb[c], q[c]) for c in range(nc)]
                dv = [_dot_tn(p[c].astype(BF16), dov[c]) for c in range(nc)]
                for pp in range(npair):
                    dqa_ref[pp] += dq[2 * pp] + dq[2 * pp + 1]
                    dka_ref[pp, _blk(j), :] += dk[2 * pp] + dk[2 * pp + 1]
                    dva_ref[pp, _blk(j), :] += dv[2 * pp] + dv[2 * pp + 1]
                col_sums = jnp.zeros((NH, BQ), F32)
                for c in range(nc):
                    col_sums = col_sums + jnp.where(sub == head0 + c, jnp.sum(ds[c], axis=0, keepdims=True), 0.0)
                    rsum_ref[c] += jnp.sum(ds[c], axis=1, keepdims=True)
                dck_ref[:, keys] = dck_ref[:, keys] - col_sums

            def kblock(j, _):
                tile(j, None)
                return 0

            lax.fori_loop(0, i, kblock, 0)
            tile(i, diag)
            row_sums = jnp.zeros((BQ, LANES), F32)
            for c in range(nc):
                row_sums = row_sums + jnp.where(lane == head0 + c, rsum_ref[c], 0.0)
            dcq_ref[_blk(i), :] += row_sums
            for pp in range(npair):
                dp_ref[_blk(i), pp * PAIR_W:pp * PAIR_W + LANES] = (dqa_ref[pp] * SCALE).astype(BF16)
            return 0

        lax.fori_loop(0, NBLK, qblock, 0)
        for pp in range(npair):
            dp_ref[:, pp * PAIR_W + LANES:pp * PAIR_W + 2 * LANES] = dka_ref[pp].astype(BF16)
            dp_ref[:, pp * PAIR_W + 2 * LANES:pp * PAIR_W + 3 * LANES] = dva_ref[pp].astype(BF16)

    return pl.pallas_call(
        body, name=name,
        out_shape=(jax.ShapeDtypeStruct(dproj.shape, BF16), jax.ShapeDtypeStruct((nseq, NH, LP), F32),
                   jax.ShapeDtypeStruct((t, LANES), F32)),
        grid=(nseq, NH // (2 * npair)),
        in_specs=[_group_spec(1, npair), SEQ_SPEC, CROW_SPEC, _gheads_spec(npair), _gheads_spec(npair), _gheads_spec(npair), ANY],
        out_specs=(_group_spec(1, npair), CROW_SPEC, SEQ_SPEC),
        input_output_aliases={6: 0},
        scratch_shapes=_qkv_scratch(npair) + [pltpu.VMEM((npair, BQ, LANES), F32), pltpu.VMEM((npair, LP, LANES), F32),
                                      pltpu.VMEM((npair, LP, LANES), F32), pltpu.VMEM((2 * npair, BQ, 1), F32)],
        compiler_params=_cparams("parallel", "arbitrary"))(proj, c, crow, o32, lse, do, dproj)


def _adamw_math(w, g, m, v):
    m = B1 * m + (1.0 - B1) * g
    v = B2 * v + (1.0 - B2) * (g * g)
    m_hat = m / (1.0 - B1 ** STEP)
    v_hat = v / (1.0 - B2 ** STEP)
    delta = -LR * (m_hat / (jnp.sqrt(v_hat) + EPS) + WD * w)
    return delta, m, v


def _sum_adamw(parts, w, m, v, tr, name):
    rows, cols = w.shape
    cp = parts.shape[2]
    assert rows % tr == 0 and parts.shape[1] == rows

    def body(p_ref, w_ref, m_ref, v_ref, g_ref, d_ref, nm_ref, nv_ref):
        gsum = p_ref[0].astype(F32)
        for s in range(1, N_DEV):
            gsum = gsum + p_ref[s].astype(F32)
        gsum = gsum[:, :cols]
        d, nm, nv = _adamw_math(w_ref[...], gsum, m_ref[...], v_ref[...])
        g_ref[...] = gsum
        d_ref[...] = d
        nm_ref[...] = nm
        nv_ref[...] = nv

    blk = pl.BlockSpec((tr, cols), lambda i: (i, 0))
    out = jax.ShapeDtypeStruct((rows, cols), F32)
    return pl.pallas_call(
        body, name=name, out_shape=(out, out, out, out), grid=(rows // tr,),
        in_specs=[pl.BlockSpec((N_DEV, tr, cp), lambda i: (0, i, 0)), blk, blk, blk],
        out_specs=(blk, blk, blk, blk), compiler_params=_cparams("parallel"))(parts, w, m, v)


def _local_step(x, tgt, meta, g_mix, w_in_p, b_forget, g_ffn, g_final, late_weights, early_grads):
    nseq = x.shape[0]
    t = nseq * LP
    tm = LP // 2
    mm = functools.partial(_matmul, tm=tm)

    h0 = _pad_rows(meta, x, nseq, "pad_x").reshape(t, D)
    tgt_p = _pad_rows(jnp.zeros((N_META, D), F32), tgt, nseq, "pad_target").reshape(t, D)
    bf = jnp.pad(b_forget.reshape(1, NH), ((0, 0), (0, LANES - NH)))

    _, n1 = _norm_fwd(h0, None, g_mix, "norm1")
    proj = mm(n1, w_in_p, out_dtype=F32, tn=1792, tk=D, name="in_proj")
    c = _gate_fwd(proj, bf, nseq, "gate_fwd")
    crow = c[:, :NH].reshape(nseq, LP, NH).transpose(0, 2, 1)
    o_sb, rs = _sb_fwd(proj, nseq, 2, "sb_fwd")
    o_fx, o_fx32, lse = _fox_fwd(proj, c, crow, nseq, 1, "fox_fwd")
    w_bsb, w_bfx, w_out, w_up_i, cw_i, w_down = late_weights(o_fx)
    p_sb = mm(o_sb, w_bsb, out_dtype=F32, tn=D, tk=W_ATT, name="branch_sb")
    p_fx = mm(o_fx, w_bfx, out_dtype=F32, tn=D, tk=W_ATT, name="branch_fox")
    merged = _merge_fwd(p_sb, p_fx, proj, "merge_fwd")
    mix = mm(merged, w_out, out_dtype=F32, tn=D, tk=D, name="out_proj")
    h1, n2 = _norm_fwd(h0, mix, g_ffn, "norm2")
    u = mm(n2, w_up_i, out_dtype=F32, tn=1408, tk=D, name="up_proj")
    act = _conv_glu_fwd(u, cw_i, nseq, "conv_glu_fwd")
    ffn = mm(act, w_down, out_dtype=F32, tn=D, tk=1408, name="down_proj")

    loss, dh2, dh2b, dg_final = _final_loss_bwd(h1, ffn, g_final, tgt_p, "final")
    d_down = _matmul(act, dh2b, out_dtype=BF16, tm=1408, tn=D, tk=tm, ta=True, name="d_w_down")
    dact = mm(dh2b, w_down, out_dtype=F32, tn=1408, tk=D, tb=True, name="d_act")
    du, d_cw = _conv_glu_bwd(u, cw_i, dact, nseq, "conv_glu_bwd")
    d_up = _matmul(n2, du, out_dtype=BF16, tm=D, tn=1408, tk=tm, ta=True, name="d_w_up")
    dn2 = mm(du, w_up_i, out_dtype=F32, tn=D, tk=1408, tb=True, name="d_n2")
    dh1, dh1b, dg_ffn = _norm_bwd(h1, dn2, g_ffn, dh2, True, "norm2_bwd")
    d_out = _matmul(merged, dh1b, out_dtype=BF16, tm=D, tn=D, tk=tm, ta=True, name="d_w_out")
    dmerged = mm(dh1b, w_out, out_dtype=F32, tn=D, tk=D, tb=True, name="d_merged")
    dp_sb, dproj = _merge_bwd(dmerged, p_sb, proj, None, 0, "merge_bwd_sb")
    dp_fx, dproj = _merge_bwd(dmerged, p_fx, proj, dproj, 1, "merge_bwd_fox")
    d_bsb = _matmul(o_sb, dp_sb, out_dtype=BF16, tm=W_ATT, tn=D, tk=tm, ta=True, name="d_w_branch_sb")
    d_bfx = _matmul(o_fx, dp_fx, out_dtype=BF16, tm=W_ATT, tn=D, tk=tm, ta=True, name="d_w_branch_fox")
    do_sb = mm(dp_sb, w_bsb, out_dtype=BF16, tn=W_ATT, tk=D, tb=True, name="d_o_sb")
    do_fx = mm(dp_fx, w_bfx, out_dtype=BF16, tn=W_ATT, tk=D, tb=True, name="d_o_fox")
    sent = early_grads(dict(w_branch_sb=d_bsb, w_branch_fox=d_bfx, w_out=d_out, w_up=d_up, conv_w=d_cw, w_down=d_down))
    dproj = _sb_bwd(proj, do_sb, rs, dproj, sent, nseq, 2, "sb_bwd")
    dproj, dck, dcq = _fox_bwd(proj, c, crow, o_fx32, lse, do_fx, dproj, nseq, 2, "fox_bwd")
    dc = dcq + jnp.pad(dck.transpose(0, 2, 1).reshape(t, NH), ((0, 0), (0, LANES - NH)))
    dproj, d_bf = _gate_bwd(proj, bf, dc, dproj, nseq, "gate_bwd")
    d_in = _matmul(n1, dproj, out_dtype=BF16, tm=D, tn=1792, tk=tm, ta=True, name="d_w_in")
    dn1 = mm(dproj, w_in_p, out_dtype=F32, tn=D, tk=1792, tb=True, name="d_n1")
    dh0, dg_mix = _norm_bwd(h0, dn1, g_mix, dh1, False, "norm1_bwd")
    dh0 = dh0.reshape(nseq, LP, D)
    grads = dict(meta_tokens=jnp.sum(dh0[:, :N_META], axis=0), norm_mix_g=dg_mix, w_in=d_in, b_forget=d_bf[:, :NH],
                 norm_ffn_g=dg_ffn, norm_final_g=dg_final)
    return loss[0, 0], _real_rows(dh0, nseq, "grad_x"), grads


REPL = (("norm_mix_g", D), ("norm_ffn_g", D), ("norm_final_g", D), ("b_forget", LANES))
REPL_ROWS = 32


def _pack_repl(tree):
    rows = [jnp.pad(tree[name].reshape(-1), (0, n - tree[name].size)).reshape(-1, LANES) for name, n in REPL]
    packed = jnp.concatenate(rows, axis=0)
    return jnp.pad(packed, ((0, REPL_ROWS - packed.shape[0]), (0, 0)))


def _unpack_repl(packed, shapes):
    out, r = {}, 0
    for name, n in REPL:
        size = 1
        for s in shapes[name]:
            size *= s
        out[name] = packed[r:r + n // LANES].reshape(-1)[:size].reshape(shapes[name])
        r += n // LANES
    return out


def kernel(x, meta_tokens, norm_mix_g, w_in, b_forget, w_branch_sb, w_branch_fox, w_out, norm_ffn_g, w_up, conv_w, w_down, norm_final_g, loss_target, m_meta_tokens, m_norm_mix_g, m_w_in, m_b_forget, m_w_branch_sb, m_w_branch_fox, m_w_out, m_norm_ffn_g, m_w_up, m_conv_w, m_w_down, m_norm_final_g, v_meta_tokens, v_norm_mix_g, v_w_in, v_b_forget, v_w_branch_sb, v_w_branch_fox, v_w_out, v_norm_ffn_g, v_w_up, v_conv_w, v_w_down, v_norm_final_g):
    w = dict(meta_tokens=meta_tokens, norm_mix_g=norm_mix_g, w_in=w_in, b_forget=b_forget, w_branch_sb=w_branch_sb,
             w_branch_fox=w_branch_fox, w_out=w_out, norm_ffn_g=norm_ffn_g, w_up=w_up, conv_w=conv_w, w_down=w_down,
             norm_final_g=norm_final_g)
    m = dict(meta_tokens=m_meta_tokens, norm_mix_g=m_norm_mix_g, w_in=m_w_in, b_forget=m_b_forget,
             w_branch_sb=m_w_branch_sb, w_branch_fox=m_w_branch_fox, w_out=m_w_out, norm_ffn_g=m_norm_ffn_g,
             w_up=m_w_up, conv_w=m_conv_w, w_down=m_w_down, norm_final_g=m_norm_final_g)
    v = dict(meta_tokens=v_meta_tokens, norm_mix_g=v_norm_mix_g, w_in=v_w_in, b_forget=v_b_forget,
             w_branch_sb=v_w_branch_sb, w_branch_fox=v_w_branch_fox, w_out=v_w_out, norm_ffn_g=v_norm_ffn_g,
             w_up=v_w_up, conv_w=v_conv_w, w_down=v_w_down, norm_final_g=v_norm_final_g)
    shapes = {k: a.shape for k, a in w.items()}
    sharded = ("w_in", "w_branch_sb", "w_branch_fox", "w_out", "w_up", "w_down", "conv_w", "meta_tokens")
    mat = lambda tree, name: tree[name].reshape(tree[name].shape[-2:])

    def lane_pad(a, width):
        return jnp.pad(a, ((0, 0), (0, width - a.shape[1])))

    late = ("w_branch_sb", "w_branch_fox", "w_out", "w_up", "w_down", "conv_w")
    pending_w, started = _remote_start(
        [mat(w, "w_branch_sb").astype(BF16), mat(w, "w_branch_fox").astype(BF16), mat(w, "w_out").astype(BF16),
         lane_pad(mat(w, "w_up").astype(BF16), SHARD_P), mat(w, "w_down").astype(BF16), mat(w, "conv_w")],
        False, "gather_late_start")
    g_in, g_meta = _all_gather([lane_pad(mat(w, "w_in").astype(BF16), SHARD_P), mat(w, "meta_tokens")], "gather_w_in")
    w_in_p = _relayout(g_in, 1, IN_P, _gathered_to_full(IN_SHARD, _in_padded_to_orig), BF16, 256, "w_in_cols")[0]
    meta_full = g_meta.transpose(1, 0, 2).reshape(N_META, D)

    def late_weights(after):
        g_bsb, g_bfx, g_out, g_up, g_down, g_cw = _remote_wait(pending_w, after, "gather_late_wait")
        w_up_i = _relayout(g_up, 1, 2 * D_FF, _gathered_to_full(UP_SHARD, _up_inter_to_orig), BF16, 256, "w_up_cols")[0]
        w_bsb = _relayout(g_bsb, 1, D, _gathered_to_full(ATT_SHARD, lambda d: d), BF16, 256, "w_bsb_cols")[0]
        w_bfx = _relayout(g_bfx, 1, D, _gathered_to_full(ATT_SHARD, lambda d: d), BF16, 256, "w_bfx_cols")[0]
        cw_full = g_cw.transpose(1, 0, 2).reshape(3, 2 * D_FF)
        cw_i = cw_full.reshape(3, 2, D_FF // FFC, FFC).transpose(0, 2, 1, 3).reshape(3, 2 * D_FF)
        return w_bsb, w_bfx, g_out.reshape(D, D), w_up_i, cw_i, g_down.reshape(D_FF, D)

    pending_g = []

    def early_grads(g):
        d_cw = g["conv_w"].reshape(3, D_FF // FFC, 2, FFC).transpose(0, 2, 1, 3).reshape(3, 2 * D_FF)
        pending, sent = _remote_start(
            [_relayout(g["w_branch_sb"][None], N_DEV, ATT_SHARD, _full_to_shards(ATT_SHARD, lambda c: c), BF16, 256, "d_w_bsb_shards"),
             _relayout(g["w_branch_fox"][None], N_DEV, ATT_SHARD, _full_to_shards(ATT_SHARD, lambda c: c), BF16, 256, "d_w_bfx_shards"),
             g["w_out"].reshape(N_DEV, D // N_DEV, D),
             _relayout(g["w_up"][None], N_DEV, SHARD_P, _full_to_shards(UP_SHARD, _UP_ORIG_TO_INTER.get), BF16, 256, "d_w_up_shards"),
             g["w_down"].reshape(N_DEV, D_FF // N_DEV, D),
             d_cw.reshape(3, N_DEV, UP_SHARD).transpose(1, 0, 2)], True, "exchange_early_start")
        pending_g.append(pending)
        return sent

    loss, grad_x, grads = _local_step(
        x, loss_target, meta_full, norm_mix_g.reshape(1, D) + started[0, 0], w_in_p, b_forget,
        norm_ffn_g.reshape(1, D), norm_final_g.reshape(1, D), late_weights, early_grads)

    p_in, p_meta = _exchange(
        [_relayout(grads["w_in"][None], N_DEV, SHARD_P, _full_to_shards(IN_SHARD, _IN_ORIG_TO_PADDED.get), BF16, 256, "d_w_in_shards"),
         grads["meta_tokens"].reshape(N_META, N_DEV, ATT_SHARD).transpose(1, 0, 2)], "exchange_grads")
    parts = dict(zip(late, _remote_wait(pending_g[0], p_in, "exchange_early_wait")), w_in=p_in, meta_tokens=p_meta)
    tiles = dict(w_in=256, w_branch_sb=256, w_branch_fox=256, w_out=D // N_DEV, w_up=256, w_down=D_FF // N_DEV,
                 conv_w=3, meta_tokens=N_META)
    new = {name: _sum_adamw(parts[name], mat(w, name), mat(m, name), mat(v, name), tiles[name], "adamw_" + name)
           for name in sharded}

    rparts, = _all_gather([_pack_repl(grads)], "gather_replicated_grads")
    routs = _sum_adamw(rparts, _pack_repl(w), _pack_repl(m), _pack_repl(v), REPL_ROWS, "adamw_replicated")
    repl = [_unpack_repl(o, shapes) for o in routs]

    result = [lax.psum(loss, ("x", "y", "c")), grad_x]
    for k in range(4):
        for name in w:
            result.append(new[name][k].reshape(shapes[name]) if name in new else repl[k][name])
    return tuple(result)
```

```python
import functools

import jax
import jax.numpy as jnp
from jax import lax
from jax.experimental import pallas as pl
from jax.experimental.pallas import tpu as pltpu

F32 = jnp.float32
BF16 = jnp.bfloat16

N_DEV = 8
LANES = 128
D = 1024
N_META = 16
SEQ = 2048
L_REAL = N_META + SEQ
LP = 2304
BQ = 256
NBLK = LP // BQ
HEAD = 64
NH = 8
W_ATT = NH * HEAD
PAIR_W = 3 * LANES
D_FF = 2816
IN_COLS = 5128
QKV = 6 * W_ATT
IN_P = 5376
GATE_COL = QKV
F_COL = QKV + 2 * D
FFC = 256
RMS_EPS = 1e-6
LR, B1, B2, EPS, WD, STEP = 0.001, 0.9, 0.999, 1e-08, 0.01, 10
VMEM_LIMIT = 56 * 1024 * 1024

MESH = pl.DeviceIdType.MESH
ANY = pl.BlockSpec(memory_space=pl.ANY)


def _cparams(*sem):
    return pltpu.CompilerParams(dimension_semantics=sem if sem else None, vmem_limit_bytes=VMEM_LIMIT)


def _all_gather(xs, name):
    n = len(xs)

    def body(*refs):
        x_refs, out_refs = refs[:n], refs[n:2 * n]
        send_sems, recv_sems, local_sems = refs[2 * n:]
        mx, my, mc = lax.axis_index("x"), lax.axis_index("y"), lax.axis_index("c")
        me, sibling = (mx, my, mc), (mx, my, 1 - mc)
        chips = [(1 - mx, my), (mx, 1 - my), (1 - mx, 1 - my)]

        def copy(a, k, block, to, own=False):
            px, py, pc = block
            slot = out_refs[a].at[4 * px + 2 * py + pc]
            return pltpu.make_async_remote_copy(
                src_ref=x_refs[a] if own else slot, dst_ref=slot,
                send_sem=send_sems.at[7 * a + k], recv_sem=recv_sems.at[7 * a + k],
                device_id=to, device_id_type=MESH)

        mine = [pltpu.make_async_copy(x_refs[a], out_refs[a].at[4 * mx + 2 * my + mc], local_sems.at[a]) for a in range(n)]
        for cp in mine:
            cp.start()
        first = []
        for a in range(n):
            first.append(copy(a, 0, me, sibling, own=True))
            first += [copy(a, 1 + j, me, (*chip, mc), own=True) for j, chip in enumerate(chips)]
        for cp in first:
            cp.start()
        passed = []
        for j, chip in enumerate(chips):
            for a in range(n):
                copy(a, 1 + j, (*chip, mc), me).wait_recv()
                fwd = copy(a, 4 + j, (*chip, mc), sibling)
                fwd.start()
                passed.append(fwd)
        for a in range(n):
            copy(a, 0, sibling, me).wait_recv()
            for j, chip in enumerate(chips):
                copy(a, 4 + j, (*chip, 1 - mc), me).wait_recv()
        for cp in first + passed:
            cp.wait_send()
        for cp in mine:
            cp.wait()

    return pl.pallas_call(
        body, name=name,
        out_shape=tuple(jax.ShapeDtypeStruct((N_DEV,) + x.shape, x.dtype) for x in xs),
        in_specs=[ANY] * n, out_specs=tuple([ANY] * n),
        scratch_shapes=[pltpu.SemaphoreType.DMA((7 * n,)), pltpu.SemaphoreType.DMA((7 * n,)),
                        pltpu.SemaphoreType.DMA((n,))],
    )(*xs)


HBM = pl.BlockSpec(memory_space=pltpu.HBM)
SEM = pl.BlockSpec(memory_space=pltpu.SEMAPHORE)
EFFECT = pltpu.SideEffectType.DATAFLOW_SIDE_EFFECTING


def _peer_copies(src_refs, land_refs, send_sems, recv_sems, per_peer):
    mx, my, mc = lax.axis_index("x"), lax.axis_index("y"), lax.axis_index("c")
    me_idx = 4 * mx + 2 * my + mc
    copies = []
    for k in range(1, N_DEV):
        px, py, pc = mx ^ (k >> 2), my ^ ((k >> 1) & 1), mc ^ (k & 1)
        for a, (src, land) in enumerate(zip(src_refs, land_refs)):
            copies.append(pltpu.make_async_remote_copy(
                src_ref=src.at[4 * px + 2 * py + pc] if per_peer else src, dst_ref=land.at[me_idx],
                send_sem=send_sems.at[7 * a + k - 1], recv_sem=recv_sems.at[7 * a + k - 1],
                device_id=(px, py, pc), device_id_type=MESH))
    return me_idx, copies


def _remote_start(srcs, per_peer, after, name):
    n = len(srcs)
    lands = [lax.empty(s.shape if per_peer else (N_DEV,) + s.shape, s.dtype) for s in srcs]

    def body(*refs):
        src_refs, land_refs = refs[:n], refs[n:2 * n]
        send_sems, recv_sems = refs[2 * n + 1:2 * n + 3]
        token = refs[4 * n + 3]
        stage, local_sems = refs[4 * n + 4:5 * n + 4], refs[5 * n + 4]
        me_idx, copies = _peer_copies(src_refs, land_refs, send_sems, recv_sems, per_peer)
        for cp in copies:
            cp.start()
        own = [src_refs[a].at[me_idx] if per_peer else src_refs[a] for a in range(n)]
        for hop in ([(own[a], stage[a]) for a in range(n)], [(stage[a], land_refs[a].at[me_idx]) for a in range(n)]):
            cps = [pltpu.make_async_copy(s, d, local_sems.at[a]) for a, (s, d) in enumerate(hop)]
            for cp in cps:
                cp.start()
            for cp in cps:
                cp.wait()
        token[...] = jnp.zeros_like(token)

    thru = [pltpu.HBM(a.shape, a.dtype) for a in list(srcs) + lands]
    out = pl.pallas_call(
        body, name=name,
        out_shape=(pltpu.SemaphoreType.DMA((7 * n,)), pltpu.SemaphoreType.DMA((7 * n,)), *thru,
                   jax.ShapeDtypeStruct((8, LANES), F32)),
        in_specs=[HBM] * (2 * n) + [ANY],
        out_specs=(SEM, SEM, *([HBM] * (2 * n)), pl.BlockSpec(memory_space=pltpu.VMEM)),
        input_output_aliases={i: 2 + i for i in range(2 * n)},
        scratch_shapes=[pltpu.VMEM(s.shape[1:] if per_peer else s.shape, s.dtype) for s in srcs]
        + [pltpu.SemaphoreType.DMA((n,))],
        compiler_params=pltpu.CompilerParams(has_side_effects=EFFECT),
    )(*[pltpu.with_memory_space_constraint(a, pltpu.HBM) for a in list(srcs) + lands], after)
    return dict(sems=out[:2], bufs=out[2:2 * n + 2], per_peer=per_peer), out[-1]


def _remote_wait(pending, after, name):
    bufs = pending["bufs"]
    n = len(bufs) // 2
    per_peer = pending["per_peer"]

    def body(*refs):
        src_refs, land_refs = refs[:n], refs[n:2 * n]
        send_sems, recv_sems = refs[2 * n:2 * n + 2]
        _, copies = _peer_copies(src_refs, land_refs, send_sems, recv_sems, per_peer)
        for cp in copies:
            cp.wait_send()
        for cp in copies:
            cp.wait_recv()

    out = pl.pallas_call(
        body, name=name, out_shape=tuple(pltpu.HBM(a.shape, a.dtype) for a in bufs),
        in_specs=[HBM] * (2 * n) + [SEM, SEM, ANY], out_specs=tuple([HBM] * (2 * n)),
        input_output_aliases={i: i for i in range(2 * n)},
        compiler_params=pltpu.CompilerParams(has_side_effects=EFFECT),
    )(*bufs, *pending["sems"], after)
    return out[n:]


ROWS_PER_COPY = 256


def _pad_rows(front, body_rows, nseq, name):
    tail = LP - L_REAL
    nblk = SEQ // ROWS_PER_COPY

    def body(f_ref, b_ref, o_ref, z_ref, sems):
        s, i = pl.program_id(0), pl.program_id(1)
        rows = pltpu.make_async_copy(b_ref, o_ref.at[pl.ds(s, 1), pl.ds(N_META + i * ROWS_PER_COPY, ROWS_PER_COPY)], sems.at[0])
        rows.start()

        @pl.when(i == 0)
        def _():
            z_ref[...] = jnp.zeros_like(z_ref)
            head = pltpu.make_async_copy(f_ref, o_ref.at[s, pl.ds(0, N_META)], sems.at[1])
            zeros = pltpu.make_async_copy(z_ref, o_ref.at[s, pl.ds(L_REAL, tail)], sems.at[2])
            head.start()
            zeros.start()
            head.wait()
            zeros.wait()

        rows.wait()

    return pl.pallas_call(
        body, name=name, out_shape=jax.ShapeDtypeStruct((nseq, LP, D), F32), grid=(nseq, nblk),
        in_specs=[pl.BlockSpec((N_META, D), lambda s, i: (0, 0)), pl.BlockSpec((1, ROWS_PER_COPY, D), lambda s, i: (s, i, 0))],
        out_specs=ANY,
        scratch_shapes=[pltpu.VMEM((tail, D), F32), pltpu.SemaphoreType.DMA((3,))],
        compiler_params=_cparams("arbitrary", "arbitrary"))(front, body_rows)


def _real_rows(h, nseq, name):
    nblk = SEQ // ROWS_PER_COPY

    def body(h_ref, o_ref, sem):
        s, i = pl.program_id(0), pl.program_id(1)
        rows = pltpu.make_async_copy(h_ref.at[pl.ds(s, 1), pl.ds(N_META + i * ROWS_PER_COPY, ROWS_PER_COPY)], o_ref, sem)
        rows.start()
        rows.wait()

    return pl.pallas_call(
        body, name=name, out_shape=jax.ShapeDtypeStruct((nseq, SEQ, D), F32), grid=(nseq, nblk),
        in_specs=[ANY], out_specs=pl.BlockSpec((1, ROWS_PER_COPY, D), lambda s, i: (s, i, 0)),
        scratch_shapes=[pltpu.SemaphoreType.DMA],
        compiler_params=_cparams("arbitrary", "arbitrary"))(h)


def _plan_cols(n_q, n_dcols, src_of):
    plan = {}
    for q in range(n_q):
        for dblk in range(n_dcols // LANES):
            segs, key, start = [], None, 0
            for lane in range(LANES + 1):
                new = None
                if lane < LANES:
                    src = src_of(q, dblk * LANES + lane)
                    if src is not None:
                        new = (src[0], src[1] // LANES, (lane - src[1] % LANES) % LANES)
                if new != key:
                    if key is not None:
                        segs.append((*key, start, lane))
                    key, start = new, lane
            plan[(q, dblk)] = segs
    return plan


def _relayout(src, n_q, n_dcols, src_of, out_dtype, tr, name):
    n_p, rows, scols = src.shape
    plan = _plan_cols(n_q, n_dcols, src_of)

    def body(s_ref, d_ref):
        lane = lax.broadcasted_iota(jnp.int32, (tr, LANES), 1)
        for (q, dblk), segs in plan.items():
            acc = jnp.zeros((tr, LANES), F32)
            for p, sblk, rot, lo, hi in segs:
                x = s_ref[p, :, sblk * LANES:(sblk + 1) * LANES].astype(F32)
                if rot:
                    x = pltpu.roll(x, rot, 1)
                acc = x if (lo, hi) == (0, LANES) else jnp.where((lane >= lo) & (lane < hi), x, acc)
            d_ref[q, :, dblk * LANES:(dblk + 1) * LANES] = acc.astype(out_dtype)

    return pl.pallas_call(
        body, name=name, out_shape=jax.ShapeDtypeStruct((n_q, rows, n_dcols), out_dtype), grid=(rows // tr,),
        in_specs=[pl.BlockSpec((n_p, tr, scols), lambda i: (0, i, 0))],
        out_specs=pl.BlockSpec((n_q, tr, n_dcols), lambda i: (0, i, 0)),
        compiler_params=_cparams("parallel"))(src)


def _in_padded_to_orig(d):
    if d < QKV:
        kind, r = divmod(d, 4 * PAIR_W)
        pair, r = divmod(r, PAIR_W)
        part, r = divmod(r, LANES)
        return kind * 3 * W_ATT + part * W_ATT + pair * LANES + r
    if d < F_COL:
        return d + NH
    if d < F_COL + NH:
        return d - 2 * D
    return None


_IN_ORIG_TO_PADDED = {_in_padded_to_orig(d): d for d in range(IN_P) if _in_padded_to_orig(d) is not None}


def _up_inter_to_orig(d):
    j, r = divmod(d, 2 * FFC)
    part, r = divmod(r, FFC)
    return part * D_FF + j * FFC + r


_UP_ORIG_TO_INTER = {_up_inter_to_orig(d): d for d in range(2 * D_FF)}
IN_SHARD = IN_COLS // N_DEV
UP_SHARD = 2 * D_FF // N_DEV
SHARD_P = 768
ATT_SHARD = D // N_DEV


def _gathered_to_full(n_shard, to_orig):
    def src_of(q, d):
        c = to_orig(d)
        return None if c is None else (c // n_shard, c % n_shard)
    return src_of


def _full_to_shards(n_shard, from_orig):
    def src_of(q, d):
        return (0, from_orig(q * n_shard + d)) if d < n_shard else None
    return src_of


def _matmul(a, b, *, out_dtype, tm, tn, tk, ta=False, tb=False, after=None, name):
    if ta:
        kdim, m = a.shape
    else:
        m, kdim = a.shape
    n = b.shape[0] if tb else b.shape[1]
    assert m % tm == 0 and n % tn == 0 and kdim % tk == 0, (name, a.shape, b.shape, tm, tn, tk)
    nk = kdim // tk

    def body(a_ref, b_ref, *rest):
        o_ref, scratch = rest[len(extra)], rest[len(extra) + 1:]
        av, bv = a_ref[...], b_ref[...]
        if ta:
            p = lax.dot_general(av, bv, (((0,), (0,)), ((), ())), preferred_element_type=F32)
        elif tb:
            p = lax.dot_general(av, bv, (((1,), (1,)), ((), ())), preferred_element_type=F32)
        else:
            p = jnp.dot(av, bv, preferred_element_type=F32)
        if nk == 1:
            o_ref[...] = p.astype(o_ref.dtype)
        else:
            acc_ref, = scratch
            k = pl.program_id(2)

            @pl.when(k == 0)
            def _():
                acc_ref[...] = p

            @pl.when(k > 0)
            def _():
                acc_ref[...] += p

            @pl.when(k == nk - 1)
            def _():
                o_ref[...] = acc_ref[...].astype(o_ref.dtype)

    extra = [] if after is None else [after]
    a_spec = pl.BlockSpec((tk, tm), lambda i, j, k: (k, i)) if ta else pl.BlockSpec((tm, tk), lambda i, j, k: (i, k))
    b_spec = pl.BlockSpec((tn, tk), lambda i, j, k: (j, k)) if tb else pl.BlockSpec((tk, tn), lambda i, j, k: (k, j))
    return pl.pallas_call(
        body, name=name,
        out_shape=jax.ShapeDtypeStruct((m, n), out_dtype),
        grid=(m // tm, n // tn, nk),
        in_specs=[a_spec, b_spec] + [ANY] * len(extra),
        out_specs=pl.BlockSpec((tm, tn), lambda i, j, k: (i, j)),
        scratch_shapes=[] if nk == 1 else [pltpu.VMEM((tm, tn), F32)],
        compiler_params=_cparams("parallel", "parallel", "arbitrary"),
    )(a, b, *extra)


TR = 288


def _rms(h):
    return lax.rsqrt(jnp.mean(h * h, axis=-1, keepdims=True) + RMS_EPS)


def _norm_fwd(h, delta, g, name):
    t = h.shape[0]
    row = pl.BlockSpec((TR, D), lambda i: (i, 0))
    vec = pl.BlockSpec((1, D), lambda i: (0, 0))

    if delta is None:
        def body(h_ref, g_ref, n_ref):
            hv = h_ref[...]
            n_ref[...] = ((hv * _rms(hv)) * g_ref[...]).astype(BF16)

        n = pl.pallas_call(
            body, name=name, out_shape=jax.ShapeDtypeStruct((t, D), BF16), grid=(t // TR,),
            in_specs=[row, vec], out_specs=row, compiler_params=_cparams("parallel"))(h, g)
        return h, n

    def body(h_ref, d_ref, g_ref, hn_ref, n_ref):
        hv = h_ref[...] + d_ref[...]
        hn_ref[...] = hv
        n_ref[...] = ((hv * _rms(hv)) * g_ref[...]).astype(BF16)

    return pl.pallas_call(
        body, name=name,
        out_shape=(jax.ShapeDtypeStruct((t, D), F32), jax.ShapeDtypeStruct((t, D), BF16)), grid=(t // TR,),
        in_specs=[row, row, vec], out_specs=(row, row), compiler_params=_cparams("parallel"))(h, delta, g)


def _rms_bwd_math(hv, dn, gv):
    r = _rms(hv)
    hr = hv * r
    dng = dn * gv
    dh = r * (dng - hr * jnp.mean(dng * hr, axis=-1, keepdims=True))
    return dh, dn * hr


def _final_loss_bwd(h1, delta, g, tgt, name):
    t = h1.shape[0]
    row = pl.BlockSpec((TR, D), lambda i: (i, 0))
    vec = pl.BlockSpec((1, D), lambda i: (0, 0))
    tiles_per_seq = LP // TR

    def body(h_ref, d_ref, g_ref, t_ref, loss_ref, dh_ref, dhb_ref, dg_ref):
        i = pl.program_id(0)
        hv = h_ref[...] + d_ref[...]
        gv = g_ref[...]
        r = _rms(hv)
        hr = hv * r
        y = hr * gv
        pos = (i % tiles_per_seq) * TR + lax.broadcasted_iota(jnp.int32, (TR, 1), 0)
        valid = (pos >= N_META) & (pos < L_REAL)
        err = jnp.where(valid, y - t_ref[...], 0.0)
        part = 0.5 * jnp.sum(jnp.mean(err * err, axis=-1, keepdims=True))
        dy = err * (1.0 / D)
        dng = dy * gv
        dh = r * (dng - hr * jnp.mean(dng * hr, axis=-1, keepdims=True))
        dh_ref[...] = dh
        dhb_ref[...] = dh.astype(BF16)
        dgp = jnp.sum(dy * hr, axis=0, keepdims=True)

        @pl.when(i == 0)
        def _():
            loss_ref[...] = jnp.zeros_like(loss_ref)
            dg_ref[...] = jnp.zeros_like(dg_ref)

        loss_ref[...] += part
        dg_ref[...] += dgp

    return pl.pallas_call(
        body, name=name,
        out_shape=(jax.ShapeDtypeStruct((8, 128), F32), jax.ShapeDtypeStruct((t, D), F32),
                   jax.ShapeDtypeStruct((t, D), BF16), jax.ShapeDtypeStruct((1, D), F32)),
        grid=(t // TR,),
        in_specs=[row, row, vec, row],
        out_specs=(pl.BlockSpec((8, 128), lambda i: (0, 0)), row, row, vec),
        compiler_params=_cparams("arbitrary"))(h1, delta, g, tgt)


def _norm_bwd(h, dn, g, dres, with_bf16, name):
    t = h.shape[0]
    row = pl.BlockSpec((TR, D), lambda i: (i, 0))
    vec = pl.BlockSpec((1, D), lambda i: (0, 0))

    def body(h_ref, dn_ref, g_ref, dres_ref, *outs):
        i = pl.program_id(0)
        dh, dgrow = _rms_bwd_math(h_ref[...], dn_ref[...], g_ref[...])
        dh = dh + dres_ref[...]
        outs[0][...] = dh
        if with_bf16:
            outs[1][...] = dh.astype(BF16)
        dg_ref = outs[-1]

        @pl.when(i == 0)
        def _():
            dg_ref[...] = jnp.zeros_like(dg_ref)

        dg_ref[...] += jnp.sum(dgrow, axis=0, keepdims=True)

    shapes = [jax.ShapeDtypeStruct((t, D), F32)]
    specs = [row]
    if with_bf16:
        shapes.append(jax.ShapeDtypeStruct((t, D), BF16))
        specs.append(row)
    shapes.append(jax.ShapeDtypeStruct((1, D), F32))
    specs.append(vec)
    return pl.pallas_call(
        body, name=name, out_shape=tuple(shapes), grid=(t // TR,),
        in_specs=[row, row, vec, row], out_specs=tuple(specs),
        compiler_params=_cparams("arbitrary"))(h, dn, g, dres)


GATE_BLK = GATE_COL // D


def _sigmoid(x):
    return 1.0 / (1.0 + jnp.exp(-x))


def _merge_fwd(p_sb, p_fx, proj, name):
    t = p_sb.shape[0]
    row = pl.BlockSpec((TR, D), lambda i: (i, 0))

    def body(ps_ref, pf_ref, gs_ref, gf_ref, o_ref):
        o_ref[...] = (_sigmoid(gs_ref[...]) * ps_ref[...] + _sigmoid(gf_ref[...]) * pf_ref[...]).astype(BF16)

    return pl.pallas_call(
        body, name=name, out_shape=jax.ShapeDtypeStruct((t, D), BF16), grid=(t // TR,),
        in_specs=[row, row, pl.BlockSpec((TR, D), lambda i: (i, GATE_BLK)),
                  pl.BlockSpec((TR, D), lambda i: (i, GATE_BLK + 1))],
        out_specs=row, compiler_params=_cparams("parallel"))(p_sb, p_fx, proj, proj)


def _merge_bwd(dm, p, proj, dproj, which, name):
    t = dm.shape[0]
    row = pl.BlockSpec((TR, D), lambda i: (i, 0))
    gate = pl.BlockSpec((TR, D), lambda i: (i, GATE_BLK + which))

    def body(dm_ref, p_ref, g_ref, *rest):
        dp_ref, dg_ref = rest[-2:]
        dmv = dm_ref[...]
        s = _sigmoid(g_ref[...])
        dp_ref[...] = (dmv * s).astype(BF16)
        dg_ref[...] = (dmv * p_ref[...] * s * (1.0 - s)).astype(BF16)

    out_shape = (jax.ShapeDtypeStruct((t, D), BF16), jax.ShapeDtypeStruct((t, IN_P), BF16))
    if dproj is None:
        return pl.pallas_call(
            body, name=name, out_shape=out_shape, grid=(t // TR,), in_specs=[row, row, gate],
            out_specs=(row, gate), compiler_params=_cparams("parallel"))(dm, p, proj)
    return pl.pallas_call(
        body, name=name, out_shape=out_shape, grid=(t // TR,), in_specs=[row, row, gate, ANY],
        out_specs=(row, gate), input_output_aliases={3: 1}, compiler_params=_cparams("parallel"))(dm, p, proj, dproj)


CH = 288


def _chunk(c, n=CH):
    return pl.ds(pl.multiple_of(c * CH, 8), n)


def _conv_taps(u_ref, c):
    x = u_ref[_chunk(c), :]
    prev = u_ref[pl.ds(pl.multiple_of(jnp.maximum(c * CH - 8, 0), 8), 8), :]
    xx = jnp.concatenate([jnp.where(c == 0, 0.0, prev), x], axis=0)
    return x, pltpu.roll(xx, 1, 0)[8:], pltpu.roll(xx, 2, 0)[8:]


def _conv_glu_fwd(u, cw, nseq, name):
    nblk = D_FF // FFC

    def body(u_ref, cw_ref, o_ref):
        cwv = cw_ref[...]

        def step(c, _):
            x, x1, x2 = _conv_taps(u_ref, c)
            uc = cwv[0:1, :] * x2 + cwv[1:2, :] * x1 + cwv[2:3, :] * x
            a, b = uc[:, :FFC], uc[:, FFC:]
            o_ref[_chunk(c), :] = (a * _sigmoid(a) * b).astype(BF16)
            return 0

        lax.fori_loop(0, LP // CH, step, 0)

    return pl.pallas_call(
        body, name=name, out_shape=jax.ShapeDtypeStruct((nseq * LP, D_FF), BF16), grid=(nseq, nblk),
        in_specs=[pl.BlockSpec((LP, 2 * FFC), lambda s, j: (s, j)), pl.BlockSpec((3, 2 * FFC), lambda s, j: (0, j))],
        out_specs=pl.BlockSpec((LP, FFC), lambda s, j: (s, j)),
        compiler_params=_cparams("parallel", "parallel"))(u, cw)


def _conv_glu_bwd(u, cw, dact, nseq, name):
    nblk = D_FF // FFC
    nch = LP // CH

    def body(u_ref, cw_ref, da_ref, du_ref, dcw_ref):
        s = pl.program_id(1)
        cwv = cw_ref[...]

        def step(k, carry):
            nxt, p0, p1, p2 = carry
            c = nch - 1 - k
            x, x1, x2 = _conv_taps(u_ref, c)
            uc = cwv[0:1, :] * x2 + cwv[1:2, :] * x1 + cwv[2:3, :] * x
            a, b = uc[:, :FFC], uc[:, FFC:]
            sa = _sigmoid(a)
            dactv = da_ref[_chunk(c), :]
            da = dactv * b * (sa * (1.0 + a * (1.0 - sa)))
            db = dactv * (a * sa)
            duc = jnp.concatenate([da, db], axis=1)
            dd = jnp.concatenate([duc, nxt], axis=0)
            du = (cwv[2:3, :] * duc + cwv[1:2, :] * pltpu.roll(dd, CH + 7, 0)[:CH]
                  + cwv[0:1, :] * pltpu.roll(dd, CH + 6, 0)[:CH])
            du_ref[_chunk(c), :] = du.astype(BF16)
            return (duc[:8], p0 + jnp.sum(duc * x2, axis=0, keepdims=True),
                    p1 + jnp.sum(duc * x1, axis=0, keepdims=True), p2 + jnp.sum(duc * x, axis=0, keepdims=True))

        zrow = jnp.zeros((1, 2 * FFC), F32)
        _, p0, p1, p2 = lax.fori_loop(0, nch, step, (jnp.zeros((8, 2 * FFC), F32), zrow, zrow, zrow))

        @pl.when(s == 0)
        def _():
            dcw_ref[...] = jnp.zeros_like(dcw_ref)

        dcw_ref[...] += jnp.concatenate([p0, p1, p2], axis=0)

    return pl.pallas_call(
        body, name=name,
        out_shape=(jax.ShapeDtypeStruct((nseq * LP, 2 * D_FF), BF16), jax.ShapeDtypeStruct((3, 2 * D_FF), F32)),
        grid=(nblk, nseq),
        in_specs=[pl.BlockSpec((LP, 2 * FFC), lambda j, s: (s, j)), pl.BlockSpec((3, 2 * FFC), lambda j, s: (0, j)),
                  pl.BlockSpec((LP, FFC), lambda j, s: (s, j))],
        out_specs=(pl.BlockSpec((LP, 2 * FFC), lambda j, s: (s, j)), pl.BlockSpec((3, 2 * FFC), lambda j, s: (0, j))),
        compiler_params=_cparams("parallel", "arbitrary"))(u, cw, dact)


F_BLK = F_COL // LANES
CB = 128


def _split3(x):
    hi = x.astype(BF16)
    r1 = x - hi.astype(F32)
    mid = r1.astype(BF16)
    lo = (r1 - mid.astype(F32)).astype(BF16)
    return hi, mid, lo


def _tri_dot(tri, x):
    hi, mid, lo = _split3(x)
    d = functools.partial(jnp.dot, preferred_element_type=F32)
    return d(tri, hi) + d(tri, mid) + d(tri, lo)


def _log_sigmoid(x):
    return jnp.minimum(x, 0.0) - jnp.log(1.0 + jnp.exp(-jnp.abs(x)))


def _gate_fwd(proj, bf, nseq, name):
    def body(f_ref, b_ref, c_ref):
        r_i = lax.broadcasted_iota(jnp.int32, (CB, CB), 0)
        c_i = lax.broadcasted_iota(jnp.int32, (CB, CB), 1)
        tri = (c_i <= r_i).astype(BF16)
        bv = b_ref[...]

        def step(k, carry):
            rows = pl.ds(pl.multiple_of(k * CB, CB), CB)
            lf = _log_sigmoid(f_ref[rows, :] + bv)
            c_ref[rows, :] = _tri_dot(tri, lf) + carry
            return carry + jnp.sum(lf, axis=0, keepdims=True)

        lax.fori_loop(0, LP // CB, step, jnp.zeros((1, LANES), F32))

    return pl.pallas_call(
        body, name=name, out_shape=jax.ShapeDtypeStruct((nseq * LP, LANES), F32), grid=(nseq,),
        in_specs=[pl.BlockSpec((LP, LANES), lambda s: (s, F_BLK)), pl.BlockSpec((1, LANES), lambda s: (0, 0))],
        out_specs=pl.BlockSpec((LP, LANES), lambda s: (s, 0)),
        compiler_params=_cparams("parallel"))(proj, bf)


def _gate_bwd(proj, bf, dc, dproj, nseq, name):
    def body(f_ref, b_ref, dc_ref, _, df_ref, db_ref):
        s = pl.program_id(0)
        r_i = lax.broadcasted_iota(jnp.int32, (CB, CB), 0)
        c_i = lax.broadcasted_iota(jnp.int32, (CB, CB), 1)
        tri = (c_i >= r_i).astype(BF16)
        bv = b_ref[...]

        def step(kk, carry):
            carry_c, carry_b = carry
            k = LP // CB - 1 - kk
            rows = pl.ds(pl.multiple_of(k * CB, CB), CB)
            dcv = dc_ref[rows, :]
            dlf = _tri_dot(tri, dcv) + carry_c
            df = dlf * _sigmoid(-(f_ref[rows, :] + bv))
            df_ref[rows, :] = jnp.concatenate([df, jnp.zeros_like(df)], axis=1).astype(BF16)
            return carry_c + jnp.sum(dcv, axis=0, keepdims=True), carry_b + jnp.sum(df, axis=0, keepdims=True)

        zero = jnp.zeros((1, LANES), F32)
        _, dbp = lax.fori_loop(0, LP // CB, step, (zero, zero))

        @pl.when(s == 0)
        def _():
            db_ref[...] = jnp.zeros_like(db_ref)

        db_ref[...] += dbp

    return pl.pallas_call(
        body, name=name,
        out_shape=(jax.ShapeDtypeStruct(dproj.shape, BF16), jax.ShapeDtypeStruct((1, LANES), F32)), grid=(nseq,),
        in_specs=[pl.BlockSpec((LP, LANES), lambda s: (s, F_BLK)), pl.BlockSpec((1, LANES), lambda s: (0, 0)),
                  pl.BlockSpec((LP, LANES), lambda s: (s, 0)), ANY],
        out_specs=(pl.BlockSpec((LP, 2 * LANES), lambda s: (s, F_COL // (2 * LANES))), pl.BlockSpec((1, LANES), lambda s: (0, 0))),
        input_output_aliases={3: 0},
        compiler_params=_cparams("arbitrary"))(proj, bf, dc, dproj)


SCALE = 0.125
NEG = -1e30


def _dot_nt(a, b):
    return lax.dot_general(a, b, (((1,), (1,)), ((), ())), preferred_element_type=F32)


def _dot_tn(a, b):
    return lax.dot_general(a, b, (((0,), (0,)), ((), ())), preferred_element_type=F32)


def _dot(a, b):
    return jnp.dot(a, b, preferred_element_type=F32)


def _blk(i):
    return pl.ds(pl.multiple_of(i * BQ, BQ), BQ)


def _tile_iotas():
    return lax.broadcasted_iota(jnp.int32, (BQ, BQ), 0), lax.broadcasted_iota(jnp.int32, (BQ, BQ), 1)


def _lane_iota():
    return lax.broadcasted_iota(jnp.int32, (BQ, LANES), 1)


def _head_masks():
    lane = _lane_iota()
    return lane < HEAD, lane >= HEAD


def _only(mask, x):
    return jnp.where(mask, x, jnp.zeros_like(x))


def _pick_lane(x, idx):
    return jnp.sum(jnp.where(_lane_iota() == idx, x, 0.0), axis=1, keepdims=True)


def _chains(npair):
    return [(pp, h) for pp in range(npair) for h in range(2)]


def _load_qkv(p_ref, q_s, k_s, v_s):
    for pp in range(q_s.shape[0]):
        base = pp * PAIR_W
        q_s[pp] = (p_ref[:, base:base + LANES] * SCALE).astype(BF16)
        k_s[pp] = p_ref[:, base + LANES:base + 2 * LANES].astype(BF16)
        v_s[pp] = p_ref[:, base + 2 * LANES:base + 3 * LANES].astype(BF16)


def _softplus(z):
    return jnp.maximum(z, 0.0) + jnp.log(1.0 + jnp.exp(-jnp.abs(z)))


def _hi_lo(x):
    hi = x.astype(BF16)
    return hi, (x - hi.astype(F32)).astype(BF16)


def _sb_tile_weights(q, k, strict, r, u_suf):
    n = len(q)
    z = [_dot_nt(q[c], k[c]) for c in range(n)]
    sp = [_softplus(zc) for zc in z]
    lk = [-spc if strict is None else jnp.where(strict, -spc, 0.0) for spc in sp]
    parts = [_hi_lo(lkc) for lkc in lk]
    suf = [_dot(hi, u_suf) + _dot(lo, u_suf) for hi, lo in parts]
    w = [jnp.exp(z[c] - sp[c] + r[c] + suf[c]) for c in range(n)]
    if strict is not None:
        w = [jnp.where(strict, wc, 0.0) for wc in w]
    r_next = [r[c] + suf[c][:, 0:1] + lk[c][:, 0:1] for c in range(n)]
    return w, sp, r_next


def _group_spec(kind, npair):
    return pl.BlockSpec((LP, npair * PAIR_W), lambda s, g: (s, (NH // (2 * npair)) * kind + g))


def _gheads_spec(npair):
    return pl.BlockSpec((LP, npair * LANES), lambda s, g: (s, g))


def _qkv_scratch(npair):
    return [pltpu.VMEM((npair, LP, LANES), BF16)] * 3


SEQ_SPEC = pl.BlockSpec((LP, LANES), lambda s, g: (s, 0))
RS_STRIDE = 16


def _pair_cols(pp):
    return slice(pp * LANES, (pp + 1) * LANES)


def _sb_fwd(proj, nseq, npair, name):
    t = nseq * LP
    chains = _chains(npair)

    def body(p_ref, o_ref, rs_ref, q_s, k_s, v_s, acc_ref, r_ref, rb_ref):
        _load_qkv(p_ref, q_s, k_s, v_s)
        row, col = _tile_iotas()
        u_suf = (row > col).astype(BF16)
        diag = col < row
        lane = _lane_iota()
        heads = _head_masks()

        def qblock(i, _):
            acc_ref[...] = jnp.zeros_like(acc_ref)
            rb_ref[...] = jnp.zeros_like(rb_ref)
            r_ref[...] = jnp.zeros_like(r_ref)
            qb = [q_s[pp, _blk(i), :] for pp in range(npair)]

            def tile(j, strict):
                kj = [k_s[pp, _blk(j), :] for pp in range(npair)]
                vj = [v_s[pp, _blk(j), :] for pp in range(npair)]
                r = [r_ref[c] for c in range(len(chains))]
                w, _, r_next = _sb_tile_weights([_only(heads[h], qb[pp]) for pp, h in chains],
                                                [kj[pp] for pp, _ in chains], strict, r, u_suf)
                pv = [_dot(w[c].astype(BF16), _only(heads[h], vj[pp])) for c, (pp, h) in enumerate(chains)]
                for pp in range(npair):
                    acc_ref[pp] += pv[2 * pp] + pv[2 * pp + 1]
                    rb_ref[pp] = jnp.where(lane == j, r[2 * pp], jnp.where(lane == RS_STRIDE + j, r[2 * pp + 1], rb_ref[pp]))
                for c in range(len(chains)):
                    r_ref[c] = r_next[c]

            tile(i, diag)

            def kblock(jj, _):
                tile(i - jj, None)
                return 0

            lax.fori_loop(1, i + 1, kblock, 0)
            for pp in range(npair):
                o_ref[_blk(i), _pair_cols(pp)] = acc_ref[pp].astype(BF16)
                rs_ref[_blk(i), _pair_cols(pp)] = rb_ref[pp]
            return 0

        lax.fori_loop(0, NBLK, qblock, 0)

    return pl.pallas_call(
        body, name=name,
        out_shape=(jax.ShapeDtypeStruct((t, W_ATT), BF16), jax.ShapeDtypeStruct((t, W_ATT), F32)),
        grid=(nseq, NH // (2 * npair)), in_specs=[_group_spec(0, npair)], out_specs=(_gheads_spec(npair), _gheads_spec(npair)),
        scratch_shapes=_qkv_scratch(npair) + [pltpu.VMEM((npair, BQ, LANES), F32), pltpu.VMEM((2 * npair, BQ, 1), F32),
                                      pltpu.VMEM((npair, BQ, LANES), F32)],
        compiler_params=_cparams("parallel", "parallel"))(proj)


def _sb_bwd(proj, do, rs, dproj, after, nseq, npair, name):
    chains = _chains(npair)

    def body(p_ref, do_ref, rs_ref, _, _after, dp_ref, q_s, k_s, v_s, dqa_ref, dka_ref, dva_ref, ep_ref):
        _load_qkv(p_ref, q_s, k_s, v_s)
        row, col = _tile_iotas()
        u_suf = (row > col).astype(BF16)
        u_pre = (row < col).astype(BF16)
        diag = col < row
        heads = _head_masks()
        dka_ref[...] = jnp.zeros_like(dka_ref)
        dva_ref[...] = jnp.zeros_like(dva_ref)
        nc = len(chains)

        def qblock(i, _):
            rb = [rs_ref[_blk(i), _pair_cols(pp)] for pp in range(npair)]
            qb = [q_s[pp, _blk(i), :] for pp in range(npair)]
            dob = [do_ref[_blk(i), _pair_cols(pp)] for pp in range(npair)]
            dqa_ref[...] = jnp.zeros_like(dqa_ref)
            ep_ref[...] = jnp.zeros_like(ep_ref)

            def tile(j, strict):
                kj = [k_s[pp, _blk(j), :] for pp in range(npair)]
                vj = [v_s[pp, _blk(j), :] for pp in range(npair)]
                q = [_only(heads[h], qb[pp]) for pp, h in chains]
                dov = [_only(heads[h], dob[pp]) for pp, h in chains]
                r = [_pick_lane(rb[pp], RS_STRIDE * h + j) for pp, h in chains]
                dw = [_dot_nt(dov[c], vj[pp]) for c, (pp, _) in enumerate(chains)]
                w, sp, _ = _sb_tile_weights(q, [kj[pp] for pp, _ in chains], strict, r, u_suf)
                e = [dw[c] * w[c] for c in range(nc)]
                e_pre = [ep_ref[c] + _dot(e[c].astype(BF16), u_pre) for c in range(nc)]
                dz = []
                for c in range(nc):
                    ep_ref[c] += jnp.sum(e[c], axis=1, keepdims=True)
                    sneg = jnp.exp(-sp[c])
                    dzc = e[c] * sneg - (1.0 - sneg) * e_pre[c]
                    if strict is not None:
                        dzc = jnp.where(strict, dzc, 0.0)
                    dz.append(dzc.astype(BF16))
                dq = [_dot(dz[c], _only(heads[h], kj[pp])) for c, (pp, h) in enumerate(chains)]
                dk = [_dot_tn(dz[c], q[c]) for c in range(nc)]
                dv = [_dot_tn(w[c].astype(BF16), dov[c]) for c in range(nc)]
                for pp in range(npair):
                    dqa_ref[pp] += dq[2 * pp] + dq[2 * pp + 1]
                    dka_ref[pp, _blk(j), :] += dk[2 * pp] + dk[2 * pp + 1]
                    dva_ref[pp, _blk(j), :] += dv[2 * pp] + dv[2 * pp + 1]

            def kblock(j, _):
                tile(j, None)
                return 0

            lax.fori_loop(0, i, kblock, 0)
            tile(i, diag)
            for pp in range(npair):
                dp_ref[_blk(i), pp * PAIR_W:pp * PAIR_W + LANES] = (dqa_ref[pp] * SCALE).astype(BF16)
            return 0

        lax.fori_loop(0, NBLK, qblock, 0)
        for pp in range(npair):
            dp_ref[:, pp * PAIR_W + LANES:pp * PAIR_W + 2 * LANES] = dka_ref[pp].astype(BF16)
            dp_ref[:, pp * PAIR_W + 2 * LANES:pp * PAIR_W + 3 * LANES] = dva_ref[pp].astype(BF16)

    return pl.pallas_call(
        body, name=name, out_shape=jax.ShapeDtypeStruct(dproj.shape, BF16), grid=(nseq, NH // (2 * npair)),
        in_specs=[_group_spec(0, npair), _gheads_spec(npair), _gheads_spec(npair), ANY, ANY], out_specs=_group_spec(0, npair),
        input_output_aliases={3: 0},
        scratch_shapes=_qkv_scratch(npair) + [pltpu.VMEM((npair, BQ, LANES), F32), pltpu.VMEM((npair, LP, LANES), F32),
                                      pltpu.VMEM((npair, LP, LANES), F32), pltpu.VMEM((2 * npair, BQ, 1), F32)],
        compiler_params=_cparams("parallel", "parallel"))(proj, do, rs, dproj, after)


CROW_SPEC = pl.BlockSpec((None, NH, LP), lambda s, g: (s, 0, 0))


def _fox_scores(qi, kj, cq, ck, causal):
    z = _dot_nt(qi, kj) + (cq - ck)
    return z if causal is None else jnp.where(causal, z, NEG)


def _key_cols(cr_ref, head, j):
    return cr_ref[pl.ds(head, 1), pl.ds(pl.multiple_of(j * BQ, BQ), BQ)]


def _fox_fwd(proj, c, crow, nseq, npair, name):
    t = nseq * LP
    chains = _chains(npair)

    def body(p_ref, c_ref, cr_ref, o_ref, o32_ref, lse_ref, q_s, k_s, v_s, acc_ref, m_ref, l_ref):
        _load_qkv(p_ref, q_s, k_s, v_s)
        row, col = _tile_iotas()
        diag = col <= row
        lane = _lane_iota()
        heads = _head_masks()
        head0 = 2 * npair * pl.program_id(1)
        nc = len(chains)

        def qblock(i, _):
            cblk = c_ref[_blk(i), :]
            qb = [q_s[pp, _blk(i), :] for pp in range(npair)]
            cq = [_pick_lane(cblk, head0 + c) for c in range(nc)]
            acc_ref[...] = jnp.zeros_like(acc_ref)
            m_ref[...] = jnp.full_like(m_ref, NEG)
            l_ref[...] = jnp.zeros_like(l_ref)

            def tile(j, causal):
                kj = [k_s[pp, _blk(j), :] for pp in range(npair)]
                vj = [v_s[pp, _blk(j), :] for pp in range(npair)]
                z = [_fox_scores(_only(heads[h], qb[pp]), kj[pp], cq[c], _key_cols(cr_ref, head0 + c, j), causal)
                     for c, (pp, h) in enumerate(chains)]
                p, alpha = [], []
                for c in range(nc):
                    m_old = m_ref[c]
                    m_new = jnp.maximum(m_old, jnp.max(z[c], axis=1, keepdims=True))
                    alpha.append(jnp.exp(m_old - m_new))
                    pc = jnp.exp(z[c] - m_new)
                    l_ref[c] = alpha[c] * l_ref[c] + jnp.sum(pc, axis=1, keepdims=True)
                    m_ref[c] = m_new
                    p.append(pc.astype(BF16))
                pv = [_dot(p[c], _only(heads[h], vj[pp])) for c, (pp, h) in enumerate(chains)]
                for c in range(nc):
                    acc_ref[c] = alpha[c] * acc_ref[c] + pv[c]

            def kblock(j, _):
                tile(j, None)
                return 0

            lax.fori_loop(0, i, kblock, 0)
            tile(i, diag)
            for pp in range(npair):
                out = acc_ref[2 * pp] / l_ref[2 * pp] + acc_ref[2 * pp + 1] / l_ref[2 * pp + 1]
                o_ref[_blk(i), _pair_cols(pp)] = out.astype(BF16)
                o32_ref[_blk(i), _pair_cols(pp)] = out
                lse = [m_ref[2 * pp + h] + jnp.log(l_ref[2 * pp + h]) for h in range(2)]
                lse_ref[_blk(i), _pair_cols(pp)] = jnp.where(lane == 0, lse[0], jnp.where(lane == 1, lse[1], 0.0))
            return 0

        lax.fori_loop(0, NBLK, qblock, 0)

    return pl.pallas_call(
        body, name=name,
        out_shape=(jax.ShapeDtypeStruct((t, W_ATT), BF16), jax.ShapeDtypeStruct((t, W_ATT), F32),
                   jax.ShapeDtypeStruct((t, W_ATT), F32)),
        grid=(nseq, NH // (2 * npair)), in_specs=[_group_spec(1, npair), SEQ_SPEC, CROW_SPEC], out_specs=(_gheads_spec(npair), _gheads_spec(npair), _gheads_spec(npair)),
        scratch_shapes=_qkv_scratch(npair) + [pltpu.VMEM((2 * npair, BQ, LANES), F32), pltpu.VMEM((2 * npair, BQ, 1), F32),
                                      pltpu.VMEM((2 * npair, BQ, 1), F32)],
        compiler_params=_cparams("parallel", "parallel"))(proj, c, crow)


def _fox_bwd(proj, c, crow, o32, lse, do, dproj, nseq, npair, name):
    t = nseq * LP
    chains = _chains(npair)

    def body(p_ref, c_ref, cr_ref, o_ref, lse_ref, do_ref, _, dp_ref, dck_ref, dcq_ref,
             q_s, k_s, v_s, dqa_ref, dka_ref, dva_ref, rsum_ref):
        _load_qkv(p_ref, q_s, k_s, v_s)
        row, col = _tile_iotas()
        diag = col <= row
        lane = _lane_iota()
        heads = _head_masks()
        sub = lax.broadcasted_iota(jnp.int32, (NH, BQ), 0)
        group = pl.program_id(1)
        head0 = 2 * npair * group
        nc = len(chains)
        dka_ref[...] = jnp.zeros_like(dka_ref)
        dva_ref[...] = jnp.zeros_like(dva_ref)

        @pl.when(group == 0)
        def _():
            dck_ref[...] = jnp.zeros_like(dck_ref)
            dcq_ref[...] = jnp.zeros_like(dcq_ref)

        def qblock(i, _):
            dqa_ref[...] = jnp.zeros_like(dqa_ref)
            rsum_ref[...] = jnp.zeros_like(rsum_ref)
            cblk = c_ref[_blk(i), :]
            qb = [q_s[pp, _blk(i), :] for pp in range(npair)]
            dob = [do_ref[_blk(i), _pair_cols(pp)] for pp in range(npair)]
            prod = [dob[pp].astype(F32) * o_ref[_blk(i), _pair_cols(pp)] for pp in range(npair)]
            cq = [_pick_lane(cblk, head0 + c) for c in range(nc)]
            lse_i = [_pick_lane(lse_ref[_blk(i), _pair_cols(pp)], h) for pp, h in chains]
            delta = [jnp.sum(_only(heads[h], prod[pp]), axis=1, keepdims=True) for pp, h in chains]

            def tile(j, causal):
                kj = [k_s[pp, _blk(j), :] for pp in range(npair)]
                vj = [v_s[pp, _blk(j), :] for pp in range(npair)]
                keys = pl.ds(pl.multiple_of(j * BQ, BQ), BQ)
                q = [_only(heads[h], qb[pp]) for pp, h in chains]
                dov = [_only(heads[h], dob[pp]) for pp, h in chains]
                z = [_fox_scores(q[c], kj[pp], cq[c], _key_cols(cr_ref, head0 + c, j), causal)
                     for c, (pp, _) in enumerate(chains)]
                dpv = [_dot_nt(dov[c], vj[pp]) for c, (pp, _) in enumerate(chains)]
                p = [jnp.exp(z[c] - lse_i[c]) for c in range(nc)]
                ds = [p[c] * (dpv[c] - delta[c]) for c in range(nc)]
                dsb = [d.astype(BF16) for d in ds]
                dq = [_dot(dsb[c], _only(heads[h], kj[pp])) for c, (pp, h) in enumerate(chains)]
                dk = [_dot_tn(dsb[c], q[c]) for c in range(nc)]
                dv = [_dot_tn(p[c].astype(BF16), dov[c]) for c in range(nc)]
                for pp in range(npair):
                    dqa_ref[pp] += dq[2 * pp] + dq[2 * pp + 1]
                    dka_ref[pp, _blk(j), :] += dk[2 * pp] + dk[2 * pp + 1]
                    dva_ref[pp, _blk(j), :] += dv[2 * pp] + dv[2 * pp + 1]
                col_sums = jnp.zeros((NH, BQ), F32)
                for c in range(nc):
                    col_sums = col_sums + jnp.where(sub == head0 + c, jnp.sum(ds[c], axis=0, keepdims=True), 0.0)
                    rsum_ref[c] += jnp.sum(ds[c], axis=1, keepdims=True)
                dck_ref[:, keys] = dck_ref[:, keys] - col_sums

            def kblock(j, _):
                tile(j, None)
                return 0

            lax.fori_loop(0, i, kblock, 0)
            tile(i, diag)
            row_sums = jnp.zeros((BQ, LANES), F32)
            for c in range(nc):
                row_sums = row_sums + jnp.where(lane == head0 + c, rsum_ref[c], 0.0)
            dcq_ref[_blk(i), :] += row_sums
            for pp in range(npair):
                dp_ref[_blk(i), pp * PAIR_W:pp * PAIR_W + LANES] = (dqa_ref[pp] * SCALE).astype(BF16)
            return 0

        lax.fori_loop(0, NBLK, qblock, 0)
        for pp in range(npair):
            dp_ref[:, pp * PAIR_W + LANES:pp * PAIR_W + 2 * LANES] = dka_ref[pp].astype(BF16)
            dp_ref[:, pp * PAIR_W + 2 * LANES:pp * PAIR_W + 3 * LANES] = dva_ref[pp].astype(BF16)

    return pl.pallas_call(
        body, name=name,
        out_shape=(jax.ShapeDtypeStruct(dproj.shape, BF16), jax.ShapeDtypeStruct((nseq, NH, LP), F32),
                   jax.ShapeDtypeStruct((t, LANES), F32)),
        grid=(nseq, NH // (2 * npair)),
        in_specs=[_group_spec(1, npair), SEQ_SPEC, CROW_SPEC, _gheads_spec(npair), _gheads_spec(npair), _gheads_spec(npair), ANY],
        out_specs=(_group_spec(1, npair), CROW_SPEC, SEQ_SPEC),
        input_output_aliases={6: 0},
        scratch_shapes=_qkv_scratch(npair) + [pltpu.VMEM((npair, BQ, LANES), F32), pltpu.VMEM((npair, LP, LANES), F32),
                                      pltpu.VMEM((npair, LP, LANES), F32), pltpu.VMEM((2 * npair, BQ, 1), F32)],
        compiler_params=_cparams("parallel", "arbitrary"))(proj, c, crow, o32, lse, do, dproj)


def _adamw_math(w, g, m, v):
    m = B1 * m + (1.0 - B1) * g
    v = B2 * v + (1.0 - B2) * (g * g)
    m_hat = m / (1.0 - B1 ** STEP)
    v_hat = v / (1.0 - B2 ** STEP)
    delta = -LR * (m_hat / (jnp.sqrt(v_hat) + EPS) + WD * w)
    return delta, m, v


def _sum_adamw(parts, w, m, v, tr, name):
    rows, cols = w.shape
    cp = parts.shape[2]
    assert rows % tr == 0 and parts.shape[1] == rows

    def body(p_ref, w_ref, m_ref, v_ref, g_ref, d_ref, nm_ref, nv_ref):
        gsum = p_ref[0].astype(F32)
        for s in range(1, N_DEV):
            gsum = gsum + p_ref[s].astype(F32)
        gsum = gsum[:, :cols]
        d, nm, nv = _adamw_math(w_ref[...], gsum, m_ref[...], v_ref[...])
        g_ref[...] = gsum
        d_ref[...] = d
        nm_ref[...] = nm
        nv_ref[...] = nv

    blk = pl.BlockSpec((tr, cols), lambda i: (i, 0))
    out = jax.ShapeDtypeStruct((rows, cols), F32)
    return pl.pallas_call(
        body, name=name, out_shape=(out, out, out, out), grid=(rows // tr,),
        in_specs=[pl.BlockSpec((N_DEV, tr, cp), lambda i: (0, i, 0)), blk, blk, blk],
        out_specs=(blk, blk, blk, blk), compiler_params=_cparams("parallel"))(parts, w, m, v)


def _local_step(x, tgt, meta, g_mix, w_in_p, b_forget, g_ffn, g_final, late_weights, early_grads, last_grad):
    nseq = x.shape[0]
    t = nseq * LP
    tm = LP // 2
    mm = functools.partial(_matmul, tm=tm)

    h0 = _pad_rows(meta, x, nseq, "pad_x").reshape(t, D)
    tgt_p = _pad_rows(jnp.zeros((N_META, D), F32), tgt, nseq, "pad_target").reshape(t, D)
    bf = jnp.pad(b_forget.reshape(1, NH), ((0, 0), (0, LANES - NH)))

    _, n1 = _norm_fwd(h0, None, g_mix, "norm1")
    proj = mm(n1, w_in_p, out_dtype=F32, tn=1792, tk=D, name="in_proj")
    c = _gate_fwd(proj, bf, nseq, "gate_fwd")
    crow = c[:, :NH].reshape(nseq, LP, NH).transpose(0, 2, 1)
    o_sb, rs = _sb_fwd(proj, nseq, 2, "sb_fwd")
    o_fx, o_fx32, lse = _fox_fwd(proj, c, crow, nseq, 1, "fox_fwd")
    w_bsb, w_bfx, w_out, w_up_i, cw_i, w_down = late_weights(o_fx)
    p_sb = mm(o_sb, w_bsb, out_dtype=F32, tn=D, tk=W_ATT, name="branch_sb")
    p_fx = mm(o_fx, w_bfx, out_dtype=F32, tn=D, tk=W_ATT, name="branch_fox")
    merged = _merge_fwd(p_sb, p_fx, proj, "merge_fwd")
    mix = mm(merged, w_out, out_dtype=F32, tn=D, tk=D, name="out_proj")
    h1, n2 = _norm_fwd(h0, mix, g_ffn, "norm2")
    u = mm(n2, w_up_i, out_dtype=F32, tn=1408, tk=D, name="up_proj")
    act = _conv_glu_fwd(u, cw_i, nseq, "conv_glu_fwd")
    ffn = mm(act, w_down, out_dtype=F32, tn=D, tk=1408, name="down_proj")

    loss, dh2, dh2b, dg_final = _final_loss_bwd(h1, ffn, g_final, tgt_p, "final")
    d_down = _matmul(act, dh2b, out_dtype=BF16, tm=1408, tn=D, tk=tm, ta=True, name="d_w_down")
    dact = mm(dh2b, w_down, out_dtype=F32, tn=1408, tk=D, tb=True, name="d_act")
    du, d_cw = _conv_glu_bwd(u, cw_i, dact, nseq, "conv_glu_bwd")
    d_up = _matmul(n2, du, out_dtype=BF16, tm=D, tn=1408, tk=tm, ta=True, name="d_w_up")
    dn2 = mm(du, w_up_i, out_dtype=F32, tn=D, tk=1408, tb=True, name="d_n2")
    dh1, dh1b, dg_ffn = _norm_bwd(h1, dn2, g_ffn, dh2, True, "norm2_bwd")
    d_out = _matmul(merged, dh1b, out_dtype=BF16, tm=D, tn=D, tk=tm, ta=True, name="d_w_out")
    dmerged = mm(dh1b, w_out, out_dtype=F32, tn=D, tk=D, tb=True, name="d_merged")
    dp_sb, dproj = _merge_bwd(dmerged, p_sb, proj, None, 0, "merge_bwd_sb")
    dp_fx, dproj = _merge_bwd(dmerged, p_fx, proj, dproj, 1, "merge_bwd_fox")
    d_bsb = _matmul(o_sb, dp_sb, out_dtype=BF16, tm=W_ATT, tn=D, tk=tm, ta=True, name="d_w_branch_sb")
    d_bfx = _matmul(o_fx, dp_fx, out_dtype=BF16, tm=W_ATT, tn=D, tk=tm, ta=True, name="d_w_branch_fox")
    do_sb = mm(dp_sb, w_bsb, out_dtype=BF16, tn=W_ATT, tk=D, tb=True, name="d_o_sb")
    do_fx = mm(dp_fx, w_bfx, out_dtype=BF16, tn=W_ATT, tk=D, tb=True, name="d_o_fox")
    sent = early_grads(dict(w_branch_sb=d_bsb, w_branch_fox=d_bfx, w_out=d_out, w_up=d_up, conv_w=d_cw, w_down=d_down))
    dproj = _sb_bwd(proj, do_sb, rs, dproj, sent, nseq, 2, "sb_bwd")
    dproj, dck, dcq = _fox_bwd(proj, c, crow, o_fx32, lse, do_fx, dproj, nseq, 2, "fox_bwd")
    dc = dcq + jnp.pad(dck.transpose(0, 2, 1).reshape(t, NH), ((0, 0), (0, LANES - NH)))
    dproj, d_bf = _gate_bwd(proj, bf, dc, dproj, nseq, "gate_bwd")
    d_in = _matmul(n1, dproj, out_dtype=BF16, tm=D, tn=1792, tk=tm, ta=True, name="d_w_in")
    dn1 = mm(dproj, w_in_p, out_dtype=F32, tn=D, tk=1792, tb=True, after=last_grad(d_in), name="d_n1")
    dh0, dg_mix = _norm_bwd(h0, dn1, g_mix, dh1, False, "norm1_bwd")
    dh0 = dh0.reshape(nseq, LP, D)
    grads = dict(meta_tokens=jnp.sum(dh0[:, :N_META], axis=0), norm_mix_g=dg_mix, b_forget=d_bf[:, :NH],
                 norm_ffn_g=dg_ffn, norm_final_g=dg_final)
    return loss[0, 0], _real_rows(dh0, nseq, "grad_x"), grads


REPL = (("norm_mix_g", D), ("norm_ffn_g", D), ("norm_final_g", D), ("b_forget", LANES))
REPL_ROWS = 32
META_ROWS = N_META * D // LANES


def _pack_repl(tree):
    rows = [jnp.pad(tree[name].reshape(-1), (0, n - tree[name].size)).reshape(-1, LANES) for name, n in REPL]
    packed = jnp.concatenate(rows, axis=0)
    return jnp.pad(packed, ((0, REPL_ROWS - packed.shape[0]), (0, 0)))


def _unpack_repl(packed, shapes):
    out, r = {}, 0
    for name, n in REPL:
        size = 1
        for s in shapes[name]:
            size *= s
        out[name] = packed[r:r + n // LANES].reshape(-1)[:size].reshape(shapes[name])
        r += n // LANES
    return out


def kernel(x, meta_tokens, norm_mix_g, w_in, b_forget, w_branch_sb, w_branch_fox, w_out, norm_ffn_g, w_up, conv_w, w_down, norm_final_g, loss_target, m_meta_tokens, m_norm_mix_g, m_w_in, m_b_forget, m_w_branch_sb, m_w_branch_fox, m_w_out, m_norm_ffn_g, m_w_up, m_conv_w, m_w_down, m_norm_final_g, v_meta_tokens, v_norm_mix_g, v_w_in, v_b_forget, v_w_branch_sb, v_w_branch_fox, v_w_out, v_norm_ffn_g, v_w_up, v_conv_w, v_w_down, v_norm_final_g):
    w = dict(meta_tokens=meta_tokens, norm_mix_g=norm_mix_g, w_in=w_in, b_forget=b_forget, w_branch_sb=w_branch_sb,
             w_branch_fox=w_branch_fox, w_out=w_out, norm_ffn_g=norm_ffn_g, w_up=w_up, conv_w=conv_w, w_down=w_down,
             norm_final_g=norm_final_g)
    m = dict(meta_tokens=m_meta_tokens, norm_mix_g=m_norm_mix_g, w_in=m_w_in, b_forget=m_b_forget,
             w_branch_sb=m_w_branch_sb, w_branch_fox=m_w_branch_fox, w_out=m_w_out, norm_ffn_g=m_norm_ffn_g,
             w_up=m_w_up, conv_w=m_conv_w, w_down=m_w_down, norm_final_g=m_norm_final_g)
    v = dict(meta_tokens=v_meta_tokens, norm_mix_g=v_norm_mix_g, w_in=v_w_in, b_forget=v_b_forget,
             w_branch_sb=v_w_branch_sb, w_branch_fox=v_w_branch_fox, w_out=v_w_out, norm_ffn_g=v_norm_ffn_g,
             w_up=v_w_up, conv_w=v_conv_w, w_down=v_w_down, norm_final_g=v_norm_final_g)
    shapes = {k: a.shape for k, a in w.items()}
    sharded = ("w_in", "w_branch_sb", "w_branch_fox", "w_out", "w_up", "w_down", "conv_w", "meta_tokens")
    mat = lambda tree, name: tree[name].reshape(tree[name].shape[-2:])

    def lane_pad(a, width):
        return jnp.pad(a, ((0, 0), (0, width - a.shape[1])))

    late = ("w_branch_sb", "w_branch_fox", "w_out", "w_up", "w_down", "conv_w")
    g_in, g_meta = _all_gather([lane_pad(mat(w, "w_in").astype(BF16), SHARD_P), mat(w, "meta_tokens")], "gather_w_in")
    pending_w, started = _remote_start(
        [mat(w, "w_branch_sb").astype(BF16), mat(w, "w_branch_fox").astype(BF16), mat(w, "w_out").astype(BF16),
         lane_pad(mat(w, "w_up").astype(BF16), SHARD_P), mat(w, "w_down").astype(BF16), mat(w, "conv_w")],
        False, g_meta, "gather_late_start")
    w_in_p = _relayout(g_in, 1, IN_P, _gathered_to_full(IN_SHARD, _in_padded_to_orig), BF16, 256, "w_in_cols")[0]
    meta_full = g_meta.transpose(1, 0, 2).reshape(N_META, D)

    def late_weights(after):
        g_bsb, g_bfx, g_out, g_up, g_down, g_cw = _remote_wait(pending_w, after, "gather_late_wait")
        w_up_i = _relayout(g_up, 1, 2 * D_FF, _gathered_to_full(UP_SHARD, _up_inter_to_orig), BF16, 256, "w_up_cols")[0]
        w_bsb = _relayout(g_bsb, 1, D, _gathered_to_full(ATT_SHARD, lambda d: d), BF16, 256, "w_bsb_cols")[0]
        w_bfx = _relayout(g_bfx, 1, D, _gathered_to_full(ATT_SHARD, lambda d: d), BF16, 256, "w_bfx_cols")[0]
        cw_full = g_cw.transpose(1, 0, 2).reshape(3, 2 * D_FF)
        cw_i = cw_full.reshape(3, 2, D_FF // FFC, FFC).transpose(0, 2, 1, 3).reshape(3, 2 * D_FF)
        return w_bsb, w_bfx, g_out.reshape(D, D), w_up_i, cw_i, g_down.reshape(D_FF, D)

    pending_g = {}

    def early_grads(g):
        d_cw = g["conv_w"].reshape(3, D_FF // FFC, 2, FFC).transpose(0, 2, 1, 3).reshape(3, 2 * D_FF)
        pending_g["early"], sent = _remote_start(
            [_relayout(g["w_branch_sb"][None], N_DEV, ATT_SHARD, _full_to_shards(ATT_SHARD, lambda c: c), BF16, 256, "d_w_bsb_shards"),
             _relayout(g["w_branch_fox"][None], N_DEV, ATT_SHARD, _full_to_shards(ATT_SHARD, lambda c: c), BF16, 256, "d_w_bfx_shards"),
             g["w_out"].reshape(N_DEV, D // N_DEV, D),
             _relayout(g["w_up"][None], N_DEV, SHARD_P, _full_to_shards(UP_SHARD, _UP_ORIG_TO_INTER.get), BF16, 256, "d_w_up_shards"),
             g["w_down"].reshape(N_DEV, D_FF // N_DEV, D),
             d_cw.reshape(3, N_DEV, UP_SHARD).transpose(1, 0, 2)], True, g["w_out"], "exchange_early_start")
        return sent

    def last_grad(d_in):
        shards = _relayout(d_in[None], N_DEV, SHARD_P, _full_to_shards(IN_SHARD, _IN_ORIG_TO_PADDED.get), BF16, 256, "d_w_in_shards")
        pending_g["last"], sent = _remote_start([shards], True, shards, "exchange_last_start")
        return sent

    loss, grad_x, grads = _local_step(
        x, loss_target, meta_full, norm_mix_g.reshape(1, D) + started[0, 0], w_in_p, b_forget,
        norm_ffn_g.reshape(1, D), norm_final_g.reshape(1, D), late_weights, early_grads, last_grad)

    small = jnp.concatenate([_pack_repl(grads), grads["meta_tokens"].reshape(META_ROWS, LANES)], axis=0)
    small, = _all_gather([small], "gather_small_grads")
    me_idx = 4 * lax.axis_index("x") + 2 * lax.axis_index("y") + lax.axis_index("c")
    p_meta = lax.dynamic_slice_in_dim(small[:, REPL_ROWS:].reshape(N_DEV, N_META, D), me_idx * ATT_SHARD, ATT_SHARD, axis=2)

    p_in, = _remote_wait(pending_g["last"], small, "exchange_last_wait")
    parts = dict(zip(late, _remote_wait(pending_g["early"], p_in, "exchange_early_wait")), w_in=p_in, meta_tokens=p_meta)
    tiles = dict(w_in=256, w_branch_sb=256, w_branch_fox=256, w_out=D // N_DEV, w_up=256, w_down=D_FF // N_DEV,
                 conv_w=3, meta_tokens=N_META)
    new = {name: _sum_adamw(parts[name], mat(w, name), mat(m, name), mat(v, name), tiles[name], "adamw_" + name)
           for name in sharded}

    routs = _sum_adamw(small[:, :REPL_ROWS], _pack_repl(w), _pack_repl(m), _pack_repl(v), REPL_ROWS, "adamw_replicated")
    repl = [_unpack_repl(o, shapes) for o in routs]

    result = [lax.psum(loss, ("x", "y", "c")), grad_x]
    for k in range(4):
        for name in w:
            result.append(new[name][k].reshape(shapes[name]) if name in new else repl[k][name])
    return tuple(result)
```

```python
import functools

import jax
import jax.numpy as jnp
from jax import lax
from jax.experimental import pallas as pl
from jax.experimental.pallas import tpu as pltpu

F32 = jnp.float32
BF16 = jnp.bfloat16

N_DEV = 8
LANES = 128
D = 1024
N_META = 16
SEQ = 2048
L_REAL = N_META + SEQ
LP = 2304
BQ = 256
NBLK = LP // BQ
HEAD = 64
NH = 8
W_ATT = NH * HEAD
PAIR_W = 3 * LANES
D_FF = 2816
IN_COLS = 5128
QKV = 6 * W_ATT
IN_P = 5376
GATE_COL = QKV
F_COL = QKV + 2 * D
FFC = 256
RMS_EPS = 1e-6
LR, B1, B2, EPS, WD, STEP = 0.001, 0.9, 0.999, 1e-08, 0.01, 10
VMEM_LIMIT = 56 * 1024 * 1024

MESH = pl.DeviceIdType.MESH
ANY = pl.BlockSpec(memory_space=pl.ANY)


def _cparams(*sem):
    return pltpu.CompilerParams(dimension_semantics=sem if sem else None, vmem_limit_bytes=VMEM_LIMIT)


def _all_gather(xs, name):
    n = len(xs)

    def body(*refs):
        x_refs, out_refs = refs[:n], refs[n:2 * n]
        send_sems, recv_sems, local_sems = refs[2 * n:]
        mx, my, mc = lax.axis_index("x"), lax.axis_index("y"), lax.axis_index("c")
        me, sibling = (mx, my, mc), (mx, my, 1 - mc)
        chips = [(1 - mx, my), (mx, 1 - my), (1 - mx, 1 - my)]

        def copy(a, k, block, to, own=False):
            px, py, pc = block
            slot = out_refs[a].at[4 * px + 2 * py + pc]
            return pltpu.make_async_remote_copy(
                src_ref=x_refs[a] if own else slot, dst_ref=slot,
                send_sem=send_sems.at[7 * a + k], recv_sem=recv_sems.at[7 * a + k],
                device_id=to, device_id_type=MESH)

        mine = [pltpu.make_async_copy(x_refs[a], out_refs[a].at[4 * mx + 2 * my + mc], local_sems.at[a]) for a in range(n)]
        for cp in mine:
            cp.start()
        first = []
        for a in range(n):
            first.append(copy(a, 0, me, sibling, own=True))
            first += [copy(a, 1 + j, me, (*chip, mc), own=True) for j, chip in enumerate(chips)]
        for cp in first:
            cp.start()
        passed = []
        for j, chip in enumerate(chips):
            for a in range(n):
                copy(a, 1 + j, (*chip, mc), me).wait_recv()
                fwd = copy(a, 4 + j, (*chip, mc), sibling)
                fwd.start()
                passed.append(fwd)
        for a in range(n):
            copy(a, 0, sibling, me).wait_recv()
            for j, chip in enumerate(chips):
                copy(a, 4 + j, (*chip, 1 - mc), me).wait_recv()
        for cp in first + passed:
            cp.wait_send()
        for cp in mine:
            cp.wait()

    return pl.pallas_call(
        body, name=name,
        out_shape=tuple(jax.ShapeDtypeStruct((N_DEV,) + x.shape, x.dtype) for x in xs),
        in_specs=[ANY] * n, out_specs=tuple([ANY] * n),
        scratch_shapes=[pltpu.SemaphoreType.DMA((7 * n,)), pltpu.SemaphoreType.DMA((7 * n,)),
                        pltpu.SemaphoreType.DMA((n,))],
    )(*xs)


HBM = pl.BlockSpec(memory_space=pltpu.HBM)
SEM = pl.BlockSpec(memory_space=pltpu.SEMAPHORE)
EFFECT = pltpu.SideEffectType.DATAFLOW_SIDE_EFFECTING


def _peer_copies(src_refs, land_refs, send_sems, recv_sems, per_peer):
    mx, my, mc = lax.axis_index("x"), lax.axis_index("y"), lax.axis_index("c")
    me_idx = 4 * mx + 2 * my + mc
    copies = []
    for k in range(1, N_DEV):
        px, py, pc = mx ^ (k >> 2), my ^ ((k >> 1) & 1), mc ^ (k & 1)
        for a, (src, land) in enumerate(zip(src_refs, land_refs)):
            copies.append(pltpu.make_async_remote_copy(
                src_ref=src.at[4 * px + 2 * py + pc] if per_peer else src, dst_ref=land.at[me_idx],
                send_sem=send_sems.at[7 * a + k - 1], recv_sem=recv_sems.at[7 * a + k - 1],
                device_id=(px, py, pc), device_id_type=MESH))
    return me_idx, copies


def _remote_start(srcs, per_peer, after, name):
    n = len(srcs)
    lands = [lax.empty(s.shape if per_peer else (N_DEV,) + s.shape, s.dtype) for s in srcs]

    def body(*refs):
        src_refs, land_refs = refs[:n], refs[n:2 * n]
        send_sems, recv_sems = refs[2 * n + 1:2 * n + 3]
        token = refs[4 * n + 3]
        stage, local_sems = refs[4 * n + 4:5 * n + 4], refs[5 * n + 4]
        me_idx, copies = _peer_copies(src_refs, land_refs, send_sems, recv_sems, per_peer)
        for cp in copies:
            cp.start()
        own = [src_refs[a].at[me_idx] if per_peer else src_refs[a] for a in range(n)]
        for hop in ([(own[a], stage[a]) for a in range(n)], [(stage[a], land_refs[a].at[me_idx]) for a in range(n)]):
            cps = [pltpu.make_async_copy(s, d, local_sems.at[a]) for a, (s, d) in enumerate(hop)]
            for cp in cps:
                cp.start()
            for cp in cps:
                cp.wait()
        token[...] = jnp.zeros_like(token)

    thru = [pltpu.HBM(a.shape, a.dtype) for a in list(srcs) + lands]
    out = pl.pallas_call(
        body, name=name,
        out_shape=(pltpu.SemaphoreType.DMA((7 * n,)), pltpu.SemaphoreType.DMA((7 * n,)), *thru,
                   jax.ShapeDtypeStruct((8, LANES), F32)),
        in_specs=[HBM] * (2 * n) + [ANY],
        out_specs=(SEM, SEM, *([HBM] * (2 * n)), pl.BlockSpec(memory_space=pltpu.VMEM)),
        input_output_aliases={i: 2 + i for i in range(2 * n)},
        scratch_shapes=[pltpu.VMEM(s.shape[1:] if per_peer else s.shape, s.dtype) for s in srcs]
        + [pltpu.SemaphoreType.DMA((n,))],
        compiler_params=pltpu.CompilerParams(has_side_effects=EFFECT),
    )(*[pltpu.with_memory_space_constraint(a, pltpu.HBM) for a in list(srcs) + lands], after)
    return dict(sems=out[:2], bufs=out[2:2 * n + 2], per_peer=per_peer), out[-1]


def _remote_wait(pending, after, name):
    bufs = pending["bufs"]
    n = len(bufs) // 2
    per_peer = pending["per_peer"]

    def body(*refs):
        src_refs, land_refs = refs[:n], refs[n:2 * n]
        send_sems, recv_sems = refs[2 * n:2 * n + 2]
        _, copies = _peer_copies(src_refs, land_refs, send_sems, recv_sems, per_peer)
        for cp in copies:
            cp.wait_send()
        for cp in copies:
            cp.wait_recv()

    out = pl.pallas_call(
        body, name=name, out_shape=tuple(pltpu.HBM(a.shape, a.dtype) for a in bufs),
        in_specs=[HBM] * (2 * n) + [SEM, SEM, ANY], out_specs=tuple([HBM] * (2 * n)),
        input_output_aliases={i: i for i in range(2 * n)},
        compiler_params=pltpu.CompilerParams(has_side_effects=EFFECT),
    )(*bufs, *pending["sems"], after)
    return out[n:]


ROWS_PER_COPY = 256


def _pad_rows(front, body_rows, nseq, name):
    tail = LP - L_REAL
    nblk = SEQ // ROWS_PER_COPY

    def body(f_ref, b_ref, o_ref, z_ref, sems):
        s, i = pl.program_id(0), pl.program_id(1)
        rows = pltpu.make_async_copy(b_ref, o_ref.at[pl.ds(s, 1), pl.ds(N_META + i * ROWS_PER_COPY, ROWS_PER_COPY)], sems.at[0])
        rows.start()

        @pl.when(i == 0)
        def _():
            z_ref[...] = jnp.zeros_like(z_ref)
            head = pltpu.make_async_copy(f_ref, o_ref.at[s, pl.ds(0, N_META)], sems.at[1])
            zeros = pltpu.make_async_copy(z_ref, o_ref.at[s, pl.ds(L_REAL, tail)], sems.at[2])
            head.start()
            zeros.start()
            head.wait()
            zeros.wait()

        rows.wait()

    return pl.pallas_call(
        body, name=name, out_shape=jax.ShapeDtypeStruct((nseq, LP, D), F32), grid=(nseq, nblk),
        in_specs=[pl.BlockSpec((N_META, D), lambda s, i: (0, 0)), pl.BlockSpec((1, ROWS_PER_COPY, D), lambda s, i: (s, i, 0))],
        out_specs=ANY,
        scratch_shapes=[pltpu.VMEM((tail, D), F32), pltpu.SemaphoreType.DMA((3,))],
        compiler_params=_cparams("arbitrary", "arbitrary"))(front, body_rows)


def _real_rows(h, nseq, name):
    nblk = SEQ // ROWS_PER_COPY

    def body(h_ref, o_ref, sem):
        s, i = pl.program_id(0), pl.program_id(1)
        rows = pltpu.make_async_copy(h_ref.at[pl.ds(s, 1), pl.ds(N_META + i * ROWS_PER_COPY, ROWS_PER_COPY)], o_ref, sem)
        rows.start()
        rows.wait()

    return pl.pallas_call(
        body, name=name, out_shape=jax.ShapeDtypeStruct((nseq, SEQ, D), F32), grid=(nseq, nblk),
        in_specs=[ANY], out_specs=pl.BlockSpec((1, ROWS_PER_COPY, D), lambda s, i: (s, i, 0)),
        scratch_shapes=[pltpu.SemaphoreType.DMA],
        compiler_params=_cparams("arbitrary", "arbitrary"))(h)


def _plan_cols(n_q, n_dcols, src_of):
    plan = {}
    for q in range(n_q):
        for dblk in range(n_dcols // LANES):
            segs, key, start = [], None, 0
            for lane in range(LANES + 1):
                new = None
                if lane < LANES:
                    src = src_of(q, dblk * LANES + lane)
                    if src is not None:
                        new = (src[0], src[1] // LANES, (lane - src[1] % LANES) % LANES)
                if new != key:
                    if key is not None:
                        segs.append((*key, start, lane))
                    key, start = new, lane
            plan[(q, dblk)] = segs
    return plan


def _relayout(src, n_q, n_dcols, src_of, out_dtype, tr, name):
    n_p, rows, scols = src.shape
    plan = _plan_cols(n_q, n_dcols, src_of)

    def body(s_ref, d_ref):
        lane = lax.broadcasted_iota(jnp.int32, (tr, LANES), 1)
        for (q, dblk), segs in plan.items():
            acc = jnp.zeros((tr, LANES), F32)
            for p, sblk, rot, lo, hi in segs:
                x = s_ref[p, :, sblk * LANES:(sblk + 1) * LANES].astype(F32)
                if rot:
                    x = pltpu.roll(x, rot, 1)
                acc = x if (lo, hi) == (0, LANES) else jnp.where((lane >= lo) & (lane < hi), x, acc)
            d_ref[q, :, dblk * LANES:(dblk + 1) * LANES] = acc.astype(out_dtype)

    return pl.pallas_call(
        body, name=name, out_shape=jax.ShapeDtypeStruct((n_q, rows, n_dcols), out_dtype), grid=(rows // tr,),
        in_specs=[pl.BlockSpec((n_p, tr, scols), lambda i: (0, i, 0))],
        out_specs=pl.BlockSpec((n_q, tr, n_dcols), lambda i: (0, i, 0)),
        compiler_params=_cparams("parallel"))(src)


def _in_padded_to_orig(d):
    if d < QKV:
        kind, r = divmod(d, 4 * PAIR_W)
        pair, r = divmod(r, PAIR_W)
        part, r = divmod(r, LANES)
        return kind * 3 * W_ATT + part * W_ATT + pair * LANES + r
    if d < F_COL:
        return d + NH
    if d < F_COL + NH:
        return d - 2 * D
    return None


_IN_ORIG_TO_PADDED = {_in_padded_to_orig(d): d for d in range(IN_P) if _in_padded_to_orig(d) is not None}


def _up_inter_to_orig(d):
    j, r = divmod(d, 2 * FFC)
    part, r = divmod(r, FFC)
    return part * D_FF + j * FFC + r


_UP_ORIG_TO_INTER = {_up_inter_to_orig(d): d for d in range(2 * D_FF)}
IN_SHARD = IN_COLS // N_DEV
UP_SHARD = 2 * D_FF // N_DEV
SHARD_P = 768
ATT_SHARD = D // N_DEV


def _gathered_to_full(n_shard, to_orig):
    def src_of(q, d):
        c = to_orig(d)
        return None if c is None else (c // n_shard, c % n_shard)
    return src_of


def _full_to_shards(n_shard, from_orig):
    def src_of(q, d):
        return (0, from_orig(q * n_shard + d)) if d < n_shard else None
    return src_of


def _matmul(a, b, *, out_dtype, tm, tn, tk, ta=False, tb=False, after=None, name):
    if ta:
        kdim, m = a.shape
    else:
        m, kdim = a.shape
    n = b.shape[0] if tb else b.shape[1]
    assert m % tm == 0 and n % tn == 0 and kdim % tk == 0, (name, a.shape, b.shape, tm, tn, tk)
    nk = kdim // tk

    def body(a_ref, b_ref, *rest):
        o_ref, scratch = rest[len(extra)], rest[len(extra) + 1:]
        av, bv = a_ref[...], b_ref[...]
        if ta:
            p = lax.dot_general(av, bv, (((0,), (0,)), ((), ())), preferred_element_type=F32)
        elif tb:
            p = lax.dot_general(av, bv, (((1,), (1,)), ((), ())), preferred_element_type=F32)
        else:
            p = jnp.dot(av, bv, preferred_element_type=F32)
        if nk == 1:
            o_ref[...] = p.astype(o_ref.dtype)
        else:
            acc_ref, = scratch
            k = pl.program_id(2)

            @pl.when(k == 0)
            def _():
                acc_ref[...] = p

            @pl.when(k > 0)
            def _():
                acc_ref[...] += p

            @pl.when(k == nk - 1)
            def _():
                o_ref[...] = acc_ref[...].astype(o_ref.dtype)

    extra = [] if after is None else [after]
    a_spec = pl.BlockSpec((tk, tm), lambda i, j, k: (k, i)) if ta else pl.BlockSpec((tm, tk), lambda i, j, k: (i, k))
    b_spec = pl.BlockSpec((tn, tk), lambda i, j, k: (j, k)) if tb else pl.BlockSpec((tk, tn), lambda i, j, k: (k, j))
    return pl.pallas_call(
        body, name=name,
        out_shape=jax.ShapeDtypeStruct((m, n), out_dtype),
        grid=(m // tm, n // tn, nk),
        in_specs=[a_spec, b_spec] + [ANY] * len(extra),
        out_specs=pl.BlockSpec((tm, tn), lambda i, j, k: (i, j)),
        scratch_shapes=[] if nk == 1 else [pltpu.VMEM((tm, tn), F32)],
        compiler_params=_cparams("parallel", "parallel", "arbitrary"),
    )(a, b, *extra)


TR = 288


def _rms(h):
    return lax.rsqrt(jnp.mean(h * h, axis=-1, keepdims=True) + RMS_EPS)


def _norm_fwd(h, g, name):
    t = h.shape[0]
    row = pl.BlockSpec((TR, D), lambda i: (i, 0))

    def body(h_ref, g_ref, n_ref):
        hv = h_ref[...]
        n_ref[...] = ((hv * _rms(hv)) * g_ref[...]).astype(BF16)

    return pl.pallas_call(
        body, name=name, out_shape=jax.ShapeDtypeStruct((t, D), BF16), grid=(t // TR,),
        in_specs=[row, pl.BlockSpec((1, D), lambda i: (0, 0))], out_specs=row, compiler_params=_cparams("parallel"))(h, g)


EPI_ROWS = 144


def _matmul_rows(a, b, rows_in, vecs_in, epilogue, row_outs, sum_outs, *, tm, tk, tb=False, after=None, name):
    m, kdim = a.shape
    assert (b.shape[0] if tb else b.shape[1]) == D and m % tm == 0 and kdim % tk == 0 and tm % EPI_ROWS == 0
    nk = kdim // tk
    n_r, n_v, n_ro, n_so = len(rows_in), len(vecs_in), len(row_outs), len(sum_outs)
    extra = [] if after is None else [after]

    def body(a_ref, b_ref, *rest):
        r_refs, v_refs = rest[:n_r], rest[n_r:n_r + n_v]
        outs = rest[n_r + n_v + len(extra):]
        ro_refs, so_refs, acc_ref = outs[:n_ro], outs[n_ro:n_ro + n_so], outs[n_ro + n_so]
        i, k = pl.program_id(0), pl.program_id(1)
        if tb:
            p = lax.dot_general(a_ref[...], b_ref[...], (((1,), (1,)), ((), ())), preferred_element_type=F32)
        else:
            p = jnp.dot(a_ref[...], b_ref[...], preferred_element_type=F32)

        @pl.when(k == 0)
        def _():
            acc_ref[...] = p

        @pl.when(k > 0)
        def _():
            acc_ref[...] += p

        @pl.when(k == nk - 1)
        def _():
            vecs = [v[...] for v in v_refs]

            def step(c, sums):
                rows = pl.ds(pl.multiple_of(c * EPI_ROWS, 8), EPI_ROWS)
                tiles, terms = epilogue(i * tm + c * EPI_ROWS, acc_ref[rows, :], *[r[rows, :] for r in r_refs], *vecs)
                for o, tile in zip(ro_refs, tiles):
                    o[rows, :] = tile.astype(o.dtype)
                return tuple(s + term for s, term in zip(sums, terms))

            sums = lax.fori_loop(0, tm // EPI_ROWS, step, tuple(jnp.zeros(s, F32) for s in sum_outs))

            @pl.when(i == 0)
            def _():
                for o in so_refs:
                    o[...] = jnp.zeros_like(o)

            for o, s in zip(so_refs, sums):
                o[...] += s

    row = pl.BlockSpec((tm, D), lambda i, k: (i, 0))
    b_spec = pl.BlockSpec((D, tk), lambda i, k: (0, k)) if tb else pl.BlockSpec((tk, D), lambda i, k: (k, 0))
    return pl.pallas_call(
        body, name=name,
        out_shape=tuple([jax.ShapeDtypeStruct((m, D), dt) for dt in row_outs] + [jax.ShapeDtypeStruct(s, F32) for s in sum_outs]),
        grid=(m // tm, nk),
        in_specs=[pl.BlockSpec((tm, tk), lambda i, k: (i, k)), b_spec] + [row] * n_r
        + [pl.BlockSpec((1, D), lambda i, k: (0, 0))] * n_v + [ANY] * len(extra),
        out_specs=tuple([row] * n_ro + [pl.BlockSpec(s, lambda i, k: (0, 0)) for s in sum_outs]),
        scratch_shapes=[pltpu.VMEM((tm, D), F32)],
        compiler_params=_cparams("arbitrary", "arbitrary"))(a, b, *rows_in, *vecs_in, *extra)


def _residual_norm(row0, acc, h, g):
    hv = h + acc
    return (hv, (hv * _rms(hv)) * g), ()


def _rms_bwd_math(hv, dn, gv):
    r = _rms(hv)
    hr = hv * r
    dng = dn * gv
    dh = r * (dng - hr * jnp.mean(dng * hr, axis=-1, keepdims=True))
    return dh, dn * hr


def _loss_head(row0, acc, h1, tgt, g):
    hv = h1 + acc
    hr = hv * _rms(hv)
    pos = row0 % LP + lax.broadcasted_iota(jnp.int32, (EPI_ROWS, 1), 0)
    valid = (pos >= N_META) & (pos < L_REAL)
    err = jnp.where(valid, hr * g - tgt, 0.0)
    part = 0.5 * jnp.sum(jnp.mean(err * err, axis=-1, keepdims=True))
    dy = err * (1.0 / D)
    dh, dgrow = _rms_bwd_math(hv, dy, g)
    return (dh, dh), (jnp.full((8, LANES), part, F32), jnp.sum(dgrow, axis=0, keepdims=True))


def _residual_norm_bwd(row0, acc, h, dres, g):
    dh, dgrow = _rms_bwd_math(h, acc, g)
    dh = dh + dres
    return (dh, dh), (jnp.sum(dgrow, axis=0, keepdims=True),)


def _residual_norm_bwd_f32(row0, acc, h, dres, g):
    tiles, sums = _residual_norm_bwd(row0, acc, h, dres, g)
    return tiles[:1], sums


GATE_BLK = GATE_COL // D


def _sigmoid(x):
    return 1.0 / (1.0 + jnp.exp(-x))


def _merge_fwd(p_sb, p_fx, proj, name):
    t = p_sb.shape[0]
    row = pl.BlockSpec((TR, D), lambda i: (i, 0))

    def body(ps_ref, pf_ref, gs_ref, gf_ref, o_ref):
        o_ref[...] = (_sigmoid(gs_ref[...]) * ps_ref[...] + _sigmoid(gf_ref[...]) * pf_ref[...]).astype(BF16)

    return pl.pallas_call(
        body, name=name, out_shape=jax.ShapeDtypeStruct((t, D), BF16), grid=(t // TR,),
        in_specs=[row, row, pl.BlockSpec((TR, D), lambda i: (i, GATE_BLK)),
                  pl.BlockSpec((TR, D), lambda i: (i, GATE_BLK + 1))],
        out_specs=row, compiler_params=_cparams("parallel"))(p_sb, p_fx, proj, proj)


def _merge_bwd(dm, p, proj, dproj, which, name):
    t = dm.shape[0]
    row = pl.BlockSpec((TR, D), lambda i: (i, 0))
    gate = pl.BlockSpec((TR, D), lambda i: (i, GATE_BLK + which))

    def body(dm_ref, p_ref, g_ref, *rest):
        dp_ref, dg_ref = rest[-2:]
        dmv = dm_ref[...]
        s = _sigmoid(g_ref[...])
        dp_ref[...] = (dmv * s).astype(BF16)
        dg_ref[...] = (dmv * p_ref[...] * s * (1.0 - s)).astype(BF16)

    out_shape = (jax.ShapeDtypeStruct((t, D), BF16), jax.ShapeDtypeStruct((t, IN_P), BF16))
    if dproj is None:
        return pl.pallas_call(
            body, name=name, out_shape=out_shape, grid=(t // TR,), in_specs=[row, row, gate],
            out_specs=(row, gate), compiler_params=_cparams("parallel"))(dm, p, proj)
    return pl.pallas_call(
        body, name=name, out_shape=out_shape, grid=(t // TR,), in_specs=[row, row, gate, ANY],
        out_specs=(row, gate), input_output_aliases={3: 1}, compiler_params=_cparams("parallel"))(dm, p, proj, dproj)


CH = 288


def _chunk(c, n=CH):
    return pl.ds(pl.multiple_of(c * CH, 8), n)


def _conv_taps(u_ref, c):
    x = u_ref[_chunk(c), :]
    prev = u_ref[pl.ds(pl.multiple_of(jnp.maximum(c * CH - 8, 0), 8), 8), :]
    xx = jnp.concatenate([jnp.where(c == 0, 0.0, prev), x], axis=0)
    return x, pltpu.roll(xx, 1, 0)[8:], pltpu.roll(xx, 2, 0)[8:]


def _conv_glu_fwd(u, cw, nseq, name):
    nblk = D_FF // FFC

    def body(u_ref, cw_ref, o_ref):
        cwv = cw_ref[...]

        def step(c, _):
            x, x1, x2 = _conv_taps(u_ref, c)
            uc = cwv[0:1, :] * x2 + cwv[1:2, :] * x1 + cwv[2:3, :] * x
            a, b = uc[:, :FFC], uc[:, FFC:]
            o_ref[_chunk(c), :] = (a * _sigmoid(a) * b).astype(BF16)
            return 0

        lax.fori_loop(0, LP // CH, step, 0)

    return pl.pallas_call(
        body, name=name, out_shape=jax.ShapeDtypeStruct((nseq * LP, D_FF), BF16), grid=(nseq, nblk),
        in_specs=[pl.BlockSpec((LP, 2 * FFC), lambda s, j: (s, j)), pl.BlockSpec((3, 2 * FFC), lambda s, j: (0, j))],
        out_specs=pl.BlockSpec((LP, FFC), lambda s, j: (s, j)),
        compiler_params=_cparams("parallel", "parallel"))(u, cw)


def _conv_glu_bwd(u, cw, dact, nseq, name):
    nblk = D_FF // FFC
    nch = LP // CH

    def body(u_ref, cw_ref, da_ref, du_ref, dcw_ref):
        s = pl.program_id(1)
        cwv = cw_ref[...]

        def step(k, carry):
            nxt, p0, p1, p2 = carry
            c = nch - 1 - k
            x, x1, x2 = _conv_taps(u_ref, c)
            uc = cwv[0:1, :] * x2 + cwv[1:2, :] * x1 + cwv[2:3, :] * x
            a, b = uc[:, :FFC], uc[:, FFC:]
            sa = _sigmoid(a)
            dactv = da_ref[_chunk(c), :]
            da = dactv * b * (sa * (1.0 + a * (1.0 - sa)))
            db = dactv * (a * sa)
            duc = jnp.concatenate([da, db], axis=1)
            dd = jnp.concatenate([duc, nxt], axis=0)
            du = (cwv[2:3, :] * duc + cwv[1:2, :] * pltpu.roll(dd, CH + 7, 0)[:CH]
                  + cwv[0:1, :] * pltpu.roll(dd, CH + 6, 0)[:CH])
            du_ref[_chunk(c), :] = du.astype(BF16)
            return (duc[:8], p0 + jnp.sum(duc * x2, axis=0, keepdims=True),
                    p1 + jnp.sum(duc * x1, axis=0, keepdims=True), p2 + jnp.sum(duc * x, axis=0, keepdims=True))

        zrow = jnp.zeros((1, 2 * FFC), F32)
        _, p0, p1, p2 = lax.fori_loop(0, nch, step, (jnp.zeros((8, 2 * FFC), F32), zrow, zrow, zrow))

        @pl.when(s == 0)
        def _():
            dcw_ref[...] = jnp.zeros_like(dcw_ref)

        dcw_ref[...] += jnp.concatenate([p0, p1, p2], axis=0)

    return pl.pallas_call(
        body, name=name,
        out_shape=(jax.ShapeDtypeStruct((nseq * LP, 2 * D_FF), BF16), jax.ShapeDtypeStruct((3, 2 * D_FF), F32)),
        grid=(nblk, nseq),
        in_specs=[pl.BlockSpec((LP, 2 * FFC), lambda j, s: (s, j)), pl.BlockSpec((3, 2 * FFC), lambda j, s: (0, j)),
                  pl.BlockSpec((LP, FFC), lambda j, s: (s, j))],
        out_specs=(pl.BlockSpec((LP, 2 * FFC), lambda j, s: (s, j)), pl.BlockSpec((3, 2 * FFC), lambda j, s: (0, j))),
        compiler_params=_cparams("parallel", "arbitrary"))(u, cw, dact)


F_BLK = F_COL // LANES
CB = 128


def _split3(x):
    hi = x.astype(BF16)
    r1 = x - hi.astype(F32)
    mid = r1.astype(BF16)
    lo = (r1 - mid.astype(F32)).astype(BF16)
    return hi, mid, lo


def _tri_dot(tri, x):
    hi, mid, lo = _split3(x)
    d = functools.partial(jnp.dot, preferred_element_type=F32)
    return d(tri, hi) + d(tri, mid) + d(tri, lo)


def _log_sigmoid(x):
    return jnp.minimum(x, 0.0) - jnp.log(1.0 + jnp.exp(-jnp.abs(x)))


def _gate_fwd(proj, bf, nseq, name):
    def body(f_ref, b_ref, c_ref):
        r_i = lax.broadcasted_iota(jnp.int32, (CB, CB), 0)
        c_i = lax.broadcasted_iota(jnp.int32, (CB, CB), 1)
        tri = (c_i <= r_i).astype(BF16)
        bv = b_ref[...]

        def step(k, carry):
            rows = pl.ds(pl.multiple_of(k * CB, CB), CB)
            lf = _log_sigmoid(f_ref[rows, :] + bv)
            c_ref[rows, :] = _tri_dot(tri, lf) + carry
            return carry + jnp.sum(lf, axis=0, keepdims=True)

        lax.fori_loop(0, LP // CB, step, jnp.zeros((1, LANES), F32))

    return pl.pallas_call(
        body, name=name, out_shape=jax.ShapeDtypeStruct((nseq * LP, LANES), F32), grid=(nseq,),
        in_specs=[pl.BlockSpec((LP, LANES), lambda s: (s, F_BLK)), pl.BlockSpec((1, LANES), lambda s: (0, 0))],
        out_specs=pl.BlockSpec((LP, LANES), lambda s: (s, 0)),
        compiler_params=_cparams("parallel"))(proj, bf)


def _gate_bwd(proj, bf, dc, dproj, nseq, name):
    def body(f_ref, b_ref, dc_ref, _, df_ref, db_ref):
        s = pl.program_id(0)
        r_i = lax.broadcasted_iota(jnp.int32, (CB, CB), 0)
        c_i = lax.broadcasted_iota(jnp.int32, (CB, CB), 1)
        tri = (c_i >= r_i).astype(BF16)
        bv = b_ref[...]

        def step(kk, carry):
            carry_c, carry_b = carry
            k = LP // CB - 1 - kk
            rows = pl.ds(pl.multiple_of(k * CB, CB), CB)
            dcv = dc_ref[rows, :]
            dlf = _tri_dot(tri, dcv) + carry_c
            df = dlf * _sigmoid(-(f_ref[rows, :] + bv))
            df_ref[rows, :] = jnp.concatenate([df, jnp.zeros_like(df)], axis=1).astype(BF16)
            return carry_c + jnp.sum(dcv, axis=0, keepdims=True), carry_b + jnp.sum(df, axis=0, keepdims=True)

        zero = jnp.zeros((1, LANES), F32)
        _, dbp = lax.fori_loop(0, LP // CB, step, (zero, zero))

        @pl.when(s == 0)
        def _():
            db_ref[...] = jnp.zeros_like(db_ref)

        db_ref[...] += dbp

    return pl.pallas_call(
        body, name=name,
        out_shape=(jax.ShapeDtypeStruct(dproj.shape, BF16), jax.ShapeDtypeStruct((1, LANES), F32)), grid=(nseq,),
        in_specs=[pl.BlockSpec((LP, LANES), lambda s: (s, F_BLK)), pl.BlockSpec((1, LANES), lambda s: (0, 0)),
                  pl.BlockSpec((LP, LANES), lambda s: (s, 0)), ANY],
        out_specs=(pl.BlockSpec((LP, 2 * LANES), lambda s: (s, F_COL // (2 * LANES))), pl.BlockSpec((1, LANES), lambda s: (0, 0))),
        input_output_aliases={3: 0},
        compiler_params=_cparams("arbitrary"))(proj, bf, dc, dproj)


SCALE = 0.125
NEG = -1e30


def _dot_nt(a, b):
    return lax.dot_general(a, b, (((1,), (1,)), ((), ())), preferred_element_type=F32)


def _dot_tn(a, b):
    return lax.dot_general(a, b, (((0,), (0,)), ((), ())), preferred_element_type=F32)


def _dot(a, b):
    return jnp.dot(a, b, preferred_element_type=F32)


def _blk(i):
    return pl.ds(pl.multiple_of(i * BQ, BQ), BQ)


def _tile_iotas():
    return lax.broadcasted_iota(jnp.int32, (BQ, BQ), 0), lax.broadcasted_iota(jnp.int32, (BQ, BQ), 1)


def _lane_iota():
    return lax.broadcasted_iota(jnp.int32, (BQ, LANES), 1)


def _head_masks():
    lane = _lane_iota()
    return lane < HEAD, lane >= HEAD


def _only(mask, x):
    return jnp.where(mask, x, jnp.zeros_like(x))


def _pick_lane(x, idx):
    return jnp.sum(jnp.where(_lane_iota() == idx, x, 0.0), axis=1, keepdims=True)


def _chains(npair):
    return [(pp, h) for pp in range(npair) for h in range(2)]


def _load_qkv(p_ref, q_s, k_s, v_s):
    for pp in range(q_s.shape[0]):
        base = pp * PAIR_W
        q_s[pp] = (p_ref[:, base:base + LANES] * SCALE).astype(BF16)
        k_s[pp] = p_ref[:, base + LANES:base + 2 * LANES].astype(BF16)
        v_s[pp] = p_ref[:, base + 2 * LANES:base + 3 * LANES].astype(BF16)


def _softplus(z):
    return jnp.maximum(z, 0.0) + jnp.log(1.0 + jnp.exp(-jnp.abs(z)))


def _hi_lo(x):
    hi = x.astype(BF16)
    return hi, (x - hi.astype(F32)).astype(BF16)


def _sb_tile_weights(q, k, strict, r, u_suf):
    n = len(q)
    z = [_dot_nt(q[c], k[c]) for c in range(n)]
    sp = [_softplus(zc) for zc in z]
    lk = [-spc if strict is None else jnp.where(strict, -spc, 0.0) for spc in sp]
    parts = [_hi_lo(lkc) for lkc in lk]
    suf = [_dot(hi, u_suf) + _dot(lo, u_suf) for hi, lo in parts]
    w = [jnp.exp(z[c] - sp[c] + r[c] + suf[c]) for c in range(n)]
    if strict is not None:
        w = [jnp.where(strict, wc, 0.0) for wc in w]
    r_next = [r[c] + suf[c][:, 0:1] + lk[c][:, 0:1] for c in range(n)]
    return w, sp, r_next


def _group_spec(kind, npair):
    return pl.BlockSpec((LP, npair * PAIR_W), lambda s, g: (s, (NH // (2 * npair)) * kind + g))


def _gheads_spec(npair):
    return pl.BlockSpec((LP, npair * LANES), lambda s, g: (s, g))


def _qkv_scratch(npair):
    return [pltpu.VMEM((npair, LP, LANES), BF16)] * 3


SEQ_SPEC = pl.BlockSpec((LP, LANES), lambda s, g: (s, 0))
RS_STRIDE = 16


def _pair_cols(pp):
    return slice(pp * LANES, (pp + 1) * LANES)


def _sb_fwd(proj, nseq, npair, name):
    t = nseq * LP
    chains = _chains(npair)

    def body(p_ref, o_ref, rs_ref, q_s, k_s, v_s, acc_ref, r_ref, rb_ref):
        _load_qkv(p_ref, q_s, k_s, v_s)
        row, col = _tile_iotas()
        u_suf = (row > col).astype(BF16)
        diag = col < row
        lane = _lane_iota()
        heads = _head_masks()

        def qblock(i, _):
            acc_ref[...] = jnp.zeros_like(acc_ref)
            rb_ref[...] = jnp.zeros_like(rb_ref)
            r_ref[...] = jnp.zeros_like(r_ref)
            qb = [q_s[pp, _blk(i), :] for pp in range(npair)]

            def tile(j, strict):
                kj = [k_s[pp, _blk(j), :] for pp in range(npair)]
                vj = [v_s[pp, _blk(j), :] for pp in range(npair)]
                r = [r_ref[c] for c in range(len(chains))]
                w, _, r_next = _sb_tile_weights([_only(heads[h], qb[pp]) for pp, h in chains],
                                                [kj[pp] for pp, _ in chains], strict, r, u_suf)
                pv = [_dot(w[c].astype(BF16), _only(heads[h], vj[pp])) for c, (pp, h) in enumerate(chains)]
                for pp in range(npair):
                    acc_ref[pp] += pv[2 * pp] + pv[2 * pp + 1]
                    rb_ref[pp] = jnp.where(lane == j, r[2 * pp], jnp.where(lane == RS_STRIDE + j, r[2 * pp + 1], rb_ref[pp]))
                for c in range(len(chains)):
                    r_ref[c] = r_next[c]

            tile(i, diag)

            def kblock(jj, _):
                tile(i - jj, None)
                return 0

            lax.fori_loop(1, i + 1, kblock, 0)
            for pp in range(npair):
                o_ref[_blk(i), _pair_cols(pp)] = acc_ref[pp].astype(BF16)
                rs_ref[_blk(i), _pair_cols(pp)] = rb_ref[pp]
            return 0

        lax.fori_loop(0, NBLK, qblock, 0)

    return pl.pallas_call(
        body, name=name,
        out_shape=(jax.ShapeDtypeStruct((t, W_ATT), BF16), jax.ShapeDtypeStruct((t, W_ATT), F32)),
        grid=(nseq, NH // (2 * npair)), in_specs=[_group_spec(0, npair)], out_specs=(_gheads_spec(npair), _gheads_spec(npair)),
        scratch_shapes=_qkv_scratch(npair) + [pltpu.VMEM((npair, BQ, LANES), F32), pltpu.VMEM((2 * npair, BQ, 1), F32),
                                      pltpu.VMEM((npair, BQ, LANES), F32)],
        compiler_params=_cparams("parallel", "parallel"))(proj)


def _sb_bwd(proj, do, rs, dproj, after, nseq, npair, name):
    chains = _chains(npair)

    def body(p_ref, do_ref, rs_ref, _, _after, dp_ref, q_s, k_s, v_s, dqa_ref, dka_ref, dva_ref, ep_ref):
        _load_qkv(p_ref, q_s, k_s, v_s)
        row, col = _tile_iotas()
        u_suf = (row > col).astype(BF16)
        u_pre = (row < col).astype(BF16)
        diag = col < row
        heads = _head_masks()
        dka_ref[...] = jnp.zeros_like(dka_ref)
        dva_ref[...] = jnp.zeros_like(dva_ref)
        nc = len(chains)

        def qblock(i, _):
            rb = [rs_ref[_blk(i), _pair_cols(pp)] for pp in range(npair)]
            qb = [q_s[pp, _blk(i), :] for pp in range(npair)]
            dob = [do_ref[_blk(i), _pair_cols(pp)] for pp in range(npair)]
            dqa_ref[...] = jnp.zeros_like(dqa_ref)
            ep_ref[...] = jnp.zeros_like(ep_ref)

            def tile(j, strict):
                kj = [k_s[pp, _blk(j), :] for pp in range(npair)]
                vj = [v_s[pp, _blk(j), :] for pp in range(npair)]
                q = [_only(heads[h], qb[pp]) for pp, h in chains]
                dov = [_only(heads[h], dob[pp]) for pp, h in chains]
                r = [_pick_lane(rb[pp], RS_STRIDE * h + j) for pp, h in chains]
                dw = [_dot_nt(dov[c], vj[pp]) for c, (pp, _) in enumerate(chains)]
                w, sp, _ = _sb_tile_weights(q, [kj[pp] for pp, _ in chains], strict, r, u_suf)
                e = [dw[c] * w[c] for c in range(nc)]
                e_pre = [ep_ref[c] + _dot(e[c].astype(BF16), u_pre) for c in range(nc)]
                dz = []
                for c in range(nc):
                    ep_ref[c] += jnp.sum(e[c], axis=1, keepdims=True)
                    sneg = jnp.exp(-sp[c])
                    dzc = e[c] * sneg - (1.0 - sneg) * e_pre[c]
                    if strict is not None:
                        dzc = jnp.where(strict, dzc, 0.0)
                    dz.append(dzc.astype(BF16))
                dq = [_dot(dz[c], _only(heads[h], kj[pp])) for c, (pp, h) in enumerate(chains)]
                dk = [_dot_tn(dz[c], q[c]) for c in range(nc)]
                dv = [_dot_tn(w[c].astype(BF16), dov[c]) for c in range(nc)]
                for pp in range(npair):
                    dqa_ref[pp] += dq[2 * pp] + dq[2 * pp + 1]
                    dka_ref[pp, _blk(j), :] += dk[2 * pp] + dk[2 * pp + 1]
                    dva_ref[pp, _blk(j), :] += dv[2 * pp] + dv[2 * pp + 1]

            def kblock(j, _):
                tile(j, None)
                return 0

            lax.fori_loop(0, i, kblock, 0)
            tile(i, diag)
            for pp in range(npair):
                dp_ref[_blk(i), pp * PAIR_W:pp * PAIR_W + LANES] = (dqa_ref[pp] * SCALE).astype(BF16)
            return 0

        lax.fori_loop(0, NBLK, qblock, 0)
        for pp in range(npair):
            dp_ref[:, pp * PAIR_W + LANES:pp * PAIR_W + 2 * LANES] = dka_ref[pp].astype(BF16)
            dp_ref[:, pp * PAIR_W + 2 * LANES:pp * PAIR_W + 3 * LANES] = dva_ref[pp].astype(BF16)

    return pl.pallas_call(
        body, name=name, out_shape=jax.ShapeDtypeStruct(dproj.shape, BF16), grid=(nseq, NH // (2 * npair)),
        in_specs=[_group_spec(0, npair), _gheads_spec(npair), _gheads_spec(npair), ANY, ANY], out_specs=_group_spec(0, npair),
        input_output_aliases={3: 0},
        scratch_shapes=_qkv_scratch(npair) + [pltpu.VMEM((npair, BQ, LANES), F32), pltpu.VMEM((npair, LP, LANES), F32),
                                      pltpu.VMEM((npair, LP, LANES), F32), pltpu.VMEM((2 * npair, BQ, 1), F32)],
        compiler_params=_cparams("parallel", "parallel"))(proj, do, rs, dproj, after)


CROW_SPEC = pl.BlockSpec((None, NH, LP), lambda s, g: (s, 0, 0))


def _fox_scores(qi, kj, cq, ck, causal):
    z = _dot_nt(qi, kj) + (cq - ck)
    return z if causal is None else jnp.where(causal, z, NEG)


def _key_cols(cr_ref, head, j):
    return cr_ref[pl.ds(head, 1), pl.ds(pl.multiple_of(j * BQ, BQ), BQ)]


def _fox_fwd(proj, c, crow, nseq, npair, name):
    t = nseq * LP
    chains = _chains(npair)

    def body(p_ref, c_ref, cr_ref, o_ref, o32_ref, lse_ref, q_s, k_s, v_s, acc_ref, m_ref, l_ref):
        _load_qkv(p_ref, q_s, k_s, v_s)
        row, col = _tile_iotas()
        diag = col <= row
        lane = _lane_iota()
        heads = _head_masks()
        head0 = 2 * npair * pl.program_id(1)
        nc = len(chains)

        def qblock(i, _):
            cblk = c_ref[_blk(i), :]
            qb = [q_s[pp, _blk(i), :] for pp in range(npair)]
            cq = [_pick_lane(cblk, head0 + c) for c in range(nc)]
            acc_ref[...] = jnp.zeros_like(acc_ref)
            m_ref[...] = jnp.full_like(m_ref, NEG)
            l_ref[...] = jnp.zeros_like(l_ref)

            def tile(j, causal):
                kj = [k_s[pp, _blk(j), :] for pp in range(npair)]
                vj = [v_s[pp, _blk(j), :] for pp in range(npair)]
                z = [_fox_scores(_only(heads[h], qb[pp]), kj[pp], cq[c], _key_cols(cr_ref, head0 + c, j), causal)
                     for c, (pp, h) in enumerate(chains)]
                p, alpha = [], []
                for c in range(nc):
                    m_old = m_ref[c]
                    m_new = jnp.maximum(m_old, jnp.max(z[c], axis=1, keepdims=True))
                    alpha.append(jnp.exp(m_old - m_new))
                    pc = jnp.exp(z[c] - m_new)
                    l_ref[c] = alpha[c] * l_ref[c] + jnp.sum(pc, axis=1, keepdims=True)
                    m_ref[c] = m_new
                    p.append(pc.astype(BF16))
                pv = [_dot(p[c], _only(heads[h], vj[pp])) for c, (pp, h) in enumerate(chains)]
                for c in range(nc):
                    acc_ref[c] = alpha[c] * acc_ref[c] + pv[c]

            def kblock(j, _):
                tile(j, None)
                return 0

            lax.fori_loop(0, i, kblock, 0)
            tile(i, diag)
            for pp in range(npair):
                out = acc_ref[2 * pp] / l_ref[2 * pp] + acc_ref[2 * pp + 1] / l_ref[2 * pp + 1]
                o_ref[_blk(i), _pair_cols(pp)] = out.astype(BF16)
                o32_ref[_blk(i), _pair_cols(pp)] = out
                lse = [m_ref[2 * pp + h] + jnp.log(l_ref[2 * pp + h]) for h in range(2)]
                lse_ref[_blk(i), _pair_cols(pp)] = jnp.where(lane == 0, lse[0], jnp.where(lane == 1, lse[1], 0.0))
            return 0

        lax.fori_loop(0, NBLK, qblock, 0)

    return pl.pallas_call(
        body, name=name,
        out_shape=(jax.ShapeDtypeStruct((t, W_ATT), BF16), jax.ShapeDtypeStruct((t, W_ATT), F32),
                   jax.ShapeDtypeStruct((t, W_ATT), F32)),
        grid=(nseq, NH // (2 * npair)), in_specs=[_group_spec(1, npair), SEQ_SPEC, CROW_SPEC], out_specs=(_gheads_spec(npair), _gheads_spec(npair), _gheads_spec(npair)),
        scratch_shapes=_qkv_scratch(npair) + [pltpu.VMEM((2 * npair, BQ, LANES), F32), pltpu.VMEM((2 * npair, BQ, 1), F32),
                                      pltpu.VMEM((2 * npair, BQ, 1), F32)],
        compiler_params=_cparams("parallel", "parallel"))(proj, c, crow)


def _fox_bwd(proj, c, crow, o32, lse, do, dproj, nseq, npair, name):
    t = nseq * LP
    chains = _chains(npair)

    def body(p_ref, c_ref, cr_ref, o_ref, lse_ref, do_ref, _, dp_ref, dck_ref, dcq_ref,
             q_s, k_s, v_s, dqa_ref, dka_ref, dva_ref, rsum_ref):
        _load_qkv(p_ref, q_s, k_s, v_s)
        row, col = _tile_iotas()
        diag = col <= row
        lane = _lane_iota()
        heads = _head_masks()
        sub = lax.broadcasted_iota(jnp.int32, (NH, BQ), 0)
        group = pl.program_id(1)
        head0 = 2 * npair * group
        nc = len(chains)
        dka_ref[...] = jnp.zeros_like(dka_ref)
        dva_ref[...] = jnp.zeros_like(dva_ref)

        @pl.when(group == 0)
        def _():
            dck_ref[...] = jnp.zeros_like(dck_ref)
            dcq_ref[...] = jnp.zeros_like(dcq_ref)

        def qblock(i, _):
            dqa_ref[...] = jnp.zeros_like(dqa_ref)
            rsum_ref[...] = jnp.zeros_like(rsum_ref)
            cblk = c_ref[_blk(i), :]
            qb = [q_s[pp, _blk(i), :] for pp in range(npair)]
            dob = [do_ref[_blk(i), _pair_cols(pp)] for pp in range(npair)]
            prod = [dob[pp].astype(F32) * o_ref[_blk(i), _pair_cols(pp)] for pp in range(npair)]
            cq = [_pick_lane(cblk, head0 + c) for c in range(nc)]
            lse_i = [_pick_lane(lse_ref[_blk(i), _pair_cols(pp)], h) for pp, h in chains]
            delta = [jnp.sum(_only(heads[h], prod[pp]), axis=1, keepdims=True) for pp, h in chains]

            def tile(j, causal):
                kj = [k_s[pp, _blk(j), :] for pp in range(npair)]
                vj = [v_s[pp, _blk(j), :] for pp in range(npair)]
                keys = pl.ds(pl.multiple_of(j * BQ, BQ), BQ)
                q = [_only(heads[h], qb[pp]) for pp, h in chains]
                dov = [_only(heads[h], dob[pp]) for pp, h in chains]
                z = [_fox_scores(q[c], kj[pp], cq[c], _key_cols(cr_ref, head0 + c, j), causal)
                     for c, (pp, _) in enumerate(chains)]
                dpv = [_dot_nt(dov[c], vj[pp]) for c, (pp, _) in enumerate(chains)]
                p = [jnp.exp(z[c] - lse_i[c]) for c in range(nc)]
                ds = [p[c] * (dpv[c] - delta[c]) for c in range(nc)]
                dsb = [d.astype(BF16) for d in ds]
                dq = [_dot(dsb[c], _only(heads[h], kj[pp])) for c, (pp, h) in enumerate(chains)]
                dk = [_dot_tn(dsb[c], q[c]) for c in range(nc)]
                dv = [_dot_tn(p[c].astype(BF16), dov[c]) for c in range(nc)]
                for pp in range(npair):
                    dqa_ref[pp] += dq[2 * pp] + dq[2 * pp + 1]
                    dka_ref[pp, _blk(j), :] += dk[2 * pp] + dk[2 * pp + 1]
                    dva_ref[pp, _blk(j), :] += dv[2 * pp] + dv[2 * pp + 1]
                col_sums = jnp.zeros((NH, BQ), F32)
                for c in range(nc):
                    col_sums = col_sums + jnp.where(sub == head0 + c, jnp.sum(ds[c], axis=0, keepdims=True), 0.0)
                    rsum_ref[c] += jnp.sum(ds[c], axis=1, keepdims=True)
                dck_ref[:, keys] = dck_ref[:, keys] - col_sums

            def kblock(j, _):
                tile(j, None)
                return 0

            lax.fori_loop(0, i, kblock, 0)
            tile(i, diag)
            row_sums = jnp.zeros((BQ, LANES), F32)
            for c in range(nc):
                row_sums = row_sums + jnp.where(lane == head0 + c, rsum_ref[c], 0.0)
            dcq_ref[_blk(i), :] += row_sums
            for pp in range(npair):
                dp_ref[_blk(i), pp * PAIR_W:pp * PAIR_W + LANES] = (dqa_ref[pp] * SCALE).astype(BF16)
            return 0

        lax.fori_loop(0, NBLK, qblock, 0)
        for pp in range(npair):
            dp_ref[:, pp * PAIR_W + LANES:pp * PAIR_W + 2 * LANES] = dka_ref[pp].astype(BF16)
            dp_ref[:, pp * PAIR_W + 2 * LANES:pp * PAIR_W + 3 * LANES] = dva_ref[pp].astype(BF16)

    return pl.pallas_call(
        body, name=name,
        out_shape=(jax.ShapeDtypeStruct(dproj.shape, BF16), jax.ShapeDtypeStruct((nseq, NH, LP), F32),
                   jax.ShapeDtypeStruct((t, LANES), F32)),
        grid=(nseq, NH // (2 * npair)),
        in_specs=[_group_spec(1, npair), SEQ_SPEC, CROW_SPEC, _gheads_spec(npair), _gheads_spec(npair), _gheads_spec(npair), ANY],
        out_specs=(_group_spec(1, npair), CROW_SPEC, SEQ_SPEC),
        input_output_aliases={6: 0},
        scratch_shapes=_qkv_scratch(npair) + [pltpu.VMEM((npair, BQ, LANES), F32), pltpu.VMEM((npair, LP, LANES), F32),
                                      pltpu.VMEM((npair, LP, LANES), F32), pltpu.VMEM((2 * npair, BQ, 1), F32)],
        compiler_params=_cparams("parallel", "arbitrary"))(proj, c, crow, o32, lse, do, dproj)


def _adamw_math(w, g, m, v):
    m = B1 * m + (1.0 - B1) * g
    v = B2 * v + (1.0 - B2) * (g * g)
    m_hat = m / (1.0 - B1 ** STEP)
    v_hat = v / (1.0 - B2 ** STEP)
    delta = -LR * (m_hat / (jnp.sqrt(v_hat) + EPS) + WD * w)
    return delta, m, v


def _sum_adamw(parts, w, m, v, tr, name):
    rows, cols = w.shape
    cp = parts.shape[2]
    assert rows % tr == 0 and parts.shape[1] == rows

    def body(p_ref, w_ref, m_ref, v_ref, g_ref, d_ref, nm_ref, nv_ref):
        gsum = p_ref[0].astype(F32)
        for s in range(1, N_DEV):
            gsum = gsum + p_ref[s].astype(F32)
        gsum = gsum[:, :cols]
        d, nm, nv = _adamw_math(w_ref[...], gsum, m_ref[...], v_ref[...])
        g_ref[...] = gsum
        d_ref[...] = d
        nm_ref[...] = nm
        nv_ref[...] = nv

    blk = pl.BlockSpec((tr, cols), lambda i: (i, 0))
    out = jax.ShapeDtypeStruct((rows, cols), F32)
    return pl.pallas_call(
        body, name=name, out_shape=(out, out, out, out), grid=(rows // tr,),
        in_specs=[pl.BlockSpec((N_DEV, tr, cp), lambda i: (0, i, 0)), blk, blk, blk],
        out_specs=(blk, blk, blk, blk), compiler_params=_cparams("parallel"))(parts, w, m, v)


def _local_step(x, tgt, meta, tgt_front, g_mix, b_forget, g_ffn, g_final, first_weights, late_weights, early_grads, last_grad):
    nseq = x.shape[0]
    t = nseq * LP
    tm = LP // 2
    mm = functools.partial(_matmul, tm=tm)

    h0 = _pad_rows(meta, x, nseq, "pad_x").reshape(t, D)
    tgt_p = _pad_rows(tgt_front, tgt, nseq, "pad_target").reshape(t, D)
    bf = jnp.pad(b_forget.reshape(1, NH), ((0, 0), (0, LANES - NH)))

    n1 = _norm_fwd(h0, g_mix, "norm1")
    w_in_p, started = first_weights(n1)
    proj = mm(n1, w_in_p, out_dtype=F32, tn=1792, tk=D, after=started, name="in_proj")
    c = _gate_fwd(proj, bf, nseq, "gate_fwd")
    crow = c[:, :NH].reshape(nseq, LP, NH).transpose(0, 2, 1)
    o_sb, rs = _sb_fwd(proj, nseq, 2, "sb_fwd")
    o_fx, o_fx32, lse = _fox_fwd(proj, c, crow, nseq, 1, "fox_fwd")
    w_bsb, w_bfx, w_out, w_up_i, cw_i, w_down = late_weights(o_fx)
    p_sb = mm(o_sb, w_bsb, out_dtype=F32, tn=D, tk=W_ATT, name="branch_sb")
    p_fx = mm(o_fx, w_bfx, out_dtype=F32, tn=D, tk=W_ATT, name="branch_fox")
    merged = _merge_fwd(p_sb, p_fx, proj, "merge_fwd")
    rows = functools.partial(_matmul_rows, tm=LP // 4)
    h1, n2 = rows(merged, w_out, [h0], [g_ffn], _residual_norm, [F32, BF16], [], tk=D, name="out_proj_norm2")
    u = mm(n2, w_up_i, out_dtype=F32, tn=1408, tk=D, name="up_proj")
    act = _conv_glu_fwd(u, cw_i, nseq, "conv_glu_fwd")

    dh2, dh2b, loss, dg_final = rows(act, w_down, [h1, tgt_p], [g_final], _loss_head, [F32, BF16],
                                     [(8, LANES), (1, D)], tk=1408, name="down_proj_loss")
    d_down = _matmul(act, dh2b, out_dtype=BF16, tm=1408, tn=D, tk=tm, ta=True, name="d_w_down")
    dact = mm(dh2b, w_down, out_dtype=F32, tn=1408, tk=D, tb=True, name="d_act")
    du, d_cw = _conv_glu_bwd(u, cw_i, dact, nseq, "conv_glu_bwd")
    d_up = _matmul(n2, du, out_dtype=BF16, tm=D, tn=1408, tk=tm, ta=True, name="d_w_up")
    dh1, dh1b, dg_ffn = rows(du, w_up_i, [h1, dh2], [g_ffn], _residual_norm_bwd, [F32, BF16], [(1, D)],
                             tk=1408, tb=True, name="d_n2_norm2_bwd")
    d_out = _matmul(merged, dh1b, out_dtype=BF16, tm=D, tn=D, tk=tm, ta=True, name="d_w_out")
    dmerged = mm(dh1b, w_out, out_dtype=F32, tn=D, tk=D, tb=True, name="d_merged")
    dp_sb, dproj = _merge_bwd(dmerged, p_sb, proj, None, 0, "merge_bwd_sb")
    dp_fx, dproj = _merge_bwd(dmerged, p_fx, proj, dproj, 1, "merge_bwd_fox")
    d_bsb = _matmul(o_sb, dp_sb, out_dtype=BF16, tm=W_ATT, tn=D, tk=tm, ta=True, name="d_w_branch_sb")
    d_bfx = _matmul(o_fx, dp_fx, out_dtype=BF16, tm=W_ATT, tn=D, tk=tm, ta=True, name="d_w_branch_fox")
    do_sb = mm(dp_sb, w_bsb, out_dtype=BF16, tn=W_ATT, tk=D, tb=True, name="d_o_sb")
    do_fx = mm(dp_fx, w_bfx, out_dtype=BF16, tn=W_ATT, tk=D, tb=True, name="d_o_fox")
    sent = early_grads(dict(w_branch_sb=d_bsb, w_branch_fox=d_bfx, w_out=d_out, w_up=d_up, conv_w=d_cw, w_down=d_down))
    dproj = _sb_bwd(proj, do_sb, rs, dproj, sent, nseq, 2, "sb_bwd")
    dproj, dck, dcq = _fox_bwd(proj, c, crow, o_fx32, lse, do_fx, dproj, nseq, 2, "fox_bwd")
    dc = dcq + jnp.pad(dck.transpose(0, 2, 1).reshape(t, NH), ((0, 0), (0, LANES - NH)))
    dproj, d_bf = _gate_bwd(proj, bf, dc, dproj, nseq, "gate_bwd")
    d_in = _matmul(n1, dproj, out_dtype=BF16, tm=D, tn=1792, tk=tm, ta=True, name="d_w_in")
    dh0, dg_mix = rows(dproj, w_in_p, [h0, dh1], [g_mix], _residual_norm_bwd_f32, [F32], [(1, D)],
                       tk=1792, tb=True, after=last_grad(d_in), name="d_n1_norm1_bwd")
    dh0 = dh0.reshape(nseq, LP, D)
    grads = dict(meta_tokens=jnp.sum(dh0[:, :N_META], axis=0), norm_mix_g=dg_mix, b_forget=d_bf[:, :NH],
                 norm_ffn_g=dg_ffn, norm_final_g=dg_final)
    return loss[0, 0], _real_rows(dh0, nseq, "grad_x"), grads


REPL = (("norm_mix_g", D), ("norm_ffn_g", D), ("norm_final_g", D), ("b_forget", LANES))
REPL_ROWS = 32
META_ROWS = N_META * D // LANES


def _pack_repl(tree):
    rows = [jnp.pad(tree[name].reshape(-1), (0, n - tree[name].size)).reshape(-1, LANES) for name, n in REPL]
    packed = jnp.concatenate(rows, axis=0)
    return jnp.pad(packed, ((0, REPL_ROWS - packed.shape[0]), (0, 0)))


def _unpack_repl(packed, shapes):
    out, r = {}, 0
    for name, n in REPL:
        size = 1
        for s in shapes[name]:
            size *= s
        out[name] = packed[r:r + n // LANES].reshape(-1)[:size].reshape(shapes[name])
        r += n // LANES
    return out


def kernel(x, meta_tokens, norm_mix_g, w_in, b_forget, w_branch_sb, w_branch_fox, w_out, norm_ffn_g, w_up, conv_w, w_down, norm_final_g, loss_target, m_meta_tokens, m_norm_mix_g, m_w_in, m_b_forget, m_w_branch_sb, m_w_branch_fox, m_w_out, m_norm_ffn_g, m_w_up, m_conv_w, m_w_down, m_norm_final_g, v_meta_tokens, v_norm_mix_g, v_w_in, v_b_forget, v_w_branch_sb, v_w_branch_fox, v_w_out, v_norm_ffn_g, v_w_up, v_conv_w, v_w_down, v_norm_final_g):
    w = dict(meta_tokens=meta_tokens, norm_mix_g=norm_mix_g, w_in=w_in, b_forget=b_forget, w_branch_sb=w_branch_sb,
             w_branch_fox=w_branch_fox, w_out=w_out, norm_ffn_g=norm_ffn_g, w_up=w_up, conv_w=conv_w, w_down=w_down,
             norm_final_g=norm_final_g)
    m = dict(meta_tokens=m_meta_tokens, norm_mix_g=m_norm_mix_g, w_in=m_w_in, b_forget=m_b_forget,
             w_branch_sb=m_w_branch_sb, w_branch_fox=m_w_branch_fox, w_out=m_w_out, norm_ffn_g=m_norm_ffn_g,
             w_up=m_w_up, conv_w=m_conv_w, w_down=m_w_down, norm_final_g=m_norm_final_g)
    v = dict(meta_tokens=v_meta_tokens, norm_mix_g=v_norm_mix_g, w_in=v_w_in, b_forget=v_b_forget,
             w_branch_sb=v_w_branch_sb, w_branch_fox=v_w_branch_fox, w_out=v_w_out, norm_ffn_g=v_norm_ffn_g,
             w_up=v_w_up, conv_w=v_conv_w, w_down=v_w_down, norm_final_g=v_norm_final_g)
    shapes = {k: a.shape for k, a in w.items()}
    sharded = ("w_in", "w_branch_sb", "w_branch_fox", "w_out", "w_up", "w_down", "conv_w", "meta_tokens")
    mat = lambda tree, name: tree[name].reshape(tree[name].shape[-2:])

    def lane_pad(a, width):
        return jnp.pad(a, ((0, 0), (0, width - a.shape[1])))

    late = ("w_branch_sb", "w_branch_fox", "w_out", "w_up", "w_down", "conv_w")
    pending_w = {}
    pending_w["in"], in_started = _remote_start(
        [lane_pad(mat(w, "w_in").astype(BF16), SHARD_P)], False, mat(w, "meta_tokens"), "gather_w_in_start")
    g_meta, = _all_gather([mat(w, "meta_tokens") + in_started[0, 0]], "gather_meta")
    meta_full = g_meta.transpose(1, 0, 2).reshape(N_META, D)

    def first_weights(after):
        g_in, = _remote_wait(pending_w["in"], after, "gather_w_in_wait")
        pending_w["late"], started = _remote_start(
            [mat(w, "w_branch_sb").astype(BF16), mat(w, "w_branch_fox").astype(BF16), mat(w, "w_out").astype(BF16),
             lane_pad(mat(w, "w_up").astype(BF16), SHARD_P), mat(w, "w_down").astype(BF16), mat(w, "conv_w")],
            False, g_in, "gather_late_start")
        w_in_p = _relayout(g_in, 1, IN_P, _gathered_to_full(IN_SHARD, _in_padded_to_orig), BF16, 256, "w_in_cols")[0]
        return w_in_p, started

    def late_weights(after):
        g_bsb, g_bfx, g_out, g_up, g_down, g_cw = _remote_wait(pending_w["late"], after, "gather_late_wait")
        w_up_i = _relayout(g_up, 1, 2 * D_FF, _gathered_to_full(UP_SHARD, _up_inter_to_orig), BF16, 256, "w_up_cols")[0]
        w_bsb = _relayout(g_bsb, 1, D, _gathered_to_full(ATT_SHARD, lambda d: d), BF16, 256, "w_bsb_cols")[0]
        w_bfx = _relayout(g_bfx, 1, D, _gathered_to_full(ATT_SHARD, lambda d: d), BF16, 256, "w_bfx_cols")[0]
        cw_full = g_cw.transpose(1, 0, 2).reshape(3, 2 * D_FF)
        cw_i = cw_full.reshape(3, 2, D_FF // FFC, FFC).transpose(0, 2, 1, 3).reshape(3, 2 * D_FF)
        return w_bsb, w_bfx, g_out.reshape(D, D), w_up_i, cw_i, g_down.reshape(D_FF, D)

    pending_g = {}

    def early_grads(g):
        d_cw = g["conv_w"].reshape(3, D_FF // FFC, 2, FFC).transpose(0, 2, 1, 3).reshape(3, 2 * D_FF)
        pending_g["early"], sent = _remote_start(
            [_relayout(g["w_branch_sb"][None], N_DEV, ATT_SHARD, _full_to_shards(ATT_SHARD, lambda c: c), BF16, 256, "d_w_bsb_shards"),
             _relayout(g["w_branch_fox"][None], N_DEV, ATT_SHARD, _full_to_shards(ATT_SHARD, lambda c: c), BF16, 256, "d_w_bfx_shards"),
             g["w_out"].reshape(N_DEV, D // N_DEV, D),
             _relayout(g["w_up"][None], N_DEV, SHARD_P, _full_to_shards(UP_SHARD, _UP_ORIG_TO_INTER.get), BF16, 256, "d_w_up_shards"),
             g["w_down"].reshape(N_DEV, D_FF // N_DEV, D),
             d_cw.reshape(3, N_DEV, UP_SHARD).transpose(1, 0, 2)], True, g["w_out"], "exchange_early_start")
        return sent

    def last_grad(d_in):
        shards = _relayout(d_in[None], N_DEV, SHARD_P, _full_to_shards(IN_SHARD, _IN_ORIG_TO_PADDED.get), BF16, 256, "d_w_in_shards")
        pending_g["last"], sent = _remote_start([shards], True, shards, "exchange_last_start")
        return sent

    loss, grad_x, grads = _local_step(
        x, loss_target, meta_full, jnp.zeros((N_META, D), F32) + in_started[0, 0], norm_mix_g.reshape(1, D), b_forget,
        norm_ffn_g.reshape(1, D), norm_final_g.reshape(1, D), first_weights, late_weights, early_grads, last_grad)

    small = jnp.concatenate([_pack_repl(grads), grads["meta_tokens"].reshape(META_ROWS, LANES)], axis=0)
    small, = _all_gather([small], "gather_small_grads")
    me_idx = 4 * lax.axis_index("x") + 2 * lax.axis_index("y") + lax.axis_index("c")
    p_meta = lax.dynamic_slice_in_dim(small[:, REPL_ROWS:].reshape(N_DEV, N_META, D), me_idx * ATT_SHARD, ATT_SHARD, axis=2)

    p_in, = _remote_wait(pending_g["last"], small, "exchange_last_wait")
    parts = dict(zip(late, _remote_wait(pending_g["early"], p_in, "exchange_early_wait")), w_in=p_in, meta_tokens=p_meta)
    tiles = dict(w_in=256, w_branch_sb=256, w_branch_fox=256, w_out=D // N_DEV, w_up=256, w_down=D_FF // N_DEV,
                 conv_w=3, meta_tokens=N_META)
    new = {name: _sum_adamw(parts[name], mat(w, name), mat(m, name), mat(v, name), tiles[name], "adamw_" + name)
           for name in sharded}

    routs = _sum_adamw(small[:, :REPL_ROWS], _pack_repl(w), _pack_repl(m), _pack_repl(v), REPL_ROWS, "adamw_replicated")
    repl = [_unpack_repl(o, shapes) for o in routs]

    result = [lax.psum(loss, ("x", "y", "c")), grad_x]
    for k in range(4):
        for name in w:
            result.append(new[name][k].reshape(shapes[name]) if name in new else repl[k][name])
    return tuple(result)
```

```python
import functools

import jax
import jax.numpy as jnp
from jax import lax
from jax.experimental import pallas as pl
from jax.experimental.pallas import tpu as pltpu

F32 = jnp.float32
BF16 = jnp.bfloat16

N_DEV = 8
LANES = 128
D = 1024
N_META = 16
SEQ = 2048
L_REAL = N_META + SEQ
LP = 2304
BQ = 256
NBLK = LP // BQ
HEAD = 64
NH = 8
W_ATT = NH * HEAD
PAIR_W = 3 * LANES
D_FF = 2816
IN_COLS = 5128
QKV = 6 * W_ATT
IN_P = 5376
GATE_COL = QKV
F_COL = QKV + 2 * D
FFC = 256
RMS_EPS = 1e-6
LR, B1, B2, EPS, WD, STEP = 0.001, 0.9, 0.999, 1e-08, 0.01, 10
VMEM_LIMIT = 56 * 1024 * 1024

MESH = pl.DeviceIdType.MESH
ANY = pl.BlockSpec(memory_space=pl.ANY)


def _cparams(*sem):
    return pltpu.CompilerParams(dimension_semantics=sem if sem else None, vmem_limit_bytes=VMEM_LIMIT)


def _all_gather(xs, name):
    n = len(xs)

    def body(*refs):
        x_refs, out_refs = refs[:n], refs[n:2 * n]
        send_sems, recv_sems, local_sems = refs[2 * n:]
        mx, my, mc = lax.axis_index("x"), lax.axis_index("y"), lax.axis_index("c")
        me, sibling = (mx, my, mc), (mx, my, 1 - mc)
        chips = [(1 - mx, my), (mx, 1 - my), (1 - mx, 1 - my)]

        def copy(a, k, block, to, own=False):
            px, py, pc = block
            slot = out_refs[a].at[4 * px + 2 * py + pc]
            return pltpu.make_async_remote_copy(
                src_ref=x_refs[a] if own else slot, dst_ref=slot,
                send_sem=send_sems.at[7 * a + k], recv_sem=recv_sems.at[7 * a + k],
                device_id=to, device_id_type=MESH)

        mine = [pltpu.make_async_copy(x_refs[a], out_refs[a].at[4 * mx + 2 * my + mc], local_sems.at[a]) for a in range(n)]
        for cp in mine:
            cp.start()
        first = []
        for a in range(n):
            first.append(copy(a, 0, me, sibling, own=True))
            first += [copy(a, 1 + j, me, (*chip, mc), own=True) for j, chip in enumerate(chips)]
        for cp in first:
            cp.start()
        passed = []
        for j, chip in enumerate(chips):
            for a in range(n):
                copy(a, 1 + j, (*chip, mc), me).wait_recv()
                fwd = copy(a, 4 + j, (*chip, mc), sibling)
                fwd.start()
                passed.append(fwd)
        for a in range(n):
            copy(a, 0, sibling, me).wait_recv()
            for j, chip in enumerate(chips):
                copy(a, 4 + j, (*chip, 1 - mc), me).wait_recv()
        for cp in first + passed:
            cp.wait_send()
        for cp in mine:
            cp.wait()

    return pl.pallas_call(
        body, name=name,
        out_shape=tuple(jax.ShapeDtypeStruct((N_DEV,) + x.shape, x.dtype) for x in xs),
        in_specs=[ANY] * n, out_specs=tuple([ANY] * n),
        scratch_shapes=[pltpu.SemaphoreType.DMA((7 * n,)), pltpu.SemaphoreType.DMA((7 * n,)),
                        pltpu.SemaphoreType.DMA((n,))],
    )(*xs)


HBM = pl.BlockSpec(memory_space=pltpu.HBM)
SEM = pl.BlockSpec(memory_space=pltpu.SEMAPHORE)
EFFECT = pltpu.SideEffectType.DATAFLOW_SIDE_EFFECTING


def _peer_copies(src_refs, land_refs, send_sems, recv_sems, per_peer):
    mx, my, mc = lax.axis_index("x"), lax.axis_index("y"), lax.axis_index("c")
    me_idx = 4 * mx + 2 * my + mc
    copies = []
    for k in range(1, N_DEV):
        px, py, pc = mx ^ (k >> 2), my ^ ((k >> 1) & 1), mc ^ (k & 1)
        for a, (src, land) in enumerate(zip(src_refs, land_refs)):
            copies.append(pltpu.make_async_remote_copy(
                src_ref=src.at[4 * px + 2 * py + pc] if per_peer else src, dst_ref=land.at[me_idx],
                send_sem=send_sems.at[7 * a + k - 1], recv_sem=recv_sems.at[7 * a + k - 1],
                device_id=(px, py, pc), device_id_type=MESH))
    return me_idx, copies


def _remote_start(srcs, per_peer, after, name):
    n = len(srcs)
    lands = [lax.empty(s.shape if per_peer else (N_DEV,) + s.shape, s.dtype) for s in srcs]

    def body(*refs):
        src_refs, land_refs = refs[:n], refs[n:2 * n]
        send_sems, recv_sems = refs[2 * n + 1:2 * n + 3]
        token = refs[4 * n + 3]
        stage, local_sems = refs[4 * n + 4:5 * n + 4], refs[5 * n + 4]
        me_idx, copies = _peer_copies(src_refs, land_refs, send_sems, recv_sems, per_peer)
        for cp in copies:
            cp.start()
        own = [src_refs[a].at[me_idx] if per_peer else src_refs[a] for a in range(n)]
        for hop in ([(own[a], stage[a]) for a in range(n)], [(stage[a], land_refs[a].at[me_idx]) for a in range(n)]):
            cps = [pltpu.make_async_copy(s, d, local_sems.at[a]) for a, (s, d) in enumerate(hop)]
            for cp in cps:
                cp.start()
            for cp in cps:
                cp.wait()
        token[...] = jnp.zeros_like(token)

    thru = [pltpu.HBM(a.shape, a.dtype) for a in list(srcs) + lands]
    out = pl.pallas_call(
        body, name=name,
        out_shape=(pltpu.SemaphoreType.DMA((7 * n,)), pltpu.SemaphoreType.DMA((7 * n,)), *thru,
                   jax.ShapeDtypeStruct((8, LANES), F32)),
        in_specs=[HBM] * (2 * n) + [ANY],
        out_specs=(SEM, SEM, *([HBM] * (2 * n)), pl.BlockSpec(memory_space=pltpu.VMEM)),
        input_output_aliases={i: 2 + i for i in range(2 * n)},
        scratch_shapes=[pltpu.VMEM(s.shape[1:] if per_peer else s.shape, s.dtype) for s in srcs]
        + [pltpu.SemaphoreType.DMA((n,))],
        compiler_params=pltpu.CompilerParams(has_side_effects=EFFECT),
    )(*[pltpu.with_memory_space_constraint(a, pltpu.HBM) for a in list(srcs) + lands], after)
    return dict(sems=out[:2], bufs=out[2:2 * n + 2], per_peer=per_peer), out[-1]


def _remote_wait(pending, after, name):
    bufs = pending["bufs"]
    n = len(bufs) // 2
    per_peer = pending["per_peer"]

    def body(*refs):
        src_refs, land_refs = refs[:n], refs[n:2 * n]
        send_sems, recv_sems = refs[2 * n:2 * n + 2]
        _, copies = _peer_copies(src_refs, land_refs, send_sems, recv_sems, per_peer)
        for cp in copies:
            cp.wait_send()
        for cp in copies:
            cp.wait_recv()

    out = pl.pallas_call(
        body, name=name, out_shape=tuple(pltpu.HBM(a.shape, a.dtype) for a in bufs),
        in_specs=[HBM] * (2 * n) + [SEM, SEM, ANY], out_specs=tuple([HBM] * (2 * n)),
        input_output_aliases={i: i for i in range(2 * n)},
        compiler_params=pltpu.CompilerParams(has_side_effects=EFFECT),
    )(*bufs, *pending["sems"], after)
    return out[n:]


ROWS_PER_COPY = 256


def _pad_rows(front, body_rows, nseq, name):
    tail = LP - L_REAL
    nblk = SEQ // ROWS_PER_COPY

    def body(f_ref, b_ref, o_ref, z_ref, sems):
        s, i = pl.program_id(0), pl.program_id(1)
        rows = pltpu.make_async_copy(b_ref, o_ref.at[pl.ds(s, 1), pl.ds(N_META + i * ROWS_PER_COPY, ROWS_PER_COPY)], sems.at[0])
        rows.start()

        @pl.when(i == 0)
        def _():
            z_ref[...] = jnp.zeros_like(z_ref)
            head = pltpu.make_async_copy(f_ref, o_ref.at[s, pl.ds(0, N_META)], sems.at[1])
            zeros = pltpu.make_async_copy(z_ref, o_ref.at[s, pl.ds(L_REAL, tail)], sems.at[2])
            head.start()
            zeros.start()
            head.wait()
            zeros.wait()

        rows.wait()

    return pl.pallas_call(
        body, name=name, out_shape=jax.ShapeDtypeStruct((nseq, LP, D), F32), grid=(nseq, nblk),
        in_specs=[pl.BlockSpec((N_META, D), lambda s, i: (0, 0)), pl.BlockSpec((1, ROWS_PER_COPY, D), lambda s, i: (s, i, 0))],
        out_specs=ANY,
        scratch_shapes=[pltpu.VMEM((tail, D), F32), pltpu.SemaphoreType.DMA((3,))],
        compiler_params=_cparams("arbitrary", "arbitrary"))(front, body_rows)


def _real_rows(h, nseq, name):
    nblk = SEQ // ROWS_PER_COPY

    def body(h_ref, o_ref, sem):
        s, i = pl.program_id(0), pl.program_id(1)
        rows = pltpu.make_async_copy(h_ref.at[pl.ds(s, 1), pl.ds(N_META + i * ROWS_PER_COPY, ROWS_PER_COPY)], o_ref, sem)
        rows.start()
        rows.wait()

    return pl.pallas_call(
        body, name=name, out_shape=jax.ShapeDtypeStruct((nseq, SEQ, D), F32), grid=(nseq, nblk),
        in_specs=[ANY], out_specs=pl.BlockSpec((1, ROWS_PER_COPY, D), lambda s, i: (s, i, 0)),
        scratch_shapes=[pltpu.SemaphoreType.DMA],
        compiler_params=_cparams("arbitrary", "arbitrary"))(h)


def _plan_cols(n_q, n_dcols, src_of):
    plan = {}
    for q in range(n_q):
        for dblk in range(n_dcols // LANES):
            segs, key, start = [], None, 0
            for lane in range(LANES + 1):
                new = None
                if lane < LANES:
                    src = src_of(q, dblk * LANES + lane)
                    if src is not None:
                        new = (src[0], src[1] // LANES, (lane - src[1] % LANES) % LANES)
                if new != key:
                    if key is not None:
                        segs.append((*key, start, lane))
                    key, start = new, lane
            plan[(q, dblk)] = segs
    return plan


def _relayout(src, n_q, n_dcols, src_of, out_dtype, tr, name):
    n_p, rows, scols = src.shape
    plan = _plan_cols(n_q, n_dcols, src_of)

    def body(s_ref, d_ref):
        lane = lax.broadcasted_iota(jnp.int32, (tr, LANES), 1)
        for (q, dblk), segs in plan.items():
            acc = jnp.zeros((tr, LANES), F32)
            for p, sblk, rot, lo, hi in segs:
                x = s_ref[p, :, sblk * LANES:(sblk + 1) * LANES].astype(F32)
                if rot:
                    x = pltpu.roll(x, rot, 1)
                acc = x if (lo, hi) == (0, LANES) else jnp.where((lane >= lo) & (lane < hi), x, acc)
            d_ref[q, :, dblk * LANES:(dblk + 1) * LANES] = acc.astype(out_dtype)

    return pl.pallas_call(
        body, name=name, out_shape=jax.ShapeDtypeStruct((n_q, rows, n_dcols), out_dtype), grid=(rows // tr,),
        in_specs=[pl.BlockSpec((n_p, tr, scols), lambda i: (0, i, 0))],
        out_specs=pl.BlockSpec((n_q, tr, n_dcols), lambda i: (0, i, 0)),
        compiler_params=_cparams("parallel"))(src)


def _in_padded_to_orig(d):
    if d < QKV:
        kind, r = divmod(d, 4 * PAIR_W)
        pair, r = divmod(r, PAIR_W)
        part, r = divmod(r, LANES)
        return kind * 3 * W_ATT + part * W_ATT + pair * LANES + r
    if d < F_COL:
        return d + NH
    if d < F_COL + NH:
        return d - 2 * D
    return None


_IN_ORIG_TO_PADDED = {_in_padded_to_orig(d): d for d in range(IN_P) if _in_padded_to_orig(d) is not None}


def _up_inter_to_orig(d):
    j, r = divmod(d, 2 * FFC)
    part, r = divmod(r, FFC)
    return part * D_FF + j * FFC + r


_UP_ORIG_TO_INTER = {_up_inter_to_orig(d): d for d in range(2 * D_FF)}
IN_SHARD = IN_COLS // N_DEV
UP_SHARD = 2 * D_FF // N_DEV
SHARD_P = 768
ATT_SHARD = D // N_DEV


def _gathered_to_full(n_shard, to_orig):
    def src_of(q, d):
        c = to_orig(d)
        return None if c is None else (c // n_shard, c % n_shard)
    return src_of


def _full_to_shards(n_shard, from_orig):
    def src_of(q, d):
        return (0, from_orig(q * n_shard + d)) if d < n_shard else None
    return src_of


def _matmul(a, b, *, out_dtype, tm, tn, tk, ta=False, tb=False, after=None, name):
    if ta:
        kdim, m = a.shape
    else:
        m, kdim = a.shape
    n = b.shape[0] if tb else b.shape[1]
    assert m % tm == 0 and n % tn == 0 and kdim % tk == 0, (name, a.shape, b.shape, tm, tn, tk)
    nk = kdim // tk

    def body(a_ref, b_ref, *rest):
        o_ref, scratch = rest[len(extra)], rest[len(extra) + 1:]
        av, bv = a_ref[...], b_ref[...]
        if ta:
            p = lax.dot_general(av, bv, (((0,), (0,)), ((), ())), preferred_element_type=F32)
        elif tb:
            p = lax.dot_general(av, bv, (((1,), (1,)), ((), ())), preferred_element_type=F32)
        else:
            p = jnp.dot(av, bv, preferred_element_type=F32)
        if nk == 1:
            o_ref[...] = p.astype(o_ref.dtype)
        else:
            acc_ref, = scratch
            k = pl.program_id(2)

            @pl.when(k == 0)
            def _():
                acc_ref[...] = p

            @pl.when(k > 0)
            def _():
                acc_ref[...] += p

            @pl.when(k == nk - 1)
            def _():
                o_ref[...] = acc_ref[...].astype(o_ref.dtype)

    extra = [] if after is None else [after]
    a_spec = pl.BlockSpec((tk, tm), lambda i, j, k: (k, i)) if ta else pl.BlockSpec((tm, tk), lambda i, j, k: (i, k))
    b_spec = pl.BlockSpec((tn, tk), lambda i, j, k: (j, k)) if tb else pl.BlockSpec((tk, tn), lambda i, j, k: (k, j))
    return pl.pallas_call(
        body, name=name,
        out_shape=jax.ShapeDtypeStruct((m, n), out_dtype),
        grid=(m // tm, n // tn, nk),
        in_specs=[a_spec, b_spec] + [ANY] * len(extra),
        out_specs=pl.BlockSpec((tm, tn), lambda i, j, k: (i, j)),
        scratch_shapes=[] if nk == 1 else [pltpu.VMEM((tm, tn), F32)],
        compiler_params=_cparams("parallel", "parallel", "arbitrary"),
    )(a, b, *extra)


TR = 288


def _rms(h):
    return lax.rsqrt(jnp.mean(h * h, axis=-1, keepdims=True) + RMS_EPS)


def _norm_fwd(h, g, name):
    t = h.shape[0]
    row = pl.BlockSpec((TR, D), lambda i: (i, 0))

    def body(h_ref, g_ref, n_ref):
        hv = h_ref[...]
        n_ref[...] = ((hv * _rms(hv)) * g_ref[...]).astype(BF16)

    return pl.pallas_call(
        body, name=name, out_shape=jax.ShapeDtypeStruct((t, D), BF16), grid=(t // TR,),
        in_specs=[row, pl.BlockSpec((1, D), lambda i: (0, 0))], out_specs=row, compiler_params=_cparams("parallel"))(h, g)


EPI_ROWS = 144


def _matmul_rows(a, b, rows_in, vecs_in, epilogue, row_outs, sum_outs, *, tm, tk, tb=False, after=None, name):
    m, kdim = a.shape
    assert (b.shape[0] if tb else b.shape[1]) == D and m % tm == 0 and kdim % tk == 0 and tm % EPI_ROWS == 0
    nk = kdim // tk
    n_r, n_v, n_ro, n_so = len(rows_in), len(vecs_in), len(row_outs), len(sum_outs)
    extra = [] if after is None else [after]

    def body(a_ref, b_ref, *rest):
        r_refs, v_refs = rest[:n_r], rest[n_r:n_r + n_v]
        outs = rest[n_r + n_v + len(extra):]
        ro_refs, so_refs, acc_ref = outs[:n_ro], outs[n_ro:n_ro + n_so], outs[n_ro + n_so]
        i, k = pl.program_id(0), pl.program_id(1)
        if tb:
            p = lax.dot_general(a_ref[...], b_ref[...], (((1,), (1,)), ((), ())), preferred_element_type=F32)
        else:
            p = jnp.dot(a_ref[...], b_ref[...], preferred_element_type=F32)

        @pl.when(k == 0)
        def _():
            acc_ref[...] = p

        @pl.when(k > 0)
        def _():
            acc_ref[...] += p

        @pl.when(k == nk - 1)
        def _():
            vecs = [v[...] for v in v_refs]

            def step(c, sums):
                rows = pl.ds(pl.multiple_of(c * EPI_ROWS, 8), EPI_ROWS)
                tiles, terms = epilogue(i * tm + c * EPI_ROWS, acc_ref[rows, :], *[r[rows, :] for r in r_refs], *vecs)
                for o, tile in zip(ro_refs, tiles):
                    o[rows, :] = tile.astype(o.dtype)
                return tuple(s + term for s, term in zip(sums, terms))

            sums = lax.fori_loop(0, tm // EPI_ROWS, step, tuple(jnp.zeros(s, F32) for s in sum_outs))

            @pl.when(i == 0)
            def _():
                for o in so_refs:
                    o[...] = jnp.zeros_like(o)

            for o, s in zip(so_refs, sums):
                o[...] += s

    row = pl.BlockSpec((tm, D), lambda i, k: (i, 0))
    b_spec = pl.BlockSpec((D, tk), lambda i, k: (0, k)) if tb else pl.BlockSpec((tk, D), lambda i, k: (k, 0))
    return pl.pallas_call(
        body, name=name,
        out_shape=tuple([jax.ShapeDtypeStruct((m, D), dt) for dt in row_outs] + [jax.ShapeDtypeStruct(s, F32) for s in sum_outs]),
        grid=(m // tm, nk),
        in_specs=[pl.BlockSpec((tm, tk), lambda i, k: (i, k)), b_spec] + [row] * n_r
        + [pl.BlockSpec((1, D), lambda i, k: (0, 0))] * n_v + [ANY] * len(extra),
        out_specs=tuple([row] * n_ro + [pl.BlockSpec(s, lambda i, k: (0, 0)) for s in sum_outs]),
        scratch_shapes=[pltpu.VMEM((tm, D), F32)],
        compiler_params=_cparams("arbitrary", "arbitrary"))(a, b, *rows_in, *vecs_in, *extra)


def _residual_norm(row0, acc, h, g):
    hv = h + acc
    return (hv, (hv * _rms(hv)) * g), ()


def _rms_bwd_math(hv, dn, gv):
    r = _rms(hv)
    hr = hv * r
    dng = dn * gv
    dh = r * (dng - hr * jnp.mean(dng * hr, axis=-1, keepdims=True))
    return dh, dn * hr


def _loss_head(row0, acc, h1, tgt, g):
    hv = h1 + acc
    hr = hv * _rms(hv)
    pos = row0 % LP + lax.broadcasted_iota(jnp.int32, (EPI_ROWS, 1), 0)
    valid = (pos >= N_META) & (pos < L_REAL)
    err = jnp.where(valid, hr * g - tgt, 0.0)
    part = 0.5 * jnp.sum(jnp.mean(err * err, axis=-1, keepdims=True))
    dy = err * (1.0 / D)
    dh, dgrow = _rms_bwd_math(hv, dy, g)
    return (dh, dh), (jnp.full((8, LANES), part, F32), jnp.sum(dgrow, axis=0, keepdims=True))


def _residual_norm_bwd(row0, acc, h, dres, g):
    dh, dgrow = _rms_bwd_math(h, acc, g)
    dh = dh + dres
    return (dh, dh), (jnp.sum(dgrow, axis=0, keepdims=True),)


def _residual_norm_bwd_f32(row0, acc, h, dres, g):
    tiles, sums = _residual_norm_bwd(row0, acc, h, dres, g)
    return tiles[:1], sums


GATE_BLK = GATE_COL // D


def _sigmoid(x):
    return 1.0 / (1.0 + jnp.exp(-x))


def _merge_fwd(p_sb, p_fx, proj, name):
    t = p_sb.shape[0]
    row = pl.BlockSpec((TR, D), lambda i: (i, 0))

    def body(ps_ref, pf_ref, gs_ref, gf_ref, o_ref):
        o_ref[...] = (_sigmoid(gs_ref[...]) * ps_ref[...] + _sigmoid(gf_ref[...]) * pf_ref[...]).astype(BF16)

    return pl.pallas_call(
        body, name=name, out_shape=jax.ShapeDtypeStruct((t, D), BF16), grid=(t // TR,),
        in_specs=[row, row, pl.BlockSpec((TR, D), lambda i: (i, GATE_BLK)),
                  pl.BlockSpec((TR, D), lambda i: (i, GATE_BLK + 1))],
        out_specs=row, compiler_params=_cparams("parallel"))(p_sb, p_fx, proj, proj)


def _merge_bwd(dm, p, proj, dproj, which, name):
    t = dm.shape[0]
    row = pl.BlockSpec((TR, D), lambda i: (i, 0))
    gate = pl.BlockSpec((TR, D), lambda i: (i, GATE_BLK + which))

    def body(dm_ref, p_ref, g_ref, *rest):
        dp_ref, dg_ref = rest[-2:]
        dmv = dm_ref[...]
        s = _sigmoid(g_ref[...])
        dp_ref[...] = (dmv * s).astype(BF16)
        dg_ref[...] = (dmv * p_ref[...] * s * (1.0 - s)).astype(BF16)

    out_shape = (jax.ShapeDtypeStruct((t, D), BF16), jax.ShapeDtypeStruct((t, IN_P), BF16))
    if dproj is None:
        return pl.pallas_call(
            body, name=name, out_shape=out_shape, grid=(t // TR,), in_specs=[row, row, gate],
            out_specs=(row, gate), compiler_params=_cparams("parallel"))(dm, p, proj)
    return pl.pallas_call(
        body, name=name, out_shape=out_shape, grid=(t // TR,), in_specs=[row, row, gate, ANY],
        out_specs=(row, gate), input_output_aliases={3: 1}, compiler_params=_cparams("parallel"))(dm, p, proj, dproj)


CH = 288


def _chunk(c, n=CH):
    return pl.ds(pl.multiple_of(c * CH, 8), n)


def _conv_taps(u_ref, c):
    x = u_ref[_chunk(c), :]
    prev = u_ref[pl.ds(pl.multiple_of(jnp.maximum(c * CH - 8, 0), 8), 8), :]
    xx = jnp.concatenate([jnp.where(c == 0, 0.0, prev), x], axis=0)
    return x, pltpu.roll(xx, 1, 0)[8:], pltpu.roll(xx, 2, 0)[8:]


def _conv_glu_fwd(u, cw, nseq, name):
    nblk = D_FF // FFC

    def body(u_ref, cw_ref, o_ref):
        cwv = cw_ref[...]

        def step(c, _):
            x, x1, x2 = _conv_taps(u_ref, c)
            uc = cwv[0:1, :] * x2 + cwv[1:2, :] * x1 + cwv[2:3, :] * x
            a, b = uc[:, :FFC], uc[:, FFC:]
            o_ref[_chunk(c), :] = (a * _sigmoid(a) * b).astype(BF16)
            return 0

        lax.fori_loop(0, LP // CH, step, 0)

    return pl.pallas_call(
        body, name=name, out_shape=jax.ShapeDtypeStruct((nseq * LP, D_FF), BF16), grid=(nseq, nblk),
        in_specs=[pl.BlockSpec((LP, 2 * FFC), lambda s, j: (s, j)), pl.BlockSpec((3, 2 * FFC), lambda s, j: (0, j))],
        out_specs=pl.BlockSpec((LP, FFC), lambda s, j: (s, j)),
        compiler_params=_cparams("parallel", "parallel"))(u, cw)


def _conv_glu_bwd(u, cw, dact, nseq, name):
    nblk = D_FF // FFC
    nch = LP // CH

    def body(u_ref, cw_ref, da_ref, du_ref, dcw_ref):
        s = pl.program_id(1)
        cwv = cw_ref[...]

        def step(k, carry):
            nxt, p0, p1, p2 = carry
            c = nch - 1 - k
            x, x1, x2 = _conv_taps(u_ref, c)
            uc = cwv[0:1, :] * x2 + cwv[1:2, :] * x1 + cwv[2:3, :] * x
            a, b = uc[:, :FFC], uc[:, FFC:]
            sa = _sigmoid(a)
            dactv = da_ref[_chunk(c), :]
            da = dactv * b * (sa * (1.0 + a * (1.0 - sa)))
            db = dactv * (a * sa)
            duc = jnp.concatenate([da, db], axis=1)
            dd = jnp.concatenate([duc, nxt], axis=0)
            du = (cwv[2:3, :] * duc + cwv[1:2, :] * pltpu.roll(dd, CH + 7, 0)[:CH]
                  + cwv[0:1, :] * pltpu.roll(dd, CH + 6, 0)[:CH])
            du_ref[_chunk(c), :] = du.astype(BF16)
            return (duc[:8], p0 + jnp.sum(duc * x2, axis=0, keepdims=True),
                    p1 + jnp.sum(duc * x1, axis=0, keepdims=True), p2 + jnp.sum(duc * x, axis=0, keepdims=True))

        zrow = jnp.zeros((1, 2 * FFC), F32)
        _, p0, p1, p2 = lax.fori_loop(0, nch, step, (jnp.zeros((8, 2 * FFC), F32), zrow, zrow, zrow))

        @pl.when(s == 0)
        def _():
            dcw_ref[...] = jnp.zeros_like(dcw_ref)

        dcw_ref[...] += jnp.concatenate([p0, p1, p2], axis=0)

    return pl.pallas_call(
        body, name=name,
        out_shape=(jax.ShapeDtypeStruct((nseq * LP, 2 * D_FF), BF16), jax.ShapeDtypeStruct((3, 2 * D_FF), F32)),
        grid=(nblk, nseq),
        in_specs=[pl.BlockSpec((LP, 2 * FFC), lambda j, s: (s, j)), pl.BlockSpec((3, 2 * FFC), lambda j, s: (0, j)),
                  pl.BlockSpec((LP, FFC), lambda j, s: (s, j))],
        out_specs=(pl.BlockSpec((LP, 2 * FFC), lambda j, s: (s, j)), pl.BlockSpec((3, 2 * FFC), lambda j, s: (0, j))),
        compiler_params=_cparams("parallel", "arbitrary"))(u, cw, dact)


F_BLK = F_COL // LANES
CB = 128


def _split3(x):
    hi = x.astype(BF16)
    r1 = x - hi.astype(F32)
    mid = r1.astype(BF16)
    lo = (r1 - mid.astype(F32)).astype(BF16)
    return hi, mid, lo


def _tri_dot(tri, x):
    hi, mid, lo = _split3(x)
    d = functools.partial(jnp.dot, preferred_element_type=F32)
    return d(tri, hi) + d(tri, mid) + d(tri, lo)


def _log_sigmoid(x):
    return jnp.minimum(x, 0.0) - jnp.log(1.0 + jnp.exp(-jnp.abs(x)))


def _gate_fwd(proj, bf, nseq, name):
    def body(f_ref, b_ref, c_ref):
        r_i = lax.broadcasted_iota(jnp.int32, (CB, CB), 0)
        c_i = lax.broadcasted_iota(jnp.int32, (CB, CB), 1)
        tri = (c_i <= r_i).astype(BF16)
        bv = b_ref[...]

        def step(k, carry):
            rows = pl.ds(pl.multiple_of(k * CB, CB), CB)
            lf = _log_sigmoid(f_ref[rows, :] + bv)
            c_ref[rows, :] = _tri_dot(tri, lf) + carry
            return carry + jnp.sum(lf, axis=0, keepdims=True)

        lax.fori_loop(0, LP // CB, step, jnp.zeros((1, LANES), F32))

    return pl.pallas_call(
        body, name=name, out_shape=jax.ShapeDtypeStruct((nseq * LP, LANES), F32), grid=(nseq,),
        in_specs=[pl.BlockSpec((LP, LANES), lambda s: (s, F_BLK)), pl.BlockSpec((1, LANES), lambda s: (0, 0))],
        out_specs=pl.BlockSpec((LP, LANES), lambda s: (s, 0)),
        compiler_params=_cparams("parallel"))(proj, bf)


def _gate_bwd(proj, bf, dc, dproj, nseq, name):
    def body(f_ref, b_ref, dc_ref, _, df_ref, db_ref):
        s = pl.program_id(0)
        r_i = lax.broadcasted_iota(jnp.int32, (CB, CB), 0)
        c_i = lax.broadcasted_iota(jnp.int32, (CB, CB), 1)
        tri = (c_i >= r_i).astype(BF16)
        bv = b_ref[...]

        def step(kk, carry):
            carry_c, carry_b = carry
            k = LP // CB - 1 - kk
            rows = pl.ds(pl.multiple_of(k * CB, CB), CB)
            dcv = dc_ref[rows, :]
            dlf = _tri_dot(tri, dcv) + carry_c
            df = dlf * _sigmoid(-(f_ref[rows, :] + bv))
            df_ref[rows, :] = jnp.concatenate([df, jnp.zeros_like(df)], axis=1).astype(BF16)
            return carry_c + jnp.sum(dcv, axis=0, keepdims=True), carry_b + jnp.sum(df, axis=0, keepdims=True)

        zero = jnp.zeros((1, LANES), F32)
        _, dbp = lax.fori_loop(0, LP // CB, step, (zero, zero))

        @pl.when(s == 0)
        def _():
            db_ref[...] = jnp.zeros_like(db_ref)

        db_ref[...] += dbp

    return pl.pallas_call(
        body, name=name,
        out_shape=(jax.ShapeDtypeStruct(dproj.shape, BF16), jax.ShapeDtypeStruct((1, LANES), F32)), grid=(nseq,),
        in_specs=[pl.BlockSpec((LP, LANES), lambda s: (s, F_BLK)), pl.BlockSpec((1, LANES), lambda s: (0, 0)),
                  pl.BlockSpec((LP, LANES), lambda s: (s, 0)), ANY],
        out_specs=(pl.BlockSpec((LP, 2 * LANES), lambda s: (s, F_COL // (2 * LANES))), pl.BlockSpec((1, LANES), lambda s: (0, 0))),
        input_output_aliases={3: 0},
        compiler_params=_cparams("arbitrary"))(proj, bf, dc, dproj)


SCALE = 0.125
NEG = -1e30


def _dot_nt(a, b):
    return lax.dot_general(a, b, (((1,), (1,)), ((), ())), preferred_element_type=F32)


def _dot_tn(a, b):
    return lax.dot_general(a, b, (((0,), (0,)), ((), ())), preferred_element_type=F32)


def _dot(a, b):
    return jnp.dot(a, b, preferred_element_type=F32)


def _blk(i):
    return pl.ds(pl.multiple_of(i * BQ, BQ), BQ)


def _tile_iotas():
    return lax.broadcasted_iota(jnp.int32, (BQ, BQ), 0), lax.broadcasted_iota(jnp.int32, (BQ, BQ), 1)


def _lane_iota():
    return lax.broadcasted_iota(jnp.int32, (BQ, LANES), 1)


def _head_masks():
    lane = _lane_iota()
    return lane < HEAD, lane >= HEAD


def _only(mask, x):
    return jnp.where(mask, x, jnp.zeros_like(x))


def _pick_lane(x, idx):
    return jnp.sum(jnp.where(_lane_iota() == idx, x, 0.0), axis=1, keepdims=True)


def _chains(npair):
    return [(pp, h) for pp in range(npair) for h in range(2)]


def _load_qkv(p_ref, q_s, k_s, v_s):
    for pp in range(q_s.shape[0]):
        base = pp * PAIR_W
        q_s[pp] = (p_ref[:, base:base + LANES] * SCALE).astype(BF16)
        k_s[pp] = p_ref[:, base + LANES:base + 2 * LANES].astype(BF16)
        v_s[pp] = p_ref[:, base + 2 * LANES:base + 3 * LANES].astype(BF16)


def _softplus(z):
    return jnp.maximum(z, 0.0) + jnp.log(1.0 + jnp.exp(-jnp.abs(z)))


def _hi_lo(x):
    hi = x.astype(BF16)
    return hi, (x - hi.astype(F32)).astype(BF16)


def _sb_tile_weights(q, k, strict, r, u_suf):
    n = len(q)
    z = [_dot_nt(q[c], k[c]) for c in range(n)]
    sp = [_softplus(zc) for zc in z]
    lk = [-spc if strict is None else jnp.where(strict, -spc, 0.0) for spc in sp]
    parts = [_hi_lo(lkc) for lkc in lk]
    suf = [_dot(hi, u_suf) + _dot(lo, u_suf) for hi, lo in parts]
    w = [jnp.exp(z[c] - sp[c] + r[c] + suf[c]) for c in range(n)]
    if strict is not None:
        w = [jnp.where(strict, wc, 0.0) for wc in w]
    r_next = [r[c] + suf[c][:, 0:1] + lk[c][:, 0:1] for c in range(n)]
    return w, sp, r_next


def _group_spec(kind, npair):
    return pl.BlockSpec((LP, npair * PAIR_W), lambda s, g: (s, (NH // (2 * npair)) * kind + g))


def _gheads_spec(npair):
    return pl.BlockSpec((LP, npair * LANES), lambda s, g: (s, g))


def _qkv_scratch(npair):
    return [pltpu.VMEM((npair, LP, LANES), BF16)] * 3


SEQ_SPEC = pl.BlockSpec((LP, LANES), lambda s, g: (s, 0))
RS_STRIDE = 16


def _pair_cols(pp):
    return slice(pp * LANES, (pp + 1) * LANES)


def _sb_fwd(proj, nseq, npair, name):
    t = nseq * LP
    chains = _chains(npair)

    def body(p_ref, o_ref, rs_ref, q_s, k_s, v_s, acc_ref, r_ref, rb_ref):
        _load_qkv(p_ref, q_s, k_s, v_s)
        row, col = _tile_iotas()
        u_suf = (row > col).astype(BF16)
        diag = col < row
        lane = _lane_iota()
        heads = _head_masks()

        def qblock(i, _):
            acc_ref[...] = jnp.zeros_like(acc_ref)
            rb_ref[...] = jnp.zeros_like(rb_ref)
            r_ref[...] = jnp.zeros_like(r_ref)
            qb = [q_s[pp, _blk(i), :] for pp in range(npair)]

            def tile(j, strict):
                kj = [k_s[pp, _blk(j), :] for pp in range(npair)]
                vj = [v_s[pp, _blk(j), :] for pp in range(npair)]
                r = [r_ref[c] for c in range(len(chains))]
                w, _, r_next = _sb_tile_weights([_only(heads[h], qb[pp]) for pp, h in chains],
                                                [kj[pp] for pp, _ in chains], strict, r, u_suf)
                pv = [_dot(w[c].astype(BF16), _only(heads[h], vj[pp])) for c, (pp, h) in enumerate(chains)]
                for pp in range(npair):
                    acc_ref[pp] += pv[2 * pp] + pv[2 * pp + 1]
                    rb_ref[pp] = jnp.where(lane == j, r[2 * pp], jnp.where(lane == RS_STRIDE + j, r[2 * pp + 1], rb_ref[pp]))
                for c in range(len(chains)):
                    r_ref[c] = r_next[c]

            tile(i, diag)

            def kblock(jj, _):
                tile(i - jj, None)
                return 0

            lax.fori_loop(1, i + 1, kblock, 0)
            for pp in range(npair):
                o_ref[_blk(i), _pair_cols(pp)] = acc_ref[pp].astype(BF16)
                rs_ref[_blk(i), _pair_cols(pp)] = rb_ref[pp]
            return 0

        lax.fori_loop(0, NBLK, qblock, 0)

    return pl.pallas_call(
        body, name=name,
        out_shape=(jax.ShapeDtypeStruct((t, W_ATT), BF16), jax.ShapeDtypeStruct((t, W_ATT), F32)),
        grid=(nseq, NH // (2 * npair)), in_specs=[_group_spec(0, npair)], out_specs=(_gheads_spec(npair), _gheads_spec(npair)),
        scratch_shapes=_qkv_scratch(npair) + [pltpu.VMEM((npair, BQ, LANES), F32), pltpu.VMEM((2 * npair, BQ, 1), F32),
                                      pltpu.VMEM((npair, BQ, LANES), F32)],
        compiler_params=_cparams("parallel", "parallel"))(proj)


def _sb_bwd(proj, do, rs, dproj, after, nseq, npair, name):
    chains = _chains(npair)

    def body(p_ref, do_ref, rs_ref, _, _after, dp_ref, q_s, k_s, v_s, dqa_ref, dka_ref, dva_ref, ep_ref):
        _load_qkv(p_ref, q_s, k_s, v_s)
        row, col = _tile_iotas()
        u_suf = (row > col).astype(BF16)
        u_pre = (row < col).astype(BF16)
        diag = col < row
        heads = _head_masks()
        dka_ref[...] = jnp.zeros_like(dka_ref)
        dva_ref[...] = jnp.zeros_like(dva_ref)
        nc = len(chains)

        def qblock(i, _):
            rb = [rs_ref[_blk(i), _pair_cols(pp)] for pp in range(npair)]
            qb = [q_s[pp, _blk(i), :] for pp in range(npair)]
            dob = [do_ref[_blk(i), _pair_cols(pp)] for pp in range(npair)]
            dqa_ref[...] = jnp.zeros_like(dqa_ref)
            ep_ref[...] = jnp.zeros_like(ep_ref)

            def tile(j, strict):
                kj = [k_s[pp, _blk(j), :] for pp in range(npair)]
                vj = [v_s[pp, _blk(j), :] for pp in range(npair)]
                q = [_only(heads[h], qb[pp]) for pp, h in chains]
                dov = [_only(heads[h], dob[pp]) for pp, h in chains]
                r = [_pick_lane(rb[pp], RS_STRIDE * h + j) for pp, h in chains]
                dw = [_dot_nt(dov[c], vj[pp]) for c, (pp, _) in enumerate(chains)]
                w, sp, _ = _sb_tile_weights(q, [kj[pp] for pp, _ in chains], strict, r, u_suf)
                e = [dw[c] * w[c] for c in range(nc)]
                e_pre = [ep_ref[c] + _dot(e[c].astype(BF16), u_pre) for c in range(nc)]
                dz = []
                for c in range(nc):
                    ep_ref[c] += jnp.sum(e[c], axis=1, keepdims=True)
                    sneg = jnp.exp(-sp[c])
                    dzc = e[c] * sneg - (1.0 - sneg) * e_pre[c]
                    if strict is not None:
                        dzc = jnp.where(strict, dzc, 0.0)
                    dz.append(dzc.astype(BF16))
                dq = [_dot(dz[c], _only(heads[h], kj[pp])) for c, (pp, h) in enumerate(chains)]
                dk = [_dot_tn(dz[c], q[c]) for c in range(nc)]
                dv = [_dot_tn(w[c].astype(BF16), dov[c]) for c in range(nc)]
                for pp in range(npair):
                    dqa_ref[pp] += dq[2 * pp] + dq[2 * pp + 1]
                    dka_ref[pp, _blk(j), :] += dk[2 * pp] + dk[2 * pp + 1]
                    dva_ref[pp, _blk(j), :] += dv[2 * pp] + dv[2 * pp + 1]

            def kblock(j, _):
                tile(j, None)
                return 0

            lax.fori_loop(0, i, kblock, 0)
            tile(i, diag)
            for pp in range(npair):
                dp_ref[_blk(i), pp * PAIR_W:pp * PAIR_W + LANES] = (dqa_ref[pp] * SCALE).astype(BF16)
            return 0

        lax.fori_loop(0, NBLK, qblock, 0)
        for pp in range(npair):
            dp_ref[:, pp * PAIR_W + LANES:pp * PAIR_W + 2 * LANES] = dka_ref[pp].astype(BF16)
            dp_ref[:, pp * PAIR_W + 2 * LANES:pp * PAIR_W + 3 * LANES] = dva_ref[pp].astype(BF16)

    return pl.pallas_call(
        body, name=name, out_shape=jax.ShapeDtypeStruct(dproj.shape, BF16), grid=(nseq, NH // (2 * npair)),
        in_specs=[_group_spec(0, npair), _gheads_spec(npair), _gheads_spec(npair), ANY, ANY], out_specs=_group_spec(0, npair),
        input_output_aliases={3: 0},
        scratch_shapes=_qkv_scratch(npair) + [pltpu.VMEM((npair, BQ, LANES), F32), pltpu.VMEM((npair, LP, LANES), F32),
                                      pltpu.VMEM((npair, LP, LANES), F32), pltpu.VMEM((2 * npair, BQ, 1), F32)],
        compiler_params=_cparams("parallel", "parallel"))(proj, do, rs, dproj, after)


CROW_SPEC = pl.BlockSpec((None, NH, LP), lambda s, g: (s, 0, 0))


def _fox_scores(qi, kj, cq, ck, causal):
    z = _dot_nt(qi, kj) + (cq - ck)
    return z if causal is None else jnp.where(causal, z, NEG)


def _key_cols(cr_ref, head, j):
    return cr_ref[pl.ds(head, 1), pl.ds(pl.multiple_of(j * BQ, BQ), BQ)]


def _fox_fwd(proj, c, crow, nseq, npair, name):
    t = nseq * LP
    chains = _chains(npair)

    def body(p_ref, c_ref, cr_ref, o_ref, o32_ref, lse_ref, q_s, k_s, v_s, acc_ref, m_ref, l_ref):
        _load_qkv(p_ref, q_s, k_s, v_s)
        row, col = _tile_iotas()
        diag = col <= row
        lane = _lane_iota()
        heads = _head_masks()
        head0 = 2 * npair * pl.program_id(1)
        nc = len(chains)

        def qblock(i, _):
            cblk = c_ref[_blk(i), :]
            qb = [q_s[pp, _blk(i), :] for pp in range(npair)]
            cq = [_pick_lane(cblk, head0 + c) for c in range(nc)]
            acc_ref[...] = jnp.zeros_like(acc_ref)
            m_ref[...] = jnp.full_like(m_ref, NEG)
            l_ref[...] = jnp.zeros_like(l_ref)

            def tile(j, causal):
                kj = [k_s[pp, _blk(j), :] for pp in range(npair)]
                vj = [v_s[pp, _blk(j), :] for pp in range(npair)]
                z = [_fox_scores(_only(heads[h], qb[pp]), kj[pp], cq[c], _key_cols(cr_ref, head0 + c, j), causal)
                     for c, (pp, h) in enumerate(chains)]
                p, alpha = [], []
                for c in range(nc):
                    m_old = m_ref[c]
                    m_new = jnp.maximum(m_old, jnp.max(z[c], axis=1, keepdims=True))
                    alpha.append(jnp.exp(m_old - m_new))
                    pc = jnp.exp(z[c] - m_new)
                    l_ref[c] = alpha[c] * l_ref[c] + jnp.sum(pc, axis=1, keepdims=True)
                    m_ref[c] = m_new
                    p.append(pc.astype(BF16))
                pv = [_dot(p[c], _only(heads[h], vj[pp])) for c, (pp, h) in enumerate(chains)]
                for c in range(nc):
                    acc_ref[c] = alpha[c] * acc_ref[c] + pv[c]

            def kblock(j, _):
                tile(j, None)
                return 0

            lax.fori_loop(0, i, kblock, 0)
            tile(i, diag)
            for pp in range(npair):
                out = acc_ref[2 * pp] / l_ref[2 * pp] + acc_ref[2 * pp + 1] / l_ref[2 * pp + 1]
                o_ref[_blk(i), _pair_cols(pp)] = out.astype(BF16)
                o32_ref[_blk(i), _pair_cols(pp)] = out
                lse = [m_ref[2 * pp + h] + jnp.log(l_ref[2 * pp + h]) for h in range(2)]
                lse_ref[_blk(i), _pair_cols(pp)] = jnp.where(lane == 0, lse[0], jnp.where(lane == 1, lse[1], 0.0))
            return 0

        lax.fori_loop(0, NBLK, qblock, 0)

    return pl.pallas_call(
        body, name=name,
        out_shape=(jax.ShapeDtypeStruct((t, W_ATT), BF16), jax.ShapeDtypeStruct((t, W_ATT), F32),
                   jax.ShapeDtypeStruct((t, W_ATT), F32)),
        grid=(nseq, NH // (2 * npair)), in_specs=[_group_spec(1, npair), SEQ_SPEC, CROW_SPEC], out_specs=(_gheads_spec(npair), _gheads_spec(npair), _gheads_spec(npair)),
        scratch_shapes=_qkv_scratch(npair) + [pltpu.VMEM((2 * npair, BQ, LANES), F32), pltpu.VMEM((2 * npair, BQ, 1), F32),
                                      pltpu.VMEM((2 * npair, BQ, 1), F32)],
        compiler_params=_cparams("parallel", "parallel"))(proj, c, crow)


def _fox_bwd(proj, c, crow, o32, lse, do, dproj, nseq, npair, name):
    t = nseq * LP
    chains = _chains(npair)

    def body(p_ref, c_ref, cr_ref, o_ref, lse_ref, do_ref, _, dp_ref, dck_ref, dcq_ref,
             q_s, k_s, v_s, dqa_ref, dka_ref, dva_ref, rsum_ref):
        _load_qkv(p_ref, q_s, k_s, v_s)
        row, col = _tile_iotas()
        diag = col <= row
        lane = _lane_iota()
        heads = _head_masks()
        sub = lax.broadcasted_iota(jnp.int32, (NH, BQ), 0)
        group = pl.program_id(1)
        head0 = 2 * npair * group
        nc = len(chains)
        dka_ref[...] = jnp.zeros_like(dka_ref)
        dva_ref[...] = jnp.zeros_like(dva_ref)

        @pl.when(group == 0)
        def _():
            dck_ref[...] = jnp.zeros_like(dck_ref)
            dcq_ref[...] = jnp.zeros_like(dcq_ref)

        def qblock(i, _):
            dqa_ref[...] = jnp.zeros_like(dqa_ref)
            rsum_ref[...] = jnp.zeros_like(rsum_ref)
            cblk = c_ref[_blk(i), :]
            qb = [q_s[pp, _blk(i), :] for pp in range(npair)]
            dob = [do_ref[_blk(i), _pair_cols(pp)] for pp in range(npair)]
            prod = [dob[pp].astype(F32) * o_ref[_blk(i), _pair_cols(pp)] for pp in range(npair)]
            cq = [_pick_lane(cblk, head0 + c) for c in range(nc)]
            lse_i = [_pick_lane(lse_ref[_blk(i), _pair_cols(pp)], h) for pp, h in chains]
            delta = [jnp.sum(_only(heads[h], prod[pp]), axis=1, keepdims=True) for pp, h in chains]

            def tile(j, causal):
                kj = [k_s[pp, _blk(j), :] for pp in range(npair)]
                vj = [v_s[pp, _blk(j), :] for pp in range(npair)]
                keys = pl.ds(pl.multiple_of(j * BQ, BQ), BQ)
                q = [_only(heads[h], qb[pp]) for pp, h in chains]
                dov = [_only(heads[h], dob[pp]) for pp, h in chains]
                z = [_fox_scores(q[c], kj[pp], cq[c], _key_cols(cr_ref, head0 + c, j), causal)
                     for c, (pp, _) in enumerate(chains)]
                dpv = [_dot_nt(dov[c], vj[pp]) for c, (pp, _) in enumerate(chains)]
                p = [jnp.exp(z[c] - lse_i[c]) for c in range(nc)]
                ds = [p[c] * (dpv[c] - delta[c]) for c in range(nc)]
                dsb = [d.astype(BF16) for d in ds]
                dq = [_dot(dsb[c], _only(heads[h], kj[pp])) for c, (pp, h) in enumerate(chains)]
                dk = [_dot_tn(dsb[c], q[c]) for c in range(nc)]
                dv = [_dot_tn(p[c].astype(BF16), dov[c]) for c in range(nc)]
                for pp in range(npair):
                    dqa_ref[pp] += dq[2 * pp] + dq[2 * pp + 1]
                    dka_ref[pp, _blk(j), :] += dk[2 * pp] + dk[2 * pp + 1]
                    dva_ref[pp, _blk(j), :] += dv[2 * pp] + dv[2 * pp + 1]
                col_sums = jnp.zeros((NH, BQ), F32)
                for c in range(nc):
                    col_sums = col_sums + jnp.where(sub == head0 + c, jnp.sum(ds[c], axis=0, keepdims=True), 0.0)
                    rsum_ref[c] += jnp.sum(ds[c], axis=1, keepdims=True)
                dck_ref[:, keys] = dck_ref[:, keys] - col_sums

            def kblock(j, _):
                tile(j, None)
                return 0

            lax.fori_loop(0, i, kblock, 0)
            tile(i, diag)
            row_sums = jnp.zeros((BQ, LANES), F32)
            for c in range(nc):
                row_sums = row_sums + jnp.where(lane == head0 + c, rsum_ref[c], 0.0)
            dcq_ref[_blk(i), :] += row_sums
            for pp in range(npair):
                dp_ref[_blk(i), pp * PAIR_W:pp * PAIR_W + LANES] = (dqa_ref[pp] * SCALE).astype(BF16)
            return 0

        lax.fori_loop(0, NBLK, qblock, 0)
        for pp in range(npair):
            dp_ref[:, pp * PAIR_W + LANES:pp * PAIR_W + 2 * LANES] = dka_ref[pp].astype(BF16)
            dp_ref[:, pp * PAIR_W + 2 * LANES:pp * PAIR_W + 3 * LANES] = dva_ref[pp].astype(BF16)

    return pl.pallas_call(
        body, name=name,
        out_shape=(jax.ShapeDtypeStruct(dproj.shape, BF16), jax.ShapeDtypeStruct((nseq, NH, LP), F32),
                   jax.ShapeDtypeStruct((t, LANES), F32)),
        grid=(nseq, NH // (2 * npair)),
        in_specs=[_group_spec(1, npair), SEQ_SPEC, CROW_SPEC, _gheads_spec(npair), _gheads_spec(npair), _gheads_spec(npair), ANY],
        out_specs=(_group_spec(1, npair), CROW_SPEC, SEQ_SPEC),
        input_output_aliases={6: 0},
        scratch_shapes=_qkv_scratch(npair) + [pltpu.VMEM((npair, BQ, LANES), F32), pltpu.VMEM((npair, LP, LANES), F32),
                                      pltpu.VMEM((npair, LP, LANES), F32), pltpu.VMEM((2 * npair, BQ, 1), F32)],
        compiler_params=_cparams("parallel", "arbitrary"))(proj, c, crow, o32, lse, do, dproj)


def _adamw_math(w, g, m, v):
    m = B1 * m + (1.0 - B1) * g
    v = B2 * v + (1.0 - B2) * (g * g)
    m_hat = m / (1.0 - B1 ** STEP)
    v_hat = v / (1.0 - B2 ** STEP)
    delta = -LR * (m_hat / (jnp.sqrt(v_hat) + EPS) + WD * w)
    return delta, m, v


def _sum_adamw(parts, w, m, v, tr, name):
    rows, cols = w.shape
    cp = parts.shape[2]
    assert rows % tr == 0 and parts.shape[1] == rows

    def body(p_ref, w_ref, m_ref, v_ref, g_ref, d_ref, nm_ref, nv_ref):
        gsum = p_ref[0].astype(F32)
        for s in range(1, N_DEV):
            gsum = gsum + p_ref[s].astype(F32)
        gsum = gsum[:, :cols]
        d, nm, nv = _adamw_math(w_ref[...], gsum, m_ref[...], v_ref[...])
        g_ref[...] = gsum
        d_ref[...] = d
        nm_ref[...] = nm
        nv_ref[...] = nv

    blk = pl.BlockSpec((tr, cols), lambda i: (i, 0))
    out = jax.ShapeDtypeStruct((rows, cols), F32)
    return pl.pallas_call(
        body, name=name, out_shape=(out, out, out, out), grid=(rows // tr,),
        in_specs=[pl.BlockSpec((N_DEV, tr, cp), lambda i: (0, i, 0)), blk, blk, blk],
        out_specs=(blk, blk, blk, blk), compiler_params=_cparams("parallel"))(parts, w, m, v)


def _local_step(x, tgt, meta, tgt_front, g_mix, b_forget, g_ffn, g_final, first_weights, late_weights, early_grads, last_grad):
    nseq = x.shape[0]
    t = nseq * LP
    tm = LP // 2
    mm = functools.partial(_matmul, tm=tm)

    h0 = _pad_rows(meta, x, nseq, "pad_x").reshape(t, D)
    tgt_p = _pad_rows(tgt_front, tgt, nseq, "pad_target").reshape(t, D)
    bf = jnp.pad(b_forget.reshape(1, NH), ((0, 0), (0, LANES - NH)))

    n1 = _norm_fwd(h0, g_mix, "norm1")
    w_in_p, started = first_weights(n1)
    proj = mm(n1, w_in_p, out_dtype=F32, tn=1792, tk=D, after=started, name="in_proj")
    c = _gate_fwd(proj, bf, nseq, "gate_fwd")
    crow = c[:, :NH].reshape(nseq, LP, NH).transpose(0, 2, 1)
    o_sb, rs = _sb_fwd(proj, nseq, 2, "sb_fwd")
    o_fx, o_fx32, lse = _fox_fwd(proj, c, crow, nseq, 1, "fox_fwd")
    w_bsb, w_bfx, w_out, w_up_i, cw_i, w_down = late_weights(o_fx)
    p_sb = mm(o_sb, w_bsb, out_dtype=F32, tn=D, tk=W_ATT, name="branch_sb")
    p_fx = mm(o_fx, w_bfx, out_dtype=F32, tn=D, tk=W_ATT, name="branch_fox")
    merged = _merge_fwd(p_sb, p_fx, proj, "merge_fwd")
    rows = functools.partial(_matmul_rows, tm=LP // 4)
    h1, n2 = rows(merged, w_out, [h0], [g_ffn], _residual_norm, [F32, BF16], [], tk=D, name="out_proj_norm2")
    u = mm(n2, w_up_i, out_dtype=F32, tn=1408, tk=D, name="up_proj")
    act = _conv_glu_fwd(u, cw_i, nseq, "conv_glu_fwd")

    dh2, dh2b, loss, dg_final = rows(act, w_down, [h1, tgt_p], [g_final], _loss_head, [F32, BF16],
                                     [(8, LANES), (1, D)], tk=1408, name="down_proj_loss")
    d_down = _matmul(act, dh2b, out_dtype=BF16, tm=1408, tn=D, tk=tm, ta=True, name="d_w_down")
    dact = mm(dh2b, w_down, out_dtype=F32, tn=1408, tk=D, tb=True, name="d_act")
    du, d_cw = _conv_glu_bwd(u, cw_i, dact, nseq, "conv_glu_bwd")
    d_up = _matmul(n2, du, out_dtype=BF16, tm=D, tn=1408, tk=tm, ta=True, name="d_w_up")
    dh1, dh1b, dg_ffn = rows(du, w_up_i, [h1, dh2], [g_ffn], _residual_norm_bwd, [F32, BF16], [(1, D)],
                             tk=1408, tb=True, name="d_n2_norm2_bwd")
    d_out = _matmul(merged, dh1b, out_dtype=BF16, tm=D, tn=D, tk=tm, ta=True, name="d_w_out")
    dmerged = mm(dh1b, w_out, out_dtype=F32, tn=D, tk=D, tb=True, name="d_merged")
    dp_sb, dproj = _merge_bwd(dmerged, p_sb, proj, None, 0, "merge_bwd_sb")
    dp_fx, dproj = _merge_bwd(dmerged, p_fx, proj, dproj, 1, "merge_bwd_fox")
    d_bsb = _matmul(o_sb, dp_sb, out_dtype=BF16, tm=W_ATT, tn=D, tk=tm, ta=True, name="d_w_branch_sb")
    d_bfx = _matmul(o_fx, dp_fx, out_dtype=BF16, tm=W_ATT, tn=D, tk=tm, ta=True, name="d_w_branch_fox")
    do_sb = mm(dp_sb, w_bsb, out_dtype=BF16, tn=W_ATT, tk=D, tb=True, name="d_o_sb")
    do_fx = mm(dp_fx, w_bfx, out_dtype=BF16, tn=W_ATT, tk=D, tb=True, name="d_o_fox")
    sent = early_grads(dict(w_branch_sb=d_bsb, w_branch_fox=d_bfx, w_out=d_out, w_up=d_up, conv_w=d_cw, w_down=d_down))
    dproj = _sb_bwd(proj, do_sb, rs, dproj, sent, nseq, 2, "sb_bwd")
    dproj, dck, dcq = _fox_bwd(proj, c, crow, o_fx32, lse, do_fx, dproj, nseq, 2, "fox_bwd")
    dc = dcq + jnp.pad(dck.transpose(0, 2, 1).reshape(t, NH), ((0, 0), (0, LANES - NH)))
    dproj, d_bf = _gate_bwd(proj, bf, dc, dproj, nseq, "gate_bwd")
    d_in = _matmul(n1, dproj, out_dtype=BF16, tm=D, tn=1792, tk=tm, ta=True, name="d_w_in")
    dh0, dg_mix = rows(dproj, w_in_p, [h0, dh1], [g_mix], _residual_norm_bwd_f32, [F32], [(1, D)],
                       tk=1792, tb=True, after=last_grad(d_in), name="d_n1_norm1_bwd")
    dh0 = dh0.reshape(nseq, LP, D)
    grads = dict(meta_tokens=jnp.sum(dh0[:, :N_META], axis=0), norm_mix_g=dg_mix, b_forget=d_bf[:, :NH],
                 norm_ffn_g=dg_ffn, norm_final_g=dg_final)
    return loss[0, 0], _real_rows(dh0, nseq, "grad_x"), grads


REPL = (("norm_mix_g", D), ("norm_ffn_g", D), ("norm_final_g", D), ("b_forget", LANES))
REPL_ROWS = 32
META_ROWS = N_META * D // LANES


def _pack_repl(tree):
    rows = [jnp.pad(tree[name].reshape(-1), (0, n - tree[name].size)).reshape(-1, LANES) for name, n in REPL]
    packed = jnp.concatenate(rows, axis=0)
    return jnp.pad(packed, ((0, REPL_ROWS - packed.shape[0]), (0, 0)))


def _unpack_repl(packed, shapes):
    out, r = {}, 0
    for name, n in REPL:
        size = 1
        for s in shapes[name]:
            size *= s
        out[name] = packed[r:r + n // LANES].reshape(-1)[:size].reshape(shapes[name])
        r += n // LANES
    return out


def kernel(x, meta_tokens, norm_mix_g, w_in, b_forget, w_branch_sb, w_branch_fox, w_out, norm_ffn_g, w_up, conv_w, w_down, norm_final_g, loss_target, m_meta_tokens, m_norm_mix_g, m_w_in, m_b_forget, m_w_branch_sb, m_w_branch_fox, m_w_out, m_norm_ffn_g, m_w_up, m_conv_w, m_w_down, m_norm_final_g, v_meta_tokens, v_norm_mix_g, v_w_in, v_b_forget, v_w_branch_sb, v_w_branch_fox, v_w_out, v_norm_ffn_g, v_w_up, v_conv_w, v_w_down, v_norm_final_g):
    w = dict(meta_tokens=meta_tokens, norm_mix_g=norm_mix_g, w_in=w_in, b_forget=b_forget, w_branch_sb=w_branch_sb,
             w_branch_fox=w_branch_fox, w_out=w_out, norm_ffn_g=norm_ffn_g, w_up=w_up, conv_w=conv_w, w_down=w_down,
             norm_final_g=norm_final_g)
    m = dict(meta_tokens=m_meta_tokens, norm_mix_g=m_norm_mix_g, w_in=m_w_in, b_forget=m_b_forget,
             w_branch_sb=m_w_branch_sb, w_branch_fox=m_w_branch_fox, w_out=m_w_out, norm_ffn_g=m_norm_ffn_g,
             w_up=m_w_up, conv_w=m_conv_w, w_down=m_w_down, norm_final_g=m_norm_final_g)
    v = dict(meta_tokens=v_meta_tokens, norm_mix_g=v_norm_mix_g, w_in=v_w_in, b_forget=v_b_forget,
             w_branch_sb=v_w_branch_sb, w_branch_fox=v_w_branch_fox, w_out=v_w_out, norm_ffn_g=v_norm_ffn_g,
             w_up=v_w_up, conv_w=v_conv_w, w_down=v_w_down, norm_final_g=v_norm_final_g)
    shapes = {k: a.shape for k, a in w.items()}
    sharded = ("w_in", "w_branch_sb", "w_branch_fox", "w_out", "w_up", "w_down", "conv_w", "meta_tokens")
    mat = lambda tree, name: tree[name].reshape(tree[name].shape[-2:])

    def lane_pad(a, width):
        return jnp.pad(a, ((0, 0), (0, width - a.shape[1])))

    late = ("w_branch_sb", "w_branch_fox", "w_out", "w_up", "w_down", "conv_w")
    pending_w = {}
    g_meta, = _all_gather([mat(w, "meta_tokens")], "gather_meta")
    pending_w["in"], in_started = _remote_start(
        [lane_pad(mat(w, "w_in").astype(BF16), SHARD_P)], False, g_meta, "gather_w_in_start")
    meta_full = g_meta.transpose(1, 0, 2).reshape(N_META, D) + in_started[0, 0]

    def first_weights(after):
        g_in, = _remote_wait(pending_w["in"], after, "gather_w_in_wait")
        pending_w["late"], started = _remote_start(
            [mat(w, "w_branch_sb").astype(BF16), mat(w, "w_branch_fox").astype(BF16), mat(w, "w_out").astype(BF16),
             lane_pad(mat(w, "w_up").astype(BF16), SHARD_P), mat(w, "w_down").astype(BF16), mat(w, "conv_w")],
            False, g_in, "gather_late_start")
        w_in_p = _relayout(g_in, 1, IN_P, _gathered_to_full(IN_SHARD, _in_padded_to_orig), BF16, 256, "w_in_cols")[0]
        return w_in_p, started

    def late_weights(after):
        g_bsb, g_bfx, g_out, g_up, g_down, g_cw = _remote_wait(pending_w["late"], after, "gather_late_wait")
        w_up_i = _relayout(g_up, 1, 2 * D_FF, _gathered_to_full(UP_SHARD, _up_inter_to_orig), BF16, 256, "w_up_cols")[0]
        w_bsb = _relayout(g_bsb, 1, D, _gathered_to_full(ATT_SHARD, lambda d: d), BF16, 256, "w_bsb_cols")[0]
        w_bfx = _relayout(g_bfx, 1, D, _gathered_to_full(ATT_SHARD, lambda d: d), BF16, 256, "w_bfx_cols")[0]
        cw_full = g_cw.transpose(1, 0, 2).reshape(3, 2 * D_FF)
        cw_i = cw_full.reshape(3, 2, D_FF // FFC, FFC).transpose(0, 2, 1, 3).reshape(3, 2 * D_FF)
        return w_bsb, w_bfx, g_out.reshape(D, D), w_up_i, cw_i, g_down.reshape(D_FF, D)

    pending_g = {}

    def early_grads(g):
        d_cw = g["conv_w"].reshape(3, D_FF // FFC, 2, FFC).transpose(0, 2, 1, 3).reshape(3, 2 * D_FF)
        pending_g["early"], sent = _remote_start(
            [_relayout(g["w_branch_sb"][None], N_DEV, ATT_SHARD, _full_to_shards(ATT_SHARD, lambda c: c), BF16, 256, "d_w_bsb_shards"),
             _relayout(g["w_branch_fox"][None], N_DEV, ATT_SHARD, _full_to_shards(ATT_SHARD, lambda c: c), BF16, 256, "d_w_bfx_shards"),
             g["w_out"].reshape(N_DEV, D // N_DEV, D),
             _relayout(g["w_up"][None], N_DEV, SHARD_P, _full_to_shards(UP_SHARD, _UP_ORIG_TO_INTER.get), BF16, 256, "d_w_up_shards"),
             g["w_down"].reshape(N_DEV, D_FF // N_DEV, D),
             d_cw.reshape(3, N_DEV, UP_SHARD).transpose(1, 0, 2)], True, g["w_out"], "exchange_early_start")
        return sent

    def last_grad(d_in):
        shards = _relayout(d_in[None], N_DEV, SHARD_P, _full_to_shards(IN_SHARD, _IN_ORIG_TO_PADDED.get), BF16, 256, "d_w_in_shards")
        pending_g["last"], sent = _remote_start([shards], True, shards, "exchange_last_start")
        return sent

    loss, grad_x, grads = _local_step(
        x, loss_target, meta_full, jnp.zeros((N_META, D), F32) + in_started[0, 0], norm_mix_g.reshape(1, D), b_forget,
        norm_ffn_g.reshape(1, D), norm_final_g.reshape(1, D), first_weights, late_weights, early_grads, last_grad)

    small = jnp.concatenate([_pack_repl(grads), grads["meta_tokens"].reshape(META_ROWS, LANES)], axis=0)
    small, = _all_gather([small], "gather_small_grads")
    me_idx = 4 * lax.axis_index("x") + 2 * lax.axis_index("y") + lax.axis_index("c")
    p_meta = lax.dynamic_slice_in_dim(small[:, REPL_ROWS:].reshape(N_DEV, N_META, D), me_idx * ATT_SHARD, ATT_SHARD, axis=2)

    p_in, = _remote_wait(pending_g["last"], small, "exchange_last_wait")
    parts = dict(zip(late, _remote_wait(pending_g["early"], p_in, "exchange_early_wait")), w_in=p_in, meta_tokens=p_meta)
    tiles = dict(w_in=256, w_branch_sb=256, w_branch_fox=256, w_out=D // N_DEV, w_up=256, w_down=D_FF // N_DEV,
                 conv_w=3, meta_tokens=N_META)
    new = {name: _sum_adamw(parts[name], mat(w, name), mat(m, name), mat(v, name), tiles[name], "adamw_" + name)
           for name in sharded}

    routs = _sum_adamw(small[:, :REPL_ROWS], _pack_repl(w), _pack_repl(m), _pack_repl(v), REPL_ROWS, "adamw_replicated")
    repl = [_unpack_repl(o, shapes) for o in routs]

    result = [lax.psum(loss, ("x", "y", "c")), grad_x]
    for k in range(4):
        for name in w:
            result.append(new[name][k].reshape(shapes[name]) if name in new else repl[k][name])
    return tuple(result)
```

```python
import functools

import jax
import jax.numpy as jnp
from jax import lax
from jax.experimental import pallas as pl
from jax.experimental.pallas import tpu as pltpu

F32 = jnp.float32
BF16 = jnp.bfloat16

N_DEV = 8
LANES = 128
D = 1024
N_META = 16
SEQ = 2048
L_REAL = N_META + SEQ
LP = 2304
BQ = 256
NBLK = LP // BQ
HEAD = 64
NH = 8
W_ATT = NH * HEAD
PAIR_W = 3 * LANES
D_FF = 2816
IN_COLS = 5128
QKV = 6 * W_ATT
IN_P = 5376
GATE_COL = QKV
F_COL = QKV + 2 * D
FFC = 256
RMS_EPS = 1e-6
LR, B1, B2, EPS, WD, STEP = 0.001, 0.9, 0.999, 1e-08, 0.01, 10
VMEM_LIMIT = 56 * 1024 * 1024

MESH = pl.DeviceIdType.MESH
ANY = pl.BlockSpec(memory_space=pl.ANY)


def _cparams(*sem):
    return pltpu.CompilerParams(dimension_semantics=sem if sem else None, vmem_limit_bytes=VMEM_LIMIT)


def _all_gather(xs, name):
    n = len(xs)

    def body(*refs):
        x_refs, out_refs = refs[:n], refs[n:2 * n]
        send_sems, recv_sems, local_sems = refs[2 * n:]
        mx, my, mc = lax.axis_index("x"), lax.axis_index("y"), lax.axis_index("c")
        me, sibling = (mx, my, mc), (mx, my, 1 - mc)
        chips = [(1 - mx, my), (mx, 1 - my), (1 - mx, 1 - my)]

        def copy(a, k, block, to, own=False):
            px, py, pc = block
            slot = out_refs[a].at[4 * px + 2 * py + pc]
            return pltpu.make_async_remote_copy(
                src_ref=x_refs[a] if own else slot, dst_ref=slot,
                send_sem=send_sems.at[7 * a + k], recv_sem=recv_sems.at[7 * a + k],
                device_id=to, device_id_type=MESH)

        mine = [pltpu.make_async_copy(x_refs[a], out_refs[a].at[4 * mx + 2 * my + mc], local_sems.at[a]) for a in range(n)]
        for cp in mine:
            cp.start()
        first = []
        for a in range(n):
            first.append(copy(a, 0, me, sibling, own=True))
            first += [copy(a, 1 + j, me, (*chip, mc), own=True) for j, chip in enumerate(chips)]
        for cp in first:
            cp.start()
        passed = []
        for j, chip in enumerate(chips):
            for a in range(n):
                copy(a, 1 + j, (*chip, mc), me).wait_recv()
                fwd = copy(a, 4 + j, (*chip, mc), sibling)
                fwd.start()
                passed.append(fwd)
        for a in range(n):
            copy(a, 0, sibling, me).wait_recv()
            for j, chip in enumerate(chips):
                copy(a, 4 + j, (*chip, 1 - mc), me).wait_recv()
        for cp in first + passed:
            cp.wait_send()
        for cp in mine:
            cp.wait()

    return pl.pallas_call(
        body, name=name,
        out_shape=tuple(jax.ShapeDtypeStruct((N_DEV,) + x.shape, x.dtype) for x in xs),
        in_specs=[ANY] * n, out_specs=tuple([ANY] * n),
        scratch_shapes=[pltpu.SemaphoreType.DMA((7 * n,)), pltpu.SemaphoreType.DMA((7 * n,)),
                        pltpu.SemaphoreType.DMA((n,))],
    )(*xs)


HBM = pl.BlockSpec(memory_space=pltpu.HBM)
SEM = pl.BlockSpec(memory_space=pltpu.SEMAPHORE)
EFFECT = pltpu.SideEffectType.DATAFLOW_SIDE_EFFECTING


def _peer_copies(src_refs, land_refs, send_sems, recv_sems, per_peer):
    mx, my, mc = lax.axis_index("x"), lax.axis_index("y"), lax.axis_index("c")
    me_idx = 4 * mx + 2 * my + mc
    copies = []
    for k in range(1, N_DEV):
        px, py, pc = mx ^ (k >> 2), my ^ ((k >> 1) & 1), mc ^ (k & 1)
        for a, (src, land) in enumerate(zip(src_refs, land_refs)):
            copies.append(pltpu.make_async_remote_copy(
                src_ref=src.at[4 * px + 2 * py + pc] if per_peer else src, dst_ref=land.at[me_idx],
                send_sem=send_sems.at[7 * a + k - 1], recv_sem=recv_sems.at[7 * a + k - 1],
                device_id=(px, py, pc), device_id_type=MESH))
    return me_idx, copies


def _remote_start(srcs, per_peer, after, name):
    n = len(srcs)
    lands = [lax.empty(s.shape if per_peer else (N_DEV,) + s.shape, s.dtype) for s in srcs]

    def body(*refs):
        src_refs, land_refs = refs[:n], refs[n:2 * n]
        send_sems, recv_sems = refs[2 * n + 1:2 * n + 3]
        token = refs[4 * n + 3]
        stage, local_sems = refs[4 * n + 4:5 * n + 4], refs[5 * n + 4]
        me_idx, copies = _peer_copies(src_refs, land_refs, send_sems, recv_sems, per_peer)
        for cp in copies:
            cp.start()
        own = [src_refs[a].at[me_idx] if per_peer else src_refs[a] for a in range(n)]
        for hop in ([(own[a], stage[a]) for a in range(n)], [(stage[a], land_refs[a].at[me_idx]) for a in range(n)]):
            cps = [pltpu.make_async_copy(s, d, local_sems.at[a]) for a, (s, d) in enumerate(hop)]
            for cp in cps:
                cp.start()
            for cp in cps:
                cp.wait()
        token[...] = jnp.zeros_like(token)

    thru = [pltpu.HBM(a.shape, a.dtype) for a in list(srcs) + lands]
    out = pl.pallas_call(
        body, name=name,
        out_shape=(pltpu.SemaphoreType.DMA((7 * n,)), pltpu.SemaphoreType.DMA((7 * n,)), *thru,
                   jax.ShapeDtypeStruct((8, LANES), F32)),
        in_specs=[HBM] * (2 * n) + [ANY],
        out_specs=(SEM, SEM, *([HBM] * (2 * n)), pl.BlockSpec(memory_space=pltpu.VMEM)),
        input_output_aliases={i: 2 + i for i in range(2 * n)},
        scratch_shapes=[pltpu.VMEM(s.shape[1:] if per_peer else s.shape, s.dtype) for s in srcs]
        + [pltpu.SemaphoreType.DMA((n,))],
        compiler_params=pltpu.CompilerParams(has_side_effects=EFFECT),
    )(*[pltpu.with_memory_space_constraint(a, pltpu.HBM) for a in list(srcs) + lands], after)
    return dict(sems=out[:2], bufs=out[2:2 * n + 2], per_peer=per_peer), out[-1]


def _remote_wait(pending, after, name):
    bufs = pending["bufs"]
    n = len(bufs) // 2
    per_peer = pending["per_peer"]

    def body(*refs):
        src_refs, land_refs = refs[:n], refs[n:2 * n]
        send_sems, recv_sems = refs[2 * n:2 * n + 2]
        _, copies = _peer_copies(src_refs, land_refs, send_sems, recv_sems, per_peer)
        for cp in copies:
            cp.wait_send()
        for cp in copies:
            cp.wait_recv()

    out = pl.pallas_call(
        body, name=name, out_shape=tuple(pltpu.HBM(a.shape, a.dtype) for a in bufs),
        in_specs=[HBM] * (2 * n) + [SEM, SEM, ANY], out_specs=tuple([HBM] * (2 * n)),
        input_output_aliases={i: i for i in range(2 * n)},
        compiler_params=pltpu.CompilerParams(has_side_effects=EFFECT),
    )(*bufs, *pending["sems"], after)
    return out[n:]


ROWS_PER_COPY = 256


def _pad_rows(front, body_rows, nseq, name):
    tail = LP - L_REAL
    nblk = SEQ // ROWS_PER_COPY

    def body(f_ref, b_ref, o_ref, z_ref, sems):
        s, i = pl.program_id(0), pl.program_id(1)
        rows = pltpu.make_async_copy(b_ref, o_ref.at[pl.ds(s, 1), pl.ds(N_META + i * ROWS_PER_COPY, ROWS_PER_COPY)], sems.at[0])
        rows.start()

        @pl.when(i == 0)
        def _():
            z_ref[...] = jnp.zeros_like(z_ref)
            head = pltpu.make_async_copy(f_ref, o_ref.at[s, pl.ds(0, N_META)], sems.at[1])
            zeros = pltpu.make_async_copy(z_ref, o_ref.at[s, pl.ds(L_REAL, tail)], sems.at[2])
            head.start()
            zeros.start()
            head.wait()
            zeros.wait()

        rows.wait()

    return pl.pallas_call(
        body, name=name, out_shape=jax.ShapeDtypeStruct((nseq, LP, D), F32), grid=(nseq, nblk),
        in_specs=[pl.BlockSpec((N_META, D), lambda s, i: (0, 0)), pl.BlockSpec((1, ROWS_PER_COPY, D), lambda s, i: (s, i, 0))],
        out_specs=ANY,
        scratch_shapes=[pltpu.VMEM((tail, D), F32), pltpu.SemaphoreType.DMA((3,))],
        compiler_params=_cparams("arbitrary", "arbitrary"))(front, body_rows)


def _real_rows(h, nseq, name):
    nblk = SEQ // ROWS_PER_COPY

    def body(h_ref, o_ref, sem):
        s, i = pl.program_id(0), pl.program_id(1)
        rows = pltpu.make_async_copy(h_ref.at[pl.ds(s, 1), pl.ds(N_META + i * ROWS_PER_COPY, ROWS_PER_COPY)], o_ref, sem)
        rows.start()
        rows.wait()

    return pl.pallas_call(
        body, name=name, out_shape=jax.ShapeDtypeStruct((nseq, SEQ, D), F32), grid=(nseq, nblk),
        in_specs=[ANY], out_specs=pl.BlockSpec((1, ROWS_PER_COPY, D), lambda s, i: (s, i, 0)),
        scratch_shapes=[pltpu.SemaphoreType.DMA],
        compiler_params=_cparams("arbitrary", "arbitrary"))(h)


def _plan_cols(n_q, n_dcols, src_of):
    plan = {}
    for q in range(n_q):
        for dblk in range(n_dcols // LANES):
            segs, key, start = [], None, 0
            for lane in range(LANES + 1):
                new = None
                if lane < LANES:
                    src = src_of(q, dblk * LANES + lane)
                    if src is not None:
                        new = (src[0], src[1] // LANES, (lane - src[1] % LANES) % LANES)
                if new != key:
                    if key is not None:
                        segs.append((*key, start, lane))
                    key, start = new, lane
            plan[(q, dblk)] = segs
    return plan


def _relayout(src, n_q, n_dcols, src_of, out_dtype, tr, name):
    n_p, rows, scols = src.shape
    plan = _plan_cols(n_q, n_dcols, src_of)

    def body(s_ref, d_ref):
        lane = lax.broadcasted_iota(jnp.int32, (tr, LANES), 1)
        for (q, dblk), segs in plan.items():
            acc = jnp.zeros((tr, LANES), F32)
            for p, sblk, rot, lo, hi in segs:
                x = s_ref[p, :, sblk * LANES:(sblk + 1) * LANES].astype(F32)
                if rot:
                    x = pltpu.roll(x, rot, 1)
                acc = x if (lo, hi) == (0, LANES) else jnp.where((lane >= lo) & (lane < hi), x, acc)
            d_ref[q, :, dblk * LANES:(dblk + 1) * LANES] = acc.astype(out_dtype)

    return pl.pallas_call(
        body, name=name, out_shape=jax.ShapeDtypeStruct((n_q, rows, n_dcols), out_dtype), grid=(rows // tr,),
        in_specs=[pl.BlockSpec((n_p, tr, scols), lambda i: (0, i, 0))],
        out_specs=pl.BlockSpec((n_q, tr, n_dcols), lambda i: (0, i, 0)),
        compiler_params=_cparams("parallel"))(src)


def _in_padded_to_orig(d):
    if d < QKV:
        kind, r = divmod(d, 4 * PAIR_W)
        pair, r = divmod(r, PAIR_W)
        part, r = divmod(r, LANES)
        return kind * 3 * W_ATT + part * W_ATT + pair * LANES + r
    if d < F_COL:
        return d + NH
    if d < F_COL + NH:
        return d - 2 * D
    return None


_IN_ORIG_TO_PADDED = {_in_padded_to_orig(d): d for d in range(IN_P) if _in_padded_to_orig(d) is not None}


def _up_inter_to_orig(d):
    j, r = divmod(d, 2 * FFC)
    part, r = divmod(r, FFC)
    return part * D_FF + j * FFC + r


_UP_ORIG_TO_INTER = {_up_inter_to_orig(d): d for d in range(2 * D_FF)}
IN_SHARD = IN_COLS // N_DEV
UP_SHARD = 2 * D_FF // N_DEV
SHARD_P = 768
ATT_SHARD = D // N_DEV


def _gathered_to_full(n_shard, to_orig):
    def src_of(q, d):
        c = to_orig(d)
        return None if c is None else (c // n_shard, c % n_shard)
    return src_of


def _full_to_shards(n_shard, from_orig):
    def src_of(q, d):
        return (0, from_orig(q * n_shard + d)) if d < n_shard else None
    return src_of


def _matmul(a, b, *, out_dtype, tm, tn, tk, ta=False, tb=False, after=None, name):
    if ta:
        kdim, m = a.shape
    else:
        m, kdim = a.shape
    n = b.shape[0] if tb else b.shape[1]
    assert m % tm == 0 and n % tn == 0 and kdim % tk == 0, (name, a.shape, b.shape, tm, tn, tk)
    nk = kdim // tk

    def body(a_ref, b_ref, *rest):
        o_ref, scratch = rest[len(extra)], rest[len(extra) + 1:]
        av, bv = a_ref[...], b_ref[...]
        if ta:
            p = lax.dot_general(av, bv, (((0,), (0,)), ((), ())), preferred_element_type=F32)
        elif tb:
            p = lax.dot_general(av, bv, (((1,), (1,)), ((), ())), preferred_element_type=F32)
        else:
            p = jnp.dot(av, bv, preferred_element_type=F32)
        if nk == 1:
            o_ref[...] = p.astype(o_ref.dtype)
        else:
            acc_ref, = scratch
            k = pl.program_id(2)

            @pl.when(k == 0)
            def _():
                acc_ref[...] = p

            @pl.when(k > 0)
            def _():
                acc_ref[...] += p

            @pl.when(k == nk - 1)
            def _():
                o_ref[...] = acc_ref[...].astype(o_ref.dtype)

    extra = [] if after is None else [after]
    a_spec = pl.BlockSpec((tk, tm), lambda i, j, k: (k, i)) if ta else pl.BlockSpec((tm, tk), lambda i, j, k: (i, k))
    b_spec = pl.BlockSpec((tn, tk), lambda i, j, k: (j, k)) if tb else pl.BlockSpec((tk, tn), lambda i, j, k: (k, j))
    return pl.pallas_call(
        body, name=name,
        out_shape=jax.ShapeDtypeStruct((m, n), out_dtype),
        grid=(m // tm, n // tn, nk),
        in_specs=[a_spec, b_spec] + [ANY] * len(extra),
        out_specs=pl.BlockSpec((tm, tn), lambda i, j, k: (i, j)),
        scratch_shapes=[] if nk == 1 else [pltpu.VMEM((tm, tn), F32)],
        compiler_params=_cparams("parallel", "parallel", "arbitrary"),
    )(a, b, *extra)


TR = 288


def _rms(h):
    return lax.rsqrt(jnp.mean(h * h, axis=-1, keepdims=True) + RMS_EPS)


def _norm_fwd(h, g, name):
    t = h.shape[0]
    row = pl.BlockSpec((TR, D), lambda i: (i, 0))

    def body(h_ref, g_ref, n_ref):
        hv = h_ref[...]
        n_ref[...] = ((hv * _rms(hv)) * g_ref[...]).astype(BF16)

    return pl.pallas_call(
        body, name=name, out_shape=jax.ShapeDtypeStruct((t, D), BF16), grid=(t // TR,),
        in_specs=[row, pl.BlockSpec((1, D), lambda i: (0, 0))], out_specs=row, compiler_params=_cparams("parallel"))(h, g)


EPI_ROWS = 144


def _matmul_rows(a, b, rows_in, vecs_in, epilogue, row_outs, sum_outs, *, tm, tk, tb=False, after=None, name):
    m, kdim = a.shape
    assert (b.shape[0] if tb else b.shape[1]) == D and m % tm == 0 and kdim % tk == 0 and tm % EPI_ROWS == 0
    nk = kdim // tk
    n_r, n_v, n_ro, n_so = len(rows_in), len(vecs_in), len(row_outs), len(sum_outs)
    extra = [] if after is None else [after]

    def body(a_ref, b_ref, *rest):
        r_refs, v_refs = rest[:n_r], rest[n_r:n_r + n_v]
        outs = rest[n_r + n_v + len(extra):]
        ro_refs, so_refs, acc_ref = outs[:n_ro], outs[n_ro:n_ro + n_so], outs[n_ro + n_so]
        i, k = pl.program_id(0), pl.program_id(1)
        if tb:
            p = lax.dot_general(a_ref[...], b_ref[...], (((1,), (1,)), ((), ())), preferred_element_type=F32)
        else:
            p = jnp.dot(a_ref[...], b_ref[...], preferred_element_type=F32)

        @pl.when(k == 0)
        def _():
            acc_ref[...] = p

        @pl.when(k > 0)
        def _():
            acc_ref[...] += p

        @pl.when(k == nk - 1)
        def _():
            vecs = [v[...] for v in v_refs]

            def step(c, sums):
                rows = pl.ds(pl.multiple_of(c * EPI_ROWS, 8), EPI_ROWS)
                tiles, terms = epilogue(i * tm + c * EPI_ROWS, acc_ref[rows, :], *[r[rows, :] for r in r_refs], *vecs)
                for o, tile in zip(ro_refs, tiles):
                    o[rows, :] = tile.astype(o.dtype)
                return tuple(s + term for s, term in zip(sums, terms))

            sums = lax.fori_loop(0, tm // EPI_ROWS, step, tuple(jnp.zeros(s, F32) for s in sum_outs))

            @pl.when(i == 0)
            def _():
                for o in so_refs:
                    o[...] = jnp.zeros_like(o)

            for o, s in zip(so_refs, sums):
                o[...] += s

    row = pl.BlockSpec((tm, D), lambda i, k: (i, 0))
    b_spec = pl.BlockSpec((D, tk), lambda i, k: (0, k)) if tb else pl.BlockSpec((tk, D), lambda i, k: (k, 0))
    return pl.pallas_call(
        body, name=name,
        out_shape=tuple([jax.ShapeDtypeStruct((m, D), dt) for dt in row_outs] + [jax.ShapeDtypeStruct(s, F32) for s in sum_outs]),
        grid=(m // tm, nk),
        in_specs=[pl.BlockSpec((tm, tk), lambda i, k: (i, k)), b_spec] + [row] * n_r
        + [pl.BlockSpec((1, D), lambda i, k: (0, 0))] * n_v + [ANY] * len(extra),
        out_specs=tuple([row] * n_ro + [pl.BlockSpec(s, lambda i, k: (0, 0)) for s in sum_outs]),
        scratch_shapes=[pltpu.VMEM((tm, D), F32)],
        compiler_params=_cparams("arbitrary", "arbitrary"))(a, b, *rows_in, *vecs_in, *extra)


def _residual_norm(row0, acc, h, g):
    hv = h + acc
    return (hv, (hv * _rms(hv)) * g), ()


def _rms_bwd_math(hv, dn, gv):
    r = _rms(hv)
    hr = hv * r
    dng = dn * gv
    dh = r * (dng - hr * jnp.mean(dng * hr, axis=-1, keepdims=True))
    return dh, dn * hr


def _loss_head(row0, acc, h1, tgt, g):
    hv = h1 + acc
    hr = hv * _rms(hv)
    pos = row0 % LP + lax.broadcasted_iota(jnp.int32, (EPI_ROWS, 1), 0)
    valid = (pos >= N_META) & (pos < L_REAL)
    err = jnp.where(valid, hr * g - tgt, 0.0)
    part = 0.5 * jnp.sum(jnp.mean(err * err, axis=-1, keepdims=True))
    dy = err * (1.0 / D)
    dh, dgrow = _rms_bwd_math(hv, dy, g)
    return (dh, dh), (jnp.full((8, LANES), part, F32), jnp.sum(dgrow, axis=0, keepdims=True))


def _residual_norm_bwd(row0, acc, h, dres, g):
    dh, dgrow = _rms_bwd_math(h, acc, g)
    dh = dh + dres
    return (dh, dh), (jnp.sum(dgrow, axis=0, keepdims=True),)


def _residual_norm_bwd_f32(row0, acc, h, dres, g):
    tiles, sums = _residual_norm_bwd(row0, acc, h, dres, g)
    return tiles[:1], sums


GATE_BLK = GATE_COL // D


def _sigmoid(x):
    return 1.0 / (1.0 + jnp.exp(-x))


def _merge_fwd(p_sb, p_fx, proj, name):
    t = p_sb.shape[0]
    row = pl.BlockSpec((TR, D), lambda i: (i, 0))

    def body(ps_ref, pf_ref, gs_ref, gf_ref, o_ref):
        o_ref[...] = (_sigmoid(gs_ref[...]) * ps_ref[...] + _sigmoid(gf_ref[...]) * pf_ref[...]).astype(BF16)

    return pl.pallas_call(
        body, name=name, out_shape=jax.ShapeDtypeStruct((t, D), BF16), grid=(t // TR,),
        in_specs=[row, row, pl.BlockSpec((TR, D), lambda i: (i, GATE_BLK)),
                  pl.BlockSpec((TR, D), lambda i: (i, GATE_BLK + 1))],
        out_specs=row, compiler_params=_cparams("parallel"))(p_sb, p_fx, proj, proj)


def _merge_bwd(dm, p, proj, dproj, which, name):
    t = dm.shape[0]
    row = pl.BlockSpec((TR, D), lambda i: (i, 0))
    gate = pl.BlockSpec((TR, D), lambda i: (i, GATE_BLK + which))

    def body(dm_ref, p_ref, g_ref, *rest):
        dp_ref, dg_ref = rest[-2:]
        dmv = dm_ref[...]
        s = _sigmoid(g_ref[...])
        dp_ref[...] = (dmv * s).astype(BF16)
        dg_ref[...] = (dmv * p_ref[...] * s * (1.0 - s)).astype(BF16)

    out_shape = (jax.ShapeDtypeStruct((t, D), BF16), jax.ShapeDtypeStruct((t, IN_P), BF16))
    if dproj is None:
        return pl.pallas_call(
            body, name=name, out_shape=out_shape, grid=(t // TR,), in_specs=[row, row, gate],
            out_specs=(row, gate), compiler_params=_cparams("parallel"))(dm, p, proj)
    return pl.pallas_call(
        body, name=name, out_shape=out_shape, grid=(t // TR,), in_specs=[row, row, gate, ANY],
        out_specs=(row, gate), input_output_aliases={3: 1}, compiler_params=_cparams("parallel"))(dm, p, proj, dproj)


CH = 288


def _chunk(c, n=CH):
    return pl.ds(pl.multiple_of(c * CH, 8), n)


def _conv_taps(u_ref, c):
    x = u_ref[_chunk(c), :]
    prev = u_ref[pl.ds(pl.multiple_of(jnp.maximum(c * CH - 8, 0), 8), 8), :]
    xx = jnp.concatenate([jnp.where(c == 0, 0.0, prev), x], axis=0)
    return x, pltpu.roll(xx, 1, 0)[8:], pltpu.roll(xx, 2, 0)[8:]


def _conv_glu_fwd(u, cw, nseq, name):
    nblk = D_FF // FFC

    def body(u_ref, cw_ref, o_ref):
        cwv = cw_ref[...]

        def step(c, _):
            x, x1, x2 = _conv_taps(u_ref, c)
            uc = cwv[0:1, :] * x2 + cwv[1:2, :] * x1 + cwv[2:3, :] * x
            a, b = uc[:, :FFC], uc[:, FFC:]
            o_ref[_chunk(c), :] = (a * _sigmoid(a) * b).astype(BF16)
            return 0

        lax.fori_loop(0, LP // CH, step, 0)

    return pl.pallas_call(
        body, name=name, out_shape=jax.ShapeDtypeStruct((nseq * LP, D_FF), BF16), grid=(nseq, nblk),
        in_specs=[pl.BlockSpec((LP, 2 * FFC), lambda s, j: (s, j)), pl.BlockSpec((3, 2 * FFC), lambda s, j: (0, j))],
        out_specs=pl.BlockSpec((LP, FFC), lambda s, j: (s, j)),
        compiler_params=_cparams("parallel", "parallel"))(u, cw)


def _conv_glu_bwd(u, cw, dact, nseq, name):
    nblk = D_FF // FFC
    nch = LP // CH

    def body(u_ref, cw_ref, da_ref, du_ref, dcw_ref):
        s = pl.program_id(1)
        cwv = cw_ref[...]

        def step(k, carry):
            nxt, p0, p1, p2 = carry
            c = nch - 1 - k
            x, x1, x2 = _conv_taps(u_ref, c)
            uc = cwv[0:1, :] * x2 + cwv[1:2, :] * x1 + cwv[2:3, :] * x
            a, b = uc[:, :FFC], uc[:, FFC:]
            sa = _sigmoid(a)
            dactv = da_ref[_chunk(c), :]
            da = dactv * b * (sa * (1.0 + a * (1.0 - sa)))
            db = dactv * (a * sa)
            duc = jnp.concatenate([da, db], axis=1)
            dd = jnp.concatenate([duc, nxt], axis=0)
            du = (cwv[2:3, :] * duc + cwv[1:2, :] * pltpu.roll(dd, CH + 7, 0)[:CH]
                  + cwv[0:1, :] * pltpu.roll(dd, CH + 6, 0)[:CH])
            du_ref[_chunk(c), :] = du.astype(BF16)
            return (duc[:8], p0 + jnp.sum(duc * x2, axis=0, keepdims=True),
                    p1 + jnp.sum(duc * x1, axis=0, keepdims=True), p2 + jnp.sum(duc * x, axis=0, keepdims=True))

        zrow = jnp.zeros((1, 2 * FFC), F32)
        _, p0, p1, p2 = lax.fori_loop(0, nch, step, (jnp.zeros((8, 2 * FFC), F32), zrow, zrow, zrow))

        @pl.when(s == 0)
        def _():
            dcw_ref[...] = jnp.zeros_like(dcw_ref)

        dcw_ref[...] += jnp.concatenate([p0, p1, p2], axis=0)

    return pl.pallas_call(
        body, name=name,
        out_shape=(jax.ShapeDtypeStruct((nseq * LP, 2 * D_FF), BF16), jax.ShapeDtypeStruct((3, 2 * D_FF), F32)),
        grid=(nblk, nseq),
        in_specs=[pl.BlockSpec((LP, 2 * FFC), lambda j, s: (s, j)), pl.BlockSpec((3, 2 * FFC), lambda j, s: (0, j)),
                  pl.BlockSpec((LP, FFC), lambda j, s: (s, j))],
        out_specs=(pl.BlockSpec((LP, 2 * FFC), lambda j, s: (s, j)), pl.BlockSpec((3, 2 * FFC), lambda j, s: (0, j))),
        compiler_params=_cparams("parallel", "arbitrary"))(u, cw, dact)


F_BLK = F_COL // LANES
CB = 128


def _split3(x):
    hi = x.astype(BF16)
    r1 = x - hi.astype(F32)
    mid = r1.astype(BF16)
    lo = (r1 - mid.astype(F32)).astype(BF16)
    return hi, mid, lo


def _tri_dot(tri, x):
    hi, mid, lo = _split3(x)
    d = functools.partial(jnp.dot, preferred_element_type=F32)
    return d(tri, hi) + d(tri, mid) + d(tri, lo)


def _log_sigmoid(x):
    return jnp.minimum(x, 0.0) - jnp.log(1.0 + jnp.exp(-jnp.abs(x)))


def _gate_fwd(proj, bf, nseq, name):
    def body(f_ref, b_ref, c_ref):
        r_i = lax.broadcasted_iota(jnp.int32, (CB, CB), 0)
        c_i = lax.broadcasted_iota(jnp.int32, (CB, CB), 1)
        tri = (c_i <= r_i).astype(BF16)
        bv = b_ref[...]

        def step(k, carry):
            rows = pl.ds(pl.multiple_of(k * CB, CB), CB)
            lf = _log_sigmoid(f_ref[rows, :] + bv)
            c_ref[rows, :] = _tri_dot(tri, lf) + carry
            return carry + jnp.sum(lf, axis=0, keepdims=True)

        lax.fori_loop(0, LP // CB, step, jnp.zeros((1, LANES), F32))

    return pl.pallas_call(
        body, name=name, out_shape=jax.ShapeDtypeStruct((nseq * LP, LANES), F32), grid=(nseq,),
        in_specs=[pl.BlockSpec((LP, LANES), lambda s: (s, F_BLK)), pl.BlockSpec((1, LANES), lambda s: (0, 0))],
        out_specs=pl.BlockSpec((LP, LANES), lambda s: (s, 0)),
        compiler_params=_cparams("parallel"))(proj, bf)


def _gate_bwd(proj, bf, dc, dproj, nseq, name):
    def body(f_ref, b_ref, dc_ref, _, df_ref, db_ref):
        s = pl.program_id(0)
        r_i = lax.broadcasted_iota(jnp.int32, (CB, CB), 0)
        c_i = lax.broadcasted_iota(jnp.int32, (CB, CB), 1)
        tri = (c_i >= r_i).astype(BF16)
        bv = b_ref[...]

        def step(kk, carry):
            carry_c, carry_b = carry
            k = LP // CB - 1 - kk
            rows = pl.ds(pl.multiple_of(k * CB, CB), CB)
            dcv = dc_ref[rows, :]
            dlf = _tri_dot(tri, dcv) + carry_c
            df = dlf * _sigmoid(-(f_ref[rows, :] + bv))
            df_ref[rows, :] = jnp.concatenate([df, jnp.zeros_like(df)], axis=1).astype(BF16)
            return carry_c + jnp.sum(dcv, axis=0, keepdims=True), carry_b + jnp.sum(df, axis=0, keepdims=True)

        zero = jnp.zeros((1, LANES), F32)
        _, dbp = lax.fori_loop(0, LP // CB, step, (zero, zero))

        @pl.when(s == 0)
        def _():
            db_ref[...] = jnp.zeros_like(db_ref)

        db_ref[...] += dbp

    return pl.pallas_call(
        body, name=name,
        out_shape=(jax.ShapeDtypeStruct(dproj.shape, BF16), jax.ShapeDtypeStruct((1, LANES), F32)), grid=(nseq,),
        in_specs=[pl.BlockSpec((LP, LANES), lambda s: (s, F_BLK)), pl.BlockSpec((1, LANES), lambda s: (0, 0)),
                  pl.BlockSpec((LP, LANES), lambda s: (s, 0)), ANY],
        out_specs=(pl.BlockSpec((LP, 2 * LANES), lambda s: (s, F_COL // (2 * LANES))), pl.BlockSpec((1, LANES), lambda s: (0, 0))),
        input_output_aliases={3: 0},
        compiler_params=_cparams("arbitrary"))(proj, bf, dc, dproj)


SCALE = 0.125
NEG = -1e30


def _dot_nt(a, b):
    return lax.dot_general(a, b, (((1,), (1,)), ((), ())), preferred_element_type=F32)


def _dot_tn(a, b):
    return lax.dot_general(a, b, (((0,), (0,)), ((), ())), preferred_element_type=F32)


def _dot(a, b):
    return jnp.dot(a, b, preferred_element_type=F32)


def _blk(i):
    return pl.ds(pl.multiple_of(i * BQ, BQ), BQ)


def _tile_iotas():
    return lax.broadcasted_iota(jnp.int32, (BQ, BQ), 0), lax.broadcasted_iota(jnp.int32, (BQ, BQ), 1)


def _lane_iota():
    return lax.broadcasted_iota(jnp.int32, (BQ, LANES), 1)


def _head_masks():
    lane = _lane_iota()
    return lane < HEAD, lane >= HEAD


def _only(mask, x):
    return jnp.where(mask, x, jnp.zeros_like(x))


def _pick_lane(x, idx):
    return jnp.sum(jnp.where(_lane_iota() == idx, x, 0.0), axis=1, keepdims=True)


def _chains(npair):
    return [(pp, h) for pp in range(npair) for h in range(2)]


def _load_qkv(p_ref, q_s, k_s, v_s):
    for pp in range(q_s.shape[0]):
        base = pp * PAIR_W
        q_s[pp] = (p_ref[:, base:base + LANES] * SCALE).astype(BF16)
        k_s[pp] = p_ref[:, base + LANES:base + 2 * LANES].astype(BF16)
        v_s[pp] = p_ref[:, base + 2 * LANES:base + 3 * LANES].astype(BF16)


def _softplus(z):
    return jnp.maximum(z, 0.0) + jnp.log(1.0 + jnp.exp(-jnp.abs(z)))


def _hi_lo(x):
    hi = x.astype(BF16)
    return hi, (x - hi.astype(F32)).astype(BF16)


def _sb_tile_weights(q, k, strict, r, u_suf):
    n = len(q)
    z = [_dot_nt(q[c], k[c]) for c in range(n)]
    sp = [_softplus(zc) for zc in z]
    lk = [-spc if strict is None else jnp.where(strict, -spc, 0.0) for spc in sp]
    parts = [_hi_lo(lkc) for lkc in lk]
    suf = [_dot(hi, u_suf) + _dot(lo, u_suf) for hi, lo in parts]
    w = [jnp.exp(z[c] - sp[c] + r[c] + suf[c]) for c in range(n)]
    if strict is not None:
        w = [jnp.where(strict, wc, 0.0) for wc in w]
    r_next = [r[c] + suf[c][:, 0:1] + lk[c][:, 0:1] for c in range(n)]
    return w, sp, r_next


def _group_spec(kind, npair):
    return pl.BlockSpec((LP, npair * PAIR_W), lambda s, g: (s, (NH // (2 * npair)) * kind + g))


def _gheads_spec(npair):
    return pl.BlockSpec((LP, npair * LANES), lambda s, g: (s, g))


def _qkv_scratch(npair):
    return [pltpu.VMEM((npair, LP, LANES), BF16)] * 3


SEQ_SPEC = pl.BlockSpec((LP, LANES), lambda s, g: (s, 0))
RS_STRIDE = 16


def _pair_cols(pp):
    return slice(pp * LANES, (pp + 1) * LANES)


def _sb_fwd(proj, nseq, npair, name):
    t = nseq * LP
    chains = _chains(npair)

    def body(p_ref, o_ref, rs_ref, q_s, k_s, v_s, acc_ref, r_ref, rb_ref):
        _load_qkv(p_ref, q_s, k_s, v_s)
        row, col = _tile_iotas()
        u_suf = (row > col).astype(BF16)
        diag = col < row
        lane = _lane_iota()
        heads = _head_masks()

        def qblock(i, _):
            acc_ref[...] = jnp.zeros_like(acc_ref)
            rb_ref[...] = jnp.zeros_like(rb_ref)
            r_ref[...] = jnp.zeros_like(r_ref)
            qb = [q_s[pp, _blk(i), :] for pp in range(npair)]

            def tile(j, strict):
                kj = [k_s[pp, _blk(j), :] for pp in range(npair)]
                vj = [v_s[pp, _blk(j), :] for pp in range(npair)]
                r = [r_ref[c] for c in range(len(chains))]
                w, _, r_next = _sb_tile_weights([_only(heads[h], qb[pp]) for pp, h in chains],
                                                [kj[pp] for pp, _ in chains], strict, r, u_suf)
                pv = [_dot(w[c].astype(BF16), _only(heads[h], vj[pp])) for c, (pp, h) in enumerate(chains)]
                for pp in range(npair):
                    acc_ref[pp] += pv[2 * pp] + pv[2 * pp + 1]
                    rb_ref[pp] = jnp.where(lane == j, r[2 * pp], jnp.where(lane == RS_STRIDE + j, r[2 * pp + 1], rb_ref[pp]))
                for c in range(len(chains)):
                    r_ref[c] = r_next[c]

            tile(i, diag)

            def kblock(jj, _):
                tile(i - jj, None)
                return 0

            lax.fori_loop(1, i + 1, kblock, 0)
            for pp in range(npair):
                o_ref[_blk(i), _pair_cols(pp)] = acc_ref[pp].astype(BF16)
                rs_ref[_blk(i), _pair_cols(pp)] = rb_ref[pp]
            return 0

        lax.fori_loop(0, NBLK, qblock, 0)

    return pl.pallas_call(
        body, name=name,
        out_shape=(jax.ShapeDtypeStruct((t, W_ATT), BF16), jax.ShapeDtypeStruct((t, W_ATT), F32)),
        grid=(nseq, NH // (2 * npair)), in_specs=[_group_spec(0, npair)], out_specs=(_gheads_spec(npair), _gheads_spec(npair)),
        scratch_shapes=_qkv_scratch(npair) + [pltpu.VMEM((npair, BQ, LANES), F32), pltpu.VMEM((2 * npair, BQ, 1), F32),
                                      pltpu.VMEM((npair, BQ, LANES), F32)],
        compiler_params=_cparams("parallel", "parallel"))(proj)


def _sb_bwd(proj, do, rs, dproj, after, nseq, npair, name):
    chains = _chains(npair)

    def body(p_ref, do_ref, rs_ref, _, _after, dp_ref, q_s, k_s, v_s, dqa_ref, dka_ref, dva_ref, ep_ref):
        _load_qkv(p_ref, q_s, k_s, v_s)
        row, col = _tile_iotas()
        u_suf = (row > col).astype(BF16)
        u_pre = (row < col).astype(BF16)
        diag = col < row
        heads = _head_masks()
        dka_ref[...] = jnp.zeros_like(dka_ref)
        dva_ref[...] = jnp.zeros_like(dva_ref)
        nc = len(chains)

        def qblock(i, _):
            rb = [rs_ref[_blk(i), _pair_cols(pp)] for pp in range(npair)]
            qb = [q_s[pp, _blk(i), :] for pp in range(npair)]
            dob = [do_ref[_blk(i), _pair_cols(pp)] for pp in range(npair)]
            dqa_ref[...] = jnp.zeros_like(dqa_ref)
            ep_ref[...] = jnp.zeros_like(ep_ref)

            def tile(j, strict):
                kj = [k_s[pp, _blk(j), :] for pp in range(npair)]
                vj = [v_s[pp, _blk(j), :] for pp in range(npair)]
                q = [_only(heads[h], qb[pp]) for pp, h in chains]
                dov = [_only(heads[h], dob[pp]) for pp, h in chains]
                r = [_pick_lane(rb[pp], RS_STRIDE * h + j) for pp, h in chains]
                dw = [_dot_nt(dov[c], vj[pp]) for c, (pp, _) in enumerate(chains)]
                w, sp, _ = _sb_tile_weights(q, [kj[pp] for pp, _ in chains], strict, r, u_suf)
                e = [dw[c] * w[c] for c in range(nc)]
                e_pre = [ep_ref[c] + _dot(e[c].astype(BF16), u_pre) for c in range(nc)]
                dz = []
                for c in range(nc):
                    ep_ref[c] += jnp.sum(e[c], axis=1, keepdims=True)
                    sneg = jnp.exp(-sp[c])
                    dzc = e[c] * sneg - (1.0 - sneg) * e_pre[c]
                    if strict is not None:
                        dzc = jnp.where(strict, dzc, 0.0)
                    dz.append(dzc.astype(BF16))
                dq = [_dot(dz[c], _only(heads[h], kj[pp])) for c, (pp, h) in enumerate(chains)]
                dk = [_dot_tn(dz[c], q[c]) for c in range(nc)]
                dv = [_dot_tn(w[c].astype(BF16), dov[c]) for c in range(nc)]
                for pp in range(npair):
                    dqa_ref[pp] += dq[2 * pp] + dq[2 * pp + 1]
                    dka_ref[pp, _blk(j), :] += dk[2 * pp] + dk[2 * pp + 1]
                    dva_ref[pp, _blk(j), :] += dv[2 * pp] + dv[2 * pp + 1]

            def kblock(j, _):
                tile(j, None)
                return 0

            lax.fori_loop(0, i, kblock, 0)
            tile(i, diag)
            for pp in range(npair):
                dp_ref[_blk(i), pp * PAIR_W:pp * PAIR_W + LANES] = (dqa_ref[pp] * SCALE).astype(BF16)
            return 0

        lax.fori_loop(0, NBLK, qblock, 0)
        for pp in range(npair):
            dp_ref[:, pp * PAIR_W + LANES:pp * PAIR_W + 2 * LANES] = dka_ref[pp].astype(BF16)
            dp_ref[:, pp * PAIR_W + 2 * LANES:pp * PAIR_W + 3 * LANES] = dva_ref[pp].astype(BF16)

    return pl.pallas_call(
        body, name=name, out_shape=jax.ShapeDtypeStruct(dproj.shape, BF16), grid=(nseq, NH // (2 * npair)),
        in_specs=[_group_spec(0, npair), _gheads_spec(npair), _gheads_spec(npair), ANY, ANY], out_specs=_group_spec(0, npair),
        input_output_aliases={3: 0},
        scratch_shapes=_qkv_scratch(npair) + [pltpu.VMEM((npair, BQ, LANES), F32), pltpu.VMEM((npair, LP, LANES), F32),
                                      pltpu.VMEM((npair, LP, LANES), F32), pltpu.VMEM((2 * npair, BQ, 1), F32)],
        compiler_params=_cparams("parallel", "parallel"))(proj, do, rs, dproj, after)


CROW_SPEC = pl.BlockSpec((None, NH, LP), lambda s, g: (s, 0, 0))


def _fox_scores(qi, kj, cq, ck, causal):
    z = _dot_nt(qi, kj) + (cq - ck)
    return z if causal is None else jnp.where(causal, z, NEG)


def _key_cols(cr_ref, head, j):
    return cr_ref[pl.ds(head, 1), pl.ds(pl.multiple_of(j * BQ, BQ), BQ)]


def _fox_fwd(proj, c, crow, nseq, npair, name):
    t = nseq * LP
    chains = _chains(npair)

    def body(p_ref, c_ref, cr_ref, o_ref, o32_ref, lse_ref, q_s, k_s, v_s, acc_ref, m_ref, l_ref):
        _load_qkv(p_ref, q_s, k_s, v_s)
        row, col = _tile_iotas()
        diag = col <= row
        lane = _lane_iota()
        heads = _head_masks()
        head0 = 2 * npair * pl.program_id(1)
        nc = len(chains)

        def qblock(i, _):
            cblk = c_ref[_blk(i), :]
            qb = [q_s[pp, _blk(i), :] for pp in range(npair)]
            cq = [_pick_lane(cblk, head0 + c) for c in range(nc)]
            acc_ref[...] = jnp.zeros_like(acc_ref)
            m_ref[...] = jnp.full_like(m_ref, NEG)
            l_ref[...] = jnp.zeros_like(l_ref)

            def tile(j, causal):
                kj = [k_s[pp, _blk(j), :] for pp in range(npair)]
                vj = [v_s[pp, _blk(j), :] for pp in range(npair)]
                z = [_fox_scores(_only(heads[h], qb[pp]), kj[pp], cq[c], _key_cols(cr_ref, head0 + c, j), causal)
                     for c, (pp, h) in enumerate(chains)]
                p, alpha = [], []
                for c in range(nc):
                    m_old = m_ref[c]
                    m_new = jnp.maximum(m_old, jnp.max(z[c], axis=1, keepdims=True))
                    alpha.append(jnp.exp(m_old - m_new))
                    pc = jnp.exp(z[c] - m_new)
                    l_ref[c] = alpha[c] * l_ref[c] + jnp.sum(pc, axis=1, keepdims=True)
                    m_ref[c] = m_new
                    p.append(pc.astype(BF16))
                pv = [_dot(p[c], _only(heads[h], vj[pp])) for c, (pp, h) in enumerate(chains)]
                for c in range(nc):
                    acc_ref[c] = alpha[c] * acc_ref[c] + pv[c]

            def kblock(j, _):
                tile(j, None)
                return 0

            lax.fori_loop(0, i, kblock, 0)
            tile(i, diag)
            for pp in range(npair):
                out = acc_ref[2 * pp] / l_ref[2 * pp] + acc_ref[2 * pp + 1] / l_ref[2 * pp + 1]
                o_ref[_blk(i), _pair_cols(pp)] = out.astype(BF16)
                o32_ref[_blk(i), _pair_cols(pp)] = out
                lse = [m_ref[2 * pp + h] + jnp.log(l_ref[2 * pp + h]) for h in range(2)]
                lse_ref[_blk(i), _pair_cols(pp)] = jnp.where(lane == 0, lse[0], jnp.where(lane == 1, lse[1], 0.0))
            return 0

        lax.fori_loop(0, NBLK, qblock, 0)

    return pl.pallas_call(
        body, name=name,
        out_shape=(jax.ShapeDtypeStruct((t, W_ATT), BF16), jax.ShapeDtypeStruct((t, W_ATT), F32),
                   jax.ShapeDtypeStruct((t, W_ATT), F32)),
        grid=(nseq, NH // (2 * npair)), in_specs=[_group_spec(1, npair), SEQ_SPEC, CROW_SPEC], out_specs=(_gheads_spec(npair), _gheads_spec(npair), _gheads_spec(npair)),
        scratch_shapes=_qkv_scratch(npair) + [pltpu.VMEM((2 * npair, BQ, LANES), F32), pltpu.VMEM((2 * npair, BQ, 1), F32),
                                      pltpu.VMEM((2 * npair, BQ, 1), F32)],
        compiler_params=_cparams("parallel", "parallel"))(proj, c, crow)


def _fox_bwd(proj, c, crow, o32, lse, do, dproj, nseq, npair, name):
    t = nseq * LP
    chains = _chains(npair)

    def body(p_ref, c_ref, cr_ref, o_ref, lse_ref, do_ref, _, dp_ref, dck_ref, dcq_ref,
             q_s, k_s, v_s, dqa_ref, dka_ref, dva_ref, rsum_ref):
        _load_qkv(p_ref, q_s, k_s, v_s)
        row, col = _tile_iotas()
        diag = col <= row
        lane = _lane_iota()
        heads = _head_masks()
        sub = lax.broadcasted_iota(jnp.int32, (NH, BQ), 0)
        group = pl.program_id(1)
        head0 = 2 * npair * group
        nc = len(chains)
        dka_ref[...] = jnp.zeros_like(dka_ref)
        dva_ref[...] = jnp.zeros_like(dva_ref)

        @pl.when(group == 0)
        def _():
            dck_ref[...] = jnp.zeros_like(dck_ref)
            dcq_ref[...] = jnp.zeros_like(dcq_ref)

        def qblock(i, _):
            dqa_ref[...] = jnp.zeros_like(dqa_ref)
            rsum_ref[...] = jnp.zeros_like(rsum_ref)
            cblk = c_ref[_blk(i), :]
            qb = [q_s[pp, _blk(i), :] for pp in range(npair)]
            dob = [do_ref[_blk(i), _pair_cols(pp)] for pp in range(npair)]
            prod = [dob[pp].astype(F32) * o_ref[_blk(i), _pair_cols(pp)] for pp in range(npair)]
            cq = [_pick_lane(cblk, head0 + c) for c in range(nc)]
            lse_i = [_pick_lane(lse_ref[_blk(i), _pair_cols(pp)], h) for pp, h in chains]
            delta = [jnp.sum(_only(heads[h], prod[pp]), axis=1, keepdims=True) for pp, h in chains]

            def tile(j, causal):
                kj = [k_s[pp, _blk(j), :] for pp in range(npair)]
                vj = [v_s[pp, _blk(j), :] for pp in range(npair)]
                keys = pl.ds(pl.multiple_of(j * BQ, BQ), BQ)
                q = [_only(heads[h], qb[pp]) for pp, h in chains]
                dov = [_only(heads[h], dob[pp]) for pp, h in chains]
                z = [_fox_scores(q[c], kj[pp], cq[c], _key_cols(cr_ref, head0 + c, j), causal)
                     for c, (pp, _) in enumerate(chains)]
                dpv = [_dot_nt(dov[c], vj[pp]) for c, (pp, _) in enumerate(chains)]
                p = [jnp.exp(z[c] - lse_i[c]) for c in range(nc)]
                ds = [p[c] * (dpv[c] - delta[c]) for c in range(nc)]
                dsb = [d.astype(BF16) for d in ds]
                dq = [_dot(dsb[c], _only(heads[h], kj[pp])) for c, (pp, h) in enumerate(chains)]
                dk = [_dot_tn(dsb[c], q[c]) for c in range(nc)]
                dv = [_dot_tn(p[c].astype(BF16), dov[c]) for c in range(nc)]
                for pp in range(npair):
                    dqa_ref[pp] += dq[2 * pp] + dq[2 * pp + 1]
                    dka_ref[pp, _blk(j), :] += dk[2 * pp] + dk[2 * pp + 1]
                    dva_ref[pp, _blk(j), :] += dv[2 * pp] + dv[2 * pp + 1]
                col_sums = jnp.zeros((NH, BQ), F32)
                for c in range(nc):
                    col_sums = col_sums + jnp.where(sub == head0 + c, jnp.sum(ds[c], axis=0, keepdims=True), 0.0)
                    rsum_ref[c] += jnp.sum(ds[c], axis=1, keepdims=True)
                dck_ref[:, keys] = dck_ref[:, keys] - col_sums

            def kblock(j, _):
                tile(j, None)
                return 0

            lax.fori_loop(0, i, kblock, 0)
            tile(i, diag)
            row_sums = jnp.zeros((BQ, LANES), F32)
            for c in range(nc):
                row_sums = row_sums + jnp.where(lane == head0 + c, rsum_ref[c], 0.0)
            dcq_ref[_blk(i), :] += row_sums
            for pp in range(npair):
                dp_ref[_blk(i), pp * PAIR_W:pp * PAIR_W + LANES] = (dqa_ref[pp] * SCALE).astype(BF16)
            return 0

        lax.fori_loop(0, NBLK, qblock, 0)
        for pp in range(npair):
            dp_ref[:, pp * PAIR_W + LANES:pp * PAIR_W + 2 * LANES] = dka_ref[pp].astype(BF16)
            dp_ref[:, pp * PAIR_W + 2 * LANES:pp * PAIR_W + 3 * LANES] = dva_ref[pp].astype(BF16)

    return pl.pallas_call(
        body, name=name,
        out_shape=(jax.ShapeDtypeStruct(dproj.shape, BF16), jax.ShapeDtypeStruct((nseq, NH, LP), F32),
                   jax.ShapeDtypeStruct((t, LANES), F32)),
        grid=(nseq, NH // (2 * npair)),
        in_specs=[_group_spec(1, npair), SEQ_SPEC, CROW_SPEC, _gheads_spec(npair), _gheads_spec(npair), _gheads_spec(npair), ANY],
        out_specs=(_group_spec(1, npair), CROW_SPEC, SEQ_SPEC),
        input_output_aliases={6: 0},
        scratch_shapes=_qkv_scratch(npair) + [pltpu.VMEM((npair, BQ, LANES), F32), pltpu.VMEM((npair, LP, LANES), F32),
                                      pltpu.VMEM((npair, LP, LANES), F32), pltpu.VMEM((2 * npair, BQ, 1), F32)],
        compiler_params=_cparams("parallel", "arbitrary"))(proj, c, crow, o32, lse, do, dproj)


def _adamw_math(w, g, m, v):
    m = B1 * m + (1.0 - B1) * g
    v = B2 * v + (1.0 - B2) * (g * g)
    m_hat = m / (1.0 - B1 ** STEP)
    v_hat = v / (1.0 - B2 ** STEP)
    delta = -LR * (m_hat / (jnp.sqrt(v_hat) + EPS) + WD * w)
    return delta, m, v


def _sum_adamw(parts, w, m, v, tr, name):
    rows, cols = w.shape
    cp = parts.shape[2]
    assert rows % tr == 0 and parts.shape[1] == rows

    def body(p_ref, w_ref, m_ref, v_ref, g_ref, d_ref, nm_ref, nv_ref):
        gsum = p_ref[0].astype(F32)
        for s in range(1, N_DEV):
            gsum = gsum + p_ref[s].astype(F32)
        gsum = gsum[:, :cols]
        d, nm, nv = _adamw_math(w_ref[...], gsum, m_ref[...], v_ref[...])
        g_ref[...] = gsum
        d_ref[...] = d
        nm_ref[...] = nm
        nv_ref[...] = nv

    blk = pl.BlockSpec((tr, cols), lambda i: (i, 0))
    out = jax.ShapeDtypeStruct((rows, cols), F32)
    return pl.pallas_call(
        body, name=name, out_shape=(out, out, out, out), grid=(rows // tr,),
        in_specs=[pl.BlockSpec((N_DEV, tr, cp), lambda i: (0, i, 0)), blk, blk, blk],
        out_specs=(blk, blk, blk, blk), compiler_params=_cparams("parallel"))(parts, w, m, v)


def _local_step(x, tgt, meta, tgt_front, g_mix, b_forget, g_ffn, g_final, first_weights, late_weights, early_grads, last_grad):
    nseq = x.shape[0]
    t = nseq * LP
    tm = LP // 2
    mm = functools.partial(_matmul, tm=tm)

    h0 = _pad_rows(meta, x, nseq, "pad_x").reshape(t, D)
    tgt_p = _pad_rows(tgt_front, tgt, nseq, "pad_target").reshape(t, D)
    bf = jnp.pad(b_forget.reshape(1, NH), ((0, 0), (0, LANES - NH)))

    n1 = _norm_fwd(h0, g_mix, "norm1")
    w_in_p, started = first_weights(n1)
    proj = mm(n1, w_in_p, out_dtype=F32, tn=1792, tk=D, after=started, name="in_proj")
    c = _gate_fwd(proj, bf, nseq, "gate_fwd")
    crow = c[:, :NH].reshape(nseq, LP, NH).transpose(0, 2, 1)
    o_sb, rs = _sb_fwd(proj, nseq, 2, "sb_fwd")
    o_fx, o_fx32, lse = _fox_fwd(proj, c, crow, nseq, 1, "fox_fwd")
    w_bsb, w_bfx, w_out, w_up_i, cw_i, w_down = late_weights(o_fx)
    p_sb = mm(o_sb, w_bsb, out_dtype=F32, tn=D, tk=W_ATT, name="branch_sb")
    p_fx = mm(o_fx, w_bfx, out_dtype=F32, tn=D, tk=W_ATT, name="branch_fox")
    merged = _merge_fwd(p_sb, p_fx, proj, "merge_fwd")
    rows = functools.partial(_matmul_rows, tm=LP // 4)
    h1, n2 = rows(merged, w_out, [h0], [g_ffn], _residual_norm, [F32, BF16], [], tk=D, name="out_proj_norm2")
    u = mm(n2, w_up_i, out_dtype=F32, tn=1408, tk=D, name="up_proj")
    act = _conv_glu_fwd(u, cw_i, nseq, "conv_glu_fwd")

    dh2, dh2b, loss, dg_final = rows(act, w_down, [h1, tgt_p], [g_final], _loss_head, [F32, BF16],
                                     [(8, LANES), (1, D)], tk=D_FF, name="down_proj_loss")
    d_down = _matmul(act, dh2b, out_dtype=BF16, tm=1408, tn=D, tk=LP, ta=True, name="d_w_down")
    dact = mm(dh2b, w_down, out_dtype=F32, tn=1408, tk=D, tb=True, name="d_act")
    du, d_cw = _conv_glu_bwd(u, cw_i, dact, nseq, "conv_glu_bwd")
    d_up = _matmul(n2, du, out_dtype=BF16, tm=D, tn=1408, tk=LP, ta=True, name="d_w_up")
    dh1, dh1b, dg_ffn = rows(du, w_up_i, [h1, dh2], [g_ffn], _residual_norm_bwd, [F32, BF16], [(1, D)],
                             tk=D_FF, tb=True, name="d_n2_norm2_bwd")
    d_out = _matmul(merged, dh1b, out_dtype=BF16, tm=D, tn=D, tk=LP, ta=True, name="d_w_out")
    dmerged = mm(dh1b, w_out, out_dtype=F32, tn=D, tk=D, tb=True, name="d_merged")
    dp_sb, dproj = _merge_bwd(dmerged, p_sb, proj, None, 0, "merge_bwd_sb")
    dp_fx, dproj = _merge_bwd(dmerged, p_fx, proj, dproj, 1, "merge_bwd_fox")
    d_bsb = _matmul(o_sb, dp_sb, out_dtype=BF16, tm=W_ATT, tn=D, tk=LP, ta=True, name="d_w_branch_sb")
    d_bfx = _matmul(o_fx, dp_fx, out_dtype=BF16, tm=W_ATT, tn=D, tk=LP, ta=True, name="d_w_branch_fox")
    do_sb = mm(dp_sb, w_bsb, out_dtype=BF16, tn=W_ATT, tk=D, tb=True, name="d_o_sb")
    do_fx = mm(dp_fx, w_bfx, out_dtype=BF16, tn=W_ATT, tk=D, tb=True, name="d_o_fox")
    sent = early_grads(dict(w_branch_sb=d_bsb, w_branch_fox=d_bfx, w_out=d_out, w_up=d_up, conv_w=d_cw, w_down=d_down))
    dproj = _sb_bwd(proj, do_sb, rs, dproj, sent, nseq, 2, "sb_bwd")
    dproj, dck, dcq = _fox_bwd(proj, c, crow, o_fx32, lse, do_fx, dproj, nseq, 2, "fox_bwd")
    dc = dcq + jnp.pad(dck.transpose(0, 2, 1).reshape(t, NH), ((0, 0), (0, LANES - NH)))
    dproj, d_bf = _gate_bwd(proj, bf, dc, dproj, nseq, "gate_bwd")
    d_in = _matmul(n1, dproj, out_dtype=BF16, tm=D, tn=1792, tk=LP, ta=True, name="d_w_in")
    dh0, dg_mix = rows(dproj, w_in_p, [h0, dh1], [g_mix], _residual_norm_bwd_f32, [F32], [(1, D)],
                       tk=IN_P // 2, tb=True, after=last_grad(d_in), name="d_n1_norm1_bwd")
    dh0 = dh0.reshape(nseq, LP, D)
    grads = dict(meta_tokens=jnp.sum(dh0[:, :N_META], axis=0), norm_mix_g=dg_mix, b_forget=d_bf[:, :NH],
                 norm_ffn_g=dg_ffn, norm_final_g=dg_final)
    return loss[0, 0], _real_rows(dh0, nseq, "grad_x"), grads


REPL = (("norm_mix_g", D), ("norm_ffn_g", D), ("norm_final_g", D), ("b_forget", LANES))
REPL_ROWS = 32
META_ROWS = N_META * D // LANES


def _pack_repl(tree):
    rows = [jnp.pad(tree[name].reshape(-1), (0, n - tree[name].size)).reshape(-1, LANES) for name, n in REPL]
    packed = jnp.concatenate(rows, axis=0)
    return jnp.pad(packed, ((0, REPL_ROWS - packed.shape[0]), (0, 0)))


def _unpack_repl(packed, shapes):
    out, r = {}, 0
    for name, n in REPL:
        size = 1
        for s in shapes[name]:
            size *= s
        out[name] = packed[r:r + n // LANES].reshape(-1)[:size].reshape(shapes[name])
        r += n // LANES
    return out


def kernel(x, meta_tokens, norm_mix_g, w_in, b_forget, w_branch_sb, w_branch_fox, w_out, norm_ffn_g, w_up, conv_w, w_down, norm_final_g, loss_target, m_meta_tokens, m_norm_mix_g, m_w_in, m_b_forget, m_w_branch_sb, m_w_branch_fox, m_w_out, m_norm_ffn_g, m_w_up, m_conv_w, m_w_down, m_norm_final_g, v_meta_tokens, v_norm_mix_g, v_w_in, v_b_forget, v_w_branch_sb, v_w_branch_fox, v_w_out, v_norm_ffn_g, v_w_up, v_conv_w, v_w_down, v_norm_final_g):
    w = dict(meta_tokens=meta_tokens, norm_mix_g=norm_mix_g, w_in=w_in, b_forget=b_forget, w_branch_sb=w_branch_sb,
             w_branch_fox=w_branch_fox, w_out=w_out, norm_ffn_g=norm_ffn_g, w_up=w_up, conv_w=conv_w, w_down=w_down,
             norm_final_g=norm_final_g)
    m = dict(meta_tokens=m_meta_tokens, norm_mix_g=m_norm_mix_g, w_in=m_w_in, b_forget=m_b_forget,
             w_branch_sb=m_w_branch_sb, w_branch_fox=m_w_branch_fox, w_out=m_w_out, norm_ffn_g=m_norm_ffn_g,
             w_up=m_w_up, conv_w=m_conv_w, w_down=m_w_down, norm_final_g=m_norm_final_g)
    v = dict(meta_tokens=v_meta_tokens, norm_mix_g=v_norm_mix_g, w_in=v_w_in, b_forget=v_b_forget,
             w_branch_sb=v_w_branch_sb, w_branch_fox=v_w_branch_fox, w_out=v_w_out, norm_ffn_g=v_norm_ffn_g,
             w_up=v_w_up, conv_w=v_conv_w, w_down=v_w_down, norm_final_g=v_norm_final_g)
    shapes = {k: a.shape for k, a in w.items()}
    sharded = ("w_in", "w_branch_sb", "w_branch_fox", "w_out", "w_up", "w_down", "conv_w", "meta_tokens")
    mat = lambda tree, name: tree[name].reshape(tree[name].shape[-2:])

    def lane_pad(a, width):
        return jnp.pad(a, ((0, 0), (0, width - a.shape[1])))

    late = ("w_branch_sb", "w_branch_fox", "w_out", "w_up", "w_down", "conv_w")
    pending_w = {}
    g_meta, = _all_gather([mat(w, "meta_tokens")], "gather_meta")
    pending_w["in"], in_started = _remote_start(
        [lane_pad(mat(w, "w_in").astype(BF16), SHARD_P)], False, g_meta, "gather_w_in_start")
    meta_full = g_meta.transpose(1, 0, 2).reshape(N_META, D) + in_started[0, 0]

    def first_weights(after):
        g_in, = _remote_wait(pending_w["in"], after, "gather_w_in_wait")
        pending_w["late"], started = _remote_start(
            [mat(w, "w_branch_sb").astype(BF16), mat(w, "w_branch_fox").astype(BF16), mat(w, "w_out").astype(BF16),
             lane_pad(mat(w, "w_up").astype(BF16), SHARD_P), mat(w, "w_down").astype(BF16), mat(w, "conv_w")],
            False, g_in, "gather_late_start")
        w_in_p = _relayout(g_in, 1, IN_P, _gathered_to_full(IN_SHARD, _in_padded_to_orig), BF16, 256, "w_in_cols")[0]
        return w_in_p, started

    def late_weights(after):
        g_bsb, g_bfx, g_out, g_up, g_down, g_cw = _remote_wait(pending_w["late"], after, "gather_late_wait")
        w_up_i = _relayout(g_up, 1, 2 * D_FF, _gathered_to_full(UP_SHARD, _up_inter_to_orig), BF16, 256, "w_up_cols")[0]
        w_bsb = _relayout(g_bsb, 1, D, _gathered_to_full(ATT_SHARD, lambda d: d), BF16, 256, "w_bsb_cols")[0]
        w_bfx = _relayout(g_bfx, 1, D, _gathered_to_full(ATT_SHARD, lambda d: d), BF16, 256, "w_bfx_cols")[0]
        cw_full = g_cw.transpose(1, 0, 2).reshape(3, 2 * D_FF)
        cw_i = cw_full.reshape(3, 2, D_FF // FFC, FFC).transpose(0, 2, 1, 3).reshape(3, 2 * D_FF)
        return w_bsb, w_bfx, g_out.reshape(D, D), w_up_i, cw_i, g_down.reshape(D_FF, D)

    pending_g = {}

    def early_grads(g):
        d_cw = g["conv_w"].reshape(3, D_FF // FFC, 2, FFC).transpose(0, 2, 1, 3).reshape(3, 2 * D_FF)
        pending_g["early"], sent = _remote_start(
            [_relayout(g["w_branch_sb"][None], N_DEV, ATT_SHARD, _full_to_shards(ATT_SHARD, lambda c: c), BF16, 256, "d_w_bsb_shards"),
             _relayout(g["w_branch_fox"][None], N_DEV, ATT_SHARD, _full_to_shards(ATT_SHARD, lambda c: c), BF16, 256, "d_w_bfx_shards"),
             g["w_out"].reshape(N_DEV, D // N_DEV, D),
             _relayout(g["w_up"][None], N_DEV, SHARD_P, _full_to_shards(UP_SHARD, _UP_ORIG_TO_INTER.get), BF16, 256, "d_w_up_shards"),
             g["w_down"].reshape(N_DEV, D_FF // N_DEV, D),
             d_cw.reshape(3, N_DEV, UP_SHARD).transpose(1, 0, 2)], True, g["w_out"], "exchange_early_start")
        return sent

    def last_grad(d_in):
        shards = _relayout(d_in[None], N_DEV, SHARD_P, _full_to_shards(IN_SHARD, _IN_ORIG_TO_PADDED.get), BF16, 256, "d_w_in_shards")
        pending_g["last"], sent = _remote_start([shards], True, shards, "exchange_last_start")
        return sent

    loss, grad_x, grads = _local_step(
        x, loss_target, meta_full, jnp.zeros((N_META, D), F32) + in_started[0, 0], norm_mix_g.reshape(1, D), b_forget,
        norm_ffn_g.reshape(1, D), norm_final_g.reshape(1, D), first_weights, late_weights, early_grads, last_grad)

    small = jnp.concatenate([_pack_repl(grads), grads["meta_tokens"].reshape(META_ROWS, LANES)], axis=0)
    small, = _all_gather([small], "gather_small_grads")
    me_idx = 4 * lax.axis_index("x") + 2 * lax.axis_index("y") + lax.axis_index("c")
    p_meta = lax.dynamic_slice_in_dim(small[:, REPL_ROWS:].reshape(N_DEV, N_META, D), me_idx * ATT_SHARD, ATT_SHARD, axis=2)

    p_in, = _remote_wait(pending_g["last"], small, "exchange_last_wait")
    parts = dict(zip(late, _remote_wait(pending_g["early"], p_in, "exchange_early_wait")), w_in=p_in, meta_tokens=p_meta)
    tiles = dict(w_in=256, w_branch_sb=256, w_branch_fox=256, w_out=D // N_DEV, w_up=256, w_down=D_FF // N_DEV,
                 conv_w=3, meta_tokens=N_META)
    new = {name: _sum_adamw(parts[name], mat(w, name), mat(m, name), mat(v, name), tiles[name], "adamw_" + name)
           for name in sharded}

    routs = _sum_adamw(small[:, :REPL_ROWS], _pack_repl(w), _pack_repl(m), _pack_repl(v), REPL_ROWS, "adamw_replicated")
    repl = [_unpack_repl(o, shapes) for o in routs]

    result = [lax.psum(loss, ("x", "y", "c")), grad_x]
    for k in range(4):
        for name in w:
            result.append(new[name][k].reshape(shapes[name]) if name in new else repl[k][name])
    return tuple(result)
```

```python
import functools

import jax
import jax.numpy as jnp
from jax import lax
from jax.experimental import pallas as pl
from jax.experimental.pallas import tpu as pltpu

F32 = jnp.float32
BF16 = jnp.bfloat16

N_DEV = 8
LANES = 128
D = 1024
N_META = 16
SEQ = 2048
L_REAL = N_META + SEQ
LP = 2304
BQ = 256
NBLK = LP // BQ
HEAD = 64
NH = 8
W_ATT = NH * HEAD
PAIR_W = 3 * LANES
D_FF = 2816
IN_COLS = 5128
QKV = 6 * W_ATT
IN_P = 5376
GATE_COL = QKV
F_COL = QKV + 2 * D
FFC = 256
RMS_EPS = 1e-6
LR, B1, B2, EPS, WD, STEP = 0.001, 0.9, 0.999, 1e-08, 0.01, 10
VMEM_LIMIT = 56 * 1024 * 1024

MESH = pl.DeviceIdType.MESH
ANY = pl.BlockSpec(memory_space=pl.ANY)


def _cparams(*sem):
    return pltpu.CompilerParams(dimension_semantics=sem if sem else None, vmem_limit_bytes=VMEM_LIMIT)


def _all_gather(xs, name):
    n = len(xs)

    def body(*refs):
        x_refs, out_refs = refs[:n], refs[n:2 * n]
        send_sems, recv_sems, local_sems = refs[2 * n:]
        mx, my, mc = lax.axis_index("x"), lax.axis_index("y"), lax.axis_index("c")
        me, sibling = (mx, my, mc), (mx, my, 1 - mc)
        chips = [(1 - mx, my), (mx, 1 - my), (1 - mx, 1 - my)]

        def copy(a, k, block, to, own=False):
            px, py, pc = block
            slot = out_refs[a].at[4 * px + 2 * py + pc]
            return pltpu.make_async_remote_copy(
                src_ref=x_refs[a] if own else slot, dst_ref=slot,
                send_sem=send_sems.at[7 * a + k], recv_sem=recv_sems.at[7 * a + k],
                device_id=to, device_id_type=MESH)

        mine = [pltpu.make_async_copy(x_refs[a], out_refs[a].at[4 * mx + 2 * my + mc], local_sems.at[a]) for a in range(n)]
        for cp in mine:
            cp.start()
        first = []
        for a in range(n):
            first.append(copy(a, 0, me, sibling, own=True))
            first += [copy(a, 1 + j, me, (*chip, mc), own=True) for j, chip in enumerate(chips)]
        for cp in first:
            cp.start()
        passed = []
        for j, chip in enumerate(chips):
            for a in range(n):
                copy(a, 1 + j, (*chip, mc), me).wait_recv()
                fwd = copy(a, 4 + j, (*chip, mc), sibling)
                fwd.start()
                passed.append(fwd)
        for a in range(n):
            copy(a, 0, sibling, me).wait_recv()
            for j, chip in enumerate(chips):
                copy(a, 4 + j, (*chip, 1 - mc), me).wait_recv()
        for cp in first + passed:
            cp.wait_send()
        for cp in mine:
            cp.wait()

    return pl.pallas_call(
        body, name=name,
        out_shape=tuple(jax.ShapeDtypeStruct((N_DEV,) + x.shape, x.dtype) for x in xs),
        in_specs=[ANY] * n, out_specs=tuple([ANY] * n),
        scratch_shapes=[pltpu.SemaphoreType.DMA((7 * n,)), pltpu.SemaphoreType.DMA((7 * n,)),
                        pltpu.SemaphoreType.DMA((n,))],
    )(*xs)


HBM = pl.BlockSpec(memory_space=pltpu.HBM)
SEM = pl.BlockSpec(memory_space=pltpu.SEMAPHORE)
EFFECT = pltpu.SideEffectType.DATAFLOW_SIDE_EFFECTING


def _peer_copies(src_refs, land_refs, send_sems, recv_sems, per_peer):
    mx, my, mc = lax.axis_index("x"), lax.axis_index("y"), lax.axis_index("c")
    me_idx = 4 * mx + 2 * my + mc
    copies = []
    for k in range(1, N_DEV):
        px, py, pc = mx ^ (k >> 2), my ^ ((k >> 1) & 1), mc ^ (k & 1)
        for a, (src, land) in enumerate(zip(src_refs, land_refs)):
            copies.append(pltpu.make_async_remote_copy(
                src_ref=src.at[4 * px + 2 * py + pc] if per_peer else src, dst_ref=land.at[me_idx],
                send_sem=send_sems.at[7 * a + k - 1], recv_sem=recv_sems.at[7 * a + k - 1],
                device_id=(px, py, pc), device_id_type=MESH))
    return me_idx, copies


def _remote_start(srcs, per_peer, after, name):
    n = len(srcs)
    lands = [lax.empty(s.shape if per_peer else (N_DEV,) + s.shape, s.dtype) for s in srcs]

    def body(*refs):
        src_refs, land_refs = refs[:n], refs[n:2 * n]
        send_sems, recv_sems = refs[2 * n + 1:2 * n + 3]
        token = refs[4 * n + 3]
        stage, local_sems = refs[4 * n + 4:5 * n + 4], refs[5 * n + 4]
        me_idx, copies = _peer_copies(src_refs, land_refs, send_sems, recv_sems, per_peer)
        for cp in copies:
            cp.start()
        own = [src_refs[a].at[me_idx] if per_peer else src_refs[a] for a in range(n)]
        for hop in ([(own[a], stage[a]) for a in range(n)], [(stage[a], land_refs[a].at[me_idx]) for a in range(n)]):
            cps = [pltpu.make_async_copy(s, d, local_sems.at[a]) for a, (s, d) in enumerate(hop)]
            for cp in cps:
                cp.start()
            for cp in cps:
                cp.wait()
        token[...] = jnp.zeros_like(token)

    thru = [pltpu.HBM(a.shape, a.dtype) for a in list(srcs) + lands]
    out = pl.pallas_call(
        body, name=name,
        out_shape=(pltpu.SemaphoreType.DMA((7 * n,)), pltpu.SemaphoreType.DMA((7 * n,)), *thru,
                   jax.ShapeDtypeStruct((8, LANES), F32)),
        in_specs=[HBM] * (2 * n) + [ANY],
        out_specs=(SEM, SEM, *([HBM] * (2 * n)), pl.BlockSpec(memory_space=pltpu.VMEM)),
        input_output_aliases={i: 2 + i for i in range(2 * n)},
        scratch_shapes=[pltpu.VMEM(s.shape[1:] if per_peer else s.shape, s.dtype) for s in srcs]
        + [pltpu.SemaphoreType.DMA((n,))],
        compiler_params=pltpu.CompilerParams(has_side_effects=EFFECT),
    )(*[pltpu.with_memory_space_constraint(a, pltpu.HBM) for a in list(srcs) + lands], after)
    return dict(sems=out[:2], bufs=out[2:2 * n + 2], per_peer=per_peer), out[-1]


def _remote_wait(pending, after, name):
    bufs = pending["bufs"]
    n = len(bufs) // 2
    per_peer = pending["per_peer"]

    def body(*refs):
        src_refs, land_refs = refs[:n], refs[n:2 * n]
        send_sems, recv_sems = refs[2 * n:2 * n + 2]
        _, copies = _peer_copies(src_refs, land_refs, send_sems, recv_sems, per_peer)
        for cp in copies:
            cp.wait_send()
        for cp in copies:
            cp.wait_recv()

    out = pl.pallas_call(
        body, name=name, out_shape=tuple(pltpu.HBM(a.shape, a.dtype) for a in bufs),
        in_specs=[HBM] * (2 * n) + [SEM, SEM, ANY], out_specs=tuple([HBM] * (2 * n)),
        input_output_aliases={i: i for i in range(2 * n)},
        compiler_params=pltpu.CompilerParams(has_side_effects=EFFECT),
    )(*bufs, *pending["sems"], after)
    return out[n:]


ROWS_PER_COPY = 256


def _pad_rows(front, body_rows, nseq, name):
    tail = LP - L_REAL
    nblk = SEQ // ROWS_PER_COPY

    def body(f_ref, b_ref, o_ref, z_ref, sems):
        s, i = pl.program_id(0), pl.program_id(1)
        rows = pltpu.make_async_copy(b_ref, o_ref.at[pl.ds(s, 1), pl.ds(N_META + i * ROWS_PER_COPY, ROWS_PER_COPY)], sems.at[0])
        rows.start()

        @pl.when(i == 0)
        def _():
            z_ref[...] = jnp.zeros_like(z_ref)
            head = pltpu.make_async_copy(f_ref, o_ref.at[s, pl.ds(0, N_META)], sems.at[1])
            zeros = pltpu.make_async_copy(z_ref, o_ref.at[s, pl.ds(L_REAL, tail)], sems.at[2])
            head.start()
            zeros.start()
            head.wait()
            zeros.wait()

        rows.wait()

    return pl.pallas_call(
        body, name=name, out_shape=jax.ShapeDtypeStruct((nseq, LP, D), F32), grid=(nseq, nblk),
        in_specs=[pl.BlockSpec((N_META, D), lambda s, i: (0, 0)), pl.BlockSpec((1, ROWS_PER_COPY, D), lambda s, i: (s, i, 0))],
        out_specs=ANY,
        scratch_shapes=[pltpu.VMEM((tail, D), F32), pltpu.SemaphoreType.DMA((3,))],
        compiler_params=_cparams("arbitrary", "arbitrary"))(front, body_rows)


def _real_rows(h, nseq, name):
    nblk = SEQ // ROWS_PER_COPY

    def body(h_ref, o_ref, sem):
        s, i = pl.program_id(0), pl.program_id(1)
        rows = pltpu.make_async_copy(h_ref.at[pl.ds(s, 1), pl.ds(N_META + i * ROWS_PER_COPY, ROWS_PER_COPY)], o_ref, sem)
        rows.start()
        rows.wait()

    return pl.pallas_call(
        body, name=name, out_shape=jax.ShapeDtypeStruct((nseq, SEQ, D), F32), grid=(nseq, nblk),
        in_specs=[ANY], out_specs=pl.BlockSpec((1, ROWS_PER_COPY, D), lambda s, i: (s, i, 0)),
        scratch_shapes=[pltpu.SemaphoreType.DMA],
        compiler_params=_cparams("arbitrary", "arbitrary"))(h)


def _plan_cols(n_q, n_dcols, src_of):
    plan = {}
    for q in range(n_q):
        for dblk in range(n_dcols // LANES):
            segs, key, start = [], None, 0
            for lane in range(LANES + 1):
                new = None
                if lane < LANES:
                    src = src_of(q, dblk * LANES + lane)
                    if src is not None:
                        new = (src[0], src[1] // LANES, (lane - src[1] % LANES) % LANES)
                if new != key:
                    if key is not None:
                        segs.append((*key, start, lane))
                    key, start = new, lane
            plan[(q, dblk)] = segs
    return plan


def _relayout(src, n_q, n_dcols, src_of, out_dtype, tr, name):
    n_p, rows, scols = src.shape
    plan = _plan_cols(n_q, n_dcols, src_of)

    def body(s_ref, d_ref):
        lane = lax.broadcasted_iota(jnp.int32, (tr, LANES), 1)
        for (q, dblk), segs in plan.items():
            acc = jnp.zeros((tr, LANES), F32)
            for p, sblk, rot, lo, hi in segs:
                x = s_ref[p, :, sblk * LANES:(sblk + 1) * LANES].astype(F32)
                if rot:
                    x = pltpu.roll(x, rot, 1)
                acc = x if (lo, hi) == (0, LANES) else jnp.where((lane >= lo) & (lane < hi), x, acc)
            d_ref[q, :, dblk * LANES:(dblk + 1) * LANES] = acc.astype(out_dtype)

    return pl.pallas_call(
        body, name=name, out_shape=jax.ShapeDtypeStruct((n_q, rows, n_dcols), out_dtype), grid=(rows // tr,),
        in_specs=[pl.BlockSpec((n_p, tr, scols), lambda i: (0, i, 0))],
        out_specs=pl.BlockSpec((n_q, tr, n_dcols), lambda i: (0, i, 0)),
        compiler_params=_cparams("parallel"))(src)


def _in_padded_to_orig(d):
    if d < QKV:
        kind, r = divmod(d, 4 * PAIR_W)
        pair, r = divmod(r, PAIR_W)
        part, r = divmod(r, LANES)
        return kind * 3 * W_ATT + part * W_ATT + pair * LANES + r
    if d < F_COL:
        return d + NH
    if d < F_COL + NH:
        return d - 2 * D
    return None


_IN_ORIG_TO_PADDED = {_in_padded_to_orig(d): d for d in range(IN_P) if _in_padded_to_orig(d) is not None}


def _up_inter_to_orig(d):
    j, r = divmod(d, 2 * FFC)
    part, r = divmod(r, FFC)
    return part * D_FF + j * FFC + r


_UP_ORIG_TO_INTER = {_up_inter_to_orig(d): d for d in range(2 * D_FF)}
IN_SHARD = IN_COLS // N_DEV
UP_SHARD = 2 * D_FF // N_DEV
SHARD_P = 768
ATT_SHARD = D // N_DEV


def _gathered_to_full(n_shard, to_orig):
    def src_of(q, d):
        c = to_orig(d)
        return None if c is None else (c // n_shard, c % n_shard)
    return src_of


def _full_to_shards(n_shard, from_orig):
    def src_of(q, d):
        return (0, from_orig(q * n_shard + d)) if d < n_shard else None
    return src_of


def _matmul(a, b, *, out_dtype, tm, tn, tk, ta=False, tb=False, after=None, name):
    if ta:
        kdim, m = a.shape
    else:
        m, kdim = a.shape
    n = b.shape[0] if tb else b.shape[1]
    assert m % tm == 0 and n % tn == 0 and kdim % tk == 0, (name, a.shape, b.shape, tm, tn, tk)
    nk = kdim // tk

    def body(a_ref, b_ref, *rest):
        o_ref, scratch = rest[len(extra)], rest[len(extra) + 1:]
        av, bv = a_ref[...], b_ref[...]
        if ta:
            p = lax.dot_general(av, bv, (((0,), (0,)), ((), ())), preferred_element_type=F32)
        elif tb:
            p = lax.dot_general(av, bv, (((1,), (1,)), ((), ())), preferred_element_type=F32)
        else:
            p = jnp.dot(av, bv, preferred_element_type=F32)
        if nk == 1:
            o_ref[...] = p.astype(o_ref.dtype)
        else:
            acc_ref, = scratch
            k = pl.program_id(2)

            @pl.when(k == 0)
            def _():
                acc_ref[...] = p

            @pl.when(k > 0)
            def _():
                acc_ref[...] += p

            @pl.when(k == nk - 1)
            def _():
                o_ref[...] = acc_ref[...].astype(o_ref.dtype)

    extra = [] if after is None else [after]
    a_spec = pl.BlockSpec((tk, tm), lambda i, j, k: (k, i)) if ta else pl.BlockSpec((tm, tk), lambda i, j, k: (i, k))
    b_spec = pl.BlockSpec((tn, tk), lambda i, j, k: (j, k)) if tb else pl.BlockSpec((tk, tn), lambda i, j, k: (k, j))
    return pl.pallas_call(
        body, name=name,
        out_shape=jax.ShapeDtypeStruct((m, n), out_dtype),
        grid=(m // tm, n // tn, nk),
        in_specs=[a_spec, b_spec] + [ANY] * len(extra),
        out_specs=pl.BlockSpec((tm, tn), lambda i, j, k: (i, j)),
        scratch_shapes=[] if nk == 1 else [pltpu.VMEM((tm, tn), F32)],
        compiler_params=_cparams("parallel", "parallel", "arbitrary"),
    )(a, b, *extra)


TR = 288


def _rms(h):
    return lax.rsqrt(jnp.mean(h * h, axis=-1, keepdims=True) + RMS_EPS)


def _norm_fwd(h, g, name):
    t = h.shape[0]
    row = pl.BlockSpec((TR, D), lambda i: (i, 0))

    def body(h_ref, g_ref, n_ref):
        hv = h_ref[...]
        n_ref[...] = ((hv * _rms(hv)) * g_ref[...]).astype(BF16)

    return pl.pallas_call(
        body, name=name, out_shape=jax.ShapeDtypeStruct((t, D), BF16), grid=(t // TR,),
        in_specs=[row, pl.BlockSpec((1, D), lambda i: (0, 0))], out_specs=row, compiler_params=_cparams("parallel"))(h, g)


EPI_ROWS = 144


def _matmul_rows(a, b, rows_in, vecs_in, epilogue, row_outs, sum_outs, *, tm, tk, tb=False, after=None, name):
    m, kdim = a.shape
    assert (b.shape[0] if tb else b.shape[1]) == D and m % tm == 0 and kdim % tk == 0 and tm % EPI_ROWS == 0
    nk = kdim // tk
    n_r, n_v, n_ro, n_so = len(rows_in), len(vecs_in), len(row_outs), len(sum_outs)
    extra = [] if after is None else [after]

    def body(a_ref, b_ref, *rest):
        r_refs, v_refs = rest[:n_r], rest[n_r:n_r + n_v]
        outs = rest[n_r + n_v + len(extra):]
        ro_refs, so_refs, acc_ref = outs[:n_ro], outs[n_ro:n_ro + n_so], outs[n_ro + n_so]
        i, k = pl.program_id(0), pl.program_id(1)
        if tb:
            p = lax.dot_general(a_ref[...], b_ref[...], (((1,), (1,)), ((), ())), preferred_element_type=F32)
        else:
            p = jnp.dot(a_ref[...], b_ref[...], preferred_element_type=F32)

        @pl.when(k == 0)
        def _():
            acc_ref[...] = p

        @pl.when(k > 0)
        def _():
            acc_ref[...] += p

        @pl.when(k == nk - 1)
        def _():
            vecs = [v[...] for v in v_refs]

            def step(c, sums):
                rows = pl.ds(pl.multiple_of(c * EPI_ROWS, 8), EPI_ROWS)
                tiles, terms = epilogue(i * tm + c * EPI_ROWS, acc_ref[rows, :], *[r[rows, :] for r in r_refs], *vecs)
                for o, tile in zip(ro_refs, tiles):
                    o[rows, :] = tile.astype(o.dtype)
                return tuple(s + term for s, term in zip(sums, terms))

            sums = lax.fori_loop(0, tm // EPI_ROWS, step, tuple(jnp.zeros(s, F32) for s in sum_outs))

            @pl.when(i == 0)
            def _():
                for o in so_refs:
                    o[...] = jnp.zeros_like(o)

            for o, s in zip(so_refs, sums):
                o[...] += s

    row = pl.BlockSpec((tm, D), lambda i, k: (i, 0))
    b_spec = pl.BlockSpec((D, tk), lambda i, k: (0, k)) if tb else pl.BlockSpec((tk, D), lambda i, k: (k, 0))
    return pl.pallas_call(
        body, name=name,
        out_shape=tuple([jax.ShapeDtypeStruct((m, D), dt) for dt in row_outs] + [jax.ShapeDtypeStruct(s, F32) for s in sum_outs]),
        grid=(m // tm, nk),
        in_specs=[pl.BlockSpec((tm, tk), lambda i, k: (i, k)), b_spec] + [row] * n_r
        + [pl.BlockSpec((1, D), lambda i, k: (0, 0))] * n_v + [ANY] * len(extra),
        out_specs=tuple([row] * n_ro + [pl.BlockSpec(s, lambda i, k: (0, 0)) for s in sum_outs]),
        scratch_shapes=[pltpu.VMEM((tm, D), F32)],
        compiler_params=_cparams("arbitrary", "arbitrary"))(a, b, *rows_in, *vecs_in, *extra)


def _residual_norm(row0, acc, h, g):
    hv = h + acc
    return (hv, (hv * _rms(hv)) * g), ()


def _rms_bwd_math(hv, dn, gv):
    r = _rms(hv)
    hr = hv * r
    dng = dn * gv
    dh = r * (dng - hr * jnp.mean(dng * hr, axis=-1, keepdims=True))
    return dh, dn * hr


def _loss_head(row0, acc, h1, tgt, g):
    hv = h1 + acc
    hr = hv * _rms(hv)
    pos = row0 % LP + lax.broadcasted_iota(jnp.int32, (EPI_ROWS, 1), 0)
    valid = (pos >= N_META) & (pos < L_REAL)
    err = jnp.where(valid, hr * g - tgt, 0.0)
    part = 0.5 * jnp.sum(jnp.mean(err * err, axis=-1, keepdims=True))
    dy = err * (1.0 / D)
    dh, dgrow = _rms_bwd_math(hv, dy, g)
    return (dh, dh), (jnp.full((8, LANES), part, F32), jnp.sum(dgrow, axis=0, keepdims=True))


def _residual_norm_bwd(row0, acc, h, dres, g):
    dh, dgrow = _rms_bwd_math(h, acc, g)
    dh = dh + dres
    return (dh, dh), (jnp.sum(dgrow, axis=0, keepdims=True),)


def _residual_norm_bwd_f32(row0, acc, h, dres, g):
    tiles, sums = _residual_norm_bwd(row0, acc, h, dres, g)
    return tiles[:1], sums


GATE_BLK = GATE_COL // D


def _sigmoid(x):
    return 1.0 / (1.0 + jnp.exp(-x))


def _merge_fwd(p_sb, p_fx, proj, name):
    t = p_sb.shape[0]
    row = pl.BlockSpec((TR, D), lambda i: (i, 0))

    def body(ps_ref, pf_ref, gs_ref, gf_ref, o_ref):
        o_ref[...] = (_sigmoid(gs_ref[...]) * ps_ref[...] + _sigmoid(gf_ref[...]) * pf_ref[...]).astype(BF16)

    return pl.pallas_call(
        body, name=name, out_shape=jax.ShapeDtypeStruct((t, D), BF16), grid=(t // TR,),
        in_specs=[row, row, pl.BlockSpec((TR, D), lambda i: (i, GATE_BLK)),
                  pl.BlockSpec((TR, D), lambda i: (i, GATE_BLK + 1))],
        out_specs=row, compiler_params=_cparams("parallel"))(p_sb, p_fx, proj, proj)


def _merge_bwd(dm, p, proj, dproj, which, name):
    t = dm.shape[0]
    row = pl.BlockSpec((TR, D), lambda i: (i, 0))
    gate = pl.BlockSpec((TR, D), lambda i: (i, GATE_BLK + which))

    def body(dm_ref, p_ref, g_ref, *rest):
        dp_ref, dg_ref = rest[-2:]
        dmv = dm_ref[...]
        s = _sigmoid(g_ref[...])
        dp_ref[...] = (dmv * s).astype(BF16)
        dg_ref[...] = (dmv * p_ref[...] * s * (1.0 - s)).astype(BF16)

    out_shape = (jax.ShapeDtypeStruct((t, D), BF16), jax.ShapeDtypeStruct((t, IN_P), BF16))
    if dproj is None:
        return pl.pallas_call(
            body, name=name, out_shape=out_shape, grid=(t // TR,), in_specs=[row, row, gate],
            out_specs=(row, gate), compiler_params=_cparams("parallel"))(dm, p, proj)
    return pl.pallas_call(
        body, name=name, out_shape=out_shape, grid=(t // TR,), in_specs=[row, row, gate, ANY],
        out_specs=(row, gate), input_output_aliases={3: 1}, compiler_params=_cparams("parallel"))(dm, p, proj, dproj)


CH = 288


def _chunk(c, n=CH):
    return pl.ds(pl.multiple_of(c * CH, 8), n)


def _conv_taps(u_ref, c):
    x = u_ref[_chunk(c), :]
    prev = u_ref[pl.ds(pl.multiple_of(jnp.maximum(c * CH - 8, 0), 8), 8), :]
    xx = jnp.concatenate([jnp.where(c == 0, 0.0, prev), x], axis=0)
    return x, pltpu.roll(xx, 1, 0)[8:], pltpu.roll(xx, 2, 0)[8:]


def _conv_glu_fwd(u, cw, nseq, name):
    nblk = D_FF // FFC

    def body(u_ref, cw_ref, o_ref):
        cwv = cw_ref[...]

        def step(c, _):
            x, x1, x2 = _conv_taps(u_ref, c)
            uc = cwv[0:1, :] * x2 + cwv[1:2, :] * x1 + cwv[2:3, :] * x
            a, b = uc[:, :FFC], uc[:, FFC:]
            o_ref[_chunk(c), :] = (a * _sigmoid(a) * b).astype(BF16)
            return 0

        lax.fori_loop(0, LP // CH, step, 0)

    return pl.pallas_call(
        body, name=name, out_shape=jax.ShapeDtypeStruct((nseq * LP, D_FF), BF16), grid=(nseq, nblk),
        in_specs=[pl.BlockSpec((LP, 2 * FFC), lambda s, j: (s, j)), pl.BlockSpec((3, 2 * FFC), lambda s, j: (0, j))],
        out_specs=pl.BlockSpec((LP, FFC), lambda s, j: (s, j)),
        compiler_params=_cparams("parallel", "parallel"))(u, cw)


def _conv_glu_bwd(u, cw, dact, nseq, name):
    nblk = D_FF // FFC
    nch = LP // CH

    def body(u_ref, cw_ref, da_ref, du_ref, dcw_ref):
        s = pl.program_id(1)
        cwv = cw_ref[...]

        def step(k, carry):
            nxt, p0, p1, p2 = carry
            c = nch - 1 - k
            x, x1, x2 = _conv_taps(u_ref, c)
            uc = cwv[0:1, :] * x2 + cwv[1:2, :] * x1 + cwv[2:3, :] * x
            a, b = uc[:, :FFC], uc[:, FFC:]
            sa = _sigmoid(a)
            dactv = da_ref[_chunk(c), :]
            da = dactv * b * (sa * (1.0 + a * (1.0 - sa)))
            db = dactv * (a * sa)
            duc = jnp.concatenate([da, db], axis=1)
            dd = jnp.concatenate([duc, nxt], axis=0)
            du = (cwv[2:3, :] * duc + cwv[1:2, :] * pltpu.roll(dd, CH + 7, 0)[:CH]
                  + cwv[0:1, :] * pltpu.roll(dd, CH + 6, 0)[:CH])
            du_ref[_chunk(c), :] = du.astype(BF16)
            return (duc[:8], p0 + jnp.sum(duc * x2, axis=0, keepdims=True),
                    p1 + jnp.sum(duc * x1, axis=0, keepdims=True), p2 + jnp.sum(duc * x, axis=0, keepdims=True))

        zrow = jnp.zeros((1, 2 * FFC), F32)
        _, p0, p1, p2 = lax.fori_loop(0, nch, step, (jnp.zeros((8, 2 * FFC), F32), zrow, zrow, zrow))

        @pl.when(s == 0)
        def _():
            dcw_ref[...] = jnp.zeros_like(dcw_ref)

        dcw_ref[...] += jnp.concatenate([p0, p1, p2], axis=0)

    return pl.pallas_call(
        body, name=name,
        out_shape=(jax.ShapeDtypeStruct((nseq * LP, 2 * D_FF), BF16), jax.ShapeDtypeStruct((3, 2 * D_FF), F32)),
        grid=(nblk, nseq),
        in_specs=[pl.BlockSpec((LP, 2 * FFC), lambda j, s: (s, j)), pl.BlockSpec((3, 2 * FFC), lambda j, s: (0, j)),
                  pl.BlockSpec((LP, FFC), lambda j, s: (s, j))],
        out_specs=(pl.BlockSpec((LP, 2 * FFC), lambda j, s: (s, j)), pl.BlockSpec((3, 2 * FFC), lambda j, s: (0, j))),
        compiler_params=_cparams("parallel", "arbitrary"))(u, cw, dact)


F_BLK = F_COL // LANES
CB = 128


def _split3(x):
    hi = x.astype(BF16)
    r1 = x - hi.astype(F32)
    mid = r1.astype(BF16)
    lo = (r1 - mid.astype(F32)).astype(BF16)
    return hi, mid, lo


def _tri_dot(tri, x):
    hi, mid, lo = _split3(x)
    d = functools.partial(jnp.dot, preferred_element_type=F32)
    return d(tri, hi) + d(tri, mid) + d(tri, lo)


def _log_sigmoid(x):
    return jnp.minimum(x, 0.0) - jnp.log(1.0 + jnp.exp(-jnp.abs(x)))


def _gate_fwd(proj, bf, nseq, name):
    def body(f_ref, b_ref, c_ref):
        r_i = lax.broadcasted_iota(jnp.int32, (CB, CB), 0)
        c_i = lax.broadcasted_iota(jnp.int32, (CB, CB), 1)
        tri = (c_i <= r_i).astype(BF16)
        bv = b_ref[...]

        def step(k, carry):
            rows = pl.ds(pl.multiple_of(k * CB, CB), CB)
            lf = _log_sigmoid(f_ref[rows, :] + bv)
            c_ref[rows, :] = _tri_dot(tri, lf) + carry
            return carry + jnp.sum(lf, axis=0, keepdims=True)

        lax.fori_loop(0, LP // CB, step, jnp.zeros((1, LANES), F32))

    return pl.pallas_call(
        body, name=name, out_shape=jax.ShapeDtypeStruct((nseq * LP, LANES), F32), grid=(nseq,),
        in_specs=[pl.BlockSpec((LP, LANES), lambda s: (s, F_BLK)), pl.BlockSpec((1, LANES), lambda s: (0, 0))],
        out_specs=pl.BlockSpec((LP, LANES), lambda s: (s, 0)),
        compiler_params=_cparams("parallel"))(proj, bf)


def _gate_bwd(proj, bf, dc, dproj, nseq, name):
    def body(f_ref, b_ref, dc_ref, _, df_ref, db_ref):
        s = pl.program_id(0)
        r_i = lax.broadcasted_iota(jnp.int32, (CB, CB), 0)
        c_i = lax.broadcasted_iota(jnp.int32, (CB, CB), 1)
        tri = (c_i >= r_i).astype(BF16)
        bv = b_ref[...]

        def step(kk, carry):
            carry_c, carry_b = carry
            k = LP // CB - 1 - kk
            rows = pl.ds(pl.multiple_of(k * CB, CB), CB)
            dcv = dc_ref[rows, :]
            dlf = _tri_dot(tri, dcv) + carry_c
            df = dlf * _sigmoid(-(f_ref[rows, :] + bv))
            df_ref[rows, :] = jnp.concatenate([df, jnp.zeros_like(df)], axis=1).astype(BF16)
            return carry_c + jnp.sum(dcv, axis=0, keepdims=True), carry_b + jnp.sum(df, axis=0, keepdims=True)

        zero = jnp.zeros((1, LANES), F32)
        _, dbp = lax.fori_loop(0, LP // CB, step, (zero, zero))

        @pl.when(s == 0)
        def _():
            db_ref[...] = jnp.zeros_like(db_ref)

        db_ref[...] += dbp

    return pl.pallas_call(
        body, name=name,
        out_shape=(jax.ShapeDtypeStruct(dproj.shape, BF16), jax.ShapeDtypeStruct((1, LANES), F32)), grid=(nseq,),
        in_specs=[pl.BlockSpec((LP, LANES), lambda s: (s, F_BLK)), pl.BlockSpec((1, LANES), lambda s: (0, 0)),
                  pl.BlockSpec((LP, LANES), lambda s: (s, 0)), ANY],
        out_specs=(pl.BlockSpec((LP, 2 * LANES), lambda s: (s, F_COL // (2 * LANES))), pl.BlockSpec((1, LANES), lambda s: (0, 0))),
        input_output_aliases={3: 0},
        compiler_params=_cparams("arbitrary"))(proj, bf, dc, dproj)


SCALE = 0.125
NEG = -1e30


def _dot_nt(a, b):
    return lax.dot_general(a, b, (((1,), (1,)), ((), ())), preferred_element_type=F32)


def _dot_tn(a, b):
    return lax.dot_general(a, b, (((0,), (0,)), ((), ())), preferred_element_type=F32)


def _dot(a, b):
    return jnp.dot(a, b, preferred_element_type=F32)


def _blk(i):
    return pl.ds(pl.multiple_of(i * BQ, BQ), BQ)


def _tile_iotas():
    return lax.broadcasted_iota(jnp.int32, (BQ, BQ), 0), lax.broadcasted_iota(jnp.int32, (BQ, BQ), 1)


def _lane_iota():
    return lax.broadcasted_iota(jnp.int32, (BQ, LANES), 1)


def _head_masks():
    lane = _lane_iota()
    return lane < HEAD, lane >= HEAD


def _only(mask, x):
    return jnp.where(mask, x, jnp.zeros_like(x))


def _pick_lane(x, idx):
    return jnp.sum(jnp.where(_lane_iota() == idx, x, 0.0), axis=1, keepdims=True)


def _chains(npair):
    return [(pp, h) for pp in range(npair) for h in range(2)]


def _load_qkv(p_ref, q_s, k_s, v_s):
    for pp in range(q_s.shape[0]):
        base = pp * PAIR_W
        q_s[pp] = (p_ref[:, base:base + LANES] * SCALE).astype(BF16)
        k_s[pp] = p_ref[:, base + LANES:base + 2 * LANES].astype(BF16)
        v_s[pp] = p_ref[:, base + 2 * LANES:base + 3 * LANES].astype(BF16)


def _softplus(z):
    return jnp.maximum(z, 0.0) + jnp.log(1.0 + jnp.exp(-jnp.abs(z)))


def _hi_lo(x):
    hi = x.astype(BF16)
    return hi, (x - hi.astype(F32)).astype(BF16)


def _sb_tile_weights(q, k, strict, r, u_suf):
    n = len(q)
    z = [_dot_nt(q[c], k[c]) for c in range(n)]
    sp = [_softplus(zc) for zc in z]
    lk = [-spc if strict is None else jnp.where(strict, -spc, 0.0) for spc in sp]
    parts = [_hi_lo(lkc) for lkc in lk]
    suf = [_dot(hi, u_suf) + _dot(lo, u_suf) for hi, lo in parts]
    w = [jnp.exp(z[c] - sp[c] + r[c] + suf[c]) for c in range(n)]
    if strict is not None:
        w = [jnp.where(strict, wc, 0.0) for wc in w]
    r_next = [r[c] + suf[c][:, 0:1] + lk[c][:, 0:1] for c in range(n)]
    return w, sp, r_next


def _group_spec(kind, npair):
    return pl.BlockSpec((LP, npair * PAIR_W), lambda s, g: (s, (NH // (2 * npair)) * kind + g))


def _gheads_spec(npair):
    return pl.BlockSpec((LP, npair * LANES), lambda s, g: (s, g))


def _qkv_scratch(npair):
    return [pltpu.VMEM((npair, LP, LANES), BF16)] * 3


SEQ_SPEC = pl.BlockSpec((LP, LANES), lambda s, g: (s, 0))
RS_STRIDE = 16


def _pair_cols(pp):
    return slice(pp * LANES, (pp + 1) * LANES)


def _sb_fwd(proj, nseq, npair, name):
    t = nseq * LP
    chains = _chains(npair)

    def body(p_ref, o_ref, rs_ref, q_s, k_s, v_s, acc_ref, r_ref, rb_ref):
        _load_qkv(p_ref, q_s, k_s, v_s)
        row, col = _tile_iotas()
        u_suf = (row > col).astype(BF16)
        diag = col < row
        lane = _lane_iota()
        heads = _head_masks()

        def qblock(i, _):
            acc_ref[...] = jnp.zeros_like(acc_ref)
            rb_ref[...] = jnp.zeros_like(rb_ref)
            r_ref[...] = jnp.zeros_like(r_ref)
            qb = [q_s[pp, _blk(i), :] for pp in range(npair)]

            def tile(j, strict):
                kj = [k_s[pp, _blk(j), :] for pp in range(npair)]
                vj = [v_s[pp, _blk(j), :] for pp in range(npair)]
                r = [r_ref[c] for c in range(len(chains))]
                w, _, r_next = _sb_tile_weights([_only(heads[h], qb[pp]) for pp, h in chains],
                                                [kj[pp] for pp, _ in chains], strict, r, u_suf)
                pv = [_dot(w[c].astype(BF16), _only(heads[h], vj[pp])) for c, (pp, h) in enumerate(chains)]
                for pp in range(npair):
                    acc_ref[pp] += pv[2 * pp] + pv[2 * pp + 1]
                    rb_ref[pp] = jnp.where(lane == j, r[2 * pp], jnp.where(lane == RS_STRIDE + j, r[2 * pp + 1], rb_ref[pp]))
                for c in range(len(chains)):
                    r_ref[c] = r_next[c]

            tile(i, diag)

            def kblock(jj, _):
                tile(i - jj, None)
                return 0

            lax.fori_loop(1, i + 1, kblock, 0)
            for pp in range(npair):
                o_ref[_blk(i), _pair_cols(pp)] = acc_ref[pp].astype(BF16)
                rs_ref[_blk(i), _pair_cols(pp)] = rb_ref[pp]
            return 0

        lax.fori_loop(0, NBLK, qblock, 0)

    return pl.pallas_call(
        body, name=name,
        out_shape=(jax.ShapeDtypeStruct((t, W_ATT), BF16), jax.ShapeDtypeStruct((t, W_ATT), F32)),
        grid=(nseq, NH // (2 * npair)), in_specs=[_group_spec(0, npair)], out_specs=(_gheads_spec(npair), _gheads_spec(npair)),
        scratch_shapes=_qkv_scratch(npair) + [pltpu.VMEM((npair, BQ, LANES), F32), pltpu.VMEM((2 * npair, BQ, 1), F32),
                                      pltpu.VMEM((npair, BQ, LANES), F32)],
        compiler_params=_cparams("parallel", "parallel"))(proj)


def _sb_bwd(proj, do, rs, dproj, after, nseq, npair, name):
    chains = _chains(npair)

    def body(p_ref, do_ref, rs_ref, _, _after, dp_ref, q_s, k_s, v_s, dqa_ref, dka_ref, dva_ref, ep_ref):
        _load_qkv(p_ref, q_s, k_s, v_s)
        row, col = _tile_iotas()
        u_suf = (row > col).astype(BF16)
        u_pre = (row < col).astype(BF16)
        diag = col < row
        heads = _head_masks()
        dka_ref[...] = jnp.zeros_like(dka_ref)
        dva_ref[...] = jnp.zeros_like(dva_ref)
        nc = len(chains)

        def qblock(i, _):
            rb = [rs_ref[_blk(i), _pair_cols(pp)] for pp in range(npair)]
            qb = [q_s[pp, _blk(i), :] for pp in range(npair)]
            dob = [do_ref[_blk(i), _pair_cols(pp)] for pp in range(npair)]
            dqa_ref[...] = jnp.zeros_like(dqa_ref)
            ep_ref[...] = jnp.zeros_like(ep_ref)

            def tile(j, strict):
                kj = [k_s[pp, _blk(j), :] for pp in range(npair)]
                vj = [v_s[pp, _blk(j), :] for pp in range(npair)]
                q = [_only(heads[h], qb[pp]) for pp, h in chains]
                dov = [_only(heads[h], dob[pp]) for pp, h in chains]
                r = [_pick_lane(rb[pp], RS_STRIDE * h + j) for pp, h in chains]
                dw = [_dot_nt(dov[c], vj[pp]) for c, (pp, _) in enumerate(chains)]
                w, sp, _ = _sb_tile_weights(q, [kj[pp] for pp, _ in chains], strict, r, u_suf)
                e = [dw[c] * w[c] for c in range(nc)]
                e_pre = [ep_ref[c] + _dot(e[c].astype(BF16), u_pre) for c in range(nc)]
                dz = []
                for c in range(nc):
                    ep_ref[c] += jnp.sum(e[c], axis=1, keepdims=True)
                    sneg = jnp.exp(-sp[c])
                    dzc = e[c] * sneg - (1.0 - sneg) * e_pre[c]
                    if strict is not None:
                        dzc = jnp.where(strict, dzc, 0.0)
                    dz.append(dzc.astype(BF16))
                dq = [_dot(dz[c], _only(heads[h], kj[pp])) for c, (pp, h) in enumerate(chains)]
                dk = [_dot_tn(dz[c], q[c]) for c in range(nc)]
                dv = [_dot_tn(w[c].astype(BF16), dov[c]) for c in range(nc)]
                for pp in range(npair):
                    dqa_ref[pp] += dq[2 * pp] + dq[2 * pp + 1]
                    dka_ref[pp, _blk(j), :] += dk[2 * pp] + dk[2 * pp + 1]
                    dva_ref[pp, _blk(j), :] += dv[2 * pp] + dv[2 * pp + 1]

            def kblock(j, _):
                tile(j, None)
                return 0

            lax.fori_loop(0, i, kblock, 0)
            tile(i, diag)
            for pp in range(npair):
                dp_ref[_blk(i), pp * PAIR_W:pp * PAIR_W + LANES] = (dqa_ref[pp] * SCALE).astype(BF16)
            return 0

        lax.fori_loop(0, NBLK, qblock, 0)
        for pp in range(npair):
            dp_ref[:, pp * PAIR_W + LANES:pp * PAIR_W + 2 * LANES] = dka_ref[pp].astype(BF16)
            dp_ref[:, pp * PAIR_W + 2 * LANES:pp * PAIR_W + 3 * LANES] = dva_ref[pp].astype(BF16)

    return pl.pallas_call(
        body, name=name, out_shape=jax.ShapeDtypeStruct(dproj.shape, BF16), grid=(nseq, NH // (2 * npair)),
        in_specs=[_group_spec(0, npair), _gheads_spec(npair), _gheads_spec(npair), ANY, ANY], out_specs=_group_spec(0, npair),
        input_output_aliases={3: 0},
        scratch_shapes=_qkv_scratch(npair) + [pltpu.VMEM((npair, BQ, LANES), F32), pltpu.VMEM((npair, LP, LANES), F32),
                                      pltpu.VMEM((npair, LP, LANES), F32), pltpu.VMEM((2 * npair, BQ, 1), F32)],
        compiler_params=_cparams("parallel", "parallel"))(proj, do, rs, dproj, after)


CROW_SPEC = pl.BlockSpec((None, NH, LP), lambda s, g: (s, 0, 0))


def _fox_scores(qi, kj, cq, ck, causal):
    z = _dot_nt(qi, kj) + (cq - ck)
    return z if causal is None else jnp.where(causal, z, NEG)


def _key_cols(cr_ref, head, j):
    return cr_ref[pl.ds(head, 1), pl.ds(pl.multiple_of(j * BQ, BQ), BQ)]


def _fox_fwd(proj, c, crow, nseq, npair, name):
    t = nseq * LP
    chains = _chains(npair)

    def body(p_ref, c_ref, cr_ref, o_ref, o32_ref, lse_ref, q_s, k_s, v_s, vt_s, ck_s, acc_ref, m_ref, l_ref):
        _load_qkv(p_ref, q_s, k_s, v_s)
        row, col = _tile_iotas()
        diag = row <= col
        sub = lax.broadcasted_iota(jnp.int32, (LANES, BQ), 0)
        rows_of = (sub < HEAD, sub >= HEAD)
        head0 = 2 * npair * pl.program_id(1)
        nc = len(chains)
        lane_all = lax.broadcasted_iota(jnp.int32, (LP, LANES), 1)
        for pp in range(npair):
            vt_s[pp] = v_s[pp].astype(F32).T.astype(BF16)
        for cidx in range(nc):
            ck_s[cidx] = jnp.sum(jnp.where(lane_all == head0 + cidx, c_ref[...], 0.0), axis=1, keepdims=True)

        def qblock(i, _):
            qt = [q_s[pp, _blk(i), :].astype(F32).T.astype(BF16) for pp in range(npair)]
            qt = [jnp.where(rows_of[h], qt[pp], jnp.zeros_like(qt[pp])) for pp, h in chains]
            cq = [_key_cols(cr_ref, head0 + cidx, i) for cidx in range(nc)]
            acc_ref[...] = jnp.zeros_like(acc_ref)
            m_ref[...] = jnp.full_like(m_ref, NEG)
            l_ref[...] = jnp.zeros_like(l_ref)

            def tile(j, causal):
                keys = pl.ds(pl.multiple_of(j * BQ, BQ), BQ)
                z = [_dot(k_s[pp, _blk(j), :], qt[cidx]) + (cq[cidx] - ck_s[cidx, _blk(j), :])
                     for cidx, (pp, _) in enumerate(chains)]
                if causal is not None:
                    z = [jnp.where(causal, zc, NEG) for zc in z]
                p, alpha = [], []
                for cidx in range(nc):
                    m_old = m_ref[cidx]
                    m_new = jnp.maximum(m_old, jnp.max(z[cidx], axis=0, keepdims=True))
                    alpha.append(jnp.exp(m_old - m_new))
                    pc = jnp.exp(z[cidx] - m_new)
                    l_ref[cidx] = alpha[cidx] * l_ref[cidx] + jnp.sum(pc, axis=0, keepdims=True)
                    m_ref[cidx] = m_new
                    p.append(pc.astype(BF16))
                pv = [_dot(jnp.where(rows_of[h], vt_s[pp, :, keys], jnp.zeros((LANES, BQ), BF16)), p[cidx])
                      for cidx, (pp, h) in enumerate(chains)]
                for cidx in range(nc):
                    acc_ref[cidx] = alpha[cidx] * acc_ref[cidx] + pv[cidx]

            def kblock(j, _):
                tile(j, None)
                return 0

            lax.fori_loop(0, i, kblock, 0)
            tile(i, diag)
            for pp in range(npair):
                out_t = acc_ref[2 * pp] / l_ref[2 * pp] + acc_ref[2 * pp + 1] / l_ref[2 * pp + 1]
                out = out_t.T
                o_ref[_blk(i), _pair_cols(pp)] = out.astype(BF16)
                o32_ref[_blk(i), _pair_cols(pp)] = out
                lse = [m_ref[2 * pp + h] + jnp.log(l_ref[2 * pp + h]) for h in range(2)]
                lse_t = jnp.where(sub == 0, lse[0], jnp.where(sub == 1, lse[1], 0.0))
                lse_ref[_blk(i), _pair_cols(pp)] = lse_t.T
            return 0

        lax.fori_loop(0, NBLK, qblock, 0)

    return pl.pallas_call(
        body, name=name,
        out_shape=(jax.ShapeDtypeStruct((t, W_ATT), BF16), jax.ShapeDtypeStruct((t, W_ATT), F32),
                   jax.ShapeDtypeStruct((t, W_ATT), F32)),
        grid=(nseq, NH // (2 * npair)), in_specs=[_group_spec(1, npair), SEQ_SPEC, CROW_SPEC], out_specs=(_gheads_spec(npair), _gheads_spec(npair), _gheads_spec(npair)),
        scratch_shapes=_qkv_scratch(npair) + [pltpu.VMEM((npair, LANES, LP), BF16), pltpu.VMEM((2 * npair, LP, 1), F32),
                                      pltpu.VMEM((2 * npair, LANES, BQ), F32), pltpu.VMEM((2 * npair, 1, BQ), F32),
                                      pltpu.VMEM((2 * npair, 1, BQ), F32)],
        compiler_params=_cparams("parallel", "parallel"))(proj, c, crow)


def _fox_bwd(proj, c, crow, o32, lse, do, dproj, nseq, npair, name):
    t = nseq * LP
    chains = _chains(npair)

    def body(p_ref, c_ref, cr_ref, o_ref, lse_ref, do_ref, _, dp_ref, dck_ref, dcq_ref,
             q_s, k_s, v_s, dqa_ref, dka_ref, dva_ref, rsum_ref):
        _load_qkv(p_ref, q_s, k_s, v_s)
        row, col = _tile_iotas()
        diag = col <= row
        lane = _lane_iota()
        heads = _head_masks()
        sub = lax.broadcasted_iota(jnp.int32, (NH, BQ), 0)
        group = pl.program_id(1)
        head0 = 2 * npair * group
        nc = len(chains)
        dka_ref[...] = jnp.zeros_like(dka_ref)
        dva_ref[...] = jnp.zeros_like(dva_ref)

        @pl.when(group == 0)
        def _():
            dck_ref[...] = jnp.zeros_like(dck_ref)
            dcq_ref[...] = jnp.zeros_like(dcq_ref)

        def qblock(i, _):
            dqa_ref[...] = jnp.zeros_like(dqa_ref)
            rsum_ref[...] = jnp.zeros_like(rsum_ref)
            cblk = c_ref[_blk(i), :]
            qb = [q_s[pp, _blk(i), :] for pp in range(npair)]
            dob = [do_ref[_blk(i), _pair_cols(pp)] for pp in range(npair)]
            prod = [dob[pp].astype(F32) * o_ref[_blk(i), _pair_cols(pp)] for pp in range(npair)]
            cq = [_pick_lane(cblk, head0 + c) for c in range(nc)]
            lse_i = [_pick_lane(lse_ref[_blk(i), _pair_cols(pp)], h) for pp, h in chains]
            delta = [jnp.sum(_only(heads[h], prod[pp]), axis=1, keepdims=True) for pp, h in chains]

            def tile(j, causal):
                kj = [k_s[pp, _blk(j), :] for pp in range(npair)]
                vj = [v_s[pp, _blk(j), :] for pp in range(npair)]
                keys = pl.ds(pl.multiple_of(j * BQ, BQ), BQ)
                q = [_only(heads[h], qb[pp]) for pp, h in chains]
                dov = [_only(heads[h], dob[pp]) for pp, h in chains]
                z = [_fox_scores(q[c], kj[pp], cq[c], _key_cols(cr_ref, head0 + c, j), causal)
                     for c, (pp, _) in enumerate(chains)]
                dpv = [_dot_nt(dov[c], vj[pp]) for c, (pp, _) in enumerate(chains)]
                p = [jnp.exp(z[c] - lse_i[c]) for c in range(nc)]
                ds = [p[c] * (dpv[c] - delta[c]) for c in range(nc)]
                dsb = [d.astype(BF16) for d in ds]
                dq = [_dot(dsb[c], _only(heads[h], kj[pp])) for c, (pp, h) in enumerate(chains)]
                dk = [_dot_tn(dsb[c], q[c]) for c in range(nc)]
                dv = [_dot_tn(p[c].astype(BF16), dov[c]) for c in range(nc)]
                for pp in range(npair):
                    dqa_ref[pp] += dq[2 * pp] + dq[2 * pp + 1]
                    dka_ref[pp, _blk(j), :] += dk[2 * pp] + dk[2 * pp + 1]
                    dva_ref[pp, _blk(j), :] += dv[2 * pp] + dv[2 * pp + 1]
                col_sums = jnp.zeros((NH, BQ), F32)
                for c in range(nc):
                    col_sums = col_sums + jnp.where(sub == head0 + c, jnp.sum(ds[c], axis=0, keepdims=True), 0.0)
                    rsum_ref[c] += jnp.sum(ds[c], axis=1, keepdims=True)
                dck_ref[:, keys] = dck_ref[:, keys] - col_sums

            def kblock(j, _):
                tile(j, None)
                return 0

            lax.fori_loop(0, i, kblock, 0)
            tile(i, diag)
            row_sums = jnp.zeros((BQ, LANES), F32)
            for c in range(nc):
                row_sums = row_sums + jnp.where(lane == head0 + c, rsum_ref[c], 0.0)
            dcq_ref[_blk(i), :] += row_sums
            for pp in range(npair):
                dp_ref[_blk(i), pp * PAIR_W:pp * PAIR_W + LANES] = (dqa_ref[pp] * SCALE).astype(BF16)
            return 0

        lax.fori_loop(0, NBLK, qblock, 0)
        for pp in range(npair):
            dp_ref[:, pp * PAIR_W + LANES:pp * PAIR_W + 2 * LANES] = dka_ref[pp].astype(BF16)
            dp_ref[:, pp * PAIR_W + 2 * LANES:pp * PAIR_W + 3 * LANES] = dva_ref[pp].astype(BF16)

    return pl.pallas_call(
        body, name=name,
        out_shape=(jax.ShapeDtypeStruct(dproj.shape, BF16), jax.ShapeDtypeStruct((nseq, NH, LP), F32),
                   jax.ShapeDtypeStruct((t, LANES), F32)),
        grid=(nseq, NH // (2 * npair)),
        in_specs=[_group_spec(1, npair), SEQ_SPEC, CROW_SPEC, _gheads_spec(npair), _gheads_spec(npair), _gheads_spec(npair), ANY],
        out_specs=(_group_spec(1, npair), CROW_SPEC, SEQ_SPEC),
        input_output_aliases={6: 0},
        scratch_shapes=_qkv_scratch(npair) + [pltpu.VMEM((npair, BQ, LANES), F32), pltpu.VMEM((npair, LP, LANES), F32),
                                      pltpu.VMEM((npair, LP, LANES), F32), pltpu.VMEM((2 * npair, BQ, 1), F32)],
        compiler_params=_cparams("parallel", "arbitrary"))(proj, c, crow, o32, lse, do, dproj)


def _adamw_math(w, g, m, v):
    m = B1 * m + (1.0 - B1) * g
    v = B2 * v + (1.0 - B2) * (g * g)
    m_hat = m / (1.0 - B1 ** STEP)
    v_hat = v / (1.0 - B2 ** STEP)
    delta = -LR * (m_hat / (jnp.sqrt(v_hat) + EPS) + WD * w)
    return delta, m, v


def _sum_adamw(parts, w, m, v, tr, name):
    rows, cols = w.shape
    cp = parts.shape[2]
    assert rows % tr == 0 and parts.shape[1] == rows

    def body(p_ref, w_ref, m_ref, v_ref, g_ref, d_ref, nm_ref, nv_ref):
        gsum = p_ref[0].astype(F32)
        for s in range(1, N_DEV):
            gsum = gsum + p_ref[s].astype(F32)
        gsum = gsum[:, :cols]
        d, nm, nv = _adamw_math(w_ref[...], gsum, m_ref[...], v_ref[...])
        g_ref[...] = gsum
        d_ref[...] = d
        nm_ref[...] = nm
        nv_ref[...] = nv

    blk = pl.BlockSpec((tr, cols), lambda i: (i, 0))
    out = jax.ShapeDtypeStruct((rows, cols), F32)
    return pl.pallas_call(
        body, name=name, out_shape=(out, out, out, out), grid=(rows // tr,),
        in_specs=[pl.BlockSpec((N_DEV, tr, cp), lambda i: (0, i, 0)), blk, blk, blk],
        out_specs=(blk, blk, blk, blk), compiler_params=_cparams("parallel"))(parts, w, m, v)


def _local_step(x, tgt, meta, tgt_front, g_mix, b_forget, g_ffn, g_final, first_weights, late_weights, early_grads, last_grad):
    nseq = x.shape[0]
    t = nseq * LP
    tm = LP // 2
    mm = functools.partial(_matmul, tm=tm)

    h0 = _pad_rows(meta, x, nseq, "pad_x").reshape(t, D)
    tgt_p = _pad_rows(tgt_front, tgt, nseq, "pad_target").reshape(t, D)
    bf = jnp.pad(b_forget.reshape(1, NH), ((0, 0), (0, LANES - NH)))

    n1 = _norm_fwd(h0, g_mix, "norm1")
    w_in_p, started = first_weights(n1)
    proj = mm(n1, w_in_p, out_dtype=F32, tn=1792, tk=D, after=started, name="in_proj")
    c = _gate_fwd(proj, bf, nseq, "gate_fwd")
    crow = c[:, :NH].reshape(nseq, LP, NH).transpose(0, 2, 1)
    o_sb, rs = _sb_fwd(proj, nseq, 2, "sb_fwd")
    o_fx, o_fx32, lse = _fox_fwd(proj, c, crow, nseq, 2, "fox_fwd")
    w_bsb, w_bfx, w_out, w_up_i, cw_i, w_down = late_weights(o_fx)
    p_sb = mm(o_sb, w_bsb, out_dtype=F32, tn=D, tk=W_ATT, name="branch_sb")
    p_fx = mm(o_fx, w_bfx, out_dtype=F32, tn=D, tk=W_ATT, name="branch_fox")
    merged = _merge_fwd(p_sb, p_fx, proj, "merge_fwd")
    rows = functools.partial(_matmul_rows, tm=LP // 4)
    h1, n2 = rows(merged, w_out, [h0], [g_ffn], _residual_norm, [F32, BF16], [], tk=D, name="out_proj_norm2")
    u = mm(n2, w_up_i, out_dtype=F32, tn=1408, tk=D, name="up_proj")
    act = _conv_glu_fwd(u, cw_i, nseq, "conv_glu_fwd")

    dh2, dh2b, loss, dg_final = rows(act, w_down, [h1, tgt_p], [g_final], _loss_head, [F32, BF16],
                                     [(8, LANES), (1, D)], tk=D_FF, name="down_proj_loss")
    d_down = _matmul(act, dh2b, out_dtype=BF16, tm=1408, tn=D, tk=LP, ta=True, name="d_w_down")
    dact = mm(dh2b, w_down, out_dtype=F32, tn=1408, tk=D, tb=True, name="d_act")
    du, d_cw = _conv_glu_bwd(u, cw_i, dact, nseq, "conv_glu_bwd")
    d_up = _matmul(n2, du, out_dtype=BF16, tm=D, tn=1408, tk=LP, ta=True, name="d_w_up")
    dh1, dh1b, dg_ffn = rows(du, w_up_i, [h1, dh2], [g_ffn], _residual_norm_bwd, [F32, BF16], [(1, D)],
                             tk=D_FF, tb=True, name="d_n2_norm2_bwd")
    d_out = _matmul(merged, dh1b, out_dtype=BF16, tm=D, tn=D, tk=LP, ta=True, name="d_w_out")
    dmerged = mm(dh1b, w_out, out_dtype=F32, tn=D, tk=D, tb=True, name="d_merged")
    dp_sb, dproj = _merge_bwd(dmerged, p_sb, proj, None, 0, "merge_bwd_sb")
    dp_fx, dproj = _merge_bwd(dmerged, p_fx, proj, dproj, 1, "merge_bwd_fox")
    d_bsb = _matmul(o_sb, dp_sb, out_dtype=BF16, tm=W_ATT, tn=D, tk=LP, ta=True, name="d_w_branch_sb")
    d_bfx = _matmul(o_fx, dp_fx, out_dtype=BF16, tm=W_ATT, tn=D, tk=LP, ta=True, name="d_w_branch_fox")
    do_sb = mm(dp_sb, w_bsb, out_dtype=BF16, tn=W_ATT, tk=D, tb=True, name="d_o_sb")
    do_fx = mm(dp_fx, w_bfx, out_dtype=BF16, tn=W_ATT, tk=D, tb=True, name="d_o_fox")
    sent = early_grads(dict(w_branch_sb=d_bsb, w_branch_fox=d_bfx, w_out=d_out, w_up=d_up, conv_w=d_cw, w_down=d_down))
    dproj = _sb_bwd(proj, do_sb, rs, dproj, sent, nseq, 2, "sb_bwd")
    dproj, dck, dcq = _fox_bwd(proj, c, crow, o_fx32, lse, do_fx, dproj, nseq, 2, "fox_bwd")
    dc = dcq + jnp.pad(dck.transpose(0, 2, 1).reshape(t, NH), ((0, 0), (0, LANES - NH)))
    dproj, d_bf = _gate_bwd(proj, bf, dc, dproj, nseq, "gate_bwd")
    d_in = _matmul(n1, dproj, out_dtype=BF16, tm=D, tn=1792, tk=LP, ta=True, name="d_w_in")
    dh0, dg_mix = rows(dproj, w_in_p, [h0, dh1], [g_mix], _residual_norm_bwd_f32, [F32], [(1, D)],
                       tk=IN_P // 2, tb=True, after=last_grad(d_in), name="d_n1_norm1_bwd")
    dh0 = dh0.reshape(nseq, LP, D)
    grads = dict(meta_tokens=jnp.sum(dh0[:, :N_META], axis=0), norm_mix_g=dg_mix, b_forget=d_bf[:, :NH],
                 norm_ffn_g=dg_ffn, norm_final_g=dg_final)
    return loss[0, 0], _real_rows(dh0, nseq, "grad_x"), grads


REPL = (("norm_mix_g", D), ("norm_ffn_g", D), ("norm_final_g", D), ("b_forget", LANES))
REPL_ROWS = 32
META_ROWS = N_META * D // LANES


def _pack_repl(tree):
    rows = [jnp.pad(tree[name].reshape(-1), (0, n - tree[name].size)).reshape(-1, LANES) for name, n in REPL]
    packed = jnp.concatenate(rows, axis=0)
    return jnp.pad(packed, ((0, REPL_ROWS - packed.shape[0]), (0, 0)))


def _unpack_repl(packed, shapes):
    out, r = {}, 0
    for name, n in REPL:
        size = 1
        for s in shapes[name]:
            size *= s
        out[name] = packed[r:r + n // LANES].reshape(-1)[:size].reshape(shapes[name])
        r += n // LANES
    return out


def kernel(x, meta_tokens, norm_mix_g, w_in, b_forget, w_branch_sb, w_branch_fox, w_out, norm_ffn_g, w_up, conv_w, w_down, norm_final_g, loss_target, m_meta_tokens, m_norm_mix_g, m_w_in, m_b_forget, m_w_branch_sb, m_w_branch_fox, m_w_out, m_norm_ffn_g, m_w_up, m_conv_w, m_w_down, m_norm_final_g, v_meta_tokens, v_norm_mix_g, v_w_in, v_b_forget, v_w_branch_sb, v_w_branch_fox, v_w_out, v_norm_ffn_g, v_w_up, v_conv_w, v_w_down, v_norm_final_g):
    w = dict(meta_tokens=meta_tokens, norm_mix_g=norm_mix_g, w_in=w_in, b_forget=b_forget, w_branch_sb=w_branch_sb,
             w_branch_fox=w_branch_fox, w_out=w_out, norm_ffn_g=norm_ffn_g, w_up=w_up, conv_w=conv_w, w_down=w_down,
             norm_final_g=norm_final_g)
    m = dict(meta_tokens=m_meta_tokens, norm_mix_g=m_norm_mix_g, w_in=m_w_in, b_forget=m_b_forget,
             w_branch_sb=m_w_branch_sb, w_branch_fox=m_w_branch_fox, w_out=m_w_out, norm_ffn_g=m_norm_ffn_g,
             w_up=m_w_up, conv_w=m_conv_w, w_down=m_w_down, norm_final_g=m_norm_final_g)
    v = dict(meta_tokens=v_meta_tokens, norm_mix_g=v_norm_mix_g, w_in=v_w_in, b_forget=v_b_forget,
             w_branch_sb=v_w_branch_sb, w_branch_fox=v_w_branch_fox, w_out=v_w_out, norm_ffn_g=v_norm_ffn_g,
             w_up=v_w_up, conv_w=v_conv_w, w_down=v_w_down, norm_final_g=v_norm_final_g)
    shapes = {k: a.shape for k, a in w.items()}
    sharded = ("w_in", "w_branch_sb", "w_branch_fox", "w_out", "w_up", "w_down", "conv_w", "meta_tokens")
    mat = lambda tree, name: tree[name].reshape(tree[name].shape[-2:])

    def lane_pad(a, width):
        return jnp.pad(a, ((0, 0), (0, width - a.shape[1])))

    late = ("w_branch_sb", "w_branch_fox", "w_out", "w_up", "w_down", "conv_w")
    pending_w = {}
    g_meta, = _all_gather([mat(w, "meta_tokens")], "gather_meta")
    pending_w["in"], in_started = _remote_start(
        [lane_pad(mat(w, "w_in").astype(BF16), SHARD_P)], False, g_meta, "gather_w_in_start")
    meta_full = g_meta.transpose(1, 0, 2).reshape(N_META, D) + in_started[0, 0]

    def first_weights(after):
        g_in, = _remote_wait(pending_w["in"], after, "gather_w_in_wait")
        pending_w["late"], started = _remote_start(
            [mat(w, "w_branch_sb").astype(BF16), mat(w, "w_branch_fox").astype(BF16), mat(w, "w_out").astype(BF16),
             lane_pad(mat(w, "w_up").astype(BF16), SHARD_P), mat(w, "w_down").astype(BF16), mat(w, "conv_w")],
            False, g_in, "gather_late_start")
        w_in_p = _relayout(g_in, 1, IN_P, _gathered_to_full(IN_SHARD, _in_padded_to_orig), BF16, 256, "w_in_cols")[0]
        return w_in_p, started

    def late_weights(after):
        g_bsb, g_bfx, g_out, g_up, g_down, g_cw = _remote_wait(pending_w["late"], after, "gather_late_wait")
        w_up_i = _relayout(g_up, 1, 2 * D_FF, _gathered_to_full(UP_SHARD, _up_inter_to_orig), BF16, 256, "w_up_cols")[0]
        w_bsb = _relayout(g_bsb, 1, D, _gathered_to_full(ATT_SHARD, lambda d: d), BF16, 256, "w_bsb_cols")[0]
        w_bfx = _relayout(g_bfx, 1, D, _gathered_to_full(ATT_SHARD, lambda d: d), BF16, 256, "w_bfx_cols")[0]
        cw_full = g_cw.transpose(1, 0, 2).reshape(3, 2 * D_FF)
        cw_i = cw_full.reshape(3, 2, D_FF // FFC, FFC).transpose(0, 2, 1, 3).reshape(3, 2 * D_FF)
        return w_bsb, w_bfx, g_out.reshape(D, D), w_up_i, cw_i, g_down.reshape(D_FF, D)

    pending_g = {}

    def early_grads(g):
        d_cw = g["conv_w"].reshape(3, D_FF // FFC, 2, FFC).transpose(0, 2, 1, 3).reshape(3, 2 * D_FF)
        pending_g["early"], sent = _remote_start(
            [_relayout(g["w_branch_sb"][None], N_DEV, ATT_SHARD, _full_to_shards(ATT_SHARD, lambda c: c), BF16, 256, "d_w_bsb_shards"),
             _relayout(g["w_branch_fox"][None], N_DEV, ATT_SHARD, _full_to_shards(ATT_SHARD, lambda c: c), BF16, 256, "d_w_bfx_shards"),
             g["w_out"].reshape(N_DEV, D // N_DEV, D),
             _relayout(g["w_up"][None], N_DEV, SHARD_P, _full_to_shards(UP_SHARD, _UP_ORIG_TO_INTER.get), BF16, 256, "d_w_up_shards"),
             g["w_down"].reshape(N_DEV, D_FF // N_DEV, D),
             d_cw.reshape(3, N_DEV, UP_SHARD).transpose(1, 0, 2)], True, g["w_out"], "exchange_early_start")
        return sent

    def last_grad(d_in):
        shards = _relayout(d_in[None], N_DEV, SHARD_P, _full_to_shards(IN_SHARD, _IN_ORIG_TO_PADDED.get), BF16, 256, "d_w_in_shards")
        pending_g["last"], sent = _remote_start([shards], True, shards, "exchange_last_start")
        return sent

    loss, grad_x, grads = _local_step(
        x, loss_target, meta_full, jnp.zeros((N_META, D), F32) + in_started[0, 0], norm_mix_g.reshape(1, D), b_forget,
        norm_ffn_g.reshape(1, D), norm_final_g.reshape(1, D), first_weights, late_weights, early_grads, last_grad)

    small = jnp.concatenate([_pack_repl(grads), grads["meta_tokens"].reshape(META_ROWS, LANES)], axis=0)
    small, = _all_gather([small], "gather_small_grads")
    me_idx = 4 * lax.axis_index("x") + 2 * lax.axis_index("y") + lax.axis_index("c")
    p_meta = lax.dynamic_slice_in_dim(small[:, REPL_ROWS:].reshape(N_DEV, N_META, D), me_idx * ATT_SHARD, ATT_SHARD, axis=2)

    p_in, = _remote_wait(pending_g["last"], small, "exchange_last_wait")
    parts = dict(zip(late, _remote_wait(pending_g["early"], p_in, "exchange_early_wait")), w_in=p_in, meta_tokens=p_meta)
    tiles = dict(w_in=256, w_branch_sb=256, w_branch_fox=256, w_out=D // N_DEV, w_up=256, w_down=D_FF // N_DEV,
                 conv_w=3, meta_tokens=N_META)
    new = {name: _sum_adamw(parts[name], mat(w, name), mat(m, name), mat(v, name), tiles[name], "adamw_" + name)
           for name in sharded}

    routs = _sum_adamw(small[:, :REPL_ROWS], _pack_repl(w), _pack_repl(m), _pack_repl(v), REPL_ROWS, "adamw_replicated")
    repl = [_unpack_repl(o, shapes) for o in routs]

    result = [lax.psum(loss, ("x", "y", "c")), grad_x]
    for k in range(4):
        for name in w:
            result.append(new[name][k].reshape(shapes[name]) if name in new else repl[k][name])
    return tuple(result)
```

```python
import functools

import jax
import jax.numpy as jnp
from jax import lax
from jax.experimental import pallas as pl
from jax.experimental.pallas import tpu as pltpu

F32 = jnp.float32
BF16 = jnp.bfloat16

N_DEV = 8
LANES = 128
D = 1024
N_META = 16
SEQ = 2048
L_REAL = N_META + SEQ
LP = 2304
BQ = 256
NBLK = LP // BQ
HEAD = 64
NH = 8
W_ATT = NH * HEAD
PAIR_W = 3 * LANES
D_FF = 2816
IN_COLS = 5128
QKV = 6 * W_ATT
IN_P = 5376
GATE_COL = QKV
F_COL = QKV + 2 * D
FFC = 256
RMS_EPS = 1e-6
LR, B1, B2, EPS, WD, STEP = 0.001, 0.9, 0.999, 1e-08, 0.01, 10
VMEM_LIMIT = 56 * 1024 * 1024

MESH = pl.DeviceIdType.MESH
ANY = pl.BlockSpec(memory_space=pl.ANY)


def _cparams(*sem):
    return pltpu.CompilerParams(dimension_semantics=sem if sem else None, vmem_limit_bytes=VMEM_LIMIT)


def _all_gather(xs, name):
    n = len(xs)

    def body(*refs):
        x_refs, out_refs = refs[:n], refs[n:2 * n]
        send_sems, recv_sems, local_sems = refs[2 * n:]
        mx, my, mc = lax.axis_index("x"), lax.axis_index("y"), lax.axis_index("c")
        me, sibling = (mx, my, mc), (mx, my, 1 - mc)
        chips = [(1 - mx, my), (mx, 1 - my), (1 - mx, 1 - my)]

        def copy(a, k, block, to, own=False):
            px, py, pc = block
            slot = out_refs[a].at[4 * px + 2 * py + pc]
            return pltpu.make_async_remote_copy(
                src_ref=x_refs[a] if own else slot, dst_ref=slot,
                send_sem=send_sems.at[7 * a + k], recv_sem=recv_sems.at[7 * a + k],
                device_id=to, device_id_type=MESH)

        mine = [pltpu.make_async_copy(x_refs[a], out_refs[a].at[4 * mx + 2 * my + mc], local_sems.at[a]) for a in range(n)]
        for cp in mine:
            cp.start()
        first = []
        for a in range(n):
            first.append(copy(a, 0, me, sibling, own=True))
            first += [copy(a, 1 + j, me, (*chip, mc), own=True) for j, chip in enumerate(chips)]
        for cp in first:
            cp.start()
        passed = []
        for j, chip in enumerate(chips):
            for a in range(n):
                copy(a, 1 + j, (*chip, mc), me).wait_recv()
                fwd = copy(a, 4 + j, (*chip, mc), sibling)
                fwd.start()
                passed.append(fwd)
        for a in range(n):
            copy(a, 0, sibling, me).wait_recv()
            for j, chip in enumerate(chips):
                copy(a, 4 + j, (*chip, 1 - mc), me).wait_recv()
        for cp in first + passed:
            cp.wait_send()
        for cp in mine:
            cp.wait()

    return pl.pallas_call(
        body, name=name,
        out_shape=tuple(jax.ShapeDtypeStruct((N_DEV,) + x.shape, x.dtype) for x in xs),
        in_specs=[ANY] * n, out_specs=tuple([ANY] * n),
        scratch_shapes=[pltpu.SemaphoreType.DMA((7 * n,)), pltpu.SemaphoreType.DMA((7 * n,)),
                        pltpu.SemaphoreType.DMA((n,))],
    )(*xs)


HBM = pl.BlockSpec(memory_space=pltpu.HBM)
SEM = pl.BlockSpec(memory_space=pltpu.SEMAPHORE)
EFFECT = pltpu.SideEffectType.DATAFLOW_SIDE_EFFECTING


def _peer_copies(src_refs, land_refs, send_sems, recv_sems, per_peer):
    mx, my, mc = lax.axis_index("x"), lax.axis_index("y"), lax.axis_index("c")
    me_idx = 4 * mx + 2 * my + mc
    copies = []
    for k in range(1, N_DEV):
        px, py, pc = mx ^ (k >> 2), my ^ ((k >> 1) & 1), mc ^ (k & 1)
        for a, (src, land) in enumerate(zip(src_refs, land_refs)):
            copies.append(pltpu.make_async_remote_copy(
                src_ref=src.at[4 * px + 2 * py + pc] if per_peer else src, dst_ref=land.at[me_idx],
                send_sem=send_sems.at[7 * a + k - 1], recv_sem=recv_sems.at[7 * a + k - 1],
                device_id=(px, py, pc), device_id_type=MESH))
    return me_idx, copies


def _remote_start(srcs, per_peer, after, name):
    n = len(srcs)
    lands = [lax.empty(s.shape if per_peer else (N_DEV,) + s.shape, s.dtype) for s in srcs]

    def body(*refs):
        src_refs, land_refs = refs[:n], refs[n:2 * n]
        send_sems, recv_sems = refs[2 * n + 1:2 * n + 3]
        token = refs[4 * n + 3]
        stage, local_sems = refs[4 * n + 4:5 * n + 4], refs[5 * n + 4]
        me_idx, copies = _peer_copies(src_refs, land_refs, send_sems, recv_sems, per_peer)
        for cp in copies:
            cp.start()
        own = [src_refs[a].at[me_idx] if per_peer else src_refs[a] for a in range(n)]
        for hop in ([(own[a], stage[a]) for a in range(n)], [(stage[a], land_refs[a].at[me_idx]) for a in range(n)]):
            cps = [pltpu.make_async_copy(s, d, local_sems.at[a]) for a, (s, d) in enumerate(hop)]
            for cp in cps:
                cp.start()
            for cp in cps:
                cp.wait()
        token[...] = jnp.zeros_like(token)

    thru = [pltpu.HBM(a.shape, a.dtype) for a in list(srcs) + lands]
    out = pl.pallas_call(
        body, name=name,
        out_shape=(pltpu.SemaphoreType.DMA((7 * n,)), pltpu.SemaphoreType.DMA((7 * n,)), *thru,
                   jax.ShapeDtypeStruct((8, LANES), F32)),
        in_specs=[HBM] * (2 * n) + [ANY],
        out_specs=(SEM, SEM, *([HBM] * (2 * n)), pl.BlockSpec(memory_space=pltpu.VMEM)),
        input_output_aliases={i: 2 + i for i in range(2 * n)},
        scratch_shapes=[pltpu.VMEM(s.shape[1:] if per_peer else s.shape, s.dtype) for s in srcs]
        + [pltpu.SemaphoreType.DMA((n,))],
        compiler_params=pltpu.CompilerParams(has_side_effects=EFFECT),
    )(*[pltpu.with_memory_space_constraint(a, pltpu.HBM) for a in list(srcs) + lands], after)
    return dict(sems=out[:2], bufs=out[2:2 * n + 2], per_peer=per_peer), out[-1]


def _remote_wait(pending, after, name):
    bufs = pending["bufs"]
    n = len(bufs) // 2
    per_peer = pending["per_peer"]

    def body(*refs):
        src_refs, land_refs = refs[:n], refs[n:2 * n]
        send_sems, recv_sems = refs[2 * n:2 * n + 2]
        _, copies = _peer_copies(src_refs, land_refs, send_sems, recv_sems, per_peer)
        for cp in copies:
            cp.wait_send()
        for cp in copies:
            cp.wait_recv()

    out = pl.pallas_call(
        body, name=name, out_shape=tuple(pltpu.HBM(a.shape, a.dtype) for a in bufs),
        in_specs=[HBM] * (2 * n) + [SEM, SEM, ANY], out_specs=tuple([HBM] * (2 * n)),
        input_output_aliases={i: i for i in range(2 * n)},
        compiler_params=pltpu.CompilerParams(has_side_effects=EFFECT),
    )(*bufs, *pending["sems"], after)
    return out[n:]


ROWS_PER_COPY = 256


def _pad_rows(front, body_rows, nseq, name):
    tail = LP - L_REAL
    nblk = SEQ // ROWS_PER_COPY

    def body(f_ref, b_ref, o_ref, z_ref, sems):
        s, i = pl.program_id(0), pl.program_id(1)
        rows = pltpu.make_async_copy(b_ref, o_ref.at[pl.ds(s, 1), pl.ds(N_META + i * ROWS_PER_COPY, ROWS_PER_COPY)], sems.at[0])
        rows.start()

        @pl.when(i == 0)
        def _():
            z_ref[...] = jnp.zeros_like(z_ref)
            head = pltpu.make_async_copy(f_ref, o_ref.at[s, pl.ds(0, N_META)], sems.at[1])
            zeros = pltpu.make_async_copy(z_ref, o_ref.at[s, pl.ds(L_REAL, tail)], sems.at[2])
            head.start()
            zeros.start()
            head.wait()
            zeros.wait()

        rows.wait()

    return pl.pallas_call(
        body, name=name, out_shape=jax.ShapeDtypeStruct((nseq, LP, D), F32), grid=(nseq, nblk),
        in_specs=[pl.BlockSpec((N_META, D), lambda s, i: (0, 0)), pl.BlockSpec((1, ROWS_PER_COPY, D), lambda s, i: (s, i, 0))],
        out_specs=ANY,
        scratch_shapes=[pltpu.VMEM((tail, D), F32), pltpu.SemaphoreType.DMA((3,))],
        compiler_params=_cparams("arbitrary", "arbitrary"))(front, body_rows)


def _real_rows(h, nseq, name):
    nblk = SEQ // ROWS_PER_COPY

    def body(h_ref, o_ref, sem):
        s, i = pl.program_id(0), pl.program_id(1)
        rows = pltpu.make_async_copy(h_ref.at[pl.ds(s, 1), pl.ds(N_META + i * ROWS_PER_COPY, ROWS_PER_COPY)], o_ref, sem)
        rows.start()
        rows.wait()

    return pl.pallas_call(
        body, name=name, out_shape=jax.ShapeDtypeStruct((nseq, SEQ, D), F32), grid=(nseq, nblk),
        in_specs=[ANY], out_specs=pl.BlockSpec((1, ROWS_PER_COPY, D), lambda s, i: (s, i, 0)),
        scratch_shapes=[pltpu.SemaphoreType.DMA],
        compiler_params=_cparams("arbitrary", "arbitrary"))(h)


def _plan_cols(n_q, n_dcols, src_of):
    plan = {}
    for q in range(n_q):
        for dblk in range(n_dcols // LANES):
            segs, key, start = [], None, 0
            for lane in range(LANES + 1):
                new = None
                if lane < LANES:
                    src = src_of(q, dblk * LANES + lane)
                    if src is not None:
                        new = (src[0], src[1] // LANES, (lane - src[1] % LANES) % LANES)
                if new != key:
                    if key is not None:
                        segs.append((*key, start, lane))
                    key, start = new, lane
            plan[(q, dblk)] = segs
    return plan


def _relayout(src, n_q, n_dcols, src_of, out_dtype, tr, name):
    n_p, rows, scols = src.shape
    plan = _plan_cols(n_q, n_dcols, src_of)

    def body(s_ref, d_ref):
        lane = lax.broadcasted_iota(jnp.int32, (tr, LANES), 1)
        for (q, dblk), segs in plan.items():
            acc = jnp.zeros((tr, LANES), F32)
            for p, sblk, rot, lo, hi in segs:
                x = s_ref[p, :, sblk * LANES:(sblk + 1) * LANES].astype(F32)
                if rot:
                    x = pltpu.roll(x, rot, 1)
                acc = x if (lo, hi) == (0, LANES) else jnp.where((lane >= lo) & (lane < hi), x, acc)
            d_ref[q, :, dblk * LANES:(dblk + 1) * LANES] = acc.astype(out_dtype)

    return pl.pallas_call(
        body, name=name, out_shape=jax.ShapeDtypeStruct((n_q, rows, n_dcols), out_dtype), grid=(rows // tr,),
        in_specs=[pl.BlockSpec((n_p, tr, scols), lambda i: (0, i, 0))],
        out_specs=pl.BlockSpec((n_q, tr, n_dcols), lambda i: (0, i, 0)),
        compiler_params=_cparams("parallel"))(src)


def _in_padded_to_orig(d):
    if d < QKV:
        kind, r = divmod(d, 4 * PAIR_W)
        pair, r = divmod(r, PAIR_W)
        part, r = divmod(r, LANES)
        return kind * 3 * W_ATT + part * W_ATT + pair * LANES + r
    if d < F_COL:
        return d + NH
    if d < F_COL + NH:
        return d - 2 * D
    return None


_IN_ORIG_TO_PADDED = {_in_padded_to_orig(d): d for d in range(IN_P) if _in_padded_to_orig(d) is not None}


def _up_inter_to_orig(d):
    j, r = divmod(d, 2 * FFC)
    part, r = divmod(r, FFC)
    return part * D_FF + j * FFC + r


_UP_ORIG_TO_INTER = {_up_inter_to_orig(d): d for d in range(2 * D_FF)}
IN_SHARD = IN_COLS // N_DEV
UP_SHARD = 2 * D_FF // N_DEV
SHARD_P = 768
ATT_SHARD = D // N_DEV


def _gathered_to_full(n_shard, to_orig):
    def src_of(q, d):
        c = to_orig(d)
        return None if c is None else (c // n_shard, c % n_shard)
    return src_of


def _full_to_shards(n_shard, from_orig):
    def src_of(q, d):
        return (0, from_orig(q * n_shard + d)) if d < n_shard else None
    return src_of


def _matmul(a, b, *, out_dtype, tm, tn, tk, ta=False, tb=False, after=None, name):
    if ta:
        kdim, m = a.shape
    else:
        m, kdim = a.shape
    n = b.shape[0] if tb else b.shape[1]
    assert m % tm == 0 and n % tn == 0 and kdim % tk == 0, (name, a.shape, b.shape, tm, tn, tk)
    nk = kdim // tk

    def body(a_ref, b_ref, *rest):
        o_ref, scratch = rest[len(extra)], rest[len(extra) + 1:]
        av, bv = a_ref[...], b_ref[...]
        if ta:
            p = lax.dot_general(av, bv, (((0,), (0,)), ((), ())), preferred_element_type=F32)
        elif tb:
            p = lax.dot_general(av, bv, (((1,), (1,)), ((), ())), preferred_element_type=F32)
        else:
            p = jnp.dot(av, bv, preferred_element_type=F32)
        if nk == 1:
            o_ref[...] = p.astype(o_ref.dtype)
        else:
            acc_ref, = scratch
            k = pl.program_id(2)

            @pl.when(k == 0)
            def _():
                acc_ref[...] = p

            @pl.when(k > 0)
            def _():
                acc_ref[...] += p

            @pl.when(k == nk - 1)
            def _():
                o_ref[...] = acc_ref[...].astype(o_ref.dtype)

    extra = [] if after is None else [after]
    a_spec = pl.BlockSpec((tk, tm), lambda i, j, k: (k, i)) if ta else pl.BlockSpec((tm, tk), lambda i, j, k: (i, k))
    b_spec = pl.BlockSpec((tn, tk), lambda i, j, k: (j, k)) if tb else pl.BlockSpec((tk, tn), lambda i, j, k: (k, j))
    return pl.pallas_call(
        body, name=name,
        out_shape=jax.ShapeDtypeStruct((m, n), out_dtype),
        grid=(m // tm, n // tn, nk),
        in_specs=[a_spec, b_spec] + [ANY] * len(extra),
        out_specs=pl.BlockSpec((tm, tn), lambda i, j, k: (i, j)),
        scratch_shapes=[] if nk == 1 else [pltpu.VMEM((tm, tn), F32)],
        compiler_params=_cparams("parallel", "parallel", "arbitrary"),
    )(a, b, *extra)


TR = 288


def _rms(h):
    return lax.rsqrt(jnp.mean(h * h, axis=-1, keepdims=True) + RMS_EPS)


def _norm_fwd(h, g, name):
    t = h.shape[0]
    row = pl.BlockSpec((TR, D), lambda i: (i, 0))

    def body(h_ref, g_ref, n_ref):
        hv = h_ref[...]
        n_ref[...] = ((hv * _rms(hv)) * g_ref[...]).astype(BF16)

    return pl.pallas_call(
        body, name=name, out_shape=jax.ShapeDtypeStruct((t, D), BF16), grid=(t // TR,),
        in_specs=[row, pl.BlockSpec((1, D), lambda i: (0, 0))], out_specs=row, compiler_params=_cparams("parallel"))(h, g)


EPI_ROWS = 144


def _matmul_rows(a, b, rows_in, vecs_in, epilogue, row_outs, sum_outs, *, tm, tk, tb=False, after=None, name):
    m, kdim = a.shape
    assert (b.shape[0] if tb else b.shape[1]) == D and m % tm == 0 and kdim % tk == 0 and tm % EPI_ROWS == 0
    nk = kdim // tk
    n_r, n_v, n_ro, n_so = len(rows_in), len(vecs_in), len(row_outs), len(sum_outs)
    extra = [] if after is None else [after]

    def body(a_ref, b_ref, *rest):
        r_refs, v_refs = rest[:n_r], rest[n_r:n_r + n_v]
        outs = rest[n_r + n_v + len(extra):]
        ro_refs, so_refs, acc_ref = outs[:n_ro], outs[n_ro:n_ro + n_so], outs[n_ro + n_so]
        i, k = pl.program_id(0), pl.program_id(1)
        if tb:
            p = lax.dot_general(a_ref[...], b_ref[...], (((1,), (1,)), ((), ())), preferred_element_type=F32)
        else:
            p = jnp.dot(a_ref[...], b_ref[...], preferred_element_type=F32)

        @pl.when(k == 0)
        def _():
            acc_ref[...] = p

        @pl.when(k > 0)
        def _():
            acc_ref[...] += p

        @pl.when(k == nk - 1)
        def _():
            vecs = [v[...] for v in v_refs]

            def step(c, sums):
                rows = pl.ds(pl.multiple_of(c * EPI_ROWS, 8), EPI_ROWS)
                tiles, terms = epilogue(i * tm + c * EPI_ROWS, acc_ref[rows, :], *[r[rows, :] for r in r_refs], *vecs)
                for o, tile in zip(ro_refs, tiles):
                    o[rows, :] = tile.astype(o.dtype)
                return tuple(s + term for s, term in zip(sums, terms))

            sums = lax.fori_loop(0, tm // EPI_ROWS, step, tuple(jnp.zeros(s, F32) for s in sum_outs))

            @pl.when(i == 0)
            def _():
                for o in so_refs:
                    o[...] = jnp.zeros_like(o)

            for o, s in zip(so_refs, sums):
                o[...] += s

    row = pl.BlockSpec((tm, D), lambda i, k: (i, 0))
    b_spec = pl.BlockSpec((D, tk), lambda i, k: (0, k)) if tb else pl.BlockSpec((tk, D), lambda i, k: (k, 0))
    return pl.pallas_call(
        body, name=name,
        out_shape=tuple([jax.ShapeDtypeStruct((m, D), dt) for dt in row_outs] + [jax.ShapeDtypeStruct(s, F32) for s in sum_outs]),
        grid=(m // tm, nk),
        in_specs=[pl.BlockSpec((tm, tk), lambda i, k: (i, k)), b_spec] + [row] * n_r
        + [pl.BlockSpec((1, D), lambda i, k: (0, 0))] * n_v + [ANY] * len(extra),
        out_specs=tuple([row] * n_ro + [pl.BlockSpec(s, lambda i, k: (0, 0)) for s in sum_outs]),
        scratch_shapes=[pltpu.VMEM((tm, D), F32)],
        compiler_params=_cparams("arbitrary", "arbitrary"))(a, b, *rows_in, *vecs_in, *extra)


def _residual_norm(row0, acc, h, g):
    hv = h + acc
    return (hv, (hv * _rms(hv)) * g), ()


def _rms_bwd_math(hv, dn, gv):
    r = _rms(hv)
    hr = hv * r
    dng = dn * gv
    dh = r * (dng - hr * jnp.mean(dng * hr, axis=-1, keepdims=True))
    return dh, dn * hr


def _loss_head(row0, acc, h1, tgt, g):
    hv = h1 + acc
    hr = hv * _rms(hv)
    pos = row0 % LP + lax.broadcasted_iota(jnp.int32, (EPI_ROWS, 1), 0)
    valid = (pos >= N_META) & (pos < L_REAL)
    err = jnp.where(valid, hr * g - tgt, 0.0)
    part = 0.5 * jnp.sum(jnp.mean(err * err, axis=-1, keepdims=True))
    dy = err * (1.0 / D)
    dh, dgrow = _rms_bwd_math(hv, dy, g)
    return (dh, dh), (jnp.full((8, LANES), part, F32), jnp.sum(dgrow, axis=0, keepdims=True))


def _residual_norm_bwd(row0, acc, h, dres, g):
    dh, dgrow = _rms_bwd_math(h, acc, g)
    dh = dh + dres
    return (dh, dh), (jnp.sum(dgrow, axis=0, keepdims=True),)


def _residual_norm_bwd_f32(row0, acc, h, dres, g):
    tiles, sums = _residual_norm_bwd(row0, acc, h, dres, g)
    return tiles[:1], sums


GATE_BLK = GATE_COL // D


def _sigmoid(x):
    return 1.0 / (1.0 + jnp.exp(-x))


def _merge_fwd(p_sb, p_fx, proj, name):
    t = p_sb.shape[0]
    row = pl.BlockSpec((TR, D), lambda i: (i, 0))

    def body(ps_ref, pf_ref, gs_ref, gf_ref, o_ref):
        o_ref[...] = (_sigmoid(gs_ref[...]) * ps_ref[...] + _sigmoid(gf_ref[...]) * pf_ref[...]).astype(BF16)

    return pl.pallas_call(
        body, name=name, out_shape=jax.ShapeDtypeStruct((t, D), BF16), grid=(t // TR,),
        in_specs=[row, row, pl.BlockSpec((TR, D), lambda i: (i, GATE_BLK)),
                  pl.BlockSpec((TR, D), lambda i: (i, GATE_BLK + 1))],
        out_specs=row, compiler_params=_cparams("parallel"))(p_sb, p_fx, proj, proj)


def _merge_bwd(dm, p, proj, dproj, which, name):
    t = dm.shape[0]
    row = pl.BlockSpec((TR, D), lambda i: (i, 0))
    gate = pl.BlockSpec((TR, D), lambda i: (i, GATE_BLK + which))

    def body(dm_ref, p_ref, g_ref, *rest):
        dp_ref, dg_ref = rest[-2:]
        dmv = dm_ref[...]
        s = _sigmoid(g_ref[...])
        dp_ref[...] = (dmv * s).astype(BF16)
        dg_ref[...] = (dmv * p_ref[...] * s * (1.0 - s)).astype(BF16)

    out_shape = (jax.ShapeDtypeStruct((t, D), BF16), jax.ShapeDtypeStruct((t, IN_P), BF16))
    if dproj is None:
        return pl.pallas_call(
            body, name=name, out_shape=out_shape, grid=(t // TR,), in_specs=[row, row, gate],
            out_specs=(row, gate), compiler_params=_cparams("parallel"))(dm, p, proj)
    return pl.pallas_call(
        body, name=name, out_shape=out_shape, grid=(t // TR,), in_specs=[row, row, gate, ANY],
        out_specs=(row, gate), input_output_aliases={3: 1}, compiler_params=_cparams("parallel"))(dm, p, proj, dproj)


CH = 288


def _chunk(c, n=CH):
    return pl.ds(pl.multiple_of(c * CH, 8), n)


def _conv_taps(u_ref, c):
    x = u_ref[_chunk(c), :]
    prev = u_ref[pl.ds(pl.multiple_of(jnp.maximum(c * CH - 8, 0), 8), 8), :]
    xx = jnp.concatenate([jnp.where(c == 0, 0.0, prev), x], axis=0)
    return x, pltpu.roll(xx, 1, 0)[8:], pltpu.roll(xx, 2, 0)[8:]


def _conv_glu_fwd(u, cw, nseq, name):
    nblk = D_FF // FFC

    def body(u_ref, cw_ref, o_ref):
        cwv = cw_ref[...]

        def step(c, _):
            x, x1, x2 = _conv_taps(u_ref, c)
            uc = cwv[0:1, :] * x2 + cwv[1:2, :] * x1 + cwv[2:3, :] * x
            a, b = uc[:, :FFC], uc[:, FFC:]
            o_ref[_chunk(c), :] = (a * _sigmoid(a) * b).astype(BF16)
            return 0

        lax.fori_loop(0, LP // CH, step, 0)

    return pl.pallas_call(
        body, name=name, out_shape=jax.ShapeDtypeStruct((nseq * LP, D_FF), BF16), grid=(nseq, nblk),
        in_specs=[pl.BlockSpec((LP, 2 * FFC), lambda s, j: (s, j)), pl.BlockSpec((3, 2 * FFC), lambda s, j: (0, j))],
        out_specs=pl.BlockSpec((LP, FFC), lambda s, j: (s, j)),
        compiler_params=_cparams("parallel", "parallel"))(u, cw)


def _conv_glu_bwd(u, cw, dact, nseq, name):
    nblk = D_FF // FFC
    nch = LP // CH

    def body(u_ref, cw_ref, da_ref, du_ref, dcw_ref):
        s = pl.program_id(1)
        cwv = cw_ref[...]

        def step(k, carry):
            nxt, p0, p1, p2 = carry
            c = nch - 1 - k
            x, x1, x2 = _conv_taps(u_ref, c)
            uc = cwv[0:1, :] * x2 + cwv[1:2, :] * x1 + cwv[2:3, :] * x
            a, b = uc[:, :FFC], uc[:, FFC:]
            sa = _sigmoid(a)
            dactv = da_ref[_chunk(c), :]
            da = dactv * b * (sa * (1.0 + a * (1.0 - sa)))
            db = dactv * (a * sa)
            duc = jnp.concatenate([da, db], axis=1)
            dd = jnp.concatenate([duc, nxt], axis=0)
            du = (cwv[2:3, :] * duc + cwv[1:2, :] * pltpu.roll(dd, CH + 7, 0)[:CH]
                  + cwv[0:1, :] * pltpu.roll(dd, CH + 6, 0)[:CH])
            du_ref[_chunk(c), :] = du.astype(BF16)
            return (duc[:8], p0 + jnp.sum(duc * x2, axis=0, keepdims=True),
                    p1 + jnp.sum(duc * x1, axis=0, keepdims=True), p2 + jnp.sum(duc * x, axis=0, keepdims=True))

        zrow = jnp.zeros((1, 2 * FFC), F32)
        _, p0, p1, p2 = lax.fori_loop(0, nch, step, (jnp.zeros((8, 2 * FFC), F32), zrow, zrow, zrow))

        @pl.when(s == 0)
        def _():
            dcw_ref[...] = jnp.zeros_like(dcw_ref)

        dcw_ref[...] += jnp.concatenate([p0, p1, p2], axis=0)

    return pl.pallas_call(
        body, name=name,
        out_shape=(jax.ShapeDtypeStruct((nseq * LP, 2 * D_FF), BF16), jax.ShapeDtypeStruct((3, 2 * D_FF), F32)),
        grid=(nblk, nseq),
        in_specs=[pl.BlockSpec((LP, 2 * FFC), lambda j, s: (s, j)), pl.BlockSpec((3, 2 * FFC), lambda j, s: (0, j)),
                  pl.BlockSpec((LP, FFC), lambda j, s: (s, j))],
        out_specs=(pl.BlockSpec((LP, 2 * FFC), lambda j, s: (s, j)), pl.BlockSpec((3, 2 * FFC), lambda j, s: (0, j))),
        compiler_params=_cparams("parallel", "arbitrary"))(u, cw, dact)


F_BLK = F_COL // LANES
CB = 128


def _split3(x):
    hi = x.astype(BF16)
    r1 = x - hi.astype(F32)
    mid = r1.astype(BF16)
    lo = (r1 - mid.astype(F32)).astype(BF16)
    return hi, mid, lo


def _tri_dot(tri, x):
    hi, mid, lo = _split3(x)
    d = functools.partial(jnp.dot, preferred_element_type=F32)
    return d(tri, hi) + d(tri, mid) + d(tri, lo)


def _log_sigmoid(x):
    return jnp.minimum(x, 0.0) - jnp.log(1.0 + jnp.exp(-jnp.abs(x)))


def _gate_fwd(proj, bf, nseq, name):
    def body(f_ref, b_ref, c_ref):
        r_i = lax.broadcasted_iota(jnp.int32, (CB, CB), 0)
        c_i = lax.broadcasted_iota(jnp.int32, (CB, CB), 1)
        tri = (c_i <= r_i).astype(BF16)
        bv = b_ref[...]

        def step(k, carry):
            rows = pl.ds(pl.multiple_of(k * CB, CB), CB)
            lf = _log_sigmoid(f_ref[rows, :] + bv)
            c_ref[rows, :] = _tri_dot(tri, lf) + carry
            return carry + jnp.sum(lf, axis=0, keepdims=True)

        lax.fori_loop(0, LP // CB, step, jnp.zeros((1, LANES), F32))

    return pl.pallas_call(
        body, name=name, out_shape=jax.ShapeDtypeStruct((nseq * LP, LANES), F32), grid=(nseq,),
        in_specs=[pl.BlockSpec((LP, LANES), lambda s: (s, F_BLK)), pl.BlockSpec((1, LANES), lambda s: (0, 0))],
        out_specs=pl.BlockSpec((LP, LANES), lambda s: (s, 0)),
        compiler_params=_cparams("parallel"))(proj, bf)


def _gate_bwd(proj, bf, dc, dproj, nseq, name):
    def body(f_ref, b_ref, dc_ref, _, df_ref, db_ref):
        s = pl.program_id(0)
        r_i = lax.broadcasted_iota(jnp.int32, (CB, CB), 0)
        c_i = lax.broadcasted_iota(jnp.int32, (CB, CB), 1)
        tri = (c_i >= r_i).astype(BF16)
        bv = b_ref[...]

        def step(kk, carry):
            carry_c, carry_b = carry
            k = LP // CB - 1 - kk
            rows = pl.ds(pl.multiple_of(k * CB, CB), CB)
            dcv = dc_ref[rows, :]
            dlf = _tri_dot(tri, dcv) + carry_c
            df = dlf * _sigmoid(-(f_ref[rows, :] + bv))
            df_ref[rows, :] = jnp.concatenate([df, jnp.zeros_like(df)], axis=1).astype(BF16)
            return carry_c + jnp.sum(dcv, axis=0, keepdims=True), carry_b + jnp.sum(df, axis=0, keepdims=True)

        zero = jnp.zeros((1, LANES), F32)
        _, dbp = lax.fori_loop(0, LP // CB, step, (zero, zero))

        @pl.when(s == 0)
        def _():
            db_ref[...] = jnp.zeros_like(db_ref)

        db_ref[...] += dbp

    return pl.pallas_call(
        body, name=name,
        out_shape=(jax.ShapeDtypeStruct(dproj.shape, BF16), jax.ShapeDtypeStruct((1, LANES), F32)), grid=(nseq,),
        in_specs=[pl.BlockSpec((LP, LANES), lambda s: (s, F_BLK)), pl.BlockSpec((1, LANES), lambda s: (0, 0)),
                  pl.BlockSpec((LP, LANES), lambda s: (s, 0)), ANY],
        out_specs=(pl.BlockSpec((LP, 2 * LANES), lambda s: (s, F_COL // (2 * LANES))), pl.BlockSpec((1, LANES), lambda s: (0, 0))),
        input_output_aliases={3: 0},
        compiler_params=_cparams("arbitrary"))(proj, bf, dc, dproj)


SCALE = 0.125
NEG = -1e30


def _dot_nt(a, b):
    return lax.dot_general(a, b, (((1,), (1,)), ((), ())), preferred_element_type=F32)


def _dot(a, b):
    return jnp.dot(a, b, preferred_element_type=F32)


def _blk(i):
    return pl.ds(pl.multiple_of(i * BQ, BQ), BQ)


def _tile_iotas():
    return lax.broadcasted_iota(jnp.int32, (BQ, BQ), 0), lax.broadcasted_iota(jnp.int32, (BQ, BQ), 1)


def _lane_iota():
    return lax.broadcasted_iota(jnp.int32, (BQ, LANES), 1)


def _head_masks():
    lane = _lane_iota()
    return lane < HEAD, lane >= HEAD


def _only(mask, x):
    return jnp.where(mask, x, jnp.zeros_like(x))


def _chains(npair):
    return [(pp, h) for pp in range(npair) for h in range(2)]


def _load_qkv(p_ref, q_s, k_s, v_s):
    for pp in range(q_s.shape[0]):
        base = pp * PAIR_W
        q_s[pp] = (p_ref[:, base:base + LANES] * SCALE).astype(BF16)
        k_s[pp] = p_ref[:, base + LANES:base + 2 * LANES].astype(BF16)
        v_s[pp] = p_ref[:, base + 2 * LANES:base + 3 * LANES].astype(BF16)


def _softplus(z):
    return jnp.maximum(z, 0.0) + jnp.log(1.0 + jnp.exp(-jnp.abs(z)))


def _hi_lo(x):
    hi = x.astype(BF16)
    return hi, (x - hi.astype(F32)).astype(BF16)


def _sb_tile_weights(q, k, strict, r, u_suf):
    n = len(q)
    z = [_dot_nt(q[c], k[c]) for c in range(n)]
    sp = [_softplus(zc) for zc in z]
    lk = [-spc if strict is None else jnp.where(strict, -spc, 0.0) for spc in sp]
    parts = [_hi_lo(lkc) for lkc in lk]
    suf = [_dot(hi, u_suf) + _dot(lo, u_suf) for hi, lo in parts]
    w = [jnp.exp(z[c] - sp[c] + r[c] + suf[c]) for c in range(n)]
    if strict is not None:
        w = [jnp.where(strict, wc, 0.0) for wc in w]
    r_next = [r[c] + suf[c][:, 0:1] + lk[c][:, 0:1] for c in range(n)]
    return w, sp, r_next


def _group_spec(kind, npair):
    return pl.BlockSpec((LP, npair * PAIR_W), lambda s, g: (s, (NH // (2 * npair)) * kind + g))


def _gheads_spec(npair):
    return pl.BlockSpec((LP, npair * LANES), lambda s, g: (s, g))


def _qkv_scratch(npair):
    return [pltpu.VMEM((npair, LP, LANES), BF16)] * 3


SEQ_SPEC = pl.BlockSpec((LP, LANES), lambda s, g: (s, 0))
RS_STRIDE = 16


def _pair_cols(pp):
    return slice(pp * LANES, (pp + 1) * LANES)


def _carry_spec(npair):
    return pl.BlockSpec((None, npair * LANES, LP), lambda s, g: (s, g, 0))


def _sb_fwd(proj, nseq, npair, name):
    t = nseq * LP
    chains = _chains(npair)

    def body(p_ref, o_ref, rs_ref, q_s, k_s, v_s, acc_ref, r_ref, rb_ref):
        _load_qkv(p_ref, q_s, k_s, v_s)
        row, col = _tile_iotas()
        u_suf = (row > col).astype(BF16)
        diag = col < row
        lane = _lane_iota()
        heads = _head_masks()

        def qblock(i, _):
            acc_ref[...] = jnp.zeros_like(acc_ref)
            rb_ref[...] = jnp.zeros_like(rb_ref)
            r_ref[...] = jnp.zeros_like(r_ref)
            qb = [q_s[pp, _blk(i), :] for pp in range(npair)]

            def tile(j, strict):
                kj = [k_s[pp, _blk(j), :] for pp in range(npair)]
                vj = [v_s[pp, _blk(j), :] for pp in range(npair)]
                r = [r_ref[c] for c in range(len(chains))]
                w, _, r_next = _sb_tile_weights([_only(heads[h], qb[pp]) for pp, h in chains],
                                                [kj[pp] for pp, _ in chains], strict, r, u_suf)
                pv = [_dot(w[c].astype(BF16), _only(heads[h], vj[pp])) for c, (pp, h) in enumerate(chains)]
                for pp in range(npair):
                    acc_ref[pp] += pv[2 * pp] + pv[2 * pp + 1]
                    rb_ref[pp] = jnp.where(lane == j, r[2 * pp], jnp.where(lane == RS_STRIDE + j, r[2 * pp + 1], rb_ref[pp]))
                for c in range(len(chains)):
                    r_ref[c] = r_next[c]

            tile(i, diag)

            def kblock(jj, _):
                tile(i - jj, None)
                return 0

            lax.fori_loop(1, i + 1, kblock, 0)
            for pp in range(npair):
                o_ref[_blk(i), _pair_cols(pp)] = acc_ref[pp].astype(BF16)
                rs_ref[_pair_cols(pp), pl.ds(pl.multiple_of(i * BQ, BQ), BQ)] = rb_ref[pp].T
            return 0

        lax.fori_loop(0, NBLK, qblock, 0)

    return pl.pallas_call(
        body, name=name,
        out_shape=(jax.ShapeDtypeStruct((t, W_ATT), BF16), jax.ShapeDtypeStruct((nseq, W_ATT, LP), F32)),
        grid=(nseq, NH // (2 * npair)), in_specs=[_group_spec(0, npair)], out_specs=(_gheads_spec(npair), _carry_spec(npair)),
        scratch_shapes=_qkv_scratch(npair) + [pltpu.VMEM((npair, BQ, LANES), F32), pltpu.VMEM((2 * npair, BQ, 1), F32),
                                      pltpu.VMEM((npair, BQ, LANES), F32)],
        compiler_params=_cparams("parallel", "parallel"))(proj)


def _sb_bwd(proj, do, rs, dproj, after, nseq, npair, name):
    chains = _chains(npair)

    def body(p_ref, do_ref, rs_ref, _, _after, dp_ref, q_s, k_s, v_s, kt_s, dqa_ref, dka_ref, dva_ref, ep_ref):
        _load_qkv(p_ref, q_s, k_s, v_s)
        row, col = _tile_iotas()
        u_after = (col > row).astype(BF16)
        u_before = (col < row).astype(BF16)
        diag = row < col
        heads = _head_masks()
        sub = lax.broadcasted_iota(jnp.int32, (LANES, BQ), 0)
        rows_of = (sub < HEAD, sub >= HEAD)
        nc = len(chains)
        for pp in range(npair):
            kt_s[pp] = k_s[pp].astype(F32).T.astype(BF16)
        dka_ref[...] = jnp.zeros_like(dka_ref)
        dva_ref[...] = jnp.zeros_like(dva_ref)

        def qblock(i, _):
            queries = pl.ds(pl.multiple_of(i * BQ, BQ), BQ)
            qb = [q_s[pp, _blk(i), :] for pp in range(npair)]
            dob = [do_ref[_blk(i), _pair_cols(pp)] for pp in range(npair)]
            qt = [qb[pp].astype(F32).T.astype(BF16) for pp in range(npair)]
            dot = [dob[pp].astype(F32).T.astype(BF16) for pp in range(npair)]
            qt_m = [jnp.where(rows_of[h], qt[pp], jnp.zeros_like(qt[pp])) for pp, h in chains]
            dot_m = [jnp.where(rows_of[h], dot[pp], jnp.zeros_like(dot[pp])) for pp, h in chains]
            q_m = [_only(heads[h], qb[pp]) for pp, h in chains]
            do_m = [_only(heads[h], dob[pp]) for pp, h in chains]
            dqa_ref[...] = jnp.zeros_like(dqa_ref)
            ep_ref[...] = jnp.zeros_like(ep_ref)

            def tile(j, strict):
                keys = pl.ds(pl.multiple_of(j * BQ, BQ), BQ)
                kj = [k_s[pp, _blk(j), :] for pp in range(npair)]
                vj = [v_s[pp, _blk(j), :] for pp in range(npair)]
                r = [rs_ref[pl.ds(pp * LANES + RS_STRIDE * h + j, 1), queries] for pp, h in chains]
                z = [_dot(kj[pp], qt_m[cidx]) for cidx, (pp, _) in enumerate(chains)]
                dw = [_dot(vj[pp], dot_m[cidx]) for cidx, (pp, _) in enumerate(chains)]
                sp = [_softplus(zc) for zc in z]
                lk = [-spc if strict is None else jnp.where(strict, -spc, 0.0) for spc in sp]
                parts = [_hi_lo(lkc) for lkc in lk]
                suf = [_dot(u_after, hi) + _dot(u_after, lo) for hi, lo in parts]
                w = [jnp.exp(z[cidx] - sp[cidx] + r[cidx] + suf[cidx]) for cidx in range(nc)]
                if strict is not None:
                    w = [jnp.where(strict, wc, 0.0) for wc in w]
                e = [dw[cidx] * w[cidx] for cidx in range(nc)]
                e_pre = [ep_ref[cidx] + _dot(u_before, e[cidx].astype(BF16)) for cidx in range(nc)]
                dz = []
                for cidx in range(nc):
                    ep_ref[cidx] += jnp.sum(e[cidx], axis=0, keepdims=True)
                    sneg = jnp.exp(-sp[cidx])
                    dzc = e[cidx] * sneg - (1.0 - sneg) * e_pre[cidx]
                    if strict is not None:
                        dzc = jnp.where(strict, dzc, 0.0)
                    dz.append(dzc.astype(BF16))
                dq = [_dot(jnp.where(rows_of[h], kt_s[pp, :, keys], jnp.zeros((LANES, BQ), BF16)), dz[cidx])
                      for cidx, (pp, h) in enumerate(chains)]
                dk = [_dot(dz[cidx], q_m[cidx]) for cidx in range(nc)]
                dv = [_dot(w[cidx].astype(BF16), do_m[cidx]) for cidx in range(nc)]
                for pp in range(npair):
                    dqa_ref[pp] += dq[2 * pp] + dq[2 * pp + 1]
                    dka_ref[pp, _blk(j), :] += dk[2 * pp] + dk[2 * pp + 1]
                    dva_ref[pp, _blk(j), :] += dv[2 * pp] + dv[2 * pp + 1]

            def kblock(j, _):
                tile(j, None)
                return 0

            lax.fori_loop(0, i, kblock, 0)
            tile(i, diag)
            for pp in range(npair):
                dp_ref[_blk(i), pp * PAIR_W:pp * PAIR_W + LANES] = (dqa_ref[pp].T * SCALE).astype(BF16)
            return 0

        lax.fori_loop(0, NBLK, qblock, 0)
        for pp in range(npair):
            dp_ref[:, pp * PAIR_W + LANES:pp * PAIR_W + 2 * LANES] = dka_ref[pp].astype(BF16)
            dp_ref[:, pp * PAIR_W + 2 * LANES:pp * PAIR_W + 3 * LANES] = dva_ref[pp].astype(BF16)

    return pl.pallas_call(
        body, name=name, out_shape=jax.ShapeDtypeStruct(dproj.shape, BF16), grid=(nseq, NH // (2 * npair)),
        in_specs=[_group_spec(0, npair), _gheads_spec(npair), _carry_spec(npair), ANY, ANY], out_specs=_group_spec(0, npair),
        input_output_aliases={3: 0},
        scratch_shapes=_qkv_scratch(npair) + [pltpu.VMEM((npair, LANES, LP), BF16), pltpu.VMEM((npair, LANES, BQ), F32),
                                      pltpu.VMEM((npair, LP, LANES), F32), pltpu.VMEM((npair, LP, LANES), F32),
                                      pltpu.VMEM((2 * npair, 1, BQ), F32)],
        compiler_params=_cparams("parallel", "parallel"))(proj, do, rs, dproj, after)


CROW_SPEC = pl.BlockSpec((None, NH, LP), lambda s, g: (s, 0, 0))


def _key_cols(cr_ref, head, j):
    return cr_ref[pl.ds(head, 1), pl.ds(pl.multiple_of(j * BQ, BQ), BQ)]


def _fox_fwd(proj, c, crow, nseq, npair, name):
    t = nseq * LP
    chains = _chains(npair)

    def body(p_ref, c_ref, cr_ref, o_ref, o32_ref, lse_ref, q_s, k_s, v_s, vt_s, ck_s, acc_ref, m_ref, l_ref):
        _load_qkv(p_ref, q_s, k_s, v_s)
        row, col = _tile_iotas()
        diag = row <= col
        sub = lax.broadcasted_iota(jnp.int32, (LANES, BQ), 0)
        rows_of = (sub < HEAD, sub >= HEAD)
        head0 = 2 * npair * pl.program_id(1)
        nc = len(chains)
        lane_all = lax.broadcasted_iota(jnp.int32, (LP, LANES), 1)
        for pp in range(npair):
            vt_s[pp] = v_s[pp].astype(F32).T.astype(BF16)
        for cidx in range(nc):
            ck_s[cidx] = jnp.sum(jnp.where(lane_all == head0 + cidx, c_ref[...], 0.0), axis=1, keepdims=True)

        def qblock(i, _):
            qt = [q_s[pp, _blk(i), :].astype(F32).T.astype(BF16) for pp in range(npair)]
            qt = [jnp.where(rows_of[h], qt[pp], jnp.zeros_like(qt[pp])) for pp, h in chains]
            cq = [_key_cols(cr_ref, head0 + cidx, i) for cidx in range(nc)]
            acc_ref[...] = jnp.zeros_like(acc_ref)
            m_ref[...] = jnp.full_like(m_ref, NEG)
            l_ref[...] = jnp.zeros_like(l_ref)

            def tile(j, causal):
                keys = pl.ds(pl.multiple_of(j * BQ, BQ), BQ)
                z = [_dot(k_s[pp, _blk(j), :], qt[cidx]) + (cq[cidx] - ck_s[cidx, _blk(j), :])
                     for cidx, (pp, _) in enumerate(chains)]
                if causal is not None:
                    z = [jnp.where(causal, zc, NEG) for zc in z]
                p, alpha = [], []
                for cidx in range(nc):
                    m_old = m_ref[cidx]
                    m_new = jnp.maximum(m_old, jnp.max(z[cidx], axis=0, keepdims=True))
                    alpha.append(jnp.exp(m_old - m_new))
                    pc = jnp.exp(z[cidx] - m_new)
                    l_ref[cidx] = alpha[cidx] * l_ref[cidx] + jnp.sum(pc, axis=0, keepdims=True)
                    m_ref[cidx] = m_new
                    p.append(pc.astype(BF16))
                pv = [_dot(jnp.where(rows_of[h], vt_s[pp, :, keys], jnp.zeros((LANES, BQ), BF16)), p[cidx])
                      for cidx, (pp, h) in enumerate(chains)]
                for cidx in range(nc):
                    acc_ref[cidx] = alpha[cidx] * acc_ref[cidx] + pv[cidx]

            def kblock(j, _):
                tile(j, None)
                return 0

            lax.fori_loop(0, i, kblock, 0)
            tile(i, diag)
            for pp in range(npair):
                out_t = acc_ref[2 * pp] / l_ref[2 * pp] + acc_ref[2 * pp + 1] / l_ref[2 * pp + 1]
                out = out_t.T
                o_ref[_blk(i), _pair_cols(pp)] = out.astype(BF16)
                o32_ref[_blk(i), _pair_cols(pp)] = out
                lse = [m_ref[2 * pp + h] + jnp.log(l_ref[2 * pp + h]) for h in range(2)]
                lse_t = jnp.where(sub == 0, lse[0], jnp.where(sub == 1, lse[1], 0.0))
                lse_ref[_blk(i), _pair_cols(pp)] = lse_t.T
            return 0

        lax.fori_loop(0, NBLK, qblock, 0)

    return pl.pallas_call(
        body, name=name,
        out_shape=(jax.ShapeDtypeStruct((t, W_ATT), BF16), jax.ShapeDtypeStruct((t, W_ATT), F32),
                   jax.ShapeDtypeStruct((t, W_ATT), F32)),
        grid=(nseq, NH // (2 * npair)), in_specs=[_group_spec(1, npair), SEQ_SPEC, CROW_SPEC], out_specs=(_gheads_spec(npair), _gheads_spec(npair), _gheads_spec(npair)),
        scratch_shapes=_qkv_scratch(npair) + [pltpu.VMEM((npair, LANES, LP), BF16), pltpu.VMEM((2 * npair, LP, 1), F32),
                                      pltpu.VMEM((2 * npair, LANES, BQ), F32), pltpu.VMEM((2 * npair, 1, BQ), F32),
                                      pltpu.VMEM((2 * npair, 1, BQ), F32)],
        compiler_params=_cparams("parallel", "parallel"))(proj, c, crow)


def _fox_bwd(proj, c, crow, o32, lse, do, dproj, nseq, npair, name):
    t = nseq * LP
    chains = _chains(npair)

    def body(p_ref, c_ref, cr_ref, o_ref, lse_ref, do_ref, _, dp_ref, dc_ref,
             q_s, k_s, v_s, kt_s, ck_s, dqa_ref, dka_ref, dva_ref, rsum_ref):
        _load_qkv(p_ref, q_s, k_s, v_s)
        row, col = _tile_iotas()
        diag = row <= col
        lane = _lane_iota()
        heads = _head_masks()
        sub = lax.broadcasted_iota(jnp.int32, (LANES, BQ), 0)
        rows_of = (sub < HEAD, sub >= HEAD)
        group = pl.program_id(1)
        head0 = 2 * npair * group
        nc = len(chains)
        lane_all = lax.broadcasted_iota(jnp.int32, (LP, LANES), 1)
        for pp in range(npair):
            kt_s[pp] = k_s[pp].astype(F32).T.astype(BF16)
        for cidx in range(nc):
            ck_s[cidx] = jnp.sum(jnp.where(lane_all == head0 + cidx, c_ref[...], 0.0), axis=1, keepdims=True)
        dka_ref[...] = jnp.zeros_like(dka_ref)
        dva_ref[...] = jnp.zeros_like(dva_ref)

        @pl.when(group == 0)
        def _():
            dc_ref[...] = jnp.zeros_like(dc_ref)

        def qblock(i, _):
            dqa_ref[...] = jnp.zeros_like(dqa_ref)
            rsum_ref[...] = jnp.zeros_like(rsum_ref)
            qb = [q_s[pp, _blk(i), :] for pp in range(npair)]
            dob = [do_ref[_blk(i), _pair_cols(pp)] for pp in range(npair)]
            qt = [qb[pp].astype(F32).T.astype(BF16) for pp in range(npair)]
            dot = [dob[pp].astype(F32).T for pp in range(npair)]
            prod = [dot[pp] * o_ref[_blk(i), _pair_cols(pp)].T for pp in range(npair)]
            lse_t = [lse_ref[_blk(i), _pair_cols(pp)].T for pp in range(npair)]
            qt_m = [jnp.where(rows_of[h], qt[pp], jnp.zeros_like(qt[pp])) for pp, h in chains]
            dot_m = [jnp.where(rows_of[h], dot[pp], 0.0).astype(BF16) for pp, h in chains]
            q_m = [_only(heads[h], qb[pp]) for pp, h in chains]
            do_m = [_only(heads[h], dob[pp]) for pp, h in chains]
            cq = [_key_cols(cr_ref, head0 + cidx, i) for cidx in range(nc)]
            lse_i = [lse_t[pp][h:h + 1, :] for pp, h in chains]
            delta = [jnp.sum(jnp.where(rows_of[h], prod[pp], 0.0), axis=0, keepdims=True) for pp, h in chains]

            def tile(j, causal):
                keys = pl.ds(pl.multiple_of(j * BQ, BQ), BQ)
                kj = [k_s[pp, _blk(j), :] for pp in range(npair)]
                vj = [v_s[pp, _blk(j), :] for pp in range(npair)]
                z = [_dot(kj[pp], qt_m[cidx]) + (cq[cidx] - ck_s[cidx, _blk(j), :]) for cidx, (pp, _) in enumerate(chains)]
                if causal is not None:
                    z = [jnp.where(causal, zc, NEG) for zc in z]
                dpv = [_dot(vj[pp], dot_m[cidx]) for cidx, (pp, _) in enumerate(chains)]
                p = [jnp.exp(z[cidx] - lse_i[cidx]) for cidx in range(nc)]
                ds = [p[cidx] * (dpv[cidx] - delta[cidx]) for cidx in range(nc)]
                dsb = [d.astype(BF16) for d in ds]
                dq = [_dot(jnp.where(rows_of[h], kt_s[pp, :, keys], jnp.zeros((LANES, BQ), BF16)), dsb[cidx])
                      for cidx, (pp, h) in enumerate(chains)]
                dk = [_dot(dsb[cidx], q_m[cidx]) for cidx in range(nc)]
                dv = [_dot(p[cidx].astype(BF16), do_m[cidx]) for cidx in range(nc)]
                for pp in range(npair):
                    dqa_ref[pp] += dq[2 * pp] + dq[2 * pp + 1]
                    dka_ref[pp, _blk(j), :] += dk[2 * pp] + dk[2 * pp + 1]
                    dva_ref[pp, _blk(j), :] += dv[2 * pp] + dv[2 * pp + 1]
                col_sums = jnp.zeros((BQ, LANES), F32)
                for cidx in range(nc):
                    col_sums = col_sums + jnp.where(lane == head0 + cidx, jnp.sum(ds[cidx], axis=1, keepdims=True), 0.0)
                    rsum_ref[cidx] += jnp.sum(ds[cidx], axis=0, keepdims=True)
                dc_ref[_blk(j), :] = dc_ref[_blk(j), :] - col_sums

            def kblock(j, _):
                tile(j, None)
                return 0

            lax.fori_loop(0, i, kblock, 0)
            tile(i, diag)
            row_sums = jnp.zeros((LANES, BQ), F32)
            for cidx in range(nc):
                row_sums = row_sums + jnp.where(sub == head0 + cidx, rsum_ref[cidx], 0.0)
            dc_ref[_blk(i), :] += row_sums.T
            for pp in range(npair):
                dp_ref[_blk(i), pp * PAIR_W:pp * PAIR_W + LANES] = (dqa_ref[pp].T * SCALE).astype(BF16)
            return 0

        lax.fori_loop(0, NBLK, qblock, 0)
        for pp in range(npair):
            dp_ref[:, pp * PAIR_W + LANES:pp * PAIR_W + 2 * LANES] = dka_ref[pp].astype(BF16)
            dp_ref[:, pp * PAIR_W + 2 * LANES:pp * PAIR_W + 3 * LANES] = dva_ref[pp].astype(BF16)

    return pl.pallas_call(
        body, name=name,
        out_shape=(jax.ShapeDtypeStruct(dproj.shape, BF16), jax.ShapeDtypeStruct((t, LANES), F32)),
        grid=(nseq, NH // (2 * npair)),
        in_specs=[_group_spec(1, npair), SEQ_SPEC, CROW_SPEC, _gheads_spec(npair), _gheads_spec(npair), _gheads_spec(npair), ANY],
        out_specs=(_group_spec(1, npair), SEQ_SPEC),
        input_output_aliases={6: 0},
        scratch_shapes=_qkv_scratch(npair) + [pltpu.VMEM((npair, LANES, LP), BF16), pltpu.VMEM((2 * npair, LP, 1), F32),
                                      pltpu.VMEM((npair, LANES, BQ), F32), pltpu.VMEM((npair, LP, LANES), F32),
                                      pltpu.VMEM((npair, LP, LANES), F32), pltpu.VMEM((2 * npair, 1, BQ), F32)],
        compiler_params=_cparams("parallel", "arbitrary"))(proj, c, crow, o32, lse, do, dproj)


def _adamw_math(w, g, m, v):
    m = B1 * m + (1.0 - B1) * g
    v = B2 * v + (1.0 - B2) * (g * g)
    m_hat = m / (1.0 - B1 ** STEP)
    v_hat = v / (1.0 - B2 ** STEP)
    delta = -LR * (m_hat / (jnp.sqrt(v_hat) + EPS) + WD * w)
    return delta, m, v


def _sum_adamw(parts, w, m, v, tr, name):
    rows, cols = w.shape
    cp = parts.shape[2]
    assert rows % tr == 0 and parts.shape[1] == rows

    def body(p_ref, w_ref, m_ref, v_ref, g_ref, d_ref, nm_ref, nv_ref):
        gsum = p_ref[0].astype(F32)
        for s in range(1, N_DEV):
            gsum = gsum + p_ref[s].astype(F32)
        gsum = gsum[:, :cols]
        d, nm, nv = _adamw_math(w_ref[...], gsum, m_ref[...], v_ref[...])
        g_ref[...] = gsum
        d_ref[...] = d
        nm_ref[...] = nm
        nv_ref[...] = nv

    blk = pl.BlockSpec((tr, cols), lambda i: (i, 0))
    out = jax.ShapeDtypeStruct((rows, cols), F32)
    return pl.pallas_call(
        body, name=name, out_shape=(out, out, out, out), grid=(rows // tr,),
        in_specs=[pl.BlockSpec((N_DEV, tr, cp), lambda i: (0, i, 0)), blk, blk, blk],
        out_specs=(blk, blk, blk, blk), compiler_params=_cparams("parallel"))(parts, w, m, v)


def _local_step(x, tgt, meta, tgt_front, g_mix, b_forget, g_ffn, g_final, first_weights, late_weights, early_grads, last_grad):
    nseq = x.shape[0]
    t = nseq * LP
    tm = LP // 2
    mm = functools.partial(_matmul, tm=tm)

    h0 = _pad_rows(meta, x, nseq, "pad_x").reshape(t, D)
    tgt_p = _pad_rows(tgt_front, tgt, nseq, "pad_target").reshape(t, D)
    bf = jnp.pad(b_forget.reshape(1, NH), ((0, 0), (0, LANES - NH)))

    n1 = _norm_fwd(h0, g_mix, "norm1")
    w_in_p, started = first_weights(n1)
    proj = mm(n1, w_in_p, out_dtype=F32, tn=1792, tk=D, after=started, name="in_proj")
    c = _gate_fwd(proj, bf, nseq, "gate_fwd")
    crow = c[:, :NH].reshape(nseq, LP, NH).transpose(0, 2, 1)
    o_sb, rs = _sb_fwd(proj, nseq, 2, "sb_fwd")
    o_fx, o_fx32, lse = _fox_fwd(proj, c, crow, nseq, 2, "fox_fwd")
    w_bsb, w_bfx, w_out, w_up_i, cw_i, w_down = late_weights(o_fx)
    p_sb = mm(o_sb, w_bsb, out_dtype=F32, tn=D, tk=W_ATT, name="branch_sb")
    p_fx = mm(o_fx, w_bfx, out_dtype=F32, tn=D, tk=W_ATT, name="branch_fox")
    merged = _merge_fwd(p_sb, p_fx, proj, "merge_fwd")
    rows = functools.partial(_matmul_rows, tm=LP // 4)
    h1, n2 = rows(merged, w_out, [h0], [g_ffn], _residual_norm, [F32, BF16], [], tk=D, name="out_proj_norm2")
    u = mm(n2, w_up_i, out_dtype=F32, tn=1408, tk=D, name="up_proj")
    act = _conv_glu_fwd(u, cw_i, nseq, "conv_glu_fwd")

    dh2, dh2b, loss, dg_final = rows(act, w_down, [h1, tgt_p], [g_final], _loss_head, [F32, BF16],
                                     [(8, LANES), (1, D)], tk=D_FF, name="down_proj_loss")
    d_down = _matmul(act, dh2b, out_dtype=BF16, tm=1408, tn=D, tk=LP, ta=True, name="d_w_down")
    dact = mm(dh2b, w_down, out_dtype=F32, tn=1408, tk=D, tb=True, name="d_act")
    du, d_cw = _conv_glu_bwd(u, cw_i, dact, nseq, "conv_glu_bwd")
    d_up = _matmul(n2, du, out_dtype=BF16, tm=D, tn=1408, tk=LP, ta=True, name="d_w_up")
    dh1, dh1b, dg_ffn = rows(du, w_up_i, [h1, dh2], [g_ffn], _residual_norm_bwd, [F32, BF16], [(1, D)],
                             tk=D_FF, tb=True, name="d_n2_norm2_bwd")
    d_out = _matmul(merged, dh1b, out_dtype=BF16, tm=D, tn=D, tk=LP, ta=True, name="d_w_out")
    dmerged = mm(dh1b, w_out, out_dtype=F32, tn=D, tk=D, tb=True, name="d_merged")
    dp_sb, dproj = _merge_bwd(dmerged, p_sb, proj, None, 0, "merge_bwd_sb")
    dp_fx, dproj = _merge_bwd(dmerged, p_fx, proj, dproj, 1, "merge_bwd_fox")
    d_bsb = _matmul(o_sb, dp_sb, out_dtype=BF16, tm=W_ATT, tn=D, tk=LP, ta=True, name="d_w_branch_sb")
    d_bfx = _matmul(o_fx, dp_fx, out_dtype=BF16, tm=W_ATT, tn=D, tk=LP, ta=True, name="d_w_branch_fox")
    do_sb = mm(dp_sb, w_bsb, out_dtype=BF16, tn=W_ATT, tk=D, tb=True, name="d_o_sb")
    do_fx = mm(dp_fx, w_bfx, out_dtype=BF16, tn=W_ATT, tk=D, tb=True, name="d_o_fox")
    sent = early_grads(dict(w_branch_sb=d_bsb, w_branch_fox=d_bfx, w_out=d_out, w_up=d_up, conv_w=d_cw, w_down=d_down))
    dproj = _sb_bwd(proj, do_sb, rs, dproj, sent, nseq, 2, "sb_bwd")
    dproj, dc = _fox_bwd(proj, c, crow, o_fx32, lse, do_fx, dproj, nseq, 2, "fox_bwd")
    dproj, d_bf = _gate_bwd(proj, bf, dc, dproj, nseq, "gate_bwd")
    d_in = _matmul(n1, dproj, out_dtype=BF16, tm=D, tn=1792, tk=LP, ta=True, name="d_w_in")
    dh0, dg_mix = rows(dproj, w_in_p, [h0, dh1], [g_mix], _residual_norm_bwd_f32, [F32], [(1, D)],
                       tk=IN_P // 2, tb=True, after=last_grad(d_in), name="d_n1_norm1_bwd")
    dh0 = dh0.reshape(nseq, LP, D)
    grads = dict(meta_tokens=jnp.sum(dh0[:, :N_META], axis=0), norm_mix_g=dg_mix, b_forget=d_bf[:, :NH],
                 norm_ffn_g=dg_ffn, norm_final_g=dg_final)
    return loss[0, 0], _real_rows(dh0, nseq, "grad_x"), grads


REPL = (("norm_mix_g", D), ("norm_ffn_g", D), ("norm_final_g", D), ("b_forget", LANES))
REPL_ROWS = 32
META_ROWS = N_META * D // LANES


def _pack_repl(tree):
    rows = [jnp.pad(tree[name].reshape(-1), (0, n - tree[name].size)).reshape(-1, LANES) for name, n in REPL]
    packed = jnp.concatenate(rows, axis=0)
    return jnp.pad(packed, ((0, REPL_ROWS - packed.shape[0]), (0, 0)))


def _unpack_repl(packed, shapes):
    out, r = {}, 0
    for name, n in REPL:
        size = 1
        for s in shapes[name]:
            size *= s
        out[name] = packed[r:r + n // LANES].reshape(-1)[:size].reshape(shapes[name])
        r += n // LANES
    return out


def kernel(x, meta_tokens, norm_mix_g, w_in, b_forget, w_branch_sb, w_branch_fox, w_out, norm_ffn_g, w_up, conv_w, w_down, norm_final_g, loss_target, m_meta_tokens, m_norm_mix_g, m_w_in, m_b_forget, m_w_branch_sb, m_w_branch_fox, m_w_out, m_norm_ffn_g, m_w_up, m_conv_w, m_w_down, m_norm_final_g, v_meta_tokens, v_norm_mix_g, v_w_in, v_b_forget, v_w_branch_sb, v_w_branch_fox, v_w_out, v_norm_ffn_g, v_w_up, v_conv_w, v_w_down, v_norm_final_g):
    w = dict(meta_tokens=meta_tokens, norm_mix_g=norm_mix_g, w_in=w_in, b_forget=b_forget, w_branch_sb=w_branch_sb,
             w_branch_fox=w_branch_fox, w_out=w_out, norm_ffn_g=norm_ffn_g, w_up=w_up, conv_w=conv_w, w_down=w_down,
             norm_final_g=norm_final_g)
    m = dict(meta_tokens=m_meta_tokens, norm_mix_g=m_norm_mix_g, w_in=m_w_in, b_forget=m_b_forget,
             w_branch_sb=m_w_branch_sb, w_branch_fox=m_w_branch_fox, w_out=m_w_out, norm_ffn_g=m_norm_ffn_g,
             w_up=m_w_up, conv_w=m_conv_w, w_down=m_w_down, norm_final_g=m_norm_final_g)
    v = dict(meta_tokens=v_meta_tokens, norm_mix_g=v_norm_mix_g, w_in=v_w_in, b_forget=v_b_forget,
             w_branch_sb=v_w_branch_sb, w_branch_fox=v_w_branch_fox, w_out=v_w_out, norm_ffn_g=v_norm_ffn_g,
             w_up=v_w_up, conv_w=v_conv_w, w_down=v_w_down, norm_final_g=v_norm_final_g)
    shapes = {k: a.shape for k, a in w.items()}
    sharded = ("w_in", "w_branch_sb", "w_branch_fox", "w_out", "w_up", "w_down", "conv_w", "meta_tokens")
    mat = lambda tree, name: tree[name].reshape(tree[name].shape[-2:])

    def lane_pad(a, width):
        return jnp.pad(a, ((0, 0), (0, width - a.shape[1])))

    late = ("w_branch_sb", "w_branch_fox", "w_out", "w_up", "w_down", "conv_w")
    pending_w = {}
    g_meta, = _all_gather([mat(w, "meta_tokens")], "gather_meta")
    pending_w["in"], in_started = _remote_start(
        [lane_pad(mat(w, "w_in").astype(BF16), SHARD_P)], False, g_meta, "gather_w_in_start")
    meta_full = g_meta.transpose(1, 0, 2).reshape(N_META, D) + in_started[0, 0]

    def first_weights(after):
        g_in, = _remote_wait(pending_w["in"], after, "gather_w_in_wait")
        pending_w["late"], started = _remote_start(
            [mat(w, "w_branch_sb").astype(BF16), mat(w, "w_branch_fox").astype(BF16), mat(w, "w_out").astype(BF16),
             lane_pad(mat(w, "w_up").astype(BF16), SHARD_P), mat(w, "w_down").astype(BF16), mat(w, "conv_w")],
            False, g_in, "gather_late_start")
        w_in_p = _relayout(g_in, 1, IN_P, _gathered_to_full(IN_SHARD, _in_padded_to_orig), BF16, 256, "w_in_cols")[0]
        return w_in_p, started

    def late_weights(after):
        g_bsb, g_bfx, g_out, g_up, g_down, g_cw = _remote_wait(pending_w["late"], after, "gather_late_wait")
        w_up_i = _relayout(g_up, 1, 2 * D_FF, _gathered_to_full(UP_SHARD, _up_inter_to_orig), BF16, 256, "w_up_cols")[0]
        w_bsb = _relayout(g_bsb, 1, D, _gathered_to_full(ATT_SHARD, lambda d: d), BF16, 256, "w_bsb_cols")[0]
        w_bfx = _relayout(g_bfx, 1, D, _gathered_to_full(ATT_SHARD, lambda d: d), BF16, 256, "w_bfx_cols")[0]
        cw_full = g_cw.transpose(1, 0, 2).reshape(3, 2 * D_FF)
        cw_i = cw_full.reshape(3, 2, D_FF // FFC, FFC).transpose(0, 2, 1, 3).reshape(3, 2 * D_FF)
        return w_bsb, w_bfx, g_out.reshape(D, D), w_up_i, cw_i, g_down.reshape(D_FF, D)

    pending_g = {}

    def early_grads(g):
        d_cw = g["conv_w"].reshape(3, D_FF // FFC, 2, FFC).transpose(0, 2, 1, 3).reshape(3, 2 * D_FF)
        pending_g["early"], sent = _remote_start(
            [_relayout(g["w_branch_sb"][None], N_DEV, ATT_SHARD, _full_to_shards(ATT_SHARD, lambda c: c), BF16, 256, "d_w_bsb_shards"),
             _relayout(g["w_branch_fox"][None], N_DEV, ATT_SHARD, _full_to_shards(ATT_SHARD, lambda c: c), BF16, 256, "d_w_bfx_shards"),
             g["w_out"].reshape(N_DEV, D // N_DEV, D),
             _relayout(g["w_up"][None], N_DEV, SHARD_P, _full_to_shards(UP_SHARD, _UP_ORIG_TO_INTER.get), BF16, 256, "d_w_up_shards"),
             g["w_down"].reshape(N_DEV, D_FF // N_DEV, D),
             d_cw.reshape(3, N_DEV, UP_SHARD).transpose(1, 0, 2)], True, g["w_out"], "exchange_early_start")
        return sent

    def last_grad(d_in):
        shards = _relayout(d_in[None], N_DEV, SHARD_P, _full_to_shards(IN_SHARD, _IN_ORIG_TO_PADDED.get), BF16, 256, "d_w_in_shards")
        pending_g["last"], sent = _remote_start([shards], True, shards, "exchange_last_start")
        return sent

    loss, grad_x, grads = _local_step(
        x, loss_target, meta_full, jnp.zeros((N_META, D), F32) + in_started[0, 0], norm_mix_g.reshape(1, D), b_forget,
        norm_ffn_g.reshape(1, D), norm_final_g.reshape(1, D), first_weights, late_weights, early_grads, last_grad)

    small = jnp.concatenate([_pack_repl(grads), grads["meta_tokens"].reshape(META_ROWS, LANES)], axis=0)
    small, = _all_gather([small], "gather_small_grads")
    me_idx = 4 * lax.axis_index("x") + 2 * lax.axis_index("y") + lax.axis_index("c")
    p_meta = lax.dynamic_slice_in_dim(small[:, REPL_ROWS:].reshape(N_DEV, N_META, D), me_idx * ATT_SHARD, ATT_SHARD, axis=2)

    p_in, = _remote_wait(pending_g["last"], small, "exchange_last_wait")
    parts = dict(zip(late, _remote_wait(pending_g["early"], p_in, "exchange_early_wait")), w_in=p_in, meta_tokens=p_meta)
    tiles = dict(w_in=256, w_branch_sb=256, w_branch_fox=256, w_out=D // N_DEV, w_up=256, w_down=D_FF // N_DEV,
                 conv_w=3, meta_tokens=N_META)
    new = {name: _sum_adamw(parts[name], mat(w, name), mat(m, name), mat(v, name), tiles[name], "adamw_" + name)
           for name in sharded}

    routs = _sum_adamw(small[:, :REPL_ROWS], _pack_repl(w), _pack_repl(m), _pack_repl(v), REPL_ROWS, "adamw_replicated")
    repl = [_unpack_repl(o, shapes) for o in routs]

    result = [lax.psum(loss, ("x", "y", "c")), grad_x]
    for k in range(4):
        for name in w:
            result.append(new[name][k].reshape(shapes[name]) if name in new else repl[k][name])
    return tuple(result)
```

```python
import functools

import jax
import jax.numpy as jnp
from jax import lax
from jax.experimental import pallas as pl
from jax.experimental.pallas import tpu as pltpu

F32 = jnp.float32
BF16 = jnp.bfloat16

N_DEV = 8
LANES = 128
D = 1024
N_META = 16
SEQ = 2048
L_REAL = N_META + SEQ
LP = 2304
BQ = 256
NBLK = LP // BQ
HEAD = 64
NH = 8
W_ATT = NH * HEAD
PAIR_W = 3 * LANES
D_FF = 2816
IN_COLS = 5128
QKV = 6 * W_ATT
IN_P = 5376
GATE_COL = QKV
F_COL = QKV + 2 * D
FFC = 256
RMS_EPS = 1e-6
LR, B1, B2, EPS, WD, STEP = 0.001, 0.9, 0.999, 1e-08, 0.01, 10
VMEM_LIMIT = 56 * 1024 * 1024

MESH = pl.DeviceIdType.MESH
ANY = pl.BlockSpec(memory_space=pl.ANY)


def _cparams(*sem):
    return pltpu.CompilerParams(dimension_semantics=sem if sem else None, vmem_limit_bytes=VMEM_LIMIT)


def _all_gather(xs, name):
    n = len(xs)

    def body(*refs):
        x_refs, out_refs = refs[:n], refs[n:2 * n]
        send_sems, recv_sems, local_sems = refs[2 * n:]
        mx, my, mc = lax.axis_index("x"), lax.axis_index("y"), lax.axis_index("c")
        me, sibling = (mx, my, mc), (mx, my, 1 - mc)
        chips = [(1 - mx, my), (mx, 1 - my), (1 - mx, 1 - my)]

        def copy(a, k, block, to, own=False):
            px, py, pc = block
            slot = out_refs[a].at[4 * px + 2 * py + pc]
            return pltpu.make_async_remote_copy(
                src_ref=x_refs[a] if own else slot, dst_ref=slot,
                send_sem=send_sems.at[7 * a + k], recv_sem=recv_sems.at[7 * a + k],
                device_id=to, device_id_type=MESH)

        mine = [pltpu.make_async_copy(x_refs[a], out_refs[a].at[4 * mx + 2 * my + mc], local_sems.at[a]) for a in range(n)]
        for cp in mine:
            cp.start()
        first = []
        for a in range(n):
            first.append(copy(a, 0, me, sibling, own=True))
            first += [copy(a, 1 + j, me, (*chip, mc), own=True) for j, chip in enumerate(chips)]
        for cp in first:
            cp.start()
        passed = []
        for j, chip in enumerate(chips):
            for a in range(n):
                copy(a, 1 + j, (*chip, mc), me).wait_recv()
                fwd = copy(a, 4 + j, (*chip, mc), sibling)
                fwd.start()
                passed.append(fwd)
        for a in range(n):
            copy(a, 0, sibling, me).wait_recv()
            for j, chip in enumerate(chips):
                copy(a, 4 + j, (*chip, 1 - mc), me).wait_recv()
        for cp in first + passed:
            cp.wait_send()
        for cp in mine:
            cp.wait()

    return pl.pallas_call(
        body, name=name,
        out_shape=tuple(jax.ShapeDtypeStruct((N_DEV,) + x.shape, x.dtype) for x in xs),
        in_specs=[ANY] * n, out_specs=tuple([ANY] * n),
        scratch_shapes=[pltpu.SemaphoreType.DMA((7 * n,)), pltpu.SemaphoreType.DMA((7 * n,)),
                        pltpu.SemaphoreType.DMA((n,))],
    )(*xs)


HBM = pl.BlockSpec(memory_space=pltpu.HBM)
SEM = pl.BlockSpec(memory_space=pltpu.SEMAPHORE)
EFFECT = pltpu.SideEffectType.DATAFLOW_SIDE_EFFECTING


def _peer_copies(src_refs, land_refs, send_sems, recv_sems, per_peer):
    mx, my, mc = lax.axis_index("x"), lax.axis_index("y"), lax.axis_index("c")
    me_idx = 4 * mx + 2 * my + mc
    copies = []
    for k in range(1, N_DEV):
        px, py, pc = mx ^ (k >> 2), my ^ ((k >> 1) & 1), mc ^ (k & 1)
        for a, (src, land) in enumerate(zip(src_refs, land_refs)):
            copies.append(pltpu.make_async_remote_copy(
                src_ref=src.at[4 * px + 2 * py + pc] if per_peer else src, dst_ref=land.at[me_idx],
                send_sem=send_sems.at[7 * a + k - 1], recv_sem=recv_sems.at[7 * a + k - 1],
                device_id=(px, py, pc), device_id_type=MESH))
    return me_idx, copies


def _remote_start(srcs, per_peer, after, name):
    n = len(srcs)
    lands = [lax.empty(s.shape if per_peer else (N_DEV,) + s.shape, s.dtype) for s in srcs]

    def body(*refs):
        src_refs, land_refs = refs[:n], refs[n:2 * n]
        send_sems, recv_sems = refs[2 * n + 1:2 * n + 3]
        token = refs[4 * n + 3]
        stage, local_sems = refs[4 * n + 4:5 * n + 4], refs[5 * n + 4]
        me_idx, copies = _peer_copies(src_refs, land_refs, send_sems, recv_sems, per_peer)
        for cp in copies:
            cp.start()
        own = [src_refs[a].at[me_idx] if per_peer else src_refs[a] for a in range(n)]
        for hop in ([(own[a], stage[a]) for a in range(n)], [(stage[a], land_refs[a].at[me_idx]) for a in range(n)]):
            cps = [pltpu.make_async_copy(s, d, local_sems.at[a]) for a, (s, d) in enumerate(hop)]
            for cp in cps:
                cp.start()
            for cp in cps:
                cp.wait()
        token[...] = jnp.zeros_like(token)

    thru = [pltpu.HBM(a.shape, a.dtype) for a in list(srcs) + lands]
    out = pl.pallas_call(
        body, name=name,
        out_shape=(pltpu.SemaphoreType.DMA((7 * n,)), pltpu.SemaphoreType.DMA((7 * n,)), *thru,
                   jax.ShapeDtypeStruct((8, LANES), F32)),
        in_specs=[HBM] * (2 * n) + [ANY],
        out_specs=(SEM, SEM, *([HBM] * (2 * n)), pl.BlockSpec(memory_space=pltpu.VMEM)),
        input_output_aliases={i: 2 + i for i in range(2 * n)},
        scratch_shapes=[pltpu.VMEM(s.shape[1:] if per_peer else s.shape, s.dtype) for s in srcs]
        + [pltpu.SemaphoreType.DMA((n,))],
        compiler_params=pltpu.CompilerParams(has_side_effects=EFFECT),
    )(*[pltpu.with_memory_space_constraint(a, pltpu.HBM) for a in list(srcs) + lands], after)
    return dict(sems=out[:2], bufs=out[2:2 * n + 2], per_peer=per_peer), out[-1]


def _remote_wait(pending, after, name):
    bufs = pending["bufs"]
    n = len(bufs) // 2
    per_peer = pending["per_peer"]

    def body(*refs):
        src_refs, land_refs = refs[:n], refs[n:2 * n]
        send_sems, recv_sems = refs[2 * n:2 * n + 2]
        _, copies = _peer_copies(src_refs, land_refs, send_sems, recv_sems, per_peer)
        for cp in copies:
            cp.wait_send()
        for cp in copies:
            cp.wait_recv()

    out = pl.pallas_call(
        body, name=name, out_shape=tuple(pltpu.HBM(a.shape, a.dtype) for a in bufs),
        in_specs=[HBM] * (2 * n) + [SEM, SEM, ANY], out_specs=tuple([HBM] * (2 * n)),
        input_output_aliases={i: i for i in range(2 * n)},
        compiler_params=pltpu.CompilerParams(has_side_effects=EFFECT),
    )(*bufs, *pending["sems"], after)
    return out[n:]


ROWS_PER_COPY = 256


def _pad_rows(front, body_rows, nseq, name):
    tail = LP - L_REAL
    nblk = SEQ // ROWS_PER_COPY

    def body(f_ref, b_ref, o_ref, z_ref, sems):
        s, i = pl.program_id(0), pl.program_id(1)
        rows = pltpu.make_async_copy(b_ref, o_ref.at[pl.ds(s, 1), pl.ds(N_META + i * ROWS_PER_COPY, ROWS_PER_COPY)], sems.at[0])
        rows.start()

        @pl.when(i == 0)
        def _():
            z_ref[...] = jnp.zeros_like(z_ref)
            head = pltpu.make_async_copy(f_ref, o_ref.at[s, pl.ds(0, N_META)], sems.at[1])
            zeros = pltpu.make_async_copy(z_ref, o_ref.at[s, pl.ds(L_REAL, tail)], sems.at[2])
            head.start()
            zeros.start()
            head.wait()
            zeros.wait()

        rows.wait()

    return pl.pallas_call(
        body, name=name, out_shape=jax.ShapeDtypeStruct((nseq, LP, D), F32), grid=(nseq, nblk),
        in_specs=[pl.BlockSpec((N_META, D), lambda s, i: (0, 0)), pl.BlockSpec((1, ROWS_PER_COPY, D), lambda s, i: (s, i, 0))],
        out_specs=ANY,
        scratch_shapes=[pltpu.VMEM((tail, D), F32), pltpu.SemaphoreType.DMA((3,))],
        compiler_params=_cparams("arbitrary", "arbitrary"))(front, body_rows)


def _real_rows(h, nseq, name):
    nblk = SEQ // ROWS_PER_COPY

    def body(h_ref, o_ref, sem):
        s, i = pl.program_id(0), pl.program_id(1)
        rows = pltpu.make_async_copy(h_ref.at[pl.ds(s, 1), pl.ds(N_META + i * ROWS_PER_COPY, ROWS_PER_COPY)], o_ref, sem)
        rows.start()
        rows.wait()

    return pl.pallas_call(
        body, name=name, out_shape=jax.ShapeDtypeStruct((nseq, SEQ, D), F32), grid=(nseq, nblk),
        in_specs=[ANY], out_specs=pl.BlockSpec((1, ROWS_PER_COPY, D), lambda s, i: (s, i, 0)),
        scratch_shapes=[pltpu.SemaphoreType.DMA],
        compiler_params=_cparams("arbitrary", "arbitrary"))(h)


def _plan_cols(n_q, n_dcols, src_of):
    plan = {}
    for q in range(n_q):
        for dblk in range(n_dcols // LANES):
            segs, key, start = [], None, 0
            for lane in range(LANES + 1):
                new = None
                if lane < LANES:
                    src = src_of(q, dblk * LANES + lane)
                    if src is not None:
                        new = (src[0], src[1] // LANES, (lane - src[1] % LANES) % LANES)
                if new != key:
                    if key is not None:
                        segs.append((*key, start, lane))
                    key, start = new, lane
            plan[(q, dblk)] = segs
    return plan


def _relayout(src, n_q, n_dcols, src_of, out_dtype, tr, name):
    n_p, rows, scols = src.shape
    plan = _plan_cols(n_q, n_dcols, src_of)

    def body(s_ref, d_ref):
        lane = lax.broadcasted_iota(jnp.int32, (tr, LANES), 1)
        for (q, dblk), segs in plan.items():
            acc = jnp.zeros((tr, LANES), F32)
            for p, sblk, rot, lo, hi in segs:
                x = s_ref[p, :, sblk * LANES:(sblk + 1) * LANES].astype(F32)
                if rot:
                    x = pltpu.roll(x, rot, 1)
                acc = x if (lo, hi) == (0, LANES) else jnp.where((lane >= lo) & (lane < hi), x, acc)
            d_ref[q, :, dblk * LANES:(dblk + 1) * LANES] = acc.astype(out_dtype)

    return pl.pallas_call(
        body, name=name, out_shape=jax.ShapeDtypeStruct((n_q, rows, n_dcols), out_dtype), grid=(rows // tr,),
        in_specs=[pl.BlockSpec((n_p, tr, scols), lambda i: (0, i, 0))],
        out_specs=pl.BlockSpec((n_q, tr, n_dcols), lambda i: (0, i, 0)),
        compiler_params=_cparams("parallel"))(src)


def _in_padded_to_orig(d):
    if d < QKV:
        kind, r = divmod(d, 4 * PAIR_W)
        pair, r = divmod(r, PAIR_W)
        part, r = divmod(r, LANES)
        return kind * 3 * W_ATT + part * W_ATT + pair * LANES + r
    if d < F_COL:
        return d + NH
    if d < F_COL + NH:
        return d - 2 * D
    return None


_IN_ORIG_TO_PADDED = {_in_padded_to_orig(d): d for d in range(IN_P) if _in_padded_to_orig(d) is not None}


def _up_inter_to_orig(d):
    j, r = divmod(d, 2 * FFC)
    part, r = divmod(r, FFC)
    return part * D_FF + j * FFC + r


_UP_ORIG_TO_INTER = {_up_inter_to_orig(d): d for d in range(2 * D_FF)}
IN_SHARD = IN_COLS // N_DEV
UP_SHARD = 2 * D_FF // N_DEV
SHARD_P = 768
ATT_SHARD = D // N_DEV


def _gathered_to_full(n_shard, to_orig):
    def src_of(q, d):
        c = to_orig(d)
        return None if c is None else (c // n_shard, c % n_shard)
    return src_of


def _full_to_shards(n_shard, from_orig):
    def src_of(q, d):
        return (0, from_orig(q * n_shard + d)) if d < n_shard else None
    return src_of


def _matmul(a, b, *, out_dtype, tm, tn, tk, ta=False, tb=False, after=None, name):
    if ta:
        kdim, m = a.shape
    else:
        m, kdim = a.shape
    n = b.shape[0] if tb else b.shape[1]
    assert m % tm == 0 and n % tn == 0 and kdim % tk == 0, (name, a.shape, b.shape, tm, tn, tk)
    nk = kdim // tk

    def body(a_ref, b_ref, *rest):
        o_ref, scratch = rest[len(extra)], rest[len(extra) + 1:]
        av, bv = a_ref[...], b_ref[...]
        if ta:
            p = lax.dot_general(av, bv, (((0,), (0,)), ((), ())), preferred_element_type=F32)
        elif tb:
            p = lax.dot_general(av, bv, (((1,), (1,)), ((), ())), preferred_element_type=F32)
        else:
            p = jnp.dot(av, bv, preferred_element_type=F32)
        if nk == 1:
            o_ref[...] = p.astype(o_ref.dtype)
        else:
            acc_ref, = scratch
            k = pl.program_id(2)

            @pl.when(k == 0)
            def _():
                acc_ref[...] = p

            @pl.when(k > 0)
            def _():
                acc_ref[...] += p

            @pl.when(k == nk - 1)
            def _():
                o_ref[...] = acc_ref[...].astype(o_ref.dtype)

    extra = [] if after is None else [after]
    a_spec = pl.BlockSpec((tk, tm), lambda i, j, k: (k, i)) if ta else pl.BlockSpec((tm, tk), lambda i, j, k: (i, k))
    b_spec = pl.BlockSpec((tn, tk), lambda i, j, k: (j, k)) if tb else pl.BlockSpec((tk, tn), lambda i, j, k: (k, j))
    return pl.pallas_call(
        body, name=name,
        out_shape=jax.ShapeDtypeStruct((m, n), out_dtype),
        grid=(m // tm, n // tn, nk),
        in_specs=[a_spec, b_spec] + [ANY] * len(extra),
        out_specs=pl.BlockSpec((tm, tn), lambda i, j, k: (i, j)),
        scratch_shapes=[] if nk == 1 else [pltpu.VMEM((tm, tn), F32)],
        compiler_params=_cparams("parallel", "parallel", "arbitrary"),
    )(a, b, *extra)


TR = 288


def _rms(h):
    return lax.rsqrt(jnp.mean(h * h, axis=-1, keepdims=True) + RMS_EPS)


def _norm_fwd(h, g, name):
    t = h.shape[0]
    row = pl.BlockSpec((TR, D), lambda i: (i, 0))

    def body(h_ref, g_ref, n_ref):
        hv = h_ref[...]
        n_ref[...] = ((hv * _rms(hv)) * g_ref[...]).astype(BF16)

    return pl.pallas_call(
        body, name=name, out_shape=jax.ShapeDtypeStruct((t, D), BF16), grid=(t // TR,),
        in_specs=[row, pl.BlockSpec((1, D), lambda i: (0, 0))], out_specs=row, compiler_params=_cparams("parallel"))(h, g)


EPI_ROWS = 144


def _matmul_rows(a, b, rows_in, vecs_in, epilogue, row_outs, sum_outs, *, tm, tk, tb=False, after=None, name):
    m, kdim = a.shape
    assert (b.shape[0] if tb else b.shape[1]) == D and m % tm == 0 and kdim % tk == 0 and tm % EPI_ROWS == 0
    nk = kdim // tk
    n_r, n_v, n_ro, n_so = len(rows_in), len(vecs_in), len(row_outs), len(sum_outs)
    extra = [] if after is None else [after]

    def body(a_ref, b_ref, *rest):
        r_refs, v_refs = rest[:n_r], rest[n_r:n_r + n_v]
        outs = rest[n_r + n_v + len(extra):]
        ro_refs, so_refs, acc_ref = outs[:n_ro], outs[n_ro:n_ro + n_so], outs[n_ro + n_so]
        i, k = pl.program_id(0), pl.program_id(1)
        if tb:
            p = lax.dot_general(a_ref[...], b_ref[...], (((1,), (1,)), ((), ())), preferred_element_type=F32)
        else:
            p = jnp.dot(a_ref[...], b_ref[...], preferred_element_type=F32)

        @pl.when(k == 0)
        def _():
            acc_ref[...] = p

        @pl.when(k > 0)
        def _():
            acc_ref[...] += p

        @pl.when(k == nk - 1)
        def _():
            vecs = [v[...] for v in v_refs]

            def step(c, sums):
                rows = pl.ds(pl.multiple_of(c * EPI_ROWS, 8), EPI_ROWS)
                tiles, terms = epilogue(i * tm + c * EPI_ROWS, acc_ref[rows, :], *[r[rows, :] for r in r_refs], *vecs)
                for o, tile in zip(ro_refs, tiles):
                    o[rows, :] = tile.astype(o.dtype)
                return tuple(s + term for s, term in zip(sums, terms))

            sums = lax.fori_loop(0, tm // EPI_ROWS, step, tuple(jnp.zeros(s, F32) for s in sum_outs))

            @pl.when(i == 0)
            def _():
                for o in so_refs:
                    o[...] = jnp.zeros_like(o)

            for o, s in zip(so_refs, sums):
                o[...] += s

    row = pl.BlockSpec((tm, D), lambda i, k: (i, 0))
    b_spec = pl.BlockSpec((D, tk), lambda i, k: (0, k)) if tb else pl.BlockSpec((tk, D), lambda i, k: (k, 0))
    return pl.pallas_call(
        body, name=name,
        out_shape=tuple([jax.ShapeDtypeStruct((m, D), dt) for dt in row_outs] + [jax.ShapeDtypeStruct(s, F32) for s in sum_outs]),
        grid=(m // tm, nk),
        in_specs=[pl.BlockSpec((tm, tk), lambda i, k: (i, k)), b_spec] + [row] * n_r
        + [pl.BlockSpec((1, D), lambda i, k: (0, 0))] * n_v + [ANY] * len(extra),
        out_specs=tuple([row] * n_ro + [pl.BlockSpec(s, lambda i, k: (0, 0)) for s in sum_outs]),
        scratch_shapes=[pltpu.VMEM((tm, D), F32)],
        compiler_params=_cparams("arbitrary", "arbitrary"))(a, b, *rows_in, *vecs_in, *extra)


def _residual_norm(row0, acc, h, g):
    hv = h + acc
    return (hv, (hv * _rms(hv)) * g), ()


def _rms_bwd_math(hv, dn, gv):
    r = _rms(hv)
    hr = hv * r
    dng = dn * gv
    dh = r * (dng - hr * jnp.mean(dng * hr, axis=-1, keepdims=True))
    return dh, dn * hr


def _loss_head(row0, acc, h1, tgt, g):
    hv = h1 + acc
    hr = hv * _rms(hv)
    pos = row0 % LP + lax.broadcasted_iota(jnp.int32, (EPI_ROWS, 1), 0)
    valid = (pos >= N_META) & (pos < L_REAL)
    err = jnp.where(valid, hr * g - tgt, 0.0)
    part = 0.5 * jnp.sum(jnp.mean(err * err, axis=-1, keepdims=True))
    dy = err * (1.0 / D)
    dh, dgrow = _rms_bwd_math(hv, dy, g)
    return (dh, dh), (jnp.full((8, LANES), part, F32), jnp.sum(dgrow, axis=0, keepdims=True))


def _residual_norm_bwd(row0, acc, h, dres, g):
    dh, dgrow = _rms_bwd_math(h, acc, g)
    dh = dh + dres
    return (dh, dh), (jnp.sum(dgrow, axis=0, keepdims=True),)


def _residual_norm_bwd_f32(row0, acc, h, dres, g):
    tiles, sums = _residual_norm_bwd(row0, acc, h, dres, g)
    return tiles[:1], sums


GATE_BLK = GATE_COL // D


def _sigmoid(x):
    return 1.0 / (1.0 + jnp.exp(-x))


def _merge_fwd(p_sb, p_fx, proj, name):
    t = p_sb.shape[0]
    row = pl.BlockSpec((TR, D), lambda i: (i, 0))

    def body(ps_ref, pf_ref, gs_ref, gf_ref, o_ref):
        o_ref[...] = (_sigmoid(gs_ref[...]) * ps_ref[...] + _sigmoid(gf_ref[...]) * pf_ref[...]).astype(BF16)

    return pl.pallas_call(
        body, name=name, out_shape=jax.ShapeDtypeStruct((t, D), BF16), grid=(t // TR,),
        in_specs=[row, row, pl.BlockSpec((TR, D), lambda i: (i, GATE_BLK)),
                  pl.BlockSpec((TR, D), lambda i: (i, GATE_BLK + 1))],
        out_specs=row, compiler_params=_cparams("parallel"))(p_sb, p_fx, proj, proj)


def _merge_bwd(dm, p, proj, dproj, which, name):
    t = dm.shape[0]
    row = pl.BlockSpec((TR, D), lambda i: (i, 0))
    gate = pl.BlockSpec((TR, D), lambda i: (i, GATE_BLK + which))

    def body(dm_ref, p_ref, g_ref, *rest):
        dp_ref, dg_ref = rest[-2:]
        dmv = dm_ref[...]
        s = _sigmoid(g_ref[...])
        dp_ref[...] = (dmv * s).astype(BF16)
        dg_ref[...] = (dmv * p_ref[...] * s * (1.0 - s)).astype(BF16)

    out_shape = (jax.ShapeDtypeStruct((t, D), BF16), jax.ShapeDtypeStruct((t, IN_P), BF16))
    if dproj is None:
        return pl.pallas_call(
            body, name=name, out_shape=out_shape, grid=(t // TR,), in_specs=[row, row, gate],
            out_specs=(row, gate), compiler_params=_cparams("parallel"))(dm, p, proj)
    return pl.pallas_call(
        body, name=name, out_shape=out_shape, grid=(t // TR,), in_specs=[row, row, gate, ANY],
        out_specs=(row, gate), input_output_aliases={3: 1}, compiler_params=_cparams("parallel"))(dm, p, proj, dproj)


CH = 288


def _chunk(c, n=CH):
    return pl.ds(pl.multiple_of(c * CH, 8), n)


def _conv_taps(u_ref, c):
    x = u_ref[_chunk(c), :]
    prev = u_ref[pl.ds(pl.multiple_of(jnp.maximum(c * CH - 8, 0), 8), 8), :]
    xx = jnp.concatenate([jnp.where(c == 0, 0.0, prev), x], axis=0)
    return x, pltpu.roll(xx, 1, 0)[8:], pltpu.roll(xx, 2, 0)[8:]


def _conv_glu_fwd(u, cw, nseq, name):
    nblk = D_FF // FFC

    def body(u_ref, cw_ref, o_ref):
        cwv = cw_ref[...]

        def step(c, _):
            x, x1, x2 = _conv_taps(u_ref, c)
            uc = cwv[0:1, :] * x2 + cwv[1:2, :] * x1 + cwv[2:3, :] * x
            a, b = uc[:, :FFC], uc[:, FFC:]
            o_ref[_chunk(c), :] = (a * _sigmoid(a) * b).astype(BF16)
            return 0

        lax.fori_loop(0, LP // CH, step, 0)

    return pl.pallas_call(
        body, name=name, out_shape=jax.ShapeDtypeStruct((nseq * LP, D_FF), BF16), grid=(nseq, nblk),
        in_specs=[pl.BlockSpec((LP, 2 * FFC), lambda s, j: (s, j)), pl.BlockSpec((3, 2 * FFC), lambda s, j: (0, j))],
        out_specs=pl.BlockSpec((LP, FFC), lambda s, j: (s, j)),
        compiler_params=_cparams("parallel", "parallel"))(u, cw)


def _conv_glu_bwd(u, cw, dact, nseq, name):
    nblk = D_FF // FFC
    nch = LP // CH

    def body(u_ref, cw_ref, da_ref, du_ref, dcw_ref):
        s = pl.program_id(1)
        cwv = cw_ref[...]

        def step(k, carry):
            nxt, p0, p1, p2 = carry
            c = nch - 1 - k
            x, x1, x2 = _conv_taps(u_ref, c)
            uc = cwv[0:1, :] * x2 + cwv[1:2, :] * x1 + cwv[2:3, :] * x
            a, b = uc[:, :FFC], uc[:, FFC:]
            sa = _sigmoid(a)
            dactv = da_ref[_chunk(c), :]
            da = dactv * b * (sa * (1.0 + a * (1.0 - sa)))
            db = dactv * (a * sa)
            duc = jnp.concatenate([da, db], axis=1)
            dd = jnp.concatenate([duc, nxt], axis=0)
            du = (cwv[2:3, :] * duc + cwv[1:2, :] * pltpu.roll(dd, CH + 7, 0)[:CH]
                  + cwv[0:1, :] * pltpu.roll(dd, CH + 6, 0)[:CH])
            du_ref[_chunk(c), :] = du.astype(BF16)
            return (duc[:8], p0 + jnp.sum(duc * x2, axis=0, keepdims=True),
                    p1 + jnp.sum(duc * x1, axis=0, keepdims=True), p2 + jnp.sum(duc * x, axis=0, keepdims=True))

        zrow = jnp.zeros((1, 2 * FFC), F32)
        _, p0, p1, p2 = lax.fori_loop(0, nch, step, (jnp.zeros((8, 2 * FFC), F32), zrow, zrow, zrow))

        @pl.when(s == 0)
        def _():
            dcw_ref[...] = jnp.zeros_like(dcw_ref)

        dcw_ref[...] += jnp.concatenate([p0, p1, p2], axis=0)

    return pl.pallas_call(
        body, name=name,
        out_shape=(jax.ShapeDtypeStruct((nseq * LP, 2 * D_FF), BF16), jax.ShapeDtypeStruct((3, 2 * D_FF), F32)),
        grid=(nblk, nseq),
        in_specs=[pl.BlockSpec((LP, 2 * FFC), lambda j, s: (s, j)), pl.BlockSpec((3, 2 * FFC), lambda j, s: (0, j)),
                  pl.BlockSpec((LP, FFC), lambda j, s: (s, j))],
        out_specs=(pl.BlockSpec((LP, 2 * FFC), lambda j, s: (s, j)), pl.BlockSpec((3, 2 * FFC), lambda j, s: (0, j))),
        compiler_params=_cparams("parallel", "arbitrary"))(u, cw, dact)


F_BLK = F_COL // LANES
CB = 128


def _split3(x):
    hi = x.astype(BF16)
    r1 = x - hi.astype(F32)
    mid = r1.astype(BF16)
    lo = (r1 - mid.astype(F32)).astype(BF16)
    return hi, mid, lo


def _tri_dot(tri, x):
    hi, mid, lo = _split3(x)
    d = functools.partial(jnp.dot, preferred_element_type=F32)
    return d(tri, hi) + d(tri, mid) + d(tri, lo)


def _log_sigmoid(x):
    return jnp.minimum(x, 0.0) - jnp.log(1.0 + jnp.exp(-jnp.abs(x)))


def _gate_fwd(proj, bf, nseq, name):
    def body(f_ref, b_ref, c_ref):
        r_i = lax.broadcasted_iota(jnp.int32, (CB, CB), 0)
        c_i = lax.broadcasted_iota(jnp.int32, (CB, CB), 1)
        tri = (c_i <= r_i).astype(BF16)
        bv = b_ref[...]

        def step(k, carry):
            rows = pl.ds(pl.multiple_of(k * CB, CB), CB)
            lf = _log_sigmoid(f_ref[rows, :] + bv)
            c_ref[rows, :] = _tri_dot(tri, lf) + carry
            return carry + jnp.sum(lf, axis=0, keepdims=True)

        lax.fori_loop(0, LP // CB, step, jnp.zeros((1, LANES), F32))

    return pl.pallas_call(
        body, name=name, out_shape=jax.ShapeDtypeStruct((nseq * LP, LANES), F32), grid=(nseq,),
        in_specs=[pl.BlockSpec((LP, LANES), lambda s: (s, F_BLK)), pl.BlockSpec((1, LANES), lambda s: (0, 0))],
        out_specs=pl.BlockSpec((LP, LANES), lambda s: (s, 0)),
        compiler_params=_cparams("parallel"))(proj, bf)


def _gate_bwd(proj, bf, dc, dproj, nseq, name):
    def body(f_ref, b_ref, dc_ref, _, df_ref, db_ref):
        s = pl.program_id(0)
        r_i = lax.broadcasted_iota(jnp.int32, (CB, CB), 0)
        c_i = lax.broadcasted_iota(jnp.int32, (CB, CB), 1)
        tri = (c_i >= r_i).astype(BF16)
        bv = b_ref[...]

        def step(kk, carry):
            carry_c, carry_b = carry
            k = LP // CB - 1 - kk
            rows = pl.ds(pl.multiple_of(k * CB, CB), CB)
            dcv = dc_ref[rows, :]
            dlf = _tri_dot(tri, dcv) + carry_c
            df = dlf * _sigmoid(-(f_ref[rows, :] + bv))
            df_ref[rows, :] = jnp.concatenate([df, jnp.zeros_like(df)], axis=1).astype(BF16)
            return carry_c + jnp.sum(dcv, axis=0, keepdims=True), carry_b + jnp.sum(df, axis=0, keepdims=True)

        zero = jnp.zeros((1, LANES), F32)
        _, dbp = lax.fori_loop(0, LP // CB, step, (zero, zero))

        @pl.when(s == 0)
        def _():
            db_ref[...] = jnp.zeros_like(db_ref)

        db_ref[...] += dbp

    return pl.pallas_call(
        body, name=name,
        out_shape=(jax.ShapeDtypeStruct(dproj.shape, BF16), jax.ShapeDtypeStruct((1, LANES), F32)), grid=(nseq,),
        in_specs=[pl.BlockSpec((LP, LANES), lambda s: (s, F_BLK)), pl.BlockSpec((1, LANES), lambda s: (0, 0)),
                  pl.BlockSpec((LP, LANES), lambda s: (s, 0)), ANY],
        out_specs=(pl.BlockSpec((LP, 2 * LANES), lambda s: (s, F_COL // (2 * LANES))), pl.BlockSpec((1, LANES), lambda s: (0, 0))),
        input_output_aliases={3: 0},
        compiler_params=_cparams("arbitrary"))(proj, bf, dc, dproj)


SCALE = 0.125
NEG = -1e30


def _dot_nt(a, b):
    return lax.dot_general(a, b, (((1,), (1,)), ((), ())), preferred_element_type=F32)


def _dot(a, b):
    return jnp.dot(a, b, preferred_element_type=F32)


def _blk(i, n=BQ):
    return pl.ds(pl.multiple_of(i * BQ, BQ), n)


def _query_blocks(qblock):
    def full(i, _):
        qblock(i, BQ)
        return 0

    lax.fori_loop(0, NBLK - 1, full, 0)
    qblock(jnp.minimum(pl.program_id(0) + NBLK, NBLK - 1), Q_LAST)


def _tile_iotas():
    return lax.broadcasted_iota(jnp.int32, (BQ, BQ), 0), lax.broadcasted_iota(jnp.int32, (BQ, BQ), 1)


def _lane_iota():
    return lax.broadcasted_iota(jnp.int32, (BQ, LANES), 1)


def _head_masks():
    lane = _lane_iota()
    return lane < HEAD, lane >= HEAD


def _only(mask, x):
    return jnp.where(mask, x, jnp.zeros_like(x))


def _chains(npair):
    return [(pp, h) for pp in range(npair) for h in range(2)]


def _load_qkv(p_ref, q_s, k_s, v_s):
    for pp in range(q_s.shape[0]):
        base = pp * PAIR_W
        q_s[pp] = (p_ref[:, base:base + LANES] * SCALE).astype(BF16)
        k_s[pp] = p_ref[:, base + LANES:base + 2 * LANES].astype(BF16)
        v_s[pp] = p_ref[:, base + 2 * LANES:base + 3 * LANES].astype(BF16)


def _softplus(z):
    return jnp.maximum(z, 0.0) + jnp.log(1.0 + jnp.exp(-jnp.abs(z)))


def _sb_tile_weights(q, k, strict, r, u_suf):
    n = len(q)
    z = [_dot_nt(q[c], k[c]) for c in range(n)]
    sp = [_softplus(zc) for zc in z]
    lk = [-spc if strict is None else jnp.where(strict, -spc, 0.0) for spc in sp]
    suf = [_dot(lkc.astype(BF16), u_suf) for lkc in lk]
    w = [jnp.exp(z[c] - sp[c] + r[c] + suf[c]) for c in range(n)]
    if strict is not None:
        w = [jnp.where(strict, wc, 0.0) for wc in w]
    r_next = [r[c] + suf[c][:, 0:1] + lk[c][:, 0:1] for c in range(n)]
    return w, sp, r_next


def _group_spec(kind, npair):
    return pl.BlockSpec((LP, npair * PAIR_W), lambda s, g: (s, (NH // (2 * npair)) * kind + g))


def _gheads_spec(npair):
    return pl.BlockSpec((LP, npair * LANES), lambda s, g: (s, g))


def _qkv_scratch(npair):
    return [pltpu.VMEM((npair, LP, LANES), BF16)] * 3


SEQ_SPEC = pl.BlockSpec((LP, LANES), lambda s, g: (s, 0))
RS_STRIDE = 16
Q_LAST = 128


def _pair_cols(pp):
    return slice(pp * LANES, (pp + 1) * LANES)


def _carry_spec(npair):
    return pl.BlockSpec((None, npair * LANES, LP), lambda s, g: (s, g, 0))


def _sb_fwd(proj, nseq, npair, name):
    t = nseq * LP
    chains = _chains(npair)

    def body(p_ref, o_ref, rs_ref, q_s, k_s, v_s, acc_ref, r_ref, rb_ref):
        _load_qkv(p_ref, q_s, k_s, v_s)
        row, col = _tile_iotas()
        u_suf = (row > col).astype(BF16)
        heads = _head_masks()

        def qblock(i, nq):
            diag = (col < row)[:nq]
            lane = _lane_iota()[:nq]
            heads_q = [hm[:nq] for hm in heads]
            acc_ref[...] = jnp.zeros_like(acc_ref)
            rb_ref[...] = jnp.zeros_like(rb_ref)
            r_ref[...] = jnp.zeros_like(r_ref)
            qb = [q_s[pp, _blk(i, nq), :] for pp in range(npair)]

            def tile(j, strict):
                kj = [k_s[pp, _blk(j), :] for pp in range(npair)]
                vj = [v_s[pp, _blk(j), :] for pp in range(npair)]
                r = [r_ref[c, :nq] for c in range(len(chains))]
                w, _, r_next = _sb_tile_weights([_only(heads_q[h], qb[pp]) for pp, h in chains],
                                                [kj[pp] for pp, _ in chains], strict, r, u_suf)
                pv = [_dot(w[c].astype(BF16), _only(heads[h], vj[pp])) for c, (pp, h) in enumerate(chains)]
                for pp in range(npair):
                    acc_ref[pp, :nq] += pv[2 * pp] + pv[2 * pp + 1]
                    rb_ref[pp, :nq] = jnp.where(lane == j, r[2 * pp], jnp.where(lane == RS_STRIDE + j, r[2 * pp + 1], rb_ref[pp, :nq]))
                for c in range(len(chains)):
                    r_ref[c, :nq] = r_next[c]

            tile(i, diag)

            def kblock(jj, _):
                tile(i - jj, None)
                return 0

            lax.fori_loop(1, i + 1, kblock, 0)
            for pp in range(npair):
                o_ref[_blk(i), _pair_cols(pp)] = acc_ref[pp].astype(BF16)
                rs_ref[_pair_cols(pp), _blk(i)] = rb_ref[pp].T

        _query_blocks(qblock)

    return pl.pallas_call(
        body, name=name,
        out_shape=(jax.ShapeDtypeStruct((t, W_ATT), BF16), jax.ShapeDtypeStruct((nseq, W_ATT, LP), F32)),
        grid=(nseq, NH // (2 * npair)), in_specs=[_group_spec(0, npair)], out_specs=(_gheads_spec(npair), _carry_spec(npair)),
        scratch_shapes=_qkv_scratch(npair) + [pltpu.VMEM((npair, BQ, LANES), F32), pltpu.VMEM((2 * npair, BQ, 1), F32),
                                      pltpu.VMEM((npair, BQ, LANES), F32)],
        compiler_params=_cparams("parallel", "parallel"))(proj)


def _sb_bwd(proj, do, rs, dproj, after, nseq, npair, name):
    chains = _chains(npair)

    def body(p_ref, do_ref, rs_ref, _, _after, dp_ref, q_s, k_s, v_s, kt_s, dqa_ref, dka_ref, dva_ref, ep_ref):
        _load_qkv(p_ref, q_s, k_s, v_s)
        row, col = _tile_iotas()
        u_after = (col > row).astype(BF16)
        u_before = (col < row).astype(BF16)
        heads = _head_masks()
        sub_all = lax.broadcasted_iota(jnp.int32, (LANES, BQ), 0)
        rows_k = (sub_all < HEAD, sub_all >= HEAD)
        nc = len(chains)
        for pp in range(npair):
            kt_s[pp] = k_s[pp].astype(F32).T.astype(BF16)
        dka_ref[...] = jnp.zeros_like(dka_ref)
        dva_ref[...] = jnp.zeros_like(dva_ref)

        def qblock(i, nq):
            diag = (row < col)[:, :nq]
            rows_of = [m[:, :nq] for m in rows_k]
            heads_q = [m[:nq] for m in heads]
            queries = _blk(i, nq)
            qb = [q_s[pp, queries, :] for pp in range(npair)]
            dob = [do_ref[queries, _pair_cols(pp)] for pp in range(npair)]
            qt = [qb[pp].astype(F32).T.astype(BF16) for pp in range(npair)]
            dot = [dob[pp].astype(F32).T.astype(BF16) for pp in range(npair)]
            qt_m = [jnp.where(rows_of[h], qt[pp], jnp.zeros_like(qt[pp])) for pp, h in chains]
            dot_m = [jnp.where(rows_of[h], dot[pp], jnp.zeros_like(dot[pp])) for pp, h in chains]
            q_m = [_only(heads_q[h], qb[pp]) for pp, h in chains]
            do_m = [_only(heads_q[h], dob[pp]) for pp, h in chains]
            dqa_ref[...] = jnp.zeros_like(dqa_ref)
            ep_ref[...] = jnp.zeros_like(ep_ref)

            def tile(j, strict):
                keys = pl.ds(pl.multiple_of(j * BQ, BQ), BQ)
                kj = [k_s[pp, _blk(j), :] for pp in range(npair)]
                vj = [v_s[pp, _blk(j), :] for pp in range(npair)]
                r = [_key_cols(rs_ref, pp * LANES + RS_STRIDE * h + j, i, nq) for pp, h in chains]
                z = [_dot(kj[pp], qt_m[cidx]) for cidx, (pp, _) in enumerate(chains)]
                dw = [_dot(vj[pp], dot_m[cidx]) for cidx, (pp, _) in enumerate(chains)]
                sp = [_softplus(zc) for zc in z]
                lk = [-spc if strict is None else jnp.where(strict, -spc, 0.0) for spc in sp]
                suf = [_dot(u_after, lkc.astype(BF16)) for lkc in lk]
                w = [jnp.exp(z[cidx] - sp[cidx] + r[cidx] + suf[cidx]) for cidx in range(nc)]
                if strict is not None:
                    w = [jnp.where(strict, wc, 0.0) for wc in w]
                e = [dw[cidx] * w[cidx] for cidx in range(nc)]
                e_pre = [ep_ref[cidx, :, :nq] + _dot(u_before, e[cidx].astype(BF16)) for cidx in range(nc)]
                dz = []
                for cidx in range(nc):
                    ep_ref[cidx, :, :nq] += jnp.sum(e[cidx], axis=0, keepdims=True)
                    sneg = jnp.exp(-sp[cidx])
                    dzc = e[cidx] * sneg - (1.0 - sneg) * e_pre[cidx]
                    if strict is not None:
                        dzc = jnp.where(strict, dzc, 0.0)
                    dz.append(dzc.astype(BF16))
                dq = [_dot(jnp.where(rows_k[h], kt_s[pp, :, keys], jnp.zeros((LANES, BQ), BF16)), dz[cidx])
                      for cidx, (pp, h) in enumerate(chains)]
                dk = [_dot(dz[cidx], q_m[cidx]) for cidx in range(nc)]
                dv = [_dot(w[cidx].astype(BF16), do_m[cidx]) for cidx in range(nc)]
                for pp in range(npair):
                    dqa_ref[pp, :, :nq] += dq[2 * pp] + dq[2 * pp + 1]
                    dka_ref[pp, _blk(j), :] += dk[2 * pp] + dk[2 * pp + 1]
                    dva_ref[pp, _blk(j), :] += dv[2 * pp] + dv[2 * pp + 1]

            def kblock(j, _):
                tile(j, None)
                return 0

            lax.fori_loop(0, i, kblock, 0)
            tile(i, diag)
            for pp in range(npair):
                dp_ref[_blk(i), pp * PAIR_W:pp * PAIR_W + LANES] = (dqa_ref[pp].T * SCALE).astype(BF16)

        _query_blocks(qblock)
        for pp in range(npair):
            dp_ref[:, pp * PAIR_W + LANES:pp * PAIR_W + 2 * LANES] = dka_ref[pp].astype(BF16)
            dp_ref[:, pp * PAIR_W + 2 * LANES:pp * PAIR_W + 3 * LANES] = dva_ref[pp].astype(BF16)

    return pl.pallas_call(
        body, name=name, out_shape=jax.ShapeDtypeStruct(dproj.shape, BF16), grid=(nseq, NH // (2 * npair)),
        in_specs=[_group_spec(0, npair), _gheads_spec(npair), _carry_spec(npair), ANY, ANY], out_specs=_group_spec(0, npair),
        input_output_aliases={3: 0},
        scratch_shapes=_qkv_scratch(npair) + [pltpu.VMEM((npair, LANES, LP), BF16), pltpu.VMEM((npair, LANES, BQ), F32),
                                      pltpu.VMEM((npair, LP, LANES), F32), pltpu.VMEM((npair, LP, LANES), F32),
                                      pltpu.VMEM((2 * npair, 1, BQ), F32)],
        compiler_params=_cparams("parallel", "parallel"))(proj, do, rs, dproj, after)


CROW_SPEC = pl.BlockSpec((None, NH, LP), lambda s, g: (s, 0, 0))


def _key_cols(cr_ref, head, j, n=BQ):
    return cr_ref[pl.ds(head, 1), _blk(j)][:, :n]


def _fox_fwd(proj, c, crow, nseq, npair, name):
    t = nseq * LP
    chains = _chains(npair)

    def body(p_ref, c_ref, cr_ref, o_ref, o32_ref, lse_ref, q_s, k_s, v_s, vt_s, ck_s, acc_ref, m_ref, l_ref):
        _load_qkv(p_ref, q_s, k_s, v_s)
        row, col = _tile_iotas()
        sub_all = lax.broadcasted_iota(jnp.int32, (LANES, BQ), 0)
        rows_k = (sub_all < HEAD, sub_all >= HEAD)
        head0 = 2 * npair * pl.program_id(1)
        nc = len(chains)
        lane_all = lax.broadcasted_iota(jnp.int32, (LP, LANES), 1)
        for pp in range(npair):
            vt_s[pp] = v_s[pp].astype(F32).T.astype(BF16)
        for cidx in range(nc):
            ck_s[cidx] = jnp.sum(jnp.where(lane_all == head0 + cidx, c_ref[...], 0.0), axis=1, keepdims=True)

        def qblock(i, nq):
            diag = (row <= col)[:, :nq]
            sub = sub_all[:, :nq]
            rows_of = (sub < HEAD, sub >= HEAD)
            qt = [q_s[pp, _blk(i, nq), :].astype(F32).T.astype(BF16) for pp in range(npair)]
            qt = [jnp.where(rows_of[h], qt[pp], jnp.zeros_like(qt[pp])) for pp, h in chains]
            cq = [_key_cols(cr_ref, head0 + cidx, i, nq) for cidx in range(nc)]
            acc_ref[...] = jnp.zeros_like(acc_ref)
            m_ref[...] = jnp.full_like(m_ref, NEG)
            l_ref[...] = jnp.zeros_like(l_ref)

            def tile(j, causal):
                keys = pl.ds(pl.multiple_of(j * BQ, BQ), BQ)
                z = [_dot(k_s[pp, _blk(j), :], qt[cidx]) + (cq[cidx] - ck_s[cidx, _blk(j), :])
                     for cidx, (pp, _) in enumerate(chains)]
                if causal is not None:
                    z = [jnp.where(causal, zc, NEG) for zc in z]
                p, alpha = [], []
                for cidx in range(nc):
                    m_old = m_ref[cidx, :, :nq]
                    m_new = jnp.maximum(m_old, jnp.max(z[cidx], axis=0, keepdims=True))
                    alpha.append(jnp.exp(m_old - m_new))
                    pc = jnp.exp(z[cidx] - m_new)
                    l_ref[cidx, :, :nq] = alpha[cidx] * l_ref[cidx, :, :nq] + jnp.sum(pc, axis=0, keepdims=True)
                    m_ref[cidx, :, :nq] = m_new
                    p.append(pc.astype(BF16))
                pv = [_dot(jnp.where(rows_k[h], vt_s[pp, :, keys], jnp.zeros((LANES, BQ), BF16)), p[cidx])
                      for cidx, (pp, h) in enumerate(chains)]
                for cidx in range(nc):
                    acc_ref[cidx, :, :nq] = alpha[cidx] * acc_ref[cidx, :, :nq] + pv[cidx]

            def kblock(j, _):
                tile(j, None)
                return 0

            lax.fori_loop(0, i, kblock, 0)
            tile(i, diag)
            for pp in range(npair):
                acc = [acc_ref[2 * pp + h, :, :nq] for h in range(2)]
                l = [l_ref[2 * pp + h, :, :nq] for h in range(2)]
                out = (acc[0] / l[0] + acc[1] / l[1]).T
                o_ref[_blk(i, nq), _pair_cols(pp)] = out.astype(BF16)
                o32_ref[_blk(i, nq), _pair_cols(pp)] = out
                lse = [m_ref[2 * pp + h, :, :nq] + jnp.log(l[h]) for h in range(2)]
                lse_t = jnp.where(sub == 0, lse[0], jnp.where(sub == 1, lse[1], 0.0))
                lse_ref[_blk(i, nq), _pair_cols(pp)] = lse_t.T
                if nq < BQ:
                    rest = pl.ds(pl.multiple_of(i * BQ + nq, nq), BQ - nq)
                    o_ref[rest, _pair_cols(pp)] = jnp.zeros((BQ - nq, LANES), BF16)
                    o32_ref[rest, _pair_cols(pp)] = jnp.zeros((BQ - nq, LANES), F32)
                    lse_ref[rest, _pair_cols(pp)] = jnp.zeros((BQ - nq, LANES), F32)

        _query_blocks(qblock)

    return pl.pallas_call(
        body, name=name,
        out_shape=(jax.ShapeDtypeStruct((t, W_ATT), BF16), jax.ShapeDtypeStruct((t, W_ATT), F32),
                   jax.ShapeDtypeStruct((t, W_ATT), F32)),
        grid=(nseq, NH // (2 * npair)), in_specs=[_group_spec(1, npair), SEQ_SPEC, CROW_SPEC], out_specs=(_gheads_spec(npair), _gheads_spec(npair), _gheads_spec(npair)),
        scratch_shapes=_qkv_scratch(npair) + [pltpu.VMEM((npair, LANES, LP), BF16), pltpu.VMEM((2 * npair, LP, 1), F32),
                                      pltpu.VMEM((2 * npair, LANES, BQ), F32), pltpu.VMEM((2 * npair, 1, BQ), F32),
                                      pltpu.VMEM((2 * npair, 1, BQ), F32)],
        compiler_params=_cparams("parallel", "parallel"))(proj, c, crow)


def _fox_bwd(proj, c, crow, o32, lse, do, dproj, nseq, npair, name):
    t = nseq * LP
    chains = _chains(npair)

    def body(p_ref, c_ref, cr_ref, o_ref, lse_ref, do_ref, _, dp_ref, dc_ref,
             q_s, k_s, v_s, kt_s, ck_s, dqa_ref, dka_ref, dva_ref, rsum_ref):
        _load_qkv(p_ref, q_s, k_s, v_s)
        row, col = _tile_iotas()
        lane = _lane_iota()
        heads = _head_masks()
        sub_all = lax.broadcasted_iota(jnp.int32, (LANES, BQ), 0)
        rows_k = (sub_all < HEAD, sub_all >= HEAD)
        group = pl.program_id(1)
        head0 = 2 * npair * group
        nc = len(chains)
        lane_all = lax.broadcasted_iota(jnp.int32, (LP, LANES), 1)
        for pp in range(npair):
            kt_s[pp] = k_s[pp].astype(F32).T.astype(BF16)
        for cidx in range(nc):
            ck_s[cidx] = jnp.sum(jnp.where(lane_all == head0 + cidx, c_ref[...], 0.0), axis=1, keepdims=True)
        dka_ref[...] = jnp.zeros_like(dka_ref)
        dva_ref[...] = jnp.zeros_like(dva_ref)

        @pl.when(group == 0)
        def _():
            dc_ref[...] = jnp.zeros_like(dc_ref)

        def qblock(i, nq):
            diag = (row <= col)[:, :nq]
            sub = sub_all[:, :nq]
            rows_of = [m[:, :nq] for m in rows_k]
            heads_q = [m[:nq] for m in heads]
            queries = _blk(i, nq)
            dqa_ref[...] = jnp.zeros_like(dqa_ref)
            rsum_ref[...] = jnp.zeros_like(rsum_ref)
            qb = [q_s[pp, queries, :] for pp in range(npair)]
            dob = [do_ref[queries, _pair_cols(pp)] for pp in range(npair)]
            qt = [qb[pp].astype(F32).T.astype(BF16) for pp in range(npair)]
            dot = [dob[pp].astype(F32).T for pp in range(npair)]
            prod = [dot[pp] * o_ref[queries, _pair_cols(pp)].T for pp in range(npair)]
            lse_t = [lse_ref[queries, _pair_cols(pp)].T for pp in range(npair)]
            qt_m = [jnp.where(rows_of[h], qt[pp], jnp.zeros_like(qt[pp])) for pp, h in chains]
            dot_m = [jnp.where(rows_of[h], dot[pp], 0.0).astype(BF16) for pp, h in chains]
            q_m = [_only(heads_q[h], qb[pp]) for pp, h in chains]
            do_m = [_only(heads_q[h], dob[pp]) for pp, h in chains]
            cq = [_key_cols(cr_ref, head0 + cidx, i, nq) for cidx in range(nc)]
            lse_i = [lse_t[pp][h:h + 1, :] for pp, h in chains]
            delta = [jnp.sum(jnp.where(rows_of[h], prod[pp], 0.0), axis=0, keepdims=True) for pp, h in chains]

            def tile(j, causal):
                keys = pl.ds(pl.multiple_of(j * BQ, BQ), BQ)
                kj = [k_s[pp, _blk(j), :] for pp in range(npair)]
                vj = [v_s[pp, _blk(j), :] for pp in range(npair)]
                z = [_dot(kj[pp], qt_m[cidx]) + (cq[cidx] - ck_s[cidx, _blk(j), :]) for cidx, (pp, _) in enumerate(chains)]
                if causal is not None:
                    z = [jnp.where(causal, zc, NEG) for zc in z]
                dpv = [_dot(vj[pp], dot_m[cidx]) for cidx, (pp, _) in enumerate(chains)]
                p = [jnp.exp(z[cidx] - lse_i[cidx]) for cidx in range(nc)]
                ds = [p[cidx] * (dpv[cidx] - delta[cidx]) for cidx in range(nc)]
                dsb = [d.astype(BF16) for d in ds]
                dq = [_dot(jnp.where(rows_k[h], kt_s[pp, :, keys], jnp.zeros((LANES, BQ), BF16)), dsb[cidx])
                      for cidx, (pp, h) in enumerate(chains)]
                dk = [_dot(dsb[cidx], q_m[cidx]) for cidx in range(nc)]
                dv = [_dot(p[cidx].astype(BF16), do_m[cidx]) for cidx in range(nc)]
                for pp in range(npair):
                    dqa_ref[pp, :, :nq] += dq[2 * pp] + dq[2 * pp + 1]
                    dka_ref[pp, _blk(j), :] += dk[2 * pp] + dk[2 * pp + 1]
                    dva_ref[pp, _blk(j), :] += dv[2 * pp] + dv[2 * pp + 1]
                col_sums = jnp.zeros((BQ, LANES), F32)
                for cidx in range(nc):
                    col_sums = col_sums + jnp.where(lane == head0 + cidx, jnp.sum(ds[cidx], axis=1, keepdims=True), 0.0)
                    rsum_ref[cidx, :, :nq] += jnp.sum(ds[cidx], axis=0, keepdims=True)
                dc_ref[_blk(j), :] = dc_ref[_blk(j), :] - col_sums

            def kblock(j, _):
                tile(j, None)
                return 0

            lax.fori_loop(0, i, kblock, 0)
            tile(i, diag)
            row_sums = jnp.zeros((LANES, nq), F32)
            for cidx in range(nc):
                row_sums = row_sums + jnp.where(sub == head0 + cidx, rsum_ref[cidx, :, :nq], 0.0)
            dc_ref[queries, :] += row_sums.T
            for pp in range(npair):
                dp_ref[_blk(i), pp * PAIR_W:pp * PAIR_W + LANES] = (dqa_ref[pp].T * SCALE).astype(BF16)

        _query_blocks(qblock)
        for pp in range(npair):
            dp_ref[:, pp * PAIR_W + LANES:pp * PAIR_W + 2 * LANES] = dka_ref[pp].astype(BF16)
            dp_ref[:, pp * PAIR_W + 2 * LANES:pp * PAIR_W + 3 * LANES] = dva_ref[pp].astype(BF16)

    return pl.pallas_call(
        body, name=name,
        out_shape=(jax.ShapeDtypeStruct(dproj.shape, BF16), jax.ShapeDtypeStruct((t, LANES), F32)),
        grid=(nseq, NH // (2 * npair)),
        in_specs=[_group_spec(1, npair), SEQ_SPEC, CROW_SPEC, _gheads_spec(npair), _gheads_spec(npair), _gheads_spec(npair), ANY],
        out_specs=(_group_spec(1, npair), SEQ_SPEC),
        input_output_aliases={6: 0},
        scratch_shapes=_qkv_scratch(npair) + [pltpu.VMEM((npair, LANES, LP), BF16), pltpu.VMEM((2 * npair, LP, 1), F32),
                                      pltpu.VMEM((npair, LANES, BQ), F32), pltpu.VMEM((npair, LP, LANES), F32),
                                      pltpu.VMEM((npair, LP, LANES), F32), pltpu.VMEM((2 * npair, 1, BQ), F32)],
        compiler_params=_cparams("parallel", "arbitrary"))(proj, c, crow, o32, lse, do, dproj)


def _adamw_math(w, g, m, v):
    m = B1 * m + (1.0 - B1) * g
    v = B2 * v + (1.0 - B2) * (g * g)
    m_hat = m / (1.0 - B1 ** STEP)
    v_hat = v / (1.0 - B2 ** STEP)
    delta = -LR * (m_hat / (jnp.sqrt(v_hat) + EPS) + WD * w)
    return delta, m, v


def _sum_adamw(parts, w, m, v, tr, name):
    rows, cols = w.shape
    cp = parts.shape[2]
    assert rows % tr == 0 and parts.shape[1] == rows

    def body(p_ref, w_ref, m_ref, v_ref, g_ref, d_ref, nm_ref, nv_ref):
        gsum = p_ref[0].astype(F32)
        for s in range(1, N_DEV):
            gsum = gsum + p_ref[s].astype(F32)
        gsum = gsum[:, :cols]
        d, nm, nv = _adamw_math(w_ref[...], gsum, m_ref[...], v_ref[...])
        g_ref[...] = gsum
        d_ref[...] = d
        nm_ref[...] = nm
        nv_ref[...] = nv

    blk = pl.BlockSpec((tr, cols), lambda i: (i, 0))
    out = jax.ShapeDtypeStruct((rows, cols), F32)
    return pl.pallas_call(
        body, name=name, out_shape=(out, out, out, out), grid=(rows // tr,),
        in_specs=[pl.BlockSpec((N_DEV, tr, cp), lambda i: (0, i, 0)), blk, blk, blk],
        out_specs=(blk, blk, blk, blk), compiler_params=_cparams("parallel"))(parts, w, m, v)


def _local_step(x, tgt, meta, tgt_front, g_mix, b_forget, g_ffn, g_final, first_weights, late_weights, early_grads, last_grad):
    nseq = x.shape[0]
    t = nseq * LP
    tm = LP // 2
    mm = functools.partial(_matmul, tm=tm)

    h0 = _pad_rows(meta, x, nseq, "pad_x").reshape(t, D)
    tgt_p = _pad_rows(tgt_front, tgt, nseq, "pad_target").reshape(t, D)
    bf = jnp.pad(b_forget.reshape(1, NH), ((0, 0), (0, LANES - NH)))

    n1 = _norm_fwd(h0, g_mix, "norm1")
    w_in_p, started = first_weights(n1)
    proj = mm(n1, w_in_p, out_dtype=F32, tn=1792, tk=D, after=started, name="in_proj")
    c = _gate_fwd(proj, bf, nseq, "gate_fwd")
    crow = c[:, :NH].reshape(nseq, LP, NH).transpose(0, 2, 1)
    o_sb, rs = _sb_fwd(proj, nseq, 2, "sb_fwd")
    o_fx, o_fx32, lse = _fox_fwd(proj, c, crow, nseq, 2, "fox_fwd")
    w_bsb, w_bfx, w_out, w_up_i, cw_i, w_down = late_weights(o_fx)
    p_sb = mm(o_sb, w_bsb, out_dtype=F32, tn=D, tk=W_ATT, name="branch_sb")
    p_fx = mm(o_fx, w_bfx, out_dtype=F32, tn=D, tk=W_ATT, name="branch_fox")
    merged = _merge_fwd(p_sb, p_fx, proj, "merge_fwd")
    rows = functools.partial(_matmul_rows, tm=LP // 4)
    h1, n2 = rows(merged, w_out, [h0], [g_ffn], _residual_norm, [F32, BF16], [], tk=D, name="out_proj_norm2")
    u = mm(n2, w_up_i, out_dtype=F32, tn=1408, tk=D, name="up_proj")
    act = _conv_glu_fwd(u, cw_i, nseq, "conv_glu_fwd")

    dh2, dh2b, loss, dg_final = rows(act, w_down, [h1, tgt_p], [g_final], _loss_head, [F32, BF16],
                                     [(8, LANES), (1, D)], tk=D_FF, name="down_proj_loss")
    d_down = _matmul(act, dh2b, out_dtype=BF16, tm=1408, tn=D, tk=LP, ta=True, name="d_w_down")
    dact = mm(dh2b, w_down, out_dtype=F32, tn=1408, tk=D, tb=True, name="d_act")
    du, d_cw = _conv_glu_bwd(u, cw_i, dact, nseq, "conv_glu_bwd")
    d_up = _matmul(n2, du, out_dtype=BF16, tm=D, tn=1408, tk=LP, ta=True, name="d_w_up")
    dh1, dh1b, dg_ffn = rows(du, w_up_i, [h1, dh2], [g_ffn], _residual_norm_bwd, [F32, BF16], [(1, D)],
                             tk=D_FF, tb=True, name="d_n2_norm2_bwd")
    d_out = _matmul(merged, dh1b, out_dtype=BF16, tm=D, tn=D, tk=LP, ta=True, name="d_w_out")
    dmerged = mm(dh1b, w_out, out_dtype=F32, tn=D, tk=D, tb=True, name="d_merged")
    dp_sb, dproj = _merge_bwd(dmerged, p_sb, proj, None, 0, "merge_bwd_sb")
    dp_fx, dproj = _merge_bwd(dmerged, p_fx, proj, dproj, 1, "merge_bwd_fox")
    d_bsb = _matmul(o_sb, dp_sb, out_dtype=BF16, tm=W_ATT, tn=D, tk=LP, ta=True, name="d_w_branch_sb")
    d_bfx = _matmul(o_fx, dp_fx, out_dtype=BF16, tm=W_ATT, tn=D, tk=LP, ta=True, name="d_w_branch_fox")
    do_sb = mm(dp_sb, w_bsb, out_dtype=BF16, tn=W_ATT, tk=D, tb=True, name="d_o_sb")
    do_fx = mm(dp_fx, w_bfx, out_dtype=BF16, tn=W_ATT, tk=D, tb=True, name="d_o_fox")
    sent = early_grads(dict(w_branch_sb=d_bsb, w_branch_fox=d_bfx, w_out=d_out, w_up=d_up, conv_w=d_cw, w_down=d_down))
    dproj = _sb_bwd(proj, do_sb, rs, dproj, sent, nseq, 2, "sb_bwd")
    dproj, dc = _fox_bwd(proj, c, crow, o_fx32, lse, do_fx, dproj, nseq, 2, "fox_bwd")
    dproj, d_bf = _gate_bwd(proj, bf, dc, dproj, nseq, "gate_bwd")
    d_in = _matmul(n1, dproj, out_dtype=BF16, tm=D, tn=1792, tk=LP, ta=True, name="d_w_in")
    dh0, dg_mix = rows(dproj, w_in_p, [h0, dh1], [g_mix], _residual_norm_bwd_f32, [F32], [(1, D)],
                       tk=IN_P // 2, tb=True, after=last_grad(d_in), name="d_n1_norm1_bwd")
    dh0 = dh0.reshape(nseq, LP, D)
    grads = dict(meta_tokens=jnp.sum(dh0[:, :N_META], axis=0), norm_mix_g=dg_mix, b_forget=d_bf[:, :NH],
                 norm_ffn_g=dg_ffn, norm_final_g=dg_final)
    return loss[0, 0], _real_rows(dh0, nseq, "grad_x"), grads


REPL = (("norm_mix_g", D), ("norm_ffn_g", D), ("norm_final_g", D), ("b_forget", LANES))
REPL_ROWS = 32
META_ROWS = N_META * D // LANES


def _pack_repl(tree):
    rows = [jnp.pad(tree[name].reshape(-1), (0, n - tree[name].size)).reshape(-1, LANES) for name, n in REPL]
    packed = jnp.concatenate(rows, axis=0)
    return jnp.pad(packed, ((0, REPL_ROWS - packed.shape[0]), (0, 0)))


def _unpack_repl(packed, shapes):
    out, r = {}, 0
    for name, n in REPL:
        size = 1
        for s in shapes[name]:
            size *= s
        out[name] = packed[r:r + n // LANES].reshape(-1)[:size].reshape(shapes[name])
        r += n // LANES
    return out


def kernel(x, meta_tokens, norm_mix_g, w_in, b_forget, w_branch_sb, w_branch_fox, w_out, norm_ffn_g, w_up, conv_w, w_down, norm_final_g, loss_target, m_meta_tokens, m_norm_mix_g, m_w_in, m_b_forget, m_w_branch_sb, m_w_branch_fox, m_w_out, m_norm_ffn_g, m_w_up, m_conv_w, m_w_down, m_norm_final_g, v_meta_tokens, v_norm_mix_g, v_w_in, v_b_forget, v_w_branch_sb, v_w_branch_fox, v_w_out, v_norm_ffn_g, v_w_up, v_conv_w, v_w_down, v_norm_final_g):
    w = dict(meta_tokens=meta_tokens, norm_mix_g=norm_mix_g, w_in=w_in, b_forget=b_forget, w_branch_sb=w_branch_sb,
             w_branch_fox=w_branch_fox, w_out=w_out, norm_ffn_g=norm_ffn_g, w_up=w_up, conv_w=conv_w, w_down=w_down,
             norm_final_g=norm_final_g)
    m = dict(meta_tokens=m_meta_tokens, norm_mix_g=m_norm_mix_g, w_in=m_w_in, b_forget=m_b_forget,
             w_branch_sb=m_w_branch_sb, w_branch_fox=m_w_branch_fox, w_out=m_w_out, norm_ffn_g=m_norm_ffn_g,
             w_up=m_w_up, conv_w=m_conv_w, w_down=m_w_down, norm_final_g=m_norm_final_g)
    v = dict(meta_tokens=v_meta_tokens, norm_mix_g=v_norm_mix_g, w_in=v_w_in, b_forget=v_b_forget,
             w_branch_sb=v_w_branch_sb, w_branch_fox=v_w_branch_fox, w_out=v_w_out, norm_ffn_g=v_norm_ffn_g,
             w_up=v_w_up, conv_w=v_conv_w, w_down=v_w_down, norm_final_g=v_norm_final_g)
    shapes = {k: a.shape for k, a in w.items()}
    sharded = ("w_in", "w_branch_sb", "w_branch_fox", "w_out", "w_up", "w_down", "conv_w", "meta_tokens")
    mat = lambda tree, name: tree[name].reshape(tree[name].shape[-2:])

    def lane_pad(a, width):
        return jnp.pad(a, ((0, 0), (0, width - a.shape[1])))

    late = ("w_branch_sb", "w_branch_fox", "w_out", "w_up", "w_down", "conv_w")
    pending_w = {}
    g_meta, = _all_gather([mat(w, "meta_tokens")], "gather_meta")
    pending_w["in"], in_started = _remote_start(
        [lane_pad(mat(w, "w_in").astype(BF16), SHARD_P)], False, g_meta, "gather_w_in_start")
    meta_full = g_meta.transpose(1, 0, 2).reshape(N_META, D) + in_started[0, 0]

    def first_weights(after):
        g_in, = _remote_wait(pending_w["in"], after, "gather_w_in_wait")
        pending_w["late"], started = _remote_start(
            [mat(w, "w_branch_sb").astype(BF16), mat(w, "w_branch_fox").astype(BF16), mat(w, "w_out").astype(BF16),
             lane_pad(mat(w, "w_up").astype(BF16), SHARD_P), mat(w, "w_down").astype(BF16), mat(w, "conv_w")],
            False, g_in, "gather_late_start")
        w_in_p = _relayout(g_in, 1, IN_P, _gathered_to_full(IN_SHARD, _in_padded_to_orig), BF16, 256, "w_in_cols")[0]
        return w_in_p, started

    def late_weights(after):
        g_bsb, g_bfx, g_out, g_up, g_down, g_cw = _remote_wait(pending_w["late"], after, "gather_late_wait")
        w_up_i = _relayout(g_up, 1, 2 * D_FF, _gathered_to_full(UP_SHARD, _up_inter_to_orig), BF16, 256, "w_up_cols")[0]
        w_bsb = _relayout(g_bsb, 1, D, _gathered_to_full(ATT_SHARD, lambda d: d), BF16, 256, "w_bsb_cols")[0]
        w_bfx = _relayout(g_bfx, 1, D, _gathered_to_full(ATT_SHARD, lambda d: d), BF16, 256, "w_bfx_cols")[0]
        cw_full = g_cw.transpose(1, 0, 2).reshape(3, 2 * D_FF)
        cw_i = cw_full.reshape(3, 2, D_FF // FFC, FFC).transpose(0, 2, 1, 3).reshape(3, 2 * D_FF)
        return w_bsb, w_bfx, g_out.reshape(D, D), w_up_i, cw_i, g_down.reshape(D_FF, D)

    pending_g = {}

    def early_grads(g):
        d_cw = g["conv_w"].reshape(3, D_FF // FFC, 2, FFC).transpose(0, 2, 1, 3).reshape(3, 2 * D_FF)
        pending_g["early"], sent = _remote_start(
            [_relayout(g["w_branch_sb"][None], N_DEV, ATT_SHARD, _full_to_shards(ATT_SHARD, lambda c: c), BF16, 256, "d_w_bsb_shards"),
             _relayout(g["w_branch_fox"][None], N_DEV, ATT_SHARD, _full_to_shards(ATT_SHARD, lambda c: c), BF16, 256, "d_w_bfx_shards"),
             g["w_out"].reshape(N_DEV, D // N_DEV, D),
             _relayout(g["w_up"][None], N_DEV, SHARD_P, _full_to_shards(UP_SHARD, _UP_ORIG_TO_INTER.get), BF16, 256, "d_w_up_shards"),
             g["w_down"].reshape(N_DEV, D_FF // N_DEV, D),
             d_cw.reshape(3, N_DEV, UP_SHARD).transpose(1, 0, 2)], True, g["w_out"], "exchange_early_start")
        return sent

    def last_grad(d_in):
        shards = _relayout(d_in[None], N_DEV, SHARD_P, _full_to_shards(IN_SHARD, _IN_ORIG_TO_PADDED.get), BF16, 256, "d_w_in_shards")
        pending_g["last"], sent = _remote_start([shards], True, shards, "exchange_last_start")
        return sent

    loss, grad_x, grads = _local_step(
        x, loss_target, meta_full, jnp.zeros((N_META, D), F32) + in_started[0, 0], norm_mix_g.reshape(1, D), b_forget,
        norm_ffn_g.reshape(1, D), norm_final_g.reshape(1, D), first_weights, late_weights, early_grads, last_grad)

    small = jnp.concatenate([_pack_repl(grads), grads["meta_tokens"].reshape(META_ROWS, LANES)], axis=0)
    small, = _all_gather([small], "gather_small_grads")
    me_idx = 4 * lax.axis_index("x") + 2 * lax.axis_index("y") + lax.axis_index("c")
    p_meta = lax.dynamic_slice_in_dim(small[:, REPL_ROWS:].reshape(N_DEV, N_META, D), me_idx * ATT_SHARD, ATT_SHARD, axis=2)

    p_in, = _remote_wait(pending_g["last"], small, "exchange_last_wait")
    parts = dict(zip(late, _remote_wait(pending_g["early"], p_in, "exchange_early_wait")), w_in=p_in, meta_tokens=p_meta)
    tiles = dict(w_in=256, w_branch_sb=256, w_branch_fox=256, w_out=D // N_DEV, w_up=256, w_down=D_FF // N_DEV,
                 conv_w=3, meta_tokens=N_META)
    new = {name: _sum_adamw(parts[name], mat(w, name), mat(m, name), mat(v, name), tiles[name], "adamw_" + name)
           for name in sharded}

    routs = _sum_adamw(small[:, :REPL_ROWS], _pack_repl(w), _pack_repl(m), _pack_repl(v), REPL_ROWS, "adamw_replicated")
    repl = [_unpack_repl(o, shapes) for o in routs]

    result = [lax.psum(loss, ("x", "y", "c")), grad_x]
    for k in range(4):
        for name in w:
            result.append(new[name][k].reshape(shapes[name]) if name in new else repl[k][name])
    return tuple(result)
```

```python
import functools

import jax
import jax.numpy as jnp
from jax import lax
from jax.experimental import pallas as pl
from jax.experimental.pallas import tpu as pltpu

F32 = jnp.float32
BF16 = jnp.bfloat16

N_DEV = 8
LANES = 128
D = 1024
N_META = 16
SEQ = 2048
L_REAL = N_META + SEQ
LP = 2304
BQ = 256
NBLK = LP // BQ
HEAD = 64
NH = 8
W_ATT = NH * HEAD
PAIR_W = 3 * LANES
D_FF = 2816
IN_COLS = 5128
QKV = 6 * W_ATT
IN_P = 5376
GATE_COL = QKV
F_COL = QKV + 2 * D
FFC = 256
RMS_EPS = 1e-6
LR, B1, B2, EPS, WD, STEP = 0.001, 0.9, 0.999, 1e-08, 0.01, 10
VMEM_LIMIT = 56 * 1024 * 1024

MESH = pl.DeviceIdType.MESH
ANY = pl.BlockSpec(memory_space=pl.ANY)


def _cparams(*sem):
    return pltpu.CompilerParams(dimension_semantics=sem if sem else None, vmem_limit_bytes=VMEM_LIMIT)


def _all_gather(xs, name):
    n = len(xs)

    def body(*refs):
        x_refs, out_refs = refs[:n], refs[n:2 * n]
        send_sems, recv_sems, local_sems = refs[2 * n:]
        mx, my, mc = lax.axis_index("x"), lax.axis_index("y"), lax.axis_index("c")
        me, sibling = (mx, my, mc), (mx, my, 1 - mc)
        chips = [(1 - mx, my), (mx, 1 - my), (1 - mx, 1 - my)]

        def copy(a, k, block, to, own=False):
            px, py, pc = block
            slot = out_refs[a].at[4 * px + 2 * py + pc]
            return pltpu.make_async_remote_copy(
                src_ref=x_refs[a] if own else slot, dst_ref=slot,
                send_sem=send_sems.at[7 * a + k], recv_sem=recv_sems.at[7 * a + k],
                device_id=to, device_id_type=MESH)

        mine = [pltpu.make_async_copy(x_refs[a], out_refs[a].at[4 * mx + 2 * my + mc], local_sems.at[a]) for a in range(n)]
        for cp in mine:
            cp.start()
        first = []
        for a in range(n):
            first.append(copy(a, 0, me, sibling, own=True))
            first += [copy(a, 1 + j, me, (*chip, mc), own=True) for j, chip in enumerate(chips)]
        for cp in first:
            cp.start()
        passed = []
        for j, chip in enumerate(chips):
            for a in range(n):
                copy(a, 1 + j, (*chip, mc), me).wait_recv()
                fwd = copy(a, 4 + j, (*chip, mc), sibling)
                fwd.start()
                passed.append(fwd)
        for a in range(n):
            copy(a, 0, sibling, me).wait_recv()
            for j, chip in enumerate(chips):
                copy(a, 4 + j, (*chip, 1 - mc), me).wait_recv()
        for cp in first + passed:
            cp.wait_send()
        for cp in mine:
            cp.wait()

    return pl.pallas_call(
        body, name=name,
        out_shape=tuple(jax.ShapeDtypeStruct((N_DEV,) + x.shape, x.dtype) for x in xs),
        in_specs=[ANY] * n, out_specs=tuple([ANY] * n),
        scratch_shapes=[pltpu.SemaphoreType.DMA((7 * n,)), pltpu.SemaphoreType.DMA((7 * n,)),
                        pltpu.SemaphoreType.DMA((n,))],
    )(*xs)


HBM = pl.BlockSpec(memory_space=pltpu.HBM)
SEM = pl.BlockSpec(memory_space=pltpu.SEMAPHORE)
EFFECT = pltpu.SideEffectType.DATAFLOW_SIDE_EFFECTING


def _peer_copies(src_refs, land_refs, send_sems, recv_sems, per_peer):
    mx, my, mc = lax.axis_index("x"), lax.axis_index("y"), lax.axis_index("c")
    me_idx = 4 * mx + 2 * my + mc
    copies = []
    for k in range(1, N_DEV):
        px, py, pc = mx ^ (k >> 2), my ^ ((k >> 1) & 1), mc ^ (k & 1)
        for a, (src, land) in enumerate(zip(src_refs, land_refs)):
            copies.append(pltpu.make_async_remote_copy(
                src_ref=src.at[4 * px + 2 * py + pc] if per_peer else src, dst_ref=land.at[me_idx],
                send_sem=send_sems.at[7 * a + k - 1], recv_sem=recv_sems.at[7 * a + k - 1],
                device_id=(px, py, pc), device_id_type=MESH))
    return me_idx, copies


def _remote_start(srcs, per_peer, after, name):
    n = len(srcs)
    lands = [lax.empty(s.shape if per_peer else (N_DEV,) + s.shape, s.dtype) for s in srcs]

    def body(*refs):
        src_refs, land_refs = refs[:n], refs[n:2 * n]
        send_sems, recv_sems = refs[2 * n + 1:2 * n + 3]
        token = refs[4 * n + 3]
        stage, local_sems = refs[4 * n + 4:5 * n + 4], refs[5 * n + 4]
        me_idx, copies = _peer_copies(src_refs, land_refs, send_sems, recv_sems, per_peer)
        for cp in copies:
            cp.start()
        own = [src_refs[a].at[me_idx] if per_peer else src_refs[a] for a in range(n)]
        for hop in ([(own[a], stage[a]) for a in range(n)], [(stage[a], land_refs[a].at[me_idx]) for a in range(n)]):
            cps = [pltpu.make_async_copy(s, d, local_sems.at[a]) for a, (s, d) in enumerate(hop)]
            for cp in cps:
                cp.start()
            for cp in cps:
                cp.wait()
        token[...] = jnp.zeros_like(token)

    thru = [pltpu.HBM(a.shape, a.dtype) for a in list(srcs) + lands]
    out = pl.pallas_call(
        body, name=name,
        out_shape=(pltpu.SemaphoreType.DMA((7 * n,)), pltpu.SemaphoreType.DMA((7 * n,)), *thru,
                   jax.ShapeDtypeStruct((8, LANES), F32)),
        in_specs=[HBM] * (2 * n) + [ANY],
        out_specs=(SEM, SEM, *([HBM] * (2 * n)), pl.BlockSpec(memory_space=pltpu.VMEM)),
        input_output_aliases={i: 2 + i for i in range(2 * n)},
        scratch_shapes=[pltpu.VMEM(s.shape[1:] if per_peer else s.shape, s.dtype) for s in srcs]
        + [pltpu.SemaphoreType.DMA((n,))],
        compiler_params=pltpu.CompilerParams(has_side_effects=EFFECT),
    )(*[pltpu.with_memory_space_constraint(a, pltpu.HBM) for a in list(srcs) + lands], after)
    return dict(sems=out[:2], bufs=out[2:2 * n + 2], per_peer=per_peer), out[-1]


def _remote_wait(pending, after, name):
    bufs = pending["bufs"]
    n = len(bufs) // 2
    per_peer = pending["per_peer"]

    def body(*refs):
        src_refs, land_refs = refs[:n], refs[n:2 * n]
        send_sems, recv_sems = refs[2 * n:2 * n + 2]
        _, copies = _peer_copies(src_refs, land_refs, send_sems, recv_sems, per_peer)
        for cp in copies:
            cp.wait_send()
        for cp in copies:
            cp.wait_recv()

    out = pl.pallas_call(
        body, name=name, out_shape=tuple(pltpu.HBM(a.shape, a.dtype) for a in bufs),
        in_specs=[HBM] * (2 * n) + [SEM, SEM, ANY], out_specs=tuple([HBM] * (2 * n)),
        input_output_aliases={i: i for i in range(2 * n)},
        compiler_params=pltpu.CompilerParams(has_side_effects=EFFECT),
    )(*bufs, *pending["sems"], after)
    return out[n:]


ROWS_PER_COPY = 256


def _pad_rows(front, body_rows, nseq, name):
    tail = LP - L_REAL
    nblk = SEQ // ROWS_PER_COPY

    def body(f_ref, b_ref, o_ref, z_ref, sems):
        s, i = pl.program_id(0), pl.program_id(1)
        rows = pltpu.make_async_copy(b_ref, o_ref.at[pl.ds(s, 1), pl.ds(N_META + i * ROWS_PER_COPY, ROWS_PER_COPY)], sems.at[0])
        rows.start()

        @pl.when(i == 0)
        def _():
            z_ref[...] = jnp.zeros_like(z_ref)
            head = pltpu.make_async_copy(f_ref, o_ref.at[s, pl.ds(0, N_META)], sems.at[1])
            zeros = pltpu.make_async_copy(z_ref, o_ref.at[s, pl.ds(L_REAL, tail)], sems.at[2])
            head.start()
            zeros.start()
            head.wait()
            zeros.wait()

        rows.wait()

    return pl.pallas_call(
        body, name=name, out_shape=jax.ShapeDtypeStruct((nseq, LP, D), F32), grid=(nseq, nblk),
        in_specs=[pl.BlockSpec((N_META, D), lambda s, i: (0, 0)), pl.BlockSpec((1, ROWS_PER_COPY, D), lambda s, i: (s, i, 0))],
        out_specs=ANY,
        scratch_shapes=[pltpu.VMEM((tail, D), F32), pltpu.SemaphoreType.DMA((3,))],
        compiler_params=_cparams("arbitrary", "arbitrary"))(front, body_rows)


def _real_rows(h, nseq, name):
    nblk = SEQ // ROWS_PER_COPY

    def body(h_ref, o_ref, sem):
        s, i = pl.program_id(0), pl.program_id(1)
        rows = pltpu.make_async_copy(h_ref.at[pl.ds(s, 1), pl.ds(N_META + i * ROWS_PER_COPY, ROWS_PER_COPY)], o_ref, sem)
        rows.start()
        rows.wait()

    return pl.pallas_call(
        body, name=name, out_shape=jax.ShapeDtypeStruct((nseq, SEQ, D), F32), grid=(nseq, nblk),
        in_specs=[ANY], out_specs=pl.BlockSpec((1, ROWS_PER_COPY, D), lambda s, i: (s, i, 0)),
        scratch_shapes=[pltpu.SemaphoreType.DMA],
        compiler_params=_cparams("arbitrary", "arbitrary"))(h)


def _plan_cols(n_q, n_dcols, src_of):
    plan = {}
    for q in range(n_q):
        for dblk in range(n_dcols // LANES):
            segs, key, start = [], None, 0
            for lane in range(LANES + 1):
                new = None
                if lane < LANES:
                    src = src_of(q, dblk * LANES + lane)
                    if src is not None:
                        new = (src[0], src[1] // LANES, (lane - src[1] % LANES) % LANES)
                if new != key:
                    if key is not None:
                        segs.append((*key, start, lane))
                    key, start = new, lane
            plan[(q, dblk)] = segs
    return plan


def _relayout(src, n_q, n_dcols, src_of, out_dtype, tr, name):
    n_p, rows, scols = src.shape
    plan = _plan_cols(n_q, n_dcols, src_of)

    def body(s_ref, d_ref):
        lane = lax.broadcasted_iota(jnp.int32, (tr, LANES), 1)
        for (q, dblk), segs in plan.items():
            acc = jnp.zeros((tr, LANES), F32)
            for p, sblk, rot, lo, hi in segs:
                x = s_ref[p, :, sblk * LANES:(sblk + 1) * LANES].astype(F32)
                if rot:
                    x = pltpu.roll(x, rot, 1)
                acc = x if (lo, hi) == (0, LANES) else jnp.where((lane >= lo) & (lane < hi), x, acc)
            d_ref[q, :, dblk * LANES:(dblk + 1) * LANES] = acc.astype(out_dtype)

    return pl.pallas_call(
        body, name=name, out_shape=jax.ShapeDtypeStruct((n_q, rows, n_dcols), out_dtype), grid=(rows // tr,),
        in_specs=[pl.BlockSpec((n_p, tr, scols), lambda i: (0, i, 0))],
        out_specs=pl.BlockSpec((n_q, tr, n_dcols), lambda i: (0, i, 0)),
        compiler_params=_cparams("parallel"))(src)


def _in_padded_to_orig(d):
    if d < QKV:
        kind, r = divmod(d, 4 * PAIR_W)
        pair, r = divmod(r, PAIR_W)
        part, r = divmod(r, LANES)
        return kind * 3 * W_ATT + part * W_ATT + pair * LANES + r
    if d < F_COL:
        return d + NH
    if d < F_COL + NH:
        return d - 2 * D
    return None


_IN_ORIG_TO_PADDED = {_in_padded_to_orig(d): d for d in range(IN_P) if _in_padded_to_orig(d) is not None}


def _up_inter_to_orig(d):
    j, r = divmod(d, 2 * FFC)
    part, r = divmod(r, FFC)
    return part * D_FF + j * FFC + r


_UP_ORIG_TO_INTER = {_up_inter_to_orig(d): d for d in range(2 * D_FF)}
IN_SHARD = IN_COLS // N_DEV
UP_SHARD = 2 * D_FF // N_DEV
SHARD_P = 768
ATT_SHARD = D // N_DEV


def _gathered_to_full(n_shard, to_orig):
    def src_of(q, d):
        c = to_orig(d)
        return None if c is None else (c // n_shard, c % n_shard)
    return src_of


def _full_to_shards(n_shard, from_orig):
    def src_of(q, d):
        return (0, from_orig(q * n_shard + d)) if d < n_shard else None
    return src_of


def _matmul(a, b, *, out_dtype, tm, tn, tk, ta=False, tb=False, after=None, name):
    if ta:
        kdim, m = a.shape
    else:
        m, kdim = a.shape
    n = b.shape[0] if tb else b.shape[1]
    assert m % tm == 0 and n % tn == 0 and kdim % tk == 0, (name, a.shape, b.shape, tm, tn, tk)
    nk = kdim // tk

    def body(a_ref, b_ref, *rest):
        o_ref, scratch = rest[len(extra)], rest[len(extra) + 1:]
        av, bv = a_ref[...], b_ref[...]
        if ta:
            p = lax.dot_general(av, bv, (((0,), (0,)), ((), ())), preferred_element_type=F32)
        elif tb:
            p = lax.dot_general(av, bv, (((1,), (1,)), ((), ())), preferred_element_type=F32)
        else:
            p = jnp.dot(av, bv, preferred_element_type=F32)
        if nk == 1:
            o_ref[...] = p.astype(o_ref.dtype)
        else:
            acc_ref, = scratch
            k = pl.program_id(2)

            @pl.when(k == 0)
            def _():
                acc_ref[...] = p

            @pl.when(k > 0)
            def _():
                acc_ref[...] += p

            @pl.when(k == nk - 1)
            def _():
                o_ref[...] = acc_ref[...].astype(o_ref.dtype)

    extra = [] if after is None else [after]
    a_spec = pl.BlockSpec((tk, tm), lambda i, j, k: (k, i)) if ta else pl.BlockSpec((tm, tk), lambda i, j, k: (i, k))
    b_spec = pl.BlockSpec((tn, tk), lambda i, j, k: (j, k)) if tb else pl.BlockSpec((tk, tn), lambda i, j, k: (k, j))
    return pl.pallas_call(
        body, name=name,
        out_shape=jax.ShapeDtypeStruct((m, n), out_dtype),
        grid=(m // tm, n // tn, nk),
        in_specs=[a_spec, b_spec] + [ANY] * len(extra),
        out_specs=pl.BlockSpec((tm, tn), lambda i, j, k: (i, j)),
        scratch_shapes=[] if nk == 1 else [pltpu.VMEM((tm, tn), F32)],
        compiler_params=_cparams("parallel", "parallel", "arbitrary"),
    )(a, b, *extra)


TR = 288


def _rms(h):
    return lax.rsqrt(jnp.mean(h * h, axis=-1, keepdims=True) + RMS_EPS)


def _norm_fwd(h, g, name):
    t = h.shape[0]
    row = pl.BlockSpec((TR, D), lambda i: (i, 0))

    def body(h_ref, g_ref, n_ref):
        hv = h_ref[...]
        n_ref[...] = ((hv * _rms(hv)) * g_ref[...]).astype(BF16)

    return pl.pallas_call(
        body, name=name, out_shape=jax.ShapeDtypeStruct((t, D), BF16), grid=(t // TR,),
        in_specs=[row, pl.BlockSpec((1, D), lambda i: (0, 0))], out_specs=row, compiler_params=_cparams("parallel"))(h, g)


EPI_ROWS = 144


def _matmul_rows(a, b, rows_in, vecs_in, epilogue, row_outs, sum_outs, *, tm, tk, tb=False, after=None,
                 unpadded_in=None, name):
    m, kdim = a.shape
    assert (b.shape[0] if tb else b.shape[1]) == D and m % tm == 0 and kdim % tk == 0 and tm % EPI_ROWS == 0
    nk = kdim // tk
    n_r, n_v, n_ro, n_so = len(rows_in), len(vecs_in), len(row_outs), len(sum_outs)
    extra = ([] if after is None else [after]) + ([] if unpadded_in is None else [unpadded_in])
    spans = []
    for q in range(LP // tm):
        lo, hi = max(q * tm, N_META), min((q + 1) * tm, L_REAL)
        spans.append((lo - N_META, hi - lo, lo - q * tm))

    def body(a_ref, b_ref, *rest):
        r_refs, v_refs = rest[:n_r], rest[n_r:n_r + n_v]
        outs = rest[n_r + n_v + len(extra):]
        ro_refs, so_refs, acc_ref = outs[:n_ro], outs[n_ro:n_ro + n_so], outs[n_ro + n_so]
        i, k = pl.program_id(0), pl.program_id(1)

        def unpadded_copies(do):
            src, tile, sem = rest[n_r + n_v + len(extra) - 1], outs[n_ro + n_so + 1], outs[n_ro + n_so + 2]
            for q, (src0, n, dst0) in enumerate(spans):
                @pl.when(i % len(spans) == q)
                def _():
                    do(pltpu.make_async_copy(src.at[i // len(spans), pl.ds(src0, n)], tile.at[pl.ds(dst0, n)], sem),
                       tile, dst0, n)

        def start(cp, tile, dst0, n):
            if dst0:
                tile[pl.ds(0, dst0), :] = jnp.zeros((dst0, D), F32)
            if dst0 + n < tm:
                tile[pl.ds(dst0 + n, tm - dst0 - n), :] = jnp.zeros((tm - dst0 - n, D), F32)
            cp.start()

        if unpadded_in is not None:
            @pl.when(k == 0)
            def _():
                unpadded_copies(start)
        if tb:
            p = lax.dot_general(a_ref[...], b_ref[...], (((1,), (1,)), ((), ())), preferred_element_type=F32)
        else:
            p = jnp.dot(a_ref[...], b_ref[...], preferred_element_type=F32)

        @pl.when(k == 0)
        def _():
            acc_ref[...] = p

        @pl.when(k > 0)
        def _():
            acc_ref[...] += p

        @pl.when(k == nk - 1)
        def _():
            vecs = [v[...] for v in v_refs]
            tiles_in = list(r_refs)
            if unpadded_in is not None:
                unpadded_copies(lambda cp, *_: cp.wait())
                tiles_in.append(outs[n_ro + n_so + 1])

            def step(c, sums):
                rows = pl.ds(pl.multiple_of(c * EPI_ROWS, 8), EPI_ROWS)
                tiles, terms = epilogue(i * tm + c * EPI_ROWS, acc_ref[rows, :], *[r[rows, :] for r in tiles_in], *vecs)
                for o, tile in zip(ro_refs, tiles):
                    o[rows, :] = tile.astype(o.dtype)
                return tuple(s + term for s, term in zip(sums, terms))

            sums = lax.fori_loop(0, tm // EPI_ROWS, step, tuple(jnp.zeros(s, F32) for s in sum_outs))

            @pl.when(i == 0)
            def _():
                for o in so_refs:
                    o[...] = jnp.zeros_like(o)

            for o, s in zip(so_refs, sums):
                o[...] += s

    row = pl.BlockSpec((tm, D), lambda i, k: (i, 0))
    b_spec = pl.BlockSpec((D, tk), lambda i, k: (0, k)) if tb else pl.BlockSpec((tk, D), lambda i, k: (k, 0))
    return pl.pallas_call(
        body, name=name,
        out_shape=tuple([jax.ShapeDtypeStruct((m, D), dt) for dt in row_outs] + [jax.ShapeDtypeStruct(s, F32) for s in sum_outs]),
        grid=(m // tm, nk),
        in_specs=[pl.BlockSpec((tm, tk), lambda i, k: (i, k)), b_spec] + [row] * n_r
        + [pl.BlockSpec((1, D), lambda i, k: (0, 0))] * n_v + [ANY] * len(extra),
        out_specs=tuple([row] * n_ro + [pl.BlockSpec(s, lambda i, k: (0, 0)) for s in sum_outs]),
        scratch_shapes=[pltpu.VMEM((tm, D), F32)]
        + ([] if unpadded_in is None else [pltpu.VMEM((tm, D), F32), pltpu.SemaphoreType.DMA]),
        compiler_params=_cparams("arbitrary", "arbitrary"))(a, b, *rows_in, *vecs_in, *extra)


def _residual_norm(row0, acc, h, g):
    hv = h + acc
    return (hv, (hv * _rms(hv)) * g), ()


def _rms_bwd_math(hv, dn, gv):
    r = _rms(hv)
    hr = hv * r
    dng = dn * gv
    dh = r * (dng - hr * jnp.mean(dng * hr, axis=-1, keepdims=True))
    return dh, dn * hr


def _loss_head(row0, acc, h1, tgt, g):
    hv = h1 + acc
    hr = hv * _rms(hv)
    pos = row0 % LP + lax.broadcasted_iota(jnp.int32, (EPI_ROWS, 1), 0)
    valid = (pos >= N_META) & (pos < L_REAL)
    err = jnp.where(valid, hr * g - tgt, 0.0)
    part = 0.5 * jnp.sum(jnp.mean(err * err, axis=-1, keepdims=True))
    dy = err * (1.0 / D)
    dh, dgrow = _rms_bwd_math(hv, dy, g)
    return (dh, dh), (jnp.full((8, LANES), part, F32), jnp.sum(dgrow, axis=0, keepdims=True))


def _residual_norm_bwd(row0, acc, h, dres, g):
    dh, dgrow = _rms_bwd_math(h, acc, g)
    dh = dh + dres
    return (dh, dh), (jnp.sum(dgrow, axis=0, keepdims=True),)


def _residual_norm_bwd_f32(row0, acc, h, dres, g):
    tiles, sums = _residual_norm_bwd(row0, acc, h, dres, g)
    return tiles[:1], sums


GATE_BLK = GATE_COL // D


def _sigmoid(x):
    return 1.0 / (1.0 + jnp.exp(-x))


def _merge_fwd(p_sb, p_fx, proj, name):
    t = p_sb.shape[0]
    row = pl.BlockSpec((TR, D), lambda i: (i, 0))

    def body(ps_ref, pf_ref, gs_ref, gf_ref, o_ref):
        o_ref[...] = (_sigmoid(gs_ref[...]) * ps_ref[...] + _sigmoid(gf_ref[...]) * pf_ref[...]).astype(BF16)

    return pl.pallas_call(
        body, name=name, out_shape=jax.ShapeDtypeStruct((t, D), BF16), grid=(t // TR,),
        in_specs=[row, row, pl.BlockSpec((TR, D), lambda i: (i, GATE_BLK)),
                  pl.BlockSpec((TR, D), lambda i: (i, GATE_BLK + 1))],
        out_specs=row, compiler_params=_cparams("parallel"))(p_sb, p_fx, proj, proj)


def _merge_bwd(dm, p, proj, dproj, which, name):
    t = dm.shape[0]
    row = pl.BlockSpec((TR, D), lambda i: (i, 0))
    gate = pl.BlockSpec((TR, D), lambda i: (i, GATE_BLK + which))

    def body(dm_ref, p_ref, g_ref, *rest):
        dp_ref, dg_ref = rest[-2:]
        dmv = dm_ref[...]
        s = _sigmoid(g_ref[...])
        dp_ref[...] = (dmv * s).astype(BF16)
        dg_ref[...] = (dmv * p_ref[...] * s * (1.0 - s)).astype(BF16)

    out_shape = (jax.ShapeDtypeStruct((t, D), BF16), jax.ShapeDtypeStruct((t, IN_P), BF16))
    if dproj is None:
        return pl.pallas_call(
            body, name=name, out_shape=out_shape, grid=(t // TR,), in_specs=[row, row, gate],
            out_specs=(row, gate), compiler_params=_cparams("parallel"))(dm, p, proj)
    return pl.pallas_call(
        body, name=name, out_shape=out_shape, grid=(t // TR,), in_specs=[row, row, gate, ANY],
        out_specs=(row, gate), input_output_aliases={3: 1}, compiler_params=_cparams("parallel"))(dm, p, proj, dproj)


CH = 288


def _chunk(c, n=CH):
    return pl.ds(pl.multiple_of(c * CH, 8), n)


def _conv_taps(u_ref, c):
    x = u_ref[_chunk(c), :]
    prev = u_ref[pl.ds(pl.multiple_of(jnp.maximum(c * CH - 8, 0), 8), 8), :]
    xx = jnp.concatenate([jnp.where(c == 0, 0.0, prev), x], axis=0)
    return x, pltpu.roll(xx, 1, 0)[8:], pltpu.roll(xx, 2, 0)[8:]


def _conv_glu_fwd(u, cw, nseq, name):
    nblk = D_FF // FFC

    def body(u_ref, cw_ref, o_ref):
        cwv = cw_ref[...]

        def step(c, _):
            x, x1, x2 = _conv_taps(u_ref, c)
            uc = cwv[0:1, :] * x2 + cwv[1:2, :] * x1 + cwv[2:3, :] * x
            a, b = uc[:, :FFC], uc[:, FFC:]
            o_ref[_chunk(c), :] = (a * _sigmoid(a) * b).astype(BF16)
            return 0

        lax.fori_loop(0, LP // CH, step, 0)

    return pl.pallas_call(
        body, name=name, out_shape=jax.ShapeDtypeStruct((nseq * LP, D_FF), BF16), grid=(nseq, nblk),
        in_specs=[pl.BlockSpec((LP, 2 * FFC), lambda s, j: (s, j)), pl.BlockSpec((3, 2 * FFC), lambda s, j: (0, j))],
        out_specs=pl.BlockSpec((LP, FFC), lambda s, j: (s, j)),
        compiler_params=_cparams("parallel", "parallel"))(u, cw)


def _conv_glu_bwd(u, cw, dact, nseq, name):
    nblk = D_FF // FFC
    nch = LP // CH

    def body(u_ref, cw_ref, da_ref, du_ref, dcw_ref):
        s = pl.program_id(1)
        cwv = cw_ref[...]

        def step(k, carry):
            nxt, p0, p1, p2 = carry
            c = nch - 1 - k
            x, x1, x2 = _conv_taps(u_ref, c)
            uc = cwv[0:1, :] * x2 + cwv[1:2, :] * x1 + cwv[2:3, :] * x
            a, b = uc[:, :FFC], uc[:, FFC:]
            sa = _sigmoid(a)
            dactv = da_ref[_chunk(c), :]
            da = dactv * b * (sa * (1.0 + a * (1.0 - sa)))
            db = dactv * (a * sa)
            duc = jnp.concatenate([da, db], axis=1)
            dd = jnp.concatenate([duc, nxt], axis=0)
            du = (cwv[2:3, :] * duc + cwv[1:2, :] * pltpu.roll(dd, CH + 7, 0)[:CH]
                  + cwv[0:1, :] * pltpu.roll(dd, CH + 6, 0)[:CH])
            du_ref[_chunk(c), :] = du.astype(BF16)
            return (duc[:8], p0 + jnp.sum(duc * x2, axis=0, keepdims=True),
                    p1 + jnp.sum(duc * x1, axis=0, keepdims=True), p2 + jnp.sum(duc * x, axis=0, keepdims=True))

        zrow = jnp.zeros((1, 2 * FFC), F32)
        _, p0, p1, p2 = lax.fori_loop(0, nch, step, (jnp.zeros((8, 2 * FFC), F32), zrow, zrow, zrow))

        @pl.when(s == 0)
        def _():
            dcw_ref[...] = jnp.zeros_like(dcw_ref)

        dcw_ref[...] += jnp.concatenate([p0, p1, p2], axis=0)

    return pl.pallas_call(
        body, name=name,
        out_shape=(jax.ShapeDtypeStruct((nseq * LP, 2 * D_FF), BF16), jax.ShapeDtypeStruct((3, 2 * D_FF), F32)),
        grid=(nblk, nseq),
        in_specs=[pl.BlockSpec((LP, 2 * FFC), lambda j, s: (s, j)), pl.BlockSpec((3, 2 * FFC), lambda j, s: (0, j)),
                  pl.BlockSpec((LP, FFC), lambda j, s: (s, j))],
        out_specs=(pl.BlockSpec((LP, 2 * FFC), lambda j, s: (s, j)), pl.BlockSpec((3, 2 * FFC), lambda j, s: (0, j))),
        compiler_params=_cparams("parallel", "arbitrary"))(u, cw, dact)


F_BLK = F_COL // LANES
CB = 128


def _split3(x):
    hi = x.astype(BF16)
    r1 = x - hi.astype(F32)
    mid = r1.astype(BF16)
    lo = (r1 - mid.astype(F32)).astype(BF16)
    return hi, mid, lo


def _tri_dot(tri, x):
    hi, mid, lo = _split3(x)
    d = functools.partial(jnp.dot, preferred_element_type=F32)
    return d(tri, hi) + d(tri, mid) + d(tri, lo)


def _log_sigmoid(x):
    return jnp.minimum(x, 0.0) - jnp.log(1.0 + jnp.exp(-jnp.abs(x)))


def _gate_fwd(proj, bf, nseq, name):
    def body(f_ref, b_ref, c_ref):
        r_i = lax.broadcasted_iota(jnp.int32, (CB, CB), 0)
        c_i = lax.broadcasted_iota(jnp.int32, (CB, CB), 1)
        tri = (c_i <= r_i).astype(BF16)
        bv = b_ref[...]

        def step(k, carry):
            rows = pl.ds(pl.multiple_of(k * CB, CB), CB)
            lf = _log_sigmoid(f_ref[rows, :] + bv)
            c_ref[rows, :] = _tri_dot(tri, lf) + carry
            return carry + jnp.sum(lf, axis=0, keepdims=True)

        lax.fori_loop(0, LP // CB, step, jnp.zeros((1, LANES), F32))

    return pl.pallas_call(
        body, name=name, out_shape=jax.ShapeDtypeStruct((nseq * LP, LANES), F32), grid=(nseq,),
        in_specs=[pl.BlockSpec((LP, LANES), lambda s: (s, F_BLK)), pl.BlockSpec((1, LANES), lambda s: (0, 0))],
        out_specs=pl.BlockSpec((LP, LANES), lambda s: (s, 0)),
        compiler_params=_cparams("parallel"))(proj, bf)


def _gate_bwd(proj, bf, dc, dproj, nseq, name):
    def body(f_ref, b_ref, dc_ref, _, df_ref, db_ref):
        s = pl.program_id(0)
        r_i = lax.broadcasted_iota(jnp.int32, (CB, CB), 0)
        c_i = lax.broadcasted_iota(jnp.int32, (CB, CB), 1)
        tri = (c_i >= r_i).astype(BF16)
        bv = b_ref[...]

        def step(kk, carry):
            carry_c, carry_b = carry
            k = LP // CB - 1 - kk
            rows = pl.ds(pl.multiple_of(k * CB, CB), CB)
            dcv = dc_ref[rows, :]
            dlf = _tri_dot(tri, dcv) + carry_c
            df = dlf * _sigmoid(-(f_ref[rows, :] + bv))
            df_ref[rows, :] = jnp.concatenate([df, jnp.zeros_like(df)], axis=1).astype(BF16)
            return carry_c + jnp.sum(dcv, axis=0, keepdims=True), carry_b + jnp.sum(df, axis=0, keepdims=True)

        zero = jnp.zeros((1, LANES), F32)
        _, dbp = lax.fori_loop(0, LP // CB, step, (zero, zero))

        @pl.when(s == 0)
        def _():
            db_ref[...] = jnp.zeros_like(db_ref)

        db_ref[...] += dbp

    return pl.pallas_call(
        body, name=name,
        out_shape=(jax.ShapeDtypeStruct(dproj.shape, BF16), jax.ShapeDtypeStruct((1, LANES), F32)), grid=(nseq,),
        in_specs=[pl.BlockSpec((LP, LANES), lambda s: (s, F_BLK)), pl.BlockSpec((1, LANES), lambda s: (0, 0)),
                  pl.BlockSpec((LP, LANES), lambda s: (s, 0)), ANY],
        out_specs=(pl.BlockSpec((LP, 2 * LANES), lambda s: (s, F_COL // (2 * LANES))), pl.BlockSpec((1, LANES), lambda s: (0, 0))),
        input_output_aliases={3: 0},
        compiler_params=_cparams("arbitrary"))(proj, bf, dc, dproj)


SCALE = 0.125
NEG = -1e30


def _dot_nt(a, b):
    return lax.dot_general(a, b, (((1,), (1,)), ((), ())), preferred_element_type=F32)


def _dot(a, b):
    return jnp.dot(a, b, preferred_element_type=F32)


def _blk(i, n=BQ):
    return pl.ds(pl.multiple_of(i * BQ, BQ), n)


def _query_blocks(qblock, n_last):
    def full(i, _):
        qblock(i, BQ)
        return 0

    lax.fori_loop(0, NBLK - 1, full, 0)
    qblock(jnp.minimum(pl.program_id(0) + NBLK, NBLK - 1), n_last)


def _tile_iotas():
    return lax.broadcasted_iota(jnp.int32, (BQ, BQ), 0), lax.broadcasted_iota(jnp.int32, (BQ, BQ), 1)


def _lane_iota():
    return lax.broadcasted_iota(jnp.int32, (BQ, LANES), 1)


def _head_masks():
    lane = _lane_iota()
    return lane < HEAD, lane >= HEAD


def _only(mask, x):
    return jnp.where(mask, x, jnp.zeros_like(x))


def _chains(npair):
    return [(pp, h) for pp in range(npair) for h in range(2)]


def _load_qkv(p_ref, q_s, k_s, v_s):
    for pp in range(q_s.shape[0]):
        base = pp * PAIR_W
        q_s[pp] = (p_ref[:, base:base + LANES] * SCALE).astype(BF16)
        k_s[pp] = p_ref[:, base + LANES:base + 2 * LANES].astype(BF16)
        v_s[pp] = p_ref[:, base + 2 * LANES:base + 3 * LANES].astype(BF16)


def _softplus(z):
    return jnp.maximum(z, 0.0) + jnp.log(1.0 + jnp.exp(-jnp.abs(z)))


def _sb_tile_weights(q, k, strict, r, u_suf):
    n = len(q)
    z = [_dot_nt(q[c], k[c]) for c in range(n)]
    sp = [_softplus(zc) for zc in z]
    lk = [-spc if strict is None else jnp.where(strict, -spc, 0.0) for spc in sp]
    suf = [_dot(lkc.astype(BF16), u_suf) for lkc in lk]
    w = [jnp.exp(z[c] - sp[c] + r[c] + suf[c]) for c in range(n)]
    if strict is not None:
        w = [jnp.where(strict, wc, 0.0) for wc in w]
    r_next = [r[c] + suf[c][:, 0:1] + lk[c][:, 0:1] for c in range(n)]
    return w, sp, r_next


def _group_spec(kind, npair):
    return pl.BlockSpec((LP, npair * PAIR_W), lambda s, g: (s, (NH // (2 * npair)) * kind + g))


def _gheads_spec(npair):
    return pl.BlockSpec((LP, npair * LANES), lambda s, g: (s, g))


def _qkv_scratch(npair):
    return [pltpu.VMEM((npair, LP, LANES), BF16)] * 3


SEQ_SPEC = pl.BlockSpec((LP, LANES), lambda s, g: (s, 0))
RS_STRIDE = 16
Q_LAST = 128
Q_LAST_ROWS = 16


def _pair_cols(pp):
    return slice(pp * LANES, (pp + 1) * LANES)


def _carry_spec(npair):
    return pl.BlockSpec((None, npair * LANES, LP), lambda s, g: (s, g, 0))


def _sb_fwd(proj, nseq, npair, name):
    t = nseq * LP
    chains = _chains(npair)

    def body(p_ref, o_ref, rs_ref, q_s, k_s, v_s, acc_ref, r_ref, rb_ref):
        _load_qkv(p_ref, q_s, k_s, v_s)
        row, col = _tile_iotas()
        u_suf = (row > col).astype(BF16)
        heads = _head_masks()

        def qblock(i, nq):
            diag = (col < row)[:nq]
            lane = _lane_iota()[:nq]
            heads_q = [hm[:nq] for hm in heads]
            acc_ref[...] = jnp.zeros_like(acc_ref)
            rb_ref[...] = jnp.zeros_like(rb_ref)
            r_ref[...] = jnp.zeros_like(r_ref)
            qb = [q_s[pp, _blk(i, nq), :] for pp in range(npair)]

            def tile(j, strict):
                kj = [k_s[pp, _blk(j), :] for pp in range(npair)]
                vj = [v_s[pp, _blk(j), :] for pp in range(npair)]
                r = [r_ref[c, :nq] for c in range(len(chains))]
                w, _, r_next = _sb_tile_weights([_only(heads_q[h], qb[pp]) for pp, h in chains],
                                                [kj[pp] for pp, _ in chains], strict, r, u_suf)
                pv = [_dot(w[c].astype(BF16), _only(heads[h], vj[pp])) for c, (pp, h) in enumerate(chains)]
                for pp in range(npair):
                    acc_ref[pp, :nq] += pv[2 * pp] + pv[2 * pp + 1]
                    rb_ref[pp, :nq] = jnp.where(lane == j, r[2 * pp], jnp.where(lane == RS_STRIDE + j, r[2 * pp + 1], rb_ref[pp, :nq]))
                for c in range(len(chains)):
                    r_ref[c, :nq] = r_next[c]

            tile(i, diag)

            def kblock(jj, _):
                tile(i - jj, None)
                return 0

            lax.fori_loop(1, i + 1, kblock, 0)
            for pp in range(npair):
                o_ref[_blk(i), _pair_cols(pp)] = acc_ref[pp].astype(BF16)
                rs_ref[_pair_cols(pp), _blk(i)] = rb_ref[pp].T

        _query_blocks(qblock, Q_LAST_ROWS)

    return pl.pallas_call(
        body, name=name,
        out_shape=(jax.ShapeDtypeStruct((t, W_ATT), BF16), jax.ShapeDtypeStruct((nseq, W_ATT, LP), F32)),
        grid=(nseq, NH // (2 * npair)), in_specs=[_group_spec(0, npair)], out_specs=(_gheads_spec(npair), _carry_spec(npair)),
        scratch_shapes=_qkv_scratch(npair) + [pltpu.VMEM((npair, BQ, LANES), F32), pltpu.VMEM((2 * npair, BQ, 1), F32),
                                      pltpu.VMEM((npair, BQ, LANES), F32)],
        compiler_params=_cparams("parallel", "parallel"))(proj)


def _sb_bwd(proj, do, rs, dproj, after, nseq, npair, name):
    chains = _chains(npair)

    def body(p_ref, do_ref, rs_ref, _, _after, dp_ref, q_s, k_s, v_s, kt_s, dqa_ref, dka_ref, dva_ref, ep_ref):
        _load_qkv(p_ref, q_s, k_s, v_s)
        row, col = _tile_iotas()
        u_after = (col > row).astype(BF16)
        u_before = (col < row).astype(BF16)
        heads = _head_masks()
        sub_all = lax.broadcasted_iota(jnp.int32, (LANES, BQ), 0)
        rows_k = (sub_all < HEAD, sub_all >= HEAD)
        nc = len(chains)
        for pp in range(npair):
            kt_s[pp] = k_s[pp].astype(F32).T.astype(BF16)
        dka_ref[...] = jnp.zeros_like(dka_ref)
        dva_ref[...] = jnp.zeros_like(dva_ref)

        def qblock(i, nq):
            diag = (row < col)[:, :nq]
            rows_of = [m[:, :nq] for m in rows_k]
            heads_q = [m[:nq] for m in heads]
            queries = _blk(i, nq)
            qb = [q_s[pp, queries, :] for pp in range(npair)]
            dob = [do_ref[queries, _pair_cols(pp)] for pp in range(npair)]
            qt = [qb[pp].astype(F32).T.astype(BF16) for pp in range(npair)]
            dot = [dob[pp].astype(F32).T.astype(BF16) for pp in range(npair)]
            qt_m = [jnp.where(rows_of[h], qt[pp], jnp.zeros_like(qt[pp])) for pp, h in chains]
            dot_m = [jnp.where(rows_of[h], dot[pp], jnp.zeros_like(dot[pp])) for pp, h in chains]
            q_m = [_only(heads_q[h], qb[pp]) for pp, h in chains]
            do_m = [_only(heads_q[h], dob[pp]) for pp, h in chains]
            dqa_ref[...] = jnp.zeros_like(dqa_ref)
            ep_ref[...] = jnp.zeros_like(ep_ref)

            def tile(j, strict):
                keys = pl.ds(pl.multiple_of(j * BQ, BQ), BQ)
                kj = [k_s[pp, _blk(j), :] for pp in range(npair)]
                vj = [v_s[pp, _blk(j), :] for pp in range(npair)]
                r = [_key_cols(rs_ref, pp * LANES + RS_STRIDE * h + j, i, nq) for pp, h in chains]
                z = [_dot(kj[pp], qt_m[cidx]) for cidx, (pp, _) in enumerate(chains)]
                dw = [_dot(vj[pp], dot_m[cidx]) for cidx, (pp, _) in enumerate(chains)]
                sp = [_softplus(zc) for zc in z]
                lk = [-spc if strict is None else jnp.where(strict, -spc, 0.0) for spc in sp]
                suf = [_dot(u_after, lkc.astype(BF16)) for lkc in lk]
                w = [jnp.exp(z[cidx] - sp[cidx] + r[cidx] + suf[cidx]) for cidx in range(nc)]
                if strict is not None:
                    w = [jnp.where(strict, wc, 0.0) for wc in w]
                e = [dw[cidx] * w[cidx] for cidx in range(nc)]
                e_pre = [ep_ref[cidx, :, :nq] + _dot(u_before, e[cidx].astype(BF16)) for cidx in range(nc)]
                dz = []
                for cidx in range(nc):
                    ep_ref[cidx, :, :nq] += jnp.sum(e[cidx], axis=0, keepdims=True)
                    sneg = jnp.exp(-sp[cidx])
                    dzc = e[cidx] * sneg - (1.0 - sneg) * e_pre[cidx]
                    if strict is not None:
                        dzc = jnp.where(strict, dzc, 0.0)
                    dz.append(dzc.astype(BF16))
                dq = [_dot(jnp.where(rows_k[h], kt_s[pp, :, keys], jnp.zeros((LANES, BQ), BF16)), dz[cidx])
                      for cidx, (pp, h) in enumerate(chains)]
                dk = [_dot(dz[cidx], q_m[cidx]) for cidx in range(nc)]
                dv = [_dot(w[cidx].astype(BF16), do_m[cidx]) for cidx in range(nc)]
                for pp in range(npair):
                    dqa_ref[pp, :, :nq] += dq[2 * pp] + dq[2 * pp + 1]
                    dka_ref[pp, _blk(j), :] += dk[2 * pp] + dk[2 * pp + 1]
                    dva_ref[pp, _blk(j), :] += dv[2 * pp] + dv[2 * pp + 1]

            def kblock(j, _):
                tile(j, None)
                return 0

            lax.fori_loop(0, i, kblock, 0)
            tile(i, diag)
            for pp in range(npair):
                dp_ref[_blk(i), pp * PAIR_W:pp * PAIR_W + LANES] = (dqa_ref[pp].T * SCALE).astype(BF16)

        _query_blocks(qblock, Q_LAST)
        for pp in range(npair):
            dp_ref[:, pp * PAIR_W + LANES:pp * PAIR_W + 2 * LANES] = dka_ref[pp].astype(BF16)
            dp_ref[:, pp * PAIR_W + 2 * LANES:pp * PAIR_W + 3 * LANES] = dva_ref[pp].astype(BF16)

    return pl.pallas_call(
        body, name=name, out_shape=jax.ShapeDtypeStruct(dproj.shape, BF16), grid=(nseq, NH // (2 * npair)),
        in_specs=[_group_spec(0, npair), _gheads_spec(npair), _carry_spec(npair), ANY, ANY], out_specs=_group_spec(0, npair),
        input_output_aliases={3: 0},
        scratch_shapes=_qkv_scratch(npair) + [pltpu.VMEM((npair, LANES, LP), BF16), pltpu.VMEM((npair, LANES, BQ), F32),
                                      pltpu.VMEM((npair, LP, LANES), F32), pltpu.VMEM((npair, LP, LANES), F32),
                                      pltpu.VMEM((2 * npair, 1, BQ), F32)],
        compiler_params=_cparams("parallel", "parallel"))(proj, do, rs, dproj, after)


CROW_SPEC = pl.BlockSpec((None, NH, LP), lambda s, g: (s, 0, 0))


def _key_cols(cr_ref, head, j, n=BQ):
    return cr_ref[pl.ds(head, 1), _blk(j)][:, :n]


def _fox_fwd(proj, c, crow, nseq, npair, name):
    t = nseq * LP
    chains = _chains(npair)

    def body(p_ref, c_ref, cr_ref, o_ref, o32_ref, lse_ref, q_s, k_s, v_s, vt_s, ck_s, acc_ref, m_ref, l_ref):
        _load_qkv(p_ref, q_s, k_s, v_s)
        row, col = _tile_iotas()
        sub_all = lax.broadcasted_iota(jnp.int32, (LANES, BQ), 0)
        rows_k = (sub_all < HEAD, sub_all >= HEAD)
        head0 = 2 * npair * pl.program_id(1)
        nc = len(chains)
        lane_all = lax.broadcasted_iota(jnp.int32, (LP, LANES), 1)
        for pp in range(npair):
            vt_s[pp] = v_s[pp].astype(F32).T.astype(BF16)
        for cidx in range(nc):
            ck_s[cidx] = jnp.sum(jnp.where(lane_all == head0 + cidx, c_ref[...], 0.0), axis=1, keepdims=True)

        def qblock(i, nq):
            diag = (row <= col)[:, :nq]
            sub = sub_all[:, :nq]
            rows_of = (sub < HEAD, sub >= HEAD)
            qt = [q_s[pp, _blk(i, nq), :].astype(F32).T.astype(BF16) for pp in range(npair)]
            qt = [jnp.where(rows_of[h], qt[pp], jnp.zeros_like(qt[pp])) for pp, h in chains]
            cq = [_key_cols(cr_ref, head0 + cidx, i, nq) for cidx in range(nc)]
            acc_ref[...] = jnp.zeros_like(acc_ref)
            m_ref[...] = jnp.full_like(m_ref, NEG)
            l_ref[...] = jnp.zeros_like(l_ref)

            def tile(j, causal):
                keys = pl.ds(pl.multiple_of(j * BQ, BQ), BQ)
                z = [_dot(k_s[pp, _blk(j), :], qt[cidx]) + (cq[cidx] - ck_s[cidx, _blk(j), :])
                     for cidx, (pp, _) in enumerate(chains)]
                if causal is not None:
                    z = [jnp.where(causal, zc, NEG) for zc in z]
                p, alpha = [], []
                for cidx in range(nc):
                    m_old = m_ref[cidx, :, :nq]
                    m_new = jnp.maximum(m_old, jnp.max(z[cidx], axis=0, keepdims=True))
                    alpha.append(jnp.exp(m_old - m_new))
                    pc = jnp.exp(z[cidx] - m_new)
                    l_ref[cidx, :, :nq] = alpha[cidx] * l_ref[cidx, :, :nq] + jnp.sum(pc, axis=0, keepdims=True)
                    m_ref[cidx, :, :nq] = m_new
                    p.append(pc.astype(BF16))
                pv = [_dot(jnp.where(rows_k[h], vt_s[pp, :, keys], jnp.zeros((LANES, BQ), BF16)), p[cidx])
                      for cidx, (pp, h) in enumerate(chains)]
                for cidx in range(nc):
                    acc_ref[cidx, :, :nq] = alpha[cidx] * acc_ref[cidx, :, :nq] + pv[cidx]

            def kblock(j, _):
                tile(j, None)
                return 0

            lax.fori_loop(0, i, kblock, 0)
            tile(i, diag)
            for pp in range(npair):
                acc = [acc_ref[2 * pp + h, :, :nq] for h in range(2)]
                l = [l_ref[2 * pp + h, :, :nq] for h in range(2)]
                out = (acc[0] / l[0] + acc[1] / l[1]).T
                o_ref[_blk(i, nq), _pair_cols(pp)] = out.astype(BF16)
                o32_ref[_blk(i, nq), _pair_cols(pp)] = out
                lse = [m_ref[2 * pp + h, :, :nq] + jnp.log(l[h]) for h in range(2)]
                lse_t = jnp.where(sub == 0, lse[0], jnp.where(sub == 1, lse[1], 0.0))
                lse_ref[_blk(i, nq), _pair_cols(pp)] = lse_t.T
                if nq < BQ:
                    rest = pl.ds(pl.multiple_of(i * BQ + nq, nq), BQ - nq)
                    o_ref[rest, _pair_cols(pp)] = jnp.zeros((BQ - nq, LANES), BF16)
                    o32_ref[rest, _pair_cols(pp)] = jnp.zeros((BQ - nq, LANES), F32)
                    lse_ref[rest, _pair_cols(pp)] = jnp.zeros((BQ - nq, LANES), F32)

        _query_blocks(qblock, Q_LAST)

    return pl.pallas_call(
        body, name=name,
        out_shape=(jax.ShapeDtypeStruct((t, W_ATT), BF16), jax.ShapeDtypeStruct((t, W_ATT), F32),
                   jax.ShapeDtypeStruct((t, W_ATT), F32)),
        grid=(nseq, NH // (2 * npair)), in_specs=[_group_spec(1, npair), SEQ_SPEC, CROW_SPEC], out_specs=(_gheads_spec(npair), _gheads_spec(npair), _gheads_spec(npair)),
        scratch_shapes=_qkv_scratch(npair) + [pltpu.VMEM((npair, LANES, LP), BF16), pltpu.VMEM((2 * npair, LP, 1), F32),
                                      pltpu.VMEM((2 * npair, LANES, BQ), F32), pltpu.VMEM((2 * npair, 1, BQ), F32),
                                      pltpu.VMEM((2 * npair, 1, BQ), F32)],
        compiler_params=_cparams("parallel", "parallel"))(proj, c, crow)


def _fox_bwd(proj, c, crow, o32, lse, do, dproj, nseq, npair, name):
    t = nseq * LP
    chains = _chains(npair)

    def body(p_ref, c_ref, cr_ref, o_ref, lse_ref, do_ref, _, dp_ref, dc_ref,
             q_s, k_s, v_s, kt_s, ck_s, dqa_ref, dka_ref, dva_ref, rsum_ref):
        _load_qkv(p_ref, q_s, k_s, v_s)
        row, col = _tile_iotas()
        lane = _lane_iota()
        heads = _head_masks()
        sub_all = lax.broadcasted_iota(jnp.int32, (LANES, BQ), 0)
        rows_k = (sub_all < HEAD, sub_all >= HEAD)
        group = pl.program_id(1)
        head0 = 2 * npair * group
        nc = len(chains)
        lane_all = lax.broadcasted_iota(jnp.int32, (LP, LANES), 1)
        for pp in range(npair):
            kt_s[pp] = k_s[pp].astype(F32).T.astype(BF16)
        for cidx in range(nc):
            ck_s[cidx] = jnp.sum(jnp.where(lane_all == head0 + cidx, c_ref[...], 0.0), axis=1, keepdims=True)
        dka_ref[...] = jnp.zeros_like(dka_ref)
        dva_ref[...] = jnp.zeros_like(dva_ref)

        @pl.when(group == 0)
        def _():
            dc_ref[...] = jnp.zeros_like(dc_ref)

        def qblock(i, nq):
            diag = (row <= col)[:, :nq]
            sub = sub_all[:, :nq]
            rows_of = [m[:, :nq] for m in rows_k]
            heads_q = [m[:nq] for m in heads]
            queries = _blk(i, nq)
            dqa_ref[...] = jnp.zeros_like(dqa_ref)
            rsum_ref[...] = jnp.zeros_like(rsum_ref)
            qb = [q_s[pp, queries, :] for pp in range(npair)]
            dob = [do_ref[queries, _pair_cols(pp)] for pp in range(npair)]
            qt = [qb[pp].astype(F32).T.astype(BF16) for pp in range(npair)]
            dot = [dob[pp].astype(F32).T for pp in range(npair)]
            prod = [dot[pp] * o_ref[queries, _pair_cols(pp)].T for pp in range(npair)]
            lse_t = [lse_ref[queries, _pair_cols(pp)].T for pp in range(npair)]
            qt_m = [jnp.where(rows_of[h], qt[pp], jnp.zeros_like(qt[pp])) for pp, h in chains]
            dot_m = [jnp.where(rows_of[h], dot[pp], 0.0).astype(BF16) for pp, h in chains]
            q_m = [_only(heads_q[h], qb[pp]) for pp, h in chains]
            do_m = [_only(heads_q[h], dob[pp]) for pp, h in chains]
            cq = [_key_cols(cr_ref, head0 + cidx, i, nq) for cidx in range(nc)]
            lse_i = [lse_t[pp][h:h + 1, :] for pp, h in chains]
            delta = [jnp.sum(jnp.where(rows_of[h], prod[pp], 0.0), axis=0, keepdims=True) for pp, h in chains]

            def tile(j, causal):
                keys = pl.ds(pl.multiple_of(j * BQ, BQ), BQ)
                kj = [k_s[pp, _blk(j), :] for pp in range(npair)]
                vj = [v_s[pp, _blk(j), :] for pp in range(npair)]
                z = [_dot(kj[pp], qt_m[cidx]) + (cq[cidx] - ck_s[cidx, _blk(j), :]) for cidx, (pp, _) in enumerate(chains)]
                if causal is not None:
                    z = [jnp.where(causal, zc, NEG) for zc in z]
                dpv = [_dot(vj[pp], dot_m[cidx]) for cidx, (pp, _) in enumerate(chains)]
                p = [jnp.exp(z[cidx] - lse_i[cidx]) for cidx in range(nc)]
                ds = [p[cidx] * (dpv[cidx] - delta[cidx]) for cidx in range(nc)]
                dsb = [d.astype(BF16) for d in ds]
                dq = [_dot(jnp.where(rows_k[h], kt_s[pp, :, keys], jnp.zeros((LANES, BQ), BF16)), dsb[cidx])
                      for cidx, (pp, h) in enumerate(chains)]
                dk = [_dot(dsb[cidx], q_m[cidx]) for cidx in range(nc)]
                dv = [_dot(p[cidx].astype(BF16), do_m[cidx]) for cidx in range(nc)]
                for pp in range(npair):
                    dqa_ref[pp, :, :nq] += dq[2 * pp] + dq[2 * pp + 1]
                    dka_ref[pp, _blk(j), :] += dk[2 * pp] + dk[2 * pp + 1]
                    dva_ref[pp, _blk(j), :] += dv[2 * pp] + dv[2 * pp + 1]
                col_sums = jnp.zeros((BQ, LANES), F32)
                for cidx in range(nc):
                    col_sums = col_sums + jnp.where(lane == head0 + cidx, jnp.sum(ds[cidx], axis=1, keepdims=True), 0.0)
                    rsum_ref[cidx, :, :nq] += jnp.sum(ds[cidx], axis=0, keepdims=True)
                dc_ref[_blk(j), :] = dc_ref[_blk(j), :] - col_sums

            def kblock(j, _):
                tile(j, None)
                return 0

            lax.fori_loop(0, i, kblock, 0)
            tile(i, diag)
            row_sums = jnp.zeros((LANES, nq), F32)
            for cidx in range(nc):
                row_sums = row_sums + jnp.where(sub == head0 + cidx, rsum_ref[cidx, :, :nq], 0.0)
            dc_ref[queries, :] += row_sums.T
            for pp in range(npair):
                dp_ref[_blk(i), pp * PAIR_W:pp * PAIR_W + LANES] = (dqa_ref[pp].T * SCALE).astype(BF16)

        _query_blocks(qblock, Q_LAST)
        for pp in range(npair):
            dp_ref[:, pp * PAIR_W + LANES:pp * PAIR_W + 2 * LANES] = dka_ref[pp].astype(BF16)
            dp_ref[:, pp * PAIR_W + 2 * LANES:pp * PAIR_W + 3 * LANES] = dva_ref[pp].astype(BF16)

    return pl.pallas_call(
        body, name=name,
        out_shape=(jax.ShapeDtypeStruct(dproj.shape, BF16), jax.ShapeDtypeStruct((t, LANES), F32)),
        grid=(nseq, NH // (2 * npair)),
        in_specs=[_group_spec(1, npair), SEQ_SPEC, CROW_SPEC, _gheads_spec(npair), _gheads_spec(npair), _gheads_spec(npair), ANY],
        out_specs=(_group_spec(1, npair), SEQ_SPEC),
        input_output_aliases={6: 0},
        scratch_shapes=_qkv_scratch(npair) + [pltpu.VMEM((npair, LANES, LP), BF16), pltpu.VMEM((2 * npair, LP, 1), F32),
                                      pltpu.VMEM((npair, LANES, BQ), F32), pltpu.VMEM((npair, LP, LANES), F32),
                                      pltpu.VMEM((npair, LP, LANES), F32), pltpu.VMEM((2 * npair, 1, BQ), F32)],
        compiler_params=_cparams("parallel", "arbitrary"))(proj, c, crow, o32, lse, do, dproj)


def _adamw_math(w, g, m, v):
    m = B1 * m + (1.0 - B1) * g
    v = B2 * v + (1.0 - B2) * (g * g)
    m_hat = m / (1.0 - B1 ** STEP)
    v_hat = v / (1.0 - B2 ** STEP)
    delta = -LR * (m_hat / (jnp.sqrt(v_hat) + EPS) + WD * w)
    return delta, m, v


def _sum_adamw(parts, w, m, v, tr, name):
    rows, cols = w.shape
    cp = parts.shape[2]
    assert rows % tr == 0 and parts.shape[1] == rows

    def body(p_ref, w_ref, m_ref, v_ref, g_ref, d_ref, nm_ref, nv_ref):
        gsum = p_ref[0].astype(F32)
        for s in range(1, N_DEV):
            gsum = gsum + p_ref[s].astype(F32)
        gsum = gsum[:, :cols]
        d, nm, nv = _adamw_math(w_ref[...], gsum, m_ref[...], v_ref[...])
        g_ref[...] = gsum
        d_ref[...] = d
        nm_ref[...] = nm
        nv_ref[...] = nv

    blk = pl.BlockSpec((tr, cols), lambda i: (i, 0))
    out = jax.ShapeDtypeStruct((rows, cols), F32)
    return pl.pallas_call(
        body, name=name, out_shape=(out, out, out, out), grid=(rows // tr,),
        in_specs=[pl.BlockSpec((N_DEV, tr, cp), lambda i: (0, i, 0)), blk, blk, blk],
        out_specs=(blk, blk, blk, blk), compiler_params=_cparams("parallel"))(parts, w, m, v)


def _local_step(x, tgt, meta, g_mix, b_forget, g_ffn, g_final, first_weights, late_weights, early_grads, last_grad):
    nseq = x.shape[0]
    t = nseq * LP
    tm = LP // 2
    mm = functools.partial(_matmul, tm=tm)

    h0 = _pad_rows(meta, x, nseq, "pad_x").reshape(t, D)
    bf = jnp.pad(b_forget.reshape(1, NH), ((0, 0), (0, LANES - NH)))

    n1 = _norm_fwd(h0, g_mix, "norm1")
    w_in_p, started = first_weights(n1)
    proj = mm(n1, w_in_p, out_dtype=F32, tn=1792, tk=D, after=started, name="in_proj")
    c = _gate_fwd(proj, bf, nseq, "gate_fwd")
    crow = c[:, :NH].reshape(nseq, LP, NH).transpose(0, 2, 1)
    o_sb, rs = _sb_fwd(proj, nseq, 2, "sb_fwd")
    o_fx, o_fx32, lse = _fox_fwd(proj, c, crow, nseq, 2, "fox_fwd")
    w_bsb, w_bfx, w_out, w_up_i, cw_i, w_down = late_weights(o_fx)
    p_sb = mm(o_sb, w_bsb, out_dtype=F32, tn=D, tk=W_ATT, name="branch_sb")
    p_fx = mm(o_fx, w_bfx, out_dtype=F32, tn=D, tk=W_ATT, name="branch_fox")
    merged = _merge_fwd(p_sb, p_fx, proj, "merge_fwd")
    rows = functools.partial(_matmul_rows, tm=LP // 4)
    h1, n2 = rows(merged, w_out, [h0], [g_ffn], _residual_norm, [F32, BF16], [], tk=D, name="out_proj_norm2")
    u = mm(n2, w_up_i, out_dtype=F32, tn=1408, tk=D, name="up_proj")
    act = _conv_glu_fwd(u, cw_i, nseq, "conv_glu_fwd")

    dh2, dh2b, loss, dg_final = rows(act, w_down, [h1], [g_final], _loss_head, [F32, BF16],
                                     [(8, LANES), (1, D)], tk=D_FF, unpadded_in=tgt, name="down_proj_loss")
    d_down = _matmul(act, dh2b, out_dtype=BF16, tm=1408, tn=D, tk=LP, ta=True, name="d_w_down")
    dact = mm(dh2b, w_down, out_dtype=F32, tn=1408, tk=D, tb=True, name="d_act")
    du, d_cw = _conv_glu_bwd(u, cw_i, dact, nseq, "conv_glu_bwd")
    d_up = _matmul(n2, du, out_dtype=BF16, tm=D, tn=1408, tk=LP, ta=True, name="d_w_up")
    dh1, dh1b, dg_ffn = rows(du, w_up_i, [h1, dh2], [g_ffn], _residual_norm_bwd, [F32, BF16], [(1, D)],
                             tk=D_FF, tb=True, name="d_n2_norm2_bwd")
    d_out = _matmul(merged, dh1b, out_dtype=BF16, tm=D, tn=D, tk=LP, ta=True, name="d_w_out")
    dmerged = mm(dh1b, w_out, out_dtype=F32, tn=D, tk=D, tb=True, name="d_merged")
    dp_sb, dproj = _merge_bwd(dmerged, p_sb, proj, None, 0, "merge_bwd_sb")
    dp_fx, dproj = _merge_bwd(dmerged, p_fx, proj, dproj, 1, "merge_bwd_fox")
    d_bsb = _matmul(o_sb, dp_sb, out_dtype=BF16, tm=W_ATT, tn=D, tk=LP, ta=True, name="d_w_branch_sb")
    d_bfx = _matmul(o_fx, dp_fx, out_dtype=BF16, tm=W_ATT, tn=D, tk=LP, ta=True, name="d_w_branch_fox")
    do_sb = mm(dp_sb, w_bsb, out_dtype=BF16, tn=W_ATT, tk=D, tb=True, name="d_o_sb")
    do_fx = mm(dp_fx, w_bfx, out_dtype=BF16, tn=W_ATT, tk=D, tb=True, name="d_o_fox")
    sent = early_grads(dict(w_branch_sb=d_bsb, w_branch_fox=d_bfx, w_out=d_out, w_up=d_up, conv_w=d_cw, w_down=d_down))
    dproj = _sb_bwd(proj, do_sb, rs, dproj, sent, nseq, 2, "sb_bwd")
    dproj, dc = _fox_bwd(proj, c, crow, o_fx32, lse, do_fx, dproj, nseq, 2, "fox_bwd")
    dproj, d_bf = _gate_bwd(proj, bf, dc, dproj, nseq, "gate_bwd")
    d_in = _matmul(n1, dproj, out_dtype=BF16, tm=D, tn=1792, tk=LP, ta=True, name="d_w_in")
    dh0, dg_mix = rows(dproj, w_in_p, [h0, dh1], [g_mix], _residual_norm_bwd_f32, [F32], [(1, D)],
                       tk=IN_P // 2, tb=True, after=last_grad(d_in), name="d_n1_norm1_bwd")
    dh0 = dh0.reshape(nseq, LP, D)
    grads = dict(meta_tokens=jnp.sum(dh0[:, :N_META], axis=0), norm_mix_g=dg_mix, b_forget=d_bf[:, :NH],
                 norm_ffn_g=dg_ffn, norm_final_g=dg_final)
    return loss[0, 0], _real_rows(dh0, nseq, "grad_x"), grads


REPL = (("norm_mix_g", D), ("norm_ffn_g", D), ("norm_final_g", D), ("b_forget", LANES))
REPL_ROWS = 32
META_ROWS = N_META * D // LANES


def _pack_repl(tree):
    rows = [jnp.pad(tree[name].reshape(-1), (0, n - tree[name].size)).reshape(-1, LANES) for name, n in REPL]
    packed = jnp.concatenate(rows, axis=0)
    return jnp.pad(packed, ((0, REPL_ROWS - packed.shape[0]), (0, 0)))


def _unpack_repl(packed, shapes):
    out, r = {}, 0
    for name, n in REPL:
        size = 1
        for s in shapes[name]:
            size *= s
        out[name] = packed[r:r + n // LANES].reshape(-1)[:size].reshape(shapes[name])
        r += n // LANES
    return out


def kernel(x, meta_tokens, norm_mix_g, w_in, b_forget, w_branch_sb, w_branch_fox, w_out, norm_ffn_g, w_up, conv_w, w_down, norm_final_g, loss_target, m_meta_tokens, m_norm_mix_g, m_w_in, m_b_forget, m_w_branch_sb, m_w_branch_fox, m_w_out, m_norm_ffn_g, m_w_up, m_conv_w, m_w_down, m_norm_final_g, v_meta_tokens, v_norm_mix_g, v_w_in, v_b_forget, v_w_branch_sb, v_w_branch_fox, v_w_out, v_norm_ffn_g, v_w_up, v_conv_w, v_w_down, v_norm_final_g):
    w = dict(meta_tokens=meta_tokens, norm_mix_g=norm_mix_g, w_in=w_in, b_forget=b_forget, w_branch_sb=w_branch_sb,
             w_branch_fox=w_branch_fox, w_out=w_out, norm_ffn_g=norm_ffn_g, w_up=w_up, conv_w=conv_w, w_down=w_down,
             norm_final_g=norm_final_g)
    m = dict(meta_tokens=m_meta_tokens, norm_mix_g=m_norm_mix_g, w_in=m_w_in, b_forget=m_b_forget,
             w_branch_sb=m_w_branch_sb, w_branch_fox=m_w_branch_fox, w_out=m_w_out, norm_ffn_g=m_norm_ffn_g,
             w_up=m_w_up, conv_w=m_conv_w, w_down=m_w_down, norm_final_g=m_norm_final_g)
    v = dict(meta_tokens=v_meta_tokens, norm_mix_g=v_norm_mix_g, w_in=v_w_in, b_forget=v_b_forget,
             w_branch_sb=v_w_branch_sb, w_branch_fox=v_w_branch_fox, w_out=v_w_out, norm_ffn_g=v_norm_ffn_g,
             w_up=v_w_up, conv_w=v_conv_w, w_down=v_w_down, norm_final_g=v_norm_final_g)
    shapes = {k: a.shape for k, a in w.items()}
    sharded = ("w_in", "w_branch_sb", "w_branch_fox", "w_out", "w_up", "w_down", "conv_w", "meta_tokens")
    mat = lambda tree, name: tree[name].reshape(tree[name].shape[-2:])

    def lane_pad(a, width):
        return jnp.pad(a, ((0, 0), (0, width - a.shape[1])))

    late = ("w_branch_sb", "w_branch_fox", "w_out", "w_up", "w_down", "conv_w")
    pending_w = {}
    g_meta, = _all_gather([mat(w, "meta_tokens")], "gather_meta")
    pending_w["in"], in_started = _remote_start(
        [lane_pad(mat(w, "w_in").astype(BF16), SHARD_P)], False, g_meta, "gather_w_in_start")
    meta_full = g_meta.transpose(1, 0, 2).reshape(N_META, D) + in_started[0, 0]

    def first_weights(after):
        g_in, = _remote_wait(pending_w["in"], after, "gather_w_in_wait")
        pending_w["late"], started = _remote_start(
            [mat(w, "w_branch_sb").astype(BF16), mat(w, "w_branch_fox").astype(BF16), mat(w, "w_out").astype(BF16),
             lane_pad(mat(w, "w_up").astype(BF16), SHARD_P), mat(w, "w_down").astype(BF16), mat(w, "conv_w")],
            False, g_in, "gather_late_start")
        w_in_p = _relayout(g_in, 1, IN_P, _gathered_to_full(IN_SHARD, _in_padded_to_orig), BF16, 256, "w_in_cols")[0]
        return w_in_p, started

    def late_weights(after):
        g_bsb, g_bfx, g_out, g_up, g_down, g_cw = _remote_wait(pending_w["late"], after, "gather_late_wait")
        w_up_i = _relayout(g_up, 1, 2 * D_FF, _gathered_to_full(UP_SHARD, _up_inter_to_orig), BF16, 256, "w_up_cols")[0]
        w_bsb = _relayout(g_bsb, 1, D, _gathered_to_full(ATT_SHARD, lambda d: d), BF16, 256, "w_bsb_cols")[0]
        w_bfx = _relayout(g_bfx, 1, D, _gathered_to_full(ATT_SHARD, lambda d: d), BF16, 256, "w_bfx_cols")[0]
        cw_full = g_cw.transpose(1, 0, 2).reshape(3, 2 * D_FF)
        cw_i = cw_full.reshape(3, 2, D_FF // FFC, FFC).transpose(0, 2, 1, 3).reshape(3, 2 * D_FF)
        return w_bsb, w_bfx, g_out.reshape(D, D), w_up_i, cw_i, g_down.reshape(D_FF, D)

    pending_g = {}

    def early_grads(g):
        d_cw = g["conv_w"].reshape(3, D_FF // FFC, 2, FFC).transpose(0, 2, 1, 3).reshape(3, 2 * D_FF)
        pending_g["early"], sent = _remote_start(
            [_relayout(g["w_branch_sb"][None], N_DEV, ATT_SHARD, _full_to_shards(ATT_SHARD, lambda c: c), BF16, 256, "d_w_bsb_shards"),
             _relayout(g["w_branch_fox"][None], N_DEV, ATT_SHARD, _full_to_shards(ATT_SHARD, lambda c: c), BF16, 256, "d_w_bfx_shards"),
             g["w_out"].reshape(N_DEV, D // N_DEV, D),
             _relayout(g["w_up"][None], N_DEV, SHARD_P, _full_to_shards(UP_SHARD, _UP_ORIG_TO_INTER.get), BF16, 256, "d_w_up_shards"),
             g["w_down"].reshape(N_DEV, D_FF // N_DEV, D),
             d_cw.reshape(3, N_DEV, UP_SHARD).transpose(1, 0, 2)], True, g["w_out"], "exchange_early_start")
        return sent

    def last_grad(d_in):
        shards = _relayout(d_in[None], N_DEV, SHARD_P, _full_to_shards(IN_SHARD, _IN_ORIG_TO_PADDED.get), BF16, 256, "d_w_in_shards")
        pending_g["last"], sent = _remote_start([shards], True, shards, "exchange_last_start")
        return sent

    loss, grad_x, grads = _local_step(
        x, loss_target, meta_full, norm_mix_g.reshape(1, D), b_forget,
        norm_ffn_g.reshape(1, D), norm_final_g.reshape(1, D), first_weights, late_weights, early_grads, last_grad)

    small = jnp.concatenate([_pack_repl(grads), grads["meta_tokens"].reshape(META_ROWS, LANES)], axis=0)
    small, = _all_gather([small], "gather_small_grads")
    me_idx = 4 * lax.axis_index("x") + 2 * lax.axis_index("y") + lax.axis_index("c")
    p_meta = lax.dynamic_slice_in_dim(small[:, REPL_ROWS:].reshape(N_DEV, N_META, D), me_idx * ATT_SHARD, ATT_SHARD, axis=2)

    p_in, = _remote_wait(pending_g["last"], small, "exchange_last_wait")
    parts = dict(zip(late, _remote_wait(pending_g["early"], p_in, "exchange_early_wait")), w_in=p_in, meta_tokens=p_meta)
    tiles = dict(w_in=256, w_branch_sb=256, w_branch_fox=256, w_out=D // N_DEV, w_up=256, w_down=D_FF // N_DEV,
                 conv_w=3, meta_tokens=N_META)
    new = {name: _sum_adamw(parts[name], mat(w, name), mat(m, name), mat(v, name), tiles[name], "adamw_" + name)
           for name in sharded}

    routs = _sum_adamw(small[:, :REPL_ROWS], _pack_repl(w), _pack_repl(m), _pack_repl(v), REPL_ROWS, "adamw_replicated")
    repl = [_unpack_repl(o, shapes) for o in routs]

    result = [lax.psum(loss, ("x", "y", "c")), grad_x]
    for k in range(4):
        for name in w:
            result.append(new[name][k].reshape(shapes[name]) if name in new else repl[k][name])
    return tuple(result)
```

```python
import functools

import jax
import jax.numpy as jnp
from jax import lax
from jax.experimental import pallas as pl
from jax.experimental.pallas import tpu as pltpu

F32 = jnp.float32
BF16 = jnp.bfloat16

N_DEV = 8
LANES = 128
D = 1024
N_META = 16
SEQ = 2048
L_REAL = N_META + SEQ
LP = 2304
BQ = 256
NBLK = LP // BQ
HEAD = 64
NH = 8
W_ATT = NH * HEAD
PAIR_W = 3 * LANES
D_FF = 2816
IN_COLS = 5128
QKV = 6 * W_ATT
IN_P = 5376
GATE_COL = QKV
F_COL = QKV + 2 * D
FFC = 256
RMS_EPS = 1e-6
LR, B1, B2, EPS, WD, STEP = 0.001, 0.9, 0.999, 1e-08, 0.01, 10
VMEM_LIMIT = 56 * 1024 * 1024

MESH = pl.DeviceIdType.MESH
ANY = pl.BlockSpec(memory_space=pl.ANY)


def _cparams(*sem):
    return pltpu.CompilerParams(dimension_semantics=sem if sem else None, vmem_limit_bytes=VMEM_LIMIT)


def _all_gather(xs, name):
    n = len(xs)

    def body(*refs):
        x_refs, out_refs = refs[:n], refs[n:2 * n]
        send_sems, recv_sems, local_sems = refs[2 * n:]
        mx, my, mc = lax.axis_index("x"), lax.axis_index("y"), lax.axis_index("c")
        me, sibling = (mx, my, mc), (mx, my, 1 - mc)
        chips = [(1 - mx, my), (mx, 1 - my), (1 - mx, 1 - my)]

        def copy(a, k, block, to, own=False):
            px, py, pc = block
            slot = out_refs[a].at[4 * px + 2 * py + pc]
            return pltpu.make_async_remote_copy(
                src_ref=x_refs[a] if own else slot, dst_ref=slot,
                send_sem=send_sems.at[7 * a + k], recv_sem=recv_sems.at[7 * a + k],
                device_id=to, device_id_type=MESH)

        mine = [pltpu.make_async_copy(x_refs[a], out_refs[a].at[4 * mx + 2 * my + mc], local_sems.at[a]) for a in range(n)]
        for cp in mine:
            cp.start()
        first = []
        for a in range(n):
            first.append(copy(a, 0, me, sibling, own=True))
            first += [copy(a, 1 + j, me, (*chip, mc), own=True) for j, chip in enumerate(chips)]
        for cp in first:
            cp.start()
        passed = []
        for j, chip in enumerate(chips):
            for a in range(n):
                copy(a, 1 + j, (*chip, mc), me).wait_recv()
                fwd = copy(a, 4 + j, (*chip, mc), sibling)
                fwd.start()
                passed.append(fwd)
        for a in range(n):
            copy(a, 0, sibling, me).wait_recv()
            for j, chip in enumerate(chips):
                copy(a, 4 + j, (*chip, 1 - mc), me).wait_recv()
        for cp in first + passed:
            cp.wait_send()
        for cp in mine:
            cp.wait()

    return pl.pallas_call(
        body, name=name,
        out_shape=tuple(jax.ShapeDtypeStruct((N_DEV,) + x.shape, x.dtype) for x in xs),
        in_specs=[ANY] * n, out_specs=tuple([ANY] * n),
        scratch_shapes=[pltpu.SemaphoreType.DMA((7 * n,)), pltpu.SemaphoreType.DMA((7 * n,)),
                        pltpu.SemaphoreType.DMA((n,))],
    )(*xs)


HBM = pl.BlockSpec(memory_space=pltpu.HBM)
SEM = pl.BlockSpec(memory_space=pltpu.SEMAPHORE)
EFFECT = pltpu.SideEffectType.DATAFLOW_SIDE_EFFECTING


def _peer_copies(src_refs, land_refs, send_sems, recv_sems, per_peer):
    mx, my, mc = lax.axis_index("x"), lax.axis_index("y"), lax.axis_index("c")
    me_idx = 4 * mx + 2 * my + mc
    copies = []
    for k in range(1, N_DEV):
        px, py, pc = mx ^ (k >> 2), my ^ ((k >> 1) & 1), mc ^ (k & 1)
        for a, (src, land) in enumerate(zip(src_refs, land_refs)):
            copies.append(pltpu.make_async_remote_copy(
                src_ref=src.at[4 * px + 2 * py + pc] if per_peer else src, dst_ref=land.at[me_idx],
                send_sem=send_sems.at[7 * a + k - 1], recv_sem=recv_sems.at[7 * a + k - 1],
                device_id=(px, py, pc), device_id_type=MESH))
    return me_idx, copies


def _remote_start(srcs, per_peer, after, name):
    n = len(srcs)
    lands = [lax.empty(s.shape if per_peer else (N_DEV,) + s.shape, s.dtype) for s in srcs]

    def body(*refs):
        src_refs, land_refs = refs[:n], refs[n:2 * n]
        send_sems, recv_sems = refs[2 * n + 1:2 * n + 3]
        token = refs[4 * n + 3]
        stage, local_sems = refs[4 * n + 4:5 * n + 4], refs[5 * n + 4]
        me_idx, copies = _peer_copies(src_refs, land_refs, send_sems, recv_sems, per_peer)
        for cp in copies:
            cp.start()
        own = [src_refs[a].at[me_idx] if per_peer else src_refs[a] for a in range(n)]
        for hop in ([(own[a], stage[a]) for a in range(n)], [(stage[a], land_refs[a].at[me_idx]) for a in range(n)]):
            cps = [pltpu.make_async_copy(s, d, local_sems.at[a]) for a, (s, d) in enumerate(hop)]
            for cp in cps:
                cp.start()
            for cp in cps:
                cp.wait()
        token[...] = jnp.zeros_like(token)

    thru = [pltpu.HBM(a.shape, a.dtype) for a in list(srcs) + lands]
    out = pl.pallas_call(
        body, name=name,
        out_shape=(pltpu.SemaphoreType.DMA((7 * n,)), pltpu.SemaphoreType.DMA((7 * n,)), *thru,
                   jax.ShapeDtypeStruct((8, LANES), F32)),
        in_specs=[HBM] * (2 * n) + [ANY],
        out_specs=(SEM, SEM, *([HBM] * (2 * n)), pl.BlockSpec(memory_space=pltpu.VMEM)),
        input_output_aliases={i: 2 + i for i in range(2 * n)},
        scratch_shapes=[pltpu.VMEM(s.shape[1:] if per_peer else s.shape, s.dtype) for s in srcs]
        + [pltpu.SemaphoreType.DMA((n,))],
        compiler_params=pltpu.CompilerParams(has_side_effects=EFFECT),
    )(*[pltpu.with_memory_space_constraint(a, pltpu.HBM) for a in list(srcs) + lands], after)
    return dict(sems=out[:2], bufs=out[2:2 * n + 2], per_peer=per_peer), out[-1]


def _remote_wait(pending, after, name):
    bufs = pending["bufs"]
    n = len(bufs) // 2
    per_peer = pending["per_peer"]

    def body(*refs):
        src_refs, land_refs = refs[:n], refs[n:2 * n]
        send_sems, recv_sems = refs[2 * n:2 * n + 2]
        _, copies = _peer_copies(src_refs, land_refs, send_sems, recv_sems, per_peer)
        for cp in copies:
            cp.wait_send()
        for cp in copies:
            cp.wait_recv()

    out = pl.pallas_call(
        body, name=name, out_shape=tuple(pltpu.HBM(a.shape, a.dtype) for a in bufs),
        in_specs=[HBM] * (2 * n) + [SEM, SEM, ANY], out_specs=tuple([HBM] * (2 * n)),
        input_output_aliases={i: i for i in range(2 * n)},
        compiler_params=pltpu.CompilerParams(has_side_effects=EFFECT),
    )(*bufs, *pending["sems"], after)
    return out[n:]


ROWS_PER_COPY = 256


def _pad_rows(front, body_rows, nseq, name):
    tail = LP - L_REAL
    nblk = SEQ // ROWS_PER_COPY

    def body(f_ref, b_ref, o_ref, z_ref, sems):
        s, i = pl.program_id(0), pl.program_id(1)
        rows = pltpu.make_async_copy(b_ref, o_ref.at[pl.ds(s, 1), pl.ds(N_META + i * ROWS_PER_COPY, ROWS_PER_COPY)], sems.at[0])
        rows.start()

        @pl.when(i == 0)
        def _():
            z_ref[...] = jnp.zeros_like(z_ref)
            head = pltpu.make_async_copy(f_ref, o_ref.at[s, pl.ds(0, N_META)], sems.at[1])
            zeros = pltpu.make_async_copy(z_ref, o_ref.at[s, pl.ds(L_REAL, tail)], sems.at[2])
            head.start()
            zeros.start()
            head.wait()
            zeros.wait()

        rows.wait()

    return pl.pallas_call(
        body, name=name, out_shape=jax.ShapeDtypeStruct((nseq, LP, D), F32), grid=(nseq, nblk),
        in_specs=[pl.BlockSpec((N_META, D), lambda s, i: (0, 0)), pl.BlockSpec((1, ROWS_PER_COPY, D), lambda s, i: (s, i, 0))],
        out_specs=ANY,
        scratch_shapes=[pltpu.VMEM((tail, D), F32), pltpu.SemaphoreType.DMA((3,))],
        compiler_params=_cparams("arbitrary", "arbitrary"))(front, body_rows)


def _plan_cols(n_q, n_dcols, src_of):
    plan = {}
    for q in range(n_q):
        for dblk in range(n_dcols // LANES):
            segs, key, start = [], None, 0
            for lane in range(LANES + 1):
                new = None
                if lane < LANES:
                    src = src_of(q, dblk * LANES + lane)
                    if src is not None:
                        new = (src[0], src[1] // LANES, (lane - src[1] % LANES) % LANES)
                if new != key:
                    if key is not None:
                        segs.append((*key, start, lane))
                    key, start = new, lane
            plan[(q, dblk)] = segs
    return plan


def _relayout(src, n_q, n_dcols, src_of, out_dtype, tr, name):
    n_p, rows, scols = src.shape
    plan = _plan_cols(n_q, n_dcols, src_of)

    def body(s_ref, d_ref):
        lane = lax.broadcasted_iota(jnp.int32, (tr, LANES), 1)
        for (q, dblk), segs in plan.items():
            acc = jnp.zeros((tr, LANES), F32)
            for p, sblk, rot, lo, hi in segs:
                x = s_ref[p, :, sblk * LANES:(sblk + 1) * LANES].astype(F32)
                if rot:
                    x = pltpu.roll(x, rot, 1)
                acc = x if (lo, hi) == (0, LANES) else jnp.where((lane >= lo) & (lane < hi), x, acc)
            d_ref[q, :, dblk * LANES:(dblk + 1) * LANES] = acc.astype(out_dtype)

    return pl.pallas_call(
        body, name=name, out_shape=jax.ShapeDtypeStruct((n_q, rows, n_dcols), out_dtype), grid=(rows // tr,),
        in_specs=[pl.BlockSpec((n_p, tr, scols), lambda i: (0, i, 0))],
        out_specs=pl.BlockSpec((n_q, tr, n_dcols), lambda i: (0, i, 0)),
        compiler_params=_cparams("parallel"))(src)


def _in_padded_to_orig(d):
    if d < QKV:
        kind, r = divmod(d, 4 * PAIR_W)
        pair, r = divmod(r, PAIR_W)
        part, r = divmod(r, LANES)
        return kind * 3 * W_ATT + part * W_ATT + pair * LANES + r
    if d < F_COL:
        return d + NH
    if d < F_COL + NH:
        return d - 2 * D
    return None


_IN_ORIG_TO_PADDED = {_in_padded_to_orig(d): d for d in range(IN_P) if _in_padded_to_orig(d) is not None}


def _up_inter_to_orig(d):
    j, r = divmod(d, 2 * FFC)
    part, r = divmod(r, FFC)
    return part * D_FF + j * FFC + r


_UP_ORIG_TO_INTER = {_up_inter_to_orig(d): d for d in range(2 * D_FF)}
IN_SHARD = IN_COLS // N_DEV
UP_SHARD = 2 * D_FF // N_DEV
SHARD_P = 768
ATT_SHARD = D // N_DEV


def _gathered_to_full(n_shard, to_orig):
    def src_of(q, d):
        c = to_orig(d)
        return None if c is None else (c // n_shard, c % n_shard)
    return src_of


def _full_to_shards(n_shard, from_orig):
    def src_of(q, d):
        return (0, from_orig(q * n_shard + d)) if d < n_shard else None
    return src_of


def _matmul(a, b, *, out_dtype, tm, tn, tk, ta=False, tb=False, after=None, name):
    if ta:
        kdim, m = a.shape
    else:
        m, kdim = a.shape
    n = b.shape[0] if tb else b.shape[1]
    assert m % tm == 0 and n % tn == 0 and kdim % tk == 0, (name, a.shape, b.shape, tm, tn, tk)
    nk = kdim // tk

    def body(a_ref, b_ref, *rest):
        o_ref, scratch = rest[len(extra)], rest[len(extra) + 1:]
        av, bv = a_ref[...], b_ref[...]
        if ta:
            p = lax.dot_general(av, bv, (((0,), (0,)), ((), ())), preferred_element_type=F32)
        elif tb:
            p = lax.dot_general(av, bv, (((1,), (1,)), ((), ())), preferred_element_type=F32)
        else:
            p = jnp.dot(av, bv, preferred_element_type=F32)
        if nk == 1:
            o_ref[...] = p.astype(o_ref.dtype)
        else:
            acc_ref, = scratch
            k = pl.program_id(2)

            @pl.when(k == 0)
            def _():
                acc_ref[...] = p

            @pl.when(k > 0)
            def _():
                acc_ref[...] += p

            @pl.when(k == nk - 1)
            def _():
                o_ref[...] = acc_ref[...].astype(o_ref.dtype)

    extra = [] if after is None else [after]
    a_spec = pl.BlockSpec((tk, tm), lambda i, j, k: (k, i)) if ta else pl.BlockSpec((tm, tk), lambda i, j, k: (i, k))
    b_spec = pl.BlockSpec((tn, tk), lambda i, j, k: (j, k)) if tb else pl.BlockSpec((tk, tn), lambda i, j, k: (k, j))
    return pl.pallas_call(
        body, name=name,
        out_shape=jax.ShapeDtypeStruct((m, n), out_dtype),
        grid=(m // tm, n // tn, nk),
        in_specs=[a_spec, b_spec] + [ANY] * len(extra),
        out_specs=pl.BlockSpec((tm, tn), lambda i, j, k: (i, j)),
        scratch_shapes=[] if nk == 1 else [pltpu.VMEM((tm, tn), F32)],
        compiler_params=_cparams("parallel", "parallel", "arbitrary"),
    )(a, b, *extra)


TR = 288


def _rms(h):
    return lax.rsqrt(jnp.mean(h * h, axis=-1, keepdims=True) + RMS_EPS)


def _norm_fwd(h, g, name):
    t = h.shape[0]
    row = pl.BlockSpec((TR, D), lambda i: (i, 0))

    def body(h_ref, g_ref, n_ref):
        hv = h_ref[...]
        n_ref[...] = ((hv * _rms(hv)) * g_ref[...]).astype(BF16)

    return pl.pallas_call(
        body, name=name, out_shape=jax.ShapeDtypeStruct((t, D), BF16), grid=(t // TR,),
        in_specs=[row, pl.BlockSpec((1, D), lambda i: (0, 0))], out_specs=row, compiler_params=_cparams("parallel"))(h, g)


EPI_ROWS = 144


def _matmul_rows(a, b, rows_in, vecs_in, epilogue, row_outs, sum_outs, *, tm, tk, tb=False, after=None,
                 unpadded_in=None, unpadded_out=False, name):
    m, kdim = a.shape
    assert (b.shape[0] if tb else b.shape[1]) == D and m % tm == 0 and kdim % tk == 0 and tm % EPI_ROWS == 0
    nk = kdim // tk
    n_r, n_v, n_ro, n_so = len(rows_in), len(vecs_in), len(row_outs), len(sum_outs)
    extra = ([] if after is None else [after]) + ([] if unpadded_in is None else [unpadded_in])
    spans = []
    for q in range(LP // tm):
        lo, hi = max(q * tm, N_META), min((q + 1) * tm, L_REAL)
        spans.append((lo - N_META, hi - lo, lo - q * tm))

    def body(a_ref, b_ref, *rest):
        r_refs, v_refs = rest[:n_r], rest[n_r:n_r + n_v]
        outs = rest[n_r + n_v + len(extra):]
        ro_refs, so_refs, acc_ref = outs[:n_ro], outs[n_ro:n_ro + n_so], outs[n_ro + n_so]
        i, k = pl.program_id(0), pl.program_id(1)

        def unpadded_copies(do):
            src, tile, sem = rest[n_r + n_v + len(extra) - 1], outs[n_ro + n_so + 1], outs[n_ro + n_so + 2]
            for q, (src0, n, dst0) in enumerate(spans):
                @pl.when(i % len(spans) == q)
                def _():
                    do(pltpu.make_async_copy(src.at[i // len(spans), pl.ds(src0, n)], tile.at[pl.ds(dst0, n)], sem),
                       tile, dst0, n)

        def start(cp, tile, dst0, n):
            if dst0:
                tile[pl.ds(0, dst0), :] = jnp.zeros((dst0, D), F32)
            if dst0 + n < tm:
                tile[pl.ds(dst0 + n, tm - dst0 - n), :] = jnp.zeros((tm - dst0 - n, D), F32)
            cp.start()

        if unpadded_in is not None:
            @pl.when(k == 0)
            def _():
                unpadded_copies(start)

        def out_copies(tile_idx, do):
            stage, sems = outs[-2], outs[-1]
            seq = tile_idx // len(spans)
            for q, (dst0, n, src0) in enumerate(spans):
                @pl.when(tile_idx % len(spans) == q)
                def _():
                    do(pltpu.make_async_copy(stage.at[pl.ds(src0, n)], ro_refs[0].at[seq, pl.ds(dst0, n)], sems.at[0]))
                    if q == 0:
                        do(pltpu.make_async_copy(stage.at[pl.ds(0, N_META)], ro_refs[1].at[seq], sems.at[1]))
        if tb:
            p = lax.dot_general(a_ref[...], b_ref[...], (((1,), (1,)), ((), ())), preferred_element_type=F32)
        else:
            p = jnp.dot(a_ref[...], b_ref[...], preferred_element_type=F32)

        @pl.when(k == 0)
        def _():
            acc_ref[...] = p

        @pl.when(k > 0)
        def _():
            acc_ref[...] += p

        @pl.when(k == nk - 1)
        def _():
            vecs = [v[...] for v in v_refs]
            tiles_in = list(r_refs)
            if unpadded_in is not None:
                unpadded_copies(lambda cp, *_: cp.wait())
                tiles_in.append(outs[n_ro + n_so + 1])
            tile_outs = list(ro_refs)
            if unpadded_out:
                tile_outs = [outs[-2]] + tile_outs[2:]

                @pl.when(i > 0)
                def _():
                    out_copies(i - 1, lambda cp: cp.wait())

            def step(c, sums):
                rows = pl.ds(pl.multiple_of(c * EPI_ROWS, 8), EPI_ROWS)
                tiles, terms = epilogue(i * tm + c * EPI_ROWS, acc_ref[rows, :], *[r[rows, :] for r in tiles_in], *vecs)
                for o, tile in zip(tile_outs, tiles):
                    o[rows, :] = tile.astype(o.dtype)
                return tuple(s + term for s, term in zip(sums, terms))

            sums = lax.fori_loop(0, tm // EPI_ROWS, step, tuple(jnp.zeros(s, F32) for s in sum_outs))
            if unpadded_out:
                out_copies(i, lambda cp: cp.start())

                @pl.when(i == m // tm - 1)
                def _():
                    out_copies(i, lambda cp: cp.wait())

            @pl.when(i == 0)
            def _():
                for o in so_refs:
                    o[...] = jnp.zeros_like(o)

            for o, s in zip(so_refs, sums):
                o[...] += s

    row = pl.BlockSpec((tm, D), lambda i, k: (i, 0))
    b_spec = pl.BlockSpec((D, tk), lambda i, k: (0, k)) if tb else pl.BlockSpec((tk, D), lambda i, k: (k, 0))
    row_shapes = [jax.ShapeDtypeStruct((m, D), dt) for dt in row_outs]
    row_specs = [row] * len(row_outs)
    if unpadded_out:
        nseq = m // LP
        row_shapes[:1] = [jax.ShapeDtypeStruct((nseq, SEQ, D), F32), jax.ShapeDtypeStruct((nseq, N_META, D), F32)]
        row_specs[:1] = [ANY, ANY]
        n_ro += 1
    return pl.pallas_call(
        body, name=name,
        out_shape=tuple(row_shapes + [jax.ShapeDtypeStruct(s, F32) for s in sum_outs]),
        grid=(m // tm, nk),
        in_specs=[pl.BlockSpec((tm, tk), lambda i, k: (i, k)), b_spec] + [row] * n_r
        + [pl.BlockSpec((1, D), lambda i, k: (0, 0))] * n_v + [ANY] * len(extra),
        out_specs=tuple(row_specs + [pl.BlockSpec(s, lambda i, k: (0, 0)) for s in sum_outs]),
        scratch_shapes=[pltpu.VMEM((tm, D), F32)]
        + ([] if unpadded_in is None else [pltpu.VMEM((tm, D), F32), pltpu.SemaphoreType.DMA])
        + ([pltpu.VMEM((tm, D), F32), pltpu.SemaphoreType.DMA((2,))] if unpadded_out else []),
        compiler_params=_cparams("arbitrary", "arbitrary"))(a, b, *rows_in, *vecs_in, *extra)


def _residual_norm(row0, acc, h, g):
    hv = h + acc
    return (hv, (hv * _rms(hv)) * g), ()


def _rms_bwd_math(hv, dn, gv):
    r = _rms(hv)
    hr = hv * r
    dng = dn * gv
    dh = r * (dng - hr * jnp.mean(dng * hr, axis=-1, keepdims=True))
    return dh, dn * hr


def _loss_head(row0, acc, h1, tgt, g):
    hv = h1 + acc
    hr = hv * _rms(hv)
    pos = row0 % LP + lax.broadcasted_iota(jnp.int32, (EPI_ROWS, 1), 0)
    valid = (pos >= N_META) & (pos < L_REAL)
    err = jnp.where(valid, hr * g - tgt, 0.0)
    part = 0.5 * jnp.sum(jnp.mean(err * err, axis=-1, keepdims=True))
    dy = err * (1.0 / D)
    dh, dgrow = _rms_bwd_math(hv, dy, g)
    return (dh, dh), (jnp.full((8, LANES), part, F32), jnp.sum(dgrow, axis=0, keepdims=True))


def _residual_norm_bwd(row0, acc, h, dres, g):
    dh, dgrow = _rms_bwd_math(h, acc, g)
    dh = dh + dres
    return (dh, dh), (jnp.sum(dgrow, axis=0, keepdims=True),)


def _residual_norm_bwd_f32(row0, acc, h, dres, g):
    tiles, sums = _residual_norm_bwd(row0, acc, h, dres, g)
    return tiles[:1], sums


GATE_BLK = GATE_COL // D


def _sigmoid(x):
    return 1.0 / (1.0 + jnp.exp(-x))


def _merge_fwd(p_sb, p_fx, proj, name):
    t = p_sb.shape[0]
    row = pl.BlockSpec((TR, D), lambda i: (i, 0))

    def body(ps_ref, pf_ref, gs_ref, gf_ref, o_ref):
        o_ref[...] = (_sigmoid(gs_ref[...]) * ps_ref[...] + _sigmoid(gf_ref[...]) * pf_ref[...]).astype(BF16)

    return pl.pallas_call(
        body, name=name, out_shape=jax.ShapeDtypeStruct((t, D), BF16), grid=(t // TR,),
        in_specs=[row, row, pl.BlockSpec((TR, D), lambda i: (i, GATE_BLK)),
                  pl.BlockSpec((TR, D), lambda i: (i, GATE_BLK + 1))],
        out_specs=row, compiler_params=_cparams("parallel"))(p_sb, p_fx, proj, proj)


def _merge_bwd(dm, p, proj, dproj, which, name):
    t = dm.shape[0]
    row = pl.BlockSpec((TR, D), lambda i: (i, 0))
    gate = pl.BlockSpec((TR, D), lambda i: (i, GATE_BLK + which))

    def body(dm_ref, p_ref, g_ref, *rest):
        dp_ref, dg_ref = rest[-2:]
        dmv = dm_ref[...]
        s = _sigmoid(g_ref[...])
        dp_ref[...] = (dmv * s).astype(BF16)
        dg_ref[...] = (dmv * p_ref[...] * s * (1.0 - s)).astype(BF16)

    out_shape = (jax.ShapeDtypeStruct((t, D), BF16), jax.ShapeDtypeStruct((t, IN_P), BF16))
    if dproj is None:
        return pl.pallas_call(
            body, name=name, out_shape=out_shape, grid=(t // TR,), in_specs=[row, row, gate],
            out_specs=(row, gate), compiler_params=_cparams("parallel"))(dm, p, proj)
    return pl.pallas_call(
        body, name=name, out_shape=out_shape, grid=(t // TR,), in_specs=[row, row, gate, ANY],
        out_specs=(row, gate), input_output_aliases={3: 1}, compiler_params=_cparams("parallel"))(dm, p, proj, dproj)


CH = 288


def _chunk(c, n=CH):
    return pl.ds(pl.multiple_of(c * CH, 8), n)


def _conv_taps(u_ref, c):
    x = u_ref[_chunk(c), :]
    prev = u_ref[pl.ds(pl.multiple_of(jnp.maximum(c * CH - 8, 0), 8), 8), :]
    xx = jnp.concatenate([jnp.where(c == 0, 0.0, prev), x], axis=0)
    return x, pltpu.roll(xx, 1, 0)[8:], pltpu.roll(xx, 2, 0)[8:]


def _conv_glu_fwd(u, cw, nseq, name):
    nblk = D_FF // FFC

    def body(u_ref, cw_ref, o_ref):
        cwv = cw_ref[...]

        def step(c, _):
            x, x1, x2 = _conv_taps(u_ref, c)
            uc = cwv[0:1, :] * x2 + cwv[1:2, :] * x1 + cwv[2:3, :] * x
            a, b = uc[:, :FFC], uc[:, FFC:]
            o_ref[_chunk(c), :] = (a * _sigmoid(a) * b).astype(BF16)
            return 0

        lax.fori_loop(0, LP // CH, step, 0)

    return pl.pallas_call(
        body, name=name, out_shape=jax.ShapeDtypeStruct((nseq * LP, D_FF), BF16), grid=(nseq, nblk),
        in_specs=[pl.BlockSpec((LP, 2 * FFC), lambda s, j: (s, j)), pl.BlockSpec((3, 2 * FFC), lambda s, j: (0, j))],
        out_specs=pl.BlockSpec((LP, FFC), lambda s, j: (s, j)),
        compiler_params=_cparams("parallel", "parallel"))(u, cw)


def _conv_glu_bwd(u, cw, dact, nseq, name):
    nblk = D_FF // FFC
    nch = LP // CH

    def body(u_ref, cw_ref, da_ref, du_ref, dcw_ref):
        s = pl.program_id(1)
        cwv = cw_ref[...]

        def step(k, carry):
            nxt, p0, p1, p2 = carry
            c = nch - 1 - k
            x, x1, x2 = _conv_taps(u_ref, c)
            uc = cwv[0:1, :] * x2 + cwv[1:2, :] * x1 + cwv[2:3, :] * x
            a, b = uc[:, :FFC], uc[:, FFC:]
            sa = _sigmoid(a)
            dactv = da_ref[_chunk(c), :]
            da = dactv * b * (sa * (1.0 + a * (1.0 - sa)))
            db = dactv * (a * sa)
            duc = jnp.concatenate([da, db], axis=1)
            dd = jnp.concatenate([duc, nxt], axis=0)
            du = (cwv[2:3, :] * duc + cwv[1:2, :] * pltpu.roll(dd, CH + 7, 0)[:CH]
                  + cwv[0:1, :] * pltpu.roll(dd, CH + 6, 0)[:CH])
            du_ref[_chunk(c), :] = du.astype(BF16)
            return (duc[:8], p0 + jnp.sum(duc * x2, axis=0, keepdims=True),
                    p1 + jnp.sum(duc * x1, axis=0, keepdims=True), p2 + jnp.sum(duc * x, axis=0, keepdims=True))

        zrow = jnp.zeros((1, 2 * FFC), F32)
        _, p0, p1, p2 = lax.fori_loop(0, nch, step, (jnp.zeros((8, 2 * FFC), F32), zrow, zrow, zrow))

        @pl.when(s == 0)
        def _():
            dcw_ref[...] = jnp.zeros_like(dcw_ref)

        dcw_ref[...] += jnp.concatenate([p0, p1, p2], axis=0)

    return pl.pallas_call(
        body, name=name,
        out_shape=(jax.ShapeDtypeStruct((nseq * LP, 2 * D_FF), BF16), jax.ShapeDtypeStruct((3, 2 * D_FF), F32)),
        grid=(nblk, nseq),
        in_specs=[pl.BlockSpec((LP, 2 * FFC), lambda j, s: (s, j)), pl.BlockSpec((3, 2 * FFC), lambda j, s: (0, j)),
                  pl.BlockSpec((LP, FFC), lambda j, s: (s, j))],
        out_specs=(pl.BlockSpec((LP, 2 * FFC), lambda j, s: (s, j)), pl.BlockSpec((3, 2 * FFC), lambda j, s: (0, j))),
        compiler_params=_cparams("parallel", "arbitrary"))(u, cw, dact)


F_BLK = F_COL // LANES
CB = 128


def _split3(x):
    hi = x.astype(BF16)
    r1 = x - hi.astype(F32)
    mid = r1.astype(BF16)
    lo = (r1 - mid.astype(F32)).astype(BF16)
    return hi, mid, lo


def _tri_dot(tri, x):
    hi, mid, lo = _split3(x)
    d = functools.partial(jnp.dot, preferred_element_type=F32)
    return d(tri, hi) + d(tri, mid) + d(tri, lo)


def _log_sigmoid(x):
    return jnp.minimum(x, 0.0) - jnp.log(1.0 + jnp.exp(-jnp.abs(x)))


def _gate_fwd(proj, bf, nseq, name):
    def body(f_ref, b_ref, c_ref):
        r_i = lax.broadcasted_iota(jnp.int32, (CB, CB), 0)
        c_i = lax.broadcasted_iota(jnp.int32, (CB, CB), 1)
        tri = (c_i <= r_i).astype(BF16)
        bv = b_ref[...]

        def step(k, carry):
            rows = pl.ds(pl.multiple_of(k * CB, CB), CB)
            lf = _log_sigmoid(f_ref[rows, :] + bv)
            c_ref[rows, :] = _tri_dot(tri, lf) + carry
            return carry + jnp.sum(lf, axis=0, keepdims=True)

        lax.fori_loop(0, LP // CB, step, jnp.zeros((1, LANES), F32))

    return pl.pallas_call(
        body, name=name, out_shape=jax.ShapeDtypeStruct((nseq * LP, LANES), F32), grid=(nseq,),
        in_specs=[pl.BlockSpec((LP, LANES), lambda s: (s, F_BLK)), pl.BlockSpec((1, LANES), lambda s: (0, 0))],
        out_specs=pl.BlockSpec((LP, LANES), lambda s: (s, 0)),
        compiler_params=_cparams("parallel"))(proj, bf)


def _gate_bwd(proj, bf, dc, dproj, nseq, name):
    def body(f_ref, b_ref, dc_ref, _, df_ref, db_ref):
        s = pl.program_id(0)
        r_i = lax.broadcasted_iota(jnp.int32, (CB, CB), 0)
        c_i = lax.broadcasted_iota(jnp.int32, (CB, CB), 1)
        tri = (c_i >= r_i).astype(BF16)
        bv = b_ref[...]

        def step(kk, carry):
            carry_c, carry_b = carry
            k = LP // CB - 1 - kk
            rows = pl.ds(pl.multiple_of(k * CB, CB), CB)
            dcv = dc_ref[rows, :]
            dlf = _tri_dot(tri, dcv) + carry_c
            df = dlf * _sigmoid(-(f_ref[rows, :] + bv))
            df_ref[rows, :] = jnp.concatenate([df, jnp.zeros_like(df)], axis=1).astype(BF16)
            return carry_c + jnp.sum(dcv, axis=0, keepdims=True), carry_b + jnp.sum(df, axis=0, keepdims=True)

        zero = jnp.zeros((1, LANES), F32)
        _, dbp = lax.fori_loop(0, LP // CB, step, (zero, zero))

        @pl.when(s == 0)
        def _():
            db_ref[...] = jnp.zeros_like(db_ref)

        db_ref[...] += dbp

    return pl.pallas_call(
        body, name=name,
        out_shape=(jax.ShapeDtypeStruct(dproj.shape, BF16), jax.ShapeDtypeStruct((1, LANES), F32)), grid=(nseq,),
        in_specs=[pl.BlockSpec((LP, LANES), lambda s: (s, F_BLK)), pl.BlockSpec((1, LANES), lambda s: (0, 0)),
                  pl.BlockSpec((LP, LANES), lambda s: (s, 0)), ANY],
        out_specs=(pl.BlockSpec((LP, 2 * LANES), lambda s: (s, F_COL // (2 * LANES))), pl.BlockSpec((1, LANES), lambda s: (0, 0))),
        input_output_aliases={3: 0},
        compiler_params=_cparams("arbitrary"))(proj, bf, dc, dproj)


SCALE = 0.125
NEG = -1e30


def _dot_nt(a, b):
    return lax.dot_general(a, b, (((1,), (1,)), ((), ())), preferred_element_type=F32)


def _dot(a, b):
    return jnp.dot(a, b, preferred_element_type=F32)


def _blk(i, n=BQ):
    return pl.ds(pl.multiple_of(i * BQ, BQ), n)


def _query_blocks(qblock, n_last):
    def full(i, _):
        qblock(i, BQ)
        return 0

    lax.fori_loop(0, NBLK - 1, full, 0)
    qblock(jnp.minimum(pl.program_id(0) + NBLK, NBLK - 1), n_last)


def _tile_iotas():
    return lax.broadcasted_iota(jnp.int32, (BQ, BQ), 0), lax.broadcasted_iota(jnp.int32, (BQ, BQ), 1)


def _lane_iota():
    return lax.broadcasted_iota(jnp.int32, (BQ, LANES), 1)


def _head_masks():
    lane = _lane_iota()
    return lane < HEAD, lane >= HEAD


def _only(mask, x):
    return jnp.where(mask, x, jnp.zeros_like(x))


def _chains(npair):
    return [(pp, h) for pp in range(npair) for h in range(2)]


def _load_qkv(p_ref, q_s, k_s, v_s):
    for pp in range(q_s.shape[0]):
        base = pp * PAIR_W
        q_s[pp] = (p_ref[:, base:base + LANES] * SCALE).astype(BF16)
        k_s[pp] = p_ref[:, base + LANES:base + 2 * LANES].astype(BF16)
        v_s[pp] = p_ref[:, base + 2 * LANES:base + 3 * LANES].astype(BF16)


def _softplus(z):
    return jnp.maximum(z, 0.0) + jnp.log(1.0 + jnp.exp(-jnp.abs(z)))


def _sb_tile_weights(q, k, strict, r, u_suf):
    n = len(q)
    z = [_dot_nt(q[c], k[c]) for c in range(n)]
    sp = [_softplus(zc) for zc in z]
    lk = [-spc if strict is None else jnp.where(strict, -spc, 0.0) for spc in sp]
    suf = [_dot(lkc.astype(BF16), u_suf) for lkc in lk]
    w = [jnp.exp(z[c] - sp[c] + r[c] + suf[c]) for c in range(n)]
    if strict is not None:
        w = [jnp.where(strict, wc, 0.0) for wc in w]
    r_next = [r[c] + suf[c][:, 0:1] + lk[c][:, 0:1] for c in range(n)]
    return w, sp, r_next


def _group_spec(kind, npair):
    return pl.BlockSpec((LP, npair * PAIR_W), lambda s, g: (s, (NH // (2 * npair)) * kind + g))


def _gheads_spec(npair):
    return pl.BlockSpec((LP, npair * LANES), lambda s, g: (s, g))


def _qkv_scratch(npair):
    return [pltpu.VMEM((npair, LP, LANES), BF16)] * 3


SEQ_SPEC = pl.BlockSpec((LP, LANES), lambda s, g: (s, 0))
RS_STRIDE = 16
Q_LAST = 128
Q_LAST_ROWS = 16


def _pair_cols(pp):
    return slice(pp * LANES, (pp + 1) * LANES)


def _carry_spec(npair):
    return pl.BlockSpec((None, npair * LANES, LP), lambda s, g: (s, g, 0))


def _sb_fwd(proj, nseq, npair, name):
    t = nseq * LP
    chains = _chains(npair)

    def body(p_ref, o_ref, rs_ref, q_s, k_s, v_s, acc_ref, r_ref, rb_ref):
        _load_qkv(p_ref, q_s, k_s, v_s)
        row, col = _tile_iotas()
        u_suf = (row > col).astype(BF16)
        heads = _head_masks()

        def qblock(i, nq):
            diag = (col < row)[:nq]
            lane = _lane_iota()[:nq]
            heads_q = [hm[:nq] for hm in heads]
            acc_ref[...] = jnp.zeros_like(acc_ref)
            rb_ref[...] = jnp.zeros_like(rb_ref)
            r_ref[...] = jnp.zeros_like(r_ref)
            qb = [q_s[pp, _blk(i, nq), :] for pp in range(npair)]

            def tile(j, strict):
                kj = [k_s[pp, _blk(j), :] for pp in range(npair)]
                vj = [v_s[pp, _blk(j), :] for pp in range(npair)]
                r = [r_ref[c, :nq] for c in range(len(chains))]
                w, _, r_next = _sb_tile_weights([_only(heads_q[h], qb[pp]) for pp, h in chains],
                                                [kj[pp] for pp, _ in chains], strict, r, u_suf)
                pv = [_dot(w[c].astype(BF16), _only(heads[h], vj[pp])) for c, (pp, h) in enumerate(chains)]
                for pp in range(npair):
                    acc_ref[pp, :nq] += pv[2 * pp] + pv[2 * pp + 1]
                    rb_ref[pp, :nq] = jnp.where(lane == j, r[2 * pp], jnp.where(lane == RS_STRIDE + j, r[2 * pp + 1], rb_ref[pp, :nq]))
                for c in range(len(chains)):
                    r_ref[c, :nq] = r_next[c]

            tile(i, diag)

            def kblock(jj, _):
                tile(i - jj, None)
                return 0

            lax.fori_loop(1, i + 1, kblock, 0)
            for pp in range(npair):
                o_ref[_blk(i), _pair_cols(pp)] = acc_ref[pp].astype(BF16)
                rs_ref[_pair_cols(pp), _blk(i)] = rb_ref[pp].T

        _query_blocks(qblock, Q_LAST_ROWS)

    return pl.pallas_call(
        body, name=name,
        out_shape=(jax.ShapeDtypeStruct((t, W_ATT), BF16), jax.ShapeDtypeStruct((nseq, W_ATT, LP), F32)),
        grid=(nseq, NH // (2 * npair)), in_specs=[_group_spec(0, npair)], out_specs=(_gheads_spec(npair), _carry_spec(npair)),
        scratch_shapes=_qkv_scratch(npair) + [pltpu.VMEM((npair, BQ, LANES), F32), pltpu.VMEM((2 * npair, BQ, 1), F32),
                                      pltpu.VMEM((npair, BQ, LANES), F32)],
        compiler_params=_cparams("parallel", "parallel"))(proj)


def _sb_bwd(proj, do, rs, dproj, after, nseq, npair, name):
    chains = _chains(npair)

    def body(p_ref, do_ref, rs_ref, _, _after, dp_ref, q_s, k_s, v_s, kt_s, dqa_ref, dka_ref, dva_ref, ep_ref):
        _load_qkv(p_ref, q_s, k_s, v_s)
        row, col = _tile_iotas()
        u_after = (col > row).astype(BF16)
        u_before = (col < row).astype(BF16)
        heads = _head_masks()
        sub_all = lax.broadcasted_iota(jnp.int32, (LANES, BQ), 0)
        rows_k = (sub_all < HEAD, sub_all >= HEAD)
        nc = len(chains)
        for pp in range(npair):
            kt_s[pp] = k_s[pp].astype(F32).T.astype(BF16)
        dka_ref[...] = jnp.zeros_like(dka_ref)
        dva_ref[...] = jnp.zeros_like(dva_ref)

        def qblock(i, nq):
            diag = (row < col)[:, :nq]
            rows_of = [m[:, :nq] for m in rows_k]
            heads_q = [m[:nq] for m in heads]
            queries = _blk(i, nq)
            qb = [q_s[pp, queries, :] for pp in range(npair)]
            dob = [do_ref[queries, _pair_cols(pp)] for pp in range(npair)]
            qt = [qb[pp].astype(F32).T.astype(BF16) for pp in range(npair)]
            dot = [dob[pp].astype(F32).T.astype(BF16) for pp in range(npair)]
            qt_m = [jnp.where(rows_of[h], qt[pp], jnp.zeros_like(qt[pp])) for pp, h in chains]
            dot_m = [jnp.where(rows_of[h], dot[pp], jnp.zeros_like(dot[pp])) for pp, h in chains]
            q_m = [_only(heads_q[h], qb[pp]) for pp, h in chains]
            do_m = [_only(heads_q[h], dob[pp]) for pp, h in chains]
            dqa_ref[...] = jnp.zeros_like(dqa_ref)
            ep_ref[...] = jnp.zeros_like(ep_ref)

            def tile(j, strict):
                keys = pl.ds(pl.multiple_of(j * BQ, BQ), BQ)
                kj = [k_s[pp, _blk(j), :] for pp in range(npair)]
                vj = [v_s[pp, _blk(j), :] for pp in range(npair)]
                r = [_key_cols(rs_ref, pp * LANES + RS_STRIDE * h + j, i, nq) for pp, h in chains]
                z = [_dot(kj[pp], qt_m[cidx]) for cidx, (pp, _) in enumerate(chains)]
                dw = [_dot(vj[pp], dot_m[cidx]) for cidx, (pp, _) in enumerate(chains)]
                sp = [_softplus(zc) for zc in z]
                lk = [-spc if strict is None else jnp.where(strict, -spc, 0.0) for spc in sp]
                suf = [_dot(u_after, lkc.astype(BF16)) for lkc in lk]
                w = [jnp.exp(z[cidx] - sp[cidx] + r[cidx] + suf[cidx]) for cidx in range(nc)]
                if strict is not None:
                    w = [jnp.where(strict, wc, 0.0) for wc in w]
                e = [dw[cidx] * w[cidx] for cidx in range(nc)]
                e_pre = [ep_ref[cidx, :, :nq] + _dot(u_before, e[cidx].astype(BF16)) for cidx in range(nc)]
                dz = []
                for cidx in range(nc):
                    ep_ref[cidx, :, :nq] += jnp.sum(e[cidx], axis=0, keepdims=True)
                    sneg = jnp.exp(-sp[cidx])
                    dzc = e[cidx] * sneg - (1.0 - sneg) * e_pre[cidx]
                    if strict is not None:
                        dzc = jnp.where(strict, dzc, 0.0)
                    dz.append(dzc.astype(BF16))
                dq = [_dot(jnp.where(rows_k[h], kt_s[pp, :, keys], jnp.zeros((LANES, BQ), BF16)), dz[cidx])
                      for cidx, (pp, h) in enumerate(chains)]
                dk = [_dot(dz[cidx], q_m[cidx]) for cidx in range(nc)]
                dv = [_dot(w[cidx].astype(BF16), do_m[cidx]) for cidx in range(nc)]
                for pp in range(npair):
                    dqa_ref[pp, :, :nq] += dq[2 * pp] + dq[2 * pp + 1]
                    dka_ref[pp, _blk(j), :] += dk[2 * pp] + dk[2 * pp + 1]
                    dva_ref[pp, _blk(j), :] += dv[2 * pp] + dv[2 * pp + 1]

            def kblock(j, _):
                tile(j, None)
                return 0

            lax.fori_loop(0, i, kblock, 0)
            tile(i, diag)
            for pp in range(npair):
                dp_ref[_blk(i), pp * PAIR_W:pp * PAIR_W + LANES] = (dqa_ref[pp].T * SCALE).astype(BF16)

        _query_blocks(qblock, Q_LAST)
        for pp in range(npair):
            dp_ref[:, pp * PAIR_W + LANES:pp * PAIR_W + 2 * LANES] = dka_ref[pp].astype(BF16)
            dp_ref[:, pp * PAIR_W + 2 * LANES:pp * PAIR_W + 3 * LANES] = dva_ref[pp].astype(BF16)

    return pl.pallas_call(
        body, name=name, out_shape=jax.ShapeDtypeStruct(dproj.shape, BF16), grid=(nseq, NH // (2 * npair)),
        in_specs=[_group_spec(0, npair), _gheads_spec(npair), _carry_spec(npair), ANY, ANY], out_specs=_group_spec(0, npair),
        input_output_aliases={3: 0},
        scratch_shapes=_qkv_scratch(npair) + [pltpu.VMEM((npair, LANES, LP), BF16), pltpu.VMEM((npair, LANES, BQ), F32),
                                      pltpu.VMEM((npair, LP, LANES), F32), pltpu.VMEM((npair, LP, LANES), F32),
                                      pltpu.VMEM((2 * npair, 1, BQ), F32)],
        compiler_params=_cparams("parallel", "parallel"))(proj, do, rs, dproj, after)


CROW_SPEC = pl.BlockSpec((None, NH, LP), lambda s, g: (s, 0, 0))


def _key_cols(cr_ref, head, j, n=BQ):
    return cr_ref[pl.ds(head, 1), _blk(j)][:, :n]


def _fox_fwd(proj, c, crow, nseq, npair, name):
    t = nseq * LP
    chains = _chains(npair)

    def body(p_ref, c_ref, cr_ref, o_ref, o32_ref, lse_ref, q_s, k_s, v_s, vt_s, ck_s, acc_ref, m_ref, l_ref):
        _load_qkv(p_ref, q_s, k_s, v_s)
        row, col = _tile_iotas()
        sub_all = lax.broadcasted_iota(jnp.int32, (LANES, BQ), 0)
        rows_k = (sub_all < HEAD, sub_all >= HEAD)
        head0 = 2 * npair * pl.program_id(1)
        nc = len(chains)
        lane_all = lax.broadcasted_iota(jnp.int32, (LP, LANES), 1)
        for pp in range(npair):
            vt_s[pp] = v_s[pp].astype(F32).T.astype(BF16)
        for cidx in range(nc):
            ck_s[cidx] = jnp.sum(jnp.where(lane_all == head0 + cidx, c_ref[...], 0.0), axis=1, keepdims=True)

        def qblock(i, nq):
            diag = (row <= col)[:, :nq]
            sub = sub_all[:, :nq]
            rows_of = (sub < HEAD, sub >= HEAD)
            qt = [q_s[pp, _blk(i, nq), :].astype(F32).T.astype(BF16) for pp in range(npair)]
            qt = [jnp.where(rows_of[h], qt[pp], jnp.zeros_like(qt[pp])) for pp, h in chains]
            cq = [_key_cols(cr_ref, head0 + cidx, i, nq) for cidx in range(nc)]
            acc_ref[...] = jnp.zeros_like(acc_ref)
            m_ref[...] = jnp.full_like(m_ref, NEG)
            l_ref[...] = jnp.zeros_like(l_ref)

            def tile(j, causal):
                keys = pl.ds(pl.multiple_of(j * BQ, BQ), BQ)
                z = [_dot(k_s[pp, _blk(j), :], qt[cidx]) + (cq[cidx] - ck_s[cidx, _blk(j), :])
                     for cidx, (pp, _) in enumerate(chains)]
                if causal is not None:
                    z = [jnp.where(causal, zc, NEG) for zc in z]
                p, alpha = [], []
                for cidx in range(nc):
                    m_old = m_ref[cidx, :, :nq]
                    m_new = jnp.maximum(m_old, jnp.max(z[cidx], axis=0, keepdims=True))
                    alpha.append(jnp.exp(m_old - m_new))
                    pc = jnp.exp(z[cidx] - m_new)
                    l_ref[cidx, :, :nq] = alpha[cidx] * l_ref[cidx, :, :nq] + jnp.sum(pc, axis=0, keepdims=True)
                    m_ref[cidx, :, :nq] = m_new
                    p.append(pc.astype(BF16))
                pv = [_dot(jnp.where(rows_k[h], vt_s[pp, :, keys], jnp.zeros((LANES, BQ), BF16)), p[cidx])
                      for cidx, (pp, h) in enumerate(chains)]
                for cidx in range(nc):
                    acc_ref[cidx, :, :nq] = alpha[cidx] * acc_ref[cidx, :, :nq] + pv[cidx]

            def kblock(j, _):
                tile(j, None)
                return 0

            lax.fori_loop(0, i, kblock, 0)
            tile(i, diag)
            for pp in range(npair):
                acc = [acc_ref[2 * pp + h, :, :nq] for h in range(2)]
                l = [l_ref[2 * pp + h, :, :nq] for h in range(2)]
                out = (acc[0] / l[0] + acc[1] / l[1]).T
                o_ref[_blk(i, nq), _pair_cols(pp)] = out.astype(BF16)
                o32_ref[_blk(i, nq), _pair_cols(pp)] = out
                lse = [m_ref[2 * pp + h, :, :nq] + jnp.log(l[h]) for h in range(2)]
                lse_t = jnp.where(sub == 0, lse[0], jnp.where(sub == 1, lse[1], 0.0))
                lse_ref[_blk(i, nq), _pair_cols(pp)] = lse_t.T
                if nq < BQ:
                    rest = pl.ds(pl.multiple_of(i * BQ + nq, nq), BQ - nq)
                    o_ref[rest, _pair_cols(pp)] = jnp.zeros((BQ - nq, LANES), BF16)
                    o32_ref[rest, _pair_cols(pp)] = jnp.zeros((BQ - nq, LANES), F32)
                    lse_ref[rest, _pair_cols(pp)] = jnp.zeros((BQ - nq, LANES), F32)

        _query_blocks(qblock, Q_LAST)

    return pl.pallas_call(
        body, name=name,
        out_shape=(jax.ShapeDtypeStruct((t, W_ATT), BF16), jax.ShapeDtypeStruct((t, W_ATT), F32),
                   jax.ShapeDtypeStruct((t, W_ATT), F32)),
        grid=(nseq, NH // (2 * npair)), in_specs=[_group_spec(1, npair), SEQ_SPEC, CROW_SPEC], out_specs=(_gheads_spec(npair), _gheads_spec(npair), _gheads_spec(npair)),
        scratch_shapes=_qkv_scratch(npair) + [pltpu.VMEM((npair, LANES, LP), BF16), pltpu.VMEM((2 * npair, LP, 1), F32),
                                      pltpu.VMEM((2 * npair, LANES, BQ), F32), pltpu.VMEM((2 * npair, 1, BQ), F32),
                                      pltpu.VMEM((2 * npair, 1, BQ), F32)],
        compiler_params=_cparams("parallel", "parallel"))(proj, c, crow)


def _fox_bwd(proj, c, crow, o32, lse, do, dproj, nseq, npair, name):
    t = nseq * LP
    chains = _chains(npair)

    def body(p_ref, c_ref, cr_ref, o_ref, lse_ref, do_ref, _, dp_ref, dc_ref,
             q_s, k_s, v_s, kt_s, ck_s, dqa_ref, dka_ref, dva_ref, rsum_ref):
        _load_qkv(p_ref, q_s, k_s, v_s)
        row, col = _tile_iotas()
        lane = _lane_iota()
        heads = _head_masks()
        sub_all = lax.broadcasted_iota(jnp.int32, (LANES, BQ), 0)
        rows_k = (sub_all < HEAD, sub_all >= HEAD)
        group = pl.program_id(1)
        head0 = 2 * npair * group
        nc = len(chains)
        lane_all = lax.broadcasted_iota(jnp.int32, (LP, LANES), 1)
        for pp in range(npair):
            kt_s[pp] = k_s[pp].astype(F32).T.astype(BF16)
        for cidx in range(nc):
            ck_s[cidx] = jnp.sum(jnp.where(lane_all == head0 + cidx, c_ref[...], 0.0), axis=1, keepdims=True)
        dka_ref[...] = jnp.zeros_like(dka_ref)
        dva_ref[...] = jnp.zeros_like(dva_ref)

        @pl.when(group == 0)
        def _():
            dc_ref[...] = jnp.zeros_like(dc_ref)

        def qblock(i, nq):
            diag = (row <= col)[:, :nq]
            sub = sub_all[:, :nq]
            rows_of = [m[:, :nq] for m in rows_k]
            heads_q = [m[:nq] for m in heads]
            queries = _blk(i, nq)
            dqa_ref[...] = jnp.zeros_like(dqa_ref)
            rsum_ref[...] = jnp.zeros_like(rsum_ref)
            qb = [q_s[pp, queries, :] for pp in range(npair)]
            dob = [do_ref[queries, _pair_cols(pp)] for pp in range(npair)]
            qt = [qb[pp].astype(F32).T.astype(BF16) for pp in range(npair)]
            dot = [dob[pp].astype(F32).T for pp in range(npair)]
            prod = [dot[pp] * o_ref[queries, _pair_cols(pp)].T for pp in range(npair)]
            lse_t = [lse_ref[queries, _pair_cols(pp)].T for pp in range(npair)]
            qt_m = [jnp.where(rows_of[h], qt[pp], jnp.zeros_like(qt[pp])) for pp, h in chains]
            dot_m = [jnp.where(rows_of[h], dot[pp], 0.0).astype(BF16) for pp, h in chains]
            q_m = [_only(heads_q[h], qb[pp]) for pp, h in chains]
            do_m = [_only(heads_q[h], dob[pp]) for pp, h in chains]
            cq = [_key_cols(cr_ref, head0 + cidx, i, nq) for cidx in range(nc)]
            lse_i = [lse_t[pp][h:h + 1, :] for pp, h in chains]
            delta = [jnp.sum(jnp.where(rows_of[h], prod[pp], 0.0), axis=0, keepdims=True) for pp, h in chains]

            def tile(j, causal):
                keys = pl.ds(pl.multiple_of(j * BQ, BQ), BQ)
                kj = [k_s[pp, _blk(j), :] for pp in range(npair)]
                vj = [v_s[pp, _blk(j), :] for pp in range(npair)]
                z = [_dot(kj[pp], qt_m[cidx]) + (cq[cidx] - ck_s[cidx, _blk(j), :]) for cidx, (pp, _) in enumerate(chains)]
                if causal is not None:
                    z = [jnp.where(causal, zc, NEG) for zc in z]
                dpv = [_dot(vj[pp], dot_m[cidx]) for cidx, (pp, _) in enumerate(chains)]
                p = [jnp.exp(z[cidx] - lse_i[cidx]) for cidx in range(nc)]
                ds = [p[cidx] * (dpv[cidx] - delta[cidx]) for cidx in range(nc)]
                dsb = [d.astype(BF16) for d in ds]
                dq = [_dot(jnp.where(rows_k[h], kt_s[pp, :, keys], jnp.zeros((LANES, BQ), BF16)), dsb[cidx])
                      for cidx, (pp, h) in enumerate(chains)]
                dk = [_dot(dsb[cidx], q_m[cidx]) for cidx in range(nc)]
                dv = [_dot(p[cidx].astype(BF16), do_m[cidx]) for cidx in range(nc)]
                for pp in range(npair):
                    dqa_ref[pp, :, :nq] += dq[2 * pp] + dq[2 * pp + 1]
                    dka_ref[pp, _blk(j), :] += dk[2 * pp] + dk[2 * pp + 1]
                    dva_ref[pp, _blk(j), :] += dv[2 * pp] + dv[2 * pp + 1]
                col_sums = jnp.zeros((BQ, LANES), F32)
                for cidx in range(nc):
                    col_sums = col_sums + jnp.where(lane == head0 + cidx, jnp.sum(ds[cidx], axis=1, keepdims=True), 0.0)
                    rsum_ref[cidx, :, :nq] += jnp.sum(ds[cidx], axis=0, keepdims=True)
                dc_ref[_blk(j), :] = dc_ref[_blk(j), :] - col_sums

            def kblock(j, _):
                tile(j, None)
                return 0

            lax.fori_loop(0, i, kblock, 0)
            tile(i, diag)
            row_sums = jnp.zeros((LANES, nq), F32)
            for cidx in range(nc):
                row_sums = row_sums + jnp.where(sub == head0 + cidx, rsum_ref[cidx, :, :nq], 0.0)
            dc_ref[queries, :] += row_sums.T
            for pp in range(npair):
                dp_ref[_blk(i), pp * PAIR_W:pp * PAIR_W + LANES] = (dqa_ref[pp].T * SCALE).astype(BF16)

        _query_blocks(qblock, Q_LAST)
        for pp in range(npair):
            dp_ref[:, pp * PAIR_W + LANES:pp * PAIR_W + 2 * LANES] = dka_ref[pp].astype(BF16)
            dp_ref[:, pp * PAIR_W + 2 * LANES:pp * PAIR_W + 3 * LANES] = dva_ref[pp].astype(BF16)

    return pl.pallas_call(
        body, name=name,
        out_shape=(jax.ShapeDtypeStruct(dproj.shape, BF16), jax.ShapeDtypeStruct((t, LANES), F32)),
        grid=(nseq, NH // (2 * npair)),
        in_specs=[_group_spec(1, npair), SEQ_SPEC, CROW_SPEC, _gheads_spec(npair), _gheads_spec(npair), _gheads_spec(npair), ANY],
        out_specs=(_group_spec(1, npair), SEQ_SPEC),
        input_output_aliases={6: 0},
        scratch_shapes=_qkv_scratch(npair) + [pltpu.VMEM((npair, LANES, LP), BF16), pltpu.VMEM((2 * npair, LP, 1), F32),
                                      pltpu.VMEM((npair, LANES, BQ), F32), pltpu.VMEM((npair, LP, LANES), F32),
                                      pltpu.VMEM((npair, LP, LANES), F32), pltpu.VMEM((2 * npair, 1, BQ), F32)],
        compiler_params=_cparams("parallel", "arbitrary"))(proj, c, crow, o32, lse, do, dproj)


def _adamw_math(w, g, m, v):
    m = B1 * m + (1.0 - B1) * g
    v = B2 * v + (1.0 - B2) * (g * g)
    m_hat = m / (1.0 - B1 ** STEP)
    v_hat = v / (1.0 - B2 ** STEP)
    delta = -LR * (m_hat / (jnp.sqrt(v_hat) + EPS) + WD * w)
    return delta, m, v


def _sum_adamw(parts, w, m, v, tr, name):
    rows, cols = w.shape
    cp = parts.shape[2]
    assert rows % tr == 0 and parts.shape[1] == rows

    def body(p_ref, w_ref, m_ref, v_ref, g_ref, d_ref, nm_ref, nv_ref):
        gsum = p_ref[0].astype(F32)
        for s in range(1, N_DEV):
            gsum = gsum + p_ref[s].astype(F32)
        gsum = gsum[:, :cols]
        d, nm, nv = _adamw_math(w_ref[...], gsum, m_ref[...], v_ref[...])
        g_ref[...] = gsum
        d_ref[...] = d
        nm_ref[...] = nm
        nv_ref[...] = nv

    blk = pl.BlockSpec((tr, cols), lambda i: (i, 0))
    out = jax.ShapeDtypeStruct((rows, cols), F32)
    return pl.pallas_call(
        body, name=name, out_shape=(out, out, out, out), grid=(rows // tr,),
        in_specs=[pl.BlockSpec((N_DEV, tr, cp), lambda i: (0, i, 0)), blk, blk, blk],
        out_specs=(blk, blk, blk, blk), compiler_params=_cparams("parallel"))(parts, w, m, v)


def _local_step(x, tgt, meta, g_mix, b_forget, g_ffn, g_final, first_weights, late_weights, early_grads, last_grad):
    nseq = x.shape[0]
    t = nseq * LP
    tm = LP // 2
    mm = functools.partial(_matmul, tm=tm)

    h0 = _pad_rows(meta, x, nseq, "pad_x").reshape(t, D)
    bf = jnp.pad(b_forget.reshape(1, NH), ((0, 0), (0, LANES - NH)))

    n1 = _norm_fwd(h0, g_mix, "norm1")
    w_in_p, started = first_weights(n1)
    proj = mm(n1, w_in_p, out_dtype=F32, tn=1792, tk=D, after=started, name="in_proj")
    c = _gate_fwd(proj, bf, nseq, "gate_fwd")
    crow = c[:, :NH].reshape(nseq, LP, NH).transpose(0, 2, 1)
    o_sb, rs = _sb_fwd(proj, nseq, 2, "sb_fwd")
    o_fx, o_fx32, lse = _fox_fwd(proj, c, crow, nseq, 2, "fox_fwd")
    w_bsb, w_bfx, w_out, w_up_i, cw_i, w_down = late_weights(o_fx)
    p_sb = mm(o_sb, w_bsb, out_dtype=F32, tn=D, tk=W_ATT, name="branch_sb")
    p_fx = mm(o_fx, w_bfx, out_dtype=F32, tn=D, tk=W_ATT, name="branch_fox")
    merged = _merge_fwd(p_sb, p_fx, proj, "merge_fwd")
    rows = functools.partial(_matmul_rows, tm=LP // 4)
    h1, n2 = rows(merged, w_out, [h0], [g_ffn], _residual_norm, [F32, BF16], [], tk=D, name="out_proj_norm2")
    u = mm(n2, w_up_i, out_dtype=F32, tn=1408, tk=D, name="up_proj")
    act = _conv_glu_fwd(u, cw_i, nseq, "conv_glu_fwd")

    dh2, dh2b, loss, dg_final = rows(act, w_down, [h1], [g_final], _loss_head, [F32, BF16],
                                     [(8, LANES), (1, D)], tk=D_FF, unpadded_in=tgt, name="down_proj_loss")
    d_down = _matmul(act, dh2b, out_dtype=BF16, tm=1408, tn=D, tk=LP, ta=True, name="d_w_down")
    dact = mm(dh2b, w_down, out_dtype=F32, tn=1408, tk=D, tb=True, name="d_act")
    du, d_cw = _conv_glu_bwd(u, cw_i, dact, nseq, "conv_glu_bwd")
    d_up = _matmul(n2, du, out_dtype=BF16, tm=D, tn=1408, tk=LP, ta=True, name="d_w_up")
    dh1, dh1b, dg_ffn = rows(du, w_up_i, [h1, dh2], [g_ffn], _residual_norm_bwd, [F32, BF16], [(1, D)],
                             tk=D_FF, tb=True, name="d_n2_norm2_bwd")
    d_out = _matmul(merged, dh1b, out_dtype=BF16, tm=D, tn=D, tk=LP, ta=True, name="d_w_out")
    dmerged = mm(dh1b, w_out, out_dtype=F32, tn=D, tk=D, tb=True, name="d_merged")
    dp_sb, dproj = _merge_bwd(dmerged, p_sb, proj, None, 0, "merge_bwd_sb")
    dp_fx, dproj = _merge_bwd(dmerged, p_fx, proj, dproj, 1, "merge_bwd_fox")
    d_bsb = _matmul(o_sb, dp_sb, out_dtype=BF16, tm=W_ATT, tn=D, tk=LP, ta=True, name="d_w_branch_sb")
    d_bfx = _matmul(o_fx, dp_fx, out_dtype=BF16, tm=W_ATT, tn=D, tk=LP, ta=True, name="d_w_branch_fox")
    do_sb = mm(dp_sb, w_bsb, out_dtype=BF16, tn=W_ATT, tk=D, tb=True, name="d_o_sb")
    do_fx = mm(dp_fx, w_bfx, out_dtype=BF16, tn=W_ATT, tk=D, tb=True, name="d_o_fox")
    sent = early_grads(dict(w_branch_sb=d_bsb, w_branch_fox=d_bfx, w_out=d_out, w_up=d_up, conv_w=d_cw, w_down=d_down))
    dproj = _sb_bwd(proj, do_sb, rs, dproj, sent, nseq, 2, "sb_bwd")
    dproj, dc = _fox_bwd(proj, c, crow, o_fx32, lse, do_fx, dproj, nseq, 2, "fox_bwd")
    dproj, d_bf = _gate_bwd(proj, bf, dc, dproj, nseq, "gate_bwd")
    d_in = _matmul(n1, dproj, out_dtype=BF16, tm=D, tn=1792, tk=LP, ta=True, name="d_w_in")
    grad_x, d_front, dg_mix = rows(dproj, w_in_p, [h0, dh1], [g_mix], _residual_norm_bwd_f32, [F32], [(1, D)],
                                   tk=IN_P // 2, tb=True, after=last_grad(d_in), unpadded_out=True, name="d_n1_norm1_bwd")
    grads = dict(meta_tokens=jnp.sum(d_front, axis=0), norm_mix_g=dg_mix, b_forget=d_bf[:, :NH],
                 norm_ffn_g=dg_ffn, norm_final_g=dg_final)
    return loss[0, 0], grad_x, grads


REPL = (("norm_mix_g", D), ("norm_ffn_g", D), ("norm_final_g", D), ("b_forget", LANES))
REPL_ROWS = 32
META_ROWS = N_META * D // LANES


def _pack_repl(tree):
    rows = [jnp.pad(tree[name].reshape(-1), (0, n - tree[name].size)).reshape(-1, LANES) for name, n in REPL]
    packed = jnp.concatenate(rows, axis=0)
    return jnp.pad(packed, ((0, REPL_ROWS - packed.shape[0]), (0, 0)))


def _unpack_repl(packed, shapes):
    out, r = {}, 0
    for name, n in REPL:
        size = 1
        for s in shapes[name]:
            size *= s
        out[name] = packed[r:r + n // LANES].reshape(-1)[:size].reshape(shapes[name])
        r += n // LANES
    return out


def kernel(x, meta_tokens, norm_mix_g, w_in, b_forget, w_branch_sb, w_branch_fox, w_out, norm_ffn_g, w_up, conv_w, w_down, norm_final_g, loss_target, m_meta_tokens, m_norm_mix_g, m_w_in, m_b_forget, m_w_branch_sb, m_w_branch_fox, m_w_out, m_norm_ffn_g, m_w_up, m_conv_w, m_w_down, m_norm_final_g, v_meta_tokens, v_norm_mix_g, v_w_in, v_b_forget, v_w_branch_sb, v_w_branch_fox, v_w_out, v_norm_ffn_g, v_w_up, v_conv_w, v_w_down, v_norm_final_g):
    w = dict(meta_tokens=meta_tokens, norm_mix_g=norm_mix_g, w_in=w_in, b_forget=b_forget, w_branch_sb=w_branch_sb,
             w_branch_fox=w_branch_fox, w_out=w_out, norm_ffn_g=norm_ffn_g, w_up=w_up, conv_w=conv_w, w_down=w_down,
             norm_final_g=norm_final_g)
    m = dict(meta_tokens=m_meta_tokens, norm_mix_g=m_norm_mix_g, w_in=m_w_in, b_forget=m_b_forget,
             w_branch_sb=m_w_branch_sb, w_branch_fox=m_w_branch_fox, w_out=m_w_out, norm_ffn_g=m_norm_ffn_g,
             w_up=m_w_up, conv_w=m_conv_w, w_down=m_w_down, norm_final_g=m_norm_final_g)
    v = dict(meta_tokens=v_meta_tokens, norm_mix_g=v_norm_mix_g, w_in=v_w_in, b_forget=v_b_forget,
             w_branch_sb=v_w_branch_sb, w_branch_fox=v_w_branch_fox, w_out=v_w_out, norm_ffn_g=v_norm_ffn_g,
             w_up=v_w_up, conv_w=v_conv_w, w_down=v_w_down, norm_final_g=v_norm_final_g)
    shapes = {k: a.shape for k, a in w.items()}
    sharded = ("w_in", "w_branch_sb", "w_branch_fox", "w_out", "w_up", "w_down", "conv_w", "meta_tokens")
    mat = lambda tree, name: tree[name].reshape(tree[name].shape[-2:])

    def lane_pad(a, width):
        return jnp.pad(a, ((0, 0), (0, width - a.shape[1])))

    late = ("w_branch_sb", "w_branch_fox", "w_out", "w_up", "w_down", "conv_w")
    pending_w = {}
    g_meta, = _all_gather([mat(w, "meta_tokens")], "gather_meta")
    pending_w["in"], in_started = _remote_start(
        [lane_pad(mat(w, "w_in").astype(BF16), SHARD_P)], False, g_meta, "gather_w_in_start")
    meta_full = g_meta.transpose(1, 0, 2).reshape(N_META, D) + in_started[0, 0]

    def first_weights(after):
        g_in, = _remote_wait(pending_w["in"], after, "gather_w_in_wait")
        pending_w["late"], started = _remote_start(
            [mat(w, "w_branch_sb").astype(BF16), mat(w, "w_branch_fox").astype(BF16), mat(w, "w_out").astype(BF16),
             lane_pad(mat(w, "w_up").astype(BF16), SHARD_P), mat(w, "w_down").astype(BF16), mat(w, "conv_w")],
            False, g_in, "gather_late_start")
        w_in_p = _relayout(g_in, 1, IN_P, _gathered_to_full(IN_SHARD, _in_padded_to_orig), BF16, 256, "w_in_cols")[0]
        return w_in_p, started

    def late_weights(after):
        g_bsb, g_bfx, g_out, g_up, g_down, g_cw = _remote_wait(pending_w["late"], after, "gather_late_wait")
        w_up_i = _relayout(g_up, 1, 2 * D_FF, _gathered_to_full(UP_SHARD, _up_inter_to_orig), BF16, 256, "w_up_cols")[0]
        w_bsb = _relayout(g_bsb, 1, D, _gathered_to_full(ATT_SHARD, lambda d: d), BF16, 256, "w_bsb_cols")[0]
        w_bfx = _relayout(g_bfx, 1, D, _gathered_to_full(ATT_SHARD, lambda d: d), BF16, 256, "w_bfx_cols")[0]
        cw_full = g_cw.transpose(1, 0, 2).reshape(3, 2 * D_FF)
        cw_i = cw_full.reshape(3, 2, D_FF // FFC, FFC).transpose(0, 2, 1, 3).reshape(3, 2 * D_FF)
        return w_bsb, w_bfx, g_out.reshape(D, D), w_up_i, cw_i, g_down.reshape(D_FF, D)

    pending_g = {}

    def early_grads(g):
        d_cw = g["conv_w"].reshape(3, D_FF // FFC, 2, FFC).transpose(0, 2, 1, 3).reshape(3, 2 * D_FF)
        pending_g["early"], sent = _remote_start(
            [_relayout(g["w_branch_sb"][None], N_DEV, ATT_SHARD, _full_to_shards(ATT_SHARD, lambda c: c), BF16, 256, "d_w_bsb_shards"),
             _relayout(g["w_branch_fox"][None], N_DEV, ATT_SHARD, _full_to_shards(ATT_SHARD, lambda c: c), BF16, 256, "d_w_bfx_shards"),
             g["w_out"].reshape(N_DEV, D // N_DEV, D),
             _relayout(g["w_up"][None], N_DEV, SHARD_P, _full_to_shards(UP_SHARD, _UP_ORIG_TO_INTER.get), BF16, 256, "d_w_up_shards"),
             g["w_down"].reshape(N_DEV, D_FF // N_DEV, D),
             d_cw.reshape(3, N_DEV, UP_SHARD).transpose(1, 0, 2)], True, g["w_out"], "exchange_early_start")
        return sent

    def last_grad(d_in):
        shards = _relayout(d_in[None], N_DEV, SHARD_P, _full_to_shards(IN_SHARD, _IN_ORIG_TO_PADDED.get), BF16, 256, "d_w_in_shards")
        pending_g["last"], sent = _remote_start([shards], True, shards, "exchange_last_start")
        return sent

    loss, grad_x, grads = _local_step(
        x, loss_target, meta_full, norm_mix_g.reshape(1, D), b_forget,
        norm_ffn_g.reshape(1, D), norm_final_g.reshape(1, D), first_weights, late_weights, early_grads, last_grad)

    small = jnp.concatenate([_pack_repl(grads), grads["meta_tokens"].reshape(META_ROWS, LANES)], axis=0)
    small, = _all_gather([small], "gather_small_grads")
    me_idx = 4 * lax.axis_index("x") + 2 * lax.axis_index("y") + lax.axis_index("c")
    p_meta = lax.dynamic_slice_in_dim(small[:, REPL_ROWS:].reshape(N_DEV, N_META, D), me_idx * ATT_SHARD, ATT_SHARD, axis=2)

    p_in, = _remote_wait(pending_g["last"], small, "exchange_last_wait")
    parts = dict(zip(late, _remote_wait(pending_g["early"], p_in, "exchange_early_wait")), w_in=p_in, meta_tokens=p_meta)
    tiles = dict(w_in=256, w_branch_sb=256, w_branch_fox=256, w_out=D // N_DEV, w_up=256, w_down=D_FF // N_DEV,
                 conv_w=3, meta_tokens=N_META)
    new = {name: _sum_adamw(parts[name], mat(w, name), mat(m, name), mat(v, name), tiles[name], "adamw_" + name)
           for name in sharded}

    routs = _sum_adamw(small[:, :REPL_ROWS], _pack_repl(w), _pack_repl(m), _pack_repl(v), REPL_ROWS, "adamw_replicated")
    repl = [_unpack_repl(o, shapes) for o in routs]

    result = [lax.psum(loss, ("x", "y", "c")), grad_x]
    for k in range(4):
        for name in w:
            result.append(new[name][k].reshape(shapes[name]) if name in new else repl[k][name])
    return tuple(result)
```

```python
import functools

import jax
import jax.numpy as jnp
from jax import lax
from jax.experimental import pallas as pl
from jax.experimental.pallas import tpu as pltpu

F32 = jnp.float32
BF16 = jnp.bfloat16

N_DEV = 8
LANES = 128
D = 1024
N_META = 16
SEQ = 2048
L_REAL = N_META + SEQ
LP = 2304
BQ = 256
NBLK = LP // BQ
HEAD = 64
NH = 8
W_ATT = NH * HEAD
PAIR_W = 3 * LANES
D_FF = 2816
IN_COLS = 5128
QKV = 6 * W_ATT
IN_P = 5376
GATE_COL = QKV
F_COL = QKV + 2 * D
FFC = 256
RMS_EPS = 1e-6
LR, B1, B2, EPS, WD, STEP = 0.001, 0.9, 0.999, 1e-08, 0.01, 10
VMEM_LIMIT = 56 * 1024 * 1024

MESH = pl.DeviceIdType.MESH
ANY = pl.BlockSpec(memory_space=pl.ANY)


def _cparams(*sem):
    return pltpu.CompilerParams(dimension_semantics=sem if sem else None, vmem_limit_bytes=VMEM_LIMIT)


def _all_gather(xs, name):
    n = len(xs)

    def body(*refs):
        x_refs, out_refs = refs[:n], refs[n:2 * n]
        send_sems, recv_sems, local_sems = refs[2 * n:]
        mx, my, mc = lax.axis_index("x"), lax.axis_index("y"), lax.axis_index("c")
        me, sibling = (mx, my, mc), (mx, my, 1 - mc)
        chips = [(1 - mx, my), (mx, 1 - my), (1 - mx, 1 - my)]

        def copy(a, k, block, to, own=False):
            px, py, pc = block
            slot = out_refs[a].at[4 * px + 2 * py + pc]
            return pltpu.make_async_remote_copy(
                src_ref=x_refs[a] if own else slot, dst_ref=slot,
                send_sem=send_sems.at[7 * a + k], recv_sem=recv_sems.at[7 * a + k],
                device_id=to, device_id_type=MESH)

        mine = [pltpu.make_async_copy(x_refs[a], out_refs[a].at[4 * mx + 2 * my + mc], local_sems.at[a]) for a in range(n)]
        for cp in mine:
            cp.start()
        first = []
        for a in range(n):
            first.append(copy(a, 0, me, sibling, own=True))
            first += [copy(a, 1 + j, me, (*chip, mc), own=True) for j, chip in enumerate(chips)]
        for cp in first:
            cp.start()
        passed = []
        for j, chip in enumerate(chips):
            for a in range(n):
                copy(a, 1 + j, (*chip, mc), me).wait_recv()
                fwd = copy(a, 4 + j, (*chip, mc), sibling)
                fwd.start()
                passed.append(fwd)
        for a in range(n):
            copy(a, 0, sibling, me).wait_recv()
            for j, chip in enumerate(chips):
                copy(a, 4 + j, (*chip, 1 - mc), me).wait_recv()
        for cp in first + passed:
            cp.wait_send()
        for cp in mine:
            cp.wait()

    return pl.pallas_call(
        body, name=name,
        out_shape=tuple(jax.ShapeDtypeStruct((N_DEV,) + x.shape, x.dtype) for x in xs),
        in_specs=[ANY] * n, out_specs=tuple([ANY] * n),
        scratch_shapes=[pltpu.SemaphoreType.DMA((7 * n,)), pltpu.SemaphoreType.DMA((7 * n,)),
                        pltpu.SemaphoreType.DMA((n,))],
    )(*xs)


HBM = pl.BlockSpec(memory_space=pltpu.HBM)
SEM = pl.BlockSpec(memory_space=pltpu.SEMAPHORE)
EFFECT = pltpu.SideEffectType.DATAFLOW_SIDE_EFFECTING


def _peer_copies(src_refs, land_refs, send_sems, recv_sems, per_peer):
    mx, my, mc = lax.axis_index("x"), lax.axis_index("y"), lax.axis_index("c")
    me_idx = 4 * mx + 2 * my + mc
    copies = []
    for k in range(1, N_DEV):
        px, py, pc = mx ^ (k >> 2), my ^ ((k >> 1) & 1), mc ^ (k & 1)
        for a, (src, land) in enumerate(zip(src_refs, land_refs)):
            copies.append(pltpu.make_async_remote_copy(
                src_ref=src.at[4 * px + 2 * py + pc] if per_peer else src, dst_ref=land.at[me_idx],
                send_sem=send_sems.at[7 * a + k - 1], recv_sem=recv_sems.at[7 * a + k - 1],
                device_id=(px, py, pc), device_id_type=MESH))
    return me_idx, copies


def _remote_start(srcs, per_peer, after, name):
    n = len(srcs)
    lands = [lax.empty(s.shape if per_peer else (N_DEV,) + s.shape, s.dtype) for s in srcs]

    def body(*refs):
        src_refs, land_refs = refs[:n], refs[n:2 * n]
        send_sems, recv_sems = refs[2 * n + 1:2 * n + 3]
        token = refs[4 * n + 3]
        stage, local_sems = refs[4 * n + 4:5 * n + 4], refs[5 * n + 4]
        me_idx, copies = _peer_copies(src_refs, land_refs, send_sems, recv_sems, per_peer)
        for cp in copies:
            cp.start()
        own = [src_refs[a].at[me_idx] if per_peer else src_refs[a] for a in range(n)]
        for hop in ([(own[a], stage[a]) for a in range(n)], [(stage[a], land_refs[a].at[me_idx]) for a in range(n)]):
            cps = [pltpu.make_async_copy(s, d, local_sems.at[a]) for a, (s, d) in enumerate(hop)]
            for cp in cps:
                cp.start()
            for cp in cps:
                cp.wait()
        token[...] = jnp.zeros_like(token)

    thru = [pltpu.HBM(a.shape, a.dtype) for a in list(srcs) + lands]
    out = pl.pallas_call(
        body, name=name,
        out_shape=(pltpu.SemaphoreType.DMA((7 * n,)), pltpu.SemaphoreType.DMA((7 * n,)), *thru,
                   jax.ShapeDtypeStruct((8, LANES), F32)),
        in_specs=[HBM] * (2 * n) + [ANY],
        out_specs=(SEM, SEM, *([HBM] * (2 * n)), pl.BlockSpec(memory_space=pltpu.VMEM)),
        input_output_aliases={i: 2 + i for i in range(2 * n)},
        scratch_shapes=[pltpu.VMEM(s.shape[1:] if per_peer else s.shape, s.dtype) for s in srcs]
        + [pltpu.SemaphoreType.DMA((n,))],
        compiler_params=pltpu.CompilerParams(has_side_effects=EFFECT),
    )(*[pltpu.with_memory_space_constraint(a, pltpu.HBM) for a in list(srcs) + lands], after)
    return dict(sems=out[:2], bufs=out[2:2 * n + 2], per_peer=per_peer), out[-1]


def _remote_wait(pending, after, name):
    bufs = pending["bufs"]
    n = len(bufs) // 2
    per_peer = pending["per_peer"]

    def body(*refs):
        src_refs, land_refs = refs[:n], refs[n:2 * n]
        send_sems, recv_sems = refs[2 * n:2 * n + 2]
        _, copies = _peer_copies(src_refs, land_refs, send_sems, recv_sems, per_peer)
        for cp in copies:
            cp.wait_send()
        for cp in copies:
            cp.wait_recv()

    out = pl.pallas_call(
        body, name=name, out_shape=tuple(pltpu.HBM(a.shape, a.dtype) for a in bufs),
        in_specs=[HBM] * (2 * n) + [SEM, SEM, ANY], out_specs=tuple([HBM] * (2 * n)),
        input_output_aliases={i: i for i in range(2 * n)},
        compiler_params=pltpu.CompilerParams(has_side_effects=EFFECT),
    )(*bufs, *pending["sems"], after)
    return out[n:]


ROWS_PER_COPY = 512


def _pad_rows(front, body_rows, nseq, name):
    tail = LP - L_REAL
    nblk = SEQ // ROWS_PER_COPY

    def body(f_ref, b_ref, o_ref, z_ref, sems):
        s, i = pl.program_id(0), pl.program_id(1)
        rows = pltpu.make_async_copy(b_ref, o_ref.at[pl.ds(s, 1), pl.ds(N_META + i * ROWS_PER_COPY, ROWS_PER_COPY)], sems.at[0])
        rows.start()

        @pl.when(i == 0)
        def _():
            z_ref[...] = jnp.zeros_like(z_ref)
            head = pltpu.make_async_copy(f_ref, o_ref.at[s, pl.ds(0, N_META)], sems.at[1])
            zeros = pltpu.make_async_copy(z_ref, o_ref.at[s, pl.ds(L_REAL, tail)], sems.at[2])
            head.start()
            zeros.start()
            head.wait()
            zeros.wait()

        rows.wait()

    return pl.pallas_call(
        body, name=name, out_shape=jax.ShapeDtypeStruct((nseq, LP, D), F32), grid=(nseq, nblk),
        in_specs=[pl.BlockSpec((N_META, D), lambda s, i: (0, 0)), pl.BlockSpec((1, ROWS_PER_COPY, D), lambda s, i: (s, i, 0))],
        out_specs=ANY,
        scratch_shapes=[pltpu.VMEM((tail, D), F32), pltpu.SemaphoreType.DMA((3,))],
        compiler_params=_cparams("arbitrary", "arbitrary"))(front, body_rows)


def _plan_cols(n_q, n_dcols, src_of):
    plan = {}
    for q in range(n_q):
        for dblk in range(n_dcols // LANES):
            segs, key, start = [], None, 0
            for lane in range(LANES + 1):
                new = None
                if lane < LANES:
                    src = src_of(q, dblk * LANES + lane)
                    if src is not None:
                        new = (src[0], src[1] // LANES, (lane - src[1] % LANES) % LANES)
                if new != key:
                    if key is not None:
                        segs.append((*key, start, lane))
                    key, start = new, lane
            plan[(q, dblk)] = segs
    return plan


def _relayout(src, n_q, n_dcols, src_of, out_dtype, tr, name):
    n_p, rows, scols = src.shape
    plan = _plan_cols(n_q, n_dcols, src_of)

    def body(s_ref, d_ref):
        lane = lax.broadcasted_iota(jnp.int32, (tr, LANES), 1)
        for (q, dblk), segs in plan.items():
            acc = jnp.zeros((tr, LANES), F32)
            for p, sblk, rot, lo, hi in segs:
                x = s_ref[p, :, sblk * LANES:(sblk + 1) * LANES].astype(F32)
                if rot:
                    x = pltpu.roll(x, rot, 1)
                acc = x if (lo, hi) == (0, LANES) else jnp.where((lane >= lo) & (lane < hi), x, acc)
            d_ref[q, :, dblk * LANES:(dblk + 1) * LANES] = acc.astype(out_dtype)

    return pl.pallas_call(
        body, name=name, out_shape=jax.ShapeDtypeStruct((n_q, rows, n_dcols), out_dtype), grid=(rows // tr,),
        in_specs=[pl.BlockSpec((n_p, tr, scols), lambda i: (0, i, 0))],
        out_specs=pl.BlockSpec((n_q, tr, n_dcols), lambda i: (0, i, 0)),
        compiler_params=_cparams("parallel"))(src)


def _in_padded_to_orig(d):
    if d < QKV:
        kind, r = divmod(d, 4 * PAIR_W)
        pair, r = divmod(r, PAIR_W)
        part, r = divmod(r, LANES)
        return kind * 3 * W_ATT + part * W_ATT + pair * LANES + r
    if d < F_COL:
        return d + NH
    if d < F_COL + NH:
        return d - 2 * D
    return None


_IN_ORIG_TO_PADDED = {_in_padded_to_orig(d): d for d in range(IN_P) if _in_padded_to_orig(d) is not None}


def _up_inter_to_orig(d):
    j, r = divmod(d, 2 * FFC)
    part, r = divmod(r, FFC)
    return part * D_FF + j * FFC + r


_UP_ORIG_TO_INTER = {_up_inter_to_orig(d): d for d in range(2 * D_FF)}
IN_SHARD = IN_COLS // N_DEV
UP_SHARD = 2 * D_FF // N_DEV
SHARD_P = 768
ATT_SHARD = D // N_DEV


def _gathered_to_full(n_shard, to_orig):
    def src_of(q, d):
        c = to_orig(d)
        return None if c is None else (c // n_shard, c % n_shard)
    return src_of


def _full_to_shards(n_shard, from_orig):
    def src_of(q, d):
        return (0, from_orig(q * n_shard + d)) if d < n_shard else None
    return src_of


def _matmul(a, b, *, out_dtype, tm, tn, tk, ta=False, tb=False, after=None, name):
    if ta:
        kdim, m = a.shape
    else:
        m, kdim = a.shape
    n = b.shape[0] if tb else b.shape[1]
    assert m % tm == 0 and n % tn == 0 and kdim % tk == 0, (name, a.shape, b.shape, tm, tn, tk)
    nk = kdim // tk

    def body(a_ref, b_ref, *rest):
        o_ref, scratch = rest[len(extra)], rest[len(extra) + 1:]
        av, bv = a_ref[...], b_ref[...]
        if ta:
            p = lax.dot_general(av, bv, (((0,), (0,)), ((), ())), preferred_element_type=F32)
        elif tb:
            p = lax.dot_general(av, bv, (((1,), (1,)), ((), ())), preferred_element_type=F32)
        else:
            p = jnp.dot(av, bv, preferred_element_type=F32)
        if nk == 1:
            o_ref[...] = p.astype(o_ref.dtype)
        else:
            acc_ref, = scratch
            k = pl.program_id(2)

            @pl.when(k == 0)
            def _():
                acc_ref[...] = p

            @pl.when(k > 0)
            def _():
                acc_ref[...] += p

            @pl.when(k == nk - 1)
            def _():
                o_ref[...] = acc_ref[...].astype(o_ref.dtype)

    extra = [] if after is None else [after]
    a_spec = pl.BlockSpec((tk, tm), lambda i, j, k: (k, i)) if ta else pl.BlockSpec((tm, tk), lambda i, j, k: (i, k))
    b_spec = pl.BlockSpec((tn, tk), lambda i, j, k: (j, k)) if tb else pl.BlockSpec((tk, tn), lambda i, j, k: (k, j))
    return pl.pallas_call(
        body, name=name,
        out_shape=jax.ShapeDtypeStruct((m, n), out_dtype),
        grid=(m // tm, n // tn, nk),
        in_specs=[a_spec, b_spec] + [ANY] * len(extra),
        out_specs=pl.BlockSpec((tm, tn), lambda i, j, k: (i, j)),
        scratch_shapes=[] if nk == 1 else [pltpu.VMEM((tm, tn), F32)],
        compiler_params=_cparams("parallel", "parallel", "arbitrary"),
    )(a, b, *extra)


TR = 576


def _rms(h):
    return lax.rsqrt(jnp.mean(h * h, axis=-1, keepdims=True) + RMS_EPS)


def _norm_fwd(h, g, name):
    t = h.shape[0]
    row = pl.BlockSpec((TR, D), lambda i: (i, 0))

    def body(h_ref, g_ref, n_ref):
        hv = h_ref[...]
        n_ref[...] = ((hv * _rms(hv)) * g_ref[...]).astype(BF16)

    return pl.pallas_call(
        body, name=name, out_shape=jax.ShapeDtypeStruct((t, D), BF16), grid=(t // TR,),
        in_specs=[row, pl.BlockSpec((1, D), lambda i: (0, 0))], out_specs=row, compiler_params=_cparams("parallel"))(h, g)


EPI_ROWS = 144


def _matmul_rows(a, b, rows_in, vecs_in, epilogue, row_outs, sum_outs, *, tm, tk, tb=False, after=None,
                 unpadded_in=None, unpadded_out=False, name):
    m, kdim = a.shape
    assert (b.shape[0] if tb else b.shape[1]) == D and m % tm == 0 and kdim % tk == 0 and tm % EPI_ROWS == 0
    nk = kdim // tk
    n_r, n_v, n_ro, n_so = len(rows_in), len(vecs_in), len(row_outs), len(sum_outs)
    extra = ([] if after is None else [after]) + ([] if unpadded_in is None else [unpadded_in])
    spans = []
    for q in range(LP // tm):
        lo, hi = max(q * tm, N_META), min((q + 1) * tm, L_REAL)
        spans.append((lo - N_META, hi - lo, lo - q * tm))

    def body(a_ref, b_ref, *rest):
        r_refs, v_refs = rest[:n_r], rest[n_r:n_r + n_v]
        outs = rest[n_r + n_v + len(extra):]
        ro_refs, so_refs, acc_ref = outs[:n_ro], outs[n_ro:n_ro + n_so], outs[n_ro + n_so]
        i, k = pl.program_id(0), pl.program_id(1)

        def unpadded_copies(do):
            src, tile, sem = rest[n_r + n_v + len(extra) - 1], outs[n_ro + n_so + 1], outs[n_ro + n_so + 2]
            for q, (src0, n, dst0) in enumerate(spans):
                @pl.when(i % len(spans) == q)
                def _():
                    do(pltpu.make_async_copy(src.at[i // len(spans), pl.ds(src0, n)], tile.at[pl.ds(dst0, n)], sem),
                       tile, dst0, n)

        def start(cp, tile, dst0, n):
            if dst0:
                tile[pl.ds(0, dst0), :] = jnp.zeros((dst0, D), F32)
            if dst0 + n < tm:
                tile[pl.ds(dst0 + n, tm - dst0 - n), :] = jnp.zeros((tm - dst0 - n, D), F32)
            cp.start()

        if unpadded_in is not None:
            @pl.when(k == 0)
            def _():
                unpadded_copies(start)

        def out_copies(tile_idx, do):
            stage, sems = outs[-2], outs[-1]
            seq = tile_idx // len(spans)
            for q, (dst0, n, src0) in enumerate(spans):
                @pl.when(tile_idx % len(spans) == q)
                def _():
                    do(pltpu.make_async_copy(stage.at[pl.ds(src0, n)], ro_refs[0].at[seq, pl.ds(dst0, n)], sems.at[0]))
                    if q == 0:
                        do(pltpu.make_async_copy(stage.at[pl.ds(0, N_META)], ro_refs[1].at[seq], sems.at[1]))
        if tb:
            p = lax.dot_general(a_ref[...], b_ref[...], (((1,), (1,)), ((), ())), preferred_element_type=F32)
        else:
            p = jnp.dot(a_ref[...], b_ref[...], preferred_element_type=F32)

        @pl.when(k == 0)
        def _():
            acc_ref[...] = p

        @pl.when(k > 0)
        def _():
            acc_ref[...] += p

        @pl.when(k == nk - 1)
        def _():
            vecs = [v[...] for v in v_refs]
            tiles_in = list(r_refs)
            if unpadded_in is not None:
                unpadded_copies(lambda cp, *_: cp.wait())
                tiles_in.append(outs[n_ro + n_so + 1])
            tile_outs = list(ro_refs)
            if unpadded_out:
                tile_outs = [outs[-2]] + tile_outs[2:]

                @pl.when(i > 0)
                def _():
                    out_copies(i - 1, lambda cp: cp.wait())

            def step(c, sums):
                rows = pl.ds(pl.multiple_of(c * EPI_ROWS, 8), EPI_ROWS)
                tiles, terms = epilogue(i * tm + c * EPI_ROWS, acc_ref[rows, :], *[r[rows, :] for r in tiles_in], *vecs)
                for o, tile in zip(tile_outs, tiles):
                    o[rows, :] = tile.astype(o.dtype)
                return tuple(s + term for s, term in zip(sums, terms))

            sums = lax.fori_loop(0, tm // EPI_ROWS, step, tuple(jnp.zeros(s, F32) for s in sum_outs))
            if unpadded_out:
                out_copies(i, lambda cp: cp.start())

                @pl.when(i == m // tm - 1)
                def _():
                    out_copies(i, lambda cp: cp.wait())

            @pl.when(i == 0)
            def _():
                for o in so_refs:
                    o[...] = jnp.zeros_like(o)

            for o, s in zip(so_refs, sums):
                o[...] += s

    row = pl.BlockSpec((tm, D), lambda i, k: (i, 0))
    b_spec = pl.BlockSpec((D, tk), lambda i, k: (0, k)) if tb else pl.BlockSpec((tk, D), lambda i, k: (k, 0))
    row_shapes = [jax.ShapeDtypeStruct((m, D), dt) for dt in row_outs]
    row_specs = [row] * len(row_outs)
    if unpadded_out:
        nseq = m // LP
        row_shapes[:1] = [jax.ShapeDtypeStruct((nseq, SEQ, D), F32), jax.ShapeDtypeStruct((nseq, N_META, D), F32)]
        row_specs[:1] = [ANY, ANY]
        n_ro += 1
    return pl.pallas_call(
        body, name=name,
        out_shape=tuple(row_shapes + [jax.ShapeDtypeStruct(s, F32) for s in sum_outs]),
        grid=(m // tm, nk),
        in_specs=[pl.BlockSpec((tm, tk), lambda i, k: (i, k)), b_spec] + [row] * n_r
        + [pl.BlockSpec((1, D), lambda i, k: (0, 0))] * n_v + [ANY] * len(extra),
        out_specs=tuple(row_specs + [pl.BlockSpec(s, lambda i, k: (0, 0)) for s in sum_outs]),
        scratch_shapes=[pltpu.VMEM((tm, D), F32)]
        + ([] if unpadded_in is None else [pltpu.VMEM((tm, D), F32), pltpu.SemaphoreType.DMA])
        + ([pltpu.VMEM((tm, D), F32), pltpu.SemaphoreType.DMA((2,))] if unpadded_out else []),
        compiler_params=_cparams("arbitrary", "arbitrary"))(a, b, *rows_in, *vecs_in, *extra)


def _residual_norm(row0, acc, h, g):
    hv = h + acc
    return (hv, (hv * _rms(hv)) * g), ()


def _rms_bwd_math(hv, dn, gv):
    r = _rms(hv)
    hr = hv * r
    dng = dn * gv
    dh = r * (dng - hr * jnp.mean(dng * hr, axis=-1, keepdims=True))
    return dh, dn * hr


def _loss_head(row0, acc, h1, tgt, g):
    hv = h1 + acc
    hr = hv * _rms(hv)
    pos = row0 % LP + lax.broadcasted_iota(jnp.int32, (EPI_ROWS, 1), 0)
    valid = (pos >= N_META) & (pos < L_REAL)
    err = jnp.where(valid, hr * g - tgt, 0.0)
    part = 0.5 * jnp.sum(jnp.mean(err * err, axis=-1, keepdims=True))
    dy = err * (1.0 / D)
    dh, dgrow = _rms_bwd_math(hv, dy, g)
    return (dh, dh), (jnp.full((8, LANES), part, F32), jnp.sum(dgrow, axis=0, keepdims=True))


def _residual_norm_bwd(row0, acc, h, dres, g):
    dh, dgrow = _rms_bwd_math(h, acc, g)
    dh = dh + dres
    return (dh, dh), (jnp.sum(dgrow, axis=0, keepdims=True),)


def _residual_norm_bwd_f32(row0, acc, h, dres, g):
    tiles, sums = _residual_norm_bwd(row0, acc, h, dres, g)
    return tiles[:1], sums


GATE_BLK = GATE_COL // D


def _sigmoid(x):
    return 1.0 / (1.0 + jnp.exp(-x))


def _merge_fwd(p_sb, p_fx, proj, name):
    t = p_sb.shape[0]
    row = pl.BlockSpec((TR, D), lambda i: (i, 0))

    def body(ps_ref, pf_ref, gs_ref, gf_ref, o_ref):
        o_ref[...] = (_sigmoid(gs_ref[...]) * ps_ref[...] + _sigmoid(gf_ref[...]) * pf_ref[...]).astype(BF16)

    return pl.pallas_call(
        body, name=name, out_shape=jax.ShapeDtypeStruct((t, D), BF16), grid=(t // TR,),
        in_specs=[row, row, pl.BlockSpec((TR, D), lambda i: (i, GATE_BLK)),
                  pl.BlockSpec((TR, D), lambda i: (i, GATE_BLK + 1))],
        out_specs=row, compiler_params=_cparams("parallel"))(p_sb, p_fx, proj, proj)


def _merge_bwd(dm, p, proj, dproj, which, name):
    t = dm.shape[0]
    row = pl.BlockSpec((TR, D), lambda i: (i, 0))
    gate = pl.BlockSpec((TR, D), lambda i: (i, GATE_BLK + which))

    def body(dm_ref, p_ref, g_ref, *rest):
        dp_ref, dg_ref = rest[-2:]
        dmv = dm_ref[...]
        s = _sigmoid(g_ref[...])
        dp_ref[...] = (dmv * s).astype(BF16)
        dg_ref[...] = (dmv * p_ref[...] * s * (1.0 - s)).astype(BF16)

    out_shape = (jax.ShapeDtypeStruct((t, D), BF16), jax.ShapeDtypeStruct((t, IN_P), BF16))
    if dproj is None:
        return pl.pallas_call(
            body, name=name, out_shape=out_shape, grid=(t // TR,), in_specs=[row, row, gate],
            out_specs=(row, gate), compiler_params=_cparams("parallel"))(dm, p, proj)
    return pl.pallas_call(
        body, name=name, out_shape=out_shape, grid=(t // TR,), in_specs=[row, row, gate, ANY],
        out_specs=(row, gate), input_output_aliases={3: 1}, compiler_params=_cparams("parallel"))(dm, p, proj, dproj)


CH = 288


def _chunk(c, n=CH):
    return pl.ds(pl.multiple_of(c * CH, 8), n)


def _conv_taps(u_ref, c):
    x = u_ref[_chunk(c), :]
    prev = u_ref[pl.ds(pl.multiple_of(jnp.maximum(c * CH - 8, 0), 8), 8), :]
    xx = jnp.concatenate([jnp.where(c == 0, 0.0, prev), x], axis=0)
    return x, pltpu.roll(xx, 1, 0)[8:], pltpu.roll(xx, 2, 0)[8:]


def _conv_glu_fwd(u, cw, nseq, name):
    nblk = D_FF // FFC

    def body(u_ref, cw_ref, o_ref):
        cwv = cw_ref[...]

        def step(c, _):
            x, x1, x2 = _conv_taps(u_ref, c)
            uc = cwv[0:1, :] * x2 + cwv[1:2, :] * x1 + cwv[2:3, :] * x
            a, b = uc[:, :FFC], uc[:, FFC:]
            o_ref[_chunk(c), :] = (a * _sigmoid(a) * b).astype(BF16)
            return 0

        lax.fori_loop(0, LP // CH, step, 0)

    return pl.pallas_call(
        body, name=name, out_shape=jax.ShapeDtypeStruct((nseq * LP, D_FF), BF16), grid=(nseq, nblk),
        in_specs=[pl.BlockSpec((LP, 2 * FFC), lambda s, j: (s, j)), pl.BlockSpec((3, 2 * FFC), lambda s, j: (0, j))],
        out_specs=pl.BlockSpec((LP, FFC), lambda s, j: (s, j)),
        compiler_params=_cparams("parallel", "parallel"))(u, cw)


def _conv_glu_bwd(u, cw, dact, nseq, name):
    nblk = D_FF // FFC
    nch = LP // CH

    def body(u_ref, cw_ref, da_ref, du_ref, dcw_ref):
        s = pl.program_id(1)
        cwv = cw_ref[...]

        def step(k, carry):
            nxt, p0, p1, p2 = carry
            c = nch - 1 - k
            x, x1, x2 = _conv_taps(u_ref, c)
            uc = cwv[0:1, :] * x2 + cwv[1:2, :] * x1 + cwv[2:3, :] * x
            a, b = uc[:, :FFC], uc[:, FFC:]
            sa = _sigmoid(a)
            dactv = da_ref[_chunk(c), :]
            da = dactv * b * (sa * (1.0 + a * (1.0 - sa)))
            db = dactv * (a * sa)
            duc = jnp.concatenate([da, db], axis=1)
            dd = jnp.concatenate([duc, nxt], axis=0)
            du = (cwv[2:3, :] * duc + cwv[1:2, :] * pltpu.roll(dd, CH + 7, 0)[:CH]
                  + cwv[0:1, :] * pltpu.roll(dd, CH + 6, 0)[:CH])
            du_ref[_chunk(c), :] = du.astype(BF16)
            return (duc[:8], p0 + jnp.sum(duc * x2, axis=0, keepdims=True),
                    p1 + jnp.sum(duc * x1, axis=0, keepdims=True), p2 + jnp.sum(duc * x, axis=0, keepdims=True))

        zrow = jnp.zeros((1, 2 * FFC), F32)
        _, p0, p1, p2 = lax.fori_loop(0, nch, step, (jnp.zeros((8, 2 * FFC), F32), zrow, zrow, zrow))

        @pl.when(s == 0)
        def _():
            dcw_ref[...] = jnp.zeros_like(dcw_ref)

        dcw_ref[...] += jnp.concatenate([p0, p1, p2], axis=0)

    return pl.pallas_call(
        body, name=name,
        out_shape=(jax.ShapeDtypeStruct((nseq * LP, 2 * D_FF), BF16), jax.ShapeDtypeStruct((3, 2 * D_FF), F32)),
        grid=(nblk, nseq),
        in_specs=[pl.BlockSpec((LP, 2 * FFC), lambda j, s: (s, j)), pl.BlockSpec((3, 2 * FFC), lambda j, s: (0, j)),
                  pl.BlockSpec((LP, FFC), lambda j, s: (s, j))],
        out_specs=(pl.BlockSpec((LP, 2 * FFC), lambda j, s: (s, j)), pl.BlockSpec((3, 2 * FFC), lambda j, s: (0, j))),
        compiler_params=_cparams("parallel", "arbitrary"))(u, cw, dact)


F_BLK = F_COL // LANES
CB = 128


def _split3(x):
    hi = x.astype(BF16)
    r1 = x - hi.astype(F32)
    mid = r1.astype(BF16)
    lo = (r1 - mid.astype(F32)).astype(BF16)
    return hi, mid, lo


def _tri_dot(tri, x):
    hi, mid, lo = _split3(x)
    d = functools.partial(jnp.dot, preferred_element_type=F32)
    return d(tri, hi) + d(tri, mid) + d(tri, lo)


def _log_sigmoid(x):
    return jnp.minimum(x, 0.0) - jnp.log(1.0 + jnp.exp(-jnp.abs(x)))


def _gate_fwd(proj, bf, nseq, name):
    def body(f_ref, b_ref, c_ref):
        r_i = lax.broadcasted_iota(jnp.int32, (CB, CB), 0)
        c_i = lax.broadcasted_iota(jnp.int32, (CB, CB), 1)
        tri = (c_i <= r_i).astype(BF16)
        bv = b_ref[...]

        def step(k, carry):
            rows = pl.ds(pl.multiple_of(k * CB, CB), CB)
            lf = _log_sigmoid(f_ref[rows, :] + bv)
            c_ref[rows, :] = _tri_dot(tri, lf) + carry
            return carry + jnp.sum(lf, axis=0, keepdims=True)

        lax.fori_loop(0, LP // CB, step, jnp.zeros((1, LANES), F32))

    return pl.pallas_call(
        body, name=name, out_shape=jax.ShapeDtypeStruct((nseq * LP, LANES), F32), grid=(nseq,),
        in_specs=[pl.BlockSpec((LP, LANES), lambda s: (s, F_BLK)), pl.BlockSpec((1, LANES), lambda s: (0, 0))],
        out_specs=pl.BlockSpec((LP, LANES), lambda s: (s, 0)),
        compiler_params=_cparams("parallel"))(proj, bf)


def _gate_bwd(proj, bf, dc, dproj, nseq, name):
    def body(f_ref, b_ref, dc_ref, _, df_ref, db_ref):
        s = pl.program_id(0)
        r_i = lax.broadcasted_iota(jnp.int32, (CB, CB), 0)
        c_i = lax.broadcasted_iota(jnp.int32, (CB, CB), 1)
        tri = (c_i >= r_i).astype(BF16)
        bv = b_ref[...]

        def step(kk, carry):
            carry_c, carry_b = carry
            k = LP // CB - 1 - kk
            rows = pl.ds(pl.multiple_of(k * CB, CB), CB)
            dcv = dc_ref[rows, :]
            dlf = _tri_dot(tri, dcv) + carry_c
            df = dlf * _sigmoid(-(f_ref[rows, :] + bv))
            df_ref[rows, :] = jnp.concatenate([df, jnp.zeros_like(df)], axis=1).astype(BF16)
            return carry_c + jnp.sum(dcv, axis=0, keepdims=True), carry_b + jnp.sum(df, axis=0, keepdims=True)

        zero = jnp.zeros((1, LANES), F32)
        _, dbp = lax.fori_loop(0, LP // CB, step, (zero, zero))

        @pl.when(s == 0)
        def _():
            db_ref[...] = jnp.zeros_like(db_ref)

        db_ref[...] += dbp

    return pl.pallas_call(
        body, name=name,
        out_shape=(jax.ShapeDtypeStruct(dproj.shape, BF16), jax.ShapeDtypeStruct((1, LANES), F32)), grid=(nseq,),
        in_specs=[pl.BlockSpec((LP, LANES), lambda s: (s, F_BLK)), pl.BlockSpec((1, LANES), lambda s: (0, 0)),
                  pl.BlockSpec((LP, LANES), lambda s: (s, 0)), ANY],
        out_specs=(pl.BlockSpec((LP, 2 * LANES), lambda s: (s, F_COL // (2 * LANES))), pl.BlockSpec((1, LANES), lambda s: (0, 0))),
        input_output_aliases={3: 0},
        compiler_params=_cparams("arbitrary"))(proj, bf, dc, dproj)


SCALE = 0.125
NEG = -1e30


def _dot_nt(a, b):
    return lax.dot_general(a, b, (((1,), (1,)), ((), ())), preferred_element_type=F32)


def _dot(a, b):
    return jnp.dot(a, b, preferred_element_type=F32)


def _blk(i, n=BQ):
    return pl.ds(pl.multiple_of(i * BQ, BQ), n)


def _query_blocks(qblock, n_last):
    def full(i, _):
        qblock(i, BQ)
        return 0

    lax.fori_loop(0, NBLK - 1, full, 0)
    qblock(jnp.minimum(pl.program_id(0) + NBLK, NBLK - 1), n_last)


def _tile_iotas():
    return lax.broadcasted_iota(jnp.int32, (BQ, BQ), 0), lax.broadcasted_iota(jnp.int32, (BQ, BQ), 1)


def _lane_iota():
    return lax.broadcasted_iota(jnp.int32, (BQ, LANES), 1)


def _head_masks():
    lane = _lane_iota()
    return lane < HEAD, lane >= HEAD


def _only(mask, x):
    return jnp.where(mask, x, jnp.zeros_like(x))


def _chains(npair):
    return [(pp, h) for pp in range(npair) for h in range(2)]


def _load_qkv(p_ref, q_s, k_s, v_s):
    for pp in range(q_s.shape[0]):
        base = pp * PAIR_W
        q_s[pp] = (p_ref[:, base:base + LANES] * SCALE).astype(BF16)
        k_s[pp] = p_ref[:, base + LANES:base + 2 * LANES].astype(BF16)
        v_s[pp] = p_ref[:, base + 2 * LANES:base + 3 * LANES].astype(BF16)


def _softplus(z):
    return jnp.maximum(z, 0.0) + jnp.log(1.0 + jnp.exp(-jnp.abs(z)))


def _sb_tile_weights(q, k, strict, r, u_suf):
    n = len(q)
    z = [_dot_nt(q[c], k[c]) for c in range(n)]
    sp = [_softplus(zc) for zc in z]
    lk = [-spc if strict is None else jnp.where(strict, -spc, 0.0) for spc in sp]
    suf = [_dot(lkc.astype(BF16), u_suf) for lkc in lk]
    w = [jnp.exp(z[c] - sp[c] + r[c] + suf[c]) for c in range(n)]
    if strict is not None:
        w = [jnp.where(strict, wc, 0.0) for wc in w]
    r_next = [r[c] + suf[c][:, 0:1] + lk[c][:, 0:1] for c in range(n)]
    return w, sp, r_next


def _group_spec(kind, npair):
    return pl.BlockSpec((LP, npair * PAIR_W), lambda s, g: (s, (NH // (2 * npair)) * kind + g))


def _gheads_spec(npair):
    return pl.BlockSpec((LP, npair * LANES), lambda s, g: (s, g))


def _qkv_scratch(npair):
    return [pltpu.VMEM((npair, LP, LANES), BF16)] * 3


SEQ_SPEC = pl.BlockSpec((LP, LANES), lambda s, g: (s, 0))
RS_STRIDE = 16
Q_LAST = 128
Q_LAST_ROWS = 16


def _pair_cols(pp):
    return slice(pp * LANES, (pp + 1) * LANES)


def _carry_spec(npair):
    return pl.BlockSpec((None, npair * LANES, LP), lambda s, g: (s, g, 0))


def _sb_fwd(proj, nseq, npair, name):
    t = nseq * LP
    chains = _chains(npair)

    def body(p_ref, o_ref, rs_ref, q_s, k_s, v_s, acc_ref, r_ref, rb_ref):
        _load_qkv(p_ref, q_s, k_s, v_s)
        row, col = _tile_iotas()
        u_suf = (row > col).astype(BF16)
        heads = _head_masks()

        def qblock(i, nq):
            diag = (col < row)[:nq]
            lane = _lane_iota()[:nq]
            heads_q = [hm[:nq] for hm in heads]
            acc_ref[...] = jnp.zeros_like(acc_ref)
            rb_ref[...] = jnp.zeros_like(rb_ref)
            r_ref[...] = jnp.zeros_like(r_ref)
            qb = [q_s[pp, _blk(i, nq), :] for pp in range(npair)]

            def tile(j, strict):
                kj = [k_s[pp, _blk(j), :] for pp in range(npair)]
                vj = [v_s[pp, _blk(j), :] for pp in range(npair)]
                r = [r_ref[c, :nq] for c in range(len(chains))]
                w, _, r_next = _sb_tile_weights([_only(heads_q[h], qb[pp]) for pp, h in chains],
                                                [kj[pp] for pp, _ in chains], strict, r, u_suf)
                pv = [_dot(w[c].astype(BF16), _only(heads[h], vj[pp])) for c, (pp, h) in enumerate(chains)]
                for pp in range(npair):
                    acc_ref[pp, :nq] += pv[2 * pp] + pv[2 * pp + 1]
                    rb_ref[pp, :nq] = jnp.where(lane == j, r[2 * pp], jnp.where(lane == RS_STRIDE + j, r[2 * pp + 1], rb_ref[pp, :nq]))
                for c in range(len(chains)):
                    r_ref[c, :nq] = r_next[c]

            tile(i, diag)

            def kblock(jj, _):
                tile(i - jj, None)
                return 0

            lax.fori_loop(1, i + 1, kblock, 0)
            for pp in range(npair):
                o_ref[_blk(i), _pair_cols(pp)] = acc_ref[pp].astype(BF16)
                rs_ref[_pair_cols(pp), _blk(i)] = rb_ref[pp].T

        _query_blocks(qblock, Q_LAST_ROWS)

    return pl.pallas_call(
        body, name=name,
        out_shape=(jax.ShapeDtypeStruct((t, W_ATT), BF16), jax.ShapeDtypeStruct((nseq, W_ATT, LP), F32)),
        grid=(nseq, NH // (2 * npair)), in_specs=[_group_spec(0, npair)], out_specs=(_gheads_spec(npair), _carry_spec(npair)),
        scratch_shapes=_qkv_scratch(npair) + [pltpu.VMEM((npair, BQ, LANES), F32), pltpu.VMEM((2 * npair, BQ, 1), F32),
                                      pltpu.VMEM((npair, BQ, LANES), F32)],
        compiler_params=_cparams("parallel", "parallel"))(proj)


def _sb_bwd(proj, do, rs, dproj, after, nseq, npair, name):
    chains = _chains(npair)

    def body(p_ref, do_ref, rs_ref, _, _after, dp_ref, q_s, k_s, v_s, kt_s, dqa_ref, dka_ref, dva_ref, ep_ref):
        _load_qkv(p_ref, q_s, k_s, v_s)
        row, col = _tile_iotas()
        u_after = (col > row).astype(BF16)
        u_before = (col < row).astype(BF16)
        heads = _head_masks()
        sub_all = lax.broadcasted_iota(jnp.int32, (LANES, BQ), 0)
        rows_k = (sub_all < HEAD, sub_all >= HEAD)
        nc = len(chains)
        for pp in range(npair):
            kt_s[pp] = k_s[pp].astype(F32).T.astype(BF16)
        dka_ref[...] = jnp.zeros_like(dka_ref)
        dva_ref[...] = jnp.zeros_like(dva_ref)

        def qblock(i, nq):
            diag = (row < col)[:, :nq]
            rows_of = [m[:, :nq] for m in rows_k]
            heads_q = [m[:nq] for m in heads]
            queries = _blk(i, nq)
            qb = [q_s[pp, queries, :] for pp in range(npair)]
            dob = [do_ref[queries, _pair_cols(pp)] for pp in range(npair)]
            qt = [qb[pp].astype(F32).T.astype(BF16) for pp in range(npair)]
            dot = [dob[pp].astype(F32).T.astype(BF16) for pp in range(npair)]
            qt_m = [jnp.where(rows_of[h], qt[pp], jnp.zeros_like(qt[pp])) for pp, h in chains]
            dot_m = [jnp.where(rows_of[h], dot[pp], jnp.zeros_like(dot[pp])) for pp, h in chains]
            q_m = [_only(heads_q[h], qb[pp]) for pp, h in chains]
            do_m = [_only(heads_q[h], dob[pp]) for pp, h in chains]
            dqa_ref[...] = jnp.zeros_like(dqa_ref)
            ep_ref[...] = jnp.zeros_like(ep_ref)

            def tile(j, strict):
                keys = pl.ds(pl.multiple_of(j * BQ, BQ), BQ)
                kj = [k_s[pp, _blk(j), :] for pp in range(npair)]
                vj = [v_s[pp, _blk(j), :] for pp in range(npair)]
                r = [_key_cols(rs_ref, pp * LANES + RS_STRIDE * h + j, i, nq) for pp, h in chains]
                z = [_dot(kj[pp], qt_m[cidx]) for cidx, (pp, _) in enumerate(chains)]
                dw = [_dot(vj[pp], dot_m[cidx]) for cidx, (pp, _) in enumerate(chains)]
                sp = [_softplus(zc) for zc in z]
                lk = [-spc if strict is None else jnp.where(strict, -spc, 0.0) for spc in sp]
                suf = [_dot(u_after, lkc.astype(BF16)) for lkc in lk]
                w = [jnp.exp(z[cidx] - sp[cidx] + r[cidx] + suf[cidx]) for cidx in range(nc)]
                if strict is not None:
                    w = [jnp.where(strict, wc, 0.0) for wc in w]
                e = [dw[cidx] * w[cidx] for cidx in range(nc)]
                e_pre = [ep_ref[cidx, :, :nq] + _dot(u_before, e[cidx].astype(BF16)) for cidx in range(nc)]
                dz = []
                for cidx in range(nc):
                    ep_ref[cidx, :, :nq] += jnp.sum(e[cidx], axis=0, keepdims=True)
                    sneg = jnp.exp(-sp[cidx])
                    dzc = e[cidx] * sneg - (1.0 - sneg) * e_pre[cidx]
                    if strict is not None:
                        dzc = jnp.where(strict, dzc, 0.0)
                    dz.append(dzc.astype(BF16))
                dq = [_dot(jnp.where(rows_k[h], kt_s[pp, :, keys], jnp.zeros((LANES, BQ), BF16)), dz[cidx])
                      for cidx, (pp, h) in enumerate(chains)]
                dk = [_dot(dz[cidx], q_m[cidx]) for cidx in range(nc)]
                dv = [_dot(w[cidx].astype(BF16), do_m[cidx]) for cidx in range(nc)]
                for pp in range(npair):
                    dqa_ref[pp, :, :nq] += dq[2 * pp] + dq[2 * pp + 1]
                    dka_ref[pp, _blk(j), :] += dk[2 * pp] + dk[2 * pp + 1]
                    dva_ref[pp, _blk(j), :] += dv[2 * pp] + dv[2 * pp + 1]

            def kblock(j, _):
                tile(j, None)
                return 0

            lax.fori_loop(0, i, kblock, 0)
            tile(i, diag)
            for pp in range(npair):
                dp_ref[_blk(i), pp * PAIR_W:pp * PAIR_W + LANES] = (dqa_ref[pp].T * SCALE).astype(BF16)

        _query_blocks(qblock, Q_LAST)
        for pp in range(npair):
            dp_ref[:, pp * PAIR_W + LANES:pp * PAIR_W + 2 * LANES] = dka_ref[pp].astype(BF16)
            dp_ref[:, pp * PAIR_W + 2 * LANES:pp * PAIR_W + 3 * LANES] = dva_ref[pp].astype(BF16)

    return pl.pallas_call(
        body, name=name, out_shape=jax.ShapeDtypeStruct(dproj.shape, BF16), grid=(nseq, NH // (2 * npair)),
        in_specs=[_group_spec(0, npair), _gheads_spec(npair), _carry_spec(npair), ANY, ANY], out_specs=_group_spec(0, npair),
        input_output_aliases={3: 0},
        scratch_shapes=_qkv_scratch(npair) + [pltpu.VMEM((npair, LANES, LP), BF16), pltpu.VMEM((npair, LANES, BQ), F32),
                                      pltpu.VMEM((npair, LP, LANES), F32), pltpu.VMEM((npair, LP, LANES), F32),
                                      pltpu.VMEM((2 * npair, 1, BQ), F32)],
        compiler_params=_cparams("parallel", "parallel"))(proj, do, rs, dproj, after)


CROW_SPEC = pl.BlockSpec((None, NH, LP), lambda s, g: (s, 0, 0))


def _key_cols(cr_ref, head, j, n=BQ):
    return cr_ref[pl.ds(head, 1), _blk(j)][:, :n]


def _fox_fwd(proj, c, crow, nseq, npair, name):
    t = nseq * LP
    chains = _chains(npair)

    def body(p_ref, c_ref, cr_ref, o_ref, o32_ref, lse_ref, q_s, k_s, v_s, vt_s, ck_s, acc_ref, m_ref, l_ref):
        _load_qkv(p_ref, q_s, k_s, v_s)
        row, col = _tile_iotas()
        sub_all = lax.broadcasted_iota(jnp.int32, (LANES, BQ), 0)
        rows_k = (sub_all < HEAD, sub_all >= HEAD)
        head0 = 2 * npair * pl.program_id(1)
        nc = len(chains)
        lane_all = lax.broadcasted_iota(jnp.int32, (LP, LANES), 1)
        for pp in range(npair):
            vt_s[pp] = v_s[pp].astype(F32).T.astype(BF16)
        for cidx in range(nc):
            ck_s[cidx] = jnp.sum(jnp.where(lane_all == head0 + cidx, c_ref[...], 0.0), axis=1, keepdims=True)

        def qblock(i, nq):
            diag = (row <= col)[:, :nq]
            sub = sub_all[:, :nq]
            rows_of = (sub < HEAD, sub >= HEAD)
            qt = [q_s[pp, _blk(i, nq), :].astype(F32).T.astype(BF16) for pp in range(npair)]
            qt = [jnp.where(rows_of[h], qt[pp], jnp.zeros_like(qt[pp])) for pp, h in chains]
            cq = [_key_cols(cr_ref, head0 + cidx, i, nq) for cidx in range(nc)]
            acc_ref[...] = jnp.zeros_like(acc_ref)
            m_ref[...] = jnp.full_like(m_ref, NEG)
            l_ref[...] = jnp.zeros_like(l_ref)

            def tile(j, causal):
                keys = pl.ds(pl.multiple_of(j * BQ, BQ), BQ)
                z = [_dot(k_s[pp, _blk(j), :], qt[cidx]) + (cq[cidx] - ck_s[cidx, _blk(j), :])
                     for cidx, (pp, _) in enumerate(chains)]
                if causal is not None:
                    z = [jnp.where(causal, zc, NEG) for zc in z]
                p, alpha = [], []
                for cidx in range(nc):
                    m_old = m_ref[cidx, :, :nq]
                    m_new = jnp.maximum(m_old, jnp.max(z[cidx], axis=0, keepdims=True))
                    alpha.append(jnp.exp(m_old - m_new))
                    pc = jnp.exp(z[cidx] - m_new)
                    l_ref[cidx, :, :nq] = alpha[cidx] * l_ref[cidx, :, :nq] + jnp.sum(pc, axis=0, keepdims=True)
                    m_ref[cidx, :, :nq] = m_new
                    p.append(pc.astype(BF16))
                pv = [_dot(jnp.where(rows_k[h], vt_s[pp, :, keys], jnp.zeros((LANES, BQ), BF16)), p[cidx])
                      for cidx, (pp, h) in enumerate(chains)]
                for cidx in range(nc):
                    acc_ref[cidx, :, :nq] = alpha[cidx] * acc_ref[cidx, :, :nq] + pv[cidx]

            def kblock(j, _):
                tile(j, None)
                return 0

            lax.fori_loop(0, i, kblock, 0)
            tile(i, diag)
            for pp in range(npair):
                acc = [acc_ref[2 * pp + h, :, :nq] for h in range(2)]
                l = [l_ref[2 * pp + h, :, :nq] for h in range(2)]
                out = (acc[0] / l[0] + acc[1] / l[1]).T
                o_ref[_blk(i, nq), _pair_cols(pp)] = out.astype(BF16)
                o32_ref[_blk(i, nq), _pair_cols(pp)] = out
                lse = [m_ref[2 * pp + h, :, :nq] + jnp.log(l[h]) for h in range(2)]
                lse_t = jnp.where(sub == 0, lse[0], jnp.where(sub == 1, lse[1], 0.0))
                lse_ref[_blk(i, nq), _pair_cols(pp)] = lse_t.T
                if nq < BQ:
                    rest = pl.ds(pl.multiple_of(i * BQ + nq, nq), BQ - nq)
                    o_ref[rest, _pair_cols(pp)] = jnp.zeros((BQ - nq, LANES), BF16)
                    o32_ref[rest, _pair_cols(pp)] = jnp.zeros((BQ - nq, LANES), F32)
                    lse_ref[rest, _pair_cols(pp)] = jnp.zeros((BQ - nq, LANES), F32)

        _query_blocks(qblock, Q_LAST)

    return pl.pallas_call(
        body, name=name,
        out_shape=(jax.ShapeDtypeStruct((t, W_ATT), BF16), jax.ShapeDtypeStruct((t, W_ATT), F32),
                   jax.ShapeDtypeStruct((t, W_ATT), F32)),
        grid=(nseq, NH // (2 * npair)), in_specs=[_group_spec(1, npair), SEQ_SPEC, CROW_SPEC], out_specs=(_gheads_spec(npair), _gheads_spec(npair), _gheads_spec(npair)),
        scratch_shapes=_qkv_scratch(npair) + [pltpu.VMEM((npair, LANES, LP), BF16), pltpu.VMEM((2 * npair, LP, 1), F32),
                                      pltpu.VMEM((2 * npair, LANES, BQ), F32), pltpu.VMEM((2 * npair, 1, BQ), F32),
                                      pltpu.VMEM((2 * npair, 1, BQ), F32)],
        compiler_params=_cparams("parallel", "parallel"))(proj, c, crow)


def _fox_bwd(proj, c, crow, o32, lse, do, dproj, nseq, npair, name):
    t = nseq * LP
    chains = _chains(npair)

    def body(p_ref, c_ref, cr_ref, o_ref, lse_ref, do_ref, _, dp_ref, dc_ref,
             q_s, k_s, v_s, kt_s, ck_s, dqa_ref, dka_ref, dva_ref, rsum_ref):
        _load_qkv(p_ref, q_s, k_s, v_s)
        row, col = _tile_iotas()
        lane = _lane_iota()
        heads = _head_masks()
        sub_all = lax.broadcasted_iota(jnp.int32, (LANES, BQ), 0)
        rows_k = (sub_all < HEAD, sub_all >= HEAD)
        group = pl.program_id(1)
        head0 = 2 * npair * group
        nc = len(chains)
        lane_all = lax.broadcasted_iota(jnp.int32, (LP, LANES), 1)
        for pp in range(npair):
            kt_s[pp] = k_s[pp].astype(F32).T.astype(BF16)
        for cidx in range(nc):
            ck_s[cidx] = jnp.sum(jnp.where(lane_all == head0 + cidx, c_ref[...], 0.0), axis=1, keepdims=True)
        dka_ref[...] = jnp.zeros_like(dka_ref)
        dva_ref[...] = jnp.zeros_like(dva_ref)

        @pl.when(group == 0)
        def _():
            dc_ref[...] = jnp.zeros_like(dc_ref)

        def qblock(i, nq):
            diag = (row <= col)[:, :nq]
            sub = sub_all[:, :nq]
            rows_of = [m[:, :nq] for m in rows_k]
            heads_q = [m[:nq] for m in heads]
            queries = _blk(i, nq)
            dqa_ref[...] = jnp.zeros_like(dqa_ref)
            rsum_ref[...] = jnp.zeros_like(rsum_ref)
            qb = [q_s[pp, queries, :] for pp in range(npair)]
            dob = [do_ref[queries, _pair_cols(pp)] for pp in range(npair)]
            qt = [qb[pp].astype(F32).T.astype(BF16) for pp in range(npair)]
            dot = [dob[pp].astype(F32).T for pp in range(npair)]
            prod = [dot[pp] * o_ref[queries, _pair_cols(pp)].T for pp in range(npair)]
            lse_t = [lse_ref[queries, _pair_cols(pp)].T for pp in range(npair)]
            qt_m = [jnp.where(rows_of[h], qt[pp], jnp.zeros_like(qt[pp])) for pp, h in chains]
            dot_m = [jnp.where(rows_of[h], dot[pp], 0.0).astype(BF16) for pp, h in chains]
            q_m = [_only(heads_q[h], qb[pp]) for pp, h in chains]
            do_m = [_only(heads_q[h], dob[pp]) for pp, h in chains]
            cq = [_key_cols(cr_ref, head0 + cidx, i, nq) for cidx in range(nc)]
            lse_i = [lse_t[pp][h:h + 1, :] for pp, h in chains]
            delta = [jnp.sum(jnp.where(rows_of[h], prod[pp], 0.0), axis=0, keepdims=True) for pp, h in chains]

            def tile(j, causal):
                keys = pl.ds(pl.multiple_of(j * BQ, BQ), BQ)
                kj = [k_s[pp, _blk(j), :] for pp in range(npair)]
                vj = [v_s[pp, _blk(j), :] for pp in range(npair)]
                z = [_dot(kj[pp], qt_m[cidx]) + (cq[cidx] - ck_s[cidx, _blk(j), :]) for cidx, (pp, _) in enumerate(chains)]
                if causal is not None:
                    z = [jnp.where(causal, zc, NEG) for zc in z]
                dpv = [_dot(vj[pp], dot_m[cidx]) for cidx, (pp, _) in enumerate(chains)]
                p = [jnp.exp(z[cidx] - lse_i[cidx]) for cidx in range(nc)]
                ds = [p[cidx] * (dpv[cidx] - delta[cidx]) for cidx in range(nc)]
                dsb = [d.astype(BF16) for d in ds]
                dq = [_dot(jnp.where(rows_k[h], kt_s[pp, :, keys], jnp.zeros((LANES, BQ), BF16)), dsb[cidx])
                      for cidx, (pp, h) in enumerate(chains)]
                dk = [_dot(dsb[cidx], q_m[cidx]) for cidx in range(nc)]
                dv = [_dot(p[cidx].astype(BF16), do_m[cidx]) for cidx in range(nc)]
                for pp in range(npair):
                    dqa_ref[pp, :, :nq] += dq[2 * pp] + dq[2 * pp + 1]
                    dka_ref[pp, _blk(j), :] += dk[2 * pp] + dk[2 * pp + 1]
                    dva_ref[pp, _blk(j), :] += dv[2 * pp] + dv[2 * pp + 1]
                col_sums = jnp.zeros((BQ, LANES), F32)
                for cidx in range(nc):
                    col_sums = col_sums + jnp.where(lane == head0 + cidx, jnp.sum(ds[cidx], axis=1, keepdims=True), 0.0)
                    rsum_ref[cidx, :, :nq] += jnp.sum(ds[cidx], axis=0, keepdims=True)
                dc_ref[_blk(j), :] = dc_ref[_blk(j), :] - col_sums

            def kblock(j, _):
                tile(j, None)
                return 0

            lax.fori_loop(0, i, kblock, 0)
            tile(i, diag)
            row_sums = jnp.zeros((LANES, nq), F32)
            for cidx in range(nc):
                row_sums = row_sums + jnp.where(sub == head0 + cidx, rsum_ref[cidx, :, :nq], 0.0)
            dc_ref[queries, :] += row_sums.T
            for pp in range(npair):
                dp_ref[_blk(i), pp * PAIR_W:pp * PAIR_W + LANES] = (dqa_ref[pp].T * SCALE).astype(BF16)

        _query_blocks(qblock, Q_LAST)
        for pp in range(npair):
            dp_ref[:, pp * PAIR_W + LANES:pp * PAIR_W + 2 * LANES] = dka_ref[pp].astype(BF16)
            dp_ref[:, pp * PAIR_W + 2 * LANES:pp * PAIR_W + 3 * LANES] = dva_ref[pp].astype(BF16)

    return pl.pallas_call(
        body, name=name,
        out_shape=(jax.ShapeDtypeStruct(dproj.shape, BF16), jax.ShapeDtypeStruct((t, LANES), F32)),
        grid=(nseq, NH // (2 * npair)),
        in_specs=[_group_spec(1, npair), SEQ_SPEC, CROW_SPEC, _gheads_spec(npair), _gheads_spec(npair), _gheads_spec(npair), ANY],
        out_specs=(_group_spec(1, npair), SEQ_SPEC),
        input_output_aliases={6: 0},
        scratch_shapes=_qkv_scratch(npair) + [pltpu.VMEM((npair, LANES, LP), BF16), pltpu.VMEM((2 * npair, LP, 1), F32),
                                      pltpu.VMEM((npair, LANES, BQ), F32), pltpu.VMEM((npair, LP, LANES), F32),
                                      pltpu.VMEM((npair, LP, LANES), F32), pltpu.VMEM((2 * npair, 1, BQ), F32)],
        compiler_params=_cparams("parallel", "arbitrary"))(proj, c, crow, o32, lse, do, dproj)


def _adamw_math(w, g, m, v):
    m = B1 * m + (1.0 - B1) * g
    v = B2 * v + (1.0 - B2) * (g * g)
    m_hat = m / (1.0 - B1 ** STEP)
    v_hat = v / (1.0 - B2 ** STEP)
    delta = -LR * (m_hat / (jnp.sqrt(v_hat) + EPS) + WD * w)
    return delta, m, v


def _sum_adamw(parts, w, m, v, tr, name):
    rows, cols = w.shape
    cp = parts.shape[2]
    assert rows % tr == 0 and parts.shape[1] == rows

    def body(p_ref, w_ref, m_ref, v_ref, g_ref, d_ref, nm_ref, nv_ref):
        gsum = p_ref[0].astype(F32)
        for s in range(1, N_DEV):
            gsum = gsum + p_ref[s].astype(F32)
        gsum = gsum[:, :cols]
        d, nm, nv = _adamw_math(w_ref[...], gsum, m_ref[...], v_ref[...])
        g_ref[...] = gsum
        d_ref[...] = d
        nm_ref[...] = nm
        nv_ref[...] = nv

    blk = pl.BlockSpec((tr, cols), lambda i: (i, 0))
    out = jax.ShapeDtypeStruct((rows, cols), F32)
    return pl.pallas_call(
        body, name=name, out_shape=(out, out, out, out), grid=(rows // tr,),
        in_specs=[pl.BlockSpec((N_DEV, tr, cp), lambda i: (0, i, 0)), blk, blk, blk],
        out_specs=(blk, blk, blk, blk), compiler_params=_cparams("parallel"))(parts, w, m, v)


def _local_step(x, tgt, meta, g_mix, b_forget, g_ffn, g_final, first_weights, late_weights, early_grads, last_grad):
    nseq = x.shape[0]
    t = nseq * LP
    tm = LP // 2
    mm = functools.partial(_matmul, tm=tm)

    h0 = _pad_rows(meta, x, nseq, "pad_x").reshape(t, D)
    bf = jnp.pad(b_forget.reshape(1, NH), ((0, 0), (0, LANES - NH)))

    n1 = _norm_fwd(h0, g_mix, "norm1")
    w_in_p, started = first_weights(n1)
    proj = mm(n1, w_in_p, out_dtype=F32, tn=1792, tk=D, after=started, name="in_proj")
    c = _gate_fwd(proj, bf, nseq, "gate_fwd")
    crow = c[:, :NH].reshape(nseq, LP, NH).transpose(0, 2, 1)
    o_sb, rs = _sb_fwd(proj, nseq, 2, "sb_fwd")
    o_fx, o_fx32, lse = _fox_fwd(proj, c, crow, nseq, 2, "fox_fwd")
    w_bsb, w_bfx, w_out, w_up_i, cw_i, w_down = late_weights(o_fx)
    p_sb = mm(o_sb, w_bsb, out_dtype=F32, tn=D, tk=W_ATT, name="branch_sb")
    p_fx = mm(o_fx, w_bfx, out_dtype=F32, tn=D, tk=W_ATT, name="branch_fox")
    merged = _merge_fwd(p_sb, p_fx, proj, "merge_fwd")
    rows = functools.partial(_matmul_rows, tm=LP // 4)
    h1, n2 = rows(merged, w_out, [h0], [g_ffn], _residual_norm, [F32, BF16], [], tk=D, name="out_proj_norm2")
    u = mm(n2, w_up_i, out_dtype=F32, tn=1408, tk=D, name="up_proj")
    act = _conv_glu_fwd(u, cw_i, nseq, "conv_glu_fwd")

    dh2, dh2b, loss, dg_final = rows(act, w_down, [h1], [g_final], _loss_head, [F32, BF16],
                                     [(8, LANES), (1, D)], tk=D_FF, unpadded_in=tgt, name="down_proj_loss")
    d_down = _matmul(act, dh2b, out_dtype=BF16, tm=1408, tn=D, tk=LP, ta=True, name="d_w_down")
    dact = mm(dh2b, w_down, out_dtype=F32, tn=1408, tk=D, tb=True, name="d_act")
    du, d_cw = _conv_glu_bwd(u, cw_i, dact, nseq, "conv_glu_bwd")
    d_up = _matmul(n2, du, out_dtype=BF16, tm=D, tn=1408, tk=LP, ta=True, name="d_w_up")
    dh1, dh1b, dg_ffn = rows(du, w_up_i, [h1, dh2], [g_ffn], _residual_norm_bwd, [F32, BF16], [(1, D)],
                             tk=D_FF, tb=True, name="d_n2_norm2_bwd")
    d_out = _matmul(merged, dh1b, out_dtype=BF16, tm=D, tn=D, tk=LP, ta=True, name="d_w_out")
    dmerged = mm(dh1b, w_out, out_dtype=F32, tn=D, tk=D, tb=True, name="d_merged")
    dp_sb, dproj = _merge_bwd(dmerged, p_sb, proj, None, 0, "merge_bwd_sb")
    dp_fx, dproj = _merge_bwd(dmerged, p_fx, proj, dproj, 1, "merge_bwd_fox")
    d_bsb = _matmul(o_sb, dp_sb, out_dtype=BF16, tm=W_ATT, tn=D, tk=LP, ta=True, name="d_w_branch_sb")
    d_bfx = _matmul(o_fx, dp_fx, out_dtype=BF16, tm=W_ATT, tn=D, tk=LP, ta=True, name="d_w_branch_fox")
    do_sb = mm(dp_sb, w_bsb, out_dtype=BF16, tn=W_ATT, tk=D, tb=True, name="d_o_sb")
    do_fx = mm(dp_fx, w_bfx, out_dtype=BF16, tn=W_ATT, tk=D, tb=True, name="d_o_fox")
    sent = early_grads(dict(w_branch_sb=d_bsb, w_branch_fox=d_bfx, w_out=d_out, w_up=d_up, conv_w=d_cw, w_down=d_down))
    dproj = _sb_bwd(proj, do_sb, rs, dproj, sent, nseq, 2, "sb_bwd")
    dproj, dc = _fox_bwd(proj, c, crow, o_fx32, lse, do_fx, dproj, nseq, 2, "fox_bwd")
    dproj, d_bf = _gate_bwd(proj, bf, dc, dproj, nseq, "gate_bwd")
    d_in = _matmul(n1, dproj, out_dtype=BF16, tm=D, tn=1792, tk=LP, ta=True, name="d_w_in")
    grad_x, d_front, dg_mix = rows(dproj, w_in_p, [h0, dh1], [g_mix], _residual_norm_bwd_f32, [F32], [(1, D)],
                                   tk=IN_P // 2, tb=True, after=last_grad(d_in), unpadded_out=True, name="d_n1_norm1_bwd")
    grads = dict(meta_tokens=jnp.sum(d_front, axis=0), norm_mix_g=dg_mix, b_forget=d_bf[:, :NH],
                 norm_ffn_g=dg_ffn, norm_final_g=dg_final)
    return loss[0, 0], grad_x, grads


REPL = (("norm_mix_g", D), ("norm_ffn_g", D), ("norm_final_g", D), ("b_forget", LANES))
REPL_ROWS = 32
META_ROWS = N_META * D // LANES


def _pack_repl(tree):
    rows = [jnp.pad(tree[name].reshape(-1), (0, n - tree[name].size)).reshape(-1, LANES) for name, n in REPL]
    packed = jnp.concatenate(rows, axis=0)
    return jnp.pad(packed, ((0, REPL_ROWS - packed.shape[0]), (0, 0)))


def _unpack_repl(packed, shapes):
    out, r = {}, 0
    for name, n in REPL:
        size = 1
        for s in shapes[name]:
            size *= s
        out[name] = packed[r:r + n // LANES].reshape(-1)[:size].reshape(shapes[name])
        r += n // LANES
    return out


def kernel(x, meta_tokens, norm_mix_g, w_in, b_forget, w_branch_sb, w_branch_fox, w_out, norm_ffn_g, w_up, conv_w, w_down, norm_final_g, loss_target, m_meta_tokens, m_norm_mix_g, m_w_in, m_b_forget, m_w_branch_sb, m_w_branch_fox, m_w_out, m_norm_ffn_g, m_w_up, m_conv_w, m_w_down, m_norm_final_g, v_meta_tokens, v_norm_mix_g, v_w_in, v_b_forget, v_w_branch_sb, v_w_branch_fox, v_w_out, v_norm_ffn_g, v_w_up, v_conv_w, v_w_down, v_norm_final_g):
    w = dict(meta_tokens=meta_tokens, norm_mix_g=norm_mix_g, w_in=w_in, b_forget=b_forget, w_branch_sb=w_branch_sb,
             w_branch_fox=w_branch_fox, w_out=w_out, norm_ffn_g=norm_ffn_g, w_up=w_up, conv_w=conv_w, w_down=w_down,
             norm_final_g=norm_final_g)
    m = dict(meta_tokens=m_meta_tokens, norm_mix_g=m_norm_mix_g, w_in=m_w_in, b_forget=m_b_forget,
             w_branch_sb=m_w_branch_sb, w_branch_fox=m_w_branch_fox, w_out=m_w_out, norm_ffn_g=m_norm_ffn_g,
             w_up=m_w_up, conv_w=m_conv_w, w_down=m_w_down, norm_final_g=m_norm_final_g)
    v = dict(meta_tokens=v_meta_tokens, norm_mix_g=v_norm_mix_g, w_in=v_w_in, b_forget=v_b_forget,
             w_branch_sb=v_w_branch_sb, w_branch_fox=v_w_branch_fox, w_out=v_w_out, norm_ffn_g=v_norm_ffn_g,
             w_up=v_w_up, conv_w=v_conv_w, w_down=v_w_down, norm_final_g=v_norm_final_g)
    shapes = {k: a.shape for k, a in w.items()}
    sharded = ("w_in", "w_branch_sb", "w_branch_fox", "w_out", "w_up", "w_down", "conv_w", "meta_tokens")
    mat = lambda tree, name: tree[name].reshape(tree[name].shape[-2:])

    def lane_pad(a, width):
        return jnp.pad(a, ((0, 0), (0, width - a.shape[1])))

    late = ("w_branch_sb", "w_branch_fox", "w_out", "w_up", "w_down", "conv_w")
    pending_w = {}
    g_meta, = _all_gather([mat(w, "meta_tokens")], "gather_meta")
    pending_w["in"], in_started = _remote_start(
        [lane_pad(mat(w, "w_in").astype(BF16), SHARD_P)], False, g_meta, "gather_w_in_start")
    meta_full = g_meta.transpose(1, 0, 2).reshape(N_META, D) + in_started[0, 0]

    def first_weights(after):
        g_in, = _remote_wait(pending_w["in"], after, "gather_w_in_wait")
        pending_w["late"], started = _remote_start(
            [mat(w, "w_branch_sb").astype(BF16), mat(w, "w_branch_fox").astype(BF16), mat(w, "w_out").astype(BF16),
             lane_pad(mat(w, "w_up").astype(BF16), SHARD_P), mat(w, "w_down").astype(BF16), mat(w, "conv_w")],
            False, g_in, "gather_late_start")
        w_in_p = _relayout(g_in, 1, IN_P, _gathered_to_full(IN_SHARD, _in_padded_to_orig), BF16, 256, "w_in_cols")[0]
        return w_in_p, started

    def late_weights(after):
        g_bsb, g_bfx, g_out, g_up, g_down, g_cw = _remote_wait(pending_w["late"], after, "gather_late_wait")
        w_up_i = _relayout(g_up, 1, 2 * D_FF, _gathered_to_full(UP_SHARD, _up_inter_to_orig), BF16, 256, "w_up_cols")[0]
        w_bsb = _relayout(g_bsb, 1, D, _gathered_to_full(ATT_SHARD, lambda d: d), BF16, 256, "w_bsb_cols")[0]
        w_bfx = _relayout(g_bfx, 1, D, _gathered_to_full(ATT_SHARD, lambda d: d), BF16, 256, "w_bfx_cols")[0]
        cw_full = g_cw.transpose(1, 0, 2).reshape(3, 2 * D_FF)
        cw_i = cw_full.reshape(3, 2, D_FF // FFC, FFC).transpose(0, 2, 1, 3).reshape(3, 2 * D_FF)
        return w_bsb, w_bfx, g_out.reshape(D, D), w_up_i, cw_i, g_down.reshape(D_FF, D)

    pending_g = {}

    def early_grads(g):
        d_cw = g["conv_w"].reshape(3, D_FF // FFC, 2, FFC).transpose(0, 2, 1, 3).reshape(3, 2 * D_FF)
        pending_g["early"], sent = _remote_start(
            [_relayout(g["w_branch_sb"][None], N_DEV, ATT_SHARD, _full_to_shards(ATT_SHARD, lambda c: c), BF16, 256, "d_w_bsb_shards"),
             _relayout(g["w_branch_fox"][None], N_DEV, ATT_SHARD, _full_to_shards(ATT_SHARD, lambda c: c), BF16, 256, "d_w_bfx_shards"),
             g["w_out"].reshape(N_DEV, D // N_DEV, D),
             _relayout(g["w_up"][None], N_DEV, SHARD_P, _full_to_shards(UP_SHARD, _UP_ORIG_TO_INTER.get), BF16, 256, "d_w_up_shards"),
             g["w_down"].reshape(N_DEV, D_FF // N_DEV, D),
             d_cw.reshape(3, N_DEV, UP_SHARD).transpose(1, 0, 2)], True, g["w_out"], "exchange_early_start")
        return sent

    def last_grad(d_in):
        shards = _relayout(d_in[None], N_DEV, SHARD_P, _full_to_shards(IN_SHARD, _IN_ORIG_TO_PADDED.get), BF16, 256, "d_w_in_shards")
        pending_g["last"], sent = _remote_start([shards], True, shards, "exchange_last_start")
        return sent

    loss, grad_x, grads = _local_step(
        x, loss_target, meta_full, norm_mix_g.reshape(1, D), b_forget,
        norm_ffn_g.reshape(1, D), norm_final_g.reshape(1, D), first_weights, late_weights, early_grads, last_grad)

    small = jnp.concatenate([_pack_repl(grads), grads["meta_tokens"].reshape(META_ROWS, LANES)], axis=0)
    small, = _all_gather([small], "gather_small_grads")
    me_idx = 4 * lax.axis_index("x") + 2 * lax.axis_index("y") + lax.axis_index("c")
    p_meta = lax.dynamic_slice_in_dim(small[:, REPL_ROWS:].reshape(N_DEV, N_META, D), me_idx * ATT_SHARD, ATT_SHARD, axis=2)

    p_in, = _remote_wait(pending_g["last"], small, "exchange_last_wait")
    parts = dict(zip(late, _remote_wait(pending_g["early"], p_in, "exchange_early_wait")), w_in=p_in, meta_tokens=p_meta)
    tiles = dict(w_in=256, w_branch_sb=256, w_branch_fox=256, w_out=D // N_DEV, w_up=256, w_down=D_FF // N_DEV,
                 conv_w=3, meta_tokens=N_META)
    new = {name: _sum_adamw(parts[name], mat(w, name), mat(m, name), mat(v, name), tiles[name], "adamw_" + name)
           for name in sharded}

    routs = _sum_adamw(small[:, :REPL_ROWS], _pack_repl(w), _pack_repl(m), _pack_repl(v), REPL_ROWS, "adamw_replicated")
    repl = [_unpack_repl(o, shapes) for o in routs]

    result = [lax.psum(loss, ("x", "y", "c")), grad_x]
    for k in range(4):
        for name in w:
            result.append(new[name][k].reshape(shapes[name]) if name in new else repl[k][name])
    return tuple(result)
```

```python
import functools

import jax
import jax.numpy as jnp
from jax import lax
from jax.experimental import pallas as pl
from jax.experimental.pallas import tpu as pltpu

F32 = jnp.float32
BF16 = jnp.bfloat16

N_DEV = 8
LANES = 128
D = 1024
N_META = 16
SEQ = 2048
L_REAL = N_META + SEQ
LP = 2304
BQ = 256
NBLK = LP // BQ
HEAD = 64
NH = 8
W_ATT = NH * HEAD
PAIR_W = 3 * LANES
D_FF = 2816
IN_COLS = 5128
QKV = 6 * W_ATT
IN_P = 5376
GATE_COL = QKV
F_COL = QKV + 2 * D
FFC = 256
RMS_EPS = 1e-6
LR, B1, B2, EPS, WD, STEP = 0.001, 0.9, 0.999, 1e-08, 0.01, 10
VMEM_LIMIT = 56 * 1024 * 1024

MESH = pl.DeviceIdType.MESH
ANY = pl.BlockSpec(memory_space=pl.ANY)


def _cparams(*sem):
    return pltpu.CompilerParams(dimension_semantics=sem if sem else None, vmem_limit_bytes=VMEM_LIMIT)


def _all_gather(xs, name):
    n = len(xs)

    def body(*refs):
        x_refs, out_refs = refs[:n], refs[n:2 * n]
        send_sems, recv_sems, local_sems = refs[2 * n:]
        mx, my, mc = lax.axis_index("x"), lax.axis_index("y"), lax.axis_index("c")
        me, sibling = (mx, my, mc), (mx, my, 1 - mc)
        chips = [(1 - mx, my), (mx, 1 - my), (1 - mx, 1 - my)]

        def copy(a, k, block, to, own=False):
            px, py, pc = block
            slot = out_refs[a].at[4 * px + 2 * py + pc]
            return pltpu.make_async_remote_copy(
                src_ref=x_refs[a] if own else slot, dst_ref=slot,
                send_sem=send_sems.at[7 * a + k], recv_sem=recv_sems.at[7 * a + k],
                device_id=to, device_id_type=MESH)

        mine = [pltpu.make_async_copy(x_refs[a], out_refs[a].at[4 * mx + 2 * my + mc], local_sems.at[a]) for a in range(n)]
        for cp in mine:
            cp.start()
        first = []
        for a in range(n):
            first.append(copy(a, 0, me, sibling, own=True))
            first += [copy(a, 1 + j, me, (*chip, mc), own=True) for j, chip in enumerate(chips)]
        for cp in first:
            cp.start()
        passed = []
        for j, chip in enumerate(chips):
            for a in range(n):
                copy(a, 1 + j, (*chip, mc), me).wait_recv()
                fwd = copy(a, 4 + j, (*chip, mc), sibling)
                fwd.start()
                passed.append(fwd)
        for a in range(n):
            copy(a, 0, sibling, me).wait_recv()
            for j, chip in enumerate(chips):
                copy(a, 4 + j, (*chip, 1 - mc), me).wait_recv()
        for cp in first + passed:
            cp.wait_send()
        for cp in mine:
            cp.wait()

    return pl.pallas_call(
        body, name=name,
        out_shape=tuple(jax.ShapeDtypeStruct((N_DEV,) + x.shape, x.dtype) for x in xs),
        in_specs=[ANY] * n, out_specs=tuple([ANY] * n),
        scratch_shapes=[pltpu.SemaphoreType.DMA((7 * n,)), pltpu.SemaphoreType.DMA((7 * n,)),
                        pltpu.SemaphoreType.DMA((n,))],
    )(*xs)


HBM = pl.BlockSpec(memory_space=pltpu.HBM)
SEM = pl.BlockSpec(memory_space=pltpu.SEMAPHORE)
EFFECT = pltpu.SideEffectType.DATAFLOW_SIDE_EFFECTING


def _peer_copies(src_refs, land_refs, send_sems, recv_sems, per_peer):
    mx, my, mc = lax.axis_index("x"), lax.axis_index("y"), lax.axis_index("c")
    me_idx = 4 * mx + 2 * my + mc
    copies = []
    for k in range(1, N_DEV):
        px, py, pc = mx ^ (k >> 2), my ^ ((k >> 1) & 1), mc ^ (k & 1)
        for a, (src, land) in enumerate(zip(src_refs, land_refs)):
            copies.append(pltpu.make_async_remote_copy(
                src_ref=src.at[4 * px + 2 * py + pc] if per_peer else src, dst_ref=land.at[me_idx],
                send_sem=send_sems.at[7 * a + k - 1], recv_sem=recv_sems.at[7 * a + k - 1],
                device_id=(px, py, pc), device_id_type=MESH))
    return me_idx, copies


def _remote_start(srcs, per_peer, after, name):
    n = len(srcs)
    lands = [lax.empty(s.shape if per_peer else (N_DEV,) + s.shape, s.dtype) for s in srcs]

    def body(*refs):
        src_refs, land_refs = refs[:n], refs[n:2 * n]
        send_sems, recv_sems = refs[2 * n + 1:2 * n + 3]
        token = refs[4 * n + 3]
        stage, local_sems = refs[4 * n + 4:5 * n + 4], refs[5 * n + 4]
        me_idx, copies = _peer_copies(src_refs, land_refs, send_sems, recv_sems, per_peer)
        for cp in copies:
            cp.start()
        own = [src_refs[a].at[me_idx] if per_peer else src_refs[a] for a in range(n)]
        for hop in ([(own[a], stage[a]) for a in range(n)], [(stage[a], land_refs[a].at[me_idx]) for a in range(n)]):
            cps = [pltpu.make_async_copy(s, d, local_sems.at[a]) for a, (s, d) in enumerate(hop)]
            for cp in cps:
                cp.start()
            for cp in cps:
                cp.wait()
        token[...] = jnp.zeros_like(token)

    thru = [pltpu.HBM(a.shape, a.dtype) for a in list(srcs) + lands]
    out = pl.pallas_call(
        body, name=name,
        out_shape=(pltpu.SemaphoreType.DMA((7 * n,)), pltpu.SemaphoreType.DMA((7 * n,)), *thru,
                   jax.ShapeDtypeStruct((8, LANES), F32)),
        in_specs=[HBM] * (2 * n) + [ANY],
        out_specs=(SEM, SEM, *([HBM] * (2 * n)), pl.BlockSpec(memory_space=pltpu.VMEM)),
        input_output_aliases={i: 2 + i for i in range(2 * n)},
        scratch_shapes=[pltpu.VMEM(s.shape[1:] if per_peer else s.shape, s.dtype) for s in srcs]
        + [pltpu.SemaphoreType.DMA((n,))],
        compiler_params=pltpu.CompilerParams(has_side_effects=EFFECT),
    )(*[pltpu.with_memory_space_constraint(a, pltpu.HBM) for a in list(srcs) + lands], after)
    return dict(sems=out[:2], bufs=out[2:2 * n + 2], per_peer=per_peer), out[-1]


def _remote_wait(pending, after, name):
    bufs = pending["bufs"]
    n = len(bufs) // 2
    per_peer = pending["per_peer"]

    def body(*refs):
        src_refs, land_refs = refs[:n], refs[n:2 * n]
        send_sems, recv_sems = refs[2 * n:2 * n + 2]
        _, copies = _peer_copies(src_refs, land_refs, send_sems, recv_sems, per_peer)
        for cp in copies:
            cp.wait_send()
        for cp in copies:
            cp.wait_recv()

    out = pl.pallas_call(
        body, name=name, out_shape=tuple(pltpu.HBM(a.shape, a.dtype) for a in bufs),
        in_specs=[HBM] * (2 * n) + [SEM, SEM, ANY], out_specs=tuple([HBM] * (2 * n)),
        input_output_aliases={i: i for i in range(2 * n)},
        compiler_params=pltpu.CompilerParams(has_side_effects=EFFECT),
    )(*bufs, *pending["sems"], after)
    return out[n:]


ROWS_PER_COPY = 512


def _pad_rows(front, body_rows, nseq, name):
    tail = LP - L_REAL
    nblk = SEQ // ROWS_PER_COPY

    def body(f_ref, b_ref, o_ref, z_ref, sems):
        s, i = pl.program_id(0), pl.program_id(1)
        rows = pltpu.make_async_copy(b_ref, o_ref.at[pl.ds(s, 1), pl.ds(N_META + i * ROWS_PER_COPY, ROWS_PER_COPY)], sems.at[0])
        rows.start()

        @pl.when(i == 0)
        def _():
            z_ref[...] = jnp.zeros_like(z_ref)
            head = pltpu.make_async_copy(f_ref, o_ref.at[s, pl.ds(0, N_META)], sems.at[1])
            zeros = pltpu.make_async_copy(z_ref, o_ref.at[s, pl.ds(L_REAL, tail)], sems.at[2])
            head.start()
            zeros.start()
            head.wait()
            zeros.wait()

        rows.wait()

    return pl.pallas_call(
        body, name=name, out_shape=jax.ShapeDtypeStruct((nseq, LP, D), F32), grid=(nseq, nblk),
        in_specs=[pl.BlockSpec((N_META, D), lambda s, i: (0, 0)), pl.BlockSpec((1, ROWS_PER_COPY, D), lambda s, i: (s, i, 0))],
        out_specs=ANY,
        scratch_shapes=[pltpu.VMEM((tail, D), F32), pltpu.SemaphoreType.DMA((3,))],
        compiler_params=_cparams("arbitrary", "arbitrary"))(front, body_rows)


def _plan_cols(n_q, n_dcols, src_of):
    plan = {}
    for q in range(n_q):
        for dblk in range(n_dcols // LANES):
            segs, key, start = [], None, 0
            for lane in range(LANES + 1):
                new = None
                if lane < LANES:
                    src = src_of(q, dblk * LANES + lane)
                    if src is not None:
                        new = (src[0], src[1] // LANES, (lane - src[1] % LANES) % LANES)
                if new != key:
                    if key is not None:
                        segs.append((*key, start, lane))
                    key, start = new, lane
            plan[(q, dblk)] = segs
    return plan


def _relayout(src, n_q, n_dcols, src_of, out_dtype, tr, name):
    n_p, rows, scols = src.shape
    plan = _plan_cols(n_q, n_dcols, src_of)

    def body(s_ref, d_ref):
        lane = lax.broadcasted_iota(jnp.int32, (tr, LANES), 1)
        for (q, dblk), segs in plan.items():
            acc = jnp.zeros((tr, LANES), F32)
            for p, sblk, rot, lo, hi in segs:
                x = s_ref[p, :, sblk * LANES:(sblk + 1) * LANES].astype(F32)
                if rot:
                    x = pltpu.roll(x, rot, 1)
                acc = x if (lo, hi) == (0, LANES) else jnp.where((lane >= lo) & (lane < hi), x, acc)
            d_ref[q, :, dblk * LANES:(dblk + 1) * LANES] = acc.astype(out_dtype)

    return pl.pallas_call(
        body, name=name, out_shape=jax.ShapeDtypeStruct((n_q, rows, n_dcols), out_dtype), grid=(rows // tr,),
        in_specs=[pl.BlockSpec((n_p, tr, scols), lambda i: (0, i, 0))],
        out_specs=pl.BlockSpec((n_q, tr, n_dcols), lambda i: (0, i, 0)),
        compiler_params=_cparams("parallel"))(src)


def _in_padded_to_orig(d):
    if d < QKV:
        kind, r = divmod(d, 4 * PAIR_W)
        pair, r = divmod(r, PAIR_W)
        part, r = divmod(r, LANES)
        return kind * 3 * W_ATT + part * W_ATT + pair * LANES + r
    if d < F_COL:
        return d + NH
    if d < F_COL + NH:
        return d - 2 * D
    return None


_IN_ORIG_TO_PADDED = {_in_padded_to_orig(d): d for d in range(IN_P) if _in_padded_to_orig(d) is not None}


def _up_inter_to_orig(d):
    j, r = divmod(d, 2 * FFC)
    part, r = divmod(r, FFC)
    return part * D_FF + j * FFC + r


_UP_ORIG_TO_INTER = {_up_inter_to_orig(d): d for d in range(2 * D_FF)}
IN_SHARD = IN_COLS // N_DEV
UP_SHARD = 2 * D_FF // N_DEV
SHARD_P = 768
ATT_SHARD = D // N_DEV


def _gathered_to_full(n_shard, to_orig):
    def src_of(q, d):
        c = to_orig(d)
        return None if c is None else (c // n_shard, c % n_shard)
    return src_of


def _full_to_shards(n_shard, from_orig):
    def src_of(q, d):
        return (0, from_orig(q * n_shard + d)) if d < n_shard else None
    return src_of


def _matmul(a, b, *, out_dtype, tm, tn, tk, ta=False, tb=False, after=None, name):
    if ta:
        kdim, m = a.shape
    else:
        m, kdim = a.shape
    n = b.shape[0] if tb else b.shape[1]
    assert m % tm == 0 and n % tn == 0 and kdim % tk == 0, (name, a.shape, b.shape, tm, tn, tk)
    nk = kdim // tk

    def body(a_ref, b_ref, *rest):
        o_ref, scratch = rest[len(extra)], rest[len(extra) + 1:]
        av, bv = a_ref[...], b_ref[...]
        if ta:
            p = lax.dot_general(av, bv, (((0,), (0,)), ((), ())), preferred_element_type=F32)
        elif tb:
            p = lax.dot_general(av, bv, (((1,), (1,)), ((), ())), preferred_element_type=F32)
        else:
            p = jnp.dot(av, bv, preferred_element_type=F32)
        if nk == 1:
            o_ref[...] = p.astype(o_ref.dtype)
        else:
            acc_ref, = scratch
            k = pl.program_id(2)

            @pl.when(k == 0)
            def _():
                acc_ref[...] = p

            @pl.when(k > 0)
            def _():
                acc_ref[...] += p

            @pl.when(k == nk - 1)
            def _():
                o_ref[...] = acc_ref[...].astype(o_ref.dtype)

    extra = [] if after is None else [after]
    a_spec = pl.BlockSpec((tk, tm), lambda i, j, k: (k, i)) if ta else pl.BlockSpec((tm, tk), lambda i, j, k: (i, k))
    b_spec = pl.BlockSpec((tn, tk), lambda i, j, k: (j, k)) if tb else pl.BlockSpec((tk, tn), lambda i, j, k: (k, j))
    return pl.pallas_call(
        body, name=name,
        out_shape=jax.ShapeDtypeStruct((m, n), out_dtype),
        grid=(m // tm, n // tn, nk),
        in_specs=[a_spec, b_spec] + [ANY] * len(extra),
        out_specs=pl.BlockSpec((tm, tn), lambda i, j, k: (i, j)),
        scratch_shapes=[] if nk == 1 else [pltpu.VMEM((tm, tn), F32)],
        compiler_params=_cparams("parallel", "parallel", "arbitrary"),
    )(a, b, *extra)


TR = 576


def _rms(h):
    return lax.rsqrt(jnp.mean(h * h, axis=-1, keepdims=True) + RMS_EPS)


def _norm_fwd(h, g, name):
    t = h.shape[0]
    row = pl.BlockSpec((TR, D), lambda i: (i, 0))

    def body(h_ref, g_ref, n_ref):
        hv = h_ref[...]
        n_ref[...] = ((hv * _rms(hv)) * g_ref[...]).astype(BF16)

    return pl.pallas_call(
        body, name=name, out_shape=jax.ShapeDtypeStruct((t, D), BF16), grid=(t // TR,),
        in_specs=[row, pl.BlockSpec((1, D), lambda i: (0, 0))], out_specs=row, compiler_params=_cparams("parallel"))(h, g)


EPI_ROWS = 288


def _matmul_rows(a, b, rows_in, vecs_in, epilogue, row_outs, sum_outs, *, tm, tk, tb=False, after=None,
                 unpadded_in=None, unpadded_out=False, name):
    m, kdim = a.shape
    assert (b.shape[0] if tb else b.shape[1]) == D and m % tm == 0 and kdim % tk == 0 and tm % EPI_ROWS == 0
    nk = kdim // tk
    n_r, n_v, n_ro, n_so = len(rows_in), len(vecs_in), len(row_outs), len(sum_outs)
    extra = ([] if after is None else [after]) + ([] if unpadded_in is None else [unpadded_in])
    spans = []
    for q in range(LP // tm):
        lo, hi = max(q * tm, N_META), min((q + 1) * tm, L_REAL)
        spans.append((lo - N_META, hi - lo, lo - q * tm))

    def body(a_ref, b_ref, *rest):
        r_refs, v_refs = rest[:n_r], rest[n_r:n_r + n_v]
        outs = rest[n_r + n_v + len(extra):]
        ro_refs, so_refs, acc_ref = outs[:n_ro], outs[n_ro:n_ro + n_so], outs[n_ro + n_so]
        i, k = pl.program_id(0), pl.program_id(1)

        def unpadded_copies(do):
            src, tile, sem = rest[n_r + n_v + len(extra) - 1], outs[n_ro + n_so + 1], outs[n_ro + n_so + 2]
            for q, (src0, n, dst0) in enumerate(spans):
                @pl.when(i % len(spans) == q)
                def _():
                    do(pltpu.make_async_copy(src.at[i // len(spans), pl.ds(src0, n)], tile.at[pl.ds(dst0, n)], sem),
                       tile, dst0, n)

        def start(cp, tile, dst0, n):
            if dst0:
                tile[pl.ds(0, dst0), :] = jnp.zeros((dst0, D), F32)
            if dst0 + n < tm:
                tile[pl.ds(dst0 + n, tm - dst0 - n), :] = jnp.zeros((tm - dst0 - n, D), F32)
            cp.start()

        if unpadded_in is not None:
            @pl.when(k == 0)
            def _():
                unpadded_copies(start)

        def out_copies(tile_idx, do):
            stage, sems = outs[-2], outs[-1]
            seq = tile_idx // len(spans)
            for q, (dst0, n, src0) in enumerate(spans):
                @pl.when(tile_idx % len(spans) == q)
                def _():
                    do(pltpu.make_async_copy(stage.at[pl.ds(src0, n)], ro_refs[0].at[seq, pl.ds(dst0, n)], sems.at[0]))
                    if q == 0:
                        do(pltpu.make_async_copy(stage.at[pl.ds(0, N_META)], ro_refs[1].at[seq], sems.at[1]))
        if tb:
            p = lax.dot_general(a_ref[...], b_ref[...], (((1,), (1,)), ((), ())), preferred_element_type=F32)
        else:
            p = jnp.dot(a_ref[...], b_ref[...], preferred_element_type=F32)

        @pl.when(k == 0)
        def _():
            acc_ref[...] = p

        @pl.when(k > 0)
        def _():
            acc_ref[...] += p

        @pl.when(k == nk - 1)
        def _():
            vecs = [v[...] for v in v_refs]
            tiles_in = list(r_refs)
            if unpadded_in is not None:
                unpadded_copies(lambda cp, *_: cp.wait())
                tiles_in.append(outs[n_ro + n_so + 1])
            tile_outs = list(ro_refs)
            if unpadded_out:
                tile_outs = [outs[-2]] + tile_outs[2:]

                @pl.when(i > 0)
                def _():
                    out_copies(i - 1, lambda cp: cp.wait())

            def step(c, sums):
                rows = pl.ds(pl.multiple_of(c * EPI_ROWS, 8), EPI_ROWS)
                tiles, terms = epilogue(i * tm + c * EPI_ROWS, acc_ref[rows, :], *[r[rows, :] for r in tiles_in], *vecs)
                for o, tile in zip(tile_outs, tiles):
                    o[rows, :] = tile.astype(o.dtype)
                return tuple(s + term for s, term in zip(sums, terms))

            sums = lax.fori_loop(0, tm // EPI_ROWS, step, tuple(jnp.zeros(s, F32) for s in sum_outs))
            if unpadded_out:
                out_copies(i, lambda cp: cp.start())

                @pl.when(i == m // tm - 1)
                def _():
                    out_copies(i, lambda cp: cp.wait())

            @pl.when(i == 0)
            def _():
                for o in so_refs:
                    o[...] = jnp.zeros_like(o)

            for o, s in zip(so_refs, sums):
                o[...] += s

    row = pl.BlockSpec((tm, D), lambda i, k: (i, 0))
    b_spec = pl.BlockSpec((D, tk), lambda i, k: (0, k)) if tb else pl.BlockSpec((tk, D), lambda i, k: (k, 0))
    row_shapes = [jax.ShapeDtypeStruct((m, D), dt) for dt in row_outs]
    row_specs = [row] * len(row_outs)
    if unpadded_out:
        nseq = m // LP
        row_shapes[:1] = [jax.ShapeDtypeStruct((nseq, SEQ, D), F32), jax.ShapeDtypeStruct((nseq, N_META, D), F32)]
        row_specs[:1] = [ANY, ANY]
        n_ro += 1
    return pl.pallas_call(
        body, name=name,
        out_shape=tuple(row_shapes + [jax.ShapeDtypeStruct(s, F32) for s in sum_outs]),
        grid=(m // tm, nk),
        in_specs=[pl.BlockSpec((tm, tk), lambda i, k: (i, k)), b_spec] + [row] * n_r
        + [pl.BlockSpec((1, D), lambda i, k: (0, 0))] * n_v + [ANY] * len(extra),
        out_specs=tuple(row_specs + [pl.BlockSpec(s, lambda i, k: (0, 0)) for s in sum_outs]),
        scratch_shapes=[pltpu.VMEM((tm, D), F32)]
        + ([] if unpadded_in is None else [pltpu.VMEM((tm, D), F32), pltpu.SemaphoreType.DMA])
        + ([pltpu.VMEM((tm, D), F32), pltpu.SemaphoreType.DMA((2,))] if unpadded_out else []),
        compiler_params=_cparams("arbitrary", "arbitrary"))(a, b, *rows_in, *vecs_in, *extra)


def _residual_norm(row0, acc, h, g):
    hv = h + acc
    return (hv, (hv * _rms(hv)) * g), ()


def _rms_bwd_math(hv, dn, gv):
    r = _rms(hv)
    hr = hv * r
    dng = dn * gv
    dh = r * (dng - hr * jnp.mean(dng * hr, axis=-1, keepdims=True))
    return dh, dn * hr


def _loss_head(row0, acc, h1, tgt, g):
    hv = h1 + acc
    hr = hv * _rms(hv)
    pos = row0 % LP + lax.broadcasted_iota(jnp.int32, (EPI_ROWS, 1), 0)
    valid = (pos >= N_META) & (pos < L_REAL)
    err = jnp.where(valid, hr * g - tgt, 0.0)
    part = 0.5 * jnp.sum(jnp.mean(err * err, axis=-1, keepdims=True))
    dy = err * (1.0 / D)
    dh, dgrow = _rms_bwd_math(hv, dy, g)
    return (dh, dh), (jnp.full((8, LANES), part, F32), jnp.sum(dgrow, axis=0, keepdims=True))


def _residual_norm_bwd(row0, acc, h, dres, g):
    dh, dgrow = _rms_bwd_math(h, acc, g)
    dh = dh + dres
    return (dh, dh), (jnp.sum(dgrow, axis=0, keepdims=True),)


def _residual_norm_bwd_f32(row0, acc, h, dres, g):
    tiles, sums = _residual_norm_bwd(row0, acc, h, dres, g)
    return tiles[:1], sums


GATE_BLK = GATE_COL // D


def _sigmoid(x):
    return 1.0 / (1.0 + jnp.exp(-x))


def _merge_fwd(p_sb, p_fx, proj, name):
    t = p_sb.shape[0]
    row = pl.BlockSpec((TR, D), lambda i: (i, 0))

    def body(ps_ref, pf_ref, gs_ref, gf_ref, o_ref):
        o_ref[...] = (_sigmoid(gs_ref[...]) * ps_ref[...] + _sigmoid(gf_ref[...]) * pf_ref[...]).astype(BF16)

    return pl.pallas_call(
        body, name=name, out_shape=jax.ShapeDtypeStruct((t, D), BF16), grid=(t // TR,),
        in_specs=[row, row, pl.BlockSpec((TR, D), lambda i: (i, GATE_BLK)),
                  pl.BlockSpec((TR, D), lambda i: (i, GATE_BLK + 1))],
        out_specs=row, compiler_params=_cparams("parallel"))(p_sb, p_fx, proj, proj)


def _merge_bwd(dm, p, proj, dproj, which, name):
    t = dm.shape[0]
    row = pl.BlockSpec((TR, D), lambda i: (i, 0))
    gate = pl.BlockSpec((TR, D), lambda i: (i, GATE_BLK + which))

    def body(dm_ref, p_ref, g_ref, *rest):
        dp_ref, dg_ref = rest[-2:]
        dmv = dm_ref[...]
        s = _sigmoid(g_ref[...])
        dp_ref[...] = (dmv * s).astype(BF16)
        dg_ref[...] = (dmv * p_ref[...] * s * (1.0 - s)).astype(BF16)

    out_shape = (jax.ShapeDtypeStruct((t, D), BF16), jax.ShapeDtypeStruct((t, IN_P), BF16))
    if dproj is None:
        return pl.pallas_call(
            body, name=name, out_shape=out_shape, grid=(t // TR,), in_specs=[row, row, gate],
            out_specs=(row, gate), compiler_params=_cparams("parallel"))(dm, p, proj)
    return pl.pallas_call(
        body, name=name, out_shape=out_shape, grid=(t // TR,), in_specs=[row, row, gate, ANY],
        out_specs=(row, gate), input_output_aliases={3: 1}, compiler_params=_cparams("parallel"))(dm, p, proj, dproj)


CH = 288


def _chunk(c, n=CH):
    return pl.ds(pl.multiple_of(c * CH, 8), n)


def _conv_taps(u_ref, c):
    x = u_ref[_chunk(c), :]
    prev = u_ref[pl.ds(pl.multiple_of(jnp.maximum(c * CH - 8, 0), 8), 8), :]
    xx = jnp.concatenate([jnp.where(c == 0, 0.0, prev), x], axis=0)
    return x, pltpu.roll(xx, 1, 0)[8:], pltpu.roll(xx, 2, 0)[8:]


def _conv_glu_fwd(u, cw, nseq, name):
    nblk = D_FF // FFC

    def body(u_ref, cw_ref, o_ref):
        cwv = cw_ref[...]

        def step(c, _):
            x, x1, x2 = _conv_taps(u_ref, c)
            uc = cwv[0:1, :] * x2 + cwv[1:2, :] * x1 + cwv[2:3, :] * x
            a, b = uc[:, :FFC], uc[:, FFC:]
            o_ref[_chunk(c), :] = (a * _sigmoid(a) * b).astype(BF16)
            return 0

        lax.fori_loop(0, LP // CH, step, 0)

    return pl.pallas_call(
        body, name=name, out_shape=jax.ShapeDtypeStruct((nseq * LP, D_FF), BF16), grid=(nseq, nblk),
        in_specs=[pl.BlockSpec((LP, 2 * FFC), lambda s, j: (s, j)), pl.BlockSpec((3, 2 * FFC), lambda s, j: (0, j))],
        out_specs=pl.BlockSpec((LP, FFC), lambda s, j: (s, j)),
        compiler_params=_cparams("parallel", "parallel"))(u, cw)


def _conv_glu_bwd(u, cw, dact, nseq, name):
    nblk = D_FF // FFC
    nch = LP // CH

    def body(u_ref, cw_ref, da_ref, du_ref, dcw_ref):
        s = pl.program_id(1)
        cwv = cw_ref[...]

        def step(k, carry):
            nxt, p0, p1, p2 = carry
            c = nch - 1 - k
            x, x1, x2 = _conv_taps(u_ref, c)
            uc = cwv[0:1, :] * x2 + cwv[1:2, :] * x1 + cwv[2:3, :] * x
            a, b = uc[:, :FFC], uc[:, FFC:]
            sa = _sigmoid(a)
            dactv = da_ref[_chunk(c), :]
            da = dactv * b * (sa * (1.0 + a * (1.0 - sa)))
            db = dactv * (a * sa)
            duc = jnp.concatenate([da, db], axis=1)
            dd = jnp.concatenate([duc, nxt], axis=0)
            du = (cwv[2:3, :] * duc + cwv[1:2, :] * pltpu.roll(dd, CH + 7, 0)[:CH]
                  + cwv[0:1, :] * pltpu.roll(dd, CH + 6, 0)[:CH])
            du_ref[_chunk(c), :] = du.astype(BF16)
            return (duc[:8], p0 + jnp.sum(duc * x2, axis=0, keepdims=True),
                    p1 + jnp.sum(duc * x1, axis=0, keepdims=True), p2 + jnp.sum(duc * x, axis=0, keepdims=True))

        zrow = jnp.zeros((1, 2 * FFC), F32)
        _, p0, p1, p2 = lax.fori_loop(0, nch, step, (jnp.zeros((8, 2 * FFC), F32), zrow, zrow, zrow))

        @pl.when(s == 0)
        def _():
            dcw_ref[...] = jnp.zeros_like(dcw_ref)

        dcw_ref[...] += jnp.concatenate([p0, p1, p2], axis=0)

    return pl.pallas_call(
        body, name=name,
        out_shape=(jax.ShapeDtypeStruct((nseq * LP, 2 * D_FF), BF16), jax.ShapeDtypeStruct((3, 2 * D_FF), F32)),
        grid=(nblk, nseq),
        in_specs=[pl.BlockSpec((LP, 2 * FFC), lambda j, s: (s, j)), pl.BlockSpec((3, 2 * FFC), lambda j, s: (0, j)),
                  pl.BlockSpec((LP, FFC), lambda j, s: (s, j))],
        out_specs=(pl.BlockSpec((LP, 2 * FFC), lambda j, s: (s, j)), pl.BlockSpec((3, 2 * FFC), lambda j, s: (0, j))),
        compiler_params=_cparams("parallel", "arbitrary"))(u, cw, dact)


F_BLK = F_COL // LANES
CB = 128


def _split3(x):
    hi = x.astype(BF16)
    r1 = x - hi.astype(F32)
    mid = r1.astype(BF16)
    lo = (r1 - mid.astype(F32)).astype(BF16)
    return hi, mid, lo


def _tri_dot(tri, x):
    hi, mid, lo = _split3(x)
    d = functools.partial(jnp.dot, preferred_element_type=F32)
    return d(tri, hi) + d(tri, mid) + d(tri, lo)


def _log_sigmoid(x):
    return jnp.minimum(x, 0.0) - jnp.log(1.0 + jnp.exp(-jnp.abs(x)))


def _gate_fwd(proj, bf, nseq, name):
    def body(f_ref, b_ref, c_ref):
        r_i = lax.broadcasted_iota(jnp.int32, (CB, CB), 0)
        c_i = lax.broadcasted_iota(jnp.int32, (CB, CB), 1)
        tri = (c_i <= r_i).astype(BF16)
        bv = b_ref[...]

        def step(k, carry):
            rows = pl.ds(pl.multiple_of(k * CB, CB), CB)
            lf = _log_sigmoid(f_ref[rows, :] + bv)
            c_ref[rows, :] = _tri_dot(tri, lf) + carry
            return carry + jnp.sum(lf, axis=0, keepdims=True)

        lax.fori_loop(0, LP // CB, step, jnp.zeros((1, LANES), F32))

    return pl.pallas_call(
        body, name=name, out_shape=jax.ShapeDtypeStruct((nseq * LP, LANES), F32), grid=(nseq,),
        in_specs=[pl.BlockSpec((LP, LANES), lambda s: (s, F_BLK)), pl.BlockSpec((1, LANES), lambda s: (0, 0))],
        out_specs=pl.BlockSpec((LP, LANES), lambda s: (s, 0)),
        compiler_params=_cparams("parallel"))(proj, bf)


def _gate_bwd(proj, bf, dc, dproj, nseq, name):
    def body(f_ref, b_ref, dc_ref, _, df_ref, db_ref):
        s = pl.program_id(0)
        r_i = lax.broadcasted_iota(jnp.int32, (CB, CB), 0)
        c_i = lax.broadcasted_iota(jnp.int32, (CB, CB), 1)
        tri = (c_i >= r_i).astype(BF16)
        bv = b_ref[...]

        def step(kk, carry):
            carry_c, carry_b = carry
            k = LP // CB - 1 - kk
            rows = pl.ds(pl.multiple_of(k * CB, CB), CB)
            dcv = dc_ref[rows, :]
            dlf = _tri_dot(tri, dcv) + carry_c
            df = dlf * _sigmoid(-(f_ref[rows, :] + bv))
            df_ref[rows, :] = jnp.concatenate([df, jnp.zeros_like(df)], axis=1).astype(BF16)
            return carry_c + jnp.sum(dcv, axis=0, keepdims=True), carry_b + jnp.sum(df, axis=0, keepdims=True)

        zero = jnp.zeros((1, LANES), F32)
        _, dbp = lax.fori_loop(0, LP // CB, step, (zero, zero))

        @pl.when(s == 0)
        def _():
            db_ref[...] = jnp.zeros_like(db_ref)

        db_ref[...] += dbp

    return pl.pallas_call(
        body, name=name,
        out_shape=(jax.ShapeDtypeStruct(dproj.shape, BF16), jax.ShapeDtypeStruct((1, LANES), F32)), grid=(nseq,),
        in_specs=[pl.BlockSpec((LP, LANES), lambda s: (s, F_BLK)), pl.BlockSpec((1, LANES), lambda s: (0, 0)),
                  pl.BlockSpec((LP, LANES), lambda s: (s, 0)), ANY],
        out_specs=(pl.BlockSpec((LP, 2 * LANES), lambda s: (s, F_COL // (2 * LANES))), pl.BlockSpec((1, LANES), lambda s: (0, 0))),
        input_output_aliases={3: 0},
        compiler_params=_cparams("arbitrary"))(proj, bf, dc, dproj)


SCALE = 0.125
NEG = -1e30


def _dot_nt(a, b):
    return lax.dot_general(a, b, (((1,), (1,)), ((), ())), preferred_element_type=F32)


def _dot(a, b):
    return jnp.dot(a, b, preferred_element_type=F32)


def _blk(i, n=BQ):
    return pl.ds(pl.multiple_of(i * BQ, BQ), n)


def _query_blocks(qblock, n_last):
    def full(i, _):
        qblock(i, BQ)
        return 0

    lax.fori_loop(0, NBLK - 1, full, 0)
    qblock(jnp.minimum(pl.program_id(0) + NBLK, NBLK - 1), n_last)


def _tile_iotas():
    return lax.broadcasted_iota(jnp.int32, (BQ, BQ), 0), lax.broadcasted_iota(jnp.int32, (BQ, BQ), 1)


def _lane_iota():
    return lax.broadcasted_iota(jnp.int32, (BQ, LANES), 1)


def _head_masks():
    lane = _lane_iota()
    return lane < HEAD, lane >= HEAD


def _only(mask, x):
    return jnp.where(mask, x, jnp.zeros_like(x))


def _chains(npair):
    return [(pp, h) for pp in range(npair) for h in range(2)]


def _load_qkv(p_ref, q_s, k_s, v_s):
    for pp in range(q_s.shape[0]):
        base = pp * PAIR_W
        q_s[pp] = (p_ref[:, base:base + LANES] * SCALE).astype(BF16)
        k_s[pp] = p_ref[:, base + LANES:base + 2 * LANES].astype(BF16)
        v_s[pp] = p_ref[:, base + 2 * LANES:base + 3 * LANES].astype(BF16)


def _softplus(z):
    return jnp.maximum(z, 0.0) + jnp.log(1.0 + jnp.exp(-jnp.abs(z)))


def _sb_tile_weights(q, k, strict, r, u_suf):
    n = len(q)
    z = [_dot_nt(q[c], k[c]) for c in range(n)]
    sp = [_softplus(zc) for zc in z]
    lk = [-spc if strict is None else jnp.where(strict, -spc, 0.0) for spc in sp]
    suf = [_dot(lkc.astype(BF16), u_suf) for lkc in lk]
    w = [jnp.exp(z[c] - sp[c] + r[c] + suf[c]) for c in range(n)]
    if strict is not None:
        w = [jnp.where(strict, wc, 0.0) for wc in w]
    r_next = [r[c] + suf[c][:, 0:1] + lk[c][:, 0:1] for c in range(n)]
    return w, sp, r_next


def _group_spec(kind, npair):
    return pl.BlockSpec((LP, npair * PAIR_W), lambda s, g: (s, (NH // (2 * npair)) * kind + g))


def _gheads_spec(npair):
    return pl.BlockSpec((LP, npair * LANES), lambda s, g: (s, g))


def _qkv_scratch(npair):
    return [pltpu.VMEM((npair, LP, LANES), BF16)] * 3


SEQ_SPEC = pl.BlockSpec((LP, LANES), lambda s, g: (s, 0))
RS_STRIDE = 16
Q_LAST = 128
Q_LAST_ROWS = 16


def _pair_cols(pp):
    return slice(pp * LANES, (pp + 1) * LANES)


def _carry_spec(npair):
    return pl.BlockSpec((None, npair * LANES, LP), lambda s, g: (s, g, 0))


def _sb_fwd(proj, nseq, npair, name):
    t = nseq * LP
    chains = _chains(npair)

    def body(p_ref, o_ref, rs_ref, q_s, k_s, v_s, acc_ref, r_ref, rb_ref):
        _load_qkv(p_ref, q_s, k_s, v_s)
        row, col = _tile_iotas()
        u_suf = (row > col).astype(BF16)
        heads = _head_masks()

        def qblock(i, nq):
            diag = (col < row)[:nq]
            lane = _lane_iota()[:nq]
            heads_q = [hm[:nq] for hm in heads]
            acc_ref[...] = jnp.zeros_like(acc_ref)
            rb_ref[...] = jnp.zeros_like(rb_ref)
            r_ref[...] = jnp.zeros_like(r_ref)
            qb = [q_s[pp, _blk(i, nq), :] for pp in range(npair)]

            def tile(j, strict):
                kj = [k_s[pp, _blk(j), :] for pp in range(npair)]
                vj = [v_s[pp, _blk(j), :] for pp in range(npair)]
                r = [r_ref[c, :nq] for c in range(len(chains))]
                w, _, r_next = _sb_tile_weights([_only(heads_q[h], qb[pp]) for pp, h in chains],
                                                [kj[pp] for pp, _ in chains], strict, r, u_suf)
                pv = [_dot(w[c].astype(BF16), _only(heads[h], vj[pp])) for c, (pp, h) in enumerate(chains)]
                for pp in range(npair):
                    acc_ref[pp, :nq] += pv[2 * pp] + pv[2 * pp + 1]
                    rb_ref[pp, :nq] = jnp.where(lane == j, r[2 * pp], jnp.where(lane == RS_STRIDE + j, r[2 * pp + 1], rb_ref[pp, :nq]))
                for c in range(len(chains)):
                    r_ref[c, :nq] = r_next[c]

            tile(i, diag)

            def kblock(jj, _):
                tile(i - jj, None)
                return 0

            lax.fori_loop(1, i + 1, kblock, 0)
            for pp in range(npair):
                o_ref[_blk(i), _pair_cols(pp)] = acc_ref[pp].astype(BF16)
                rs_ref[_pair_cols(pp), _blk(i)] = rb_ref[pp].T

        _query_blocks(qblock, Q_LAST_ROWS)

    return pl.pallas_call(
        body, name=name,
        out_shape=(jax.ShapeDtypeStruct((t, W_ATT), BF16), jax.ShapeDtypeStruct((nseq, W_ATT, LP), F32)),
        grid=(nseq, NH // (2 * npair)), in_specs=[_group_spec(0, npair)], out_specs=(_gheads_spec(npair), _carry_spec(npair)),
        scratch_shapes=_qkv_scratch(npair) + [pltpu.VMEM((npair, BQ, LANES), F32), pltpu.VMEM((2 * npair, BQ, 1), F32),
                                      pltpu.VMEM((npair, BQ, LANES), F32)],
        compiler_params=_cparams("parallel", "parallel"))(proj)


def _sb_bwd(proj, do, rs, dproj, after, nseq, npair, name):
    chains = _chains(npair)

    def body(p_ref, do_ref, rs_ref, _, _after, dp_ref, q_s, k_s, v_s, kt_s, dqa_ref, dka_ref, dva_ref, ep_ref):
        _load_qkv(p_ref, q_s, k_s, v_s)
        row, col = _tile_iotas()
        u_after = (col > row).astype(BF16)
        u_before = (col < row).astype(BF16)
        heads = _head_masks()
        sub_all = lax.broadcasted_iota(jnp.int32, (LANES, BQ), 0)
        rows_k = (sub_all < HEAD, sub_all >= HEAD)
        nc = len(chains)
        for pp in range(npair):
            kt_s[pp] = k_s[pp].astype(F32).T.astype(BF16)
        dka_ref[...] = jnp.zeros_like(dka_ref)
        dva_ref[...] = jnp.zeros_like(dva_ref)

        def qblock(i, nq):
            diag = (row < col)[:, :nq]
            rows_of = [m[:, :nq] for m in rows_k]
            heads_q = [m[:nq] for m in heads]
            queries = _blk(i, nq)
            qb = [q_s[pp, queries, :] for pp in range(npair)]
            dob = [do_ref[queries, _pair_cols(pp)] for pp in range(npair)]
            qt = [qb[pp].astype(F32).T.astype(BF16) for pp in range(npair)]
            dot = [dob[pp].astype(F32).T.astype(BF16) for pp in range(npair)]
            qt_m = [jnp.where(rows_of[h], qt[pp], jnp.zeros_like(qt[pp])) for pp, h in chains]
            dot_m = [jnp.where(rows_of[h], dot[pp], jnp.zeros_like(dot[pp])) for pp, h in chains]
            q_m = [_only(heads_q[h], qb[pp]) for pp, h in chains]
            do_m = [_only(heads_q[h], dob[pp]) for pp, h in chains]
            dqa_ref[...] = jnp.zeros_like(dqa_ref)
            ep_ref[...] = jnp.zeros_like(ep_ref)

            def tile(j, strict):
                keys = pl.ds(pl.multiple_of(j * BQ, BQ), BQ)
                kj = [k_s[pp, _blk(j), :] for pp in range(npair)]
                vj = [v_s[pp, _blk(j), :] for pp in range(npair)]
                r = [_key_cols(rs_ref, pp * LANES + RS_STRIDE * h + j, i, nq) for pp, h in chains]
                z = [_dot(kj[pp], qt_m[cidx]) for cidx, (pp, _) in enumerate(chains)]
                dw = [_dot(vj[pp], dot_m[cidx]) for cidx, (pp, _) in enumerate(chains)]
                sp = [_softplus(zc) for zc in z]
                lk = [-spc if strict is None else jnp.where(strict, -spc, 0.0) for spc in sp]
                suf = [_dot(u_after, lkc.astype(BF16)) for lkc in lk]
                w = [jnp.exp(z[cidx] - sp[cidx] + r[cidx] + suf[cidx]) for cidx in range(nc)]
                if strict is not None:
                    w = [jnp.where(strict, wc, 0.0) for wc in w]
                e = [dw[cidx] * w[cidx] for cidx in range(nc)]
                e_pre = [ep_ref[cidx, :, :nq] + _dot(u_before, e[cidx].astype(BF16)) for cidx in range(nc)]
                dz = []
                for cidx in range(nc):
                    ep_ref[cidx, :, :nq] += jnp.sum(e[cidx], axis=0, keepdims=True)
                    sneg = jnp.exp(-sp[cidx])
                    dzc = e[cidx] * sneg - (1.0 - sneg) * e_pre[cidx]
                    if strict is not None:
                        dzc = jnp.where(strict, dzc, 0.0)
                    dz.append(dzc.astype(BF16))
                dq = [_dot(jnp.where(rows_k[h], kt_s[pp, :, keys], jnp.zeros((LANES, BQ), BF16)), dz[cidx])
                      for cidx, (pp, h) in enumerate(chains)]
                dk = [_dot(dz[cidx], q_m[cidx]) for cidx in range(nc)]
                dv = [_dot(w[cidx].astype(BF16), do_m[cidx]) for cidx in range(nc)]
                for pp in range(npair):
                    dqa_ref[pp, :, :nq] += dq[2 * pp] + dq[2 * pp + 1]
                    dka_ref[pp, _blk(j), :] += dk[2 * pp] + dk[2 * pp + 1]
                    dva_ref[pp, _blk(j), :] += dv[2 * pp] + dv[2 * pp + 1]

            def kblock(j, _):
                tile(j, None)
                return 0

            lax.fori_loop(0, i, kblock, 0)
            tile(i, diag)
            for pp in range(npair):
                dp_ref[_blk(i), pp * PAIR_W:pp * PAIR_W + LANES] = (dqa_ref[pp].T * SCALE).astype(BF16)

        _query_blocks(qblock, Q_LAST)
        for pp in range(npair):
            dp_ref[:, pp * PAIR_W + LANES:pp * PAIR_W + 2 * LANES] = dka_ref[pp].astype(BF16)
            dp_ref[:, pp * PAIR_W + 2 * LANES:pp * PAIR_W + 3 * LANES] = dva_ref[pp].astype(BF16)

    return pl.pallas_call(
        body, name=name, out_shape=jax.ShapeDtypeStruct(dproj.shape, BF16), grid=(nseq, NH // (2 * npair)),
        in_specs=[_group_spec(0, npair), _gheads_spec(npair), _carry_spec(npair), ANY, ANY], out_specs=_group_spec(0, npair),
        input_output_aliases={3: 0},
        scratch_shapes=_qkv_scratch(npair) + [pltpu.VMEM((npair, LANES, LP), BF16), pltpu.VMEM((npair, LANES, BQ), F32),
                                      pltpu.VMEM((npair, LP, LANES), F32), pltpu.VMEM((npair, LP, LANES), F32),
                                      pltpu.VMEM((2 * npair, 1, BQ), F32)],
        compiler_params=_cparams("parallel", "parallel"))(proj, do, rs, dproj, after)


CROW_SPEC = pl.BlockSpec((None, NH, LP), lambda s, g: (s, 0, 0))


def _key_cols(cr_ref, head, j, n=BQ):
    return cr_ref[pl.ds(head, 1), _blk(j)][:, :n]


def _fox_fwd(proj, c, crow, nseq, npair, name):
    t = nseq * LP
    chains = _chains(npair)

    def body(p_ref, c_ref, cr_ref, o_ref, o32_ref, lse_ref, q_s, k_s, v_s, vt_s, ck_s, acc_ref, m_ref, l_ref):
        _load_qkv(p_ref, q_s, k_s, v_s)
        row, col = _tile_iotas()
        sub_all = lax.broadcasted_iota(jnp.int32, (LANES, BQ), 0)
        rows_k = (sub_all < HEAD, sub_all >= HEAD)
        head0 = 2 * npair * pl.program_id(1)
        nc = len(chains)
        lane_all = lax.broadcasted_iota(jnp.int32, (LP, LANES), 1)
        for pp in range(npair):
            vt_s[pp] = v_s[pp].astype(F32).T.astype(BF16)
        for cidx in range(nc):
            ck_s[cidx] = jnp.sum(jnp.where(lane_all == head0 + cidx, c_ref[...], 0.0), axis=1, keepdims=True)

        def qblock(i, nq):
            diag = (row <= col)[:, :nq]
            sub = sub_all[:, :nq]
            rows_of = (sub < HEAD, sub >= HEAD)
            qt = [q_s[pp, _blk(i, nq), :].astype(F32).T.astype(BF16) for pp in range(npair)]
            qt = [jnp.where(rows_of[h], qt[pp], jnp.zeros_like(qt[pp])) for pp, h in chains]
            cq = [_key_cols(cr_ref, head0 + cidx, i, nq) for cidx in range(nc)]
            acc_ref[...] = jnp.zeros_like(acc_ref)
            m_ref[...] = jnp.full_like(m_ref, NEG)
            l_ref[...] = jnp.zeros_like(l_ref)

            def tile(j, causal):
                keys = pl.ds(pl.multiple_of(j * BQ, BQ), BQ)
                z = [_dot(k_s[pp, _blk(j), :], qt[cidx]) + (cq[cidx] - ck_s[cidx, _blk(j), :])
                     for cidx, (pp, _) in enumerate(chains)]
                if causal is not None:
                    z = [jnp.where(causal, zc, NEG) for zc in z]
                p, alpha = [], []
                for cidx in range(nc):
                    m_old = m_ref[cidx, :, :nq]
                    m_new = jnp.maximum(m_old, jnp.max(z[cidx], axis=0, keepdims=True))
                    alpha.append(jnp.exp(m_old - m_new))
                    pc = jnp.exp(z[cidx] - m_new)
                    l_ref[cidx, :, :nq] = alpha[cidx] * l_ref[cidx, :, :nq] + jnp.sum(pc, axis=0, keepdims=True)
                    m_ref[cidx, :, :nq] = m_new
                    p.append(pc.astype(BF16))
                pv = [_dot(jnp.where(rows_k[h], vt_s[pp, :, keys], jnp.zeros((LANES, BQ), BF16)), p[cidx])
                      for cidx, (pp, h) in enumerate(chains)]
                for cidx in range(nc):
                    acc_ref[cidx, :, :nq] = alpha[cidx] * acc_ref[cidx, :, :nq] + pv[cidx]

            def kblock(j, _):
                tile(j, None)
                return 0

            lax.fori_loop(0, i, kblock, 0)
            tile(i, diag)
            for pp in range(npair):
                acc = [acc_ref[2 * pp + h, :, :nq] for h in range(2)]
                l = [l_ref[2 * pp + h, :, :nq] for h in range(2)]
                out = (acc[0] / l[0] + acc[1] / l[1]).T
                o_ref[_blk(i, nq), _pair_cols(pp)] = out.astype(BF16)
                o32_ref[_blk(i, nq), _pair_cols(pp)] = out
                lse = [m_ref[2 * pp + h, :, :nq] + jnp.log(l[h]) for h in range(2)]
                lse_t = jnp.where(sub == 0, lse[0], jnp.where(sub == 1, lse[1], 0.0))
                lse_ref[_blk(i, nq), _pair_cols(pp)] = lse_t.T
                if nq < BQ:
                    rest = pl.ds(pl.multiple_of(i * BQ + nq, nq), BQ - nq)
                    o_ref[rest, _pair_cols(pp)] = jnp.zeros((BQ - nq, LANES), BF16)
                    o32_ref[rest, _pair_cols(pp)] = jnp.zeros((BQ - nq, LANES), F32)
                    lse_ref[rest, _pair_cols(pp)] = jnp.zeros((BQ - nq, LANES), F32)

        _query_blocks(qblock, Q_LAST)

    return pl.pallas_call(
        body, name=name,
        out_shape=(jax.ShapeDtypeStruct((t, W_ATT), BF16), jax.ShapeDtypeStruct((t, W_ATT), F32),
                   jax.ShapeDtypeStruct((t, W_ATT), F32)),
        grid=(nseq, NH // (2 * npair)), in_specs=[_group_spec(1, npair), SEQ_SPEC, CROW_SPEC], out_specs=(_gheads_spec(npair), _gheads_spec(npair), _gheads_spec(npair)),
        scratch_shapes=_qkv_scratch(npair) + [pltpu.VMEM((npair, LANES, LP), BF16), pltpu.VMEM((2 * npair, LP, 1), F32),
                                      pltpu.VMEM((2 * npair, LANES, BQ), F32), pltpu.VMEM((2 * npair, 1, BQ), F32),
                                      pltpu.VMEM((2 * npair, 1, BQ), F32)],
        compiler_params=_cparams("parallel", "parallel"))(proj, c, crow)


def _fox_bwd(proj, c, crow, o32, lse, do, dproj, nseq, npair, name):
    t = nseq * LP
    chains = _chains(npair)

    def body(p_ref, c_ref, cr_ref, o_ref, lse_ref, do_ref, _, dp_ref, dc_ref,
             q_s, k_s, v_s, kt_s, ck_s, dqa_ref, dka_ref, dva_ref, rsum_ref):
        _load_qkv(p_ref, q_s, k_s, v_s)
        row, col = _tile_iotas()
        lane = _lane_iota()
        heads = _head_masks()
        sub_all = lax.broadcasted_iota(jnp.int32, (LANES, BQ), 0)
        rows_k = (sub_all < HEAD, sub_all >= HEAD)
        group = pl.program_id(1)
        head0 = 2 * npair * group
        nc = len(chains)
        lane_all = lax.broadcasted_iota(jnp.int32, (LP, LANES), 1)
        for pp in range(npair):
            kt_s[pp] = k_s[pp].astype(F32).T.astype(BF16)
        for cidx in range(nc):
            ck_s[cidx] = jnp.sum(jnp.where(lane_all == head0 + cidx, c_ref[...], 0.0), axis=1, keepdims=True)
        dka_ref[...] = jnp.zeros_like(dka_ref)
        dva_ref[...] = jnp.zeros_like(dva_ref)

        @pl.when(group == 0)
        def _():
            dc_ref[...] = jnp.zeros_like(dc_ref)

        def qblock(i, nq):
            diag = (row <= col)[:, :nq]
            sub = sub_all[:, :nq]
            rows_of = [m[:, :nq] for m in rows_k]
            heads_q = [m[:nq] for m in heads]
            queries = _blk(i, nq)
            dqa_ref[...] = jnp.zeros_like(dqa_ref)
            rsum_ref[...] = jnp.zeros_like(rsum_ref)
            qb = [q_s[pp, queries, :] for pp in range(npair)]
            dob = [do_ref[queries, _pair_cols(pp)] for pp in range(npair)]
            qt = [qb[pp].astype(F32).T.astype(BF16) for pp in range(npair)]
            dot = [dob[pp].astype(F32).T for pp in range(npair)]
            prod = [dot[pp] * o_ref[queries, _pair_cols(pp)].T for pp in range(npair)]
            lse_t = [lse_ref[queries, _pair_cols(pp)].T for pp in range(npair)]
            qt_m = [jnp.where(rows_of[h], qt[pp], jnp.zeros_like(qt[pp])) for pp, h in chains]
            dot_m = [jnp.where(rows_of[h], dot[pp], 0.0).astype(BF16) for pp, h in chains]
            q_m = [_only(heads_q[h], qb[pp]) for pp, h in chains]
            do_m = [_only(heads_q[h], dob[pp]) for pp, h in chains]
            cq = [_key_cols(cr_ref, head0 + cidx, i, nq) for cidx in range(nc)]
            lse_i = [lse_t[pp][h:h + 1, :] for pp, h in chains]
            delta = [jnp.sum(jnp.where(rows_of[h], prod[pp], 0.0), axis=0, keepdims=True) for pp, h in chains]

            def tile(j, causal):
                keys = pl.ds(pl.multiple_of(j * BQ, BQ), BQ)
                kj = [k_s[pp, _blk(j), :] for pp in range(npair)]
                vj = [v_s[pp, _blk(j), :] for pp in range(npair)]
                z = [_dot(kj[pp], qt_m[cidx]) + (cq[cidx] - ck_s[cidx, _blk(j), :]) for cidx, (pp, _) in enumerate(chains)]
                if causal is not None:
                    z = [jnp.where(causal, zc, NEG) for zc in z]
                dpv = [_dot(vj[pp], dot_m[cidx]) for cidx, (pp, _) in enumerate(chains)]
                p = [jnp.exp(z[cidx] - lse_i[cidx]) for cidx in range(nc)]
                ds = [p[cidx] * (dpv[cidx] - delta[cidx]) for cidx in range(nc)]
                dsb = [d.astype(BF16) for d in ds]
                dq = [_dot(jnp.where(rows_k[h], kt_s[pp, :, keys], jnp.zeros((LANES, BQ), BF16)), dsb[cidx])
                      for cidx, (pp, h) in enumerate(chains)]
                dk = [_dot(dsb[cidx], q_m[cidx]) for cidx in range(nc)]
                dv = [_dot(p[cidx].astype(BF16), do_m[cidx]) for cidx in range(nc)]
                for pp in range(npair):
                    dqa_ref[pp, :, :nq] += dq[2 * pp] + dq[2 * pp + 1]
                    dka_ref[pp, _blk(j), :] += dk[2 * pp] + dk[2 * pp + 1]
                    dva_ref[pp, _blk(j), :] += dv[2 * pp] + dv[2 * pp + 1]
                col_sums = jnp.zeros((BQ, LANES), F32)
                for cidx in range(nc):
                    col_sums = col_sums + jnp.where(lane == head0 + cidx, jnp.sum(ds[cidx], axis=1, keepdims=True), 0.0)
                    rsum_ref[cidx, :, :nq] += jnp.sum(ds[cidx], axis=0, keepdims=True)
                dc_ref[_blk(j), :] = dc_ref[_blk(j), :] - col_sums

            def kblock(j, _):
                tile(j, None)
                return 0

            lax.fori_loop(0, i, kblock, 0)
            tile(i, diag)
            row_sums = jnp.zeros((LANES, nq), F32)
            for cidx in range(nc):
                row_sums = row_sums + jnp.where(sub == head0 + cidx, rsum_ref[cidx, :, :nq], 0.0)
            dc_ref[queries, :] += row_sums.T
            for pp in range(npair):
                dp_ref[_blk(i), pp * PAIR_W:pp * PAIR_W + LANES] = (dqa_ref[pp].T * SCALE).astype(BF16)

        _query_blocks(qblock, Q_LAST)
        for pp in range(npair):
            dp_ref[:, pp * PAIR_W + LANES:pp * PAIR_W + 2 * LANES] = dka_ref[pp].astype(BF16)
            dp_ref[:, pp * PAIR_W + 2 * LANES:pp * PAIR_W + 3 * LANES] = dva_ref[pp].astype(BF16)

    return pl.pallas_call(
        body, name=name,
        out_shape=(jax.ShapeDtypeStruct(dproj.shape, BF16), jax.ShapeDtypeStruct((t, LANES), F32)),
        grid=(nseq, NH // (2 * npair)),
        in_specs=[_group_spec(1, npair), SEQ_SPEC, CROW_SPEC, _gheads_spec(npair), _gheads_spec(npair), _gheads_spec(npair), ANY],
        out_specs=(_group_spec(1, npair), SEQ_SPEC),
        input_output_aliases={6: 0},
        scratch_shapes=_qkv_scratch(npair) + [pltpu.VMEM((npair, LANES, LP), BF16), pltpu.VMEM((2 * npair, LP, 1), F32),
                                      pltpu.VMEM((npair, LANES, BQ), F32), pltpu.VMEM((npair, LP, LANES), F32),
                                      pltpu.VMEM((npair, LP, LANES), F32), pltpu.VMEM((2 * npair, 1, BQ), F32)],
        compiler_params=_cparams("parallel", "arbitrary"))(proj, c, crow, o32, lse, do, dproj)


def _adamw_math(w, g, m, v):
    m = B1 * m + (1.0 - B1) * g
    v = B2 * v + (1.0 - B2) * (g * g)
    m_hat = m / (1.0 - B1 ** STEP)
    v_hat = v / (1.0 - B2 ** STEP)
    delta = -LR * (m_hat / (jnp.sqrt(v_hat) + EPS) + WD * w)
    return delta, m, v


def _sum_adamw(parts, w, m, v, tr, name):
    rows, cols = w.shape
    cp = parts.shape[2]
    assert rows % tr == 0 and parts.shape[1] == rows

    def body(p_ref, w_ref, m_ref, v_ref, g_ref, d_ref, nm_ref, nv_ref):
        gsum = p_ref[0].astype(F32)
        for s in range(1, N_DEV):
            gsum = gsum + p_ref[s].astype(F32)
        gsum = gsum[:, :cols]
        d, nm, nv = _adamw_math(w_ref[...], gsum, m_ref[...], v_ref[...])
        g_ref[...] = gsum
        d_ref[...] = d
        nm_ref[...] = nm
        nv_ref[...] = nv

    blk = pl.BlockSpec((tr, cols), lambda i: (i, 0))
    out = jax.ShapeDtypeStruct((rows, cols), F32)
    return pl.pallas_call(
        body, name=name, out_shape=(out, out, out, out), grid=(rows // tr,),
        in_specs=[pl.BlockSpec((N_DEV, tr, cp), lambda i: (0, i, 0)), blk, blk, blk],
        out_specs=(blk, blk, blk, blk), compiler_params=_cparams("parallel"))(parts, w, m, v)


def _local_step(x, tgt, meta, g_mix, b_forget, g_ffn, g_final, first_weights, late_weights, early_grads, last_grad):
    nseq = x.shape[0]
    t = nseq * LP
    tm = LP // 2
    mm = functools.partial(_matmul, tm=tm)

    h0 = _pad_rows(meta, x, nseq, "pad_x").reshape(t, D)
    bf = jnp.pad(b_forget.reshape(1, NH), ((0, 0), (0, LANES - NH)))

    n1 = _norm_fwd(h0, g_mix, "norm1")
    w_in_p, started = first_weights(n1)
    proj = mm(n1, w_in_p, out_dtype=F32, tn=1792, tk=D, after=started, name="in_proj")
    c = _gate_fwd(proj, bf, nseq, "gate_fwd")
    crow = c[:, :NH].reshape(nseq, LP, NH).transpose(0, 2, 1)
    o_sb, rs = _sb_fwd(proj, nseq, 2, "sb_fwd")
    o_fx, o_fx32, lse = _fox_fwd(proj, c, crow, nseq, 2, "fox_fwd")
    w_bsb, w_bfx, w_out, w_up_i, cw_i, w_down = late_weights(o_fx)
    p_sb = mm(o_sb, w_bsb, out_dtype=F32, tn=D, tk=W_ATT, name="branch_sb")
    p_fx = mm(o_fx, w_bfx, out_dtype=F32, tn=D, tk=W_ATT, name="branch_fox")
    merged = _merge_fwd(p_sb, p_fx, proj, "merge_fwd")
    rows = functools.partial(_matmul_rows, tm=LP // 4)
    h1, n2 = rows(merged, w_out, [h0], [g_ffn], _residual_norm, [F32, BF16], [], tk=D, name="out_proj_norm2")
    u = mm(n2, w_up_i, out_dtype=F32, tn=1408, tk=D, name="up_proj")
    act = _conv_glu_fwd(u, cw_i, nseq, "conv_glu_fwd")

    dh2, dh2b, loss, dg_final = rows(act, w_down, [h1], [g_final], _loss_head, [F32, BF16],
                                     [(8, LANES), (1, D)], tk=D_FF, unpadded_in=tgt, name="down_proj_loss")
    d_down = _matmul(act, dh2b, out_dtype=BF16, tm=1408, tn=D, tk=LP, ta=True, name="d_w_down")
    dact = mm(dh2b, w_down, out_dtype=F32, tn=1408, tk=D, tb=True, name="d_act")
    du, d_cw = _conv_glu_bwd(u, cw_i, dact, nseq, "conv_glu_bwd")
    d_up = _matmul(n2, du, out_dtype=BF16, tm=D, tn=1408, tk=LP, ta=True, name="d_w_up")
    dh1, dh1b, dg_ffn = rows(du, w_up_i, [h1, dh2], [g_ffn], _residual_norm_bwd, [F32, BF16], [(1, D)],
                             tk=D_FF, tb=True, name="d_n2_norm2_bwd")
    d_out = _matmul(merged, dh1b, out_dtype=BF16, tm=D, tn=D, tk=LP, ta=True, name="d_w_out")
    dmerged = mm(dh1b, w_out, out_dtype=F32, tn=D, tk=D, tb=True, name="d_merged")
    dp_sb, dproj = _merge_bwd(dmerged, p_sb, proj, None, 0, "merge_bwd_sb")
    dp_fx, dproj = _merge_bwd(dmerged, p_fx, proj, dproj, 1, "merge_bwd_fox")
    d_bsb = _matmul(o_sb, dp_sb, out_dtype=BF16, tm=W_ATT, tn=D, tk=LP, ta=True, name="d_w_branch_sb")
    d_bfx = _matmul(o_fx, dp_fx, out_dtype=BF16, tm=W_ATT, tn=D, tk=LP, ta=True, name="d_w_branch_fox")
    do_sb = mm(dp_sb, w_bsb, out_dtype=BF16, tn=W_ATT, tk=D, tb=True, name="d_o_sb")
    do_fx = mm(dp_fx, w_bfx, out_dtype=BF16, tn=W_ATT, tk=D, tb=True, name="d_o_fox")
    sent = early_grads(dict(w_branch_sb=d_bsb, w_branch_fox=d_bfx, w_out=d_out, w_up=d_up, conv_w=d_cw, w_down=d_down))
    dproj = _sb_bwd(proj, do_sb, rs, dproj, sent, nseq, 2, "sb_bwd")
    dproj, dc = _fox_bwd(proj, c, crow, o_fx32, lse, do_fx, dproj, nseq, 2, "fox_bwd")
    dproj, d_bf = _gate_bwd(proj, bf, dc, dproj, nseq, "gate_bwd")
    d_in = _matmul(n1, dproj, out_dtype=BF16, tm=D, tn=1792, tk=LP, ta=True, name="d_w_in")
    grad_x, d_front, dg_mix = rows(dproj, w_in_p, [h0, dh1], [g_mix], _residual_norm_bwd_f32, [F32], [(1, D)],
                                   tk=IN_P // 2, tb=True, after=last_grad(d_in), unpadded_out=True, name="d_n1_norm1_bwd")
    grads = dict(meta_tokens=jnp.sum(d_front, axis=0), norm_mix_g=dg_mix, b_forget=d_bf[:, :NH],
                 norm_ffn_g=dg_ffn, norm_final_g=dg_final)
    return loss[0, 0], grad_x, grads


REPL = (("norm_mix_g", D), ("norm_ffn_g", D), ("norm_final_g", D), ("b_forget", LANES))
REPL_ROWS = 32
META_ROWS = N_META * D // LANES


def _pack_repl(tree):
    rows = [jnp.pad(tree[name].reshape(-1), (0, n - tree[name].size)).reshape(-1, LANES) for name, n in REPL]
    packed = jnp.concatenate(rows, axis=0)
    return jnp.pad(packed, ((0, REPL_ROWS - packed.shape[0]), (0, 0)))


def _unpack_repl(packed, shapes):
    out, r = {}, 0
    for name, n in REPL:
        size = 1
        for s in shapes[name]:
            size *= s
        out[name] = packed[r:r + n // LANES].reshape(-1)[:size].reshape(shapes[name])
        r += n // LANES
    return out


def kernel(x, meta_tokens, norm_mix_g, w_in, b_forget, w_branch_sb, w_branch_fox, w_out, norm_ffn_g, w_up, conv_w, w_down, norm_final_g, loss_target, m_meta_tokens, m_norm_mix_g, m_w_in, m_b_forget, m_w_branch_sb, m_w_branch_fox, m_w_out, m_norm_ffn_g, m_w_up, m_conv_w, m_w_down, m_norm_final_g, v_meta_tokens, v_norm_mix_g, v_w_in, v_b_forget, v_w_branch_sb, v_w_branch_fox, v_w_out, v_norm_ffn_g, v_w_up, v_conv_w, v_w_down, v_norm_final_g):
    w = dict(meta_tokens=meta_tokens, norm_mix_g=norm_mix_g, w_in=w_in, b_forget=b_forget, w_branch_sb=w_branch_sb,
             w_branch_fox=w_branch_fox, w_out=w_out, norm_ffn_g=norm_ffn_g, w_up=w_up, conv_w=conv_w, w_down=w_down,
             norm_final_g=norm_final_g)
    m = dict(meta_tokens=m_meta_tokens, norm_mix_g=m_norm_mix_g, w_in=m_w_in, b_forget=m_b_forget,
             w_branch_sb=m_w_branch_sb, w_branch_fox=m_w_branch_fox, w_out=m_w_out, norm_ffn_g=m_norm_ffn_g,
             w_up=m_w_up, conv_w=m_conv_w, w_down=m_w_down, norm_final_g=m_norm_final_g)
    v = dict(meta_tokens=v_meta_tokens, norm_mix_g=v_norm_mix_g, w_in=v_w_in, b_forget=v_b_forget,
             w_branch_sb=v_w_branch_sb, w_branch_fox=v_w_branch_fox, w_out=v_w_out, norm_ffn_g=v_norm_ffn_g,
             w_up=v_w_up, conv_w=v_conv_w, w_down=v_w_down, norm_final_g=v_norm_final_g)
    shapes = {k: a.shape for k, a in w.items()}
    sharded = ("w_in", "w_branch_sb", "w_branch_fox", "w_out", "w_up", "w_down", "conv_w", "meta_tokens")
    mat = lambda tree, name: tree[name].reshape(tree[name].shape[-2:])

    def lane_pad(a, width):
        return jnp.pad(a, ((0, 0), (0, width - a.shape[1])))

    late = ("w_branch_sb", "w_branch_fox", "w_out", "w_up", "w_down", "conv_w")
    pending_w = {}
    g_meta, = _all_gather([mat(w, "meta_tokens")], "gather_meta")
    pending_w["in"], in_started = _remote_start(
        [lane_pad(mat(w, "w_in").astype(BF16), SHARD_P)], False, g_meta, "gather_w_in_start")
    meta_full = g_meta.transpose(1, 0, 2).reshape(N_META, D) + in_started[0, 0]

    def first_weights(after):
        g_in, = _remote_wait(pending_w["in"], after, "gather_w_in_wait")
        pending_w["late"], started = _remote_start(
            [mat(w, "w_branch_sb").astype(BF16), mat(w, "w_branch_fox").astype(BF16), mat(w, "w_out").astype(BF16),
             lane_pad(mat(w, "w_up").astype(BF16), SHARD_P), mat(w, "w_down").astype(BF16), mat(w, "conv_w")],
            False, g_in, "gather_late_start")
        w_in_p = _relayout(g_in, 1, IN_P, _gathered_to_full(IN_SHARD, _in_padded_to_orig), BF16, 256, "w_in_cols")[0]
        return w_in_p, started

    def late_weights(after):
        g_bsb, g_bfx, g_out, g_up, g_down, g_cw = _remote_wait(pending_w["late"], after, "gather_late_wait")
        w_up_i = _relayout(g_up, 1, 2 * D_FF, _gathered_to_full(UP_SHARD, _up_inter_to_orig), BF16, 256, "w_up_cols")[0]
        w_bsb = _relayout(g_bsb, 1, D, _gathered_to_full(ATT_SHARD, lambda d: d), BF16, 256, "w_bsb_cols")[0]
        w_bfx = _relayout(g_bfx, 1, D, _gathered_to_full(ATT_SHARD, lambda d: d), BF16, 256, "w_bfx_cols")[0]
        cw_full = g_cw.transpose(1, 0, 2).reshape(3, 2 * D_FF)
        cw_i = cw_full.reshape(3, 2, D_FF // FFC, FFC).transpose(0, 2, 1, 3).reshape(3, 2 * D_FF)
        return w_bsb, w_bfx, g_out.reshape(D, D), w_up_i, cw_i, g_down.reshape(D_FF, D)

    pending_g = {}

    def early_grads(g):
        d_cw = g["conv_w"].reshape(3, D_FF // FFC, 2, FFC).transpose(0, 2, 1, 3).reshape(3, 2 * D_FF)
        pending_g["early"], sent = _remote_start(
            [_relayout(g["w_branch_sb"][None], N_DEV, ATT_SHARD, _full_to_shards(ATT_SHARD, lambda c: c), BF16, 256, "d_w_bsb_shards"),
             _relayout(g["w_branch_fox"][None], N_DEV, ATT_SHARD, _full_to_shards(ATT_SHARD, lambda c: c), BF16, 256, "d_w_bfx_shards"),
             g["w_out"].reshape(N_DEV, D // N_DEV, D),
             _relayout(g["w_up"][None], N_DEV, SHARD_P, _full_to_shards(UP_SHARD, _UP_ORIG_TO_INTER.get), BF16, 256, "d_w_up_shards"),
             g["w_down"].reshape(N_DEV, D_FF // N_DEV, D),
             d_cw.reshape(3, N_DEV, UP_SHARD).transpose(1, 0, 2)], True, g["w_out"], "exchange_early_start")
        return sent

    def last_grad(d_in):
        shards = _relayout(d_in[None], N_DEV, SHARD_P, _full_to_shards(IN_SHARD, _IN_ORIG_TO_PADDED.get), BF16, 256, "d_w_in_shards")
        pending_g["last"], sent = _remote_start([shards], True, shards, "exchange_last_start")
        return sent

    loss, grad_x, grads = _local_step(
        x, loss_target, meta_full, norm_mix_g.reshape(1, D), b_forget,
        norm_ffn_g.reshape(1, D), norm_final_g.reshape(1, D), first_weights, late_weights, early_grads, last_grad)

    small = jnp.concatenate([_pack_repl(grads), grads["meta_tokens"].reshape(META_ROWS, LANES)], axis=0)
    small, = _all_gather([small], "gather_small_grads")
    me_idx = 4 * lax.axis_index("x") + 2 * lax.axis_index("y") + lax.axis_index("c")
    p_meta = lax.dynamic_slice_in_dim(small[:, REPL_ROWS:].reshape(N_DEV, N_META, D), me_idx * ATT_SHARD, ATT_SHARD, axis=2)

    p_in, = _remote_wait(pending_g["last"], small, "exchange_last_wait")
    parts = dict(zip(late, _remote_wait(pending_g["early"], p_in, "exchange_early_wait")), w_in=p_in, meta_tokens=p_meta)
    tiles = dict(w_in=256, w_branch_sb=256, w_branch_fox=256, w_out=D // N_DEV, w_up=256, w_down=D_FF // N_DEV,
                 conv_w=3, meta_tokens=N_META)
    new = {name: _sum_adamw(parts[name], mat(w, name), mat(m, name), mat(v, name), tiles[name], "adamw_" + name)
           for name in sharded}

    routs = _sum_adamw(small[:, :REPL_ROWS], _pack_repl(w), _pack_repl(m), _pack_repl(v), REPL_ROWS, "adamw_replicated")
    repl = [_unpack_repl(o, shapes) for o in routs]

    result = [lax.psum(loss, ("x", "y", "c")), grad_x]
    for k in range(4):
        for name in w:
            result.append(new[name][k].reshape(shapes[name]) if name in new else repl[k][name])
    return tuple(result)
```

```python
import functools

import jax
import jax.numpy as jnp
from jax import lax
from jax.experimental import pallas as pl
from jax.experimental.pallas import tpu as pltpu

F32 = jnp.float32
BF16 = jnp.bfloat16

N_DEV = 8
LANES = 128
D = 1024
N_META = 16
SEQ = 2048
L_REAL = N_META + SEQ
LP = 2304
BQ = 256
NBLK = LP // BQ
HEAD = 64
NH = 8
W_ATT = NH * HEAD
PAIR_W = 3 * LANES
D_FF = 2816
IN_COLS = 5128
QKV = 6 * W_ATT
IN_P = 5376
GATE_COL = QKV
F_COL = QKV + 2 * D
FFC = 256
RMS_EPS = 1e-6
LR, B1, B2, EPS, WD, STEP = 0.001, 0.9, 0.999, 1e-08, 0.01, 10
VMEM_LIMIT = 56 * 1024 * 1024

MESH = pl.DeviceIdType.MESH
ANY = pl.BlockSpec(memory_space=pl.ANY)


def _cparams(*sem):
    return pltpu.CompilerParams(dimension_semantics=sem if sem else None, vmem_limit_bytes=VMEM_LIMIT)


def _all_gather(xs, name):
    n = len(xs)

    def body(*refs):
        x_refs, out_refs = refs[:n], refs[n:2 * n]
        send_sems, recv_sems, local_sems = refs[2 * n:]
        mx, my, mc = lax.axis_index("x"), lax.axis_index("y"), lax.axis_index("c")
        me, sibling = (mx, my, mc), (mx, my, 1 - mc)
        chips = [(1 - mx, my), (mx, 1 - my), (1 - mx, 1 - my)]

        def copy(a, k, block, to, own=False):
            px, py, pc = block
            slot = out_refs[a].at[4 * px + 2 * py + pc]
            return pltpu.make_async_remote_copy(
                src_ref=x_refs[a] if own else slot, dst_ref=slot,
                send_sem=send_sems.at[7 * a + k], recv_sem=recv_sems.at[7 * a + k],
                device_id=to, device_id_type=MESH)

        mine = [pltpu.make_async_copy(x_refs[a], out_refs[a].at[4 * mx + 2 * my + mc], local_sems.at[a]) for a in range(n)]
        for cp in mine:
            cp.start()
        first = []
        for a in range(n):
            first.append(copy(a, 0, me, sibling, own=True))
            first += [copy(a, 1 + j, me, (*chip, mc), own=True) for j, chip in enumerate(chips)]
        for cp in first:
            cp.start()
        passed = []
        for j, chip in enumerate(chips):
            for a in range(n):
                copy(a, 1 + j, (*chip, mc), me).wait_recv()
                fwd = copy(a, 4 + j, (*chip, mc), sibling)
                fwd.start()
                passed.append(fwd)
        for a in range(n):
            copy(a, 0, sibling, me).wait_recv()
            for j, chip in enumerate(chips):
                copy(a, 4 + j, (*chip, 1 - mc), me).wait_recv()
        for cp in first + passed:
            cp.wait_send()
        for cp in mine:
            cp.wait()

    return pl.pallas_call(
        body, name=name,
        out_shape=tuple(jax.ShapeDtypeStruct((N_DEV,) + x.shape, x.dtype) for x in xs),
        in_specs=[ANY] * n, out_specs=tuple([ANY] * n),
        scratch_shapes=[pltpu.SemaphoreType.DMA((7 * n,)), pltpu.SemaphoreType.DMA((7 * n,)),
                        pltpu.SemaphoreType.DMA((n,))],
    )(*xs)


HBM = pl.BlockSpec(memory_space=pltpu.HBM)
SEM = pl.BlockSpec(memory_space=pltpu.SEMAPHORE)
EFFECT = pltpu.SideEffectType.DATAFLOW_SIDE_EFFECTING


def _peer_copies(src_refs, land_refs, send_sems, recv_sems, per_peer):
    mx, my, mc = lax.axis_index("x"), lax.axis_index("y"), lax.axis_index("c")
    me_idx = 4 * mx + 2 * my + mc
    copies = []
    for k in range(1, N_DEV):
        px, py, pc = mx ^ (k >> 2), my ^ ((k >> 1) & 1), mc ^ (k & 1)
        for a, (src, land) in enumerate(zip(src_refs, land_refs)):
            copies.append(pltpu.make_async_remote_copy(
                src_ref=src.at[4 * px + 2 * py + pc] if per_peer else src, dst_ref=land.at[me_idx],
                send_sem=send_sems.at[7 * a + k - 1], recv_sem=recv_sems.at[7 * a + k - 1],
                device_id=(px, py, pc), device_id_type=MESH))
    return me_idx, copies


def _remote_start(srcs, per_peer, after, name):
    n = len(srcs)
    lands = [lax.empty(s.shape if per_peer else (N_DEV,) + s.shape, s.dtype) for s in srcs]

    def body(*refs):
        src_refs, land_refs = refs[:n], refs[n:2 * n]
        send_sems, recv_sems = refs[2 * n + 1:2 * n + 3]
        token = refs[4 * n + 3]
        stage, local_sems = refs[4 * n + 4:5 * n + 4], refs[5 * n + 4]
        me_idx, copies = _peer_copies(src_refs, land_refs, send_sems, recv_sems, per_peer)
        for cp in copies:
            cp.start()
        own = [src_refs[a].at[me_idx] if per_peer else src_refs[a] for a in range(n)]
        for hop in ([(own[a], stage[a]) for a in range(n)], [(stage[a], land_refs[a].at[me_idx]) for a in range(n)]):
            cps = [pltpu.make_async_copy(s, d, local_sems.at[a]) for a, (s, d) in enumerate(hop)]
            for cp in cps:
                cp.start()
            for cp in cps:
                cp.wait()
        token[...] = jnp.zeros_like(token)

    thru = [pltpu.HBM(a.shape, a.dtype) for a in list(srcs) + lands]
    out = pl.pallas_call(
        body, name=name,
        out_shape=(pltpu.SemaphoreType.DMA((7 * n,)), pltpu.SemaphoreType.DMA((7 * n,)), *thru,
                   jax.ShapeDtypeStruct((8, LANES), F32)),
        in_specs=[HBM] * (2 * n) + [ANY],
        out_specs=(SEM, SEM, *([HBM] * (2 * n)), pl.BlockSpec(memory_space=pltpu.VMEM)),
        input_output_aliases={i: 2 + i for i in range(2 * n)},
        scratch_shapes=[pltpu.VMEM(s.shape[1:] if per_peer else s.shape, s.dtype) for s in srcs]
        + [pltpu.SemaphoreType.DMA((n,))],
        compiler_params=pltpu.CompilerParams(has_side_effects=EFFECT),
    )(*[pltpu.with_memory_space_constraint(a, pltpu.HBM) for a in list(srcs) + lands], after)
    return dict(sems=out[:2], bufs=out[2:2 * n + 2], per_peer=per_peer), out[-1]


def _remote_wait(pending, after, name):
    bufs = pending["bufs"]
    n = len(bufs) // 2
    per_peer = pending["per_peer"]

    def body(*refs):
        src_refs, land_refs = refs[:n], refs[n:2 * n]
        send_sems, recv_sems = refs[2 * n:2 * n + 2]
        _, copies = _peer_copies(src_refs, land_refs, send_sems, recv_sems, per_peer)
        for cp in copies:
            cp.wait_send()
        for cp in copies:
            cp.wait_recv()

    out = pl.pallas_call(
        body, name=name, out_shape=tuple(pltpu.HBM(a.shape, a.dtype) for a in bufs),
        in_specs=[HBM] * (2 * n) + [SEM, SEM, ANY], out_specs=tuple([HBM] * (2 * n)),
        input_output_aliases={i: i for i in range(2 * n)},
        compiler_params=pltpu.CompilerParams(has_side_effects=EFFECT),
    )(*bufs, *pending["sems"], after)
    return out[n:]


ROWS_PER_COPY = 512


def _pad_rows(front, body_rows, nseq, name):
    tail = LP - L_REAL
    nblk = SEQ // ROWS_PER_COPY

    def body(f_ref, b_ref, o_ref, z_ref, sems):
        s, i = pl.program_id(0), pl.program_id(1)
        rows = pltpu.make_async_copy(b_ref, o_ref.at[pl.ds(s, 1), pl.ds(N_META + i * ROWS_PER_COPY, ROWS_PER_COPY)], sems.at[0])
        rows.start()

        @pl.when(i == 0)
        def _():
            z_ref[...] = jnp.zeros_like(z_ref)
            head = pltpu.make_async_copy(f_ref, o_ref.at[s, pl.ds(0, N_META)], sems.at[1])
            zeros = pltpu.make_async_copy(z_ref, o_ref.at[s, pl.ds(L_REAL, tail)], sems.at[2])
            head.start()
            zeros.start()
            head.wait()
            zeros.wait()

        rows.wait()

    return pl.pallas_call(
        body, name=name, out_shape=jax.ShapeDtypeStruct((nseq, LP, D), F32), grid=(nseq, nblk),
        in_specs=[pl.BlockSpec((N_META, D), lambda s, i: (0, 0)), pl.BlockSpec((1, ROWS_PER_COPY, D), lambda s, i: (s, i, 0))],
        out_specs=ANY,
        scratch_shapes=[pltpu.VMEM((tail, D), F32), pltpu.SemaphoreType.DMA((3,))],
        compiler_params=_cparams("arbitrary", "arbitrary"))(front, body_rows)


def _plan_cols(n_q, n_dcols, src_of):
    plan = {}
    for q in range(n_q):
        for dblk in range(n_dcols // LANES):
            segs, key, start = [], None, 0
            for lane in range(LANES + 1):
                new = None
                if lane < LANES:
                    src = src_of(q, dblk * LANES + lane)
                    if src is not None:
                        new = (src[0], src[1] // LANES, (lane - src[1] % LANES) % LANES)
                if new != key:
                    if key is not None:
                        segs.append((*key, start, lane))
                    key, start = new, lane
            plan[(q, dblk)] = segs
    return plan


def _relayout(src, n_q, n_dcols, src_of, out_dtype, tr, name):
    n_p, rows, scols = src.shape
    plan = _plan_cols(n_q, n_dcols, src_of)

    def body(s_ref, d_ref):
        lane = lax.broadcasted_iota(jnp.int32, (tr, LANES), 1)
        for (q, dblk), segs in plan.items():
            acc = jnp.zeros((tr, LANES), F32)
            for p, sblk, rot, lo, hi in segs:
                x = s_ref[p, :, sblk * LANES:(sblk + 1) * LANES].astype(F32)
                if rot:
                    x = pltpu.roll(x, rot, 1)
                acc = x if (lo, hi) == (0, LANES) else jnp.where((lane >= lo) & (lane < hi), x, acc)
            d_ref[q, :, dblk * LANES:(dblk + 1) * LANES] = acc.astype(out_dtype)

    return pl.pallas_call(
        body, name=name, out_shape=jax.ShapeDtypeStruct((n_q, rows, n_dcols), out_dtype), grid=(rows // tr,),
        in_specs=[pl.BlockSpec((n_p, tr, scols), lambda i: (0, i, 0))],
        out_specs=pl.BlockSpec((n_q, tr, n_dcols), lambda i: (0, i, 0)),
        compiler_params=_cparams("parallel"))(src)


def _in_padded_to_orig(d):
    if d < QKV:
        kind, r = divmod(d, 4 * PAIR_W)
        pair, r = divmod(r, PAIR_W)
        part, r = divmod(r, LANES)
        return kind * 3 * W_ATT + part * W_ATT + pair * LANES + r
    if d < F_COL:
        return d + NH
    if d < F_COL + NH:
        return d - 2 * D
    return None


_IN_ORIG_TO_PADDED = {_in_padded_to_orig(d): d for d in range(IN_P) if _in_padded_to_orig(d) is not None}


def _up_inter_to_orig(d):
    j, r = divmod(d, 2 * FFC)
    part, r = divmod(r, FFC)
    return part * D_FF + j * FFC + r


_UP_ORIG_TO_INTER = {_up_inter_to_orig(d): d for d in range(2 * D_FF)}
IN_SHARD = IN_COLS // N_DEV
UP_SHARD = 2 * D_FF // N_DEV
SHARD_P = 768
ATT_SHARD = D // N_DEV


def _gathered_to_full(n_shard, to_orig):
    def src_of(q, d):
        c = to_orig(d)
        return None if c is None else (c // n_shard, c % n_shard)
    return src_of


def _full_to_shards(n_shard, from_orig):
    def src_of(q, d):
        return (0, from_orig(q * n_shard + d)) if d < n_shard else None
    return src_of


def _matmul(a, b, *, out_dtype, tm, tn, tk, ta=False, tb=False, after=None, name):
    if ta:
        kdim, m = a.shape
    else:
        m, kdim = a.shape
    n = b.shape[0] if tb else b.shape[1]
    assert m % tm == 0 and n % tn == 0 and kdim % tk == 0, (name, a.shape, b.shape, tm, tn, tk)
    nk = kdim // tk

    def body(a_ref, b_ref, *rest):
        o_ref, scratch = rest[len(extra)], rest[len(extra) + 1:]
        av, bv = a_ref[...], b_ref[...]
        if ta:
            p = lax.dot_general(av, bv, (((0,), (0,)), ((), ())), preferred_element_type=F32)
        elif tb:
            p = lax.dot_general(av, bv, (((1,), (1,)), ((), ())), preferred_element_type=F32)
        else:
            p = jnp.dot(av, bv, preferred_element_type=F32)
        if nk == 1:
            o_ref[...] = p.astype(o_ref.dtype)
        else:
            acc_ref, = scratch
            k = pl.program_id(2)

            @pl.when(k == 0)
            def _():
                acc_ref[...] = p

            @pl.when(k > 0)
            def _():
                acc_ref[...] += p

            @pl.when(k == nk - 1)
            def _():
                o_ref[...] = acc_ref[...].astype(o_ref.dtype)

    extra = [] if after is None else [after]
    a_spec = pl.BlockSpec((tk, tm), lambda i, j, k: (k, i)) if ta else pl.BlockSpec((tm, tk), lambda i, j, k: (i, k))
    b_spec = pl.BlockSpec((tn, tk), lambda i, j, k: (j, k)) if tb else pl.BlockSpec((tk, tn), lambda i, j, k: (k, j))
    return pl.pallas_call(
        body, name=name,
        out_shape=jax.ShapeDtypeStruct((m, n), out_dtype),
        grid=(m // tm, n // tn, nk),
        in_specs=[a_spec, b_spec] + [ANY] * len(extra),
        out_specs=pl.BlockSpec((tm, tn), lambda i, j, k: (i, j)),
        scratch_shapes=[] if nk == 1 else [pltpu.VMEM((tm, tn), F32)],
        compiler_params=_cparams("parallel", "parallel", "arbitrary"),
    )(a, b, *extra)


TR = 576


def _rms(h):
    return lax.rsqrt(jnp.mean(h * h, axis=-1, keepdims=True) + RMS_EPS)


def _norm_fwd(h, g, name):
    t = h.shape[0]
    row = pl.BlockSpec((TR, D), lambda i: (i, 0))

    def body(h_ref, g_ref, n_ref):
        hv = h_ref[...]
        n_ref[...] = ((hv * _rms(hv)) * g_ref[...]).astype(BF16)

    return pl.pallas_call(
        body, name=name, out_shape=jax.ShapeDtypeStruct((t, D), BF16), grid=(t // TR,),
        in_specs=[row, pl.BlockSpec((1, D), lambda i: (0, 0))], out_specs=row, compiler_params=_cparams("parallel"))(h, g)


EPI_ROWS = 288


def _matmul_rows(a, b, rows_in, vecs_in, epilogue, row_outs, sum_outs, *, tm, tk, tb=False, after=None,
                 unpadded_in=None, unpadded_out=False, name):
    m, kdim = a.shape
    assert (b.shape[0] if tb else b.shape[1]) == D and m % tm == 0 and kdim % tk == 0 and tm % EPI_ROWS == 0
    nk = kdim // tk
    n_r, n_v, n_ro, n_so = len(rows_in), len(vecs_in), len(row_outs), len(sum_outs)
    extra = ([] if after is None else [after]) + ([] if unpadded_in is None else [unpadded_in])
    spans = []
    for q in range(LP // tm):
        lo, hi = max(q * tm, N_META), min((q + 1) * tm, L_REAL)
        spans.append((lo - N_META, hi - lo, lo - q * tm))

    def body(a_ref, b_ref, *rest):
        r_refs, v_refs = rest[:n_r], rest[n_r:n_r + n_v]
        outs = rest[n_r + n_v + len(extra):]
        ro_refs, so_refs, acc_ref = outs[:n_ro], outs[n_ro:n_ro + n_so], outs[n_ro + n_so]
        i, k = pl.program_id(0), pl.program_id(1)

        def unpadded_copies(do):
            src, tile, sem = rest[n_r + n_v + len(extra) - 1], outs[n_ro + n_so + 1], outs[n_ro + n_so + 2]
            for q, (src0, n, dst0) in enumerate(spans):
                @pl.when(i % len(spans) == q)
                def _():
                    do(pltpu.make_async_copy(src.at[i // len(spans), pl.ds(src0, n)], tile.at[pl.ds(dst0, n)], sem),
                       tile, dst0, n)

        def start(cp, tile, dst0, n):
            if dst0:
                tile[pl.ds(0, dst0), :] = jnp.zeros((dst0, D), F32)
            if dst0 + n < tm:
                tile[pl.ds(dst0 + n, tm - dst0 - n), :] = jnp.zeros((tm - dst0 - n, D), F32)
            cp.start()

        if unpadded_in is not None:
            @pl.when(k == 0)
            def _():
                unpadded_copies(start)

        def out_copies(tile_idx, do):
            stage, sems = outs[-2], outs[-1]
            seq = tile_idx // len(spans)
            for q, (dst0, n, src0) in enumerate(spans):
                @pl.when(tile_idx % len(spans) == q)
                def _():
                    do(pltpu.make_async_copy(stage.at[pl.ds(src0, n)], ro_refs[0].at[seq, pl.ds(dst0, n)], sems.at[0]))
                    if q == 0:
                        do(pltpu.make_async_copy(stage.at[pl.ds(0, N_META)], ro_refs[1].at[seq], sems.at[1]))
        if tb:
            p = lax.dot_general(a_ref[...], b_ref[...], (((1,), (1,)), ((), ())), preferred_element_type=F32)
        else:
            p = jnp.dot(a_ref[...], b_ref[...], preferred_element_type=F32)

        @pl.when(k == 0)
        def _():
            acc_ref[...] = p

        @pl.when(k > 0)
        def _():
            acc_ref[...] += p

        @pl.when(k == nk - 1)
        def _():
            vecs = [v[...] for v in v_refs]
            tiles_in = list(r_refs)
            if unpadded_in is not None:
                unpadded_copies(lambda cp, *_: cp.wait())
                tiles_in.append(outs[n_ro + n_so + 1])
            tile_outs = list(ro_refs)
            if unpadded_out:
                tile_outs = [outs[-2]] + tile_outs[2:]

                @pl.when(i > 0)
                def _():
                    out_copies(i - 1, lambda cp: cp.wait())

            def step(c, sums):
                rows = pl.ds(pl.multiple_of(c * EPI_ROWS, 8), EPI_ROWS)
                tiles, terms = epilogue(i * tm + c * EPI_ROWS, acc_ref[rows, :], *[r[rows, :] for r in tiles_in], *vecs)
                for o, tile in zip(tile_outs, tiles):
                    o[rows, :] = tile.astype(o.dtype)
                return tuple(s + term for s, term in zip(sums, terms))

            sums = lax.fori_loop(0, tm // EPI_ROWS, step, tuple(jnp.zeros(s, F32) for s in sum_outs))
            if unpadded_out:
                out_copies(i, lambda cp: cp.start())

                @pl.when(i == m // tm - 1)
                def _():
                    out_copies(i, lambda cp: cp.wait())

            @pl.when(i == 0)
            def _():
                for o in so_refs:
                    o[...] = jnp.zeros_like(o)

            for o, s in zip(so_refs, sums):
                o[...] += s

    row = pl.BlockSpec((tm, D), lambda i, k: (i, 0))
    row_in_specs = [pl.BlockSpec((tm, D), lambda i, k, c=r[1]: (i, c)) if isinstance(r, tuple) else row for r in rows_in]
    rows_in = [r[0] if isinstance(r, tuple) else r for r in rows_in]
    b_spec = pl.BlockSpec((D, tk), lambda i, k: (0, k)) if tb else pl.BlockSpec((tk, D), lambda i, k: (k, 0))
    row_shapes = [jax.ShapeDtypeStruct((m, D), dt) for dt in row_outs]
    row_specs = [row] * len(row_outs)
    if unpadded_out:
        nseq = m // LP
        row_shapes[:1] = [jax.ShapeDtypeStruct((nseq, SEQ, D), F32), jax.ShapeDtypeStruct((nseq, N_META, D), F32)]
        row_specs[:1] = [ANY, ANY]
        n_ro += 1
    return pl.pallas_call(
        body, name=name,
        out_shape=tuple(row_shapes + [jax.ShapeDtypeStruct(s, F32) for s in sum_outs]),
        grid=(m // tm, nk),
        in_specs=[pl.BlockSpec((tm, tk), lambda i, k: (i, k)), b_spec] + row_in_specs
        + [pl.BlockSpec((1, D), lambda i, k: (0, 0))] * n_v + [ANY] * len(extra),
        out_specs=tuple(row_specs + [pl.BlockSpec(s, lambda i, k: (0, 0)) for s in sum_outs]),
        scratch_shapes=[pltpu.VMEM((tm, D), F32)]
        + ([] if unpadded_in is None else [pltpu.VMEM((tm, D), F32), pltpu.SemaphoreType.DMA])
        + ([pltpu.VMEM((tm, D), F32), pltpu.SemaphoreType.DMA((2,))] if unpadded_out else []),
        compiler_params=_cparams("arbitrary", "arbitrary"))(a, b, *rows_in, *vecs_in, *extra)


def _residual_norm(row0, acc, h, g):
    hv = h + acc
    return (hv, (hv * _rms(hv)) * g), ()


def _rms_bwd_math(hv, dn, gv):
    r = _rms(hv)
    hr = hv * r
    dng = dn * gv
    dh = r * (dng - hr * jnp.mean(dng * hr, axis=-1, keepdims=True))
    return dh, dn * hr


def _loss_head(row0, acc, h1, tgt, g):
    hv = h1 + acc
    hr = hv * _rms(hv)
    pos = row0 % LP + lax.broadcasted_iota(jnp.int32, (EPI_ROWS, 1), 0)
    valid = (pos >= N_META) & (pos < L_REAL)
    err = jnp.where(valid, hr * g - tgt, 0.0)
    part = 0.5 * jnp.sum(jnp.mean(err * err, axis=-1, keepdims=True))
    dy = err * (1.0 / D)
    dh, dgrow = _rms_bwd_math(hv, dy, g)
    return (dh, dh), (jnp.full((8, LANES), part, F32), jnp.sum(dgrow, axis=0, keepdims=True))


def _residual_norm_bwd(row0, acc, h, dres, g):
    dh, dgrow = _rms_bwd_math(h, acc, g)
    dh = dh + dres
    return (dh, dh), (jnp.sum(dgrow, axis=0, keepdims=True),)


def _residual_norm_bwd_f32(row0, acc, h, dres, g):
    tiles, sums = _residual_norm_bwd(row0, acc, h, dres, g)
    return tiles[:1], sums


GATE_BLK = GATE_COL // D


def _sigmoid(x):
    return 1.0 / (1.0 + jnp.exp(-x))


def _branch_merge(row0, acc, p_sb, g_sb, g_fx):
    return (acc, _sigmoid(g_sb) * p_sb + _sigmoid(g_fx) * acc), ()


def _merge_bwd(dm, p, proj, dproj, which, name):
    t = dm.shape[0]
    row = pl.BlockSpec((TR, D), lambda i: (i, 0))
    gate = pl.BlockSpec((TR, D), lambda i: (i, GATE_BLK + which))

    def body(dm_ref, p_ref, g_ref, *rest):
        dp_ref, dg_ref = rest[-2:]
        dmv = dm_ref[...]
        s = _sigmoid(g_ref[...])
        dp_ref[...] = (dmv * s).astype(BF16)
        dg_ref[...] = (dmv * p_ref[...] * s * (1.0 - s)).astype(BF16)

    out_shape = (jax.ShapeDtypeStruct((t, D), BF16), jax.ShapeDtypeStruct((t, IN_P), BF16))
    if dproj is None:
        return pl.pallas_call(
            body, name=name, out_shape=out_shape, grid=(t // TR,), in_specs=[row, row, gate],
            out_specs=(row, gate), compiler_params=_cparams("parallel"))(dm, p, proj)
    return pl.pallas_call(
        body, name=name, out_shape=out_shape, grid=(t // TR,), in_specs=[row, row, gate, ANY],
        out_specs=(row, gate), input_output_aliases={3: 1}, compiler_params=_cparams("parallel"))(dm, p, proj, dproj)


CH = 288


def _chunk(c, n=CH):
    return pl.ds(pl.multiple_of(c * CH, 8), n)


def _conv_taps(u_ref, c):
    x = u_ref[_chunk(c), :]
    prev = u_ref[pl.ds(pl.multiple_of(jnp.maximum(c * CH - 8, 0), 8), 8), :]
    xx = jnp.concatenate([jnp.where(c == 0, 0.0, prev), x], axis=0)
    return x, pltpu.roll(xx, 1, 0)[8:], pltpu.roll(xx, 2, 0)[8:]


def _conv_glu_fwd(u, cw, nseq, name):
    nblk = D_FF // FFC

    def body(u_ref, cw_ref, o_ref):
        cwv = cw_ref[...]

        def step(c, _):
            x, x1, x2 = _conv_taps(u_ref, c)
            uc = cwv[0:1, :] * x2 + cwv[1:2, :] * x1 + cwv[2:3, :] * x
            a, b = uc[:, :FFC], uc[:, FFC:]
            o_ref[_chunk(c), :] = (a * _sigmoid(a) * b).astype(BF16)
            return 0

        lax.fori_loop(0, LP // CH, step, 0)

    return pl.pallas_call(
        body, name=name, out_shape=jax.ShapeDtypeStruct((nseq * LP, D_FF), BF16), grid=(nseq, nblk),
        in_specs=[pl.BlockSpec((LP, 2 * FFC), lambda s, j: (s, j)), pl.BlockSpec((3, 2 * FFC), lambda s, j: (0, j))],
        out_specs=pl.BlockSpec((LP, FFC), lambda s, j: (s, j)),
        compiler_params=_cparams("parallel", "parallel"))(u, cw)


def _conv_glu_bwd(u, cw, dact, nseq, name):
    nblk = D_FF // FFC
    nch = LP // CH

    def body(u_ref, cw_ref, da_ref, du_ref, dcw_ref):
        s = pl.program_id(1)
        cwv = cw_ref[...]

        def step(k, carry):
            nxt, p0, p1, p2 = carry
            c = nch - 1 - k
            x, x1, x2 = _conv_taps(u_ref, c)
            uc = cwv[0:1, :] * x2 + cwv[1:2, :] * x1 + cwv[2:3, :] * x
            a, b = uc[:, :FFC], uc[:, FFC:]
            sa = _sigmoid(a)
            dactv = da_ref[_chunk(c), :]
            da = dactv * b * (sa * (1.0 + a * (1.0 - sa)))
            db = dactv * (a * sa)
            duc = jnp.concatenate([da, db], axis=1)
            dd = jnp.concatenate([duc, nxt], axis=0)
            du = (cwv[2:3, :] * duc + cwv[1:2, :] * pltpu.roll(dd, CH + 7, 0)[:CH]
                  + cwv[0:1, :] * pltpu.roll(dd, CH + 6, 0)[:CH])
            du_ref[_chunk(c), :] = du.astype(BF16)
            return (duc[:8], p0 + jnp.sum(duc * x2, axis=0, keepdims=True),
                    p1 + jnp.sum(duc * x1, axis=0, keepdims=True), p2 + jnp.sum(duc * x, axis=0, keepdims=True))

        zrow = jnp.zeros((1, 2 * FFC), F32)
        _, p0, p1, p2 = lax.fori_loop(0, nch, step, (jnp.zeros((8, 2 * FFC), F32), zrow, zrow, zrow))

        @pl.when(s == 0)
        def _():
            dcw_ref[...] = jnp.zeros_like(dcw_ref)

        dcw_ref[...] += jnp.concatenate([p0, p1, p2], axis=0)

    return pl.pallas_call(
        body, name=name,
        out_shape=(jax.ShapeDtypeStruct((nseq * LP, 2 * D_FF), BF16), jax.ShapeDtypeStruct((3, 2 * D_FF), F32)),
        grid=(nblk, nseq),
        in_specs=[pl.BlockSpec((LP, 2 * FFC), lambda j, s: (s, j)), pl.BlockSpec((3, 2 * FFC), lambda j, s: (0, j)),
                  pl.BlockSpec((LP, FFC), lambda j, s: (s, j))],
        out_specs=(pl.BlockSpec((LP, 2 * FFC), lambda j, s: (s, j)), pl.BlockSpec((3, 2 * FFC), lambda j, s: (0, j))),
        compiler_params=_cparams("parallel", "arbitrary"))(u, cw, dact)


F_BLK = F_COL // LANES
CB = 128


def _split3(x):
    hi = x.astype(BF16)
    r1 = x - hi.astype(F32)
    mid = r1.astype(BF16)
    lo = (r1 - mid.astype(F32)).astype(BF16)
    return hi, mid, lo


def _tri_dot(tri, x):
    hi, mid, lo = _split3(x)
    d = functools.partial(jnp.dot, preferred_element_type=F32)
    return d(tri, hi) + d(tri, mid) + d(tri, lo)


def _log_sigmoid(x):
    return jnp.minimum(x, 0.0) - jnp.log(1.0 + jnp.exp(-jnp.abs(x)))


def _gate_fwd(proj, bf, nseq, name):
    def body(f_ref, b_ref, c_ref):
        r_i = lax.broadcasted_iota(jnp.int32, (CB, CB), 0)
        c_i = lax.broadcasted_iota(jnp.int32, (CB, CB), 1)
        tri = (c_i <= r_i).astype(BF16)
        bv = b_ref[...]

        def step(k, carry):
            rows = pl.ds(pl.multiple_of(k * CB, CB), CB)
            lf = _log_sigmoid(f_ref[rows, :] + bv)
            c_ref[rows, :] = _tri_dot(tri, lf) + carry
            return carry + jnp.sum(lf, axis=0, keepdims=True)

        lax.fori_loop(0, LP // CB, step, jnp.zeros((1, LANES), F32))

    return pl.pallas_call(
        body, name=name, out_shape=jax.ShapeDtypeStruct((nseq * LP, LANES), F32), grid=(nseq,),
        in_specs=[pl.BlockSpec((LP, LANES), lambda s: (s, F_BLK)), pl.BlockSpec((1, LANES), lambda s: (0, 0))],
        out_specs=pl.BlockSpec((LP, LANES), lambda s: (s, 0)),
        compiler_params=_cparams("parallel"))(proj, bf)


def _gate_bwd(proj, bf, dc, dproj, nseq, name):
    def body(f_ref, b_ref, dc_ref, _, df_ref, db_ref):
        s = pl.program_id(0)
        r_i = lax.broadcasted_iota(jnp.int32, (CB, CB), 0)
        c_i = lax.broadcasted_iota(jnp.int32, (CB, CB), 1)
        tri = (c_i >= r_i).astype(BF16)
        bv = b_ref[...]

        def step(kk, carry):
            carry_c, carry_b = carry
            k = LP // CB - 1 - kk
            rows = pl.ds(pl.multiple_of(k * CB, CB), CB)
            dcv = dc_ref[rows, :]
            dlf = _tri_dot(tri, dcv) + carry_c
            df = dlf * _sigmoid(-(f_ref[rows, :] + bv))
            df_ref[rows, :] = jnp.concatenate([df, jnp.zeros_like(df)], axis=1).astype(BF16)
            return carry_c + jnp.sum(dcv, axis=0, keepdims=True), carry_b + jnp.sum(df, axis=0, keepdims=True)

        zero = jnp.zeros((1, LANES), F32)
        _, dbp = lax.fori_loop(0, LP // CB, step, (zero, zero))

        @pl.when(s == 0)
        def _():
            db_ref[...] = jnp.zeros_like(db_ref)

        db_ref[...] += dbp

    return pl.pallas_call(
        body, name=name,
        out_shape=(jax.ShapeDtypeStruct(dproj.shape, BF16), jax.ShapeDtypeStruct((1, LANES), F32)), grid=(nseq,),
        in_specs=[pl.BlockSpec((LP, LANES), lambda s: (s, F_BLK)), pl.BlockSpec((1, LANES), lambda s: (0, 0)),
                  pl.BlockSpec((LP, LANES), lambda s: (s, 0)), ANY],
        out_specs=(pl.BlockSpec((LP, 2 * LANES), lambda s: (s, F_COL // (2 * LANES))), pl.BlockSpec((1, LANES), lambda s: (0, 0))),
        input_output_aliases={3: 0},
        compiler_params=_cparams("arbitrary"))(proj, bf, dc, dproj)


SCALE = 0.125
NEG = -1e30


def _dot_nt(a, b):
    return lax.dot_general(a, b, (((1,), (1,)), ((), ())), preferred_element_type=F32)


def _dot(a, b):
    return jnp.dot(a, b, preferred_element_type=F32)


def _blk(i, n=BQ):
    return pl.ds(pl.multiple_of(i * BQ, BQ), n)


def _query_blocks(qblock, n_last):
    def full(i, _):
        qblock(i, BQ)
        return 0

    lax.fori_loop(0, NBLK - 1, full, 0)
    qblock(jnp.minimum(pl.program_id(0) + NBLK, NBLK - 1), n_last)


def _tile_iotas():
    return lax.broadcasted_iota(jnp.int32, (BQ, BQ), 0), lax.broadcasted_iota(jnp.int32, (BQ, BQ), 1)


def _lane_iota():
    return lax.broadcasted_iota(jnp.int32, (BQ, LANES), 1)


def _head_masks():
    lane = _lane_iota()
    return lane < HEAD, lane >= HEAD


def _only(mask, x):
    return jnp.where(mask, x, jnp.zeros_like(x))


def _chains(npair):
    return [(pp, h) for pp in range(npair) for h in range(2)]


def _load_qkv(p_ref, q_s, k_s, v_s):
    for pp in range(q_s.shape[0]):
        base = pp * PAIR_W
        q_s[pp] = (p_ref[:, base:base + LANES] * SCALE).astype(BF16)
        k_s[pp] = p_ref[:, base + LANES:base + 2 * LANES].astype(BF16)
        v_s[pp] = p_ref[:, base + 2 * LANES:base + 3 * LANES].astype(BF16)


def _softplus(z):
    return jnp.maximum(z, 0.0) + jnp.log(1.0 + jnp.exp(-jnp.abs(z)))


def _sb_tile_weights(q, k, strict, r, u_suf):
    n = len(q)
    z = [_dot_nt(q[c], k[c]) for c in range(n)]
    sp = [_softplus(zc) for zc in z]
    lk = [-spc if strict is None else jnp.where(strict, -spc, 0.0) for spc in sp]
    suf = [_dot(lkc.astype(BF16), u_suf) for lkc in lk]
    w = [jnp.exp(z[c] - sp[c] + r[c] + suf[c]) for c in range(n)]
    if strict is not None:
        w = [jnp.where(strict, wc, 0.0) for wc in w]
    r_next = [r[c] + suf[c][:, 0:1] + lk[c][:, 0:1] for c in range(n)]
    return w, sp, r_next


def _group_spec(kind, npair):
    return pl.BlockSpec((LP, npair * PAIR_W), lambda s, g: (s, (NH // (2 * npair)) * kind + g))


def _gheads_spec(npair):
    return pl.BlockSpec((LP, npair * LANES), lambda s, g: (s, g))


def _qkv_scratch(npair):
    return [pltpu.VMEM((npair, LP, LANES), BF16)] * 3


SEQ_SPEC = pl.BlockSpec((LP, LANES), lambda s, g: (s, 0))
RS_STRIDE = 16
Q_LAST = 128
Q_LAST_ROWS = 16


def _pair_cols(pp):
    return slice(pp * LANES, (pp + 1) * LANES)


def _carry_spec(npair):
    return pl.BlockSpec((None, npair * LANES, LP), lambda s, g: (s, g, 0))


def _sb_fwd(proj, nseq, npair, name):
    t = nseq * LP
    chains = _chains(npair)

    def body(p_ref, o_ref, rs_ref, q_s, k_s, v_s, acc_ref, r_ref, rb_ref):
        _load_qkv(p_ref, q_s, k_s, v_s)
        row, col = _tile_iotas()
        u_suf = (row > col).astype(BF16)
        heads = _head_masks()

        def qblock(i, nq):
            diag = (col < row)[:nq]
            lane = _lane_iota()[:nq]
            heads_q = [hm[:nq] for hm in heads]
            acc_ref[...] = jnp.zeros_like(acc_ref)
            rb_ref[...] = jnp.zeros_like(rb_ref)
            r_ref[...] = jnp.zeros_like(r_ref)
            qb = [q_s[pp, _blk(i, nq), :] for pp in range(npair)]

            def tile(j, strict):
                kj = [k_s[pp, _blk(j), :] for pp in range(npair)]
                vj = [v_s[pp, _blk(j), :] for pp in range(npair)]
                r = [r_ref[c, :nq] for c in range(len(chains))]
                w, _, r_next = _sb_tile_weights([_only(heads_q[h], qb[pp]) for pp, h in chains],
                                                [kj[pp] for pp, _ in chains], strict, r, u_suf)
                pv = [_dot(w[c].astype(BF16), _only(heads[h], vj[pp])) for c, (pp, h) in enumerate(chains)]
                for pp in range(npair):
                    acc_ref[pp, :nq] += pv[2 * pp] + pv[2 * pp + 1]
                    rb_ref[pp, :nq] = jnp.where(lane == j, r[2 * pp], jnp.where(lane == RS_STRIDE + j, r[2 * pp + 1], rb_ref[pp, :nq]))
                for c in range(len(chains)):
                    r_ref[c, :nq] = r_next[c]

            tile(i, diag)

            def kblock(jj, _):
                tile(i - jj, None)
                return 0

            lax.fori_loop(1, i + 1, kblock, 0)
            for pp in range(npair):
                o_ref[_blk(i), _pair_cols(pp)] = acc_ref[pp].astype(BF16)
                rs_ref[_pair_cols(pp), _blk(i)] = rb_ref[pp].T

        _query_blocks(qblock, Q_LAST_ROWS)

    return pl.pallas_call(
        body, name=name,
        out_shape=(jax.ShapeDtypeStruct((t, W_ATT), BF16), jax.ShapeDtypeStruct((nseq, W_ATT, LP), F32)),
        grid=(nseq, NH // (2 * npair)), in_specs=[_group_spec(0, npair)], out_specs=(_gheads_spec(npair), _carry_spec(npair)),
        scratch_shapes=_qkv_scratch(npair) + [pltpu.VMEM((npair, BQ, LANES), F32), pltpu.VMEM((2 * npair, BQ, 1), F32),
                                      pltpu.VMEM((npair, BQ, LANES), F32)],
        compiler_params=_cparams("parallel", "parallel"))(proj)


def _sb_bwd(proj, do, rs, dproj, after, nseq, npair, name):
    chains = _chains(npair)

    def body(p_ref, do_ref, rs_ref, _, _after, dp_ref, q_s, k_s, v_s, kt_s, dqa_ref, dka_ref, dva_ref, ep_ref):
        _load_qkv(p_ref, q_s, k_s, v_s)
        row, col = _tile_iotas()
        u_after = (col > row).astype(BF16)
        u_before = (col < row).astype(BF16)
        heads = _head_masks()
        sub_all = lax.broadcasted_iota(jnp.int32, (LANES, BQ), 0)
        rows_k = (sub_all < HEAD, sub_all >= HEAD)
        nc = len(chains)
        for pp in range(npair):
            kt_s[pp] = k_s[pp].astype(F32).T.astype(BF16)
        dka_ref[...] = jnp.zeros_like(dka_ref)
        dva_ref[...] = jnp.zeros_like(dva_ref)

        def qblock(i, nq):
            diag = (row < col)[:, :nq]
            rows_of = [m[:, :nq] for m in rows_k]
            heads_q = [m[:nq] for m in heads]
            queries = _blk(i, nq)
            qb = [q_s[pp, queries, :] for pp in range(npair)]
            dob = [do_ref[queries, _pair_cols(pp)] for pp in range(npair)]
            qt = [qb[pp].astype(F32).T.astype(BF16) for pp in range(npair)]
            dot = [dob[pp].astype(F32).T.astype(BF16) for pp in range(npair)]
            qt_m = [jnp.where(rows_of[h], qt[pp], jnp.zeros_like(qt[pp])) for pp, h in chains]
            dot_m = [jnp.where(rows_of[h], dot[pp], jnp.zeros_like(dot[pp])) for pp, h in chains]
            q_m = [_only(heads_q[h], qb[pp]) for pp, h in chains]
            do_m = [_only(heads_q[h], dob[pp]) for pp, h in chains]
            dqa_ref[...] = jnp.zeros_like(dqa_ref)
            ep_ref[...] = jnp.zeros_like(ep_ref)

            def tile(j, strict):
                keys = pl.ds(pl.multiple_of(j * BQ, BQ), BQ)
                kj = [k_s[pp, _blk(j), :] for pp in range(npair)]
                vj = [v_s[pp, _blk(j), :] for pp in range(npair)]
                r = [_key_cols(rs_ref, pp * LANES + RS_STRIDE * h + j, i, nq) for pp, h in chains]
                z = [_dot(kj[pp], qt_m[cidx]) for cidx, (pp, _) in enumerate(chains)]
                dw = [_dot(vj[pp], dot_m[cidx]) for cidx, (pp, _) in enumerate(chains)]
                sp = [_softplus(zc) for zc in z]
                lk = [-spc if strict is None else jnp.where(strict, -spc, 0.0) for spc in sp]
                suf = [_dot(u_after, lkc.astype(BF16)) for lkc in lk]
                w = [jnp.exp(z[cidx] - sp[cidx] + r[cidx] + suf[cidx]) for cidx in range(nc)]
                if strict is not None:
                    w = [jnp.where(strict, wc, 0.0) for wc in w]
                e = [dw[cidx] * w[cidx] for cidx in range(nc)]
                e_pre = [ep_ref[cidx, :, :nq] + _dot(u_before, e[cidx].astype(BF16)) for cidx in range(nc)]
                dz = []
                for cidx in range(nc):
                    ep_ref[cidx, :, :nq] += jnp.sum(e[cidx], axis=0, keepdims=True)
                    sneg = jnp.exp(-sp[cidx])
                    dzc = e[cidx] * sneg - (1.0 - sneg) * e_pre[cidx]
                    if strict is not None:
                        dzc = jnp.where(strict, dzc, 0.0)
                    dz.append(dzc.astype(BF16))
                dq = [_dot(jnp.where(rows_k[h], kt_s[pp, :, keys], jnp.zeros((LANES, BQ), BF16)), dz[cidx])
                      for cidx, (pp, h) in enumerate(chains)]
                dk = [_dot(dz[cidx], q_m[cidx]) for cidx in range(nc)]
                dv = [_dot(w[cidx].astype(BF16), do_m[cidx]) for cidx in range(nc)]
                for pp in range(npair):
                    dqa_ref[pp, :, :nq] += dq[2 * pp] + dq[2 * pp + 1]
                    dka_ref[pp, _blk(j), :] += dk[2 * pp] + dk[2 * pp + 1]
                    dva_ref[pp, _blk(j), :] += dv[2 * pp] + dv[2 * pp + 1]

            def kblock(j, _):
                tile(j, None)
                return 0

            lax.fori_loop(0, i, kblock, 0)
            tile(i, diag)
            for pp in range(npair):
                dp_ref[_blk(i), pp * PAIR_W:pp * PAIR_W + LANES] = (dqa_ref[pp].T * SCALE).astype(BF16)

        _query_blocks(qblock, Q_LAST)
        for pp in range(npair):
            dp_ref[:, pp * PAIR_W + LANES:pp * PAIR_W + 2 * LANES] = dka_ref[pp].astype(BF16)
            dp_ref[:, pp * PAIR_W + 2 * LANES:pp * PAIR_W + 3 * LANES] = dva_ref[pp].astype(BF16)

    return pl.pallas_call(
        body, name=name, out_shape=jax.ShapeDtypeStruct(dproj.shape, BF16), grid=(nseq, NH // (2 * npair)),
        in_specs=[_group_spec(0, npair), _gheads_spec(npair), _carry_spec(npair), ANY, ANY], out_specs=_group_spec(0, npair),
        input_output_aliases={3: 0},
        scratch_shapes=_qkv_scratch(npair) + [pltpu.VMEM((npair, LANES, LP), BF16), pltpu.VMEM((npair, LANES, BQ), F32),
                                      pltpu.VMEM((npair, LP, LANES), F32), pltpu.VMEM((npair, LP, LANES), F32),
                                      pltpu.VMEM((2 * npair, 1, BQ), F32)],
        compiler_params=_cparams("parallel", "parallel"))(proj, do, rs, dproj, after)


CROW_SPEC = pl.BlockSpec((None, NH, LP), lambda s, g: (s, 0, 0))


def _key_cols(cr_ref, head, j, n=BQ):
    return cr_ref[pl.ds(head, 1), _blk(j)][:, :n]


def _fox_fwd(proj, c, crow, nseq, npair, name):
    t = nseq * LP
    chains = _chains(npair)

    def body(p_ref, c_ref, cr_ref, o_ref, o32_ref, lse_ref, q_s, k_s, v_s, vt_s, ck_s, acc_ref, m_ref, l_ref):
        _load_qkv(p_ref, q_s, k_s, v_s)
        row, col = _tile_iotas()
        sub_all = lax.broadcasted_iota(jnp.int32, (LANES, BQ), 0)
        rows_k = (sub_all < HEAD, sub_all >= HEAD)
        head0 = 2 * npair * pl.program_id(1)
        nc = len(chains)
        lane_all = lax.broadcasted_iota(jnp.int32, (LP, LANES), 1)
        for pp in range(npair):
            vt_s[pp] = v_s[pp].astype(F32).T.astype(BF16)
        for cidx in range(nc):
            ck_s[cidx] = jnp.sum(jnp.where(lane_all == head0 + cidx, c_ref[...], 0.0), axis=1, keepdims=True)

        def qblock(i, nq):
            diag = (row <= col)[:, :nq]
            sub = sub_all[:, :nq]
            rows_of = (sub < HEAD, sub >= HEAD)
            qt = [q_s[pp, _blk(i, nq), :].astype(F32).T.astype(BF16) for pp in range(npair)]
            qt = [jnp.where(rows_of[h], qt[pp], jnp.zeros_like(qt[pp])) for pp, h in chains]
            cq = [_key_cols(cr_ref, head0 + cidx, i, nq) for cidx in range(nc)]
            acc_ref[...] = jnp.zeros_like(acc_ref)
            m_ref[...] = jnp.full_like(m_ref, NEG)
            l_ref[...] = jnp.zeros_like(l_ref)

            def tile(j, causal):
                keys = pl.ds(pl.multiple_of(j * BQ, BQ), BQ)
                z = [_dot(k_s[pp, _blk(j), :], qt[cidx]) + (cq[cidx] - ck_s[cidx, _blk(j), :])
                     for cidx, (pp, _) in enumerate(chains)]
                if causal is not None:
                    z = [jnp.where(causal, zc, NEG) for zc in z]
                p, alpha = [], []
                for cidx in range(nc):
                    m_old = m_ref[cidx, :, :nq]
                    m_new = jnp.maximum(m_old, jnp.max(z[cidx], axis=0, keepdims=True))
                    alpha.append(jnp.exp(m_old - m_new))
                    pc = jnp.exp(z[cidx] - m_new)
                    l_ref[cidx, :, :nq] = alpha[cidx] * l_ref[cidx, :, :nq] + jnp.sum(pc, axis=0, keepdims=True)
                    m_ref[cidx, :, :nq] = m_new
                    p.append(pc.astype(BF16))
                pv = [_dot(jnp.where(rows_k[h], vt_s[pp, :, keys], jnp.zeros((LANES, BQ), BF16)), p[cidx])
                      for cidx, (pp, h) in enumerate(chains)]
                for cidx in range(nc):
                    acc_ref[cidx, :, :nq] = alpha[cidx] * acc_ref[cidx, :, :nq] + pv[cidx]

            def kblock(j, _):
                tile(j, None)
                return 0

            lax.fori_loop(0, i, kblock, 0)
            tile(i, diag)
            for pp in range(npair):
                acc = [acc_ref[2 * pp + h, :, :nq] for h in range(2)]
                l = [l_ref[2 * pp + h, :, :nq] for h in range(2)]
                out = (acc[0] / l[0] + acc[1] / l[1]).T
                o_ref[_blk(i, nq), _pair_cols(pp)] = out.astype(BF16)
                o32_ref[_blk(i, nq), _pair_cols(pp)] = out
                lse = [m_ref[2 * pp + h, :, :nq] + jnp.log(l[h]) for h in range(2)]
                lse_t = jnp.where(sub == 0, lse[0], jnp.where(sub == 1, lse[1], 0.0))
                lse_ref[_blk(i, nq), _pair_cols(pp)] = lse_t.T
                if nq < BQ:
                    rest = pl.ds(pl.multiple_of(i * BQ + nq, nq), BQ - nq)
                    o_ref[rest, _pair_cols(pp)] = jnp.zeros((BQ - nq, LANES), BF16)
                    o32_ref[rest, _pair_cols(pp)] = jnp.zeros((BQ - nq, LANES), F32)
                    lse_ref[rest, _pair_cols(pp)] = jnp.zeros((BQ - nq, LANES), F32)

        _query_blocks(qblock, Q_LAST)

    return pl.pallas_call(
        body, name=name,
        out_shape=(jax.ShapeDtypeStruct((t, W_ATT), BF16), jax.ShapeDtypeStruct((t, W_ATT), F32),
                   jax.ShapeDtypeStruct((t, W_ATT), F32)),
        grid=(nseq, NH // (2 * npair)), in_specs=[_group_spec(1, npair), SEQ_SPEC, CROW_SPEC], out_specs=(_gheads_spec(npair), _gheads_spec(npair), _gheads_spec(npair)),
        scratch_shapes=_qkv_scratch(npair) + [pltpu.VMEM((npair, LANES, LP), BF16), pltpu.VMEM((2 * npair, LP, 1), F32),
                                      pltpu.VMEM((2 * npair, LANES, BQ), F32), pltpu.VMEM((2 * npair, 1, BQ), F32),
                                      pltpu.VMEM((2 * npair, 1, BQ), F32)],
        compiler_params=_cparams("parallel", "parallel"))(proj, c, crow)


def _fox_bwd(proj, c, crow, o32, lse, do, dproj, nseq, npair, name):
    t = nseq * LP
    chains = _chains(npair)

    def body(p_ref, c_ref, cr_ref, o_ref, lse_ref, do_ref, _, dp_ref, dc_ref,
             q_s, k_s, v_s, kt_s, ck_s, dqa_ref, dka_ref, dva_ref, rsum_ref):
        _load_qkv(p_ref, q_s, k_s, v_s)
        row, col = _tile_iotas()
        lane = _lane_iota()
        heads = _head_masks()
        sub_all = lax.broadcasted_iota(jnp.int32, (LANES, BQ), 0)
        rows_k = (sub_all < HEAD, sub_all >= HEAD)
        group = pl.program_id(1)
        head0 = 2 * npair * group
        nc = len(chains)
        lane_all = lax.broadcasted_iota(jnp.int32, (LP, LANES), 1)
        for pp in range(npair):
            kt_s[pp] = k_s[pp].astype(F32).T.astype(BF16)
        for cidx in range(nc):
            ck_s[cidx] = jnp.sum(jnp.where(lane_all == head0 + cidx, c_ref[...], 0.0), axis=1, keepdims=True)
        dka_ref[...] = jnp.zeros_like(dka_ref)
        dva_ref[...] = jnp.zeros_like(dva_ref)

        @pl.when(group == 0)
        def _():
            dc_ref[...] = jnp.zeros_like(dc_ref)

        def qblock(i, nq):
            diag = (row <= col)[:, :nq]
            sub = sub_all[:, :nq]
            rows_of = [m[:, :nq] for m in rows_k]
            heads_q = [m[:nq] for m in heads]
            queries = _blk(i, nq)
            dqa_ref[...] = jnp.zeros_like(dqa_ref)
            rsum_ref[...] = jnp.zeros_like(rsum_ref)
            qb = [q_s[pp, queries, :] for pp in range(npair)]
            dob = [do_ref[queries, _pair_cols(pp)] for pp in range(npair)]
            qt = [qb[pp].astype(F32).T.astype(BF16) for pp in range(npair)]
            dot = [dob[pp].astype(F32).T for pp in range(npair)]
            prod = [dot[pp] * o_ref[queries, _pair_cols(pp)].T for pp in range(npair)]
            lse_t = [lse_ref[queries, _pair_cols(pp)].T for pp in range(npair)]
            qt_m = [jnp.where(rows_of[h], qt[pp], jnp.zeros_like(qt[pp])) for pp, h in chains]
            dot_m = [jnp.where(rows_of[h], dot[pp], 0.0).astype(BF16) for pp, h in chains]
            q_m = [_only(heads_q[h], qb[pp]) for pp, h in chains]
            do_m = [_only(heads_q[h], dob[pp]) for pp, h in chains]
            cq = [_key_cols(cr_ref, head0 + cidx, i, nq) for cidx in range(nc)]
            lse_i = [lse_t[pp][h:h + 1, :] for pp, h in chains]
            delta = [jnp.sum(jnp.where(rows_of[h], prod[pp], 0.0), axis=0, keepdims=True) for pp, h in chains]

            def tile(j, causal):
                keys = pl.ds(pl.multiple_of(j * BQ, BQ), BQ)
                kj = [k_s[pp, _blk(j), :] for pp in range(npair)]
                vj = [v_s[pp, _blk(j), :] for pp in range(npair)]
                z = [_dot(kj[pp], qt_m[cidx]) + (cq[cidx] - ck_s[cidx, _blk(j), :]) for cidx, (pp, _) in enumerate(chains)]
                if causal is not None:
                    z = [jnp.where(causal, zc, NEG) for zc in z]
                dpv = [_dot(vj[pp], dot_m[cidx]) for cidx, (pp, _) in enumerate(chains)]
                p = [jnp.exp(z[cidx] - lse_i[cidx]) for cidx in range(nc)]
                ds = [p[cidx] * (dpv[cidx] - delta[cidx]) for cidx in range(nc)]
                dsb = [d.astype(BF16) for d in ds]
                dq = [_dot(jnp.where(rows_k[h], kt_s[pp, :, keys], jnp.zeros((LANES, BQ), BF16)), dsb[cidx])
                      for cidx, (pp, h) in enumerate(chains)]
                dk = [_dot(dsb[cidx], q_m[cidx]) for cidx in range(nc)]
                dv = [_dot(p[cidx].astype(BF16), do_m[cidx]) for cidx in range(nc)]
                for pp in range(npair):
                    dqa_ref[pp, :, :nq] += dq[2 * pp] + dq[2 * pp + 1]
                    dka_ref[pp, _blk(j), :] += dk[2 * pp] + dk[2 * pp + 1]
                    dva_ref[pp, _blk(j), :] += dv[2 * pp] + dv[2 * pp + 1]
                col_sums = jnp.zeros((BQ, LANES), F32)
                for cidx in range(nc):
                    col_sums = col_sums + jnp.where(lane == head0 + cidx, jnp.sum(ds[cidx], axis=1, keepdims=True), 0.0)
                    rsum_ref[cidx, :, :nq] += jnp.sum(ds[cidx], axis=0, keepdims=True)
                dc_ref[_blk(j), :] = dc_ref[_blk(j), :] - col_sums

            def kblock(j, _):
                tile(j, None)
                return 0

            lax.fori_loop(0, i, kblock, 0)
            tile(i, diag)
            row_sums = jnp.zeros((LANES, nq), F32)
            for cidx in range(nc):
                row_sums = row_sums + jnp.where(sub == head0 + cidx, rsum_ref[cidx, :, :nq], 0.0)
            dc_ref[queries, :] += row_sums.T
            for pp in range(npair):
                dp_ref[_blk(i), pp * PAIR_W:pp * PAIR_W + LANES] = (dqa_ref[pp].T * SCALE).astype(BF16)

        _query_blocks(qblock, Q_LAST)
        for pp in range(npair):
            dp_ref[:, pp * PAIR_W + LANES:pp * PAIR_W + 2 * LANES] = dka_ref[pp].astype(BF16)
            dp_ref[:, pp * PAIR_W + 2 * LANES:pp * PAIR_W + 3 * LANES] = dva_ref[pp].astype(BF16)

    return pl.pallas_call(
        body, name=name,
        out_shape=(jax.ShapeDtypeStruct(dproj.shape, BF16), jax.ShapeDtypeStruct((t, LANES), F32)),
        grid=(nseq, NH // (2 * npair)),
        in_specs=[_group_spec(1, npair), SEQ_SPEC, CROW_SPEC, _gheads_spec(npair), _gheads_spec(npair), _gheads_spec(npair), ANY],
        out_specs=(_group_spec(1, npair), SEQ_SPEC),
        input_output_aliases={6: 0},
        scratch_shapes=_qkv_scratch(npair) + [pltpu.VMEM((npair, LANES, LP), BF16), pltpu.VMEM((2 * npair, LP, 1), F32),
                                      pltpu.VMEM((npair, LANES, BQ), F32), pltpu.VMEM((npair, LP, LANES), F32),
                                      pltpu.VMEM((npair, LP, LANES), F32), pltpu.VMEM((2 * npair, 1, BQ), F32)],
        compiler_params=_cparams("parallel", "arbitrary"))(proj, c, crow, o32, lse, do, dproj)


def _adamw_math(w, g, m, v):
    m = B1 * m + (1.0 - B1) * g
    v = B2 * v + (1.0 - B2) * (g * g)
    m_hat = m / (1.0 - B1 ** STEP)
    v_hat = v / (1.0 - B2 ** STEP)
    delta = -LR * (m_hat / (jnp.sqrt(v_hat) + EPS) + WD * w)
    return delta, m, v


def _sum_adamw(parts, w, m, v, tr, name):
    rows, cols = w.shape
    cp = parts.shape[2]
    assert rows % tr == 0 and parts.shape[1] == rows

    def body(p_ref, w_ref, m_ref, v_ref, g_ref, d_ref, nm_ref, nv_ref):
        gsum = p_ref[0].astype(F32)
        for s in range(1, N_DEV):
            gsum = gsum + p_ref[s].astype(F32)
        gsum = gsum[:, :cols]
        d, nm, nv = _adamw_math(w_ref[...], gsum, m_ref[...], v_ref[...])
        g_ref[...] = gsum
        d_ref[...] = d
        nm_ref[...] = nm
        nv_ref[...] = nv

    blk = pl.BlockSpec((tr, cols), lambda i: (i, 0))
    out = jax.ShapeDtypeStruct((rows, cols), F32)
    return pl.pallas_call(
        body, name=name, out_shape=(out, out, out, out), grid=(rows // tr,),
        in_specs=[pl.BlockSpec((N_DEV, tr, cp), lambda i: (0, i, 0)), blk, blk, blk],
        out_specs=(blk, blk, blk, blk), compiler_params=_cparams("parallel"))(parts, w, m, v)


def _local_step(x, tgt, meta, g_mix, b_forget, g_ffn, g_final, first_weights, late_weights, early_grads, last_grad):
    nseq = x.shape[0]
    t = nseq * LP
    tm = LP // 2
    mm = functools.partial(_matmul, tm=tm)

    h0 = _pad_rows(meta, x, nseq, "pad_x").reshape(t, D)
    bf = jnp.pad(b_forget.reshape(1, NH), ((0, 0), (0, LANES - NH)))

    n1 = _norm_fwd(h0, g_mix, "norm1")
    w_in_p, started = first_weights(n1)
    proj = mm(n1, w_in_p, out_dtype=F32, tn=1792, tk=D, after=started, name="in_proj")
    c = _gate_fwd(proj, bf, nseq, "gate_fwd")
    crow = c[:, :NH].reshape(nseq, LP, NH).transpose(0, 2, 1)
    o_sb, rs = _sb_fwd(proj, nseq, 2, "sb_fwd")
    o_fx, o_fx32, lse = _fox_fwd(proj, c, crow, nseq, 2, "fox_fwd")
    w_bsb, w_bfx, w_out, w_up_i, cw_i, w_down = late_weights(o_fx)
    p_sb = mm(o_sb, w_bsb, out_dtype=F32, tn=D, tk=W_ATT, name="branch_sb")
    rows = functools.partial(_matmul_rows, tm=LP // 4)
    p_fx, merged = rows(o_fx, w_bfx, [p_sb, (proj, GATE_BLK), (proj, GATE_BLK + 1)], [], _branch_merge, [F32, BF16], [],
                        tk=W_ATT, name="branch_fox_merge")
    h1, n2 = rows(merged, w_out, [h0], [g_ffn], _residual_norm, [F32, BF16], [], tk=D, name="out_proj_norm2")
    u = mm(n2, w_up_i, out_dtype=F32, tn=1408, tk=D, name="up_proj")
    act = _conv_glu_fwd(u, cw_i, nseq, "conv_glu_fwd")

    dh2, dh2b, loss, dg_final = rows(act, w_down, [h1], [g_final], _loss_head, [F32, BF16],
                                     [(8, LANES), (1, D)], tk=D_FF, unpadded_in=tgt, name="down_proj_loss")
    d_down = _matmul(act, dh2b, out_dtype=BF16, tm=1408, tn=D, tk=LP, ta=True, name="d_w_down")
    dact = mm(dh2b, w_down, out_dtype=F32, tn=1408, tk=D, tb=True, name="d_act")
    du, d_cw = _conv_glu_bwd(u, cw_i, dact, nseq, "conv_glu_bwd")
    d_up = _matmul(n2, du, out_dtype=BF16, tm=D, tn=1408, tk=LP, ta=True, name="d_w_up")
    dh1, dh1b, dg_ffn = rows(du, w_up_i, [h1, dh2], [g_ffn], _residual_norm_bwd, [F32, BF16], [(1, D)],
                             tk=D_FF, tb=True, name="d_n2_norm2_bwd")
    d_out = _matmul(merged, dh1b, out_dtype=BF16, tm=D, tn=D, tk=LP, ta=True, name="d_w_out")
    dmerged = mm(dh1b, w_out, out_dtype=F32, tn=D, tk=D, tb=True, name="d_merged")
    dp_sb, dproj = _merge_bwd(dmerged, p_sb, proj, None, 0, "merge_bwd_sb")
    dp_fx, dproj = _merge_bwd(dmerged, p_fx, proj, dproj, 1, "merge_bwd_fox")
    d_bsb = _matmul(o_sb, dp_sb, out_dtype=BF16, tm=W_ATT, tn=D, tk=LP, ta=True, name="d_w_branch_sb")
    d_bfx = _matmul(o_fx, dp_fx, out_dtype=BF16, tm=W_ATT, tn=D, tk=LP, ta=True, name="d_w_branch_fox")
    do_sb = mm(dp_sb, w_bsb, out_dtype=BF16, tn=W_ATT, tk=D, tb=True, name="d_o_sb")
    do_fx = mm(dp_fx, w_bfx, out_dtype=BF16, tn=W_ATT, tk=D, tb=True, name="d_o_fox")
    sent = early_grads(dict(w_branch_sb=d_bsb, w_branch_fox=d_bfx, w_out=d_out, w_up=d_up, conv_w=d_cw, w_down=d_down))
    dproj = _sb_bwd(proj, do_sb, rs, dproj, sent, nseq, 2, "sb_bwd")
    dproj, dc = _fox_bwd(proj, c, crow, o_fx32, lse, do_fx, dproj, nseq, 2, "fox_bwd")
    dproj, d_bf = _gate_bwd(proj, bf, dc, dproj, nseq, "gate_bwd")
    d_in = _matmul(n1, dproj, out_dtype=BF16, tm=D, tn=1792, tk=LP, ta=True, name="d_w_in")
    grad_x, d_front, dg_mix = rows(dproj, w_in_p, [h0, dh1], [g_mix], _residual_norm_bwd_f32, [F32], [(1, D)],
                                   tk=IN_P // 2, tb=True, after=last_grad(d_in), unpadded_out=True, name="d_n1_norm1_bwd")
    grads = dict(meta_tokens=jnp.sum(d_front, axis=0), norm_mix_g=dg_mix, b_forget=d_bf[:, :NH],
                 norm_ffn_g=dg_ffn, norm_final_g=dg_final)
    return loss[0, 0], grad_x, grads


REPL = (("norm_mix_g", D), ("norm_ffn_g", D), ("norm_final_g", D), ("b_forget", LANES))
REPL_ROWS = 32
META_ROWS = N_META * D // LANES


def _pack_repl(tree):
    rows = [jnp.pad(tree[name].reshape(-1), (0, n - tree[name].size)).reshape(-1, LANES) for name, n in REPL]
    packed = jnp.concatenate(rows, axis=0)
    return jnp.pad(packed, ((0, REPL_ROWS - packed.shape[0]), (0, 0)))


def _unpack_repl(packed, shapes):
    out, r = {}, 0
    for name, n in REPL:
        size = 1
        for s in shapes[name]:
            size *= s
        out[name] = packed[r:r + n // LANES].reshape(-1)[:size].reshape(shapes[name])
        r += n // LANES
    return out


def kernel(x, meta_tokens, norm_mix_g, w_in, b_forget, w_branch_sb, w_branch_fox, w_out, norm_ffn_g, w_up, conv_w, w_down, norm_final_g, loss_target, m_meta_tokens, m_norm_mix_g, m_w_in, m_b_forget, m_w_branch_sb, m_w_branch_fox, m_w_out, m_norm_ffn_g, m_w_up, m_conv_w, m_w_down, m_norm_final_g, v_meta_tokens, v_norm_mix_g, v_w_in, v_b_forget, v_w_branch_sb, v_w_branch_fox, v_w_out, v_norm_ffn_g, v_w_up, v_conv_w, v_w_down, v_norm_final_g):
    w = dict(meta_tokens=meta_tokens, norm_mix_g=norm_mix_g, w_in=w_in, b_forget=b_forget, w_branch_sb=w_branch_sb,
             w_branch_fox=w_branch_fox, w_out=w_out, norm_ffn_g=norm_ffn_g, w_up=w_up, conv_w=conv_w, w_down=w_down,
             norm_final_g=norm_final_g)
    m = dict(meta_tokens=m_meta_tokens, norm_mix_g=m_norm_mix_g, w_in=m_w_in, b_forget=m_b_forget,
             w_branch_sb=m_w_branch_sb, w_branch_fox=m_w_branch_fox, w_out=m_w_out, norm_ffn_g=m_norm_ffn_g,
             w_up=m_w_up, conv_w=m_conv_w, w_down=m_w_down, norm_final_g=m_norm_final_g)
    v = dict(meta_tokens=v_meta_tokens, norm_mix_g=v_norm_mix_g, w_in=v_w_in, b_forget=v_b_forget,
             w_branch_sb=v_w_branch_sb, w_branch_fox=v_w_branch_fox, w_out=v_w_out, norm_ffn_g=v_norm_ffn_g,
             w_up=v_w_up, conv_w=v_conv_w, w_down=v_w_down, norm_final_g=v_norm_final_g)
    shapes = {k: a.shape for k, a in w.items()}
    sharded = ("w_in", "w_branch_sb", "w_branch_fox", "w_out", "w_up", "w_down", "conv_w", "meta_tokens")
    mat = lambda tree, name: tree[name].reshape(tree[name].shape[-2:])

    def lane_pad(a, width):
        return jnp.pad(a, ((0, 0), (0, width - a.shape[1])))

    late = ("w_branch_sb", "w_branch_fox", "w_out", "w_up", "w_down", "conv_w")
    pending_w = {}
    g_meta, = _all_gather([mat(w, "meta_tokens")], "gather_meta")
    pending_w["in"], in_started = _remote_start(
        [lane_pad(mat(w, "w_in").astype(BF16), SHARD_P)], False, g_meta, "gather_w_in_start")
    meta_full = g_meta.transpose(1, 0, 2).reshape(N_META, D) + in_started[0, 0]

    def first_weights(after):
        g_in, = _remote_wait(pending_w["in"], after, "gather_w_in_wait")
        pending_w["late"], started = _remote_start(
            [mat(w, "w_branch_sb").astype(BF16), mat(w, "w_branch_fox").astype(BF16), mat(w, "w_out").astype(BF16),
             lane_pad(mat(w, "w_up").astype(BF16), SHARD_P), mat(w, "w_down").astype(BF16), mat(w, "conv_w")],
            False, g_in, "gather_late_start")
        w_in_p = _relayout(g_in, 1, IN_P, _gathered_to_full(IN_SHARD, _in_padded_to_orig), BF16, 256, "w_in_cols")[0]
        return w_in_p, started

    def late_weights(after):
        g_bsb, g_bfx, g_out, g_up, g_down, g_cw = _remote_wait(pending_w["late"], after, "gather_late_wait")
        w_up_i = _relayout(g_up, 1, 2 * D_FF, _gathered_to_full(UP_SHARD, _up_inter_to_orig), BF16, 256, "w_up_cols")[0]
        w_bsb = _relayout(g_bsb, 1, D, _gathered_to_full(ATT_SHARD, lambda d: d), BF16, 256, "w_bsb_cols")[0]
        w_bfx = _relayout(g_bfx, 1, D, _gathered_to_full(ATT_SHARD, lambda d: d), BF16, 256, "w_bfx_cols")[0]
        cw_full = g_cw.transpose(1, 0, 2).reshape(3, 2 * D_FF)
        cw_i = cw_full.reshape(3, 2, D_FF // FFC, FFC).transpose(0, 2, 1, 3).reshape(3, 2 * D_FF)
        return w_bsb, w_bfx, g_out.reshape(D, D), w_up_i, cw_i, g_down.reshape(D_FF, D)

    pending_g = {}

    def early_grads(g):
        d_cw = g["conv_w"].reshape(3, D_FF // FFC, 2, FFC).transpose(0, 2, 1, 3).reshape(3, 2 * D_FF)
        pending_g["early"], sent = _remote_start(
            [_relayout(g["w_branch_sb"][None], N_DEV, ATT_SHARD, _full_to_shards(ATT_SHARD, lambda c: c), BF16, 256, "d_w_bsb_shards"),
             _relayout(g["w_branch_fox"][None], N_DEV, ATT_SHARD, _full_to_shards(ATT_SHARD, lambda c: c), BF16, 256, "d_w_bfx_shards"),
             g["w_out"].reshape(N_DEV, D // N_DEV, D),
             _relayout(g["w_up"][None], N_DEV, SHARD_P, _full_to_shards(UP_SHARD, _UP_ORIG_TO_INTER.get), BF16, 256, "d_w_up_shards"),
             g["w_down"].reshape(N_DEV, D_FF // N_DEV, D),
             d_cw.reshape(3, N_DEV, UP_SHARD).transpose(1, 0, 2)], True, g["w_out"], "exchange_early_start")
        return sent

    def last_grad(d_in):
        shards = _relayout(d_in[None], N_DEV, SHARD_P, _full_to_shards(IN_SHARD, _IN_ORIG_TO_PADDED.get), BF16, 256, "d_w_in_shards")
        pending_g["last"], sent = _remote_start([shards], True, shards, "exchange_last_start")
        return sent

    loss, grad_x, grads = _local_step(
        x, loss_target, meta_full, norm_mix_g.reshape(1, D), b_forget,
        norm_ffn_g.reshape(1, D), norm_final_g.reshape(1, D), first_weights, late_weights, early_grads, last_grad)

    small = jnp.concatenate([_pack_repl(grads), grads["meta_tokens"].reshape(META_ROWS, LANES)], axis=0)
    small, = _all_gather([small], "gather_small_grads")
    me_idx = 4 * lax.axis_index("x") + 2 * lax.axis_index("y") + lax.axis_index("c")
    p_meta = lax.dynamic_slice_in_dim(small[:, REPL_ROWS:].reshape(N_DEV, N_META, D), me_idx * ATT_SHARD, ATT_SHARD, axis=2)

    p_in, = _remote_wait(pending_g["last"], small, "exchange_last_wait")
    parts = dict(zip(late, _remote_wait(pending_g["early"], p_in, "exchange_early_wait")), w_in=p_in, meta_tokens=p_meta)
    tiles = dict(w_in=256, w_branch_sb=256, w_branch_fox=256, w_out=D // N_DEV, w_up=256, w_down=D_FF // N_DEV,
                 conv_w=3, meta_tokens=N_META)
    new = {name: _sum_adamw(parts[name], mat(w, name), mat(m, name), mat(v, name), tiles[name], "adamw_" + name)
           for name in sharded}

    routs = _sum_adamw(small[:, :REPL_ROWS], _pack_repl(w), _pack_repl(m), _pack_repl(v), REPL_ROWS, "adamw_replicated")
    repl = [_unpack_repl(o, shapes) for o in routs]

    result = [lax.psum(loss, ("x", "y", "c")), grad_x]
    for k in range(4):
        for name in w:
            result.append(new[name][k].reshape(shapes[name]) if name in new else repl[k][name])
    return tuple(result)
```
